```python
import math
import jax, jax.numpy as jnp
from jax import lax
import numpy as np

D_MODEL = 2048
BATCH = 4
SEQ = 2048
DEPTH = 1

CHUNK = 64
Q_BLOCK = 128
TOKEN_BLOCK = 128
HEAD_DIM = 128
WIDTH_A = D_MODEL // 2
WIDTH_B = D_MODEL // 2
N_HEADS_A = WIDTH_A // HEAD_DIM
N_HEADS_B = WIDTH_B // HEAD_DIM
D_LAT = 256
N_IDX_HEADS = 16
D_IDX = 64
TOPK_MAX = 256
N_GROUPS = 8
EXPERTS_PER_GROUP = 8
N_EXPERTS = N_GROUPS * EXPERTS_PER_GROUP
TOP_K_IN_GROUP = 2
D_EXPERT = 512
N_BRANCHES = 2
RMS_EPS = 1e-6
IN_SIZES = (WIDTH_A, D_LAT, N_IDX_HEADS * D_IDX, D_IDX, N_IDX_HEADS,
            WIDTH_B, WIDTH_B, WIDTH_B, N_HEADS_B, N_BRANCHES * D_MODEL)
D_IN = sum(IN_SIZES)

kernel_name = "hybrid_dsa_fox_hiermoe_block"


def _split_points():
    return [int(v) for v in np.cumsum(np.array(IN_SIZES))[:-1]]


def rms_norm(x, g):
    xf = x.astype(jnp.float32)
    y = xf * lax.rsqrt(jnp.mean(xf * xf, axis=-1, keepdims=True) + RMS_EPS)
    return (y * g.astype(jnp.float32)).astype(x.dtype)


def _to_blocks(a):
    b, s = a.shape[:2]
    a = a.reshape((b, s // Q_BLOCK, Q_BLOCK) + a.shape[2:])
    return jnp.moveaxis(a, 1, 0)


def _from_blocks(a):
    a = jnp.moveaxis(a, 0, 1)
    return a.reshape((a.shape[0], a.shape[1] * a.shape[2]) + a.shape[3:])


def dsa_sparse_attention(q_a, c_kv, q_idx, k_idx, w_idx, w_uk, w_uv, g_qa, g_kv):
    b, s = q_a.shape[:2]
    topk = min(TOPK_MAX, s // 4)
    q_lat = rms_norm(jnp.einsum('bshd,hdl->bshl', q_a, w_uk), g_qa)
    kv = rms_norm(c_kv, g_kv)
    slopes = jnp.exp2(-8.0 * (jnp.arange(N_HEADS_A, dtype=jnp.float32) + 1.0) / N_HEADS_A)
    key_pos = jnp.arange(s)
    k_idx_f = k_idx.astype(jnp.float32)

    def block(args):
        i, ql, qi, wi = args
        q_pos = i * Q_BLOCK + jnp.arange(Q_BLOCK)
        admissible = (key_pos[None, :] // CHUNK) <= (q_pos[:, None] // CHUNK)
        dots = jnp.einsum('bqhd,bsd->bqhs', qi.astype(jnp.float32), k_idx_f) * (D_IDX ** -0.5)
        score = jnp.einsum('bqh,bqhs->bqs', wi.astype(jnp.float32) * (N_IDX_HEADS ** -0.5),
                           jax.nn.relu(dots))
        score = jnp.where(admissible[None], score, -jnp.inf)
        _, sel = lax.top_k(score, topk)
        kv_sel = jax.vmap(lambda kb, ib: kb[ib])(kv, sel)
        valid = (sel // CHUNK) <= (q_pos[None, :, None] // CHUNK)
        dist = jnp.abs(q_pos[None, :, None] - sel).astype(jnp.float32)
        logits = (jnp.einsum('bqhl,bqkl->bhqk', ql, kv_sel).astype(jnp.float32) * (D_LAT ** -0.5)
                  - slopes[None, :, None, None] * dist[:, None])
        logits = jnp.where(valid[:, None], logits, -jnp.inf)
        p = jax.nn.softmax(logits, axis=-1).astype(kv.dtype)
        return jnp.einsum('bhqk,bqkl->bqhl', p, kv_sel)

    nb = s // Q_BLOCK
    o_lat = _from_blocks(lax.map(block, (jnp.arange(nb), _to_blocks(q_lat),
                                         _to_blocks(q_idx), _to_blocks(w_idx))))
    return jnp.einsum('bshl,hld->bshd', o_lat, w_uv)


def forgetting_attention(q, k, v, f_logit, g_q, g_k):
    b, s = q.shape[:2]
    q = rms_norm(q, g_q)
    k = rms_norm(k, g_k)
    cum = jnp.cumsum(jax.nn.log_sigmoid(f_logit.astype(jnp.float32)), axis=1)
    cum_k = jnp.transpose(cum, (0, 2, 1))
    key_pos = jnp.arange(s)

    def block(args):
        i, qb, cq = args
        q_pos = i * Q_BLOCK + jnp.arange(Q_BLOCK)
        causal = key_pos[None, :] <= q_pos[:, None]
        decay = jnp.transpose(cq, (0, 2, 1))[..., None] - cum_k[:, :, None, :]
        logits = (jnp.einsum('bqhd,bshd->bhqs', qb, k).astype(jnp.float32) * (HEAD_DIM ** -0.5)
                  + decay)
        logits = jnp.where(causal[None, None], logits, -jnp.inf)
        p = jax.nn.softmax(logits, axis=-1).astype(v.dtype)
        return jnp.einsum('bhqs,bshd->bqhd', p, v)

    nb = s // Q_BLOCK
    return _from_blocks(lax.map(block, (jnp.arange(nb), _to_blocks(q), _to_blocks(cum))))


def hierarchical_moe(h, w_rg, b_rg, w_re, b_re, w1, w3, w2):
    b, s, d = h.shape
    t = h.reshape(b * s, d)
    n_tok = b * s
    g_logits = (t @ w_rg).astype(jnp.float32) + b_rg.astype(jnp.float32)
    g_prob = jax.nn.softmax(g_logits, axis=-1)
    g_idx = jnp.argmax(g_logits, axis=-1)
    g_w = jnp.take_along_axis(g_prob, g_idx[:, None], axis=1)
    e_logits = ((t @ w_re).astype(jnp.float32) + b_re.astype(jnp.float32)).reshape(
        n_tok, N_GROUPS, EXPERTS_PER_GROUP)
    e_in = jnp.take_along_axis(e_logits, g_idx[:, None, None], axis=1)[:, 0]
    top_v, top_i = lax.top_k(e_in, TOP_K_IN_GROUP)
    top_p = jax.nn.softmax(top_v, axis=-1) * g_w
    expert_id = g_idx[:, None] * EXPERTS_PER_GROUP + top_i
    gates = jnp.sum(jax.nn.one_hot(expert_id, N_EXPERTS, dtype=jnp.float32) * top_p[..., None],
                    axis=1)

    def ffn_block(args):
        tb, gb = args
        a = jnp.einsum('nd,edf->nef', tb, w1)
        u = jnp.einsum('nd,edf->nef', tb, w3)
        return jnp.einsum('nef,efd->nd', jax.nn.silu(a) * u * gb[..., None].astype(tb.dtype), w2)

    y = lax.map(ffn_block, (t.reshape(-1, TOKEN_BLOCK, d),
                            gates.reshape(-1, TOKEN_BLOCK, N_EXPERTS)))
    return y.reshape(b, s, d)


def setup_inputs(seed: int = 0) -> dict:
    key = jax.random.key(seed)
    ks = jax.random.split(key, 32)

    def nrm(k, shape, fan_in, scale=1.0):
        return jax.random.normal(k, shape, jnp.float32) * (scale * fan_in ** -0.5)

    def gain(k, shape):
        return 1.0 + 0.05 * jax.random.normal(k, shape, jnp.float32)

    def small(k, shape, scale=0.02):
        return scale * jax.random.normal(k, shape, jnp.float32)

    L = DEPTH
    return {
        "x": jax.random.normal(ks[0], (BATCH, SEQ, D_MODEL), jnp.float32),
        "c": jax.random.normal(ks[1], (BATCH, D_MODEL), jnp.float32),
        "w_ada": nrm(ks[2], (L, D_MODEL, 6 * D_MODEL), D_MODEL, 0.5),
        "b_ada": small(ks[3], (L, 6 * D_MODEL)),
        "g_mix": gain(ks[4], (L, D_MODEL)),
        "w_in": nrm(ks[5], (L, D_MODEL, D_IN), D_MODEL),
        "b_gate": small(ks[6], (L, N_BRANCHES * D_MODEL)),
        "b_forget": 2.0 + 0.5 * jax.random.normal(ks[7], (L, N_HEADS_B), jnp.float32),
        "w_uk": nrm(ks[8], (L, N_HEADS_A, HEAD_DIM, D_LAT), HEAD_DIM),
        "w_uv": nrm(ks[9], (L, N_HEADS_A, D_LAT, HEAD_DIM), D_LAT),
        "g_qa": gain(ks[10], (L, D_LAT)),
        "g_kv": gain(ks[11], (L, D_LAT)),
        "g_qb": gain(ks[12], (L, HEAD_DIM)),
        "g_kb": gain(ks[13], (L, HEAD_DIM)),
        "w_pa": nrm(ks[14], (L, WIDTH_A, D_MODEL), WIDTH_A),
        "w_pb": nrm(ks[15], (L, WIDTH_B, D_MODEL), WIDTH_B),
        "w_o": nrm(ks[16], (L, D_MODEL, D_MODEL), D_MODEL),
        "g_ffn": gain(ks[17], (L, D_MODEL)),
        "w_rg": nrm(ks[18], (L, D_MODEL, N_GROUPS), D_MODEL),
        "b_rg": small(ks[19], (L, N_GROUPS), 0.01),
        "w_re": nrm(ks[20], (L, D_MODEL, N_EXPERTS), D_MODEL),
        "b_re": small(ks[21], (L, N_EXPERTS), 0.01),
        "w1": nrm(ks[22], (L, N_EXPERTS, D_MODEL, D_EXPERT), D_MODEL),
        "w3": nrm(ks[23], (L, N_EXPERTS, D_MODEL, D_EXPERT), D_MODEL),
        "w2": nrm(ks[24], (L, N_EXPERTS, D_EXPERT, D_MODEL), D_EXPERT),
    }


def reference(x, c, w_ada, b_ada, g_mix, w_in, b_gate, b_forget, w_uk, w_uv, g_qa, g_kv,
              g_qb, g_kb, w_pa, w_pb, w_o, g_ffn, w_rg, b_rg, w_re, b_re, w1, w3, w2):
    b, s, d = x.shape
    cond = jax.nn.silu(c)
    for l in range(DEPTH):
        mod = (cond @ w_ada[l] + b_ada[l])[:, None, :]
        sh_m, sc_m, gt_m, sh_f, sc_f, gt_f = jnp.split(mod, 6, axis=-1)

        h = rms_norm(x, g_mix[l]) * (1.0 + sc_m) + sh_m
        proj = h @ w_in[l]
        (q_a, c_kv, q_idx, k_idx, w_idx, q_b, k_b, v_b, f_b, gate_logits) = jnp.split(
            proj, _split_points(), axis=-1)
        o_a = dsa_sparse_attention(
            q_a.reshape(b, s, N_HEADS_A, HEAD_DIM), c_kv,
            q_idx.reshape(b, s, N_IDX_HEADS, D_IDX), k_idx, w_idx,
            w_uk[l], w_uv[l], g_qa[l], g_kv[l])
        o_b = forgetting_attention(
            q_b.reshape(b, s, N_HEADS_B, HEAD_DIM), k_b.reshape(b, s, N_HEADS_B, HEAD_DIM),
            v_b.reshape(b, s, N_HEADS_B, HEAD_DIM), f_b + b_forget[l], g_qb[l], g_kb[l])
        g_a, g_b = jnp.split(jax.nn.sigmoid(gate_logits + b_gate[l]), 2, axis=-1)
        merged = (g_a * (o_a.reshape(b, s, WIDTH_A) @ w_pa[l])
                  + g_b * (o_b.reshape(b, s, WIDTH_B) @ w_pb[l]))
        x = x + gt_m * (merged @ w_o[l])

        h = rms_norm(x, g_ffn[l]) * (1.0 + sc_f) + sh_f
        x = x + gt_f * hierarchical_moe(h, w_rg[l], b_rg[l], w_re[l], b_re[l],
                                        w1[l], w3[l], w2[l])
    return x
```

```python
import functools

import jax
import jax.numpy as jnp
from jax import lax
from jax.experimental import pallas as pl
from jax.experimental.pallas import tpu as pltpu

F32 = jnp.float32
I32 = jnp.int32
MXU_DTYPE = jnp.bfloat16

CHUNK = 64
HEAD_DIM = 128
D_LAT = 256
N_IDX_HEADS = 16
D_IDX = 64
TOPK_MAX = 256
N_GROUPS = 8
EXPERTS_PER_GROUP = 8
N_EXPERTS = N_GROUPS * EXPERTS_PER_GROUP
RMS_EPS = 1e-6

LANES = 128
VMEM_LIMIT = 56 * 1024 * 1024

NEG_BIG = -1e30
INT_MIN = -2147483648


def _cparams(sem):
    return pltpu.CompilerParams(dimension_semantics=sem, vmem_limit_bytes=VMEM_LIMIT)


def _dot(a, b):
    return jnp.dot(a, b, preferred_element_type=F32)


def _dot_nt(a, b):
    return lax.dot_general(a, b, (((1,), (1,)), ((), ())), preferred_element_type=F32)


def _rms(x, g):
    return x * lax.rsqrt(jnp.mean(x * x, axis=-1, keepdims=True) + RMS_EPS) * g


def _ada_kernel(c_ref, w_ref, b_ref, o_ref):
    c = c_ref[...]
    a = c * jax.nn.sigmoid(c)
    o_ref[...] = _dot(a.astype(MXU_DTYPE), w_ref[...].astype(MXU_DTYPE)) + b_ref[...]


def _ada(c8, w_ada, b_ada):
    d, n = w_ada.shape
    tn = 1024
    return pl.pallas_call(
        _ada_kernel,
        grid=(n // tn,),
        in_specs=[pl.BlockSpec((8, d), lambda j: (0, 0)),
                  pl.BlockSpec((d, tn), lambda j: (0, j)),
                  pl.BlockSpec((1, tn), lambda j: (0, j))],
        out_specs=pl.BlockSpec((8, tn), lambda j: (0, j)),
        out_shape=jax.ShapeDtypeStruct((8, n), F32),
        compiler_params=_cparams(("arbitrary",)),
        name="ada",
    )(c8, w_ada, b_ada)


NP_COLS = 75 * LANES
U_QA, U_QIDX, U_QB, U_KB, U_VB, U_CKV, U_MISC = 32, 40, 48, 56, 64, 72, 74
TN_PROJ = 5 * LANES
MISC_TILE = (U_MISC * LANES) // TN_PROJ
MISC_OFF = U_MISC * LANES - MISC_TILE * TN_PROJ
MISC_F, MISC_W, MISC_K = 0, 8, 64


def _pack_w_in(w_in):
    d = w_in.shape[0]
    sizes = (1024, 256, 1024, 64, 16, 1024, 1024, 1024, 8, 4096)
    offs = [0]
    for s in sizes:
        offs.append(offs[-1] + s)
    seg = lambda k: w_in[:, offs[k]:offs[k + 1]]
    q_a, c_kv, q_idx, k_idx, w_idx, q_b, k_b, v_b, f_b, gate = [seg(k) for k in range(10)]
    misc = jnp.concatenate([f_b, w_idx, jnp.zeros((d, 40), w_in.dtype), k_idx], axis=1)
    return jnp.concatenate([gate, q_a, q_idx, q_b, k_b, v_b, c_kv, misc], axis=1).astype(MXU_DTYPE)


def _inproj_kernel(x_ref, mod_ref, g_ref, w_ref, o_ref, misc_ref, h_scr):
    j = pl.program_id(1)

    @pl.when(j == 0)
    def _():
        y = _rms(x_ref[...], g_ref[...])
        sh = mod_ref[0, 0:1, :]
        sc = mod_ref[0, 1:2, :]
        h_scr[...] = (y * (1.0 + sc) + sh).astype(h_scr.dtype)

    acc = _dot(h_scr[...], w_ref[...])
    o_ref[...] = acc.astype(o_ref.dtype)

    @pl.when(j == MISC_TILE)
    def _():
        misc_ref[...] = acc[:, MISC_OFF:MISC_OFF + LANES]


def _inproj(x2, mod3, g_mix, w_packed, seq):
    t, d = x2.shape
    tm = min(1024, seq)
    per_b = seq // tm
    return pl.pallas_call(
        _inproj_kernel,
        grid=(t // tm, NP_COLS // TN_PROJ),
        in_specs=[pl.BlockSpec((tm, d), lambda i, j: (i, 0)),
                  pl.BlockSpec((1, 6, d), lambda i, j: (i // per_b, 0, 0)),
                  pl.BlockSpec((1, d), lambda i, j: (0, 0)),
                  pl.BlockSpec((d, TN_PROJ), lambda i, j: (0, j))],
        out_specs=[pl.BlockSpec((tm, TN_PROJ), lambda i, j: (i, j)),
                   pl.BlockSpec((tm, LANES), lambda i, j: (i, 0))],
        out_shape=[jax.ShapeDtypeStruct((t, NP_COLS), MXU_DTYPE),
                   jax.ShapeDtypeStruct((t, LANES), F32)],
        scratch_shapes=[pltpu.VMEM((tm, d), MXU_DTYPE)],
        compiler_params=_cparams(("parallel", "arbitrary")),
        name="inproj",
    )(x2, mod3, g_mix, w_packed)


TK = 256


def _prep_kernel(ckv_ref, misc_ref, gkv_ref, bf_ref, kv_ref, cum_ref, cumt_ref):
    s = ckv_ref.shape[1]
    kv_ref[0] = _rms(ckv_ref[0].astype(F32), gkv_ref[...]).astype(kv_ref.dtype)

    r = lax.broadcasted_iota(I32, (LANES, LANES), 0)
    c = lax.broadcasted_iota(I32, (LANES, LANES), 1)
    tri = jnp.where(c <= r, 1.0, 0.0).astype(MXU_DTYPE)
    carry = jnp.zeros((1, LANES), F32)
    for blk in range(s // LANES):
        z = misc_ref[0, blk * LANES:(blk + 1) * LANES, :] + bf_ref[...]
        ls = jnp.minimum(z, 0.0) - jnp.log1p(jnp.exp(-jnp.abs(z)))
        p1 = ls.astype(MXU_DTYPE)
        r1 = ls - p1.astype(F32)
        p2 = r1.astype(MXU_DTYPE)
        p3 = (r1 - p2.astype(F32)).astype(MXU_DTYPE)
        cs = _dot(tri, p1) + _dot(tri, p2) + _dot(tri, p3) + carry
        carry = cs[LANES - 1:LANES, :]
        cum_ref[0, blk * LANES:(blk + 1) * LANES, :] = cs
        cst = cs.T
        kt, half = divmod(blk * LANES, TK)
        for hh in range(8):
            cumt_ref[0, hh, kt:kt + 1, half:half + LANES] = cst[hh:hh + 1, :]


def _prep(proj3, misc3, g_kv, bf_row):
    b, s, _ = proj3.shape
    return pl.pallas_call(
        _prep_kernel,
        grid=(b,),
        in_specs=[pl.BlockSpec((1, s, D_LAT), lambda i: (i, 0, U_CKV * LANES // D_LAT)),
                  pl.BlockSpec((1, s, LANES), lambda i: (i, 0, 0)),
                  pl.BlockSpec((1, D_LAT), lambda i: (0, 0)),
                  pl.BlockSpec((1, LANES), lambda i: (0, 0))],
        out_specs=[pl.BlockSpec((1, s, D_LAT), lambda i: (i, 0, 0)),
                   pl.BlockSpec((1, s, LANES), lambda i: (i, 0, 0)),
                   pl.BlockSpec((1, 8, s // TK, TK), lambda i: (i, 0, 0, 0))],
        out_shape=[jax.ShapeDtypeStruct((b, s, D_LAT), MXU_DTYPE),
                   jax.ShapeDtypeStruct((b, s, LANES), F32),
                   jax.ShapeDtypeStruct((b, 8, s // TK, TK), F32)],
        compiler_params=_cparams(("parallel",)),
        name="prep",
    )(proj3, misc3, g_kv, bf_row)


TQ_A = 256


def _dsa_kernel(qa_ref, qidx_ref, miscq_ref, misck_ref, kv_ref, wuk_ref, wuv_ref, gqa_ref, o_ref,
                key_scr, bias_scr, logit_scr, qlat_scr, mpart_scr, lpart_scr, oacc_scr, *, topk, n_heads):
    i = pl.program_id(1)
    tq = TQ_A
    nk = i + 1
    q0 = i * tq

    wt = miscq_ref[0].T
    wq = wt[MISC_W:MISC_W + N_IDX_HEADS, :] * (D_IDX ** -0.5 * N_IDX_HEADS ** -0.5)

    def score_body(kt, carry):
        k0 = pl.multiple_of(kt * LANES, LANES)
        kx = misck_ref[0, pl.ds(k0, LANES), :][:, MISC_K:MISC_K + D_IDX].astype(MXU_DTYPE)
        acc = jnp.zeros((LANES, tq), F32)
        for h in range(N_IDX_HEADS):
            qh = qidx_ref[0, :, h * D_IDX:(h + 1) * D_IDX]
            d = _dot_nt(kx, qh)
            acc = acc + jnp.maximum(d, 0.0) * wq[h:h + 1, :]
        kpos = k0 + lax.broadcasted_iota(I32, (LANES, tq), 0)
        qpos = q0 + lax.broadcasted_iota(I32, (LANES, tq), 1)
        adm = (kpos // CHUNK) <= (qpos // CHUNK)
        bits = pltpu.bitcast(acc, I32)
        skey = bits ^ ((bits >> 31) & 0x7FFFFFFF)
        key_scr[pl.ds(k0, LANES), :] = jnp.where(adm, skey, INT_MIN)
        return carry

    lax.fori_loop(0, nk * (TK // LANES), score_body, 0)

    def bit_body(it, ans_u):
        bitval = jnp.left_shift(jnp.int32(1), 31 - it)
        cand_u = ans_u | bitval
        cand_s = cand_u ^ INT_MIN

        def cnt_body(kt, cnt):
            k0 = pl.multiple_of(kt * TK, TK)
            tile = key_scr[pl.ds(k0, TK), :]
            return cnt + jnp.sum(jnp.where(tile >= cand_s, 1.0, 0.0), axis=0, keepdims=True)

        cnt = lax.fori_loop(0, nk, cnt_body, jnp.zeros((1, tq), F32))
        return jnp.where(cnt >= topk, cand_u, ans_u)

    ans_u = lax.fori_loop(0, 32, bit_body, jnp.zeros((1, tq), I32))
    thr = jnp.maximum(ans_u ^ INT_MIN, INT_MIN + 1)

    def bias_body(kt, carry):
        k0 = pl.multiple_of(kt * TK, TK)
        sel_t = key_scr[pl.ds(k0, TK), :] >= thr
        bias_scr[kt] = jnp.where(sel_t, 0.0, NEG_BIG).T
        return carry

    lax.fori_loop(0, nk, bias_body, 0)

    for h in range(n_heads):
        ql = _dot(qa_ref[0, :, h * HEAD_DIM:(h + 1) * HEAD_DIM], wuk_ref[h])
        ql = _rms(ql, gqa_ref[...]) * (D_LAT ** -0.5)
        qlat_scr[h * tq:(h + 1) * tq, :] = ql.astype(qlat_scr.dtype)
    mpart_scr[...] = jnp.full(mpart_scr.shape, NEG_BIG, F32)
    lpart_scr[...] = jnp.zeros(lpart_scr.shape, F32)
    oacc_scr[...] = jnp.zeros(oacc_scr.shape, F32)

    def logit_body(kt, carry):
        k0 = pl.multiple_of(kt * TK, TK)
        kvt = kv_ref[0, pl.ds(k0, TK), :]
        qpos = q0 + lax.broadcasted_iota(I32, (tq, TK), 0)
        kpos = k0 + lax.broadcasted_iota(I32, (tq, TK), 1)
        dist = jnp.abs(qpos - kpos).astype(F32)
        bias = bias_scr[kt]
        for h in range(n_heads):
            rows = slice(h * tq, (h + 1) * tq)
            slope = 2.0 ** (-8.0 * (h + 1) / n_heads)
            lg = _dot_nt(qlat_scr[rows, :], kvt) + (bias - slope * dist)
            logit_scr[kt, rows, :] = lg
            mpart_scr[rows, :] = jnp.maximum(mpart_scr[rows, :],
                                             jnp.maximum(lg[:, :LANES], lg[:, LANES:]))
        return carry

    lax.fori_loop(0, nk, logit_body, 0)

    m = jnp.max(mpart_scr[...], axis=-1, keepdims=True)
    mpart_scr[...] = jnp.broadcast_to(m, mpart_scr.shape)

    def pv_body(kt, carry):
        k0 = pl.multiple_of(kt * TK, TK)
        kvt = kv_ref[0, pl.ds(k0, TK), :]
        for h in range(n_heads):
            rows = slice(h * tq, (h + 1) * tq)
            mb = mpart_scr[rows, :]
            lg = logit_scr[kt, rows, :]
            p = jnp.exp(lg - jnp.concatenate([mb, mb], axis=1))
            lpart_scr[rows, :] += p[:, :LANES] + p[:, LANES:]
            oacc_scr[rows, :] += _dot(p.astype(MXU_DTYPE), kvt)
        return carry

    lax.fori_loop(0, nk, pv_body, 0)

    for h in range(n_heads):
        rows = slice(h * tq, (h + 1) * tq)
        lsum = jnp.sum(lpart_scr[rows, :], axis=-1, keepdims=True)
        o_lat = oacc_scr[rows, :] / lsum
        o_ref[0, :, h * HEAD_DIM:(h + 1) * HEAD_DIM] = _dot(
            o_lat.astype(MXU_DTYPE), wuv_ref[h]).astype(o_ref.dtype)


def _dsa(proj3, misc3, kv, w_uk, w_uv, g_qa):
    b, s, _ = proj3.shape
    n_heads = w_uk.shape[0]
    width = n_heads * HEAD_DIM
    topk = min(TOPK_MAX, s // 4)
    tq = TQ_A
    nkt = s // TK
    kern = functools.partial(_dsa_kernel, topk=topk, n_heads=n_heads)
    return pl.pallas_call(
        kern,
        grid=(b, s // tq),
        in_specs=[pl.BlockSpec((1, tq, width), lambda bb, i: (bb, i, U_QA * LANES // width)),
                  pl.BlockSpec((1, tq, N_IDX_HEADS * D_IDX),
                               lambda bb, i: (bb, i, U_QIDX * LANES // (N_IDX_HEADS * D_IDX))),
                  pl.BlockSpec((1, tq, LANES), lambda bb, i: (bb, i, 0)),
                  pl.BlockSpec((1, s, LANES), lambda bb, i: (bb, 0, 0)),
                  pl.BlockSpec((1, s, D_LAT), lambda bb, i: (bb, 0, 0)),
                  pl.BlockSpec((n_heads, HEAD_DIM, D_LAT), lambda bb, i: (0, 0, 0)),
                  pl.BlockSpec((n_heads, D_LAT, HEAD_DIM), lambda bb, i: (0, 0, 0)),
                  pl.BlockSpec((1, D_LAT), lambda bb, i: (0, 0))],
        out_specs=pl.BlockSpec((1, tq, width), lambda bb, i: (bb, i, 0)),
        out_shape=jax.ShapeDtypeStruct((b, s, width), MXU_DTYPE),
        scratch_shapes=[pltpu.VMEM((s, tq), I32),
                        pltpu.VMEM((nkt, tq, TK), F32),
                        pltpu.VMEM((nkt, n_heads * tq, TK), F32),
                        pltpu.VMEM((n_heads * tq, D_LAT), MXU_DTYPE),
                        pltpu.VMEM((n_heads * tq, LANES), F32),
                        pltpu.VMEM((n_heads * tq, LANES), F32),
                        pltpu.VMEM((n_heads * tq, D_LAT), F32)],
        compiler_params=_cparams(("parallel", "arbitrary")),
        name="dsa",
    )(proj3, proj3, misc3, misc3, kv, w_uk, w_uv, g_qa)


TQ_B = 256


def _fox_kernel(q_ref, k_ref, v_ref, cumq_ref, cumt_ref, gq_ref, gk_ref, o_ref, kn_scr, logit_scr):
    h = pl.program_id(1)
    qi = pl.program_id(2)
    tq = TQ_B
    nk = qi + 1

    @pl.when(qi == 0)
    def _():
        kn_scr[...] = _rms(k_ref[0].astype(F32), gk_ref[...]).astype(kn_scr.dtype)

    qn = _rms(q_ref[0].astype(F32), gq_ref[...]).astype(MXU_DTYPE)
    lane = lax.broadcasted_iota(I32, (tq, LANES), 1)
    cq = jnp.sum(jnp.where(lane == h, cumq_ref[0], 0.0), axis=-1, keepdims=True)
    qpos = qi * tq + lax.broadcasted_iota(I32, (tq, TK), 0)

    def logit_body(kt, mpart):
        k0 = pl.multiple_of(kt * TK, TK)
        s = _dot_nt(qn, kn_scr[pl.ds(k0, TK), :])
        ck = cumt_ref[0, h, pl.ds(kt, 1), :]
        lg = s * (HEAD_DIM ** -0.5) + (cq - ck)
        kpos = k0 + lax.broadcasted_iota(I32, (tq, TK), 1)
        lg = jnp.where(kpos <= qpos, lg, NEG_BIG)
        logit_scr[kt] = lg
        return jnp.maximum(mpart, jnp.maximum(lg[:, :LANES], lg[:, LANES:]))

    mpart = lax.fori_loop(0, nk, logit_body, jnp.full((tq, LANES), NEG_BIG, F32))
    m = jnp.max(mpart, axis=-1, keepdims=True)

    def pv_body(kt, carry):
        lpart, acc = carry
        k0 = pl.multiple_of(kt * TK, TK)
        p = jnp.exp(logit_scr[kt] - m)
        lpart = lpart + p[:, :LANES] + p[:, LANES:]
        acc = acc + _dot(p.astype(MXU_DTYPE), v_ref[0, pl.ds(k0, TK), :])
        return lpart, acc

    lpart, acc = lax.fori_loop(0, nk, pv_body,
                               (jnp.zeros((tq, LANES), F32), jnp.zeros((tq, HEAD_DIM), F32)))
    o_ref[0] = (acc / jnp.sum(lpart, axis=-1, keepdims=True)).astype(o_ref.dtype)


def _fox(proj3, cum, cumt, g_qb, g_kb, n_heads):
    b, s, _ = proj3.shape
    tq = TQ_B
    return pl.pallas_call(
        _fox_kernel,
        grid=(b, n_heads, s // tq),
        in_specs=[pl.BlockSpec((1, tq, HEAD_DIM), lambda bb, h, i: (bb, i, U_QB + h)),
                  pl.BlockSpec((1, s, HEAD_DIM), lambda bb, h, i: (bb, 0, U_KB + h)),
                  pl.BlockSpec((1, s, HEAD_DIM), lambda bb, h, i: (bb, 0, U_VB + h)),
                  pl.BlockSpec((1, tq, LANES), lambda bb, h, i: (bb, i, 0)),
                  pl.BlockSpec((1, 8, s // TK, TK), lambda bb, h, i: (bb, 0, 0, 0)),
                  pl.BlockSpec((1, HEAD_DIM), lambda bb, h, i: (0, 0)),
                  pl.BlockSpec((1, HEAD_DIM), lambda bb, h, i: (0, 0))],
        out_specs=pl.BlockSpec((1, tq, HEAD_DIM), lambda bb, h, i: (bb, i, h)),
        out_shape=jax.ShapeDtypeStruct((b, s, n_heads * HEAD_DIM), MXU_DTYPE),
        scratch_shapes=[pltpu.VMEM((s, HEAD_DIM), MXU_DTYPE),
                        pltpu.VMEM((s // TK, tq, TK), F32)],
        compiler_params=_cparams(("parallel", "parallel", "arbitrary")),
        name="fox",
    )(proj3, proj3, proj3, cum, cumt, g_qb, g_kb)


TM_MERGE = 256


def _merge_kernel(ga_ref, gb_ref, oa_ref, ob_ref, x_ref, mod_ref, bga_ref, bgb_ref, wpa_ref, wpb_ref,
                  wo_ref, gffn_ref, wrh_ref, wrl_ref, br_ref, x1_ref, h2_ref, rl_ref):
    ga = jax.nn.sigmoid(ga_ref[...].astype(F32) + bga_ref[...])
    gb = jax.nn.sigmoid(gb_ref[...].astype(F32) + bgb_ref[...])
    merged = ga * _dot(oa_ref[...], wpa_ref[...]) + gb * _dot(ob_ref[...], wpb_ref[...])
    upd = _dot(merged.astype(MXU_DTYPE), wo_ref[...])
    x1 = x_ref[...] + mod_ref[0, 2:3, :] * upd
    x1_ref[...] = x1
    h2 = _rms(x1, gffn_ref[...]) * (1.0 + mod_ref[0, 4:5, :]) + mod_ref[0, 3:4, :]
    h2_ref[...] = h2
    hh = h2.astype(MXU_DTYPE)
    hl = (h2 - hh.astype(F32)).astype(MXU_DTYPE)
    rl_ref[...] = (_dot(hh, wrh_ref[...]) + _dot(hl, wrh_ref[...]) + _dot(hh, wrl_ref[...])
                   + br_ref[...])


def _merge(proj, o_a, o_b, x2, mod3, b_gate, w_pa, w_pb, w_o, g_ffn, wr_hi, wr_lo, b_r, seq):
    t, d = x2.shape
    tm = TM_MERGE
    per_b = seq // tm
    wa = o_a.shape[1]
    res = lambda shape: pl.BlockSpec(shape, lambda i: (0,) * len(shape), pipeline_mode=pl.Buffered(1))
    return pl.pallas_call(
        _merge_kernel,
        grid=(t // tm,),
        in_specs=[pl.BlockSpec((tm, d), lambda i: (i, 0)),
                  pl.BlockSpec((tm, d), lambda i: (i, 1)),
                  pl.BlockSpec((tm, wa), lambda i: (i, 0)),
                  pl.BlockSpec((tm, wa), lambda i: (i, 0)),
                  pl.BlockSpec((tm, d), lambda i: (i, 0)),
                  pl.BlockSpec((1, 6, d), lambda i: (i // per_b, 0, 0)),
                  pl.BlockSpec((1, d), lambda i: (0, 0)),
                  pl.BlockSpec((1, d), lambda i: (0, 1)),
                  res((wa, d)), res((wa, d)), res((d, d)),
                  pl.BlockSpec((1, d), lambda i: (0, 0)),
                  res((d, LANES)), res((d, LANES)),
                  pl.BlockSpec((1, LANES), lambda i: (0, 0))],
        out_specs=[pl.BlockSpec((tm, d), lambda i: (i, 0)),
                   pl.BlockSpec((tm, d), lambda i: (i, 0)),
                   pl.BlockSpec((tm, LANES), lambda i: (i, 0))],
        out_shape=[jax.ShapeDtypeStruct((t, d), F32),
                   jax.ShapeDtypeStruct((t, d), F32),
                   jax.ShapeDtypeStruct((t, LANES), F32)],
        compiler_params=_cparams(("parallel",)),
        name="merge",
    )(proj, proj, o_a, o_b, x2, mod3, b_gate, b_gate, w_pa, w_pb, w_o, g_ffn, wr_hi, wr_lo, b_r)


TM_ROUTE = 256
R_E0, R_E1, R_RK0, R_RK1, R_P0, R_P1 = range(6)


def _route_kernel(rl_ref, route_ref, cnt_ref, carry_scr):
    tm = TM_ROUTE

    @pl.when(pl.program_id(0) == 0)
    def _():
        carry_scr[...] = jnp.zeros(carry_scr.shape, F32)

    r = rl_ref[...]
    lane = lax.broadcasted_iota(I32, (tm, LANES), 1).astype(F32)
    neg_inf = -jnp.inf
    gmask = lane < N_GROUPS
    gl = jnp.where(gmask, r, neg_inf)
    gmax = jnp.max(gl, axis=-1, keepdims=True)
    gidx = jnp.min(jnp.where(gl == gmax, lane, float(LANES)), axis=-1, keepdims=True)
    gsum = jnp.sum(jnp.where(gmask, jnp.exp(r - gmax), 0.0), axis=-1, keepdims=True)
    gw = 1.0 / gsum
    lo = N_GROUPS + EXPERTS_PER_GROUP * gidx
    emask = (lane >= lo) & (lane < lo + EXPERTS_PER_GROUP)
    el = jnp.where(emask, r, neg_inf)
    v0 = jnp.max(el, axis=-1, keepdims=True)
    i0 = jnp.min(jnp.where(el == v0, lane, float(LANES)), axis=-1, keepdims=True)
    el2 = jnp.where(lane == i0, neg_inf, el)
    v1 = jnp.max(el2, axis=-1, keepdims=True)
    i1 = jnp.min(jnp.where(el2 == v1, lane, float(LANES)), axis=-1, keepdims=True)
    tt = jnp.exp(v1 - v0)
    p0 = gw / (1.0 + tt)
    p1 = gw * tt / (1.0 + tt)
    e0 = i0 - N_GROUPS
    e1 = i1 - N_GROUPS

    hit0 = lane == e0
    hit1 = lane == e1
    oh = jnp.where(hit0 | hit1, 1.0, 0.0)
    rr = lax.broadcasted_iota(I32, (tm, tm), 0)
    cc = lax.broadcasted_iota(I32, (tm, tm), 1)
    ltri = jnp.where(cc < rr, 1.0, 0.0).astype(MXU_DTYPE)
    before = _dot(ltri, oh.astype(MXU_DTYPE)) + carry_scr[...]
    rk0 = jnp.sum(jnp.where(hit0, before, 0.0), axis=-1, keepdims=True)
    rk1 = jnp.sum(jnp.where(hit1, before, 0.0), axis=-1, keepdims=True)
    carry_scr[...] = carry_scr[...] + jnp.sum(oh, axis=0, keepdims=True)
    cnt_ref[...] = jnp.broadcast_to(carry_scr[...], cnt_ref.shape)

    out = jnp.zeros((tm, LANES), F32)
    for k, val in ((R_E0, e0), (R_E1, e1), (R_RK0, rk0), (R_RK1, rk1),
                   (R_P0, p0), (R_P1, p1)):
        out = jnp.where(lane == k, val, out)
    route_ref[...] = out


def _route(rlog):
    t = rlog.shape[0]
    tm = TM_ROUTE
    return pl.pallas_call(
        _route_kernel,
        grid=(t // tm,),
        in_specs=[pl.BlockSpec((tm, LANES), lambda i: (i, 0))],
        out_specs=[pl.BlockSpec((tm, LANES), lambda i: (i, 0)),
                   pl.BlockSpec((8, LANES), lambda i: (0, 0))],
        out_shape=[jax.ShapeDtypeStruct((t, LANES), F32),
                   jax.ShapeDtypeStruct((8, LANES), F32)],
        scratch_shapes=[pltpu.VMEM((1, LANES), F32)],
        compiler_params=_cparams(("arbitrary",)),
        name="route",
    )(rlog)


TM_ROWS = 128
TM_DISP = 256


def _dispatch_kernel(pos0_ref, pos1_ref, h2_ref, xs_in_ref, xs_ref, sem):
    del xs_in_ref
    base = pl.program_id(0) * TM_DISP

    def row_copy(r, p):
        return pltpu.make_async_copy(h2_ref.at[pl.ds(r, 1), :], xs_ref.at[pl.ds(p, 1), :], sem)

    def issue(r, carry):
        row_copy(r, pos0_ref[base + r]).start()
        row_copy(r, pos1_ref[base + r]).start()
        return carry

    lax.fori_loop(0, TM_DISP, issue, 0)

    def drain(r, carry):
        row_copy(0, 0).wait()
        row_copy(0, 0).wait()
        return carry

    lax.fori_loop(0, TM_DISP, drain, 0)


def _dispatch(pos0, pos1, h2, n_rows):
    t, d = h2.shape
    xs0 = jnp.zeros((n_rows, d), h2.dtype)
    return pl.pallas_call(
        _dispatch_kernel,
        grid_spec=pltpu.PrefetchScalarGridSpec(
            num_scalar_prefetch=2,
            grid=(t // TM_DISP,),
            in_specs=[pl.BlockSpec((TM_DISP, d), lambda i, p0, p1: (i, 0)),
                      pl.BlockSpec(memory_space=pl.ANY)],
            out_specs=pl.BlockSpec(memory_space=pl.ANY),
            scratch_shapes=[pltpu.SemaphoreType.DMA(())]),
        out_shape=jax.ShapeDtypeStruct((n_rows, d), h2.dtype),
        input_output_aliases={3: 0},
        compiler_params=pltpu.CompilerParams(dimension_semantics=("arbitrary",),
                                             vmem_limit_bytes=VMEM_LIMIT, has_side_effects=True),
        name="dispatch",
    )(pos0, pos1, h2, xs0)


def _experts_kernel(te_ref, tf_ref, xs_ref, w1_ref, w3_ref, w2_ref, ys_ref, w1_scr, w3_scr, w2_scr):
    del te_ref
    flag = tf_ref[pl.program_id(0)]

    @pl.when(flag == 2)
    def _():
        w1_scr[...] = w1_ref[0].astype(w1_scr.dtype)
        w3_scr[...] = w3_ref[0].astype(w3_scr.dtype)
        w2_scr[...] = w2_ref[0].astype(w2_scr.dtype)

    @pl.when(flag > 0)
    def _():
        x = xs_ref[...].astype(MXU_DTYPE)
        a = _dot(x, w1_scr[...])
        u = _dot(x, w3_scr[...])
        hm = (a * jax.nn.sigmoid(a)) * u
        ys_ref[...] = _dot(hm.astype(MXU_DTYPE), w2_scr[...])

    @pl.when(flag == 0)
    def _():
        ys_ref[...] = jnp.zeros(ys_ref.shape, ys_ref.dtype)


def _experts(tile_e, tile_flag, xs, w1, w3, w2):
    n_rows, d = xs.shape
    f = w1.shape[2]
    tm = TM_ROWS
    return pl.pallas_call(
        _experts_kernel,
        grid_spec=pltpu.PrefetchScalarGridSpec(
            num_scalar_prefetch=2,
            grid=(n_rows // tm,),
            in_specs=[pl.BlockSpec((tm, d), lambda j, te, tf: (j, 0)),
                      pl.BlockSpec((1, d, f), lambda j, te, tf: (te[j], 0, 0)),
                      pl.BlockSpec((1, d, f), lambda j, te, tf: (te[j], 0, 0)),
                      pl.BlockSpec((1, f, d), lambda j, te, tf: (te[j], 0, 0))],
            out_specs=pl.BlockSpec((tm, d), lambda j, te, tf: (j, 0)),
            scratch_shapes=[pltpu.VMEM((d, f), MXU_DTYPE),
                            pltpu.VMEM((d, f), MXU_DTYPE),
                            pltpu.VMEM((f, d), MXU_DTYPE)]),
        out_shape=jax.ShapeDtypeStruct((n_rows, d), F32),
        compiler_params=_cparams(("arbitrary",)),
        name="experts",
    )(tile_e, tile_flag, xs, w1, w3, w2)


TM_COMB = 256


def _combine_kernel(pos0_ref, pos1_ref, ys_ref, x1_ref, route_ref, mod_ref, o_ref, y0_scr, y1_scr, sem):
    base = pl.program_id(0) * TM_COMB

    def row_copy(p, dst, r):
        return pltpu.make_async_copy(ys_ref.at[pl.ds(p, 1), :], dst.at[pl.ds(r, 1), :], sem)

    def issue(r, carry):
        row_copy(pos0_ref[base + r], y0_scr, r).start()
        row_copy(pos1_ref[base + r], y1_scr, r).start()
        return carry

    lax.fori_loop(0, TM_COMB, issue, 0)

    def drain(r, carry):
        row_copy(0, y0_scr, 0).wait()
        row_copy(0, y1_scr, 0).wait()
        return carry

    lax.fori_loop(0, TM_COMB, drain, 0)

    rt = route_ref[...]
    lane = lax.broadcasted_iota(I32, rt.shape, 1)
    p0 = jnp.sum(jnp.where(lane == R_P0, rt, 0.0), axis=-1, keepdims=True)
    p1 = jnp.sum(jnp.where(lane == R_P1, rt, 0.0), axis=-1, keepdims=True)
    y = p0 * y0_scr[...] + p1 * y1_scr[...]
    o_ref[...] = x1_ref[...] + mod_ref[0, 5:6, :] * y


def _combine(pos0, pos1, ys, x1, route, mod3, seq):
    t, d = x1.shape
    tm = TM_COMB
    per_b = seq // tm
    return pl.pallas_call(
        _combine_kernel,
        grid_spec=pltpu.PrefetchScalarGridSpec(
            num_scalar_prefetch=2,
            grid=(t // tm,),
            in_specs=[pl.BlockSpec(memory_space=pl.ANY),
                      pl.BlockSpec((tm, d), lambda i, p0, p1: (i, 0)),
                      pl.BlockSpec((tm, LANES), lambda i, p0, p1: (i, 0)),
                      pl.BlockSpec((1, 6, d), lambda i, p0, p1: (i // per_b, 0, 0))],
            out_specs=pl.BlockSpec((tm, d), lambda i, p0, p1: (i, 0)),
            scratch_shapes=[pltpu.VMEM((tm, d), F32),
                            pltpu.VMEM((tm, d), F32),
                            pltpu.SemaphoreType.DMA(())]),
        out_shape=jax.ShapeDtypeStruct((t, d), F32),
        compiler_params=_cparams(("arbitrary",)),
        name="combine",
    )(pos0, pos1, ys, x1, route, mod3)


def _plan(route, counts, n_rows):
    cnt = counts[0, :N_EXPERTS].astype(I32)
    gsz = ((cnt + TM_ROWS - 1) // TM_ROWS) * TM_ROWS
    ends = jnp.cumsum(gsz)
    offs = ends - gsz
    e0 = route[:, R_E0].astype(I32)
    e1 = route[:, R_E1].astype(I32)
    pos0 = offs[e0] + route[:, R_RK0].astype(I32)
    pos1 = offs[e1] + route[:, R_RK1].astype(I32)
    n_tiles = n_rows // TM_ROWS
    starts = jnp.arange(n_tiles, dtype=I32) * TM_ROWS
    te_raw = jnp.searchsorted(ends, starts, side="right").astype(I32)
    valid = starts < ends[-1]
    last_e = jnp.searchsorted(ends, ends[-1] - 1, side="right").astype(I32)
    tile_e = jnp.where(valid, jnp.minimum(te_raw, N_EXPERTS - 1), last_e)
    prev = jnp.concatenate([jnp.full((1,), -1, I32), tile_e[:-1]])
    tile_flag = jnp.where(valid, jnp.where(tile_e != prev, 2, 1), 0).astype(I32)
    return pos0, pos1, tile_e, tile_flag


def kernel(x, c, w_ada, b_ada, g_mix, w_in, b_gate, b_forget, w_uk, w_uv, g_qa, g_kv, g_qb, g_kb,
           w_pa, w_pb, w_o, g_ffn, w_rg, b_rg, w_re, b_re, w1, w3, w2):
    b, s, d = x.shape
    depth = w_ada.shape[0]
    t = b * s
    n_heads_b = b_forget.shape[1]
    n_rows = 2 * t + N_EXPERTS * TM_ROWS
    c8 = jnp.zeros((8, d), F32).at[:b].set(c)
    x2 = x.reshape(t, d)
    row = lambda v: v.reshape(1, -1)
    for l in range(depth):
        mod3 = _ada(c8, w_ada[l], row(b_ada[l]))[:b].reshape(b, 6, d)
        proj, misc = _inproj(x2, mod3, row(g_mix[l]), _pack_w_in(w_in[l]), s)
        proj3 = proj.reshape(b, s, NP_COLS)
        misc3 = misc.reshape(b, s, LANES)
        bf_row = jnp.zeros((1, LANES), F32).at[0, MISC_F:MISC_F + n_heads_b].set(b_forget[l])
        kv, cum, cumt = _prep(proj3, misc3, row(g_kv[l]), bf_row)
        o_a = _dsa(proj3, misc3, kv, w_uk[l].astype(MXU_DTYPE), w_uv[l].astype(MXU_DTYPE), row(g_qa[l]))
        o_b = _fox(proj3, cum, cumt, row(g_qb[l]), row(g_kb[l]), n_heads_b)

        w_r = jnp.zeros((d, LANES), F32).at[:, :N_GROUPS].set(w_rg[l])
        w_r = w_r.at[:, N_GROUPS:N_GROUPS + N_EXPERTS].set(w_re[l])
        wr_hi = w_r.astype(MXU_DTYPE)
        wr_lo = (w_r - wr_hi.astype(F32)).astype(MXU_DTYPE)
        b_r = jnp.zeros((1, LANES), F32).at[0, :N_GROUPS].set(b_rg[l])
        b_r = b_r.at[0, N_GROUPS:N_GROUPS + N_EXPERTS].set(b_re[l])
        x1, h2, rlog = _merge(proj, o_a.reshape(t, -1), o_b.reshape(t, -1), x2, mod3, row(b_gate[l]),
                              w_pa[l].astype(MXU_DTYPE), w_pb[l].astype(MXU_DTYPE),
                              w_o[l].astype(MXU_DTYPE), row(g_ffn[l]), wr_hi, wr_lo, b_r, s)
        route, counts = _route(rlog)
        pos0, pos1, tile_e, tile_flag = _plan(route, counts, n_rows)
        xs = _dispatch(pos0, pos1, h2, n_rows)
        ys = _experts(tile_e, tile_flag, xs, w1[l], w3[l], w2[l])
        x2 = _combine(pos0, pos1, ys, x1, route, mod3, s)
    return x2.reshape(b, s, d)
```

```python
import functools

import jax
import jax.numpy as jnp
from jax import lax
from jax.experimental import pallas as pl
from jax.experimental.pallas import tpu as pltpu

F32 = jnp.float32
I32 = jnp.int32
MXU_DTYPE = jnp.bfloat16

CHUNK = 64
HEAD_DIM = 128
D_LAT = 256
N_IDX_HEADS = 16
D_IDX = 64
TOPK_MAX = 256
N_GROUPS = 8
EXPERTS_PER_GROUP = 8
N_EXPERTS = N_GROUPS * EXPERTS_PER_GROUP
RMS_EPS = 1e-6

LANES = 128
VMEM_LIMIT = 56 * 1024 * 1024

NEG_BIG = -1e30
INT_MIN = -2147483648


def _cparams(sem):
    return pltpu.CompilerParams(dimension_semantics=sem, vmem_limit_bytes=VMEM_LIMIT)


def _dot(a, b):
    return jnp.dot(a, b, preferred_element_type=F32)


def _dot_nt(a, b):
    return lax.dot_general(a, b, (((1,), (1,)), ((), ())), preferred_element_type=F32)


def _rms(x, g):
    return x * lax.rsqrt(jnp.mean(x * x, axis=-1, keepdims=True) + RMS_EPS) * g


def _ada_kernel(c_ref, w_ref, b_ref, o_ref):
    c = c_ref[...]
    a = c * jax.nn.sigmoid(c)
    o_ref[...] = _dot(a.astype(MXU_DTYPE), w_ref[...].astype(MXU_DTYPE)) + b_ref[...]


def _ada(c8, w_ada, b_ada):
    d, n = w_ada.shape
    tn = 1024
    return pl.pallas_call(
        _ada_kernel,
        grid=(n // tn,),
        in_specs=[pl.BlockSpec((8, d), lambda j: (0, 0)),
                  pl.BlockSpec((d, tn), lambda j: (0, j)),
                  pl.BlockSpec((1, tn), lambda j: (0, j))],
        out_specs=pl.BlockSpec((8, tn), lambda j: (0, j)),
        out_shape=jax.ShapeDtypeStruct((8, n), F32),
        compiler_params=_cparams(("arbitrary",)),
        name="ada",
    )(c8, w_ada, b_ada)


NP_COLS = 76 * LANES
U_QA, U_QIDX, U_QB, U_KB, U_VB, U_CKV, U_MISC = 32, 40, 48, 56, 64, 72, 74
TN_PROJ = 4 * LANES
MISC_TILE = (U_MISC * LANES) // TN_PROJ
MISC_OFF = U_MISC * LANES - MISC_TILE * TN_PROJ
MISC_F, MISC_W, MISC_K = 0, 8, 64


def _pack_w_in(w_in):
    d = w_in.shape[0]
    sizes = (1024, 256, 1024, 64, 16, 1024, 1024, 1024, 8, 4096)
    offs = [0]
    for s in sizes:
        offs.append(offs[-1] + s)
    seg = lambda k: w_in[:, offs[k]:offs[k + 1]]
    q_a, c_kv, q_idx, k_idx, w_idx, q_b, k_b, v_b, f_b, gate = [seg(k) for k in range(10)]
    misc = jnp.concatenate([f_b, w_idx, jnp.zeros((d, 40), w_in.dtype), k_idx], axis=1)
    pad = jnp.zeros((d, LANES), w_in.dtype)
    return jnp.concatenate([gate, q_a, q_idx, q_b, k_b, v_b, c_kv, misc, pad], axis=1).astype(MXU_DTYPE)


def _inproj_kernel(x_ref, mod_ref, g_ref, w_ref, o_ref, misc_ref, h_scr):
    j = pl.program_id(1)

    @pl.when(j == 0)
    def _():
        y = _rms(x_ref[...], g_ref[...])
        sh = mod_ref[0, 0:1, :]
        sc = mod_ref[0, 1:2, :]
        h_scr[...] = (y * (1.0 + sc) + sh).astype(h_scr.dtype)

    acc = _dot(h_scr[...], w_ref[...])
    o_ref[...] = acc.astype(o_ref.dtype)

    @pl.when(j == MISC_TILE)
    def _():
        misc_ref[...] = acc[:, MISC_OFF:MISC_OFF + LANES]


def _inproj(x2, mod3, g_mix, w_packed, seq):
    t, d = x2.shape
    tm = min(1024, seq)
    per_b = seq // tm
    return pl.pallas_call(
        _inproj_kernel,
        grid=(t // tm, NP_COLS // TN_PROJ),
        in_specs=[pl.BlockSpec((tm, d), lambda i, j: (i, 0)),
                  pl.BlockSpec((1, 6, d), lambda i, j: (i // per_b, 0, 0)),
                  pl.BlockSpec((1, d), lambda i, j: (0, 0)),
                  pl.BlockSpec((d, TN_PROJ), lambda i, j: (0, j))],
        out_specs=[pl.BlockSpec((tm, TN_PROJ), lambda i, j: (i, j)),
                   pl.BlockSpec((tm, LANES), lambda i, j: (i, 0))],
        out_shape=[jax.ShapeDtypeStruct((t, NP_COLS), MXU_DTYPE),
                   jax.ShapeDtypeStruct((t, LANES), F32)],
        scratch_shapes=[pltpu.VMEM((tm, d), MXU_DTYPE)],
        compiler_params=_cparams(("parallel", "arbitrary")),
        name="inproj",
    )(x2, mod3, g_mix, w_packed)


TK = 256


def _prep_kernel(ckv_ref, misc_ref, gkv_ref, bf_ref, kv_ref, cum_ref):
    s = ckv_ref.shape[1]
    kv_ref[0] = _rms(ckv_ref[0].astype(F32), gkv_ref[...]).astype(kv_ref.dtype)

    r = lax.broadcasted_iota(I32, (LANES, LANES), 0)
    c = lax.broadcasted_iota(I32, (LANES, LANES), 1)
    tri = jnp.where(c <= r, 1.0, 0.0).astype(MXU_DTYPE)
    carry = jnp.zeros((1, LANES), F32)
    for blk in range(s // LANES):
        z = misc_ref[0, blk * LANES:(blk + 1) * LANES, :] + bf_ref[...]
        ls = jnp.minimum(z, 0.0) - jnp.log1p(jnp.exp(-jnp.abs(z)))
        p1 = ls.astype(MXU_DTYPE)
        r1 = ls - p1.astype(F32)
        p2 = r1.astype(MXU_DTYPE)
        p3 = (r1 - p2.astype(F32)).astype(MXU_DTYPE)
        cs = _dot(tri, p1) + _dot(tri, p2) + _dot(tri, p3) + carry
        carry = cs[LANES - 1:LANES, :]
        cum_ref[0, blk * LANES:(blk + 1) * LANES, :] = cs * (HEAD_DIM ** 0.5)


def _prep(proj3, misc3, g_kv, bf_row):
    b, s, _ = proj3.shape
    return pl.pallas_call(
        _prep_kernel,
        grid=(b,),
        in_specs=[pl.BlockSpec((1, s, D_LAT), lambda i: (i, 0, U_CKV * LANES // D_LAT)),
                  pl.BlockSpec((1, s, LANES), lambda i: (i, 0, 0)),
                  pl.BlockSpec((1, D_LAT), lambda i: (0, 0)),
                  pl.BlockSpec((1, LANES), lambda i: (0, 0))],
        out_specs=[pl.BlockSpec((1, s, D_LAT), lambda i: (i, 0, 0)),
                   pl.BlockSpec((1, s, LANES), lambda i: (i, 0, 0))],
        out_shape=[jax.ShapeDtypeStruct((b, s, D_LAT), MXU_DTYPE),
                   jax.ShapeDtypeStruct((b, s, LANES), F32)],
        compiler_params=_cparams(("parallel",)),
        name="prep",
    )(proj3, misc3, g_kv, bf_row)


TQ_A = 256


def _dsa_kernel(qa_ref, qidx_ref, miscq_ref, misck_ref, kv_ref, wuk_ref, wuv_ref, gqa_ref, o_ref,
                key_scr, bias_scr, logit_scr, qlat_scr, mpart_scr, lpart_scr, oacc_scr, *, topk, n_heads):
    i = pl.program_id(1)
    tq = TQ_A
    nk = i + 1
    q0 = i * tq

    wt = miscq_ref[0].T
    wq = wt[MISC_W:MISC_W + N_IDX_HEADS, :] * (D_IDX ** -0.5 * N_IDX_HEADS ** -0.5)

    def score_body(kt, carry):
        k0 = pl.multiple_of(kt * LANES, LANES)
        kx = misck_ref[0, pl.ds(k0, LANES), :][:, MISC_K:MISC_K + D_IDX].astype(MXU_DTYPE)
        acc = jnp.zeros((LANES, tq), F32)
        for h in range(N_IDX_HEADS):
            qh = qidx_ref[0, :, h * D_IDX:(h + 1) * D_IDX]
            d = _dot_nt(kx, qh)
            acc = acc + jnp.maximum(d, 0.0) * wq[h:h + 1, :]
        kpos = k0 + lax.broadcasted_iota(I32, (LANES, tq), 0)
        qpos = q0 + lax.broadcasted_iota(I32, (LANES, tq), 1)
        adm = (kpos // CHUNK) <= (qpos // CHUNK)
        bits = pltpu.bitcast(acc, I32)
        skey = bits ^ ((bits >> 31) & 0x7FFFFFFF)
        key_scr[pl.ds(k0, LANES), :] = jnp.where(adm, skey, INT_MIN)
        return carry

    lax.fori_loop(0, nk * (TK // LANES), score_body, 0)

    def bit_body(it, ans_u):
        bitval = jnp.left_shift(jnp.int32(1), 31 - it)
        cand_u = ans_u | bitval
        cand_s = cand_u ^ INT_MIN

        def cnt_body(kt, cnt):
            k0 = pl.multiple_of(kt * TK, TK)
            tile = key_scr[pl.ds(k0, TK), :]
            return cnt + jnp.sum(jnp.where(tile >= cand_s, 1.0, 0.0), axis=0, keepdims=True)

        cnt = lax.fori_loop(0, nk, cnt_body, jnp.zeros((1, tq), F32))
        return jnp.where(cnt >= topk, cand_u, ans_u)

    ans_u = lax.fori_loop(0, 32, bit_body, jnp.zeros((1, tq), I32))
    thr = jnp.maximum(ans_u ^ INT_MIN, INT_MIN + 1)

    def bias_body(kt, carry):
        k0 = pl.multiple_of(kt * TK, TK)
        sel_t = key_scr[pl.ds(k0, TK), :] >= thr
        bias_scr[kt] = jnp.where(sel_t, 0.0, NEG_BIG).T
        return carry

    lax.fori_loop(0, nk, bias_body, 0)

    for h in range(n_heads):
        ql = _dot(qa_ref[0, :, h * HEAD_DIM:(h + 1) * HEAD_DIM], wuk_ref[h])
        ql = _rms(ql, gqa_ref[...]) * (D_LAT ** -0.5)
        qlat_scr[h * tq:(h + 1) * tq, :] = ql.astype(qlat_scr.dtype)
    mpart_scr[...] = jnp.full(mpart_scr.shape, NEG_BIG, F32)
    lpart_scr[...] = jnp.zeros(lpart_scr.shape, F32)
    oacc_scr[...] = jnp.zeros(oacc_scr.shape, F32)

    def logit_body(kt, carry):
        k0 = pl.multiple_of(kt * TK, TK)
        kvt = kv_ref[0, pl.ds(k0, TK), :]
        qpos = q0 + lax.broadcasted_iota(I32, (tq, TK), 0)
        kpos = k0 + lax.broadcasted_iota(I32, (tq, TK), 1)
        dist = jnp.abs(qpos - kpos).astype(F32)
        bias = bias_scr[kt]
        for h in range(n_heads):
            rows = slice(h * tq, (h + 1) * tq)
            slope = 2.0 ** (-8.0 * (h + 1) / n_heads)
            lg = _dot_nt(qlat_scr[rows, :], kvt) + (bias - slope * dist)
            logit_scr[kt, rows, :] = lg
            mpart_scr[rows, :] = jnp.maximum(mpart_scr[rows, :],
                                             jnp.maximum(lg[:, :LANES], lg[:, LANES:]))
        return carry

    lax.fori_loop(0, nk, logit_body, 0)

    m = jnp.max(mpart_scr[...], axis=-1, keepdims=True)
    mpart_scr[...] = jnp.broadcast_to(m, mpart_scr.shape)

    def pv_body(kt, carry):
        k0 = pl.multiple_of(kt * TK, TK)
        kvt = kv_ref[0, pl.ds(k0, TK), :]
        for h in range(n_heads):
            rows = slice(h * tq, (h + 1) * tq)
            mb = mpart_scr[rows, :]
            lg = logit_scr[kt, rows, :]
            p = jnp.exp(lg - jnp.concatenate([mb, mb], axis=1))
            lpart_scr[rows, :] += p[:, :LANES] + p[:, LANES:]
            oacc_scr[rows, :] += _dot(p.astype(MXU_DTYPE), kvt)
        return carry

    lax.fori_loop(0, nk, pv_body, 0)

    for h in range(n_heads):
        rows = slice(h * tq, (h + 1) * tq)
        lsum = jnp.sum(lpart_scr[rows, :], axis=-1, keepdims=True)
        o_lat = oacc_scr[rows, :] / lsum
        o_ref[0, :, h * HEAD_DIM:(h + 1) * HEAD_DIM] = _dot(
            o_lat.astype(MXU_DTYPE), wuv_ref[h]).astype(o_ref.dtype)


def _dsa(proj3, misc3, kv, w_uk, w_uv, g_qa):
    b, s, _ = proj3.shape
    n_heads = w_uk.shape[0]
    width = n_heads * HEAD_DIM
    topk = min(TOPK_MAX, s // 4)
    tq = TQ_A
    nkt = s // TK
    kern = functools.partial(_dsa_kernel, topk=topk, n_heads=n_heads)
    return pl.pallas_call(
        kern,
        grid=(b, s // tq),
        in_specs=[pl.BlockSpec((1, tq, width), lambda bb, i: (bb, i, U_QA * LANES // width)),
                  pl.BlockSpec((1, tq, N_IDX_HEADS * D_IDX),
                               lambda bb, i: (bb, i, U_QIDX * LANES // (N_IDX_HEADS * D_IDX))),
                  pl.BlockSpec((1, tq, LANES), lambda bb, i: (bb, i, 0)),
                  pl.BlockSpec((1, s, LANES), lambda bb, i: (bb, 0, 0)),
                  pl.BlockSpec((1, s, D_LAT), lambda bb, i: (bb, 0, 0)),
                  pl.BlockSpec((n_heads, HEAD_DIM, D_LAT), lambda bb, i: (0, 0, 0)),
                  pl.BlockSpec((n_heads, D_LAT, HEAD_DIM), lambda bb, i: (0, 0, 0)),
                  pl.BlockSpec((1, D_LAT), lambda bb, i: (0, 0))],
        out_specs=pl.BlockSpec((1, tq, width), lambda bb, i: (bb, i, 0)),
        out_shape=jax.ShapeDtypeStruct((b, s, width), MXU_DTYPE),
        scratch_shapes=[pltpu.VMEM((s, tq), I32),
                        pltpu.VMEM((nkt, tq, TK), F32),
                        pltpu.VMEM((nkt, n_heads * tq, TK), F32),
                        pltpu.VMEM((n_heads * tq, D_LAT), MXU_DTYPE),
                        pltpu.VMEM((n_heads * tq, LANES), F32),
                        pltpu.VMEM((n_heads * tq, LANES), F32),
                        pltpu.VMEM((n_heads * tq, D_LAT), F32)],
        compiler_params=_cparams(("parallel", "arbitrary")),
        name="dsa",
    )(proj3, proj3, misc3, misc3, kv, w_uk, w_uv, g_qa)


TQ_B = 256


HG_B = 4
LOG2E = 1.4426950408889634
assert TQ_B == TK


def _split3(x):
    p1 = x.astype(MXU_DTYPE)
    r1 = x - p1.astype(F32)
    p2 = r1.astype(MXU_DTYPE)
    p3 = (r1 - p2.astype(F32)).astype(MXU_DTYPE)
    return p1, p2, p3


def _fox_kernel(q_ref, k_ref, v_ref, cumq_ref, cumk_ref, gq_ref, gk_ref, o_ref,
                kaug_scr, qaug_scr, logit_scr, mpart_scr, lpart_scr, oacc_scr):
    g = pl.program_id(1)
    qi = pl.program_id(2)
    tq = TQ_B
    s = k_ref.shape[1]
    cscale = (HEAD_DIM ** -0.5) * LOG2E

    rr = lax.broadcasted_iota(I32, (3 * LANES, LANES), 0)
    cc = lax.broadcasted_iota(I32, (3 * LANES, LANES), 1)
    lane_row = lax.broadcasted_iota(I32, (1, LANES), 1)
    ones_q = jnp.where((lane_row >= 3) & (lane_row < 6), 1.0, 0.0)
    ones_k = jnp.where(lane_row < 3, 1.0, 0.0)

    def aug(cum_tile, h, col0, sign, ones_row):
        hit = (((rr == h) & (cc == col0)) | ((rr == LANES + h) & (cc == col0 + 1))
               | ((rr == 2 * LANES + h) & (cc == col0 + 2)))
        e = jnp.where(hit, sign, 0.0).astype(MXU_DTYPE)
        pieces = jnp.concatenate(_split3(cum_tile), axis=1)
        return (_dot(pieces, e) + ones_row).astype(MXU_DTYPE)

    @pl.when(qi == 0)
    def _():
        def kbody(c, carry):
            r0 = pl.multiple_of(c * TK, TK)
            ck = cumk_ref[0, pl.ds(r0, TK), :]
            for hh in range(HG_B):
                kn = _rms(k_ref[0, pl.ds(r0, TK), hh * HEAD_DIM:(hh + 1) * HEAD_DIM].astype(F32), gk_ref[...])
                kaug_scr[hh, pl.ds(r0, TK), 0:HEAD_DIM] = kn.astype(MXU_DTYPE)
                kaug_scr[hh, pl.ds(r0, TK), HEAD_DIM:] = aug(ck, g * HG_B + hh, 3, -1.0, ones_k)
            return carry

        lax.fori_loop(0, s // TK, kbody, 0)

    cq = cumq_ref[0]
    for hh in range(HG_B):
        qn = _rms(q_ref[0, :, hh * HEAD_DIM:(hh + 1) * HEAD_DIM].astype(F32), gq_ref[...])
        qaug_scr[hh, :, 0:HEAD_DIM] = qn.astype(MXU_DTYPE)
        qaug_scr[hh, :, HEAD_DIM:] = aug(cq, g * HG_B + hh, 0, 1.0, ones_q)
    mpart_scr[...] = jnp.full(mpart_scr.shape, NEG_BIG, F32)
    lpart_scr[...] = jnp.zeros(lpart_scr.shape, F32)
    oacc_scr[...] = jnp.zeros(oacc_scr.shape, F32)

    def put_logits(kt, hh, lg):
        logit_scr[kt, hh] = lg
        mpart_scr[hh] = jnp.maximum(mpart_scr[hh], jnp.maximum(lg[:, :LANES], lg[:, LANES:]))

    def raw_logits(kt, hh):
        k0 = pl.multiple_of(kt * TK, TK)
        return _dot_nt(qaug_scr[hh], kaug_scr[hh, pl.ds(k0, TK), :]) * cscale

    def off_body(kt, carry):
        for hh in range(HG_B):
            put_logits(kt, hh, raw_logits(kt, hh))
        return carry

    lax.fori_loop(0, qi, off_body, 0)
    causal = (lax.broadcasted_iota(I32, (tq, TK), 1) <= lax.broadcasted_iota(I32, (tq, TK), 0))
    for hh in range(HG_B):
        put_logits(qi, hh, jnp.where(causal, raw_logits(qi, hh), NEG_BIG))

    for hh in range(HG_B):
        m = jnp.max(mpart_scr[hh], axis=-1, keepdims=True)
        mpart_scr[hh] = jnp.broadcast_to(m, (tq, LANES))

    def pv_body(kt, carry):
        k0 = pl.multiple_of(kt * TK, TK)
        for hh in range(HG_B):
            mb = mpart_scr[hh]
            p = jnp.exp2(logit_scr[kt, hh] - jnp.concatenate([mb, mb], axis=1))
            lpart_scr[hh] += p[:, :LANES] + p[:, LANES:]
            oacc_scr[hh] += _dot(p.astype(MXU_DTYPE),
                                 v_ref[0, pl.ds(k0, TK), hh * HEAD_DIM:(hh + 1) * HEAD_DIM])
        return carry

    lax.fori_loop(0, qi + 1, pv_body, 0)
    for hh in range(HG_B):
        lsum = jnp.sum(lpart_scr[hh], axis=-1, keepdims=True)
        o_ref[0, :, hh * HEAD_DIM:(hh + 1) * HEAD_DIM] = (oacc_scr[hh] / lsum).astype(o_ref.dtype)


def _fox(proj3, cums, g_qb, g_kb, n_heads):
    b, s, _ = proj3.shape
    tq = TQ_B
    gw = HG_B * HEAD_DIM
    return pl.pallas_call(
        _fox_kernel,
        grid=(b, n_heads // HG_B, s // tq),
        in_specs=[pl.BlockSpec((1, tq, gw), lambda bb, g, i: (bb, i, U_QB // HG_B + g)),
                  pl.BlockSpec((1, s, gw), lambda bb, g, i: (bb, 0, U_KB // HG_B + g)),
                  pl.BlockSpec((1, s, gw), lambda bb, g, i: (bb, 0, U_VB // HG_B + g)),
                  pl.BlockSpec((1, tq, LANES), lambda bb, g, i: (bb, i, 0)),
                  pl.BlockSpec((1, s, LANES), lambda bb, g, i: (bb, 0, 0)),
                  pl.BlockSpec((1, HEAD_DIM), lambda bb, g, i: (0, 0)),
                  pl.BlockSpec((1, HEAD_DIM), lambda bb, g, i: (0, 0))],
        out_specs=pl.BlockSpec((1, tq, gw), lambda bb, g, i: (bb, i, g)),
        out_shape=jax.ShapeDtypeStruct((b, s, n_heads * HEAD_DIM), MXU_DTYPE),
        scratch_shapes=[pltpu.VMEM((HG_B, s, 2 * HEAD_DIM), MXU_DTYPE),
                        pltpu.VMEM((HG_B, tq, 2 * HEAD_DIM), MXU_DTYPE),
                        pltpu.VMEM((s // TK, HG_B, tq, TK), F32),
                        pltpu.VMEM((HG_B, tq, LANES), F32),
                        pltpu.VMEM((HG_B, tq, LANES), F32),
                        pltpu.VMEM((HG_B, tq, HEAD_DIM), F32)],
        compiler_params=_cparams(("parallel", "parallel", "arbitrary")),
        name="fox",
    )(proj3, proj3, proj3, cums, cums, g_qb, g_kb)


TM_MERGE = 256


def _merge_kernel(ga_ref, gb_ref, oa_ref, ob_ref, x_ref, mod_ref, bga_ref, bgb_ref, wpa_ref, wpb_ref,
                  wo_ref, gffn_ref, wrh_ref, wrl_ref, br_ref, x1_ref, h2_ref, rl_ref):
    ga = jax.nn.sigmoid(ga_ref[...].astype(F32) + bga_ref[...])
    gb = jax.nn.sigmoid(gb_ref[...].astype(F32) + bgb_ref[...])
    merged = ga * _dot(oa_ref[...], wpa_ref[...]) + gb * _dot(ob_ref[...], wpb_ref[...])
    upd = _dot(merged.astype(MXU_DTYPE), wo_ref[...])
    x1 = x_ref[...] + mod_ref[0, 2:3, :] * upd
    x1_ref[...] = x1
    h2 = _rms(x1, gffn_ref[...]) * (1.0 + mod_ref[0, 4:5, :]) + mod_ref[0, 3:4, :]
    h2_ref[...] = h2
    hh = h2.astype(MXU_DTYPE)
    hl = (h2 - hh.astype(F32)).astype(MXU_DTYPE)
    rl_ref[...] = (_dot(hh, wrh_ref[...]) + _dot(hl, wrh_ref[...]) + _dot(hh, wrl_ref[...])
                   + br_ref[...])


def _merge(proj, o_a, o_b, x2, mod3, b_gate, w_pa, w_pb, w_o, g_ffn, wr_hi, wr_lo, b_r, seq):
    t, d = x2.shape
    tm = TM_MERGE
    per_b = seq // tm
    wa = o_a.shape[1]
    res = lambda shape: pl.BlockSpec(shape, lambda i: (0,) * len(shape), pipeline_mode=pl.Buffered(1))
    return pl.pallas_call(
        _merge_kernel,
        grid=(t // tm,),
        in_specs=[pl.BlockSpec((tm, d), lambda i: (i, 0)),
                  pl.BlockSpec((tm, d), lambda i: (i, 1)),
                  pl.BlockSpec((tm, wa), lambda i: (i, 0)),
                  pl.BlockSpec((tm, wa), lambda i: (i, 0)),
                  pl.BlockSpec((tm, d), lambda i: (i, 0)),
                  pl.BlockSpec((1, 6, d), lambda i: (i // per_b, 0, 0)),
                  pl.BlockSpec((1, d), lambda i: (0, 0)),
                  pl.BlockSpec((1, d), lambda i: (0, 1)),
                  res((wa, d)), res((wa, d)), res((d, d)),
                  pl.BlockSpec((1, d), lambda i: (0, 0)),
                  res((d, LANES)), res((d, LANES)),
                  pl.BlockSpec((1, LANES), lambda i: (0, 0))],
        out_specs=[pl.BlockSpec((tm, d), lambda i: (i, 0)),
                   pl.BlockSpec((tm, d), lambda i: (i, 0)),
                   pl.BlockSpec((tm, LANES), lambda i: (i, 0))],
        out_shape=[jax.ShapeDtypeStruct((t, d), F32),
                   jax.ShapeDtypeStruct((t, d), F32),
                   jax.ShapeDtypeStruct((t, LANES), F32)],
        compiler_params=_cparams(("parallel",)),
        name="merge",
    )(proj, proj, o_a, o_b, x2, mod3, b_gate, b_gate, w_pa, w_pb, w_o, g_ffn, wr_hi, wr_lo, b_r)


TM_ROUTE = 256
TM_ROWS = 128
R_E0, R_E1, R_P0, R_P1 = 0, 1, 4, 5
IT_TILE, IT_E, IT_LO, IT_HI, IT_FLAG, IT_NEXT = range(6)
F_VALID, F_FIRST_OF_EXPERT, F_FIRST_OF_TILE = 1, 2, 4


def _route_kernel(rl_ref, route_ref, post_ref, items_ref, cnt_scr, run_scr, offs_scr, *, n_tiles):
    tm = TM_ROUTE
    sweep = pl.program_id(0)
    step = pl.program_id(1)
    iw = items_ref.shape[1]

    @pl.when((sweep == 0) & (step == 0))
    def _():
        cnt_scr[...] = jnp.zeros(cnt_scr.shape, F32)

    r = rl_ref[...]
    lane = lax.broadcasted_iota(I32, (tm, LANES), 1).astype(F32)
    neg_inf = -jnp.inf
    gmask = lane < N_GROUPS
    gl = jnp.where(gmask, r, neg_inf)
    gmax = jnp.max(gl, axis=-1, keepdims=True)
    gidx = jnp.min(jnp.where(gl == gmax, lane, float(LANES)), axis=-1, keepdims=True)
    gsum = jnp.sum(jnp.where(gmask, jnp.exp(r - gmax), 0.0), axis=-1, keepdims=True)
    gw = 1.0 / gsum
    lo = N_GROUPS + EXPERTS_PER_GROUP * gidx
    emask = (lane >= lo) & (lane < lo + EXPERTS_PER_GROUP)
    el = jnp.where(emask, r, neg_inf)
    v0 = jnp.max(el, axis=-1, keepdims=True)
    i0 = jnp.min(jnp.where(el == v0, lane, float(LANES)), axis=-1, keepdims=True)
    el2 = jnp.where(lane == i0, neg_inf, el)
    v1 = jnp.max(el2, axis=-1, keepdims=True)
    i1 = jnp.min(jnp.where(el2 == v1, lane, float(LANES)), axis=-1, keepdims=True)
    tt = jnp.exp(v1 - v0)
    p0 = gw / (1.0 + tt)
    p1 = gw * tt / (1.0 + tt)
    e0 = i0 - N_GROUPS
    e1 = i1 - N_GROUPS

    hit0 = lane == e0
    hit1 = lane == e1
    oh = jnp.where(hit0 | hit1, 1.0, 0.0)

    @pl.when(sweep == 0)
    def _():
        ones = jnp.ones((tm, LANES), MXU_DTYPE)
        cnt_scr[...] += _dot(oh.T.astype(MXU_DTYPE), ones)

    @pl.when((sweep == 1) & (step == 0))
    def _():
        _plan_items(cnt_scr[...], items_ref, offs_scr, n_tiles, iw)
        run_scr[...] = jnp.zeros(run_scr.shape, F32)

    @pl.when(sweep == 1)
    def _():
        rr = lax.broadcasted_iota(I32, (tm, tm), 0)
        cc = lax.broadcasted_iota(I32, (tm, tm), 1)
        ltri = jnp.where(cc < rr, 1.0, 0.0).astype(MXU_DTYPE)
        before = _dot(ltri, oh.astype(MXU_DTYPE)) + run_scr[...] + offs_scr[...]
        pos0 = jnp.sum(jnp.where(hit0, before, 0.0), axis=-1, keepdims=True)
        pos1 = jnp.sum(jnp.where(hit1, before, 0.0), axis=-1, keepdims=True)
        run_scr[...] = run_scr[...] + jnp.sum(oh, axis=0, keepdims=True)
        out = jnp.zeros((tm, LANES), F32)
        for k, val in ((R_E0, e0), (R_E1, e1), (R_P0, p0), (R_P1, p1)):
            out = jnp.where(lane == k, val, out)
        route_ref[...] = out
        pmat = jnp.where(lane == 0.0, pos0, jnp.where(lane == 1.0, pos1, 0.0))
        post_ref[...] = pmat.T[0:8, :].astype(I32)


def _plan_items(cnt_col, items_ref, offs_scr, n_tiles, iw):
    tmr = float(TM_ROWS)
    sub = lax.broadcasted_iota(I32, (LANES, LANES), 0)
    lan = lax.broadcasted_iota(I32, (LANES, LANES), 1)
    lstrict = jnp.where(lan < sub, 1.0, 0.0).astype(MXU_DTYPE)
    hi = jnp.floor(cnt_col * (1.0 / LANES))
    lo = cnt_col - hi * LANES
    offs_col = _dot(lstrict, hi.astype(MXU_DTYPE)) * LANES + _dot(lstrict, lo.astype(MXU_DTYPE))
    first_t = jnp.floor(offs_col * (1.0 / tmr))
    last_t = jnp.floor((offs_col + cnt_col - 1.0) * (1.0 / tmr))
    n_col = jnp.where(cnt_col > 0.0, last_t - first_t + 1.0, 0.0)
    base_col = _dot(lstrict, n_col.astype(MXU_DTYPE))
    end_col = base_col + n_col
    offs_scr[...] = offs_col.T[0:1, :]

    rep = lambda col: jnp.concatenate([col] * (iw // LANES), axis=1)
    wl = lax.broadcasted_iota(I32, (LANES, iw), 1).astype(F32)
    sub_f = lax.broadcasted_iota(I32, (LANES, iw), 0).astype(F32)
    end_r = rep(end_col)
    w_total = end_r[LANES - 1:LANES, :]
    count_le = lambda v: jnp.sum(jnp.where(end_r <= v, 1.0, 0.0), axis=0, keepdims=True)
    w_row = wl[0:1, :]
    eidx = count_le(w_row)
    sel = sub_f == eidx
    pick = lambda col: jnp.sum(jnp.where(sel, rep(col), 0.0), axis=0, keepdims=True)
    e_base, e_first, e_offs, e_cnt, e_end = (pick(base_col), pick(first_t), pick(offs_col),
                                             pick(cnt_col), pick(end_col))
    valid = w_row < w_total
    tile = e_first + (w_row - e_base)
    row_lo = jnp.maximum(e_offs, tile * tmr) - tile * tmr
    row_hi = jnp.minimum(e_offs + e_cnt, (tile + 1.0) * tmr) - tile * tmr
    first_e = w_row == e_base
    flags = jnp.where(valid, F_VALID + jnp.where(first_e, float(F_FIRST_OF_EXPERT), 0.0)
                      + jnp.where(row_lo == 0.0, float(F_FIRST_OF_TILE), 0.0), 0.0)
    nxt = jnp.where(valid & first_e & (e_end < w_total), count_le(e_end), -1.0)
    e_last = count_le(w_total - 1.0)
    rows = {IT_TILE: jnp.where(valid, tile, n_tiles - 1.0),
            IT_E: jnp.where(valid, eidx, e_last),
            IT_LO: jnp.where(valid, row_lo, 0.0),
            IT_HI: jnp.where(valid, row_hi, 0.0),
            IT_FLAG: flags,
            IT_NEXT: nxt}
    sub8 = lax.broadcasted_iota(I32, (8, iw), 0)
    out = jnp.zeros((8, iw), F32)
    for k, val in rows.items():
        out = jnp.where(sub8 == k, val, out)
    items_ref[...] = out.astype(I32)


def _route(rlog, n_items):
    t = rlog.shape[0]
    tm = TM_ROUTE
    iw = ((n_items + LANES - 1) // LANES) * LANES
    kern = functools.partial(_route_kernel, n_tiles=2 * t // TM_ROWS)
    return pl.pallas_call(
        kern,
        grid=(2, t // tm),
        in_specs=[pl.BlockSpec((tm, LANES), lambda p, i: (i, 0))],
        out_specs=[pl.BlockSpec((tm, LANES), lambda p, i: (i * p, 0)),
                   pl.BlockSpec((8, tm), lambda p, i: (0, i * p)),
                   pl.BlockSpec((8, iw), lambda p, i: (0, 0))],
        out_shape=[jax.ShapeDtypeStruct((t, LANES), F32),
                   jax.ShapeDtypeStruct((8, t), I32),
                   jax.ShapeDtypeStruct((8, iw), I32)],
        scratch_shapes=[pltpu.VMEM((LANES, LANES), F32),
                        pltpu.VMEM((1, LANES), F32),
                        pltpu.VMEM((1, LANES), F32)],
        compiler_params=_cparams(("arbitrary", "arbitrary")),
        name="route",
    )(rlog)


TM_DISP = 256


def _dispatch_kernel(pos0_ref, pos1_ref, h2_ref, xs_ref, sem):
    base = pl.program_id(0) * TM_DISP

    def row_copy(r, p):
        return pltpu.make_async_copy(h2_ref.at[pl.ds(r, 1), :], xs_ref.at[pl.ds(p, 1), :], sem)

    def issue(r, carry):
        row_copy(r, pos0_ref[base + r]).start()
        row_copy(r, pos1_ref[base + r]).start()
        return carry

    lax.fori_loop(0, TM_DISP, issue, 0)

    def drain(r, carry):
        row_copy(0, 0).wait()
        row_copy(0, 0).wait()
        return carry

    lax.fori_loop(0, TM_DISP, drain, 0)


def _dispatch(pos0, pos1, h2):
    t, d = h2.shape
    return pl.pallas_call(
        _dispatch_kernel,
        grid_spec=pltpu.PrefetchScalarGridSpec(
            num_scalar_prefetch=2,
            grid=(t // TM_DISP,),
            in_specs=[pl.BlockSpec((TM_DISP, d), lambda i, p0, p1: (i, 0))],
            out_specs=pl.BlockSpec(memory_space=pl.ANY),
            scratch_shapes=[pltpu.SemaphoreType.DMA(())]),
        out_shape=jax.ShapeDtypeStruct((2 * t, d), h2.dtype),
        compiler_params=pltpu.CompilerParams(dimension_semantics=("arbitrary",),
                                             vmem_limit_bytes=VMEM_LIMIT, has_side_effects=True),
        name="dispatch",
    )(pos0, pos1, h2)


def _experts_kernel(tile_ref, e_ref, lo_ref, hi_ref, flag_ref, next_ref, xs_ref, w1_hbm, w3_hbm, w2_hbm,
                    ys_ref, f1_scr, f3_scr, f2_scr, w1_scr, w3_scr, w2_scr, sems):
    del tile_ref
    w = pl.program_id(0)
    flag = flag_ref[w]

    def weight_copies(e):
        return (pltpu.make_async_copy(w1_hbm.at[e], f1_scr, sems.at[0]),
                pltpu.make_async_copy(w3_hbm.at[e], f3_scr, sems.at[1]),
                pltpu.make_async_copy(w2_hbm.at[e], f2_scr, sems.at[2]))

    @pl.when(w == 0)
    def _():
        for cp in weight_copies(e_ref[0]):
            cp.start()

    @pl.when((flag & F_FIRST_OF_EXPERT) != 0)
    def _():
        cps = weight_copies(e_ref[w])
        cps[0].wait()
        w1_scr[...] = f1_scr[...].astype(w1_scr.dtype)
        cps[1].wait()
        w3_scr[...] = f3_scr[...].astype(w3_scr.dtype)
        cps[2].wait()
        w2_scr[...] = f2_scr[...].astype(w2_scr.dtype)
        nxt = next_ref[w]

        @pl.when(nxt >= 0)
        def _():
            for cp in weight_copies(nxt):
                cp.start()

    @pl.when((flag & F_VALID) != 0)
    def _():
        x = xs_ref[...].astype(MXU_DTYPE)
        a = _dot(x, w1_scr[...])
        u = _dot(x, w3_scr[...])
        hm = (a * jax.nn.sigmoid(a)) * u
        res = _dot(hm.astype(MXU_DTYPE), w2_scr[...])
        row = lax.broadcasted_iota(I32, res.shape, 0)
        mine = (row >= lo_ref[w]) & (row < hi_ref[w])

        @pl.when((flag & F_FIRST_OF_TILE) != 0)
        def _():
            ys_ref[...] = jnp.where(mine, res, 0.0)

        @pl.when((flag & F_FIRST_OF_TILE) == 0)
        def _():
            ys_ref[...] = jnp.where(mine, res, ys_ref[...])


def _experts(items, xs, w1, w3, w2, n_items):
    n_rows, d = xs.shape
    f = w1.shape[2]
    tm = TM_ROWS
    tile_map = lambda w, tile, *_: (tile[w], 0)
    return pl.pallas_call(
        _experts_kernel,
        grid_spec=pltpu.PrefetchScalarGridSpec(
            num_scalar_prefetch=6,
            grid=(n_items,),
            in_specs=[pl.BlockSpec((tm, d), tile_map),
                      pl.BlockSpec(memory_space=pl.ANY),
                      pl.BlockSpec(memory_space=pl.ANY),
                      pl.BlockSpec(memory_space=pl.ANY)],
            out_specs=pl.BlockSpec((tm, d), tile_map),
            scratch_shapes=[pltpu.VMEM((d, f), F32),
                            pltpu.VMEM((d, f), F32),
                            pltpu.VMEM((f, d), F32),
                            pltpu.VMEM((d, f), MXU_DTYPE),
                            pltpu.VMEM((d, f), MXU_DTYPE),
                            pltpu.VMEM((f, d), MXU_DTYPE),
                            pltpu.SemaphoreType.DMA((3,))]),
        out_shape=jax.ShapeDtypeStruct((n_rows, d), F32),
        compiler_params=_cparams(("arbitrary",)),
        name="experts",
    )(items[IT_TILE], items[IT_E], items[IT_LO], items[IT_HI], items[IT_FLAG], items[IT_NEXT],
      xs, w1, w3, w2)


TM_COMB = 256


def _combine_kernel(pos0_ref, pos1_ref, ys_ref, x1_ref, route_ref, mod_ref, o_ref, y0_scr, y1_scr, sem):
    base = pl.program_id(0) * TM_COMB

    def row_copy(p, dst, r):
        return pltpu.make_async_copy(ys_ref.at[pl.ds(p, 1), :], dst.at[pl.ds(r, 1), :], sem)

    def issue(r, carry):
        row_copy(pos0_ref[base + r], y0_scr, r).start()
        row_copy(pos1_ref[base + r], y1_scr, r).start()
        return carry

    lax.fori_loop(0, TM_COMB, issue, 0)

    def drain(r, carry):
        row_copy(0, y0_scr, 0).wait()
        row_copy(0, y1_scr, 0).wait()
        return carry

    lax.fori_loop(0, TM_COMB, drain, 0)

    rt = route_ref[...]
    lane = lax.broadcasted_iota(I32, rt.shape, 1)
    p0 = jnp.sum(jnp.where(lane == R_P0, rt, 0.0), axis=-1, keepdims=True)
    p1 = jnp.sum(jnp.where(lane == R_P1, rt, 0.0), axis=-1, keepdims=True)
    y = p0 * y0_scr[...] + p1 * y1_scr[...]
    o_ref[...] = x1_ref[...] + mod_ref[0, 5:6, :] * y


def _combine(pos0, pos1, ys, x1, route, mod3, seq):
    t, d = x1.shape
    tm = TM_COMB
    per_b = seq // tm
    return pl.pallas_call(
        _combine_kernel,
        grid_spec=pltpu.PrefetchScalarGridSpec(
            num_scalar_prefetch=2,
            grid=(t // tm,),
            in_specs=[pl.BlockSpec(memory_space=pl.ANY),
                      pl.BlockSpec((tm, d), lambda i, p0, p1: (i, 0)),
                      pl.BlockSpec((tm, LANES), lambda i, p0, p1: (i, 0)),
                      pl.BlockSpec((1, 6, d), lambda i, p0, p1: (i // per_b, 0, 0))],
            out_specs=pl.BlockSpec((tm, d), lambda i, p0, p1: (i, 0)),
            scratch_shapes=[pltpu.VMEM((tm, d), F32),
                            pltpu.VMEM((tm, d), F32),
                            pltpu.SemaphoreType.DMA(())]),
        out_shape=jax.ShapeDtypeStruct((t, d), F32),
        compiler_params=_cparams(("arbitrary",)),
        name="combine",
    )(pos0, pos1, ys, x1, route, mod3)


def kernel(x, c, w_ada, b_ada, g_mix, w_in, b_gate, b_forget, w_uk, w_uv, g_qa, g_kv, g_qb, g_kb,
           w_pa, w_pb, w_o, g_ffn, w_rg, b_rg, w_re, b_re, w1, w3, w2):
    b, s, d = x.shape
    depth = w_ada.shape[0]
    t = b * s
    n_heads_b = b_forget.shape[1]
    n_items = 2 * t // TM_ROWS + N_EXPERTS
    c8 = jnp.zeros((8, d), F32).at[:b].set(c)
    x2 = x.reshape(t, d)
    row = lambda v: v.reshape(1, -1)
    for l in range(depth):
        mod3 = _ada(c8, w_ada[l], row(b_ada[l]))[:b].reshape(b, 6, d)
        proj, misc = _inproj(x2, mod3, row(g_mix[l]), _pack_w_in(w_in[l]), s)
        proj3 = proj.reshape(b, s, NP_COLS)
        misc3 = misc.reshape(b, s, LANES)
        bf_row = jnp.zeros((1, LANES), F32).at[0, MISC_F:MISC_F + n_heads_b].set(b_forget[l])
        kv, cums = _prep(proj3, misc3, row(g_kv[l]), bf_row)
        o_a = _dsa(proj3, misc3, kv, w_uk[l].astype(MXU_DTYPE), w_uv[l].astype(MXU_DTYPE), row(g_qa[l]))
        o_b = _fox(proj3, cums, row(g_qb[l]), row(g_kb[l]), n_heads_b)

        w_r = jnp.zeros((d, LANES), F32).at[:, :N_GROUPS].set(w_rg[l])
        w_r = w_r.at[:, N_GROUPS:N_GROUPS + N_EXPERTS].set(w_re[l])
        wr_hi = w_r.astype(MXU_DTYPE)
        wr_lo = (w_r - wr_hi.astype(F32)).astype(MXU_DTYPE)
        b_r = jnp.zeros((1, LANES), F32).at[0, :N_GROUPS].set(b_rg[l])
        b_r = b_r.at[0, N_GROUPS:N_GROUPS + N_EXPERTS].set(b_re[l])
        x1, h2, rlog = _merge(proj, o_a.reshape(t, -1), o_b.reshape(t, -1), x2, mod3, row(b_gate[l]),
                              w_pa[l].astype(MXU_DTYPE), w_pb[l].astype(MXU_DTYPE),
                              w_o[l].astype(MXU_DTYPE), row(g_ffn[l]), wr_hi, wr_lo, b_r, s)
        route, pos_t, items = _route(rlog, n_items)
        pos0, pos1 = pos_t[0], pos_t[1]
        xs = _dispatch(pos0, pos1, h2)
        ys = _experts(items, xs, w1[l], w3[l], w2[l], n_items)
        x2 = _combine(pos0, pos1, ys, x1, route, mod3, s)
    return x2.reshape(b, s, d)
```

```python
import functools

import jax
import jax.numpy as jnp
from jax import lax
from jax.experimental import pallas as pl
from jax.experimental.pallas import tpu as pltpu

F32 = jnp.float32
I32 = jnp.int32
MXU_DTYPE = jnp.bfloat16

CHUNK = 64
HEAD_DIM = 128
D_LAT = 256
N_IDX_HEADS = 16
D_IDX = 64
TOPK_MAX = 256
N_GROUPS = 8
EXPERTS_PER_GROUP = 8
N_EXPERTS = N_GROUPS * EXPERTS_PER_GROUP
RMS_EPS = 1e-6

LANES = 128
VMEM_LIMIT = 56 * 1024 * 1024

NEG_BIG = -1e30
INT_MIN = -2147483648


def _cparams(sem):
    return pltpu.CompilerParams(dimension_semantics=sem, vmem_limit_bytes=VMEM_LIMIT)


def _dot(a, b):
    return jnp.dot(a, b, preferred_element_type=F32)


def _dot_nt(a, b):
    return lax.dot_general(a, b, (((1,), (1,)), ((), ())), preferred_element_type=F32)


def _rms(x, g):
    return x * lax.rsqrt(jnp.mean(x * x, axis=-1, keepdims=True) + RMS_EPS) * g


def _ada_kernel(c_ref, w_ref, b_ref, o_ref):
    c = c_ref[...]
    a = c * jax.nn.sigmoid(c)
    o_ref[...] = _dot(a.astype(MXU_DTYPE), w_ref[...].astype(MXU_DTYPE)) + b_ref[...]


def _ada(c8, w_ada, b_ada):
    d, n = w_ada.shape
    tn = 1024
    return pl.pallas_call(
        _ada_kernel,
        grid=(n // tn,),
        in_specs=[pl.BlockSpec((8, d), lambda j: (0, 0)),
                  pl.BlockSpec((d, tn), lambda j: (0, j)),
                  pl.BlockSpec((1, tn), lambda j: (0, j))],
        out_specs=pl.BlockSpec((8, tn), lambda j: (0, j)),
        out_shape=jax.ShapeDtypeStruct((8, n), F32),
        compiler_params=_cparams(("arbitrary",)),
        name="ada",
    )(c8, w_ada, b_ada)


NP_COLS = 76 * LANES
U_QA, U_QIDX, U_QB, U_KB, U_VB, U_CKV, U_MISC = 32, 40, 48, 56, 64, 72, 74
TN_PROJ = 4 * LANES
MISC_TILE = (U_MISC * LANES) // TN_PROJ
MISC_OFF = U_MISC * LANES - MISC_TILE * TN_PROJ
MISC_F, MISC_W, MISC_K = 0, 8, 64


def _pack_moves():
    sizes = (1024, 256, 1024, 64, 16, 1024, 1024, 1024, 8, 4096)
    src = [0]
    for n in sizes:
        src.append(src[-1] + n)
    q_a, c_kv, q_idx, k_idx, w_idx, q_b, k_b, v_b, f_b, gate = src[:10]
    m = U_MISC * LANES
    return ((gate, 0, 4096), (q_a, U_QA * LANES, 1024), (q_idx, U_QIDX * LANES, 1024),
            (q_b, U_QB * LANES, 1024), (k_b, U_KB * LANES, 1024), (v_b, U_VB * LANES, 1024),
            (c_kv, U_CKV * LANES, 256), (f_b, m + MISC_F, 8), (w_idx, m + MISC_W, 16),
            (k_idx, m + MISC_K, 64))


def _pack_kernel(w_ref, o_ref):
    rows = o_ref.shape[0]
    m = U_MISC * LANES
    o_ref[:, m:m + 2 * LANES] = jnp.zeros((rows, 2 * LANES), o_ref.dtype)
    for src, dst, n in _pack_moves():
        o_ref[:, dst:dst + n] = w_ref[:, src:src + n].astype(o_ref.dtype)


def _pack_w_in(w_in):
    d, n_in = w_in.shape
    tm = 256
    return pl.pallas_call(
        _pack_kernel,
        grid=(d // tm,),
        in_specs=[pl.BlockSpec((tm, n_in), lambda i: (i, 0))],
        out_specs=pl.BlockSpec((tm, NP_COLS), lambda i: (i, 0)),
        out_shape=jax.ShapeDtypeStruct((d, NP_COLS), MXU_DTYPE),
        compiler_params=_cparams(("parallel",)),
        name="pack",
    )(w_in)


def _inproj_kernel(x_ref, mod_ref, g_ref, w_ref, o_ref, misc_ref, h_scr):
    j = pl.program_id(1)

    @pl.when(j == 0)
    def _():
        y = _rms(x_ref[...], g_ref[...])
        sh = mod_ref[0, 0:1, :]
        sc = mod_ref[0, 1:2, :]
        h_scr[...] = (y * (1.0 + sc) + sh).astype(h_scr.dtype)

    acc = _dot(h_scr[...], w_ref[...])
    o_ref[...] = acc.astype(o_ref.dtype)

    @pl.when(j == MISC_TILE)
    def _():
        misc_ref[...] = acc[:, MISC_OFF:MISC_OFF + LANES]


def _inproj(x2, mod3, g_mix, w_packed, seq):
    t, d = x2.shape
    tm = min(1024, seq)
    per_b = seq // tm
    return pl.pallas_call(
        _inproj_kernel,
        grid=(t // tm, NP_COLS // TN_PROJ),
        in_specs=[pl.BlockSpec((tm, d), lambda i, j: (i, 0)),
                  pl.BlockSpec((1, 6, d), lambda i, j: (i // per_b, 0, 0)),
                  pl.BlockSpec((1, d), lambda i, j: (0, 0)),
                  pl.BlockSpec((d, TN_PROJ), lambda i, j: (0, j))],
        out_specs=[pl.BlockSpec((tm, TN_PROJ), lambda i, j: (i, j)),
                   pl.BlockSpec((tm, LANES), lambda i, j: (i, 0))],
        out_shape=[jax.ShapeDtypeStruct((t, NP_COLS), MXU_DTYPE),
                   jax.ShapeDtypeStruct((t, LANES), F32)],
        scratch_shapes=[pltpu.VMEM((tm, d), MXU_DTYPE)],
        compiler_params=_cparams(("parallel", "arbitrary")),
        name="inproj",
    )(x2, mod3, g_mix, w_packed)


TK = 256


def _prep_kernel(ckv_ref, misc_ref, gkv_ref, bf_ref, kv_ref, cum_ref):
    s = ckv_ref.shape[1]
    kv_ref[0] = _rms(ckv_ref[0].astype(F32), gkv_ref[...]).astype(kv_ref.dtype)

    r = lax.broadcasted_iota(I32, (LANES, LANES), 0)
    c = lax.broadcasted_iota(I32, (LANES, LANES), 1)
    tri = jnp.where(c <= r, 1.0, 0.0).astype(MXU_DTYPE)
    carry = jnp.zeros((1, LANES), F32)
    for blk in range(s // LANES):
        z = misc_ref[0, blk * LANES:(blk + 1) * LANES, :] + bf_ref[...]
        ls = jnp.minimum(z, 0.0) - jnp.log1p(jnp.exp(-jnp.abs(z)))
        p1 = ls.astype(MXU_DTYPE)
        r1 = ls - p1.astype(F32)
        p2 = r1.astype(MXU_DTYPE)
        p3 = (r1 - p2.astype(F32)).astype(MXU_DTYPE)
        cs = _dot(tri, p1) + _dot(tri, p2) + _dot(tri, p3) + carry
        carry = cs[LANES - 1:LANES, :]
        cum_ref[0, blk * LANES:(blk + 1) * LANES, :] = cs * (HEAD_DIM ** 0.5)


def _prep(proj3, misc3, g_kv, bf_row):
    b, s, _ = proj3.shape
    return pl.pallas_call(
        _prep_kernel,
        grid=(b,),
        in_specs=[pl.BlockSpec((1, s, D_LAT), lambda i: (i, 0, U_CKV * LANES // D_LAT)),
                  pl.BlockSpec((1, s, LANES), lambda i: (i, 0, 0)),
                  pl.BlockSpec((1, D_LAT), lambda i: (0, 0)),
                  pl.BlockSpec((1, LANES), lambda i: (0, 0))],
        out_specs=[pl.BlockSpec((1, s, D_LAT), lambda i: (i, 0, 0)),
                   pl.BlockSpec((1, s, LANES), lambda i: (i, 0, 0))],
        out_shape=[jax.ShapeDtypeStruct((b, s, D_LAT), MXU_DTYPE),
                   jax.ShapeDtypeStruct((b, s, LANES), F32)],
        compiler_params=_cparams(("parallel",)),
        name="prep",
    )(proj3, misc3, g_kv, bf_row)


TQ_A = 256
N_BISECT = 32


def _dsa_kernel(qa_ref, qidx_ref, miscq_ref, misck_ref, kv_ref, wuk_ref, wuv_ref, gqa_ref, o_ref,
                sc_scr, bias_scr, logit_scr, qlat_scr, mpart_scr, lpart_scr, oacc_scr, *, topk, n_heads):
    i = pl.program_id(1)
    tq = TQ_A
    nk = i + 1
    q0 = i * tq

    wt = miscq_ref[0].T
    wq = wt[MISC_W:MISC_W + N_IDX_HEADS, :] * (D_IDX ** -0.5 * N_IDX_HEADS ** -0.5)

    def fold8(v, op):
        return op(v.reshape(v.shape[0] // 8, 8, tq), axis=0)

    def score_body(kt, carry):
        mx8, mn8 = carry
        for half in range(TK // LANES):
            k0 = pl.multiple_of(kt * TK + half * LANES, LANES)
            kx = misck_ref[0, pl.ds(k0, LANES), :][:, MISC_K:MISC_K + D_IDX].astype(MXU_DTYPE)
            acc = jnp.zeros((LANES, tq), F32)
            for h in range(N_IDX_HEADS):
                qh = qidx_ref[0, :, h * D_IDX:(h + 1) * D_IDX]
                d = _dot_nt(kx, qh)
                acc = acc + jnp.maximum(d, 0.0) * wq[h:h + 1, :]
            kpos = k0 + lax.broadcasted_iota(I32, (LANES, tq), 0)
            qpos = q0 + lax.broadcasted_iota(I32, (LANES, tq), 1)
            adm = (kpos // CHUNK) <= (qpos // CHUNK)
            sc_scr[pl.ds(k0, LANES), :] = jnp.where(adm, acc, NEG_BIG)
            mx8 = jnp.maximum(mx8, fold8(jnp.where(adm, acc, NEG_BIG), jnp.max))
            mn8 = jnp.minimum(mn8, fold8(jnp.where(adm, acc, -NEG_BIG), jnp.min))
        return mx8, mn8

    mx8, mn8 = lax.fori_loop(0, nk, score_body,
                             (jnp.full((8, tq), NEG_BIG, F32), jnp.full((8, tq), -NEG_BIG, F32)))

    def bis_body(it, carry):
        lo, hi = carry
        mid = lo + 0.5 * (hi - lo)

        def cnt_body(kt, c8):
            k0 = pl.multiple_of(kt * TK, TK)
            return c8 + fold8(jnp.where(sc_scr[pl.ds(k0, TK), :] >= mid, 1.0, 0.0), jnp.sum)

        cnt = jnp.sum(lax.fori_loop(0, nk, cnt_body, jnp.zeros((8, tq), F32)), axis=0, keepdims=True)
        ok = cnt >= topk
        return jnp.where(ok, mid, lo), jnp.where(ok, hi, mid)

    thr, _ = lax.fori_loop(0, N_BISECT, bis_body, (jnp.min(mn8, axis=0, keepdims=True),
                                                   jnp.max(mx8, axis=0, keepdims=True)))

    def bias_body(kt, carry):
        k0 = pl.multiple_of(kt * TK, TK)
        sel_t = sc_scr[pl.ds(k0, TK), :] >= thr
        bias_scr[kt] = jnp.where(sel_t, 0.0, NEG_BIG).T
        return carry

    lax.fori_loop(0, nk, bias_body, 0)

    for h in range(n_heads):
        ql = _dot(qa_ref[0, :, h * HEAD_DIM:(h + 1) * HEAD_DIM], wuk_ref[h])
        ql = _rms(ql, gqa_ref[...]) * (D_LAT ** -0.5)
        qlat_scr[h * tq:(h + 1) * tq, :] = ql.astype(qlat_scr.dtype)
    mpart_scr[...] = jnp.full(mpart_scr.shape, NEG_BIG, F32)
    lpart_scr[...] = jnp.zeros(lpart_scr.shape, F32)
    oacc_scr[...] = jnp.zeros(oacc_scr.shape, F32)

    def logit_body(kt, carry):
        k0 = pl.multiple_of(kt * TK, TK)
        kvt = kv_ref[0, pl.ds(k0, TK), :]
        qpos = q0 + lax.broadcasted_iota(I32, (tq, TK), 0)
        kpos = k0 + lax.broadcasted_iota(I32, (tq, TK), 1)
        dist = jnp.abs(qpos - kpos).astype(F32)
        bias = bias_scr[kt]
        for h in range(n_heads):
            rows = slice(h * tq, (h + 1) * tq)
            slope = 2.0 ** (-8.0 * (h + 1) / n_heads)
            lg = _dot_nt(qlat_scr[rows, :], kvt) + (bias - slope * dist)
            logit_scr[kt, rows, :] = lg
            mpart_scr[rows, :] = jnp.maximum(mpart_scr[rows, :],
                                             jnp.maximum(lg[:, :LANES], lg[:, LANES:]))
        return carry

    lax.fori_loop(0, nk, logit_body, 0)

    m = jnp.max(mpart_scr[...], axis=-1, keepdims=True)
    mpart_scr[...] = jnp.broadcast_to(m, mpart_scr.shape)

    def pv_body(kt, carry):
        k0 = pl.multiple_of(kt * TK, TK)
        kvt = kv_ref[0, pl.ds(k0, TK), :]
        for h in range(n_heads):
            rows = slice(h * tq, (h + 1) * tq)
            mb = mpart_scr[rows, :]
            lg = logit_scr[kt, rows, :]
            p = jnp.exp(lg - jnp.concatenate([mb, mb], axis=1))
            lpart_scr[rows, :] += p[:, :LANES] + p[:, LANES:]
            oacc_scr[rows, :] += _dot(p.astype(MXU_DTYPE), kvt)
        return carry

    lax.fori_loop(0, nk, pv_body, 0)

    for h in range(n_heads):
        rows = slice(h * tq, (h + 1) * tq)
        lsum = jnp.sum(lpart_scr[rows, :], axis=-1, keepdims=True)
        o_lat = oacc_scr[rows, :] / lsum
        o_ref[0, :, h * HEAD_DIM:(h + 1) * HEAD_DIM] = _dot(
            o_lat.astype(MXU_DTYPE), wuv_ref[h]).astype(o_ref.dtype)


def _dsa(proj3, misc3, kv, w_uk, w_uv, g_qa):
    b, s, _ = proj3.shape
    n_heads = w_uk.shape[0]
    width = n_heads * HEAD_DIM
    topk = min(TOPK_MAX, s // 4)
    tq = TQ_A
    nkt = s // TK
    kern = functools.partial(_dsa_kernel, topk=topk, n_heads=n_heads)
    return pl.pallas_call(
        kern,
        grid=(b, s // tq),
        in_specs=[pl.BlockSpec((1, tq, width), lambda bb, i: (bb, i, U_QA * LANES // width)),
                  pl.BlockSpec((1, tq, N_IDX_HEADS * D_IDX),
                               lambda bb, i: (bb, i, U_QIDX * LANES // (N_IDX_HEADS * D_IDX))),
                  pl.BlockSpec((1, tq, LANES), lambda bb, i: (bb, i, 0)),
                  pl.BlockSpec((1, s, LANES), lambda bb, i: (bb, 0, 0)),
                  pl.BlockSpec((1, s, D_LAT), lambda bb, i: (bb, 0, 0)),
                  pl.BlockSpec((n_heads, HEAD_DIM, D_LAT), lambda bb, i: (0, 0, 0)),
                  pl.BlockSpec((n_heads, D_LAT, HEAD_DIM), lambda bb, i: (0, 0, 0)),
                  pl.BlockSpec((1, D_LAT), lambda bb, i: (0, 0))],
        out_specs=pl.BlockSpec((1, tq, width), lambda bb, i: (bb, i, 0)),
        out_shape=jax.ShapeDtypeStruct((b, s, width), MXU_DTYPE),
        scratch_shapes=[pltpu.VMEM((s, tq), F32),
                        pltpu.VMEM((nkt, tq, TK), F32),
                        pltpu.VMEM((nkt, n_heads * tq, TK), F32),
                        pltpu.VMEM((n_heads * tq, D_LAT), MXU_DTYPE),
                        pltpu.VMEM((n_heads * tq, LANES), F32),
                        pltpu.VMEM((n_heads * tq, LANES), F32),
                        pltpu.VMEM((n_heads * tq, D_LAT), F32)],
        compiler_params=_cparams(("parallel", "arbitrary")),
        name="dsa",
    )(proj3, proj3, misc3, misc3, kv, w_uk, w_uv, g_qa)


TQ_B = 256


HG_B = 4
LOG2E = 1.4426950408889634
assert TQ_B == TK


def _split3(x):
    p1 = x.astype(MXU_DTYPE)
    r1 = x - p1.astype(F32)
    p2 = r1.astype(MXU_DTYPE)
    p3 = (r1 - p2.astype(F32)).astype(MXU_DTYPE)
    return p1, p2, p3


def _fox_kernel(q_ref, k_ref, v_ref, cumq_ref, cumk_ref, gq_ref, gk_ref, o_ref,
                kaug_scr, qaug_scr, logit_scr, mpart_scr, lpart_scr, oacc_scr):
    g = pl.program_id(1)
    qi = pl.program_id(2)
    tq = TQ_B
    s = k_ref.shape[1]
    cscale = (HEAD_DIM ** -0.5) * LOG2E

    rr = lax.broadcasted_iota(I32, (3 * LANES, LANES), 0)
    cc = lax.broadcasted_iota(I32, (3 * LANES, LANES), 1)
    lane_row = lax.broadcasted_iota(I32, (1, LANES), 1)
    ones_q = jnp.where((lane_row >= 3) & (lane_row < 6), 1.0, 0.0)
    ones_k = jnp.where(lane_row < 3, 1.0, 0.0)

    def aug(cum_tile, h, col0, sign, ones_row):
        hit = (((rr == h) & (cc == col0)) | ((rr == LANES + h) & (cc == col0 + 1))
               | ((rr == 2 * LANES + h) & (cc == col0 + 2)))
        e = jnp.where(hit, sign, 0.0).astype(MXU_DTYPE)
        pieces = jnp.concatenate(_split3(cum_tile), axis=1)
        return (_dot(pieces, e) + ones_row).astype(MXU_DTYPE)

    @pl.when(qi == 0)
    def _():
        def kbody(c, carry):
            r0 = pl.multiple_of(c * TK, TK)
            ck = cumk_ref[0, pl.ds(r0, TK), :]
            for hh in range(HG_B):
                kn = _rms(k_ref[0, pl.ds(r0, TK), hh * HEAD_DIM:(hh + 1) * HEAD_DIM].astype(F32), gk_ref[...])
                kaug_scr[hh, pl.ds(r0, TK), 0:HEAD_DIM] = kn.astype(MXU_DTYPE)
                kaug_scr[hh, pl.ds(r0, TK), HEAD_DIM:] = aug(ck, g * HG_B + hh, 3, -1.0, ones_k)
            return carry

        lax.fori_loop(0, s // TK, kbody, 0)

    cq = cumq_ref[0]
    for hh in range(HG_B):
        qn = _rms(q_ref[0, :, hh * HEAD_DIM:(hh + 1) * HEAD_DIM].astype(F32), gq_ref[...])
        qaug_scr[hh, :, 0:HEAD_DIM] = qn.astype(MXU_DTYPE)
        qaug_scr[hh, :, HEAD_DIM:] = aug(cq, g * HG_B + hh, 0, 1.0, ones_q)
    mpart_scr[...] = jnp.full(mpart_scr.shape, NEG_BIG, F32)
    lpart_scr[...] = jnp.zeros(lpart_scr.shape, F32)
    oacc_scr[...] = jnp.zeros(oacc_scr.shape, F32)

    def put_logits(kt, hh, lg):
        logit_scr[kt, hh] = lg
        mpart_scr[hh] = jnp.maximum(mpart_scr[hh], jnp.maximum(lg[:, :LANES], lg[:, LANES:]))

    def raw_logits(kt, hh):
        k0 = pl.multiple_of(kt * TK, TK)
        return _dot_nt(qaug_scr[hh], kaug_scr[hh, pl.ds(k0, TK), :]) * cscale

    def off_body(kt, carry):
        for hh in range(HG_B):
            put_logits(kt, hh, raw_logits(kt, hh))
        return carry

    lax.fori_loop(0, qi, off_body, 0)
    causal = (lax.broadcasted_iota(I32, (tq, TK), 1) <= lax.broadcasted_iota(I32, (tq, TK), 0))
    for hh in range(HG_B):
        put_logits(qi, hh, jnp.where(causal, raw_logits(qi, hh), NEG_BIG))

    for hh in range(HG_B):
        m = jnp.max(mpart_scr[hh], axis=-1, keepdims=True)
        mpart_scr[hh] = jnp.broadcast_to(m, (tq, LANES))

    def pv_body(kt, carry):
        k0 = pl.multiple_of(kt * TK, TK)
        for hh in range(HG_B):
            mb = mpart_scr[hh]
            p = jnp.exp2(logit_scr[kt, hh] - jnp.concatenate([mb, mb], axis=1))
            lpart_scr[hh] += p[:, :LANES] + p[:, LANES:]
            oacc_scr[hh] += _dot(p.astype(MXU_DTYPE),
                                 v_ref[0, pl.ds(k0, TK), hh * HEAD_DIM:(hh + 1) * HEAD_DIM])
        return carry

    lax.fori_loop(0, qi + 1, pv_body, 0)
    for hh in range(HG_B):
        lsum = jnp.sum(lpart_scr[hh], axis=-1, keepdims=True)
        o_ref[0, :, hh * HEAD_DIM:(hh + 1) * HEAD_DIM] = (oacc_scr[hh] / lsum).astype(o_ref.dtype)


def _fox(proj3, cums, g_qb, g_kb, n_heads):
    b, s, _ = proj3.shape
    tq = TQ_B
    gw = HG_B * HEAD_DIM
    return pl.pallas_call(
        _fox_kernel,
        grid=(b, n_heads // HG_B, s // tq),
        in_specs=[pl.BlockSpec((1, tq, gw), lambda bb, g, i: (bb, i, U_QB // HG_B + g)),
                  pl.BlockSpec((1, s, gw), lambda bb, g, i: (bb, 0, U_KB // HG_B + g)),
                  pl.BlockSpec((1, s, gw), lambda bb, g, i: (bb, 0, U_VB // HG_B + g)),
                  pl.BlockSpec((1, tq, LANES), lambda bb, g, i: (bb, i, 0)),
                  pl.BlockSpec((1, s, LANES), lambda bb, g, i: (bb, 0, 0)),
                  pl.BlockSpec((1, HEAD_DIM), lambda bb, g, i: (0, 0)),
                  pl.BlockSpec((1, HEAD_DIM), lambda bb, g, i: (0, 0))],
        out_specs=pl.BlockSpec((1, tq, gw), lambda bb, g, i: (bb, i, g)),
        out_shape=jax.ShapeDtypeStruct((b, s, n_heads * HEAD_DIM), MXU_DTYPE),
        scratch_shapes=[pltpu.VMEM((HG_B, s, 2 * HEAD_DIM), MXU_DTYPE),
                        pltpu.VMEM((HG_B, tq, 2 * HEAD_DIM), MXU_DTYPE),
                        pltpu.VMEM((s // TK, HG_B, tq, TK), F32),
                        pltpu.VMEM((HG_B, tq, LANES), F32),
                        pltpu.VMEM((HG_B, tq, LANES), F32),
                        pltpu.VMEM((HG_B, tq, HEAD_DIM), F32)],
        compiler_params=_cparams(("parallel", "parallel", "arbitrary")),
        name="fox",
    )(proj3, proj3, proj3, cums, cums, g_qb, g_kb)


TM_MERGE = 256


def _merge_kernel(ga_ref, gb_ref, oa_ref, ob_ref, x_ref, mod_ref, bga_ref, bgb_ref, wpa_ref, wpb_ref,
                  wo_ref, gffn_ref, wrh_ref, wrl_ref, br_ref, x1_ref, h2_ref, rl_ref):
    ga = jax.nn.sigmoid(ga_ref[...].astype(F32) + bga_ref[...])
    gb = jax.nn.sigmoid(gb_ref[...].astype(F32) + bgb_ref[...])
    merged = ga * _dot(oa_ref[...], wpa_ref[...]) + gb * _dot(ob_ref[...], wpb_ref[...])
    upd = _dot(merged.astype(MXU_DTYPE), wo_ref[...])
    x1 = x_ref[...] + mod_ref[0, 2:3, :] * upd
    x1_ref[...] = x1
    h2 = _rms(x1, gffn_ref[...]) * (1.0 + mod_ref[0, 4:5, :]) + mod_ref[0, 3:4, :]
    h2_ref[...] = h2
    hh = h2.astype(MXU_DTYPE)
    hl = (h2 - hh.astype(F32)).astype(MXU_DTYPE)
    rl_ref[...] = (_dot(hh, wrh_ref[...]) + _dot(hl, wrh_ref[...]) + _dot(hh, wrl_ref[...])
                   + br_ref[...])


def _merge(proj, o_a, o_b, x2, mod3, b_gate, w_pa, w_pb, w_o, g_ffn, wr_hi, wr_lo, b_r, seq):
    t, d = x2.shape
    tm = TM_MERGE
    per_b = seq // tm
    wa = o_a.shape[1]
    res = lambda shape: pl.BlockSpec(shape, lambda i: (0,) * len(shape), pipeline_mode=pl.Buffered(1))
    return pl.pallas_call(
        _merge_kernel,
        grid=(t // tm,),
        in_specs=[pl.BlockSpec((tm, d), lambda i: (i, 0)),
                  pl.BlockSpec((tm, d), lambda i: (i, 1)),
                  pl.BlockSpec((tm, wa), lambda i: (i, 0)),
                  pl.BlockSpec((tm, wa), lambda i: (i, 0)),
                  pl.BlockSpec((tm, d), lambda i: (i, 0)),
                  pl.BlockSpec((1, 6, d), lambda i: (i // per_b, 0, 0)),
                  pl.BlockSpec((1, d), lambda i: (0, 0)),
                  pl.BlockSpec((1, d), lambda i: (0, 1)),
                  res((wa, d)), res((wa, d)), res((d, d)),
                  pl.BlockSpec((1, d), lambda i: (0, 0)),
                  res((d, LANES)), res((d, LANES)),
                  pl.BlockSpec((1, LANES), lambda i: (0, 0))],
        out_specs=[pl.BlockSpec((tm, d), lambda i: (i, 0)),
                   pl.BlockSpec((tm, d), lambda i: (i, 0)),
                   pl.BlockSpec((tm, LANES), lambda i: (i, 0))],
        out_shape=[jax.ShapeDtypeStruct((t, d), F32),
                   jax.ShapeDtypeStruct((t, d), F32),
                   jax.ShapeDtypeStruct((t, LANES), F32)],
        compiler_params=_cparams(("parallel",)),
        name="merge",
    )(proj, proj, o_a, o_b, x2, mod3, b_gate, b_gate, w_pa, w_pb, w_o, g_ffn, wr_hi, wr_lo, b_r)


TM_ROUTE = 256
TM_ROWS = 128
R_E0, R_E1, R_P0, R_P1 = 0, 1, 4, 5
IT_TILE, IT_E, IT_LO, IT_HI, IT_FLAG, IT_NEXT = range(6)
F_VALID, F_FIRST_OF_EXPERT, F_FIRST_OF_TILE = 1, 2, 4


def _route_kernel(rl_ref, route_ref, post_ref, items_ref, cnt_scr, run_scr, offs_scr, *, n_tiles):
    tm = TM_ROUTE
    sweep = pl.program_id(0)
    step = pl.program_id(1)
    iw = items_ref.shape[1]

    @pl.when((sweep == 0) & (step == 0))
    def _():
        cnt_scr[...] = jnp.zeros(cnt_scr.shape, F32)

    r = rl_ref[...]
    lane = lax.broadcasted_iota(I32, (tm, LANES), 1).astype(F32)
    neg_inf = -jnp.inf
    gmask = lane < N_GROUPS
    gl = jnp.where(gmask, r, neg_inf)
    gmax = jnp.max(gl, axis=-1, keepdims=True)
    gidx = jnp.min(jnp.where(gl == gmax, lane, float(LANES)), axis=-1, keepdims=True)
    gsum = jnp.sum(jnp.where(gmask, jnp.exp(r - gmax), 0.0), axis=-1, keepdims=True)
    gw = 1.0 / gsum
    lo = N_GROUPS + EXPERTS_PER_GROUP * gidx
    emask = (lane >= lo) & (lane < lo + EXPERTS_PER_GROUP)
    el = jnp.where(emask, r, neg_inf)
    v0 = jnp.max(el, axis=-1, keepdims=True)
    i0 = jnp.min(jnp.where(el == v0, lane, float(LANES)), axis=-1, keepdims=True)
    el2 = jnp.where(lane == i0, neg_inf, el)
    v1 = jnp.max(el2, axis=-1, keepdims=True)
    i1 = jnp.min(jnp.where(el2 == v1, lane, float(LANES)), axis=-1, keepdims=True)
    tt = jnp.exp(v1 - v0)
    p0 = gw / (1.0 + tt)
    p1 = gw * tt / (1.0 + tt)
    e0 = i0 - N_GROUPS
    e1 = i1 - N_GROUPS

    hit0 = lane == e0
    hit1 = lane == e1
    oh = jnp.where(hit0 | hit1, 1.0, 0.0)

    @pl.when(sweep == 0)
    def _():
        ones = jnp.ones((tm, LANES), MXU_DTYPE)
        cnt_scr[...] += _dot(oh.T.astype(MXU_DTYPE), ones)

    @pl.when((sweep == 1) & (step == 0))
    def _():
        _plan_items(cnt_scr[...], items_ref, offs_scr, n_tiles, iw)
        run_scr[...] = jnp.zeros(run_scr.shape, F32)

    @pl.when(sweep == 1)
    def _():
        rr = lax.broadcasted_iota(I32, (tm, tm), 0)
        cc = lax.broadcasted_iota(I32, (tm, tm), 1)
        ltri = jnp.where(cc < rr, 1.0, 0.0).astype(MXU_DTYPE)
        before = _dot(ltri, oh.astype(MXU_DTYPE)) + run_scr[...] + offs_scr[...]
        pos0 = jnp.sum(jnp.where(hit0, before, 0.0), axis=-1, keepdims=True)
        pos1 = jnp.sum(jnp.where(hit1, before, 0.0), axis=-1, keepdims=True)
        run_scr[...] = run_scr[...] + jnp.sum(oh, axis=0, keepdims=True)
        out = jnp.zeros((tm, LANES), F32)
        for k, val in ((R_E0, e0), (R_E1, e1), (R_P0, p0), (R_P1, p1)):
            out = jnp.where(lane == k, val, out)
        route_ref[...] = out
        pmat = jnp.where(lane == 0.0, pos0, jnp.where(lane == 1.0, pos1, 0.0))
        post_ref[...] = pmat.T[0:8, :].astype(I32)


def _plan_items(cnt_col, items_ref, offs_scr, n_tiles, iw):
    tmr = float(TM_ROWS)
    sub = lax.broadcasted_iota(I32, (LANES, LANES), 0)
    lan = lax.broadcasted_iota(I32, (LANES, LANES), 1)
    lstrict = jnp.where(lan < sub, 1.0, 0.0).astype(MXU_DTYPE)
    hi = jnp.floor(cnt_col * (1.0 / LANES))
    lo = cnt_col - hi * LANES
    offs_col = _dot(lstrict, hi.astype(MXU_DTYPE)) * LANES + _dot(lstrict, lo.astype(MXU_DTYPE))
    first_t = jnp.floor(offs_col * (1.0 / tmr))
    last_t = jnp.floor((offs_col + cnt_col - 1.0) * (1.0 / tmr))
    n_col = jnp.where(cnt_col > 0.0, last_t - first_t + 1.0, 0.0)
    base_col = _dot(lstrict, n_col.astype(MXU_DTYPE))
    end_col = base_col + n_col
    offs_scr[...] = offs_col.T[0:1, :]

    rep = lambda col: jnp.concatenate([col] * (iw // LANES), axis=1)
    wl = lax.broadcasted_iota(I32, (LANES, iw), 1).astype(F32)
    sub_f = lax.broadcasted_iota(I32, (LANES, iw), 0).astype(F32)
    end_r = rep(end_col)
    w_total = end_r[LANES - 1:LANES, :]
    count_le = lambda v: jnp.sum(jnp.where(end_r <= v, 1.0, 0.0), axis=0, keepdims=True)
    w_row = wl[0:1, :]
    eidx = count_le(w_row)
    sel = sub_f == eidx
    pick = lambda col: jnp.sum(jnp.where(sel, rep(col), 0.0), axis=0, keepdims=True)
    e_base, e_first, e_offs, e_cnt, e_end = (pick(base_col), pick(first_t), pick(offs_col),
                                             pick(cnt_col), pick(end_col))
    valid = w_row < w_total
    tile = e_first + (w_row - e_base)
    row_lo = jnp.maximum(e_offs, tile * tmr) - tile * tmr
    row_hi = jnp.minimum(e_offs + e_cnt, (tile + 1.0) * tmr) - tile * tmr
    first_e = w_row == e_base
    flags = jnp.where(valid, F_VALID + jnp.where(first_e, float(F_FIRST_OF_EXPERT), 0.0)
                      + jnp.where(row_lo == 0.0, float(F_FIRST_OF_TILE), 0.0), 0.0)
    nxt = jnp.where(valid & first_e & (e_end < w_total), count_le(e_end), -1.0)
    e_last = count_le(w_total - 1.0)
    rows = {IT_TILE: jnp.where(valid, tile, n_tiles - 1.0),
            IT_E: jnp.where(valid, eidx, e_last),
            IT_LO: jnp.where(valid, row_lo, 0.0),
            IT_HI: jnp.where(valid, row_hi, 0.0),
            IT_FLAG: flags,
            IT_NEXT: nxt}
    sub8 = lax.broadcasted_iota(I32, (8, iw), 0)
    out = jnp.zeros((8, iw), F32)
    for k, val in rows.items():
        out = jnp.where(sub8 == k, val, out)
    items_ref[...] = out.astype(I32)


def _route(rlog, n_items):
    t = rlog.shape[0]
    tm = TM_ROUTE
    iw = ((n_items + LANES - 1) // LANES) * LANES
    kern = functools.partial(_route_kernel, n_tiles=2 * t // TM_ROWS)
    return pl.pallas_call(
        kern,
        grid=(2, t // tm),
        in_specs=[pl.BlockSpec((tm, LANES), lambda p, i: (i, 0))],
        out_specs=[pl.BlockSpec((tm, LANES), lambda p, i: (i * p, 0)),
                   pl.BlockSpec((8, tm), lambda p, i: (0, i * p)),
                   pl.BlockSpec((8, iw), lambda p, i: (0, 0))],
        out_shape=[jax.ShapeDtypeStruct((t, LANES), F32),
                   jax.ShapeDtypeStruct((8, t), I32),
                   jax.ShapeDtypeStruct((8, iw), I32)],
        scratch_shapes=[pltpu.VMEM((LANES, LANES), F32),
                        pltpu.VMEM((1, LANES), F32),
                        pltpu.VMEM((1, LANES), F32)],
        compiler_params=_cparams(("arbitrary", "arbitrary")),
        name="route",
    )(rlog)


TM_DISP = 256


def _dispatch_kernel(pos0_ref, pos1_ref, h2_ref, xs_ref, sem):
    base = pl.program_id(0) * TM_DISP

    def row_copy(r, p):
        return pltpu.make_async_copy(h2_ref.at[pl.ds(r, 1), :], xs_ref.at[pl.ds(p, 1), :], sem)

    def issue(r, carry):
        row_copy(r, pos0_ref[base + r]).start(priority=0)
        row_copy(r, pos1_ref[base + r]).start(priority=1)
        return carry

    lax.fori_loop(0, TM_DISP, issue, 0, unroll=8)

    def drain(r, carry):
        row_copy(0, 0).wait()
        row_copy(0, 0).wait()
        return carry

    lax.fori_loop(0, TM_DISP, drain, 0, unroll=8)


def _dispatch(pos0, pos1, h2):
    t, d = h2.shape
    return pl.pallas_call(
        _dispatch_kernel,
        grid_spec=pltpu.PrefetchScalarGridSpec(
            num_scalar_prefetch=2,
            grid=(t // TM_DISP,),
            in_specs=[pl.BlockSpec((TM_DISP, d), lambda i, p0, p1: (i, 0))],
            out_specs=pl.BlockSpec(memory_space=pl.ANY),
            scratch_shapes=[pltpu.SemaphoreType.DMA(())]),
        out_shape=jax.ShapeDtypeStruct((2 * t, d), h2.dtype),
        compiler_params=pltpu.CompilerParams(dimension_semantics=("arbitrary",),
                                             vmem_limit_bytes=VMEM_LIMIT, has_side_effects=True),
        name="dispatch",
    )(pos0, pos1, h2)


def _experts_kernel(tile_ref, e_ref, lo_ref, hi_ref, flag_ref, next_ref, xs_ref, w1_hbm, w3_hbm, w2_hbm,
                    ys_ref, f1_scr, f3_scr, f2_scr, w1_scr, w3_scr, w2_scr, sems):
    del tile_ref
    w = pl.program_id(0)
    flag = flag_ref[w]

    mats = ((w1_hbm, f1_scr, w1_scr), (w3_hbm, f3_scr, w3_scr), (w2_hbm, f2_scr, w2_scr))

    def weight_copy(k, e):
        return pltpu.make_async_copy(mats[k][0].at[e], mats[k][1], sems.at[k])

    @pl.when(w == 0)
    def _():
        for k in range(3):
            weight_copy(k, e_ref[0]).start()

    @pl.when((flag & F_FIRST_OF_EXPERT) != 0)
    def _():
        nxt = next_ref[w]
        for k in range(3):
            weight_copy(k, e_ref[w]).wait()
            mats[k][2][...] = mats[k][1][...].astype(mats[k][2].dtype)

            @pl.when(nxt >= 0)
            def _():
                weight_copy(k, nxt).start()

    @pl.when((flag & F_VALID) != 0)
    def _():
        x = xs_ref[...].astype(MXU_DTYPE)
        a = _dot(x, w1_scr[...])
        u = _dot(x, w3_scr[...])
        hm = (a * jax.nn.sigmoid(a)) * u
        res = _dot(hm.astype(MXU_DTYPE), w2_scr[...])
        row = lax.broadcasted_iota(I32, res.shape, 0)
        mine = (row >= lo_ref[w]) & (row < hi_ref[w])

        @pl.when((flag & F_FIRST_OF_TILE) != 0)
        def _():
            ys_ref[...] = jnp.where(mine, res, 0.0)

        @pl.when((flag & F_FIRST_OF_TILE) == 0)
        def _():
            ys_ref[...] = jnp.where(mine, res, ys_ref[...])


def _experts(items, xs, w1, w3, w2, n_items):
    n_rows, d = xs.shape
    f = w1.shape[2]
    tm = TM_ROWS
    tile_map = lambda w, tile, *_: (tile[w], 0)
    return pl.pallas_call(
        _experts_kernel,
        grid_spec=pltpu.PrefetchScalarGridSpec(
            num_scalar_prefetch=6,
            grid=(n_items,),
            in_specs=[pl.BlockSpec((tm, d), tile_map),
                      pl.BlockSpec(memory_space=pl.ANY),
                      pl.BlockSpec(memory_space=pl.ANY),
                      pl.BlockSpec(memory_space=pl.ANY)],
            out_specs=pl.BlockSpec((tm, d), tile_map),
            scratch_shapes=[pltpu.VMEM((d, f), F32),
                            pltpu.VMEM((d, f), F32),
                            pltpu.VMEM((f, d), F32),
                            pltpu.VMEM((d, f), MXU_DTYPE),
                            pltpu.VMEM((d, f), MXU_DTYPE),
                            pltpu.VMEM((f, d), MXU_DTYPE),
                            pltpu.SemaphoreType.DMA((3,))]),
        out_shape=jax.ShapeDtypeStruct((n_rows, d), F32),
        compiler_params=_cparams(("arbitrary",)),
        name="experts",
    )(items[IT_TILE], items[IT_E], items[IT_LO], items[IT_HI], items[IT_FLAG], items[IT_NEXT],
      xs, w1, w3, w2)


TM_COMB = 256


def _combine_kernel(pos0_ref, pos1_ref, ys_ref, x1_ref, route_ref, mod_ref, o_ref, y0_scr, y1_scr, sem):
    base = pl.program_id(0) * TM_COMB

    def row_copy(p, dst, r):
        return pltpu.make_async_copy(ys_ref.at[pl.ds(p, 1), :], dst.at[pl.ds(r, 1), :], sem)

    def issue(r, carry):
        row_copy(pos0_ref[base + r], y0_scr, r).start(priority=0)
        row_copy(pos1_ref[base + r], y1_scr, r).start(priority=1)
        return carry

    lax.fori_loop(0, TM_COMB, issue, 0, unroll=8)

    def drain(r, carry):
        row_copy(0, y0_scr, 0).wait()
        row_copy(0, y1_scr, 0).wait()
        return carry

    lax.fori_loop(0, TM_COMB, drain, 0, unroll=8)

    rt = route_ref[...]
    lane = lax.broadcasted_iota(I32, rt.shape, 1)
    p0 = jnp.sum(jnp.where(lane == R_P0, rt, 0.0), axis=-1, keepdims=True)
    p1 = jnp.sum(jnp.where(lane == R_P1, rt, 0.0), axis=-1, keepdims=True)
    y = p0 * y0_scr[...] + p1 * y1_scr[...]
    o_ref[...] = x1_ref[...] + mod_ref[0, 5:6, :] * y


def _combine(pos0, pos1, ys, x1, route, mod3, seq):
    t, d = x1.shape
    tm = TM_COMB
    per_b = seq // tm
    return pl.pallas_call(
        _combine_kernel,
        grid_spec=pltpu.PrefetchScalarGridSpec(
            num_scalar_prefetch=2,
            grid=(t // tm,),
            in_specs=[pl.BlockSpec(memory_space=pl.ANY),
                      pl.BlockSpec((tm, d), lambda i, p0, p1: (i, 0)),
                      pl.BlockSpec((tm, LANES), lambda i, p0, p1: (i, 0)),
                      pl.BlockSpec((1, 6, d), lambda i, p0, p1: (i // per_b, 0, 0))],
            out_specs=pl.BlockSpec((tm, d), lambda i, p0, p1: (i, 0)),
            scratch_shapes=[pltpu.VMEM((tm, d), F32),
                            pltpu.VMEM((tm, d), F32),
                            pltpu.SemaphoreType.DMA(())]),
        out_shape=jax.ShapeDtypeStruct((t, d), F32),
        compiler_params=_cparams(("arbitrary",)),
        name="combine",
    )(pos0, pos1, ys, x1, route, mod3)


def kernel(x, c, w_ada, b_ada, g_mix, w_in, b_gate, b_forget, w_uk, w_uv, g_qa, g_kv, g_qb, g_kb,
           w_pa, w_pb, w_o, g_ffn, w_rg, b_rg, w_re, b_re, w1, w3, w2):
    b, s, d = x.shape
    depth = w_ada.shape[0]
    t = b * s
    n_heads_b = b_forget.shape[1]
    n_items = 2 * t // TM_ROWS + N_EXPERTS
    c8 = jnp.zeros((8, d), F32).at[:b].set(c)
    x2 = x.reshape(t, d)
    row = lambda v: v.reshape(1, -1)
    for l in range(depth):
        mod3 = _ada(c8, w_ada[l], row(b_ada[l]))[:b].reshape(b, 6, d)
        proj, misc = _inproj(x2, mod3, row(g_mix[l]), _pack_w_in(w_in[l]), s)
        proj3 = proj.reshape(b, s, NP_COLS)
        misc3 = misc.reshape(b, s, LANES)
        bf_row = jnp.zeros((1, LANES), F32).at[0, MISC_F:MISC_F + n_heads_b].set(b_forget[l])
        kv, cums = _prep(proj3, misc3, row(g_kv[l]), bf_row)
        o_a = _dsa(proj3, misc3, kv, w_uk[l].astype(MXU_DTYPE), w_uv[l].astype(MXU_DTYPE), row(g_qa[l]))
        o_b = _fox(proj3, cums, row(g_qb[l]), row(g_kb[l]), n_heads_b)

        w_r = jnp.zeros((d, LANES), F32).at[:, :N_GROUPS].set(w_rg[l])
        w_r = w_r.at[:, N_GROUPS:N_GROUPS + N_EXPERTS].set(w_re[l])
        wr_hi = w_r.astype(MXU_DTYPE)
        wr_lo = (w_r - wr_hi.astype(F32)).astype(MXU_DTYPE)
        b_r = jnp.zeros((1, LANES), F32).at[0, :N_GROUPS].set(b_rg[l])
        b_r = b_r.at[0, N_GROUPS:N_GROUPS + N_EXPERTS].set(b_re[l])
        x1, h2, rlog = _merge(proj, o_a.reshape(t, -1), o_b.reshape(t, -1), x2, mod3, row(b_gate[l]),
                              w_pa[l].astype(MXU_DTYPE), w_pb[l].astype(MXU_DTYPE),
                              w_o[l].astype(MXU_DTYPE), row(g_ffn[l]), wr_hi, wr_lo, b_r, s)
        route, pos_t, items = _route(rlog, n_items)
        pos0, pos1 = pos_t[0], pos_t[1]
        xs = _dispatch(pos0, pos1, h2)
        ys = _experts(items, xs, w1[l], w3[l], w2[l], n_items)
        x2 = _combine(pos0, pos1, ys, x1, route, mod3, s)
    return x2.reshape(b, s, d)
```

```python
import functools

import jax
import jax.numpy as jnp
from jax import lax
from jax.experimental import pallas as pl
from jax.experimental.pallas import tpu as pltpu

F32 = jnp.float32
I32 = jnp.int32
MXU_DTYPE = jnp.bfloat16

CHUNK = 64
HEAD_DIM = 128
D_LAT = 256
N_IDX_HEADS = 16
D_IDX = 64
TOPK_MAX = 256
N_GROUPS = 8
EXPERTS_PER_GROUP = 8
N_EXPERTS = N_GROUPS * EXPERTS_PER_GROUP
RMS_EPS = 1e-6

LANES = 128
VMEM_LIMIT = 56 * 1024 * 1024

NEG_BIG = -1e30
INT_MIN = -2147483648


def _cparams(sem):
    return pltpu.CompilerParams(dimension_semantics=sem, vmem_limit_bytes=VMEM_LIMIT)


def _dot(a, b):
    return jnp.dot(a, b, preferred_element_type=F32)


def _dot_nt(a, b):
    return lax.dot_general(a, b, (((1,), (1,)), ((), ())), preferred_element_type=F32)


def _rms(x, g):
    return x * lax.rsqrt(jnp.mean(x * x, axis=-1, keepdims=True) + RMS_EPS) * g


def _ada_kernel(c_ref, w_ref, b_ref, o_ref):
    c = c_ref[...]
    a = c * jax.nn.sigmoid(c)
    o_ref[...] = _dot(a.astype(MXU_DTYPE), w_ref[...].astype(MXU_DTYPE)) + b_ref[...]


def _ada(c8, w_ada, b_ada):
    d, n = w_ada.shape
    tn = 1024
    return pl.pallas_call(
        _ada_kernel,
        grid=(n // tn,),
        in_specs=[pl.BlockSpec((8, d), lambda j: (0, 0)),
                  pl.BlockSpec((d, tn), lambda j: (0, j)),
                  pl.BlockSpec((1, tn), lambda j: (0, j))],
        out_specs=pl.BlockSpec((8, tn), lambda j: (0, j)),
        out_shape=jax.ShapeDtypeStruct((8, n), F32),
        compiler_params=_cparams(("arbitrary",)),
        name="ada",
    )(c8, w_ada, b_ada)


NP_COLS = 76 * LANES
U_QA, U_QIDX, U_QB, U_KB, U_VB, U_CKV, U_MISC = 32, 40, 48, 56, 64, 72, 74
TN_PROJ = 4 * LANES
TM_PROJ = 1024
MISC_TILE = (U_MISC * LANES) // TN_PROJ
MISC_OFF = U_MISC * LANES - MISC_TILE * TN_PROJ
MISC_F, MISC_W, MISC_K = 0, 8, 64


def _pack_moves():
    sizes = (1024, 256, 1024, 64, 16, 1024, 1024, 1024, 8, 4096)
    src = [0]
    for n in sizes:
        src.append(src[-1] + n)
    q_a, c_kv, q_idx, k_idx, w_idx, q_b, k_b, v_b, f_b, gate = src[:10]
    m = U_MISC * LANES
    return ((gate, 0, 4096), (q_a, U_QA * LANES, 1024), (q_idx, U_QIDX * LANES, 1024),
            (q_b, U_QB * LANES, 1024), (k_b, U_KB * LANES, 1024), (v_b, U_VB * LANES, 1024),
            (c_kv, U_CKV * LANES, 256), (f_b, m + MISC_F, 8), (w_idx, m + MISC_W, 16),
            (k_idx, m + MISC_K, 64))


def _pack_kernel(w_ref, o_ref):
    cols = o_ref.shape[1]
    m = U_MISC * LANES
    o_ref[m:m + 2 * LANES, :] = jnp.zeros((2 * LANES, cols), o_ref.dtype)
    for src, dst, n in _pack_moves():
        o_ref[dst:dst + n, :] = w_ref[src:src + n, :].astype(o_ref.dtype)


def _pack_w_in(w_in_t):
    n_in, d = w_in_t.shape
    tc = 256
    return pl.pallas_call(
        _pack_kernel,
        grid=(d // tc,),
        in_specs=[pl.BlockSpec((n_in, tc), lambda i: (0, i))],
        out_specs=pl.BlockSpec((NP_COLS, tc), lambda i: (0, i)),
        out_shape=jax.ShapeDtypeStruct((NP_COLS, d), MXU_DTYPE),
        compiler_params=_cparams(("parallel",)),
        name="pack",
    )(w_in_t)


def _inproj_kernel(x_ref, mod_ref, g_ref, w_ref, o_ref, misc_ref, h_scr):
    j = pl.program_id(1)

    @pl.when(j == 0)
    def _():
        y = _rms(x_ref[...], g_ref[...])
        sh = mod_ref[0, 0:1, :]
        sc = mod_ref[0, 1:2, :]
        h_scr[...] = (y * (1.0 + sc) + sh).astype(h_scr.dtype)

    acc = _dot_nt(h_scr[...], w_ref[...])
    o_ref[...] = acc.astype(o_ref.dtype)

    @pl.when(j == MISC_TILE)
    def _():
        misc_ref[...] = acc[:, MISC_OFF:MISC_OFF + LANES]


def _inproj(x2, mod3, g_mix, w_packed, seq):
    t, d = x2.shape
    tm = min(TM_PROJ, seq)
    per_b = seq // tm
    return pl.pallas_call(
        _inproj_kernel,
        grid=(t // tm, NP_COLS // TN_PROJ),
        in_specs=[pl.BlockSpec((tm, d), lambda i, j: (i, 0)),
                  pl.BlockSpec((1, 6, d), lambda i, j: (i // per_b, 0, 0)),
                  pl.BlockSpec((1, d), lambda i, j: (0, 0)),
                  pl.BlockSpec((TN_PROJ, d), lambda i, j: (j, 0))],
        out_specs=[pl.BlockSpec((tm, TN_PROJ), lambda i, j: (i, j)),
                   pl.BlockSpec((tm, LANES), lambda i, j: (i, 0))],
        out_shape=[jax.ShapeDtypeStruct((t, NP_COLS), MXU_DTYPE),
                   jax.ShapeDtypeStruct((t, LANES), F32)],
        scratch_shapes=[pltpu.VMEM((tm, d), MXU_DTYPE)],
        compiler_params=_cparams(("parallel", "arbitrary")),
        name="inproj",
    )(x2, mod3, g_mix, w_packed)


TK = 256


def _prep_kernel(ckv_ref, misc_ref, gkv_ref, bf_ref, kv_ref, cum_ref):
    s = ckv_ref.shape[1]
    kv_ref[0] = _rms(ckv_ref[0].astype(F32), gkv_ref[...]).astype(kv_ref.dtype)

    r = lax.broadcasted_iota(I32, (LANES, LANES), 0)
    c = lax.broadcasted_iota(I32, (LANES, LANES), 1)
    tri = jnp.where(c <= r, 1.0, 0.0).astype(MXU_DTYPE)
    carry = jnp.zeros((1, LANES), F32)
    for blk in range(s // LANES):
        z = misc_ref[0, blk * LANES:(blk + 1) * LANES, :] + bf_ref[...]
        ls = jnp.minimum(z, 0.0) - jnp.log1p(jnp.exp(-jnp.abs(z)))
        p1 = ls.astype(MXU_DTYPE)
        r1 = ls - p1.astype(F32)
        p2 = r1.astype(MXU_DTYPE)
        p3 = (r1 - p2.astype(F32)).astype(MXU_DTYPE)
        cs = _dot(tri, p1) + _dot(tri, p2) + _dot(tri, p3) + carry
        carry = cs[LANES - 1:LANES, :]
        cum_ref[0, blk * LANES:(blk + 1) * LANES, :] = cs * (HEAD_DIM ** 0.5)


def _prep(proj3, misc3, g_kv, bf_row):
    b, s, _ = proj3.shape
    return pl.pallas_call(
        _prep_kernel,
        grid=(b,),
        in_specs=[pl.BlockSpec((1, s, D_LAT), lambda i: (i, 0, U_CKV * LANES // D_LAT)),
                  pl.BlockSpec((1, s, LANES), lambda i: (i, 0, 0)),
                  pl.BlockSpec((1, D_LAT), lambda i: (0, 0)),
                  pl.BlockSpec((1, LANES), lambda i: (0, 0))],
        out_specs=[pl.BlockSpec((1, s, D_LAT), lambda i: (i, 0, 0)),
                   pl.BlockSpec((1, s, LANES), lambda i: (i, 0, 0))],
        out_shape=[jax.ShapeDtypeStruct((b, s, D_LAT), MXU_DTYPE),
                   jax.ShapeDtypeStruct((b, s, LANES), F32)],
        compiler_params=_cparams(("parallel",)),
        name="prep",
    )(proj3, misc3, g_kv, bf_row)


TQ_A = 256
N_BISECT = 32


def _dsa_kernel(qa_ref, qidx_ref, miscq_ref, misck_ref, kv_ref, wuk_ref, wuv_ref, gqa_ref, o_ref,
                sc_scr, bias_scr, logit_scr, qlat_scr, mpart_scr, lpart_scr, oacc_scr, *, topk, n_heads):
    i = pl.program_id(1)
    tq = TQ_A
    nk = i + 1
    q0 = i * tq

    wt = miscq_ref[0].T
    wq = wt[MISC_W:MISC_W + N_IDX_HEADS, :] * (D_IDX ** -0.5 * N_IDX_HEADS ** -0.5)

    def fold8(v, op):
        return op(v.reshape(v.shape[0] // 8, 8, tq), axis=0)

    def score_body(kt, carry):
        mx8, mn8 = carry
        for half in range(TK // LANES):
            k0 = pl.multiple_of(kt * TK + half * LANES, LANES)
            kx = misck_ref[0, pl.ds(k0, LANES), :][:, MISC_K:MISC_K + D_IDX].astype(MXU_DTYPE)
            acc = jnp.zeros((LANES, tq), F32)
            for h in range(N_IDX_HEADS):
                qh = qidx_ref[0, :, h * D_IDX:(h + 1) * D_IDX]
                d = _dot_nt(kx, qh)
                acc = acc + jnp.maximum(d, 0.0) * wq[h:h + 1, :]
            kpos = k0 + lax.broadcasted_iota(I32, (LANES, tq), 0)
            qpos = q0 + lax.broadcasted_iota(I32, (LANES, tq), 1)
            adm = (kpos // CHUNK) <= (qpos // CHUNK)
            sc_scr[pl.ds(k0, LANES), :] = jnp.where(adm, acc, NEG_BIG)
            mx8 = jnp.maximum(mx8, fold8(jnp.where(adm, acc, NEG_BIG), jnp.max))
            mn8 = jnp.minimum(mn8, fold8(jnp.where(adm, acc, -NEG_BIG), jnp.min))
        return mx8, mn8

    mx8, mn8 = lax.fori_loop(0, nk, score_body,
                             (jnp.full((8, tq), NEG_BIG, F32), jnp.full((8, tq), -NEG_BIG, F32)))

    def bis_body(it, carry):
        lo, hi = carry
        mid = lo + 0.5 * (hi - lo)

        def cnt_body(kt, c8):
            k0 = pl.multiple_of(kt * TK, TK)
            return c8 + fold8(jnp.where(sc_scr[pl.ds(k0, TK), :] >= mid, 1.0, 0.0), jnp.sum)

        cnt = jnp.sum(lax.fori_loop(0, nk, cnt_body, jnp.zeros((8, tq), F32)), axis=0, keepdims=True)
        ok = cnt >= topk
        return jnp.where(ok, mid, lo), jnp.where(ok, hi, mid)

    thr, _ = lax.fori_loop(0, N_BISECT, bis_body, (jnp.min(mn8, axis=0, keepdims=True),
                                                   jnp.max(mx8, axis=0, keepdims=True)))

    def bias_body(kt, carry):
        k0 = pl.multiple_of(kt * TK, TK)
        sel_t = sc_scr[pl.ds(k0, TK), :] >= thr
        bias_scr[kt] = jnp.where(sel_t, 0.0, NEG_BIG).T
        return carry

    lax.fori_loop(0, nk, bias_body, 0)

    for h in range(n_heads):
        ql = _dot(qa_ref[0, :, h * HEAD_DIM:(h + 1) * HEAD_DIM], wuk_ref[h])
        ql = _rms(ql, gqa_ref[...]) * (D_LAT ** -0.5)
        qlat_scr[h * tq:(h + 1) * tq, :] = ql.astype(qlat_scr.dtype)
    mpart_scr[...] = jnp.full(mpart_scr.shape, NEG_BIG, F32)
    lpart_scr[...] = jnp.zeros(lpart_scr.shape, F32)
    oacc_scr[...] = jnp.zeros(oacc_scr.shape, F32)

    def logit_body(kt, carry):
        k0 = pl.multiple_of(kt * TK, TK)
        kvt = kv_ref[0, pl.ds(k0, TK), :]
        qpos = q0 + lax.broadcasted_iota(I32, (tq, TK), 0)
        kpos = k0 + lax.broadcasted_iota(I32, (tq, TK), 1)
        dist = jnp.abs(qpos - kpos).astype(F32)
        bias = bias_scr[kt]
        for h in range(n_heads):
            rows = slice(h * tq, (h + 1) * tq)
            slope = 2.0 ** (-8.0 * (h + 1) / n_heads)
            lg = _dot_nt(qlat_scr[rows, :], kvt) + (bias - slope * dist)
            logit_scr[kt, rows, :] = lg
            mpart_scr[rows, :] = jnp.maximum(mpart_scr[rows, :],
                                             jnp.maximum(lg[:, :LANES], lg[:, LANES:]))
        return carry

    lax.fori_loop(0, nk, logit_body, 0)

    m = jnp.max(mpart_scr[...], axis=-1, keepdims=True)
    mpart_scr[...] = jnp.broadcast_to(m, mpart_scr.shape)

    def pv_body(kt, carry):
        k0 = pl.multiple_of(kt * TK, TK)
        kvt = kv_ref[0, pl.ds(k0, TK), :]
        for h in range(n_heads):
            rows = slice(h * tq, (h + 1) * tq)
            mb = mpart_scr[rows, :]
            lg = logit_scr[kt, rows, :]
            p = jnp.exp(lg - jnp.concatenate([mb, mb], axis=1))
            lpart_scr[rows, :] += p[:, :LANES] + p[:, LANES:]
            oacc_scr[rows, :] += _dot(p.astype(MXU_DTYPE), kvt)
        return carry

    lax.fori_loop(0, nk, pv_body, 0)

    for h in range(n_heads):
        rows = slice(h * tq, (h + 1) * tq)
        lsum = jnp.sum(lpart_scr[rows, :], axis=-1, keepdims=True)
        o_lat = oacc_scr[rows, :] / lsum
        o_ref[0, :, h * HEAD_DIM:(h + 1) * HEAD_DIM] = _dot(
            o_lat.astype(MXU_DTYPE), wuv_ref[h]).astype(o_ref.dtype)


def _dsa(proj3, misc3, kv, w_uk, w_uv, g_qa):
    b, s, _ = proj3.shape
    n_heads = w_uk.shape[0]
    width = n_heads * HEAD_DIM
    topk = min(TOPK_MAX, s // 4)
    tq = TQ_A
    nkt = s // TK
    kern = functools.partial(_dsa_kernel, topk=topk, n_heads=n_heads)
    return pl.pallas_call(
        kern,
        grid=(b, s // tq),
        in_specs=[pl.BlockSpec((1, tq, width), lambda bb, i: (bb, i, U_QA * LANES // width)),
                  pl.BlockSpec((1, tq, N_IDX_HEADS * D_IDX),
                               lambda bb, i: (bb, i, U_QIDX * LANES // (N_IDX_HEADS * D_IDX))),
                  pl.BlockSpec((1, tq, LANES), lambda bb, i: (bb, i, 0)),
                  pl.BlockSpec((1, s, LANES), lambda bb, i: (bb, 0, 0)),
                  pl.BlockSpec((1, s, D_LAT), lambda bb, i: (bb, 0, 0)),
                  pl.BlockSpec((n_heads, HEAD_DIM, D_LAT), lambda bb, i: (0, 0, 0)),
                  pl.BlockSpec((n_heads, D_LAT, HEAD_DIM), lambda bb, i: (0, 0, 0)),
                  pl.BlockSpec((1, D_LAT), lambda bb, i: (0, 0))],
        out_specs=pl.BlockSpec((1, tq, width), lambda bb, i: (bb, i, 0)),
        out_shape=jax.ShapeDtypeStruct((b, s, width), MXU_DTYPE),
        scratch_shapes=[pltpu.VMEM((s, tq), F32),
                        pltpu.VMEM((nkt, tq, TK), F32),
                        pltpu.VMEM((nkt, n_heads * tq, TK), F32),
                        pltpu.VMEM((n_heads * tq, D_LAT), MXU_DTYPE),
                        pltpu.VMEM((n_heads * tq, LANES), F32),
                        pltpu.VMEM((n_heads * tq, LANES), F32),
                        pltpu.VMEM((n_heads * tq, D_LAT), F32)],
        compiler_params=_cparams(("parallel", "arbitrary")),
        name="dsa",
    )(proj3, proj3, misc3, misc3, kv, w_uk, w_uv, g_qa)


TQ_B = 256


HG_B = 4
LOG2E = 1.4426950408889634
assert TQ_B == TK


def _split3(x):
    p1 = x.astype(MXU_DTYPE)
    r1 = x - p1.astype(F32)
    p2 = r1.astype(MXU_DTYPE)
    p3 = (r1 - p2.astype(F32)).astype(MXU_DTYPE)
    return p1, p2, p3


def _fox_kernel(q_ref, k_ref, v_ref, cumq_ref, cumk_ref, gq_ref, gk_ref, o_ref,
                kaug_scr, qaug_scr, logit_scr, mpart_scr, lpart_scr, oacc_scr):
    g = pl.program_id(1)
    qi = pl.program_id(2)
    tq = TQ_B
    s = k_ref.shape[1]
    cscale = (HEAD_DIM ** -0.5) * LOG2E

    rr = lax.broadcasted_iota(I32, (3 * LANES, LANES), 0)
    cc = lax.broadcasted_iota(I32, (3 * LANES, LANES), 1)
    lane_row = lax.broadcasted_iota(I32, (1, LANES), 1)
    ones_q = jnp.where((lane_row >= 3) & (lane_row < 6), 1.0, 0.0)
    ones_k = jnp.where(lane_row < 3, 1.0, 0.0)
    base_q = jnp.where(cc < 3, rr - LANES * cc, -1)
    base_k = jnp.where((cc >= 3) & (cc < 6), rr - LANES * (cc - 3), -1)

    def aug(pieces, h, base, sign, ones_row):
        e = jnp.where(base == h, sign, 0.0).astype(MXU_DTYPE)
        return (_dot(pieces, e) + ones_row).astype(MXU_DTYPE)

    @pl.when(qi == 0)
    def _():
        def kbody(c, carry):
            r0 = pl.multiple_of(c * TK, TK)
            ck = jnp.concatenate(_split3(cumk_ref[0, pl.ds(r0, TK), :]), axis=1)
            for hh in range(HG_B):
                kn = _rms(k_ref[0, pl.ds(r0, TK), hh * HEAD_DIM:(hh + 1) * HEAD_DIM].astype(F32), gk_ref[...])
                kaug_scr[hh, pl.ds(r0, TK), 0:HEAD_DIM] = kn.astype(MXU_DTYPE)
                kaug_scr[hh, pl.ds(r0, TK), HEAD_DIM:] = aug(ck, g * HG_B + hh, base_k, -1.0, ones_k)
            return carry

        lax.fori_loop(0, s // TK, kbody, 0)

    cq = jnp.concatenate(_split3(cumq_ref[0]), axis=1)
    for hh in range(HG_B):
        qn = _rms(q_ref[0, :, hh * HEAD_DIM:(hh + 1) * HEAD_DIM].astype(F32), gq_ref[...])
        qaug_scr[hh, :, 0:HEAD_DIM] = qn.astype(MXU_DTYPE)
        qaug_scr[hh, :, HEAD_DIM:] = aug(cq, g * HG_B + hh, base_q, 1.0, ones_q)
    mpart_scr[...] = jnp.full(mpart_scr.shape, NEG_BIG, F32)
    lpart_scr[...] = jnp.zeros(lpart_scr.shape, F32)
    oacc_scr[...] = jnp.zeros(oacc_scr.shape, F32)

    def put_logits(kt, hh, lg):
        logit_scr[kt, hh] = lg
        mpart_scr[hh] = jnp.maximum(mpart_scr[hh], jnp.maximum(lg[:, :LANES], lg[:, LANES:]))

    def raw_logits(kt, hh):
        k0 = pl.multiple_of(kt * TK, TK)
        return _dot_nt(qaug_scr[hh], kaug_scr[hh, pl.ds(k0, TK), :]) * cscale

    def off_body(kt, carry):
        for hh in range(HG_B):
            put_logits(kt, hh, raw_logits(kt, hh))
        return carry

    lax.fori_loop(0, qi, off_body, 0)
    causal = (lax.broadcasted_iota(I32, (tq, TK), 1) <= lax.broadcasted_iota(I32, (tq, TK), 0))
    for hh in range(HG_B):
        put_logits(qi, hh, jnp.where(causal, raw_logits(qi, hh), NEG_BIG))

    for hh in range(HG_B):
        m = jnp.max(mpart_scr[hh], axis=-1, keepdims=True)
        mpart_scr[hh] = jnp.broadcast_to(m, (tq, LANES))

    def pv_body(kt, carry):
        k0 = pl.multiple_of(kt * TK, TK)
        for hh in range(HG_B):
            mb = mpart_scr[hh]
            p = jnp.exp2(logit_scr[kt, hh] - jnp.concatenate([mb, mb], axis=1))
            lpart_scr[hh] += p[:, :LANES] + p[:, LANES:]
            oacc_scr[hh] += _dot(p.astype(MXU_DTYPE),
                                 v_ref[0, pl.ds(k0, TK), hh * HEAD_DIM:(hh + 1) * HEAD_DIM])
        return carry

    lax.fori_loop(0, qi + 1, pv_body, 0)
    for hh in range(HG_B):
        lsum = jnp.sum(lpart_scr[hh], axis=-1, keepdims=True)
        o_ref[0, :, hh * HEAD_DIM:(hh + 1) * HEAD_DIM] = (oacc_scr[hh] / lsum).astype(o_ref.dtype)


def _fox(proj3, cums, g_qb, g_kb, n_heads):
    b, s, _ = proj3.shape
    tq = TQ_B
    gw = HG_B * HEAD_DIM
    return pl.pallas_call(
        _fox_kernel,
        grid=(b, n_heads // HG_B, s // tq),
        in_specs=[pl.BlockSpec((1, tq, gw), lambda bb, g, i: (bb, i, U_QB // HG_B + g)),
                  pl.BlockSpec((1, s, gw), lambda bb, g, i: (bb, 0, U_KB // HG_B + g)),
                  pl.BlockSpec((1, s, gw), lambda bb, g, i: (bb, 0, U_VB // HG_B + g)),
                  pl.BlockSpec((1, tq, LANES), lambda bb, g, i: (bb, i, 0)),
                  pl.BlockSpec((1, s, LANES), lambda bb, g, i: (bb, 0, 0)),
                  pl.BlockSpec((1, HEAD_DIM), lambda bb, g, i: (0, 0)),
                  pl.BlockSpec((1, HEAD_DIM), lambda bb, g, i: (0, 0))],
        out_specs=pl.BlockSpec((1, tq, gw), lambda bb, g, i: (bb, i, g)),
        out_shape=jax.ShapeDtypeStruct((b, s, n_heads * HEAD_DIM), MXU_DTYPE),
        scratch_shapes=[pltpu.VMEM((HG_B, s, 2 * HEAD_DIM), MXU_DTYPE),
                        pltpu.VMEM((HG_B, tq, 2 * HEAD_DIM), MXU_DTYPE),
                        pltpu.VMEM((s // TK, HG_B, tq, TK), F32),
                        pltpu.VMEM((HG_B, tq, LANES), F32),
                        pltpu.VMEM((HG_B, tq, LANES), F32),
                        pltpu.VMEM((HG_B, tq, HEAD_DIM), F32)],
        compiler_params=_cparams(("parallel", "parallel", "arbitrary")),
        name="fox",
    )(proj3, proj3, proj3, cums, cums, g_qb, g_kb)


TM_MERGE = 256


def _merge_kernel(ga_ref, gb_ref, oa_ref, ob_ref, x_ref, mod_ref, bga_ref, bgb_ref, wpa_ref, wpb_ref,
                  wo_ref, gffn_ref, wrh_ref, wrl_ref, br_ref, x1_ref, h2_ref, rl_ref):
    ga = jax.nn.sigmoid(ga_ref[...].astype(F32) + bga_ref[...])
    gb = jax.nn.sigmoid(gb_ref[...].astype(F32) + bgb_ref[...])
    merged = ga * _dot(oa_ref[...], wpa_ref[...]) + gb * _dot(ob_ref[...], wpb_ref[...])
    upd = _dot(merged.astype(MXU_DTYPE), wo_ref[...])
    x1 = x_ref[...] + mod_ref[0, 2:3, :] * upd
    x1_ref[...] = x1
    h2 = _rms(x1, gffn_ref[...]) * (1.0 + mod_ref[0, 4:5, :]) + mod_ref[0, 3:4, :]
    h2_ref[...] = h2
    hh = h2.astype(MXU_DTYPE)
    hl = (h2 - hh.astype(F32)).astype(MXU_DTYPE)
    rl_ref[...] = (_dot(hh, wrh_ref[...]) + _dot(hl, wrh_ref[...]) + _dot(hh, wrl_ref[...])
                   + br_ref[...])


def _merge(proj, o_a, o_b, x2, mod3, b_gate, w_pa, w_pb, w_o, g_ffn, wr_hi, wr_lo, b_r, seq):
    t, d = x2.shape
    tm = TM_MERGE
    per_b = seq // tm
    wa = o_a.shape[1]
    res = lambda shape: pl.BlockSpec(shape, lambda i: (0,) * len(shape), pipeline_mode=pl.Buffered(1))
    return pl.pallas_call(
        _merge_kernel,
        grid=(t // tm,),
        in_specs=[pl.BlockSpec((tm, d), lambda i: (i, 0)),
                  pl.BlockSpec((tm, d), lambda i: (i, 1)),
                  pl.BlockSpec((tm, wa), lambda i: (i, 0)),
                  pl.BlockSpec((tm, wa), lambda i: (i, 0)),
                  pl.BlockSpec((tm, d), lambda i: (i, 0)),
                  pl.BlockSpec((1, 6, d), lambda i: (i // per_b, 0, 0)),
                  pl.BlockSpec((1, d), lambda i: (0, 0)),
                  pl.BlockSpec((1, d), lambda i: (0, 1)),
                  res((wa, d)), res((wa, d)), res((d, d)),
                  pl.BlockSpec((1, d), lambda i: (0, 0)),
                  res((d, LANES)), res((d, LANES)),
                  pl.BlockSpec((1, LANES), lambda i: (0, 0))],
        out_specs=[pl.BlockSpec((tm, d), lambda i: (i, 0)),
                   pl.BlockSpec((tm, d), lambda i: (i, 0)),
                   pl.BlockSpec((tm, LANES), lambda i: (i, 0))],
        out_shape=[jax.ShapeDtypeStruct((t, d), F32),
                   jax.ShapeDtypeStruct((t, d), F32),
                   jax.ShapeDtypeStruct((t, LANES), F32)],
        compiler_params=_cparams(("parallel",)),
        name="merge",
    )(proj, proj, o_a, o_b, x2, mod3, b_gate, b_gate, w_pa, w_pb, w_o, g_ffn, wr_hi, wr_lo, b_r)


TM_ROUTE = 1024
TM_ROWS = 128
R_E0, R_E1, R_P0, R_P1 = 0, 1, 4, 5
IT_TILE, IT_E, IT_LO, IT_HI, IT_FLAG, IT_NEXT = range(6)
F_VALID, F_FIRST_OF_EXPERT, F_FIRST_OF_TILE = 1, 2, 4


def _route_kernel(rl_ref, route_ref, post_ref, items_ref, cnt_scr, run_scr, offs_scr, *, n_tiles):
    tm = rl_ref.shape[0]
    sweep = pl.program_id(0)
    step = pl.program_id(1)
    iw = items_ref.shape[1]

    @pl.when((sweep == 0) & (step == 0))
    def _():
        cnt_scr[...] = jnp.zeros(cnt_scr.shape, F32)

    r = rl_ref[...]
    lane = lax.broadcasted_iota(I32, (tm, LANES), 1).astype(F32)
    neg_inf = -jnp.inf
    gmask = lane < N_GROUPS
    gl = jnp.where(gmask, r, neg_inf)
    gmax = jnp.max(gl, axis=-1, keepdims=True)
    gidx = jnp.min(jnp.where(gl == gmax, lane, float(LANES)), axis=-1, keepdims=True)
    gsum = jnp.sum(jnp.where(gmask, jnp.exp(r - gmax), 0.0), axis=-1, keepdims=True)
    gw = 1.0 / gsum
    lo = N_GROUPS + EXPERTS_PER_GROUP * gidx
    emask = (lane >= lo) & (lane < lo + EXPERTS_PER_GROUP)
    el = jnp.where(emask, r, neg_inf)
    v0 = jnp.max(el, axis=-1, keepdims=True)
    i0 = jnp.min(jnp.where(el == v0, lane, float(LANES)), axis=-1, keepdims=True)
    el2 = jnp.where(lane == i0, neg_inf, el)
    v1 = jnp.max(el2, axis=-1, keepdims=True)
    i1 = jnp.min(jnp.where(el2 == v1, lane, float(LANES)), axis=-1, keepdims=True)
    tt = jnp.exp(v1 - v0)
    p0 = gw / (1.0 + tt)
    p1 = gw * tt / (1.0 + tt)
    e0 = i0 - N_GROUPS
    e1 = i1 - N_GROUPS

    hit0 = lane == e0
    hit1 = lane == e1
    oh = jnp.where(hit0 | hit1, 1.0, 0.0)

    @pl.when(sweep == 0)
    def _():
        ones = jnp.ones((tm, LANES), MXU_DTYPE)
        cnt_scr[...] += _dot(oh.T.astype(MXU_DTYPE), ones)

    @pl.when((sweep == 1) & (step == 0))
    def _():
        _plan_items(cnt_scr[...], items_ref, offs_scr, n_tiles, iw)
        run_scr[...] = jnp.zeros(run_scr.shape, F32)

    @pl.when(sweep == 1)
    def _():
        rr = lax.broadcasted_iota(I32, (tm, tm), 0)
        cc = lax.broadcasted_iota(I32, (tm, tm), 1)
        ltri = jnp.where(cc < rr, 1.0, 0.0).astype(MXU_DTYPE)
        before = _dot(ltri, oh.astype(MXU_DTYPE)) + run_scr[...] + offs_scr[...]
        pos0 = jnp.sum(jnp.where(hit0, before, 0.0), axis=-1, keepdims=True)
        pos1 = jnp.sum(jnp.where(hit1, before, 0.0), axis=-1, keepdims=True)
        run_scr[...] = run_scr[...] + jnp.sum(oh, axis=0, keepdims=True)
        out = jnp.zeros((tm, LANES), F32)
        for k, val in ((R_E0, e0), (R_E1, e1), (R_P0, p0), (R_P1, p1)):
            out = jnp.where(lane == k, val, out)
        route_ref[...] = out
        pmat = jnp.where(lane == 0.0, pos0, jnp.where(lane == 1.0, pos1, 0.0))
        post_ref[...] = pmat.T[0:8, :].astype(I32)


def _plan_items(cnt_col, items_ref, offs_scr, n_tiles, iw):
    tmr = float(TM_ROWS)
    sub = lax.broadcasted_iota(I32, (LANES, LANES), 0)
    lan = lax.broadcasted_iota(I32, (LANES, LANES), 1)
    lstrict = jnp.where(lan < sub, 1.0, 0.0).astype(MXU_DTYPE)
    hi = jnp.floor(cnt_col * (1.0 / LANES))
    lo = cnt_col - hi * LANES
    offs_col = _dot(lstrict, hi.astype(MXU_DTYPE)) * LANES + _dot(lstrict, lo.astype(MXU_DTYPE))
    first_t = jnp.floor(offs_col * (1.0 / tmr))
    last_t = jnp.floor((offs_col + cnt_col - 1.0) * (1.0 / tmr))
    n_col = jnp.where(cnt_col > 0.0, last_t - first_t + 1.0, 0.0)
    base_col = _dot(lstrict, n_col.astype(MXU_DTYPE))
    end_col = base_col + n_col
    offs_scr[...] = offs_col.T[0:1, :]

    rep = lambda col: jnp.concatenate([col] * (iw // LANES), axis=1)
    wl = lax.broadcasted_iota(I32, (LANES, iw), 1).astype(F32)
    sub_f = lax.broadcasted_iota(I32, (LANES, iw), 0).astype(F32)
    end_r = rep(end_col)
    w_total = end_r[LANES - 1:LANES, :]
    count_le = lambda v: jnp.sum(jnp.where(end_r <= v, 1.0, 0.0), axis=0, keepdims=True)
    w_row = wl[0:1, :]
    eidx = count_le(w_row)
    sel = sub_f == eidx
    pick = lambda col: jnp.sum(jnp.where(sel, rep(col), 0.0), axis=0, keepdims=True)
    e_base, e_first, e_offs, e_cnt, e_end = (pick(base_col), pick(first_t), pick(offs_col),
                                             pick(cnt_col), pick(end_col))
    valid = w_row < w_total
    tile = e_first + (w_row - e_base)
    row_lo = jnp.maximum(e_offs, tile * tmr) - tile * tmr
    row_hi = jnp.minimum(e_offs + e_cnt, (tile + 1.0) * tmr) - tile * tmr
    first_e = w_row == e_base
    flags = jnp.where(valid, F_VALID + jnp.where(first_e, float(F_FIRST_OF_EXPERT), 0.0)
                      + jnp.where(row_lo == 0.0, float(F_FIRST_OF_TILE), 0.0), 0.0)
    nxt = jnp.where(valid & first_e & (e_end < w_total), count_le(e_end), -1.0)
    e_last = count_le(w_total - 1.0)
    rows = {IT_TILE: jnp.where(valid, tile, n_tiles - 1.0),
            IT_E: jnp.where(valid, eidx, e_last),
            IT_LO: jnp.where(valid, row_lo, 0.0),
            IT_HI: jnp.where(valid, row_hi, 0.0),
            IT_FLAG: flags,
            IT_NEXT: nxt}
    sub8 = lax.broadcasted_iota(I32, (8, iw), 0)
    out = jnp.zeros((8, iw), F32)
    for k, val in rows.items():
        out = jnp.where(sub8 == k, val, out)
    items_ref[...] = out.astype(I32)


def _route(rlog, n_items):
    t = rlog.shape[0]
    tm = min(TM_ROUTE, t)
    iw = ((n_items + LANES - 1) // LANES) * LANES
    kern = functools.partial(_route_kernel, n_tiles=2 * t // TM_ROWS)
    return pl.pallas_call(
        kern,
        grid=(2, t // tm),
        in_specs=[pl.BlockSpec((tm, LANES), lambda p, i: (i, 0))],
        out_specs=[pl.BlockSpec((tm, LANES), lambda p, i: (i * p, 0)),
                   pl.BlockSpec((8, tm), lambda p, i: (0, i * p)),
                   pl.BlockSpec((8, iw), lambda p, i: (0, 0))],
        out_shape=[jax.ShapeDtypeStruct((t, LANES), F32),
                   jax.ShapeDtypeStruct((8, t), I32),
                   jax.ShapeDtypeStruct((8, iw), I32)],
        scratch_shapes=[pltpu.VMEM((LANES, LANES), F32),
                        pltpu.VMEM((1, LANES), F32),
                        pltpu.VMEM((1, LANES), F32)],
        compiler_params=_cparams(("arbitrary", "arbitrary")),
        name="route",
    )(rlog)


TM_DISP = 256


def _dispatch_kernel(pos0_ref, pos1_ref, h2_ref, xs_ref, sem):
    base = pl.program_id(0) * TM_DISP

    def row_copy(r, p):
        return pltpu.make_async_copy(h2_ref.at[pl.ds(r, 1), :], xs_ref.at[pl.ds(p, 1), :], sem)

    def issue(r, carry):
        row_copy(r, pos0_ref[base + r]).start(priority=0)
        row_copy(r, pos1_ref[base + r]).start(priority=1)
        return carry

    lax.fori_loop(0, TM_DISP, issue, 0, unroll=8)

    def drain(r, carry):
        row_copy(0, 0).wait()
        row_copy(0, 0).wait()
        return carry

    lax.fori_loop(0, TM_DISP, drain, 0, unroll=8)


def _dispatch(pos0, pos1, h2):
    t, d = h2.shape
    return pl.pallas_call(
        _dispatch_kernel,
        grid_spec=pltpu.PrefetchScalarGridSpec(
            num_scalar_prefetch=2,
            grid=(t // TM_DISP,),
            in_specs=[pl.BlockSpec((TM_DISP, d), lambda i, p0, p1: (i, 0))],
            out_specs=pl.BlockSpec(memory_space=pl.ANY),
            scratch_shapes=[pltpu.SemaphoreType.DMA(())]),
        out_shape=jax.ShapeDtypeStruct((2 * t, d), h2.dtype),
        compiler_params=pltpu.CompilerParams(dimension_semantics=("arbitrary",),
                                             vmem_limit_bytes=VMEM_LIMIT, has_side_effects=True),
        name="dispatch",
    )(pos0, pos1, h2)


def _experts_kernel(tile_ref, e_ref, lo_ref, hi_ref, flag_ref, next_ref, xs_ref, w1_hbm, w3_hbm, w2_hbm,
                    ys_ref, f1_scr, f3_scr, f2_scr, w1_scr, w3_scr, w2_scr, sems):
    del tile_ref
    w = pl.program_id(0)
    flag = flag_ref[w]

    mats = ((w1_hbm, f1_scr, w1_scr), (w3_hbm, f3_scr, w3_scr), (w2_hbm, f2_scr, w2_scr))

    def weight_copy(k, e):
        return pltpu.make_async_copy(mats[k][0].at[e], mats[k][1], sems.at[k])

    @pl.when(w == 0)
    def _():
        for k in range(3):
            weight_copy(k, e_ref[0]).start()

    @pl.when((flag & F_FIRST_OF_EXPERT) != 0)
    def _():
        nxt = next_ref[w]
        for k in range(3):
            weight_copy(k, e_ref[w]).wait()
            mats[k][2][...] = mats[k][1][...].astype(mats[k][2].dtype)

            @pl.when(nxt >= 0)
            def _():
                weight_copy(k, nxt).start()

    @pl.when((flag & F_VALID) != 0)
    def _():
        x = xs_ref[...].astype(MXU_DTYPE)
        a = _dot(x, w1_scr[...])
        u = _dot(x, w3_scr[...])
        hm = (a * jax.nn.sigmoid(a)) * u
        res = _dot(hm.astype(MXU_DTYPE), w2_scr[...])
        row = lax.broadcasted_iota(I32, res.shape, 0)
        mine = (row >= lo_ref[w]) & (row < hi_ref[w])

        @pl.when((flag & F_FIRST_OF_TILE) != 0)
        def _():
            ys_ref[...] = jnp.where(mine, res, 0.0)

        @pl.when((flag & F_FIRST_OF_TILE) == 0)
        def _():
            ys_ref[...] = jnp.where(mine, res, ys_ref[...])


def _experts(items, xs, w1, w3, w2, n_items):
    n_rows, d = xs.shape
    f = w1.shape[2]
    tm = TM_ROWS
    tile_map = lambda w, tile, *_: (tile[w], 0)
    return pl.pallas_call(
        _experts_kernel,
        grid_spec=pltpu.PrefetchScalarGridSpec(
            num_scalar_prefetch=6,
            grid=(n_items,),
            in_specs=[pl.BlockSpec((tm, d), tile_map),
                      pl.BlockSpec(memory_space=pl.ANY),
                      pl.BlockSpec(memory_space=pl.ANY),
                      pl.BlockSpec(memory_space=pl.ANY)],
            out_specs=pl.BlockSpec((tm, d), tile_map),
            scratch_shapes=[pltpu.VMEM((d, f), F32),
                            pltpu.VMEM((d, f), F32),
                            pltpu.VMEM((f, d), F32),
                            pltpu.VMEM((d, f), MXU_DTYPE),
                            pltpu.VMEM((d, f), MXU_DTYPE),
                            pltpu.VMEM((f, d), MXU_DTYPE),
                            pltpu.SemaphoreType.DMA((3,))]),
        out_shape=jax.ShapeDtypeStruct((n_rows, d), F32),
        compiler_params=_cparams(("arbitrary",)),
        name="experts",
    )(items[IT_TILE], items[IT_E], items[IT_LO], items[IT_HI], items[IT_FLAG], items[IT_NEXT],
      xs, w1, w3, w2)


TM_COMB = 256


def _combine_kernel(pos0_ref, pos1_ref, ys_ref, x1_ref, route_ref, mod_ref, o_ref, y0_scr, y1_scr, sem):
    base = pl.program_id(0) * TM_COMB

    def row_copy(p, dst, r):
        return pltpu.make_async_copy(ys_ref.at[pl.ds(p, 1), :], dst.at[pl.ds(r, 1), :], sem)

    def issue(r, carry):
        row_copy(pos0_ref[base + r], y0_scr, r).start(priority=0)
        row_copy(pos1_ref[base + r], y1_scr, r).start(priority=1)
        return carry

    lax.fori_loop(0, TM_COMB, issue, 0, unroll=8)

    def drain(r, carry):
        row_copy(0, y0_scr, 0).wait()
        row_copy(0, y1_scr, 0).wait()
        return carry

    lax.fori_loop(0, TM_COMB, drain, 0, unroll=8)

    rt = route_ref[...]
    lane = lax.broadcasted_iota(I32, rt.shape, 1)
    p0 = jnp.sum(jnp.where(lane == R_P0, rt, 0.0), axis=-1, keepdims=True)
    p1 = jnp.sum(jnp.where(lane == R_P1, rt, 0.0), axis=-1, keepdims=True)
    y = p0 * y0_scr[...] + p1 * y1_scr[...]
    o_ref[...] = x1_ref[...] + mod_ref[0, 5:6, :] * y


def _combine(pos0, pos1, ys, x1, route, mod3, seq):
    t, d = x1.shape
    tm = TM_COMB
    per_b = seq // tm
    return pl.pallas_call(
        _combine_kernel,
        grid_spec=pltpu.PrefetchScalarGridSpec(
            num_scalar_prefetch=2,
            grid=(t // tm,),
            in_specs=[pl.BlockSpec(memory_space=pl.ANY),
                      pl.BlockSpec((tm, d), lambda i, p0, p1: (i, 0)),
                      pl.BlockSpec((tm, LANES), lambda i, p0, p1: (i, 0)),
                      pl.BlockSpec((1, 6, d), lambda i, p0, p1: (i // per_b, 0, 0))],
            out_specs=pl.BlockSpec((tm, d), lambda i, p0, p1: (i, 0)),
            scratch_shapes=[pltpu.VMEM((tm, d), F32),
                            pltpu.VMEM((tm, d), F32),
                            pltpu.SemaphoreType.DMA(())]),
        out_shape=jax.ShapeDtypeStruct((t, d), F32),
        compiler_params=_cparams(("arbitrary",)),
        name="combine",
    )(pos0, pos1, ys, x1, route, mod3)


def kernel(x, c, w_ada, b_ada, g_mix, w_in, b_gate, b_forget, w_uk, w_uv, g_qa, g_kv, g_qb, g_kb,
           w_pa, w_pb, w_o, g_ffn, w_rg, b_rg, w_re, b_re, w1, w3, w2):
    b, s, d = x.shape
    depth = w_ada.shape[0]
    t = b * s
    n_heads_b = b_forget.shape[1]
    n_items = 2 * t // TM_ROWS + N_EXPERTS
    c8 = jnp.zeros((8, d), F32).at[:b].set(c)
    x2 = x.reshape(t, d)
    row = lambda v: v.reshape(1, -1)
    for l in range(depth):
        mod3 = _ada(c8, w_ada[l], row(b_ada[l]))[:b].reshape(b, 6, d)
        proj, misc = _inproj(x2, mod3, row(g_mix[l]), _pack_w_in(jnp.transpose(w_in[l])), s)
        proj3 = proj.reshape(b, s, NP_COLS)
        misc3 = misc.reshape(b, s, LANES)
        bf_row = jnp.zeros((1, LANES), F32).at[0, MISC_F:MISC_F + n_heads_b].set(b_forget[l])
        kv, cums = _prep(proj3, misc3, row(g_kv[l]), bf_row)
        o_a = _dsa(proj3, misc3, kv, w_uk[l].astype(MXU_DTYPE), w_uv[l].astype(MXU_DTYPE), row(g_qa[l]))
        o_b = _fox(proj3, cums, row(g_qb[l]), row(g_kb[l]), n_heads_b)

        w_r = jnp.zeros((d, LANES), F32).at[:, :N_GROUPS].set(w_rg[l])
        w_r = w_r.at[:, N_GROUPS:N_GROUPS + N_EXPERTS].set(w_re[l])
        wr_hi = w_r.astype(MXU_DTYPE)
        wr_lo = (w_r - wr_hi.astype(F32)).astype(MXU_DTYPE)
        b_r = jnp.zeros((1, LANES), F32).at[0, :N_GROUPS].set(b_rg[l])
        b_r = b_r.at[0, N_GROUPS:N_GROUPS + N_EXPERTS].set(b_re[l])
        x1, h2, rlog = _merge(proj, o_a.reshape(t, -1), o_b.reshape(t, -1), x2, mod3, row(b_gate[l]),
                              w_pa[l].astype(MXU_DTYPE), w_pb[l].astype(MXU_DTYPE),
                              w_o[l].astype(MXU_DTYPE), row(g_ffn[l]), wr_hi, wr_lo, b_r, s)
        route, pos_t, items = _route(rlog, n_items)
        pos0, pos1 = pos_t[0], pos_t[1]
        xs = _dispatch(pos0, pos1, h2)
        ys = _experts(items, xs, w1[l], w3[l], w2[l], n_items)
        x2 = _combine(pos0, pos1, ys, x1, route, mod3, s)
    return x2.reshape(b, s, d)
```

```python
import functools

import jax
import jax.numpy as jnp
from jax import lax
from jax.experimental import pallas as pl
from jax.experimental.pallas import tpu as pltpu

F32 = jnp.float32
I32 = jnp.int32
MXU_DTYPE = jnp.bfloat16

CHUNK = 64
HEAD_DIM = 128
D_LAT = 256
N_IDX_HEADS = 16
D_IDX = 64
TOPK_MAX = 256
N_GROUPS = 8
EXPERTS_PER_GROUP = 8
N_EXPERTS = N_GROUPS * EXPERTS_PER_GROUP
RMS_EPS = 1e-6

LANES = 128
VMEM_LIMIT = 56 * 1024 * 1024

NEG_BIG = -1e30
INT_MIN = -2147483648


def _cparams(sem):
    return pltpu.CompilerParams(dimension_semantics=sem, vmem_limit_bytes=VMEM_LIMIT)


def _dot(a, b):
    return jnp.dot(a, b, preferred_element_type=F32)


def _dot_nt(a, b):
    return lax.dot_general(a, b, (((1,), (1,)), ((), ())), preferred_element_type=F32)


def _rms(x, g):
    return x * lax.rsqrt(jnp.mean(x * x, axis=-1, keepdims=True) + RMS_EPS) * g


def _ada_kernel(c_ref, w_ref, b_ref, o_ref):
    c = c_ref[...]
    a = c * jax.nn.sigmoid(c)
    o_ref[...] = _dot(a.astype(MXU_DTYPE), w_ref[...].astype(MXU_DTYPE)) + b_ref[...]


def _ada(c8, w_ada, b_ada):
    d, n = w_ada.shape
    tn = 1024
    return pl.pallas_call(
        _ada_kernel,
        grid=(n // tn,),
        in_specs=[pl.BlockSpec((8, d), lambda j: (0, 0)),
                  pl.BlockSpec((d, tn), lambda j: (0, j)),
                  pl.BlockSpec((1, tn), lambda j: (0, j))],
        out_specs=pl.BlockSpec((8, tn), lambda j: (0, j)),
        out_shape=jax.ShapeDtypeStruct((8, n), F32),
        compiler_params=_cparams(("arbitrary",)),
        name="ada",
    )(c8, w_ada, b_ada)


NP_COLS = 76 * LANES
U_QA, U_QIDX, U_QB, U_KB, U_VB, U_CKV, U_MISC = 32, 40, 48, 56, 64, 72, 74
TN_PROJ = 4 * LANES
TM_PROJ = 1024
MISC_TILE = (U_MISC * LANES) // TN_PROJ
MISC_OFF = U_MISC * LANES - MISC_TILE * TN_PROJ
MISC_K, MISC_F, MISC_W = 0, 64, 72


def _pack_moves():
    sizes = (1024, 256, 1024, 64, 16, 1024, 1024, 1024, 8, 4096)
    src = [0]
    for n in sizes:
        src.append(src[-1] + n)
    q_a, c_kv, q_idx, k_idx, w_idx, q_b, k_b, v_b, f_b, gate = src[:10]
    m = U_MISC * LANES
    return ((gate, 0, 4096), (q_a, U_QA * LANES, 1024), (q_idx, U_QIDX * LANES, 1024),
            (q_b, U_QB * LANES, 1024), (k_b, U_KB * LANES, 1024), (v_b, U_VB * LANES, 1024),
            (c_kv, U_CKV * LANES, 256), (f_b, m + MISC_F, 8), (w_idx, m + MISC_W, 16),
            (k_idx, m + MISC_K, 64))


def _pack_kernel(w_ref, o_ref):
    cols = o_ref.shape[1]
    m = U_MISC * LANES
    o_ref[m:m + 2 * LANES, :] = jnp.zeros((2 * LANES, cols), o_ref.dtype)
    for src, dst, n in _pack_moves():
        o_ref[dst:dst + n, :] = w_ref[src:src + n, :].astype(o_ref.dtype)


def _pack_w_in(w_in_t):
    n_in, d = w_in_t.shape
    tc = 256
    return pl.pallas_call(
        _pack_kernel,
        grid=(d // tc,),
        in_specs=[pl.BlockSpec((n_in, tc), lambda i: (0, i))],
        out_specs=pl.BlockSpec((NP_COLS, tc), lambda i: (0, i)),
        out_shape=jax.ShapeDtypeStruct((NP_COLS, d), MXU_DTYPE),
        compiler_params=_cparams(("parallel",)),
        name="pack",
    )(w_in_t)


def _inproj_kernel(x_ref, mod_ref, g_ref, w_ref, o_ref, misc_ref, h_scr):
    j = pl.program_id(1)

    @pl.when(j == 0)
    def _():
        y = _rms(x_ref[...], g_ref[...])
        sh = mod_ref[0, 0:1, :]
        sc = mod_ref[0, 1:2, :]
        h_scr[...] = (y * (1.0 + sc) + sh).astype(h_scr.dtype)

    acc = _dot_nt(h_scr[...], w_ref[...])
    o_ref[...] = acc.astype(o_ref.dtype)

    @pl.when(j == MISC_TILE)
    def _():
        misc_ref[...] = acc[:, MISC_OFF:MISC_OFF + LANES]


def _inproj(x2, mod3, g_mix, w_packed, seq):
    t, d = x2.shape
    tm = min(TM_PROJ, seq)
    per_b = seq // tm
    return pl.pallas_call(
        _inproj_kernel,
        grid=(t // tm, NP_COLS // TN_PROJ),
        in_specs=[pl.BlockSpec((tm, d), lambda i, j: (i, 0)),
                  pl.BlockSpec((1, 6, d), lambda i, j: (i // per_b, 0, 0)),
                  pl.BlockSpec((1, d), lambda i, j: (0, 0)),
                  pl.BlockSpec((TN_PROJ, d), lambda i, j: (j, 0))],
        out_specs=[pl.BlockSpec((tm, TN_PROJ), lambda i, j: (i, j)),
                   pl.BlockSpec((tm, LANES), lambda i, j: (i, 0))],
        out_shape=[jax.ShapeDtypeStruct((t, NP_COLS), MXU_DTYPE),
                   jax.ShapeDtypeStruct((t, LANES), F32)],
        scratch_shapes=[pltpu.VMEM((tm, d), MXU_DTYPE)],
        compiler_params=_cparams(("parallel", "arbitrary")),
        name="inproj",
    )(x2, mod3, g_mix, w_packed)


TK = 256


def _prep_kernel(ckv_ref, misc_ref, gkv_ref, bf_ref, kv_ref, cum_ref):
    s = ckv_ref.shape[1]
    kv_ref[0] = _rms(ckv_ref[0].astype(F32), gkv_ref[...]).astype(kv_ref.dtype)

    r = lax.broadcasted_iota(I32, (LANES, LANES), 0)
    c = lax.broadcasted_iota(I32, (LANES, LANES), 1)
    tri = jnp.where(c <= r, 1.0, 0.0).astype(MXU_DTYPE)
    carry = jnp.zeros((1, LANES), F32)
    for blk in range(s // LANES):
        z = misc_ref[0, blk * LANES:(blk + 1) * LANES, :] + bf_ref[...]
        ls = jnp.minimum(z, 0.0) - jnp.log1p(jnp.exp(-jnp.abs(z)))
        p1 = ls.astype(MXU_DTYPE)
        r1 = ls - p1.astype(F32)
        p2 = r1.astype(MXU_DTYPE)
        p3 = (r1 - p2.astype(F32)).astype(MXU_DTYPE)
        cs = _dot(tri, p1) + _dot(tri, p2) + _dot(tri, p3) + carry
        carry = cs[LANES - 1:LANES, :]
        cum_ref[0, blk * LANES:(blk + 1) * LANES, :] = cs * (HEAD_DIM ** 0.5)


def _prep(proj3, misc3, g_kv, bf_row):
    b, s, _ = proj3.shape
    return pl.pallas_call(
        _prep_kernel,
        grid=(b,),
        in_specs=[pl.BlockSpec((1, s, D_LAT), lambda i: (i, 0, U_CKV * LANES // D_LAT)),
                  pl.BlockSpec((1, s, LANES), lambda i: (i, 0, 0)),
                  pl.BlockSpec((1, D_LAT), lambda i: (0, 0)),
                  pl.BlockSpec((1, LANES), lambda i: (0, 0))],
        out_specs=[pl.BlockSpec((1, s, D_LAT), lambda i: (i, 0, 0)),
                   pl.BlockSpec((1, s, LANES), lambda i: (i, 0, 0))],
        out_shape=[jax.ShapeDtypeStruct((b, s, D_LAT), MXU_DTYPE),
                   jax.ShapeDtypeStruct((b, s, LANES), F32)],
        compiler_params=_cparams(("parallel",)),
        name="prep",
    )(proj3, misc3, g_kv, bf_row)


TQ_A = 256
N_BISECT = 32
LOG2E = 1.4426950408889634
assert TQ_A == TK


def _pair_loop(n, body):
    def pair(j, carry):
        body(2 * j, 2)
        return carry

    lax.fori_loop(0, jnp.right_shift(n, 1), pair, 0)

    @pl.when(jnp.bitwise_and(n, 1) == 1)
    def _():
        body(n - 1, 1)


def _dsa_kernel(qa_ref, qidx_ref, miscq_ref, misck_ref, kv_ref, wuk_ref, wuv_ref, gqa_ref, o_ref,
                sc_scr, bias_scr, logit_scr, qlat_scr, qh_scr, mpart_scr, lpart_scr, oacc_scr,
                *, topk, n_heads):
    i = pl.program_id(1)
    tq = TQ_A
    nk = i + 1
    q0 = i * tq

    wt = miscq_ref[0].T
    wq = wt[MISC_W:MISC_W + N_IDX_HEADS, :] * (D_IDX ** -0.5 * N_IDX_HEADS ** -0.5)
    for h in range(N_IDX_HEADS):
        qh_scr[h] = qidx_ref[0, :, h * D_IDX:(h + 1) * D_IDX]

    def fold(v, op, rows):
        return op(v.reshape(v.shape[0] // rows, rows, tq), axis=0)

    def score_body(kt, carry):
        mx8, mn8 = carry
        for half in range(TK // LANES):
            k0 = pl.multiple_of(kt * TK + half * LANES, LANES)
            kx = misck_ref[0, pl.ds(k0, LANES), :][:, MISC_K:MISC_K + D_IDX].astype(MXU_DTYPE)
            acc = jnp.zeros((LANES, tq), F32)
            for h in range(N_IDX_HEADS):
                d = _dot_nt(kx, qh_scr[h])
                acc = acc + jnp.maximum(d, 0.0) * wq[h:h + 1, :]
            kpos = k0 + lax.broadcasted_iota(I32, (LANES, tq), 0)
            qpos = q0 + lax.broadcasted_iota(I32, (LANES, tq), 1)
            adm = (kpos // CHUNK) <= (qpos // CHUNK)
            sc_scr[pl.ds(k0, LANES), :] = jnp.where(adm, acc, NEG_BIG)
            mx8 = jnp.maximum(mx8, fold(jnp.where(adm, acc, NEG_BIG), jnp.max, 8))
            mn8 = jnp.minimum(mn8, fold(jnp.where(adm, acc, -NEG_BIG), jnp.min, 8))
        return mx8, mn8

    mx8, mn8 = lax.fori_loop(0, nk, score_body,
                             (jnp.full((8, tq), NEG_BIG, F32), jnp.full((8, tq), -NEG_BIG, F32)))

    def bis_body(it, carry):
        lo, hi = carry
        mid = lo + 0.5 * (hi - lo)

        def cnt_body(kt, c32):
            k0 = pl.multiple_of(kt * TK, TK)
            return c32 + fold(jnp.where(sc_scr[pl.ds(k0, TK), :] >= mid, 1.0, 0.0), jnp.sum, 32)

        cnt = jnp.sum(lax.fori_loop(0, nk, cnt_body, jnp.zeros((32, tq), F32)), axis=0, keepdims=True)
        ok = cnt >= topk
        return jnp.where(ok, mid, lo), jnp.where(ok, hi, mid)

    thr, _ = lax.fori_loop(0, N_BISECT, bis_body, (jnp.min(mn8, axis=0, keepdims=True),
                                                   jnp.max(mx8, axis=0, keepdims=True)))

    def bias_body(kt, carry):
        k0 = pl.multiple_of(kt * TK, TK)
        sel_t = sc_scr[pl.ds(k0, TK), :] >= thr
        bias_scr[kt] = jnp.where(sel_t, 0.0, NEG_BIG).T
        return carry

    lax.fori_loop(0, nk, bias_body, 0)

    for h in range(n_heads):
        ql = _dot(qa_ref[0, :, h * HEAD_DIM:(h + 1) * HEAD_DIM], wuk_ref[h])
        ql = _rms(ql, gqa_ref[...]) * (D_LAT ** -0.5 * LOG2E)
        qlat_scr[h * tq:(h + 1) * tq, :] = ql.astype(qlat_scr.dtype)
    mpart_scr[...] = jnp.full(mpart_scr.shape, NEG_BIG, F32)
    lpart_scr[...] = jnp.zeros(lpart_scr.shape, F32)
    oacc_scr[...] = jnp.zeros(oacc_scr.shape, F32)
    slopes2 = [2.0 ** (-8.0 * (h + 1) / n_heads) * LOG2E for h in range(n_heads)]

    def logit_tiles(kt, nt, last):
        k0 = pl.multiple_of(kt * TK, TK)
        kvt = kv_ref[0, pl.ds(k0, nt * TK), :]
        kcol = (k0 + lax.broadcasted_iota(I32, (1, nt * TK), 1)).astype(F32)
        if last:
            ahead = jnp.maximum(lax.broadcasted_iota(I32, (tq, TK), 1)
                                - lax.broadcasted_iota(I32, (tq, TK), 0), 0).astype(F32)
        for h in range(n_heads):
            rows = slice(h * tq, (h + 1) * tq)
            lg = _dot_nt(qlat_scr[rows, :], kvt) + slopes2[h] * kcol
            for t in range(nt):
                lt = lg[:, t * TK:(t + 1) * TK] + bias_scr[kt + t]
                if last:
                    lt = lt - (2.0 * slopes2[h]) * ahead
                logit_scr[kt + t, rows, :] = lt
                mpart_scr[rows, :] = jnp.maximum(mpart_scr[rows, :],
                                                 jnp.maximum(lt[:, :LANES], lt[:, LANES:]))

    _pair_loop(i, lambda kt, nt: logit_tiles(kt, nt, False))
    logit_tiles(i, 1, True)

    m = jnp.max(mpart_scr[...], axis=-1, keepdims=True)
    mpart_scr[...] = jnp.broadcast_to(m, mpart_scr.shape)

    def pv_tiles(kt, nt):
        k0 = pl.multiple_of(kt * TK, TK)
        kvt = kv_ref[0, pl.ds(k0, nt * TK), :]
        for h in range(n_heads):
            rows = slice(h * tq, (h + 1) * tq)
            mb = mpart_scr[rows, :]
            mb2 = jnp.concatenate([mb, mb], axis=1)
            ps = [jnp.exp2(logit_scr[kt + t, rows, :] - mb2) for t in range(nt)]
            lsum = ps[0][:, :LANES] + ps[0][:, LANES:]
            for p in ps[1:]:
                lsum = lsum + p[:, :LANES] + p[:, LANES:]
            lpart_scr[rows, :] += lsum
            p_all = ps[0] if nt == 1 else jnp.concatenate(ps, axis=1)
            oacc_scr[rows, :] += _dot(p_all.astype(MXU_DTYPE), kvt)

    _pair_loop(nk, pv_tiles)

    for h in range(n_heads):
        rows = slice(h * tq, (h + 1) * tq)
        lsum = jnp.sum(lpart_scr[rows, :], axis=-1, keepdims=True)
        o_lat = oacc_scr[rows, :] / lsum
        o_ref[0, :, h * HEAD_DIM:(h + 1) * HEAD_DIM] = _dot(
            o_lat.astype(MXU_DTYPE), wuv_ref[h]).astype(o_ref.dtype)


def _dsa(proj3, misc3, kv, w_uk, w_uv, g_qa):
    b, s, _ = proj3.shape
    n_heads = w_uk.shape[0]
    width = n_heads * HEAD_DIM
    topk = min(TOPK_MAX, s // 4)
    tq = TQ_A
    nkt = s // TK
    kern = functools.partial(_dsa_kernel, topk=topk, n_heads=n_heads)
    return pl.pallas_call(
        kern,
        grid=(b, s // tq),
        in_specs=[pl.BlockSpec((1, tq, width), lambda bb, i: (bb, i, U_QA * LANES // width)),
                  pl.BlockSpec((1, tq, N_IDX_HEADS * D_IDX),
                               lambda bb, i: (bb, i, U_QIDX * LANES // (N_IDX_HEADS * D_IDX))),
                  pl.BlockSpec((1, tq, LANES), lambda bb, i: (bb, i, 0)),
                  pl.BlockSpec((1, s, LANES), lambda bb, i: (bb, 0, 0)),
                  pl.BlockSpec((1, s, D_LAT), lambda bb, i: (bb, 0, 0)),
                  pl.BlockSpec((n_heads, HEAD_DIM, D_LAT), lambda bb, i: (0, 0, 0)),
                  pl.BlockSpec((n_heads, D_LAT, HEAD_DIM), lambda bb, i: (0, 0, 0)),
                  pl.BlockSpec((1, D_LAT), lambda bb, i: (0, 0))],
        out_specs=pl.BlockSpec((1, tq, width), lambda bb, i: (bb, i, 0)),
        out_shape=jax.ShapeDtypeStruct((b, s, width), MXU_DTYPE),
        scratch_shapes=[pltpu.VMEM((s, tq), F32),
                        pltpu.VMEM((nkt, tq, TK), F32),
                        pltpu.VMEM((nkt, n_heads * tq, TK), F32),
                        pltpu.VMEM((n_heads * tq, D_LAT), MXU_DTYPE),
                        pltpu.VMEM((N_IDX_HEADS, tq, D_IDX), MXU_DTYPE),
                        pltpu.VMEM((n_heads * tq, LANES), F32),
                        pltpu.VMEM((n_heads * tq, LANES), F32),
                        pltpu.VMEM((n_heads * tq, D_LAT), F32)],
        compiler_params=_cparams(("parallel", "arbitrary")),
        name="dsa",
    )(proj3, proj3, misc3, misc3, kv, w_uk, w_uv, g_qa)


TQ_B = 256


HG_B = 4
assert TQ_B == TK


def _split3(x):
    p1 = x.astype(MXU_DTYPE)
    r1 = x - p1.astype(F32)
    p2 = r1.astype(MXU_DTYPE)
    p3 = (r1 - p2.astype(F32)).astype(MXU_DTYPE)
    return p1, p2, p3


def _fox_kernel(q_ref, k_ref, v_ref, cumq_ref, cumk_ref, gq_ref, gk_ref, o_ref,
                kaug_scr, qaug_scr, logit_scr, mpart_scr, lpart_scr, oacc_scr):
    g = pl.program_id(1)
    qi = pl.program_id(2)
    tq = TQ_B
    s = k_ref.shape[1]
    cscale = (HEAD_DIM ** -0.5) * LOG2E

    rr = lax.broadcasted_iota(I32, (3 * LANES, LANES), 0)
    cc = lax.broadcasted_iota(I32, (3 * LANES, LANES), 1)
    lane_row = lax.broadcasted_iota(I32, (1, LANES), 1)
    ones_q = jnp.where((lane_row >= 3) & (lane_row < 6), 1.0, 0.0)
    ones_k = jnp.where(lane_row < 3, 1.0, 0.0)
    base_q = jnp.where(cc < 3, rr - LANES * cc, -1)
    base_k = jnp.where((cc >= 3) & (cc < 6), rr - LANES * (cc - 3), -1)

    def aug(pieces, h, base, sign, ones_row):
        e = jnp.where(base == h, sign, 0.0).astype(MXU_DTYPE)
        return (_dot(pieces, e) + ones_row).astype(MXU_DTYPE)

    @pl.when(qi == 0)
    def _():
        def kbody(c, carry):
            r0 = pl.multiple_of(c * TK, TK)
            ck = jnp.concatenate(_split3(cumk_ref[0, pl.ds(r0, TK), :]), axis=1)
            for hh in range(HG_B):
                kn = _rms(k_ref[0, pl.ds(r0, TK), hh * HEAD_DIM:(hh + 1) * HEAD_DIM].astype(F32), gk_ref[...])
                kaug_scr[hh, pl.ds(r0, TK), 0:HEAD_DIM] = kn.astype(MXU_DTYPE)
                kaug_scr[hh, pl.ds(r0, TK), HEAD_DIM:] = aug(ck, MISC_F + g * HG_B + hh, base_k, -1.0, ones_k)
            return carry

        lax.fori_loop(0, s // TK, kbody, 0)

    cq = jnp.concatenate(_split3(cumq_ref[0]), axis=1)
    for hh in range(HG_B):
        qn = _rms(q_ref[0, :, hh * HEAD_DIM:(hh + 1) * HEAD_DIM].astype(F32), gq_ref[...])
        qaug_scr[hh, :, 0:HEAD_DIM] = qn.astype(MXU_DTYPE)
        qaug_scr[hh, :, HEAD_DIM:] = aug(cq, MISC_F + g * HG_B + hh, base_q, 1.0, ones_q)
    mpart_scr[...] = jnp.full(mpart_scr.shape, NEG_BIG, F32)
    lpart_scr[...] = jnp.zeros(lpart_scr.shape, F32)
    oacc_scr[...] = jnp.zeros(oacc_scr.shape, F32)

    def put_logits(kt, hh, lg):
        logit_scr[kt, hh] = lg
        mpart_scr[hh] = jnp.maximum(mpart_scr[hh], jnp.maximum(lg[:, :LANES], lg[:, LANES:]))

    def raw_logits(kt, nt, hh):
        k0 = pl.multiple_of(kt * TK, TK)
        return _dot_nt(qaug_scr[hh], kaug_scr[hh, pl.ds(k0, nt * TK), :]) * cscale

    def off_tiles(kt, nt):
        for hh in range(HG_B):
            lg = raw_logits(kt, nt, hh)
            for t in range(nt):
                put_logits(kt + t, hh, lg[:, t * TK:(t + 1) * TK])

    _pair_loop(qi, off_tiles)
    causal = (lax.broadcasted_iota(I32, (tq, TK), 1) <= lax.broadcasted_iota(I32, (tq, TK), 0))
    for hh in range(HG_B):
        put_logits(qi, hh, jnp.where(causal, raw_logits(qi, 1, hh), NEG_BIG))

    for hh in range(HG_B):
        m = jnp.max(mpart_scr[hh], axis=-1, keepdims=True)
        mpart_scr[hh] = jnp.broadcast_to(m, (tq, LANES))

    def pv_tiles(kt, nt):
        k0 = pl.multiple_of(kt * TK, TK)
        for hh in range(HG_B):
            mb = mpart_scr[hh]
            mb2 = jnp.concatenate([mb, mb], axis=1)
            ps = [jnp.exp2(logit_scr[kt + t, hh] - mb2) for t in range(nt)]
            lsum = ps[0][:, :LANES] + ps[0][:, LANES:]
            for p in ps[1:]:
                lsum = lsum + p[:, :LANES] + p[:, LANES:]
            lpart_scr[hh] += lsum
            p_all = ps[0] if nt == 1 else jnp.concatenate(ps, axis=1)
            oacc_scr[hh] += _dot(p_all.astype(MXU_DTYPE),
                                 v_ref[0, pl.ds(k0, nt * TK), hh * HEAD_DIM:(hh + 1) * HEAD_DIM])

    _pair_loop(qi + 1, pv_tiles)
    for hh in range(HG_B):
        lsum = jnp.sum(lpart_scr[hh], axis=-1, keepdims=True)
        o_ref[0, :, hh * HEAD_DIM:(hh + 1) * HEAD_DIM] = (oacc_scr[hh] / lsum).astype(o_ref.dtype)


def _fox(proj3, cums, g_qb, g_kb, n_heads):
    b, s, _ = proj3.shape
    tq = TQ_B
    gw = HG_B * HEAD_DIM
    return pl.pallas_call(
        _fox_kernel,
        grid=(b, n_heads // HG_B, s // tq),
        in_specs=[pl.BlockSpec((1, tq, gw), lambda bb, g, i: (bb, i, U_QB // HG_B + g)),
                  pl.BlockSpec((1, s, gw), lambda bb, g, i: (bb, 0, U_KB // HG_B + g)),
                  pl.BlockSpec((1, s, gw), lambda bb, g, i: (bb, 0, U_VB // HG_B + g)),
                  pl.BlockSpec((1, tq, LANES), lambda bb, g, i: (bb, i, 0)),
                  pl.BlockSpec((1, s, LANES), lambda bb, g, i: (bb, 0, 0)),
                  pl.BlockSpec((1, HEAD_DIM), lambda bb, g, i: (0, 0)),
                  pl.BlockSpec((1, HEAD_DIM), lambda bb, g, i: (0, 0))],
        out_specs=pl.BlockSpec((1, tq, gw), lambda bb, g, i: (bb, i, g)),
        out_shape=jax.ShapeDtypeStruct((b, s, n_heads * HEAD_DIM), MXU_DTYPE),
        scratch_shapes=[pltpu.VMEM((HG_B, s, 2 * HEAD_DIM), MXU_DTYPE),
                        pltpu.VMEM((HG_B, tq, 2 * HEAD_DIM), MXU_DTYPE),
                        pltpu.VMEM((s // TK, HG_B, tq, TK), F32),
                        pltpu.VMEM((HG_B, tq, LANES), F32),
                        pltpu.VMEM((HG_B, tq, LANES), F32),
                        pltpu.VMEM((HG_B, tq, HEAD_DIM), F32)],
        compiler_params=_cparams(("parallel", "parallel", "arbitrary")),
        name="fox",
    )(proj3, proj3, proj3, cums, cums, g_qb, g_kb)


TM_MERGE = 256


def _merge_kernel(ga_ref, gb_ref, oa_ref, ob_ref, x_ref, mod_ref, bga_ref, bgb_ref, wpa_ref, wpb_ref,
                  wo_ref, gffn_ref, wrh_ref, wrl_ref, br_ref, x1_ref, h2_ref, rl_ref):
    ga = jax.nn.sigmoid(ga_ref[...].astype(F32) + bga_ref[...])
    gb = jax.nn.sigmoid(gb_ref[...].astype(F32) + bgb_ref[...])
    merged = ga * _dot(oa_ref[...], wpa_ref[...]) + gb * _dot(ob_ref[...], wpb_ref[...])
    upd = _dot(merged.astype(MXU_DTYPE), wo_ref[...])
    x1 = x_ref[...] + mod_ref[0, 2:3, :] * upd
    x1_ref[...] = x1
    h2 = _rms(x1, gffn_ref[...]) * (1.0 + mod_ref[0, 4:5, :]) + mod_ref[0, 3:4, :]
    h2_ref[...] = h2
    hh = h2.astype(MXU_DTYPE)
    hl = (h2 - hh.astype(F32)).astype(MXU_DTYPE)
    rl_ref[...] = (_dot(hh, wrh_ref[...]) + _dot(hl, wrh_ref[...]) + _dot(hh, wrl_ref[...])
                   + br_ref[...])


def _merge(proj, o_a, o_b, x2, mod3, b_gate, w_pa, w_pb, w_o, g_ffn, wr_hi, wr_lo, b_r, seq):
    t, d = x2.shape
    tm = TM_MERGE
    per_b = seq // tm
    wa = o_a.shape[1]
    res = lambda shape: pl.BlockSpec(shape, lambda i: (0,) * len(shape), pipeline_mode=pl.Buffered(1))
    return pl.pallas_call(
        _merge_kernel,
        grid=(t // tm,),
        in_specs=[pl.BlockSpec((tm, d), lambda i: (i, 0)),
                  pl.BlockSpec((tm, d), lambda i: (i, 1)),
                  pl.BlockSpec((tm, wa), lambda i: (i, 0)),
                  pl.BlockSpec((tm, wa), lambda i: (i, 0)),
                  pl.BlockSpec((tm, d), lambda i: (i, 0)),
                  pl.BlockSpec((1, 6, d), lambda i: (i // per_b, 0, 0)),
                  pl.BlockSpec((1, d), lambda i: (0, 0)),
                  pl.BlockSpec((1, d), lambda i: (0, 1)),
                  res((wa, d)), res((wa, d)), res((d, d)),
                  pl.BlockSpec((1, d), lambda i: (0, 0)),
                  res((d, LANES)), res((d, LANES)),
                  pl.BlockSpec((1, LANES), lambda i: (0, 0))],
        out_specs=[pl.BlockSpec((tm, d), lambda i: (i, 0)),
                   pl.BlockSpec((tm, d), lambda i: (i, 0)),
                   pl.BlockSpec((tm, LANES), lambda i: (i, 0))],
        out_shape=[jax.ShapeDtypeStruct((t, d), F32),
                   jax.ShapeDtypeStruct((t, d), F32),
                   jax.ShapeDtypeStruct((t, LANES), F32)],
        compiler_params=_cparams(("parallel",)),
        name="merge",
    )(proj, proj, o_a, o_b, x2, mod3, b_gate, b_gate, w_pa, w_pb, w_o, g_ffn, wr_hi, wr_lo, b_r)


TM_ROUTE = 1024
TM_ROWS = 128
R_E0, R_E1, R_P0, R_P1 = 0, 1, 4, 5
IT_TILE, IT_E, IT_LO, IT_HI, IT_FLAG, IT_NEXT = range(6)
F_VALID, F_FIRST_OF_EXPERT, F_FIRST_OF_TILE = 1, 2, 4


def _route_kernel(rl_ref, route_ref, post_ref, items_ref, cnt_scr, run_scr, offs_scr, *, n_tiles):
    tm = rl_ref.shape[0]
    sweep = pl.program_id(0)
    step = pl.program_id(1)
    iw = items_ref.shape[1]

    @pl.when((sweep == 0) & (step == 0))
    def _():
        cnt_scr[...] = jnp.zeros(cnt_scr.shape, F32)

    r = rl_ref[...]
    lane = lax.broadcasted_iota(I32, (tm, LANES), 1).astype(F32)
    neg_inf = -jnp.inf
    gmask = lane < N_GROUPS
    gl = jnp.where(gmask, r, neg_inf)
    gmax = jnp.max(gl, axis=-1, keepdims=True)
    gidx = jnp.min(jnp.where(gl == gmax, lane, float(LANES)), axis=-1, keepdims=True)
    gsum = jnp.sum(jnp.where(gmask, jnp.exp(r - gmax), 0.0), axis=-1, keepdims=True)
    gw = 1.0 / gsum
    lo = N_GROUPS + EXPERTS_PER_GROUP * gidx
    emask = (lane >= lo) & (lane < lo + EXPERTS_PER_GROUP)
    el = jnp.where(emask, r, neg_inf)
    v0 = jnp.max(el, axis=-1, keepdims=True)
    i0 = jnp.min(jnp.where(el == v0, lane, float(LANES)), axis=-1, keepdims=True)
    el2 = jnp.where(lane == i0, neg_inf, el)
    v1 = jnp.max(el2, axis=-1, keepdims=True)
    i1 = jnp.min(jnp.where(el2 == v1, lane, float(LANES)), axis=-1, keepdims=True)
    tt = jnp.exp(v1 - v0)
    p0 = gw / (1.0 + tt)
    p1 = gw * tt / (1.0 + tt)
    e0 = i0 - N_GROUPS
    e1 = i1 - N_GROUPS

    hit0 = lane == e0
    hit1 = lane == e1
    oh = jnp.where(hit0 | hit1, 1.0, 0.0)

    @pl.when(sweep == 0)
    def _():
        ones = jnp.ones((tm, LANES), MXU_DTYPE)
        cnt_scr[...] += _dot(oh.T.astype(MXU_DTYPE), ones)

    @pl.when((sweep == 1) & (step == 0))
    def _():
        _plan_items(cnt_scr[...], items_ref, offs_scr, n_tiles, iw)
        run_scr[...] = jnp.zeros(run_scr.shape, F32)

    @pl.when(sweep == 1)
    def _():
        rr = lax.broadcasted_iota(I32, (tm, tm), 0)
        cc = lax.broadcasted_iota(I32, (tm, tm), 1)
        ltri = jnp.where(cc < rr, 1.0, 0.0).astype(MXU_DTYPE)
        before = _dot(ltri, oh.astype(MXU_DTYPE)) + run_scr[...] + offs_scr[...]
        pos0 = jnp.sum(jnp.where(hit0, before, 0.0), axis=-1, keepdims=True)
        pos1 = jnp.sum(jnp.where(hit1, before, 0.0), axis=-1, keepdims=True)
        run_scr[...] = run_scr[...] + jnp.sum(oh, axis=0, keepdims=True)
        out = jnp.zeros((tm, LANES), F32)
        for k, val in ((R_E0, e0), (R_E1, e1), (R_P0, p0), (R_P1, p1)):
            out = jnp.where(lane == k, val, out)
        route_ref[...] = out
        pmat = jnp.where(lane == 0.0, pos0, jnp.where(lane == 1.0, pos1, 0.0))
        post_ref[...] = pmat.T[0:8, :].astype(I32)


def _plan_items(cnt_col, items_ref, offs_scr, n_tiles, iw):
    tmr = float(TM_ROWS)
    sub = lax.broadcasted_iota(I32, (LANES, LANES), 0)
    lan = lax.broadcasted_iota(I32, (LANES, LANES), 1)
    lstrict = jnp.where(lan < sub, 1.0, 0.0).astype(MXU_DTYPE)
    hi = jnp.floor(cnt_col * (1.0 / LANES))
    lo = cnt_col - hi * LANES
    offs_col = _dot(lstrict, hi.astype(MXU_DTYPE)) * LANES + _dot(lstrict, lo.astype(MXU_DTYPE))
    first_t = jnp.floor(offs_col * (1.0 / tmr))
    last_t = jnp.floor((offs_col + cnt_col - 1.0) * (1.0 / tmr))
    n_col = jnp.where(cnt_col > 0.0, last_t - first_t + 1.0, 0.0)
    base_col = _dot(lstrict, n_col.astype(MXU_DTYPE))
    end_col = base_col + n_col
    offs_scr[...] = offs_col.T[0:1, :]

    rep = lambda col: jnp.concatenate([col] * (iw // LANES), axis=1)
    wl = lax.broadcasted_iota(I32, (LANES, iw), 1).astype(F32)
    sub_f = lax.broadcasted_iota(I32, (LANES, iw), 0).astype(F32)
    end_r = rep(end_col)
    w_total = end_r[LANES - 1:LANES, :]
    count_le = lambda v: jnp.sum(jnp.where(end_r <= v, 1.0, 0.0), axis=0, keepdims=True)
    w_row = wl[0:1, :]
    eidx = count_le(w_row)
    sel = sub_f == eidx
    pick = lambda col: jnp.sum(jnp.where(sel, rep(col), 0.0), axis=0, keepdims=True)
    e_base, e_first, e_offs, e_cnt, e_end = (pick(base_col), pick(first_t), pick(offs_col),
                                             pick(cnt_col), pick(end_col))
    valid = w_row < w_total
    tile = e_first + (w_row - e_base)
    row_lo = jnp.maximum(e_offs, tile * tmr) - tile * tmr
    row_hi = jnp.minimum(e_offs + e_cnt, (tile + 1.0) * tmr) - tile * tmr
    first_e = w_row == e_base
    flags = jnp.where(valid, F_VALID + jnp.where(first_e, float(F_FIRST_OF_EXPERT), 0.0)
                      + jnp.where(row_lo == 0.0, float(F_FIRST_OF_TILE), 0.0), 0.0)
    nxt = jnp.where(valid & first_e & (e_end < w_total), count_le(e_end), -1.0)
    e_last = count_le(w_total - 1.0)
    rows = {IT_TILE: jnp.where(valid, tile, n_tiles - 1.0),
            IT_E: jnp.where(valid, eidx, e_last),
            IT_LO: jnp.where(valid, row_lo, 0.0),
            IT_HI: jnp.where(valid, row_hi, 0.0),
            IT_FLAG: flags,
            IT_NEXT: nxt}
    sub8 = lax.broadcasted_iota(I32, (8, iw), 0)
    out = jnp.zeros((8, iw), F32)
    for k, val in rows.items():
        out = jnp.where(sub8 == k, val, out)
    items_ref[...] = out.astype(I32)


def _route(rlog, n_items):
    t = rlog.shape[0]
    tm = min(TM_ROUTE, t)
    iw = ((n_items + LANES - 1) // LANES) * LANES
    kern = functools.partial(_route_kernel, n_tiles=2 * t // TM_ROWS)
    return pl.pallas_call(
        kern,
        grid=(2, t // tm),
        in_specs=[pl.BlockSpec((tm, LANES), lambda p, i: (i, 0))],
        out_specs=[pl.BlockSpec((tm, LANES), lambda p, i: (i * p, 0)),
                   pl.BlockSpec((8, tm), lambda p, i: (0, i * p)),
                   pl.BlockSpec((8, iw), lambda p, i: (0, 0))],
        out_shape=[jax.ShapeDtypeStruct((t, LANES), F32),
                   jax.ShapeDtypeStruct((8, t), I32),
                   jax.ShapeDtypeStruct((8, iw), I32)],
        scratch_shapes=[pltpu.VMEM((LANES, LANES), F32),
                        pltpu.VMEM((1, LANES), F32),
                        pltpu.VMEM((1, LANES), F32)],
        compiler_params=_cparams(("arbitrary", "arbitrary")),
        name="route",
    )(rlog)


TM_DISP = 256


def _dispatch_kernel(pos0_ref, pos1_ref, h2_ref, xs_ref, sem):
    base = pl.program_id(0) * TM_DISP

    def row_copy(r, p):
        return pltpu.make_async_copy(h2_ref.at[pl.ds(r, 1), :], xs_ref.at[pl.ds(p, 1), :], sem)

    def issue(r, carry):
        row_copy(r, pos0_ref[base + r]).start(priority=0)
        row_copy(r, pos1_ref[base + r]).start(priority=1)
        return carry

    lax.fori_loop(0, TM_DISP, issue, 0, unroll=8)

    def drain(r, carry):
        row_copy(0, 0).wait()
        row_copy(0, 0).wait()
        return carry

    lax.fori_loop(0, TM_DISP, drain, 0, unroll=8)


def _dispatch(pos0, pos1, h2):
    t, d = h2.shape
    return pl.pallas_call(
        _dispatch_kernel,
        grid_spec=pltpu.PrefetchScalarGridSpec(
            num_scalar_prefetch=2,
            grid=(t // TM_DISP,),
            in_specs=[pl.BlockSpec((TM_DISP, d), lambda i, p0, p1: (i, 0))],
            out_specs=pl.BlockSpec(memory_space=pl.ANY),
            scratch_shapes=[pltpu.SemaphoreType.DMA(())]),
        out_shape=jax.ShapeDtypeStruct((2 * t, d), h2.dtype),
        compiler_params=pltpu.CompilerParams(dimension_semantics=("arbitrary",),
                                             vmem_limit_bytes=VMEM_LIMIT, has_side_effects=True),
        name="dispatch",
    )(pos0, pos1, h2)


def _experts_kernel(tile_ref, e_ref, lo_ref, hi_ref, flag_ref, next_ref, xs_ref, w1_hbm, w3_hbm, w2_hbm,
                    ys_ref, f1_scr, f3_scr, f2_scr, w1_scr, w3_scr, w2_scr, sems):
    del tile_ref
    w = pl.program_id(0)
    flag = flag_ref[w]

    mats = ((w1_hbm, f1_scr, w1_scr), (w3_hbm, f3_scr, w3_scr), (w2_hbm, f2_scr, w2_scr))

    def weight_copy(k, e):
        return pltpu.make_async_copy(mats[k][0].at[e], mats[k][1], sems.at[k])

    @pl.when(w == 0)
    def _():
        for k in range(3):
            weight_copy(k, e_ref[0]).start()

    @pl.when((flag & F_FIRST_OF_EXPERT) != 0)
    def _():
        nxt = next_ref[w]
        for k in range(3):
            weight_copy(k, e_ref[w]).wait()
            mats[k][2][...] = mats[k][1][...].astype(mats[k][2].dtype)

            @pl.when(nxt >= 0)
            def _():
                weight_copy(k, nxt).start()

    @pl.when((flag & F_VALID) != 0)
    def _():
        x = xs_ref[...].astype(MXU_DTYPE)
        a = _dot(x, w1_scr[...])
        u = _dot(x, w3_scr[...])
        hm = (a * jax.nn.sigmoid(a)) * u
        res = _dot(hm.astype(MXU_DTYPE), w2_scr[...])
        row = lax.broadcasted_iota(I32, res.shape, 0)
        mine = (row >= lo_ref[w]) & (row < hi_ref[w])

        @pl.when((flag & F_FIRST_OF_TILE) != 0)
        def _():
            ys_ref[...] = jnp.where(mine, res, 0.0)

        @pl.when((flag & F_FIRST_OF_TILE) == 0)
        def _():
            ys_ref[...] = jnp.where(mine, res, ys_ref[...])


def _experts(items, xs, w1, w3, w2, n_items):
    n_rows, d = xs.shape
    f = w1.shape[2]
    tm = TM_ROWS
    tile_map = lambda w, tile, *_: (tile[w], 0)
    return pl.pallas_call(
        _experts_kernel,
        grid_spec=pltpu.PrefetchScalarGridSpec(
            num_scalar_prefetch=6,
            grid=(n_items,),
            in_specs=[pl.BlockSpec((tm, d), tile_map),
                      pl.BlockSpec(memory_space=pl.ANY),
                      pl.BlockSpec(memory_space=pl.ANY),
                      pl.BlockSpec(memory_space=pl.ANY)],
            out_specs=pl.BlockSpec((tm, d), tile_map),
            scratch_shapes=[pltpu.VMEM((d, f), F32),
                            pltpu.VMEM((d, f), F32),
                            pltpu.VMEM((f, d), F32),
                            pltpu.VMEM((d, f), MXU_DTYPE),
                            pltpu.VMEM((d, f), MXU_DTYPE),
                            pltpu.VMEM((f, d), MXU_DTYPE),
                            pltpu.SemaphoreType.DMA((3,))]),
        out_shape=jax.ShapeDtypeStruct((n_rows, d), F32),
        compiler_params=_cparams(("arbitrary",)),
        name="experts",
    )(items[IT_TILE], items[IT_E], items[IT_LO], items[IT_HI], items[IT_FLAG], items[IT_NEXT],
      xs, w1, w3, w2)


TM_COMB = 256


def _combine_kernel(pos0_ref, pos1_ref, ys_ref, x1_ref, route_ref, mod_ref, o_ref, y0_scr, y1_scr, sem):
    base = pl.program_id(0) * TM_COMB

    def row_copy(p, dst, r):
        return pltpu.make_async_copy(ys_ref.at[pl.ds(p, 1), :], dst.at[pl.ds(r, 1), :], sem)

    def issue(r, carry):
        row_copy(pos0_ref[base + r], y0_scr, r).start(priority=0)
        row_copy(pos1_ref[base + r], y1_scr, r).start(priority=1)
        return carry

    lax.fori_loop(0, TM_COMB, issue, 0, unroll=8)

    def drain(r, carry):
        row_copy(0, y0_scr, 0).wait()
        row_copy(0, y1_scr, 0).wait()
        return carry

    lax.fori_loop(0, TM_COMB, drain, 0, unroll=8)

    rt = route_ref[...]
    lane = lax.broadcasted_iota(I32, rt.shape, 1)
    p0 = jnp.sum(jnp.where(lane == R_P0, rt, 0.0), axis=-1, keepdims=True)
    p1 = jnp.sum(jnp.where(lane == R_P1, rt, 0.0), axis=-1, keepdims=True)
    y = p0 * y0_scr[...] + p1 * y1_scr[...]
    o_ref[...] = x1_ref[...] + mod_ref[0, 5:6, :] * y


def _combine(pos0, pos1, ys, x1, route, mod3, seq):
    t, d = x1.shape
    tm = TM_COMB
    per_b = seq // tm
    return pl.pallas_call(
        _combine_kernel,
        grid_spec=pltpu.PrefetchScalarGridSpec(
            num_scalar_prefetch=2,
            grid=(t // tm,),
            in_specs=[pl.BlockSpec(memory_space=pl.ANY),
                      pl.BlockSpec((tm, d), lambda i, p0, p1: (i, 0)),
                      pl.BlockSpec((tm, LANES), lambda i, p0, p1: (i, 0)),
                      pl.BlockSpec((1, 6, d), lambda i, p0, p1: (i // per_b, 0, 0))],
            out_specs=pl.BlockSpec((tm, d), lambda i, p0, p1: (i, 0)),
            scratch_shapes=[pltpu.VMEM((tm, d), F32),
                            pltpu.VMEM((tm, d), F32),
                            pltpu.SemaphoreType.DMA(())]),
        out_shape=jax.ShapeDtypeStruct((t, d), F32),
        compiler_params=_cparams(("arbitrary",)),
        name="combine",
    )(pos0, pos1, ys, x1, route, mod3)


def kernel(x, c, w_ada, b_ada, g_mix, w_in, b_gate, b_forget, w_uk, w_uv, g_qa, g_kv, g_qb, g_kb,
           w_pa, w_pb, w_o, g_ffn, w_rg, b_rg, w_re, b_re, w1, w3, w2):
    b, s, d = x.shape
    depth = w_ada.shape[0]
    t = b * s
    n_heads_b = b_forget.shape[1]
    n_items = 2 * t // TM_ROWS + N_EXPERTS
    c8 = jnp.zeros((8, d), F32).at[:b].set(c)
    x2 = x.reshape(t, d)
    row = lambda v: v.reshape(1, -1)
    for l in range(depth):
        mod3 = _ada(c8, w_ada[l], row(b_ada[l]))[:b].reshape(b, 6, d)
        proj, misc = _inproj(x2, mod3, row(g_mix[l]), _pack_w_in(jnp.transpose(w_in[l])), s)
        proj3 = proj.reshape(b, s, NP_COLS)
        misc3 = misc.reshape(b, s, LANES)
        bf_row = jnp.zeros((1, LANES), F32).at[0, MISC_F:MISC_F + n_heads_b].set(b_forget[l])
        kv, cums = _prep(proj3, misc3, row(g_kv[l]), bf_row)
        o_a = _dsa(proj3, misc3, kv, w_uk[l].astype(MXU_DTYPE), w_uv[l].astype(MXU_DTYPE), row(g_qa[l]))
        o_b = _fox(proj3, cums, row(g_qb[l]), row(g_kb[l]), n_heads_b)

        w_r = jnp.zeros((d, LANES), F32).at[:, :N_GROUPS].set(w_rg[l])
        w_r = w_r.at[:, N_GROUPS:N_GROUPS + N_EXPERTS].set(w_re[l])
        wr_hi = w_r.astype(MXU_DTYPE)
        wr_lo = (w_r - wr_hi.astype(F32)).astype(MXU_DTYPE)
        b_r = jnp.zeros((1, LANES), F32).at[0, :N_GROUPS].set(b_rg[l])
        b_r = b_r.at[0, N_GROUPS:N_GROUPS + N_EXPERTS].set(b_re[l])
        x1, h2, rlog = _merge(proj, o_a.reshape(t, -1), o_b.reshape(t, -1), x2, mod3, row(b_gate[l]),
                              w_pa[l].astype(MXU_DTYPE), w_pb[l].astype(MXU_DTYPE),
                              w_o[l].astype(MXU_DTYPE), row(g_ffn[l]), wr_hi, wr_lo, b_r, s)
        route, pos_t, items = _route(rlog, n_items)
        pos0, pos1 = pos_t[0], pos_t[1]
        xs = _dispatch(pos0, pos1, h2)
        ys = _experts(items, xs, w1[l], w3[l], w2[l], n_items)
        x2 = _combine(pos0, pos1, ys, x1, route, mod3, s)
    return x2.reshape(b, s, d)
```

```python
import functools

import jax
import jax.numpy as jnp
from jax import lax
from jax.experimental import pallas as pl
from jax.experimental.pallas import tpu as pltpu

F32 = jnp.float32
I32 = jnp.int32
MXU_DTYPE = jnp.bfloat16

CHUNK = 64
HEAD_DIM = 128
D_LAT = 256
N_IDX_HEADS = 16
D_IDX = 64
TOPK_MAX = 256
N_GROUPS = 8
EXPERTS_PER_GROUP = 8
N_EXPERTS = N_GROUPS * EXPERTS_PER_GROUP
RMS_EPS = 1e-6

LANES = 128
VMEM_LIMIT = 56 * 1024 * 1024

NEG_BIG = -1e30
INT_MIN = -2147483648


def _cparams(sem):
    return pltpu.CompilerParams(dimension_semantics=sem, vmem_limit_bytes=VMEM_LIMIT)


def _dot(a, b):
    return jnp.dot(a, b, preferred_element_type=F32)


def _dot_nt(a, b):
    return lax.dot_general(a, b, (((1,), (1,)), ((), ())), preferred_element_type=F32)


def _rms(x, g):
    return x * lax.rsqrt(jnp.mean(x * x, axis=-1, keepdims=True) + RMS_EPS) * g


def _ada_kernel(c_ref, w_ref, b_ref, o_ref):
    c = c_ref[...]
    a = c * jax.nn.sigmoid(c)
    o_ref[...] = _dot(a.astype(MXU_DTYPE), w_ref[...].astype(MXU_DTYPE)) + b_ref[...]


def _ada(c8, w_ada, b_ada):
    d, n = w_ada.shape
    tn = 1024
    return pl.pallas_call(
        _ada_kernel,
        grid=(n // tn,),
        in_specs=[pl.BlockSpec((8, d), lambda j: (0, 0)),
                  pl.BlockSpec((d, tn), lambda j: (0, j)),
                  pl.BlockSpec((1, tn), lambda j: (0, j))],
        out_specs=pl.BlockSpec((8, tn), lambda j: (0, j)),
        out_shape=jax.ShapeDtypeStruct((8, n), F32),
        compiler_params=_cparams(("arbitrary",)),
        name="ada",
    )(c8, w_ada, b_ada)


NP_COLS = 76 * LANES
U_QA, U_QIDX, U_QB, U_KB, U_VB, U_CKV, U_MISC = 32, 40, 48, 56, 64, 72, 74
TN_PROJ = 4 * LANES
TM_PROJ = 1024
MISC_TILE = (U_MISC * LANES) // TN_PROJ
MISC_OFF = U_MISC * LANES - MISC_TILE * TN_PROJ
MISC_K, MISC_F, MISC_W = 0, 64, 72


def _pack_moves():
    sizes = (1024, 256, 1024, 64, 16, 1024, 1024, 1024, 8, 4096)
    src = [0]
    for n in sizes:
        src.append(src[-1] + n)
    q_a, c_kv, q_idx, k_idx, w_idx, q_b, k_b, v_b, f_b, gate = src[:10]
    m = U_MISC * LANES
    return ((gate, 0, 4096), (q_a, U_QA * LANES, 1024), (q_idx, U_QIDX * LANES, 1024),
            (q_b, U_QB * LANES, 1024), (k_b, U_KB * LANES, 1024), (v_b, U_VB * LANES, 1024),
            (c_kv, U_CKV * LANES, 256), (f_b, m + MISC_F, 8), (w_idx, m + MISC_W, 16),
            (k_idx, m + MISC_K, 64))


def _pack_kernel(w_ref, o_ref):
    cols = o_ref.shape[1]
    m = U_MISC * LANES
    o_ref[m:m + 2 * LANES, :] = jnp.zeros((2 * LANES, cols), o_ref.dtype)
    for src, dst, n in _pack_moves():
        o_ref[dst:dst + n, :] = w_ref[src:src + n, :].astype(o_ref.dtype)


def _pack_w_in(w_in_t):
    n_in, d = w_in_t.shape
    tc = 256
    return pl.pallas_call(
        _pack_kernel,
        grid=(d // tc,),
        in_specs=[pl.BlockSpec((n_in, tc), lambda i: (0, i))],
        out_specs=pl.BlockSpec((NP_COLS, tc), lambda i: (0, i)),
        out_shape=jax.ShapeDtypeStruct((NP_COLS, d), MXU_DTYPE),
        compiler_params=_cparams(("parallel",)),
        name="pack",
    )(w_in_t)


def _inproj_kernel(x_ref, mod_ref, g_ref, w_ref, o_ref, misc_ref, h_scr):
    j = pl.program_id(1)

    @pl.when(j == 0)
    def _():
        y = _rms(x_ref[...], g_ref[...])
        sh = mod_ref[0, 0:1, :]
        sc = mod_ref[0, 1:2, :]
        h_scr[...] = (y * (1.0 + sc) + sh).astype(h_scr.dtype)

    acc = _dot_nt(h_scr[...], w_ref[...])
    o_ref[...] = acc.astype(o_ref.dtype)

    @pl.when(j == MISC_TILE)
    def _():
        misc_ref[...] = acc[:, MISC_OFF:MISC_OFF + LANES]


def _inproj(x2, mod3, g_mix, w_packed, seq):
    t, d = x2.shape
    tm = min(TM_PROJ, seq)
    per_b = seq // tm
    return pl.pallas_call(
        _inproj_kernel,
        grid=(t // tm, NP_COLS // TN_PROJ),
        in_specs=[pl.BlockSpec((tm, d), lambda i, j: (i, 0)),
                  pl.BlockSpec((1, 6, d), lambda i, j: (i // per_b, 0, 0)),
                  pl.BlockSpec((1, d), lambda i, j: (0, 0)),
                  pl.BlockSpec((TN_PROJ, d), lambda i, j: (j, 0))],
        out_specs=[pl.BlockSpec((tm, TN_PROJ), lambda i, j: (i, j)),
                   pl.BlockSpec((tm, LANES), lambda i, j: (i, 0))],
        out_shape=[jax.ShapeDtypeStruct((t, NP_COLS), MXU_DTYPE),
                   jax.ShapeDtypeStruct((t, LANES), F32)],
        scratch_shapes=[pltpu.VMEM((tm, d), MXU_DTYPE)],
        compiler_params=_cparams(("parallel", "arbitrary")),
        name="inproj",
    )(x2, mod3, g_mix, w_packed)


TK = 256


def _prep_kernel(ckv_ref, misc_ref, gkv_ref, bf_ref, kv_ref, cum_ref):
    s = ckv_ref.shape[1]
    kv_ref[0] = _rms(ckv_ref[0].astype(F32), gkv_ref[...]).astype(kv_ref.dtype)

    r = lax.broadcasted_iota(I32, (LANES, LANES), 0)
    c = lax.broadcasted_iota(I32, (LANES, LANES), 1)
    tri = jnp.where(c <= r, 1.0, 0.0).astype(MXU_DTYPE)
    carry = jnp.zeros((1, LANES), F32)
    for blk in range(s // LANES):
        z = misc_ref[0, blk * LANES:(blk + 1) * LANES, :] + bf_ref[...]
        ls = jnp.minimum(z, 0.0) - jnp.log1p(jnp.exp(-jnp.abs(z)))
        p1 = ls.astype(MXU_DTYPE)
        r1 = ls - p1.astype(F32)
        p2 = r1.astype(MXU_DTYPE)
        p3 = (r1 - p2.astype(F32)).astype(MXU_DTYPE)
        cs = _dot(tri, p1) + _dot(tri, p2) + _dot(tri, p3) + carry
        carry = cs[LANES - 1:LANES, :]
        cum_ref[0, blk * LANES:(blk + 1) * LANES, :] = cs * (HEAD_DIM ** 0.5)


def _prep(proj3, misc3, g_kv, bf_row):
    b, s, _ = proj3.shape
    return pl.pallas_call(
        _prep_kernel,
        grid=(b,),
        in_specs=[pl.BlockSpec((1, s, D_LAT), lambda i: (i, 0, U_CKV * LANES // D_LAT)),
                  pl.BlockSpec((1, s, LANES), lambda i: (i, 0, 0)),
                  pl.BlockSpec((1, D_LAT), lambda i: (0, 0)),
                  pl.BlockSpec((1, LANES), lambda i: (0, 0))],
        out_specs=[pl.BlockSpec((1, s, D_LAT), lambda i: (i, 0, 0)),
                   pl.BlockSpec((1, s, LANES), lambda i: (i, 0, 0))],
        out_shape=[jax.ShapeDtypeStruct((b, s, D_LAT), MXU_DTYPE),
                   jax.ShapeDtypeStruct((b, s, LANES), F32)],
        compiler_params=_cparams(("parallel",)),
        name="prep",
    )(proj3, misc3, g_kv, bf_row)


TQ_A = 256
N_BISECT = 32
LOG2E = 1.4426950408889634
assert TQ_A == TK


def _pair_loop(n, body):
    def pair(j, carry):
        body(2 * j, 2)
        return carry

    lax.fori_loop(0, jnp.right_shift(n, 1), pair, 0)

    @pl.when(jnp.bitwise_and(n, 1) == 1)
    def _():
        body(n - 1, 1)


def _dsa_kernel(qa_ref, qidx_ref, miscq_ref, misck_ref, kv_ref, wuk_ref, wuv_ref, gqa_ref, o_ref,
                sc_scr, bias_scr, logit_scr, qlat_scr, qh_scr, mpart_scr, lpart_scr, oacc_scr,
                *, topk, n_heads):
    i = pl.program_id(1)
    tq = TQ_A
    nk = i + 1
    q0 = i * tq

    wt = miscq_ref[0].T
    wq = wt[MISC_W:MISC_W + N_IDX_HEADS, :] * (D_IDX ** -0.5 * N_IDX_HEADS ** -0.5)
    for h in range(N_IDX_HEADS):
        qh_scr[h] = qidx_ref[0, :, h * D_IDX:(h + 1) * D_IDX]

    def fold(v, op, rows):
        return op(v.reshape(v.shape[0] // rows, rows, tq), axis=0)

    def score_body(kt, carry):
        mx8, mn8 = carry
        for half in range(TK // LANES):
            k0 = pl.multiple_of(kt * TK + half * LANES, LANES)
            kx = misck_ref[0, pl.ds(k0, LANES), :][:, MISC_K:MISC_K + D_IDX].astype(MXU_DTYPE)
            acc = jnp.zeros((LANES, tq), F32)
            for h in range(N_IDX_HEADS):
                d = _dot_nt(kx, qh_scr[h])
                acc = acc + jnp.maximum(d, 0.0) * wq[h:h + 1, :]
            kpos = k0 + lax.broadcasted_iota(I32, (LANES, tq), 0)
            qpos = q0 + lax.broadcasted_iota(I32, (LANES, tq), 1)
            adm = (kpos // CHUNK) <= (qpos // CHUNK)
            sc_scr[pl.ds(k0, LANES), :] = jnp.where(adm, acc, NEG_BIG)
            mx8 = jnp.maximum(mx8, fold(jnp.where(adm, acc, NEG_BIG), jnp.max, 8))
            mn8 = jnp.minimum(mn8, fold(jnp.where(adm, acc, -NEG_BIG), jnp.min, 8))
        return mx8, mn8

    mx8, mn8 = lax.fori_loop(0, nk, score_body,
                             (jnp.full((8, tq), NEG_BIG, F32), jnp.full((8, tq), -NEG_BIG, F32)))

    def bis_body(it, carry):
        lo, hi = carry
        mid = lo + 0.5 * (hi - lo)

        def cnt_body(kt, c32):
            k0 = pl.multiple_of(kt * TK, TK)
            return c32 + fold(jnp.where(sc_scr[pl.ds(k0, TK), :] >= mid, 1.0, 0.0), jnp.sum, 32)

        cnt = jnp.sum(lax.fori_loop(0, nk, cnt_body, jnp.zeros((32, tq), F32)), axis=0, keepdims=True)
        ok = cnt >= topk
        return jnp.where(ok, mid, lo), jnp.where(ok, hi, mid)

    thr, _ = lax.fori_loop(0, N_BISECT, bis_body, (jnp.min(mn8, axis=0, keepdims=True),
                                                   jnp.max(mx8, axis=0, keepdims=True)))

    def bias_body(kt, carry):
        k0 = pl.multiple_of(kt * TK, TK)
        sel_t = sc_scr[pl.ds(k0, TK), :] >= thr
        bias_scr[kt] = jnp.where(sel_t, 0.0, NEG_BIG).T
        return carry

    lax.fori_loop(0, nk, bias_body, 0)

    for h in range(n_heads):
        ql = _dot(qa_ref[0, :, h * HEAD_DIM:(h + 1) * HEAD_DIM], wuk_ref[h])
        ql = _rms(ql, gqa_ref[...]) * (D_LAT ** -0.5 * LOG2E)
        qlat_scr[h * tq:(h + 1) * tq, :] = ql.astype(qlat_scr.dtype)
    mpart_scr[...] = jnp.full(mpart_scr.shape, NEG_BIG, F32)
    lpart_scr[...] = jnp.zeros(lpart_scr.shape, F32)
    oacc_scr[...] = jnp.zeros(oacc_scr.shape, F32)
    slopes2 = [2.0 ** (-8.0 * (h + 1) / n_heads) * LOG2E for h in range(n_heads)]

    def logit_tiles(kt, nt, last):
        k0 = pl.multiple_of(kt * TK, TK)
        kvt = kv_ref[0, pl.ds(k0, nt * TK), :]
        kcol = (k0 + lax.broadcasted_iota(I32, (1, nt * TK), 1)).astype(F32)
        if last:
            ahead = jnp.maximum(lax.broadcasted_iota(I32, (tq, TK), 1)
                                - lax.broadcasted_iota(I32, (tq, TK), 0), 0).astype(F32)
        for h in range(n_heads):
            rows = slice(h * tq, (h + 1) * tq)
            lg = _dot_nt(qlat_scr[rows, :], kvt) + slopes2[h] * kcol
            for t in range(nt):
                lt = lg[:, t * TK:(t + 1) * TK] + bias_scr[kt + t]
                if last:
                    lt = lt - (2.0 * slopes2[h]) * ahead
                logit_scr[kt + t, rows, :] = lt
                mpart_scr[rows, :] = jnp.maximum(mpart_scr[rows, :],
                                                 jnp.maximum(lt[:, :LANES], lt[:, LANES:]))

    _pair_loop(i, lambda kt, nt: logit_tiles(kt, nt, False))
    logit_tiles(i, 1, True)

    m = jnp.max(mpart_scr[...], axis=-1, keepdims=True)
    mpart_scr[...] = jnp.broadcast_to(m, mpart_scr.shape)

    def pv_tiles(kt, nt):
        k0 = pl.multiple_of(kt * TK, TK)
        kvt = kv_ref[0, pl.ds(k0, nt * TK), :]
        for h in range(n_heads):
            rows = slice(h * tq, (h + 1) * tq)
            mb = mpart_scr[rows, :]
            mb2 = jnp.concatenate([mb, mb], axis=1)
            ps = [jnp.exp2(logit_scr[kt + t, rows, :] - mb2) for t in range(nt)]
            lsum = ps[0][:, :LANES] + ps[0][:, LANES:]
            for p in ps[1:]:
                lsum = lsum + p[:, :LANES] + p[:, LANES:]
            lpart_scr[rows, :] += lsum
            p_all = ps[0] if nt == 1 else jnp.concatenate(ps, axis=1)
            oacc_scr[rows, :] += _dot(p_all.astype(MXU_DTYPE), kvt)

    _pair_loop(nk, pv_tiles)

    for h in range(n_heads):
        rows = slice(h * tq, (h + 1) * tq)
        lsum = jnp.sum(lpart_scr[rows, :], axis=-1, keepdims=True)
        o_lat = oacc_scr[rows, :] / lsum
        o_ref[0, :, h * HEAD_DIM:(h + 1) * HEAD_DIM] = _dot(
            o_lat.astype(MXU_DTYPE), wuv_ref[h]).astype(o_ref.dtype)


def _dsa(proj3, misc3, kv, w_uk, w_uv, g_qa):
    b, s, _ = proj3.shape
    n_heads = w_uk.shape[0]
    width = n_heads * HEAD_DIM
    topk = min(TOPK_MAX, s // 4)
    tq = TQ_A
    nkt = s // TK
    kern = functools.partial(_dsa_kernel, topk=topk, n_heads=n_heads)
    return pl.pallas_call(
        kern,
        grid=(b, s // tq),
        in_specs=[pl.BlockSpec((1, tq, width), lambda bb, i: (bb, i, U_QA * LANES // width)),
                  pl.BlockSpec((1, tq, N_IDX_HEADS * D_IDX),
                               lambda bb, i: (bb, i, U_QIDX * LANES // (N_IDX_HEADS * D_IDX))),
                  pl.BlockSpec((1, tq, LANES), lambda bb, i: (bb, i, 0)),
                  pl.BlockSpec((1, s, LANES), lambda bb, i: (bb, 0, 0)),
                  pl.BlockSpec((1, s, D_LAT), lambda bb, i: (bb, 0, 0)),
                  pl.BlockSpec((n_heads, HEAD_DIM, D_LAT), lambda bb, i: (0, 0, 0)),
                  pl.BlockSpec((n_heads, D_LAT, HEAD_DIM), lambda bb, i: (0, 0, 0)),
                  pl.BlockSpec((1, D_LAT), lambda bb, i: (0, 0))],
        out_specs=pl.BlockSpec((1, tq, width), lambda bb, i: (bb, i, 0)),
        out_shape=jax.ShapeDtypeStruct((b, s, width), MXU_DTYPE),
        scratch_shapes=[pltpu.VMEM((s, tq), F32),
                        pltpu.VMEM((nkt, tq, TK), F32),
                        pltpu.VMEM((nkt, n_heads * tq, TK), F32),
                        pltpu.VMEM((n_heads * tq, D_LAT), MXU_DTYPE),
                        pltpu.VMEM((N_IDX_HEADS, tq, D_IDX), MXU_DTYPE),
                        pltpu.VMEM((n_heads * tq, LANES), F32),
                        pltpu.VMEM((n_heads * tq, LANES), F32),
                        pltpu.VMEM((n_heads * tq, D_LAT), F32)],
        compiler_params=_cparams(("parallel", "arbitrary")),
        name="dsa",
    )(proj3, proj3, misc3, misc3, kv, w_uk, w_uv, g_qa)


TQ_B = 256


HG_B = 4
assert TQ_B == TK


def _split3(x):
    p1 = x.astype(MXU_DTYPE)
    r1 = x - p1.astype(F32)
    p2 = r1.astype(MXU_DTYPE)
    p3 = (r1 - p2.astype(F32)).astype(MXU_DTYPE)
    return p1, p2, p3


def _fox_kernel(q_ref, k_ref, v_ref, cumq_ref, cumk_ref, gq_ref, gk_ref, o_ref,
                kaug_scr, qaug_scr, logit_scr, mpart_scr, lpart_scr, oacc_scr):
    g = pl.program_id(1)
    qi = pl.program_id(2)
    tq = TQ_B
    s = k_ref.shape[1]
    cscale = (HEAD_DIM ** -0.5) * LOG2E

    rr = lax.broadcasted_iota(I32, (3 * LANES, LANES), 0)
    cc = lax.broadcasted_iota(I32, (3 * LANES, LANES), 1)
    lane_row = lax.broadcasted_iota(I32, (1, LANES), 1)
    ones_q = jnp.where((lane_row >= 3) & (lane_row < 6), 1.0, 0.0)
    ones_k = jnp.where(lane_row < 3, 1.0, 0.0)
    base_q = jnp.where(cc < 3, rr - LANES * cc, -1)
    base_k = jnp.where((cc >= 3) & (cc < 6), rr - LANES * (cc - 3), -1)

    def aug(pieces, h, base, sign, ones_row):
        e = jnp.where(base == h, sign, 0.0).astype(MXU_DTYPE)
        return (_dot(pieces, e) + ones_row).astype(MXU_DTYPE)

    @pl.when(qi == 0)
    def _():
        def kbody(c, carry):
            r0 = pl.multiple_of(c * TK, TK)
            ck = jnp.concatenate(_split3(cumk_ref[0, pl.ds(r0, TK), :]), axis=1)
            for hh in range(HG_B):
                kn = _rms(k_ref[0, pl.ds(r0, TK), hh * HEAD_DIM:(hh + 1) * HEAD_DIM].astype(F32), gk_ref[...])
                kaug_scr[hh, pl.ds(r0, TK), 0:HEAD_DIM] = kn.astype(MXU_DTYPE)
                kaug_scr[hh, pl.ds(r0, TK), HEAD_DIM:] = aug(ck, MISC_F + g * HG_B + hh, base_k, -1.0, ones_k)
            return carry

        lax.fori_loop(0, s // TK, kbody, 0)

    cq = jnp.concatenate(_split3(cumq_ref[0]), axis=1)
    for hh in range(HG_B):
        qn = _rms(q_ref[0, :, hh * HEAD_DIM:(hh + 1) * HEAD_DIM].astype(F32), gq_ref[...])
        qaug_scr[hh, :, 0:HEAD_DIM] = qn.astype(MXU_DTYPE)
        qaug_scr[hh, :, HEAD_DIM:] = aug(cq, MISC_F + g * HG_B + hh, base_q, 1.0, ones_q)
    mpart_scr[...] = jnp.full(mpart_scr.shape, NEG_BIG, F32)
    lpart_scr[...] = jnp.zeros(lpart_scr.shape, F32)
    oacc_scr[...] = jnp.zeros(oacc_scr.shape, F32)

    def put_logits(kt, hh, lg):
        logit_scr[kt, hh] = lg
        mpart_scr[hh] = jnp.maximum(mpart_scr[hh], jnp.maximum(lg[:, :LANES], lg[:, LANES:]))

    def raw_logits(kt, nt, hh):
        k0 = pl.multiple_of(kt * TK, TK)
        return _dot_nt(qaug_scr[hh], kaug_scr[hh, pl.ds(k0, nt * TK), :]) * cscale

    def off_tiles(kt, nt):
        for hh in range(HG_B):
            lg = raw_logits(kt, nt, hh)
            for t in range(nt):
                put_logits(kt + t, hh, lg[:, t * TK:(t + 1) * TK])

    _pair_loop(qi, off_tiles)
    causal = (lax.broadcasted_iota(I32, (tq, TK), 1) <= lax.broadcasted_iota(I32, (tq, TK), 0))
    for hh in range(HG_B):
        put_logits(qi, hh, jnp.where(causal, raw_logits(qi, 1, hh), NEG_BIG))

    for hh in range(HG_B):
        m = jnp.max(mpart_scr[hh], axis=-1, keepdims=True)
        mpart_scr[hh] = jnp.broadcast_to(m, (tq, LANES))

    def pv_tiles(kt, nt):
        k0 = pl.multiple_of(kt * TK, TK)
        for hh in range(HG_B):
            mb = mpart_scr[hh]
            mb2 = jnp.concatenate([mb, mb], axis=1)
            ps = [jnp.exp2(logit_scr[kt + t, hh] - mb2) for t in range(nt)]
            lsum = ps[0][:, :LANES] + ps[0][:, LANES:]
            for p in ps[1:]:
                lsum = lsum + p[:, :LANES] + p[:, LANES:]
            lpart_scr[hh] += lsum
            p_all = ps[0] if nt == 1 else jnp.concatenate(ps, axis=1)
            oacc_scr[hh] += _dot(p_all.astype(MXU_DTYPE),
                                 v_ref[0, pl.ds(k0, nt * TK), hh * HEAD_DIM:(hh + 1) * HEAD_DIM])

    _pair_loop(qi + 1, pv_tiles)
    for hh in range(HG_B):
        lsum = jnp.sum(lpart_scr[hh], axis=-1, keepdims=True)
        o_ref[0, :, hh * HEAD_DIM:(hh + 1) * HEAD_DIM] = (oacc_scr[hh] / lsum).astype(o_ref.dtype)


def _fox(proj3, cums, g_qb, g_kb, n_heads):
    b, s, _ = proj3.shape
    tq = TQ_B
    gw = HG_B * HEAD_DIM
    return pl.pallas_call(
        _fox_kernel,
        grid=(b, n_heads // HG_B, s // tq),
        in_specs=[pl.BlockSpec((1, tq, gw), lambda bb, g, i: (bb, i, U_QB // HG_B + g)),
                  pl.BlockSpec((1, s, gw), lambda bb, g, i: (bb, 0, U_KB // HG_B + g)),
                  pl.BlockSpec((1, s, gw), lambda bb, g, i: (bb, 0, U_VB // HG_B + g)),
                  pl.BlockSpec((1, tq, LANES), lambda bb, g, i: (bb, i, 0)),
                  pl.BlockSpec((1, s, LANES), lambda bb, g, i: (bb, 0, 0)),
                  pl.BlockSpec((1, HEAD_DIM), lambda bb, g, i: (0, 0)),
                  pl.BlockSpec((1, HEAD_DIM), lambda bb, g, i: (0, 0))],
        out_specs=pl.BlockSpec((1, tq, gw), lambda bb, g, i: (bb, i, g)),
        out_shape=jax.ShapeDtypeStruct((b, s, n_heads * HEAD_DIM), MXU_DTYPE),
        scratch_shapes=[pltpu.VMEM((HG_B, s, 2 * HEAD_DIM), MXU_DTYPE),
                        pltpu.VMEM((HG_B, tq, 2 * HEAD_DIM), MXU_DTYPE),
                        pltpu.VMEM((s // TK, HG_B, tq, TK), F32),
                        pltpu.VMEM((HG_B, tq, LANES), F32),
                        pltpu.VMEM((HG_B, tq, LANES), F32),
                        pltpu.VMEM((HG_B, tq, HEAD_DIM), F32)],
        compiler_params=_cparams(("parallel", "parallel", "arbitrary")),
        name="fox",
    )(proj3, proj3, proj3, cums, cums, g_qb, g_kb)


TM_MERGE = 256


def _merge_kernel(ga_ref, gb_ref, oa_ref, ob_ref, x_ref, mod_ref, bga_ref, bgb_ref, wpa_ref, wpb_ref,
                  wo_ref, gffn_ref, wrh_ref, wrl_ref, br_ref, x1_ref, h2_ref, rl_ref):
    ga = jax.nn.sigmoid(ga_ref[...].astype(F32) + bga_ref[...])
    gb = jax.nn.sigmoid(gb_ref[...].astype(F32) + bgb_ref[...])
    merged = ga * _dot(oa_ref[...], wpa_ref[...]) + gb * _dot(ob_ref[...], wpb_ref[...])
    upd = _dot(merged.astype(MXU_DTYPE), wo_ref[...])
    x1 = x_ref[...] + mod_ref[0, 2:3, :] * upd
    x1_ref[...] = x1
    h2 = _rms(x1, gffn_ref[...]) * (1.0 + mod_ref[0, 4:5, :]) + mod_ref[0, 3:4, :]
    h2_ref[...] = h2
    hh = h2.astype(MXU_DTYPE)
    hl = (h2 - hh.astype(F32)).astype(MXU_DTYPE)
    rl_ref[...] = (_dot(hh, wrh_ref[...]) + _dot(hl, wrh_ref[...]) + _dot(hh, wrl_ref[...])
                   + br_ref[...])


def _merge(proj, o_a, o_b, x2, mod3, b_gate, w_pa, w_pb, w_o, g_ffn, wr_hi, wr_lo, b_r, seq):
    t, d = x2.shape
    tm = TM_MERGE
    per_b = seq // tm
    wa = o_a.shape[1]
    res = lambda shape: pl.BlockSpec(shape, lambda i: (0,) * len(shape), pipeline_mode=pl.Buffered(1))
    return pl.pallas_call(
        _merge_kernel,
        grid=(t // tm,),
        in_specs=[pl.BlockSpec((tm, d), lambda i: (i, 0)),
                  pl.BlockSpec((tm, d), lambda i: (i, 1)),
                  pl.BlockSpec((tm, wa), lambda i: (i, 0)),
                  pl.BlockSpec((tm, wa), lambda i: (i, 0)),
                  pl.BlockSpec((tm, d), lambda i: (i, 0)),
                  pl.BlockSpec((1, 6, d), lambda i: (i // per_b, 0, 0)),
                  pl.BlockSpec((1, d), lambda i: (0, 0)),
                  pl.BlockSpec((1, d), lambda i: (0, 1)),
                  res((wa, d)), res((wa, d)), res((d, d)),
                  pl.BlockSpec((1, d), lambda i: (0, 0)),
                  res((d, LANES)), res((d, LANES)),
                  pl.BlockSpec((1, LANES), lambda i: (0, 0))],
        out_specs=[pl.BlockSpec((tm, d), lambda i: (i, 0)),
                   pl.BlockSpec((tm, d), lambda i: (i, 0)),
                   pl.BlockSpec((tm, LANES), lambda i: (i, 0))],
        out_shape=[jax.ShapeDtypeStruct((t, d), F32),
                   jax.ShapeDtypeStruct((t, d), F32),
                   jax.ShapeDtypeStruct((t, LANES), F32)],
        compiler_params=_cparams(("parallel",)),
        name="merge",
    )(proj, proj, o_a, o_b, x2, mod3, b_gate, b_gate, w_pa, w_pb, w_o, g_ffn, wr_hi, wr_lo, b_r)


TM_ROUTE = 1024
TM_ROWS = 128
R_E0, R_E1, R_P0, R_P1 = 0, 1, 4, 5
IT_TILE, IT_E, IT_LO, IT_HI, IT_FLAG, IT_NEXT, IT_NEXT2, IT_SLOT = range(8)
F_VALID, F_FIRST_OF_EXPERT, F_FIRST_OF_TILE = 1, 2, 4


def _route_kernel(rl_ref, route_ref, post_ref, items_ref, cnt_scr, run_scr, offs_scr, *, n_tiles):
    tm = rl_ref.shape[0]
    sweep = pl.program_id(0)
    step = pl.program_id(1)
    iw = items_ref.shape[1]

    @pl.when((sweep == 0) & (step == 0))
    def _():
        cnt_scr[...] = jnp.zeros(cnt_scr.shape, F32)

    r = rl_ref[...]
    lane = lax.broadcasted_iota(I32, (tm, LANES), 1).astype(F32)
    neg_inf = -jnp.inf
    gmask = lane < N_GROUPS
    gl = jnp.where(gmask, r, neg_inf)
    gmax = jnp.max(gl, axis=-1, keepdims=True)
    gidx = jnp.min(jnp.where(gl == gmax, lane, float(LANES)), axis=-1, keepdims=True)
    gsum = jnp.sum(jnp.where(gmask, jnp.exp(r - gmax), 0.0), axis=-1, keepdims=True)
    gw = 1.0 / gsum
    lo = N_GROUPS + EXPERTS_PER_GROUP * gidx
    emask = (lane >= lo) & (lane < lo + EXPERTS_PER_GROUP)
    el = jnp.where(emask, r, neg_inf)
    v0 = jnp.max(el, axis=-1, keepdims=True)
    i0 = jnp.min(jnp.where(el == v0, lane, float(LANES)), axis=-1, keepdims=True)
    el2 = jnp.where(lane == i0, neg_inf, el)
    v1 = jnp.max(el2, axis=-1, keepdims=True)
    i1 = jnp.min(jnp.where(el2 == v1, lane, float(LANES)), axis=-1, keepdims=True)
    tt = jnp.exp(v1 - v0)
    p0 = gw / (1.0 + tt)
    p1 = gw * tt / (1.0 + tt)
    e0 = i0 - N_GROUPS
    e1 = i1 - N_GROUPS

    hit0 = lane == e0
    hit1 = lane == e1
    oh = jnp.where(hit0 | hit1, 1.0, 0.0)

    @pl.when(sweep == 0)
    def _():
        ones = jnp.ones((tm, LANES), MXU_DTYPE)
        cnt_scr[...] += _dot(oh.T.astype(MXU_DTYPE), ones)

    @pl.when((sweep == 1) & (step == 0))
    def _():
        _plan_items(cnt_scr[...], items_ref, offs_scr, n_tiles, iw)
        run_scr[...] = jnp.zeros(run_scr.shape, F32)

    @pl.when(sweep == 1)
    def _():
        rr = lax.broadcasted_iota(I32, (tm, tm), 0)
        cc = lax.broadcasted_iota(I32, (tm, tm), 1)
        ltri = jnp.where(cc < rr, 1.0, 0.0).astype(MXU_DTYPE)
        before = _dot(ltri, oh.astype(MXU_DTYPE)) + run_scr[...] + offs_scr[...]
        pos0 = jnp.sum(jnp.where(hit0, before, 0.0), axis=-1, keepdims=True)
        pos1 = jnp.sum(jnp.where(hit1, before, 0.0), axis=-1, keepdims=True)
        run_scr[...] = run_scr[...] + jnp.sum(oh, axis=0, keepdims=True)
        out = jnp.zeros((tm, LANES), F32)
        for k, val in ((R_E0, e0), (R_E1, e1), (R_P0, p0), (R_P1, p1)):
            out = jnp.where(lane == k, val, out)
        route_ref[...] = out
        pmat = jnp.where(lane == 0.0, pos0, jnp.where(lane == 1.0, pos1, 0.0))
        post_ref[...] = pmat.T[0:8, :].astype(I32)


def _plan_items(cnt_col, items_ref, offs_scr, n_tiles, iw):
    tmr = float(TM_ROWS)
    sub = lax.broadcasted_iota(I32, (LANES, LANES), 0)
    lan = lax.broadcasted_iota(I32, (LANES, LANES), 1)
    lstrict = jnp.where(lan < sub, 1.0, 0.0).astype(MXU_DTYPE)
    hi = jnp.floor(cnt_col * (1.0 / LANES))
    lo = cnt_col - hi * LANES
    offs_col = _dot(lstrict, hi.astype(MXU_DTYPE)) * LANES + _dot(lstrict, lo.astype(MXU_DTYPE))
    first_t = jnp.floor(offs_col * (1.0 / tmr))
    last_t = jnp.floor((offs_col + cnt_col - 1.0) * (1.0 / tmr))
    n_col = jnp.where(cnt_col > 0.0, last_t - first_t + 1.0, 0.0)
    base_col = _dot(lstrict, n_col.astype(MXU_DTYPE))
    end_col = base_col + n_col
    offs_scr[...] = offs_col.T[0:1, :]

    rep = lambda col: jnp.concatenate([col] * (iw // LANES), axis=1)
    wl = lax.broadcasted_iota(I32, (LANES, iw), 1).astype(F32)
    sub_f = lax.broadcasted_iota(I32, (LANES, iw), 0).astype(F32)
    end_r = rep(end_col)
    w_total = end_r[LANES - 1:LANES, :]
    count_le = lambda v: jnp.sum(jnp.where(end_r <= v, 1.0, 0.0), axis=0, keepdims=True)
    w_row = wl[0:1, :]
    eidx = count_le(w_row)
    sel = sub_f == eidx
    pick = lambda col: jnp.sum(jnp.where(sel, rep(col), 0.0), axis=0, keepdims=True)
    e_base, e_first, e_offs, e_cnt, e_end = (pick(base_col), pick(first_t), pick(offs_col),
                                             pick(cnt_col), pick(end_col))
    valid = w_row < w_total
    tile = e_first + (w_row - e_base)
    row_lo = jnp.maximum(e_offs, tile * tmr) - tile * tmr
    row_hi = jnp.minimum(e_offs + e_cnt, (tile + 1.0) * tmr) - tile * tmr
    first_e = w_row == e_base
    flags = jnp.where(valid, F_VALID + jnp.where(first_e, float(F_FIRST_OF_EXPERT), 0.0)
                      + jnp.where(row_lo == 0.0, float(F_FIRST_OF_TILE), 0.0), 0.0)
    has1 = valid & first_e & (e_end < w_total)
    e1 = count_le(e_end)
    end1 = jnp.sum(jnp.where(sub_f == e1, end_r, 0.0), axis=0, keepdims=True)
    has2 = has1 & (end1 < w_total)
    nxt = jnp.where(has1, e1, -1.0)
    nxt2 = jnp.where(has2, count_le(end1), -1.0)
    ord_col = _dot(lstrict, jnp.where(cnt_col > 0.0, 1.0, 0.0).astype(MXU_DTYPE))
    e_ord = pick(ord_col)
    slot = e_ord - 2.0 * jnp.floor(e_ord * 0.5)
    e_last = count_le(w_total - 1.0)
    rows = {IT_TILE: jnp.where(valid, tile, n_tiles - 1.0),
            IT_E: jnp.where(valid, eidx, e_last),
            IT_LO: jnp.where(valid, row_lo, 0.0),
            IT_HI: jnp.where(valid, row_hi, 0.0),
            IT_FLAG: flags,
            IT_NEXT: nxt,
            IT_NEXT2: nxt2,
            IT_SLOT: jnp.where(valid, slot, 0.0)}
    sub8 = lax.broadcasted_iota(I32, (8, iw), 0)
    out = jnp.zeros((8, iw), F32)
    for k, val in rows.items():
        out = jnp.where(sub8 == k, val, out)
    items_ref[...] = out.astype(I32)


def _route(rlog, n_items):
    t = rlog.shape[0]
    tm = min(TM_ROUTE, t)
    iw = ((n_items + LANES - 1) // LANES) * LANES
    kern = functools.partial(_route_kernel, n_tiles=2 * t // TM_ROWS)
    return pl.pallas_call(
        kern,
        grid=(2, t // tm),
        in_specs=[pl.BlockSpec((tm, LANES), lambda p, i: (i, 0))],
        out_specs=[pl.BlockSpec((tm, LANES), lambda p, i: (i * p, 0)),
                   pl.BlockSpec((8, tm), lambda p, i: (0, i * p)),
                   pl.BlockSpec((8, iw), lambda p, i: (0, 0))],
        out_shape=[jax.ShapeDtypeStruct((t, LANES), F32),
                   jax.ShapeDtypeStruct((8, t), I32),
                   jax.ShapeDtypeStruct((8, iw), I32)],
        scratch_shapes=[pltpu.VMEM((LANES, LANES), F32),
                        pltpu.VMEM((1, LANES), F32),
                        pltpu.VMEM((1, LANES), F32)],
        compiler_params=_cparams(("arbitrary", "arbitrary")),
        name="route",
    )(rlog)


TM_DISP = 256


def _dispatch_kernel(pos0_ref, pos1_ref, h2_ref, xs_ref, sem):
    base = pl.program_id(0) * TM_DISP

    def row_copy(r, p):
        return pltpu.make_async_copy(h2_ref.at[pl.ds(r, 1), :], xs_ref.at[pl.ds(p, 1), :], sem)

    def issue(r, carry):
        row_copy(r, pos0_ref[base + r]).start(priority=0)
        row_copy(r, pos1_ref[base + r]).start(priority=1)
        return carry

    lax.fori_loop(0, TM_DISP, issue, 0, unroll=8)

    def drain(r, carry):
        row_copy(0, 0).wait()
        row_copy(0, 0).wait()
        return carry

    lax.fori_loop(0, TM_DISP, drain, 0, unroll=8)


def _dispatch(pos0, pos1, h2):
    t, d = h2.shape
    return pl.pallas_call(
        _dispatch_kernel,
        grid_spec=pltpu.PrefetchScalarGridSpec(
            num_scalar_prefetch=2,
            grid=(t // TM_DISP,),
            in_specs=[pl.BlockSpec((TM_DISP, d), lambda i, p0, p1: (i, 0))],
            out_specs=pl.BlockSpec(memory_space=pl.ANY),
            scratch_shapes=[pltpu.SemaphoreType.DMA(())]),
        out_shape=jax.ShapeDtypeStruct((2 * t, d), h2.dtype),
        compiler_params=pltpu.CompilerParams(dimension_semantics=("arbitrary",),
                                             vmem_limit_bytes=VMEM_LIMIT, has_side_effects=True),
        name="dispatch",
    )(pos0, pos1, h2)


def _experts_kernel(tile_ref, e_ref, lo_ref, hi_ref, flag_ref, next_ref, next2_ref, slot_ref,
                    xs_ref, w1_hbm, w3_hbm, w2_hbm, ys_ref,
                    f1_scr, f3_scr, f2_scr, w1_scr, w3_scr, w2_scr, sems):
    del tile_ref
    w = pl.program_id(0)
    flag = flag_ref[w]

    mats = ((w1_hbm, f1_scr, w1_scr), (w3_hbm, f3_scr, w3_scr), (w2_hbm, f2_scr, w2_scr))

    def weight_copy(k, e, slot):
        return pltpu.make_async_copy(mats[k][0].at[e], mats[k][1].at[slot], sems.at[slot, k])

    @pl.when(w == 0)
    def _():
        for k in range(3):
            weight_copy(k, e_ref[0], 0).start()
        nxt = next_ref[0]

        @pl.when(nxt >= 0)
        def _():
            for k in range(3):
                weight_copy(k, nxt, 1).start()

    @pl.when((flag & F_FIRST_OF_EXPERT) != 0)
    def _():
        slot = slot_ref[w]
        nxt2 = next2_ref[w]
        for k in range(3):
            weight_copy(k, e_ref[w], slot).wait()
            mats[k][2][...] = mats[k][1][slot].astype(mats[k][2].dtype)

            @pl.when(nxt2 >= 0)
            def _():
                weight_copy(k, nxt2, slot).start()

    @pl.when((flag & F_VALID) != 0)
    def _():
        x = xs_ref[...].astype(MXU_DTYPE)
        a = _dot(x, w1_scr[...])
        u = _dot(x, w3_scr[...])
        hm = (a * jax.nn.sigmoid(a)) * u
        res = _dot(hm.astype(MXU_DTYPE), w2_scr[...])
        row = lax.broadcasted_iota(I32, res.shape, 0)
        mine = (row >= lo_ref[w]) & (row < hi_ref[w])

        @pl.when((flag & F_FIRST_OF_TILE) != 0)
        def _():
            ys_ref[...] = jnp.where(mine, res, 0.0)

        @pl.when((flag & F_FIRST_OF_TILE) == 0)
        def _():
            ys_ref[...] = jnp.where(mine, res, ys_ref[...])


def _experts(items, xs, w1, w3, w2, n_items):
    n_rows, d = xs.shape
    f = w1.shape[2]
    tm = TM_ROWS
    tile_map = lambda w, tile, *_: (tile[w], 0)
    return pl.pallas_call(
        _experts_kernel,
        grid_spec=pltpu.PrefetchScalarGridSpec(
            num_scalar_prefetch=8,
            grid=(n_items,),
            in_specs=[pl.BlockSpec((tm, d), tile_map),
                      pl.BlockSpec(memory_space=pl.ANY),
                      pl.BlockSpec(memory_space=pl.ANY),
                      pl.BlockSpec(memory_space=pl.ANY)],
            out_specs=pl.BlockSpec((tm, d), tile_map),
            scratch_shapes=[pltpu.VMEM((2, d, f), F32),
                            pltpu.VMEM((2, d, f), F32),
                            pltpu.VMEM((2, f, d), F32),
                            pltpu.VMEM((d, f), MXU_DTYPE),
                            pltpu.VMEM((d, f), MXU_DTYPE),
                            pltpu.VMEM((f, d), MXU_DTYPE),
                            pltpu.SemaphoreType.DMA((2, 3))]),
        out_shape=jax.ShapeDtypeStruct((n_rows, d), F32),
        compiler_params=_cparams(("arbitrary",)),
        name="experts",
    )(items[IT_TILE], items[IT_E], items[IT_LO], items[IT_HI], items[IT_FLAG], items[IT_NEXT],
      items[IT_NEXT2], items[IT_SLOT], xs, w1, w3, w2)


TM_COMB = 256


def _combine_kernel(pos0_ref, pos1_ref, ys_ref, x1_ref, route_ref, mod_ref, o_ref, y0_scr, y1_scr, sem):
    base = pl.program_id(0) * TM_COMB

    def row_copy(p, dst, r):
        return pltpu.make_async_copy(ys_ref.at[pl.ds(p, 1), :], dst.at[pl.ds(r, 1), :], sem)

    def issue(r, carry):
        row_copy(pos0_ref[base + r], y0_scr, r).start(priority=0)
        row_copy(pos1_ref[base + r], y1_scr, r).start(priority=1)
        return carry

    lax.fori_loop(0, TM_COMB, issue, 0, unroll=8)

    def drain(r, carry):
        row_copy(0, y0_scr, 0).wait()
        row_copy(0, y1_scr, 0).wait()
        return carry

    lax.fori_loop(0, TM_COMB, drain, 0, unroll=8)

    rt = route_ref[...]
    lane = lax.broadcasted_iota(I32, rt.shape, 1)
    p0 = jnp.sum(jnp.where(lane == R_P0, rt, 0.0), axis=-1, keepdims=True)
    p1 = jnp.sum(jnp.where(lane == R_P1, rt, 0.0), axis=-1, keepdims=True)
    y = p0 * y0_scr[...] + p1 * y1_scr[...]
    o_ref[...] = x1_ref[...] + mod_ref[0, 5:6, :] * y


def _combine(pos0, pos1, ys, x1, route, mod3, seq):
    t, d = x1.shape
    tm = TM_COMB
    per_b = seq // tm
    return pl.pallas_call(
        _combine_kernel,
        grid_spec=pltpu.PrefetchScalarGridSpec(
            num_scalar_prefetch=2,
            grid=(t // tm,),
            in_specs=[pl.BlockSpec(memory_space=pl.ANY),
                      pl.BlockSpec((tm, d), lambda i, p0, p1: (i, 0)),
                      pl.BlockSpec((tm, LANES), lambda i, p0, p1: (i, 0)),
                      pl.BlockSpec((1, 6, d), lambda i, p0, p1: (i // per_b, 0, 0))],
            out_specs=pl.BlockSpec((tm, d), lambda i, p0, p1: (i, 0)),
            scratch_shapes=[pltpu.VMEM((tm, d), F32),
                            pltpu.VMEM((tm, d), F32),
                            pltpu.SemaphoreType.DMA(())]),
        out_shape=jax.ShapeDtypeStruct((t, d), F32),
        compiler_params=_cparams(("arbitrary",)),
        name="combine",
    )(pos0, pos1, ys, x1, route, mod3)


def kernel(x, c, w_ada, b_ada, g_mix, w_in, b_gate, b_forget, w_uk, w_uv, g_qa, g_kv, g_qb, g_kb,
           w_pa, w_pb, w_o, g_ffn, w_rg, b_rg, w_re, b_re, w1, w3, w2):
    b, s, d = x.shape
    depth = w_ada.shape[0]
    t = b * s
    n_heads_b = b_forget.shape[1]
    n_items = 2 * t // TM_ROWS + N_EXPERTS
    c8 = jnp.zeros((8, d), F32).at[:b].set(c)
    x2 = x.reshape(t, d)
    row = lambda v: v.reshape(1, -1)
    for l in range(depth):
        mod3 = _ada(c8, w_ada[l], row(b_ada[l]))[:b].reshape(b, 6, d)
        proj, misc = _inproj(x2, mod3, row(g_mix[l]), _pack_w_in(jnp.transpose(w_in[l])), s)
        proj3 = proj.reshape(b, s, NP_COLS)
        misc3 = misc.reshape(b, s, LANES)
        bf_row = jnp.zeros((1, LANES), F32).at[0, MISC_F:MISC_F + n_heads_b].set(b_forget[l])
        kv, cums = _prep(proj3, misc3, row(g_kv[l]), bf_row)
        o_a = _dsa(proj3, misc3, kv, w_uk[l].astype(MXU_DTYPE), w_uv[l].astype(MXU_DTYPE), row(g_qa[l]))
        o_b = _fox(proj3, cums, row(g_qb[l]), row(g_kb[l]), n_heads_b)

        w_r = jnp.zeros((d, LANES), F32).at[:, :N_GROUPS].set(w_rg[l])
        w_r = w_r.at[:, N_GROUPS:N_GROUPS + N_EXPERTS].set(w_re[l])
        wr_hi = w_r.astype(MXU_DTYPE)
        wr_lo = (w_r - wr_hi.astype(F32)).astype(MXU_DTYPE)
        b_r = jnp.zeros((1, LANES), F32).at[0, :N_GROUPS].set(b_rg[l])
        b_r = b_r.at[0, N_GROUPS:N_GROUPS + N_EXPERTS].set(b_re[l])
        x1, h2, rlog = _merge(proj, o_a.reshape(t, -1), o_b.reshape(t, -1), x2, mod3, row(b_gate[l]),
                              w_pa[l].astype(MXU_DTYPE), w_pb[l].astype(MXU_DTYPE),
                              w_o[l].astype(MXU_DTYPE), row(g_ffn[l]), wr_hi, wr_lo, b_r, s)
        route, pos_t, items = _route(rlog, n_items)
        pos0, pos1 = pos_t[0], pos_t[1]
        xs = _dispatch(pos0, pos1, h2)
        ys = _experts(items, xs, w1[l], w3[l], w2[l], n_items)
        x2 = _combine(pos0, pos1, ys, x1, route, mod3, s)
    return x2.reshape(b, s, d)
```

```python
import functools

import jax
import jax.numpy as jnp
from jax import lax
from jax.experimental import pallas as pl
from jax.experimental.pallas import tpu as pltpu

F32 = jnp.float32
I32 = jnp.int32
MXU_DTYPE = jnp.bfloat16

CHUNK = 64
HEAD_DIM = 128
D_LAT = 256
N_IDX_HEADS = 16
D_IDX = 64
TOPK_MAX = 256
N_GROUPS = 8
EXPERTS_PER_GROUP = 8
N_EXPERTS = N_GROUPS * EXPERTS_PER_GROUP
RMS_EPS = 1e-6

LANES = 128
VMEM_LIMIT = 56 * 1024 * 1024

NEG_BIG = -1e30
INT_MIN = -2147483648


def _cparams(sem):
    return pltpu.CompilerParams(dimension_semantics=sem, vmem_limit_bytes=VMEM_LIMIT)


def _dot(a, b):
    return jnp.dot(a, b, preferred_element_type=F32)


def _dot_nt(a, b):
    return lax.dot_general(a, b, (((1,), (1,)), ((), ())), preferred_element_type=F32)


def _rms(x, g):
    return x * lax.rsqrt(jnp.mean(x * x, axis=-1, keepdims=True) + RMS_EPS) * g


def _ada_kernel(c_ref, w_ref, b_ref, o_ref):
    c = c_ref[...]
    a = c * jax.nn.sigmoid(c)
    o_ref[...] = _dot(a.astype(MXU_DTYPE), w_ref[...].astype(MXU_DTYPE)) + b_ref[...]


def _ada(c8, w_ada, b_ada):
    d, n = w_ada.shape
    tn = 1024
    return pl.pallas_call(
        _ada_kernel,
        grid=(n // tn,),
        in_specs=[pl.BlockSpec((8, d), lambda j: (0, 0)),
                  pl.BlockSpec((d, tn), lambda j: (0, j)),
                  pl.BlockSpec((1, tn), lambda j: (0, j))],
        out_specs=pl.BlockSpec((8, tn), lambda j: (0, j)),
        out_shape=jax.ShapeDtypeStruct((8, n), F32),
        compiler_params=_cparams(("arbitrary",)),
        name="ada",
    )(c8, w_ada, b_ada)


NP_COLS = 76 * LANES
U_QA, U_QIDX, U_QB, U_KB, U_VB, U_CKV, U_MISC = 32, 40, 48, 56, 64, 72, 74
TN_PROJ = 4 * LANES
TM_PROJ = 1024
MISC_TILE = (U_MISC * LANES) // TN_PROJ
MISC_OFF = U_MISC * LANES - MISC_TILE * TN_PROJ
MISC_K, MISC_F, MISC_W = 0, 64, 72


def _pack_moves():
    sizes = (1024, 256, 1024, 64, 16, 1024, 1024, 1024, 8, 4096)
    src = [0]
    for n in sizes:
        src.append(src[-1] + n)
    q_a, c_kv, q_idx, k_idx, w_idx, q_b, k_b, v_b, f_b, gate = src[:10]
    m = U_MISC * LANES
    return ((gate, 0, 4096), (q_a, U_QA * LANES, 1024), (q_idx, U_QIDX * LANES, 1024),
            (q_b, U_QB * LANES, 1024), (k_b, U_KB * LANES, 1024), (v_b, U_VB * LANES, 1024),
            (c_kv, U_CKV * LANES, 256), (f_b, m + MISC_F, 8), (w_idx, m + MISC_W, 16),
            (k_idx, m + MISC_K, 64))


def _pack_kernel(w_ref, o_ref):
    cols = o_ref.shape[1]
    m = U_MISC * LANES
    o_ref[m:m + 2 * LANES, :] = jnp.zeros((2 * LANES, cols), o_ref.dtype)
    for src, dst, n in _pack_moves():
        o_ref[dst:dst + n, :] = w_ref[src:src + n, :].astype(o_ref.dtype)


def _pack_w_in(w_in_t):
    n_in, d = w_in_t.shape
    tc = 256
    return pl.pallas_call(
        _pack_kernel,
        grid=(d // tc,),
        in_specs=[pl.BlockSpec((n_in, tc), lambda i: (0, i))],
        out_specs=pl.BlockSpec((NP_COLS, tc), lambda i: (0, i)),
        out_shape=jax.ShapeDtypeStruct((NP_COLS, d), MXU_DTYPE),
        compiler_params=_cparams(("parallel",)),
        name="pack",
    )(w_in_t)


def _inproj_kernel(x_ref, mod_ref, g_ref, w_ref, o_ref, misc_ref, h_scr):
    j = pl.program_id(1)

    @pl.when(j == 0)
    def _():
        y = _rms(x_ref[...], g_ref[...])
        sh = mod_ref[0, 0:1, :]
        sc = mod_ref[0, 1:2, :]
        h_scr[...] = (y * (1.0 + sc) + sh).astype(h_scr.dtype)

    acc = _dot_nt(h_scr[...], w_ref[...])
    o_ref[...] = acc.astype(o_ref.dtype)

    @pl.when(j == MISC_TILE)
    def _():
        misc_ref[...] = acc[:, MISC_OFF:MISC_OFF + LANES]


def _inproj(x2, mod3, g_mix, w_packed, seq):
    t, d = x2.shape
    tm = min(TM_PROJ, seq)
    per_b = seq // tm
    return pl.pallas_call(
        _inproj_kernel,
        grid=(t // tm, NP_COLS // TN_PROJ),
        in_specs=[pl.BlockSpec((tm, d), lambda i, j: (i, 0)),
                  pl.BlockSpec((1, 6, d), lambda i, j: (i // per_b, 0, 0)),
                  pl.BlockSpec((1, d), lambda i, j: (0, 0)),
                  pl.BlockSpec((TN_PROJ, d), lambda i, j: (j, 0))],
        out_specs=[pl.BlockSpec((tm, TN_PROJ), lambda i, j: (i, j)),
                   pl.BlockSpec((tm, LANES), lambda i, j: (i, 0))],
        out_shape=[jax.ShapeDtypeStruct((t, NP_COLS), MXU_DTYPE),
                   jax.ShapeDtypeStruct((t, LANES), F32)],
        scratch_shapes=[pltpu.VMEM((tm, d), MXU_DTYPE)],
        compiler_params=_cparams(("parallel", "arbitrary")),
        name="inproj",
    )(x2, mod3, g_mix, w_packed)


TK = 256


def _prep_kernel(ckv_ref, misc_ref, gkv_ref, bf_ref, kv_ref, cum_ref):
    s = ckv_ref.shape[1]
    kv_ref[0] = _rms(ckv_ref[0].astype(F32), gkv_ref[...]).astype(kv_ref.dtype)

    r = lax.broadcasted_iota(I32, (LANES, LANES), 0)
    c = lax.broadcasted_iota(I32, (LANES, LANES), 1)
    tri = jnp.where(c <= r, 1.0, 0.0).astype(MXU_DTYPE)
    carry = jnp.zeros((1, LANES), F32)
    for blk in range(s // LANES):
        z = misc_ref[0, blk * LANES:(blk + 1) * LANES, :] + bf_ref[...]
        ls = jnp.minimum(z, 0.0) - jnp.log1p(jnp.exp(-jnp.abs(z)))
        p1 = ls.astype(MXU_DTYPE)
        r1 = ls - p1.astype(F32)
        p2 = r1.astype(MXU_DTYPE)
        p3 = (r1 - p2.astype(F32)).astype(MXU_DTYPE)
        cs = _dot(tri, p1) + _dot(tri, p2) + _dot(tri, p3) + carry
        carry = cs[LANES - 1:LANES, :]
        cum_ref[0, blk * LANES:(blk + 1) * LANES, :] = cs * (HEAD_DIM ** 0.5)


def _prep(proj3, misc3, g_kv, bf_row):
    b, s, _ = proj3.shape
    return pl.pallas_call(
        _prep_kernel,
        grid=(b,),
        in_specs=[pl.BlockSpec((1, s, D_LAT), lambda i: (i, 0, U_CKV * LANES // D_LAT)),
                  pl.BlockSpec((1, s, LANES), lambda i: (i, 0, 0)),
                  pl.BlockSpec((1, D_LAT), lambda i: (0, 0)),
                  pl.BlockSpec((1, LANES), lambda i: (0, 0))],
        out_specs=[pl.BlockSpec((1, s, D_LAT), lambda i: (i, 0, 0)),
                   pl.BlockSpec((1, s, LANES), lambda i: (i, 0, 0))],
        out_shape=[jax.ShapeDtypeStruct((b, s, D_LAT), MXU_DTYPE),
                   jax.ShapeDtypeStruct((b, s, LANES), F32)],
        compiler_params=_cparams(("parallel",)),
        name="prep",
    )(proj3, misc3, g_kv, bf_row)


TQ_A = 256
N_BISECT = 32
LOG2E = 1.4426950408889634
assert TQ_A == TK


def _pair_loop(n, body):
    def pair(j, carry):
        body(2 * j, 2)
        return carry

    lax.fori_loop(0, jnp.right_shift(n, 1), pair, 0)

    @pl.when(jnp.bitwise_and(n, 1) == 1)
    def _():
        body(n - 1, 1)


def _dsa_kernel(qa_ref, qidx_ref, miscq_ref, misck_ref, kv_ref, wuk_ref, wuv_ref, gqa_ref, o_ref,
                sc_scr, bias_scr, logit_scr, qlat_scr, qh_scr, mpart_scr, lpart_scr, oacc_scr,
                *, topk, n_heads):
    i = pl.program_id(1)
    tq = TQ_A
    nk = i + 1
    q0 = i * tq

    wt = miscq_ref[0].T
    wq = wt[MISC_W:MISC_W + N_IDX_HEADS, :] * (D_IDX ** -0.5 * N_IDX_HEADS ** -0.5)
    for h in range(N_IDX_HEADS):
        qh_scr[h] = qidx_ref[0, :, h * D_IDX:(h + 1) * D_IDX]

    def fold(v, op, rows):
        return op(v.reshape(v.shape[0] // rows, rows, tq), axis=0)

    def score_body(kt, carry):
        mx8, mn8 = carry
        for half in range(TK // LANES):
            k0 = pl.multiple_of(kt * TK + half * LANES, LANES)
            kx = misck_ref[0, pl.ds(k0, LANES), :][:, MISC_K:MISC_K + D_IDX].astype(MXU_DTYPE)
            acc = jnp.zeros((LANES, tq), F32)
            for h in range(N_IDX_HEADS):
                d = _dot_nt(kx, qh_scr[h])
                acc = acc + jnp.maximum(d, 0.0) * wq[h:h + 1, :]
            kpos = k0 + lax.broadcasted_iota(I32, (LANES, tq), 0)
            qpos = q0 + lax.broadcasted_iota(I32, (LANES, tq), 1)
            adm = (kpos // CHUNK) <= (qpos // CHUNK)
            sc_scr[pl.ds(k0, LANES), :] = jnp.where(adm, acc, NEG_BIG)
            mx8 = jnp.maximum(mx8, fold(jnp.where(adm, acc, NEG_BIG), jnp.max, 8))
            mn8 = jnp.minimum(mn8, fold(jnp.where(adm, acc, -NEG_BIG), jnp.min, 8))
        return mx8, mn8

    mx8, mn8 = lax.fori_loop(0, nk, score_body,
                             (jnp.full((8, tq), NEG_BIG, F32), jnp.full((8, tq), -NEG_BIG, F32)))

    def bis_body(it, carry):
        lo, hi = carry
        mid = lo + 0.5 * (hi - lo)

        def cnt_body(kt, c32):
            k0 = pl.multiple_of(kt * TK, TK)
            return c32 + fold(jnp.where(sc_scr[pl.ds(k0, TK), :] >= mid, 1.0, 0.0), jnp.sum, 32)

        cnt = jnp.sum(lax.fori_loop(0, nk, cnt_body, jnp.zeros((32, tq), F32)), axis=0, keepdims=True)
        ok = cnt >= topk
        return jnp.where(ok, mid, lo), jnp.where(ok, hi, mid)

    thr, _ = lax.fori_loop(0, N_BISECT, bis_body, (jnp.min(mn8, axis=0, keepdims=True),
                                                   jnp.max(mx8, axis=0, keepdims=True)))

    def bias_body(kt, carry):
        k0 = pl.multiple_of(kt * TK, TK)
        sel_t = sc_scr[pl.ds(k0, TK), :] >= thr
        bias_scr[kt] = jnp.where(sel_t, 0.0, NEG_BIG).T
        return carry

    lax.fori_loop(0, nk, bias_body, 0)

    for h in range(n_heads):
        ql = _dot(qa_ref[0, :, h * HEAD_DIM:(h + 1) * HEAD_DIM], wuk_ref[h])
        ql = _rms(ql, gqa_ref[...]) * (D_LAT ** -0.5 * LOG2E)
        qlat_scr[h * tq:(h + 1) * tq, :] = ql.astype(qlat_scr.dtype)
    mpart_scr[...] = jnp.full(mpart_scr.shape, NEG_BIG, F32)
    lpart_scr[...] = jnp.zeros(lpart_scr.shape, F32)
    oacc_scr[...] = jnp.zeros(oacc_scr.shape, F32)
    slopes2 = [2.0 ** (-8.0 * (h + 1) / n_heads) * LOG2E for h in range(n_heads)]

    def logit_tiles(kt, nt, last):
        k0 = pl.multiple_of(kt * TK, TK)
        kvt = kv_ref[0, pl.ds(k0, nt * TK), :]
        kcol = (k0 + lax.broadcasted_iota(I32, (1, nt * TK), 1)).astype(F32)
        if last:
            ahead = jnp.maximum(lax.broadcasted_iota(I32, (tq, TK), 1)
                                - lax.broadcasted_iota(I32, (tq, TK), 0), 0).astype(F32)
        for h in range(n_heads):
            rows = slice(h * tq, (h + 1) * tq)
            lg = _dot_nt(qlat_scr[rows, :], kvt) + slopes2[h] * kcol
            for t in range(nt):
                lt = lg[:, t * TK:(t + 1) * TK] + bias_scr[kt + t]
                if last:
                    lt = lt - (2.0 * slopes2[h]) * ahead
                logit_scr[kt + t, rows, :] = lt
                mpart_scr[rows, :] = jnp.maximum(mpart_scr[rows, :],
                                                 jnp.maximum(lt[:, :LANES], lt[:, LANES:]))

    _pair_loop(i, lambda kt, nt: logit_tiles(kt, nt, False))
    logit_tiles(i, 1, True)

    m = jnp.max(mpart_scr[...], axis=-1, keepdims=True)
    mpart_scr[...] = jnp.broadcast_to(m, mpart_scr.shape)

    def pv_tiles(kt, nt):
        k0 = pl.multiple_of(kt * TK, TK)
        kvt = kv_ref[0, pl.ds(k0, nt * TK), :]
        for h in range(n_heads):
            rows = slice(h * tq, (h + 1) * tq)
            mb = mpart_scr[rows, :]
            mb2 = jnp.concatenate([mb, mb], axis=1)
            ps = [jnp.exp2(logit_scr[kt + t, rows, :] - mb2) for t in range(nt)]
            lsum = ps[0][:, :LANES] + ps[0][:, LANES:]
            for p in ps[1:]:
                lsum = lsum + p[:, :LANES] + p[:, LANES:]
            lpart_scr[rows, :] += lsum
            p_all = ps[0] if nt == 1 else jnp.concatenate(ps, axis=1)
            oacc_scr[rows, :] += _dot(p_all.astype(MXU_DTYPE), kvt)

    _pair_loop(nk, pv_tiles)

    for h in range(n_heads):
        rows = slice(h * tq, (h + 1) * tq)
        lsum = jnp.sum(lpart_scr[rows, :], axis=-1, keepdims=True)
        o_lat = oacc_scr[rows, :] / lsum
        o_ref[0, :, h * HEAD_DIM:(h + 1) * HEAD_DIM] = _dot(
            o_lat.astype(MXU_DTYPE), wuv_ref[h]).astype(o_ref.dtype)


def _dsa(proj3, misc3, kv, w_uk, w_uv, g_qa):
    b, s, _ = proj3.shape
    n_heads = w_uk.shape[0]
    width = n_heads * HEAD_DIM
    topk = min(TOPK_MAX, s // 4)
    tq = TQ_A
    nkt = s // TK
    kern = functools.partial(_dsa_kernel, topk=topk, n_heads=n_heads)
    return pl.pallas_call(
        kern,
        grid=(b, s // tq),
        in_specs=[pl.BlockSpec((1, tq, width), lambda bb, i: (bb, i, U_QA * LANES // width)),
                  pl.BlockSpec((1, tq, N_IDX_HEADS * D_IDX),
                               lambda bb, i: (bb, i, U_QIDX * LANES // (N_IDX_HEADS * D_IDX))),
                  pl.BlockSpec((1, tq, LANES), lambda bb, i: (bb, i, 0)),
                  pl.BlockSpec((1, s, LANES), lambda bb, i: (bb, 0, 0)),
                  pl.BlockSpec((1, s, D_LAT), lambda bb, i: (bb, 0, 0)),
                  pl.BlockSpec((n_heads, HEAD_DIM, D_LAT), lambda bb, i: (0, 0, 0)),
                  pl.BlockSpec((n_heads, D_LAT, HEAD_DIM), lambda bb, i: (0, 0, 0)),
                  pl.BlockSpec((1, D_LAT), lambda bb, i: (0, 0))],
        out_specs=pl.BlockSpec((1, tq, width), lambda bb, i: (bb, i, 0)),
        out_shape=jax.ShapeDtypeStruct((b, s, width), MXU_DTYPE),
        scratch_shapes=[pltpu.VMEM((s, tq), F32),
                        pltpu.VMEM((nkt, tq, TK), F32),
                        pltpu.VMEM((nkt, n_heads * tq, TK), F32),
                        pltpu.VMEM((n_heads * tq, D_LAT), MXU_DTYPE),
                        pltpu.VMEM((N_IDX_HEADS, tq, D_IDX), MXU_DTYPE),
                        pltpu.VMEM((n_heads * tq, LANES), F32),
                        pltpu.VMEM((n_heads * tq, LANES), F32),
                        pltpu.VMEM((n_heads * tq, D_LAT), F32)],
        compiler_params=_cparams(("parallel", "arbitrary")),
        name="dsa",
    )(proj3, proj3, misc3, misc3, kv, w_uk, w_uv, g_qa)


TQ_B = 256


HG_B = 4
assert TQ_B == TK


def _split3(x):
    p1 = x.astype(MXU_DTYPE)
    r1 = x - p1.astype(F32)
    p2 = r1.astype(MXU_DTYPE)
    p3 = (r1 - p2.astype(F32)).astype(MXU_DTYPE)
    return p1, p2, p3


def _fox_kernel(q_ref, k_ref, v_ref, cumq_ref, cumk_ref, gq_ref, gk_ref, o_ref,
                kaug_scr, qaug_scr, logit_scr, mpart_scr, lpart_scr, oacc_scr):
    g = pl.program_id(1)
    qi = pl.program_id(2)
    tq = TQ_B
    s = k_ref.shape[1]
    cscale = (HEAD_DIM ** -0.5) * LOG2E

    rr = lax.broadcasted_iota(I32, (3 * LANES, LANES), 0)
    cc = lax.broadcasted_iota(I32, (3 * LANES, LANES), 1)
    lane_row = lax.broadcasted_iota(I32, (1, LANES), 1)
    ones_q = jnp.where((lane_row >= 3) & (lane_row < 6), 1.0, 0.0)
    ones_k = jnp.where(lane_row < 3, 1.0, 0.0)
    base_q = jnp.where(cc < 3, rr - LANES * cc, -1)
    base_k = jnp.where((cc >= 3) & (cc < 6), rr - LANES * (cc - 3), -1)

    def aug(pieces, h, base, sign, ones_row):
        e = jnp.where(base == h, sign, 0.0).astype(MXU_DTYPE)
        return (_dot(pieces, e) + ones_row).astype(MXU_DTYPE)

    @pl.when(qi == 0)
    def _():
        def kbody(c, carry):
            r0 = pl.multiple_of(c * TK, TK)
            ck = jnp.concatenate(_split3(cumk_ref[0, pl.ds(r0, TK), :]), axis=1)
            for hh in range(HG_B):
                kn = _rms(k_ref[0, pl.ds(r0, TK), hh * HEAD_DIM:(hh + 1) * HEAD_DIM].astype(F32), gk_ref[...])
                kaug_scr[hh, pl.ds(r0, TK), 0:HEAD_DIM] = kn.astype(MXU_DTYPE)
                kaug_scr[hh, pl.ds(r0, TK), HEAD_DIM:] = aug(ck, MISC_F + g * HG_B + hh, base_k, -1.0, ones_k)
            return carry

        lax.fori_loop(0, s // TK, kbody, 0)

    cq = jnp.concatenate(_split3(cumq_ref[0]), axis=1)
    for hh in range(HG_B):
        qn = _rms(q_ref[0, :, hh * HEAD_DIM:(hh + 1) * HEAD_DIM].astype(F32), gq_ref[...])
        qaug_scr[hh, :, 0:HEAD_DIM] = qn.astype(MXU_DTYPE)
        qaug_scr[hh, :, HEAD_DIM:] = aug(cq, MISC_F + g * HG_B + hh, base_q, 1.0, ones_q)
    mpart_scr[...] = jnp.full(mpart_scr.shape, NEG_BIG, F32)
    lpart_scr[...] = jnp.zeros(lpart_scr.shape, F32)
    oacc_scr[...] = jnp.zeros(oacc_scr.shape, F32)

    def put_logits(kt, hh, lg):
        logit_scr[kt, hh] = lg
        mpart_scr[hh] = jnp.maximum(mpart_scr[hh], jnp.maximum(lg[:, :LANES], lg[:, LANES:]))

    def raw_logits(kt, nt, hh):
        k0 = pl.multiple_of(kt * TK, TK)
        return _dot_nt(qaug_scr[hh], kaug_scr[hh, pl.ds(k0, nt * TK), :]) * cscale

    def off_tiles(kt, nt):
        for hh in range(HG_B):
            lg = raw_logits(kt, nt, hh)
            for t in range(nt):
                put_logits(kt + t, hh, lg[:, t * TK:(t + 1) * TK])

    _pair_loop(qi, off_tiles)
    causal = (lax.broadcasted_iota(I32, (tq, TK), 1) <= lax.broadcasted_iota(I32, (tq, TK), 0))
    for hh in range(HG_B):
        put_logits(qi, hh, jnp.where(causal, raw_logits(qi, 1, hh), NEG_BIG))

    for hh in range(HG_B):
        m = jnp.max(mpart_scr[hh], axis=-1, keepdims=True)
        mpart_scr[hh] = jnp.broadcast_to(m, (tq, LANES))

    def pv_tiles(kt, nt):
        k0 = pl.multiple_of(kt * TK, TK)
        for hh in range(HG_B):
            mb = mpart_scr[hh]
            mb2 = jnp.concatenate([mb, mb], axis=1)
            ps = [jnp.exp2(logit_scr[kt + t, hh] - mb2) for t in range(nt)]
            lsum = ps[0][:, :LANES] + ps[0][:, LANES:]
            for p in ps[1:]:
                lsum = lsum + p[:, :LANES] + p[:, LANES:]
            lpart_scr[hh] += lsum
            p_all = ps[0] if nt == 1 else jnp.concatenate(ps, axis=1)
            oacc_scr[hh] += _dot(p_all.astype(MXU_DTYPE),
                                 v_ref[0, pl.ds(k0, nt * TK), hh * HEAD_DIM:(hh + 1) * HEAD_DIM])

    _pair_loop(qi + 1, pv_tiles)
    for hh in range(HG_B):
        lsum = jnp.sum(lpart_scr[hh], axis=-1, keepdims=True)
        o_ref[0, :, hh * HEAD_DIM:(hh + 1) * HEAD_DIM] = (oacc_scr[hh] / lsum).astype(o_ref.dtype)


def _fox(proj3, cums, g_qb, g_kb, n_heads):
    b, s, _ = proj3.shape
    tq = TQ_B
    gw = HG_B * HEAD_DIM
    return pl.pallas_call(
        _fox_kernel,
        grid=(b, n_heads // HG_B, s // tq),
        in_specs=[pl.BlockSpec((1, tq, gw), lambda bb, g, i: (bb, i, U_QB // HG_B + g)),
                  pl.BlockSpec((1, s, gw), lambda bb, g, i: (bb, 0, U_KB // HG_B + g)),
                  pl.BlockSpec((1, s, gw), lambda bb, g, i: (bb, 0, U_VB // HG_B + g)),
                  pl.BlockSpec((1, tq, LANES), lambda bb, g, i: (bb, i, 0)),
                  pl.BlockSpec((1, s, LANES), lambda bb, g, i: (bb, 0, 0)),
                  pl.BlockSpec((1, HEAD_DIM), lambda bb, g, i: (0, 0)),
                  pl.BlockSpec((1, HEAD_DIM), lambda bb, g, i: (0, 0))],
        out_specs=pl.BlockSpec((1, tq, gw), lambda bb, g, i: (bb, i, g)),
        out_shape=jax.ShapeDtypeStruct((b, s, n_heads * HEAD_DIM), MXU_DTYPE),
        scratch_shapes=[pltpu.VMEM((HG_B, s, 2 * HEAD_DIM), MXU_DTYPE),
                        pltpu.VMEM((HG_B, tq, 2 * HEAD_DIM), MXU_DTYPE),
                        pltpu.VMEM((s // TK, HG_B, tq, TK), F32),
                        pltpu.VMEM((HG_B, tq, LANES), F32),
                        pltpu.VMEM((HG_B, tq, LANES), F32),
                        pltpu.VMEM((HG_B, tq, HEAD_DIM), F32)],
        compiler_params=_cparams(("parallel", "parallel", "arbitrary")),
        name="fox",
    )(proj3, proj3, proj3, cums, cums, g_qb, g_kb)


TM_MERGE = 256


def _merge_kernel(ga_ref, gb_ref, oa_ref, ob_ref, x_ref, mod_ref, bga_ref, bgb_ref, wpa_ref, wpb_ref,
                  wo_ref, gffn_ref, wrh_ref, wrl_ref, br_ref, x1_ref, h2_ref, rl_ref):
    ga = jax.nn.sigmoid(ga_ref[...].astype(F32) + bga_ref[...])
    gb = jax.nn.sigmoid(gb_ref[...].astype(F32) + bgb_ref[...])
    merged = ga * _dot(oa_ref[...], wpa_ref[...]) + gb * _dot(ob_ref[...], wpb_ref[...])
    upd = _dot(merged.astype(MXU_DTYPE), wo_ref[...])
    x1 = x_ref[...] + mod_ref[0, 2:3, :] * upd
    x1_ref[...] = x1
    h2 = _rms(x1, gffn_ref[...]) * (1.0 + mod_ref[0, 4:5, :]) + mod_ref[0, 3:4, :]
    h2_ref[...] = h2
    hh = h2.astype(MXU_DTYPE)
    hl = (h2 - hh.astype(F32)).astype(MXU_DTYPE)
    rl_ref[...] = (_dot(hh, wrh_ref[...]) + _dot(hl, wrh_ref[...]) + _dot(hh, wrl_ref[...])
                   + br_ref[...])


def _merge(proj, o_a, o_b, x2, mod3, b_gate, w_pa, w_pb, w_o, g_ffn, wr_hi, wr_lo, b_r, seq):
    t, d = x2.shape
    tm = TM_MERGE
    per_b = seq // tm
    wa = o_a.shape[1]
    res = lambda shape: pl.BlockSpec(shape, lambda i: (0,) * len(shape), pipeline_mode=pl.Buffered(1))
    return pl.pallas_call(
        _merge_kernel,
        grid=(t // tm,),
        in_specs=[pl.BlockSpec((tm, d), lambda i: (i, 0)),
                  pl.BlockSpec((tm, d), lambda i: (i, 1)),
                  pl.BlockSpec((tm, wa), lambda i: (i, 0)),
                  pl.BlockSpec((tm, wa), lambda i: (i, 0)),
                  pl.BlockSpec((tm, d), lambda i: (i, 0)),
                  pl.BlockSpec((1, 6, d), lambda i: (i // per_b, 0, 0)),
                  pl.BlockSpec((1, d), lambda i: (0, 0)),
                  pl.BlockSpec((1, d), lambda i: (0, 1)),
                  res((wa, d)), res((wa, d)), res((d, d)),
                  pl.BlockSpec((1, d), lambda i: (0, 0)),
                  res((d, LANES)), res((d, LANES)),
                  pl.BlockSpec((1, LANES), lambda i: (0, 0))],
        out_specs=[pl.BlockSpec((tm, d), lambda i: (i, 0)),
                   pl.BlockSpec((tm, d), lambda i: (i, 0)),
                   pl.BlockSpec((tm, LANES), lambda i: (i, 0))],
        out_shape=[jax.ShapeDtypeStruct((t, d), F32),
                   jax.ShapeDtypeStruct((t, d), F32),
                   jax.ShapeDtypeStruct((t, LANES), F32)],
        compiler_params=_cparams(("parallel",)),
        name="merge",
    )(proj, proj, o_a, o_b, x2, mod3, b_gate, b_gate, w_pa, w_pb, w_o, g_ffn, wr_hi, wr_lo, b_r)


TM_ROUTE = 1024
TM_ROWS = 128
R_E0, R_E1, R_P0, R_P1 = 0, 1, 4, 5
IT_TILE, IT_E, IT_LO, IT_HI, IT_FLAG, IT_NEXT, IT_NEXT2, IT_SLOT = range(8)
F_VALID, F_FIRST_OF_EXPERT, F_FIRST_OF_TILE = 1, 2, 4


def _route_kernel(rl_ref, route_ref, post_ref, items_ref, cnt_scr, run_scr, offs_scr, *, n_tiles):
    tm = rl_ref.shape[0]
    sweep = pl.program_id(0)
    step = pl.program_id(1)
    iw = items_ref.shape[1]

    @pl.when((sweep == 0) & (step == 0))
    def _():
        cnt_scr[...] = jnp.zeros(cnt_scr.shape, F32)

    r = rl_ref[...]
    lane = lax.broadcasted_iota(I32, (tm, LANES), 1).astype(F32)
    neg_inf = -jnp.inf
    gmask = lane < N_GROUPS
    gl = jnp.where(gmask, r, neg_inf)
    gmax = jnp.max(gl, axis=-1, keepdims=True)
    gidx = jnp.min(jnp.where(gl == gmax, lane, float(LANES)), axis=-1, keepdims=True)
    gsum = jnp.sum(jnp.where(gmask, jnp.exp(r - gmax), 0.0), axis=-1, keepdims=True)
    gw = 1.0 / gsum
    lo = N_GROUPS + EXPERTS_PER_GROUP * gidx
    emask = (lane >= lo) & (lane < lo + EXPERTS_PER_GROUP)
    el = jnp.where(emask, r, neg_inf)
    v0 = jnp.max(el, axis=-1, keepdims=True)
    i0 = jnp.min(jnp.where(el == v0, lane, float(LANES)), axis=-1, keepdims=True)
    el2 = jnp.where(lane == i0, neg_inf, el)
    v1 = jnp.max(el2, axis=-1, keepdims=True)
    i1 = jnp.min(jnp.where(el2 == v1, lane, float(LANES)), axis=-1, keepdims=True)
    tt = jnp.exp(v1 - v0)
    p0 = gw / (1.0 + tt)
    p1 = gw * tt / (1.0 + tt)
    e0 = i0 - N_GROUPS
    e1 = i1 - N_GROUPS

    hit0 = lane == e0
    hit1 = lane == e1
    oh = jnp.where(hit0 | hit1, 1.0, 0.0)

    @pl.when(sweep == 0)
    def _():
        ones = jnp.ones((tm, LANES), MXU_DTYPE)
        cnt_scr[...] += _dot(oh.T.astype(MXU_DTYPE), ones)

    @pl.when((sweep == 1) & (step == 0))
    def _():
        _plan_items(cnt_scr[...], items_ref, offs_scr, n_tiles, iw)
        run_scr[...] = jnp.zeros(run_scr.shape, F32)

    @pl.when(sweep == 1)
    def _():
        rr = lax.broadcasted_iota(I32, (tm, tm), 0)
        cc = lax.broadcasted_iota(I32, (tm, tm), 1)
        ltri = jnp.where(cc < rr, 1.0, 0.0).astype(MXU_DTYPE)
        before = _dot(ltri, oh.astype(MXU_DTYPE)) + run_scr[...] + offs_scr[...]
        pos0 = jnp.sum(jnp.where(hit0, before, 0.0), axis=-1, keepdims=True)
        pos1 = jnp.sum(jnp.where(hit1, before, 0.0), axis=-1, keepdims=True)
        run_scr[...] = run_scr[...] + jnp.sum(oh, axis=0, keepdims=True)
        out = jnp.zeros((tm, LANES), F32)
        for k, val in ((R_E0, e0), (R_E1, e1), (R_P0, p0), (R_P1, p1)):
            out = jnp.where(lane == k, val, out)
        route_ref[...] = out
        pmat = jnp.where(lane == 0.0, pos0, jnp.where(lane == 1.0, pos1, 0.0))
        post_ref[...] = pmat.T[0:8, :].astype(I32)


def _plan_items(cnt_col, items_ref, offs_scr, n_tiles, iw):
    tmr = float(TM_ROWS)
    sub = lax.broadcasted_iota(I32, (LANES, LANES), 0)
    lan = lax.broadcasted_iota(I32, (LANES, LANES), 1)
    lstrict = jnp.where(lan < sub, 1.0, 0.0).astype(MXU_DTYPE)
    hi = jnp.floor(cnt_col * (1.0 / LANES))
    lo = cnt_col - hi * LANES
    offs_col = _dot(lstrict, hi.astype(MXU_DTYPE)) * LANES + _dot(lstrict, lo.astype(MXU_DTYPE))
    first_t = jnp.floor(offs_col * (1.0 / tmr))
    last_t = jnp.floor((offs_col + cnt_col - 1.0) * (1.0 / tmr))
    n_col = jnp.where(cnt_col > 0.0, last_t - first_t + 1.0, 0.0)
    base_col = _dot(lstrict, n_col.astype(MXU_DTYPE))
    end_col = base_col + n_col
    offs_scr[...] = offs_col.T[0:1, :]

    rep = lambda col: jnp.concatenate([col] * (iw // LANES), axis=1)
    wl = lax.broadcasted_iota(I32, (LANES, iw), 1).astype(F32)
    sub_f = lax.broadcasted_iota(I32, (LANES, iw), 0).astype(F32)
    end_r = rep(end_col)
    w_total = end_r[LANES - 1:LANES, :]
    count_le = lambda v: jnp.sum(jnp.where(end_r <= v, 1.0, 0.0), axis=0, keepdims=True)
    w_row = wl[0:1, :]
    eidx = count_le(w_row)
    sel = sub_f == eidx
    pick = lambda col: jnp.sum(jnp.where(sel, rep(col), 0.0), axis=0, keepdims=True)
    e_base, e_first, e_offs, e_cnt, e_end = (pick(base_col), pick(first_t), pick(offs_col),
                                             pick(cnt_col), pick(end_col))
    valid = w_row < w_total
    tile = e_first + (w_row - e_base)
    row_lo = jnp.maximum(e_offs, tile * tmr) - tile * tmr
    row_hi = jnp.minimum(e_offs + e_cnt, (tile + 1.0) * tmr) - tile * tmr
    first_e = w_row == e_base
    flags = jnp.where(valid, F_VALID + jnp.where(first_e, float(F_FIRST_OF_EXPERT), 0.0)
                      + jnp.where(row_lo == 0.0, float(F_FIRST_OF_TILE), 0.0), 0.0)
    has1 = valid & first_e & (e_end < w_total)
    e1 = count_le(e_end)
    end1 = jnp.sum(jnp.where(sub_f == e1, end_r, 0.0), axis=0, keepdims=True)
    has2 = has1 & (end1 < w_total)
    nxt = jnp.where(has1, e1, -1.0)
    nxt2 = jnp.where(has2, count_le(end1), -1.0)
    ord_col = _dot(lstrict, jnp.where(cnt_col > 0.0, 1.0, 0.0).astype(MXU_DTYPE))
    e_ord = pick(ord_col)
    slot = e_ord - 2.0 * jnp.floor(e_ord * 0.5)
    e_last = count_le(w_total - 1.0)
    rows = {IT_TILE: jnp.where(valid, tile, n_tiles - 1.0),
            IT_E: jnp.where(valid, eidx, e_last),
            IT_LO: jnp.where(valid, row_lo, 0.0),
            IT_HI: jnp.where(valid, row_hi, 0.0),
            IT_FLAG: flags,
            IT_NEXT: nxt,
            IT_NEXT2: nxt2,
            IT_SLOT: jnp.where(valid, slot, 0.0)}
    sub8 = lax.broadcasted_iota(I32, (8, iw), 0)
    out = jnp.zeros((8, iw), F32)
    for k, val in rows.items():
        out = jnp.where(sub8 == k, val, out)
    items_ref[...] = out.astype(I32)


def _route(rlog, n_items):
    t = rlog.shape[0]
    tm = min(TM_ROUTE, t)
    iw = ((n_items + LANES - 1) // LANES) * LANES
    kern = functools.partial(_route_kernel, n_tiles=2 * t // TM_ROWS)
    return pl.pallas_call(
        kern,
        grid=(2, t // tm),
        in_specs=[pl.BlockSpec((tm, LANES), lambda p, i: (i, 0))],
        out_specs=[pl.BlockSpec((tm, LANES), lambda p, i: (i * p, 0)),
                   pl.BlockSpec((8, tm), lambda p, i: (0, i * p)),
                   pl.BlockSpec((8, iw), lambda p, i: (0, 0))],
        out_shape=[jax.ShapeDtypeStruct((t, LANES), F32),
                   jax.ShapeDtypeStruct((8, t), I32),
                   jax.ShapeDtypeStruct((8, iw), I32)],
        scratch_shapes=[pltpu.VMEM((LANES, LANES), F32),
                        pltpu.VMEM((1, LANES), F32),
                        pltpu.VMEM((1, LANES), F32)],
        compiler_params=_cparams(("arbitrary", "arbitrary")),
        name="route",
    )(rlog)


TM_DISP = 256


def _dispatch_kernel(pos0_ref, pos1_ref, h2_ref, xs_ref, sem):
    base = pl.program_id(0) * TM_DISP

    def row_copy(r, p):
        return pltpu.make_async_copy(h2_ref.at[pl.ds(r, 1), :], xs_ref.at[pl.ds(p, 1), :], sem)

    def issue(r, carry):
        row_copy(r, pos0_ref[base + r]).start(priority=0)
        row_copy(r, pos1_ref[base + r]).start(priority=1)
        return carry

    lax.fori_loop(0, TM_DISP, issue, 0, unroll=8)

    def drain(r, carry):
        row_copy(0, 0).wait()
        row_copy(0, 0).wait()
        return carry

    lax.fori_loop(0, TM_DISP, drain, 0, unroll=8)


def _dispatch(pos0, pos1, h2):
    t, d = h2.shape
    return pl.pallas_call(
        _dispatch_kernel,
        grid_spec=pltpu.PrefetchScalarGridSpec(
            num_scalar_prefetch=2,
            grid=(t // TM_DISP,),
            in_specs=[pl.BlockSpec((TM_DISP, d), lambda i, p0, p1: (i, 0))],
            out_specs=pl.BlockSpec(memory_space=pl.ANY),
            scratch_shapes=[pltpu.SemaphoreType.DMA(())]),
        out_shape=jax.ShapeDtypeStruct((2 * t, d), h2.dtype),
        compiler_params=pltpu.CompilerParams(dimension_semantics=("arbitrary",),
                                             vmem_limit_bytes=VMEM_LIMIT, has_side_effects=True),
        name="dispatch",
    )(pos0, pos1, h2)


WEIGHT_DMA_PRIORITY = 1


def _experts_kernel(tile_ref, e_ref, lo_ref, hi_ref, flag_ref, next_ref, next2_ref, slot_ref,
                    xs_ref, w1_hbm, w3_hbm, w2_hbm, ys_ref,
                    f1_scr, f3_scr, f2_scr, w1_scr, w3_scr, w2_scr, sems):
    del tile_ref
    w = pl.program_id(0)
    flag = flag_ref[w]

    mats = ((w1_hbm, f1_scr, w1_scr), (w3_hbm, f3_scr, w3_scr), (w2_hbm, f2_scr, w2_scr))

    def weight_copy(k, e, slot):
        return pltpu.make_async_copy(mats[k][0].at[e], mats[k][1].at[slot], sems.at[slot, k])

    @pl.when(w == 0)
    def _():
        for k in range(3):
            weight_copy(k, e_ref[0], 0).start(priority=WEIGHT_DMA_PRIORITY)
        nxt = next_ref[0]

        @pl.when(nxt >= 0)
        def _():
            for k in range(3):
                weight_copy(k, nxt, 1).start(priority=WEIGHT_DMA_PRIORITY)

    @pl.when((flag & F_FIRST_OF_EXPERT) != 0)
    def _():
        slot = slot_ref[w]
        nxt2 = next2_ref[w]
        for k in range(3):
            weight_copy(k, e_ref[w], slot).wait()
            mats[k][2][...] = mats[k][1][slot].astype(mats[k][2].dtype)

            @pl.when(nxt2 >= 0)
            def _():
                weight_copy(k, nxt2, slot).start(priority=WEIGHT_DMA_PRIORITY)

    @pl.when((flag & F_VALID) != 0)
    def _():
        x = xs_ref[...].astype(MXU_DTYPE)
        a = _dot(x, w1_scr[...])
        u = _dot(x, w3_scr[...])
        hm = (a * jax.nn.sigmoid(a)) * u
        res = _dot(hm.astype(MXU_DTYPE), w2_scr[...])
        row = lax.broadcasted_iota(I32, res.shape, 0)
        mine = (row >= lo_ref[w]) & (row < hi_ref[w])

        @pl.when((flag & F_FIRST_OF_TILE) != 0)
        def _():
            ys_ref[...] = jnp.where(mine, res, 0.0)

        @pl.when((flag & F_FIRST_OF_TILE) == 0)
        def _():
            ys_ref[...] = jnp.where(mine, res, ys_ref[...])


def _experts(items, xs, w1, w3, w2, n_items):
    n_rows, d = xs.shape
    f = w1.shape[2]
    tm = TM_ROWS
    tile_map = lambda w, tile, *_: (tile[w], 0)
    return pl.pallas_call(
        _experts_kernel,
        grid_spec=pltpu.PrefetchScalarGridSpec(
            num_scalar_prefetch=8,
            grid=(n_items,),
            in_specs=[pl.BlockSpec((tm, d), tile_map),
                      pl.BlockSpec(memory_space=pl.ANY),
                      pl.BlockSpec(memory_space=pl.ANY),
                      pl.BlockSpec(memory_space=pl.ANY)],
            out_specs=pl.BlockSpec((tm, d), tile_map),
            scratch_shapes=[pltpu.VMEM((2, d, f), F32),
                            pltpu.VMEM((2, d, f), F32),
                            pltpu.VMEM((2, f, d), F32),
                            pltpu.VMEM((d, f), MXU_DTYPE),
                            pltpu.VMEM((d, f), MXU_DTYPE),
                            pltpu.VMEM((f, d), MXU_DTYPE),
                            pltpu.SemaphoreType.DMA((2, 3))]),
        out_shape=jax.ShapeDtypeStruct((n_rows, d), F32),
        compiler_params=_cparams(("arbitrary",)),
        name="experts",
    )(items[IT_TILE], items[IT_E], items[IT_LO], items[IT_HI], items[IT_FLAG], items[IT_NEXT],
      items[IT_NEXT2], items[IT_SLOT], xs, w1, w3, w2)


TM_COMB = 256


def _combine_kernel(pos0_ref, pos1_ref, ys_ref, x1_ref, route_ref, mod_ref, o_ref, y0_scr, y1_scr, sem):
    base = pl.program_id(0) * TM_COMB

    def row_copy(p, dst, r):
        return pltpu.make_async_copy(ys_ref.at[pl.ds(p, 1), :], dst.at[pl.ds(r, 1), :], sem)

    def issue(r, carry):
        row_copy(pos0_ref[base + r], y0_scr, r).start(priority=0)
        row_copy(pos1_ref[base + r], y1_scr, r).start(priority=1)
        return carry

    lax.fori_loop(0, TM_COMB, issue, 0, unroll=8)

    def drain(r, carry):
        row_copy(0, y0_scr, 0).wait()
        row_copy(0, y1_scr, 0).wait()
        return carry

    lax.fori_loop(0, TM_COMB, drain, 0, unroll=8)

    rt = route_ref[...]
    lane = lax.broadcasted_iota(I32, rt.shape, 1)
    p0 = jnp.sum(jnp.where(lane == R_P0, rt, 0.0), axis=-1, keepdims=True)
    p1 = jnp.sum(jnp.where(lane == R_P1, rt, 0.0), axis=-1, keepdims=True)
    y = p0 * y0_scr[...] + p1 * y1_scr[...]
    o_ref[...] = x1_ref[...] + mod_ref[0, 5:6, :] * y


def _combine(pos0, pos1, ys, x1, route, mod3, seq):
    t, d = x1.shape
    tm = TM_COMB
    per_b = seq // tm
    return pl.pallas_call(
        _combine_kernel,
        grid_spec=pltpu.PrefetchScalarGridSpec(
            num_scalar_prefetch=2,
            grid=(t // tm,),
            in_specs=[pl.BlockSpec(memory_space=pl.ANY),
                      pl.BlockSpec((tm, d), lambda i, p0, p1: (i, 0)),
                      pl.BlockSpec((tm, LANES), lambda i, p0, p1: (i, 0)),
                      pl.BlockSpec((1, 6, d), lambda i, p0, p1: (i // per_b, 0, 0))],
            out_specs=pl.BlockSpec((tm, d), lambda i, p0, p1: (i, 0)),
            scratch_shapes=[pltpu.VMEM((tm, d), F32),
                            pltpu.VMEM((tm, d), F32),
                            pltpu.SemaphoreType.DMA(())]),
        out_shape=jax.ShapeDtypeStruct((t, d), F32),
        compiler_params=_cparams(("arbitrary",)),
        name="combine",
    )(pos0, pos1, ys, x1, route, mod3)


def kernel(x, c, w_ada, b_ada, g_mix, w_in, b_gate, b_forget, w_uk, w_uv, g_qa, g_kv, g_qb, g_kb,
           w_pa, w_pb, w_o, g_ffn, w_rg, b_rg, w_re, b_re, w1, w3, w2):
    b, s, d = x.shape
    depth = w_ada.shape[0]
    t = b * s
    n_heads_b = b_forget.shape[1]
    n_items = 2 * t // TM_ROWS + N_EXPERTS
    c8 = jnp.zeros((8, d), F32).at[:b].set(c)
    x2 = x.reshape(t, d)
    row = lambda v: v.reshape(1, -1)
    for l in range(depth):
        mod3 = _ada(c8, w_ada[l], row(b_ada[l]))[:b].reshape(b, 6, d)
        proj, misc = _inproj(x2, mod3, row(g_mix[l]), _pack_w_in(jnp.transpose(w_in[l])), s)
        proj3 = proj.reshape(b, s, NP_COLS)
        misc3 = misc.reshape(b, s, LANES)
        bf_row = jnp.zeros((1, LANES), F32).at[0, MISC_F:MISC_F + n_heads_b].set(b_forget[l])
        kv, cums = _prep(proj3, misc3, row(g_kv[l]), bf_row)
        o_a = _dsa(proj3, misc3, kv, w_uk[l].astype(MXU_DTYPE), w_uv[l].astype(MXU_DTYPE), row(g_qa[l]))
        o_b = _fox(proj3, cums, row(g_qb[l]), row(g_kb[l]), n_heads_b)

        w_r = jnp.zeros((d, LANES), F32).at[:, :N_GROUPS].set(w_rg[l])
        w_r = w_r.at[:, N_GROUPS:N_GROUPS + N_EXPERTS].set(w_re[l])
        wr_hi = w_r.astype(MXU_DTYPE)
        wr_lo = (w_r - wr_hi.astype(F32)).astype(MXU_DTYPE)
        b_r = jnp.zeros((1, LANES), F32).at[0, :N_GROUPS].set(b_rg[l])
        b_r = b_r.at[0, N_GROUPS:N_GROUPS + N_EXPERTS].set(b_re[l])
        x1, h2, rlog = _merge(proj, o_a.reshape(t, -1), o_b.reshape(t, -1), x2, mod3, row(b_gate[l]),
                              w_pa[l].astype(MXU_DTYPE), w_pb[l].astype(MXU_DTYPE),
                              w_o[l].astype(MXU_DTYPE), row(g_ffn[l]), wr_hi, wr_lo, b_r, s)
        route, pos_t, items = _route(rlog, n_items)
        pos0, pos1 = pos_t[0], pos_t[1]
        xs = _dispatch(pos0, pos1, h2)
        ys = _experts(items, xs, w1[l], w3[l], w2[l], n_items)
        x2 = _combine(pos0, pos1, ys, x1, route, mod3, s)
    return x2.reshape(b, s, d)
```

```python
import functools

import jax
import jax.numpy as jnp
from jax import lax
from jax.experimental import pallas as pl
from jax.experimental.pallas import tpu as pltpu

F32 = jnp.float32
I32 = jnp.int32
MXU_DTYPE = jnp.bfloat16

CHUNK = 64
HEAD_DIM = 128
D_LAT = 256
N_IDX_HEADS = 16
D_IDX = 64
TOPK_MAX = 256
N_GROUPS = 8
EXPERTS_PER_GROUP = 8
N_EXPERTS = N_GROUPS * EXPERTS_PER_GROUP
RMS_EPS = 1e-6

LANES = 128
VMEM_LIMIT = 56 * 1024 * 1024

NEG_BIG = -1e30
INT_MIN = -2147483648


def _cparams(sem):
    return pltpu.CompilerParams(dimension_semantics=sem, vmem_limit_bytes=VMEM_LIMIT)


def _dot(a, b):
    return jnp.dot(a, b, preferred_element_type=F32)


def _dot_nt(a, b):
    return lax.dot_general(a, b, (((1,), (1,)), ((), ())), preferred_element_type=F32)


def _rms(x, g):
    return x * lax.rsqrt(jnp.mean(x * x, axis=-1, keepdims=True) + RMS_EPS) * g


def _ada_kernel(c_ref, w_ref, b_ref, o_ref):
    c = c_ref[...]
    a = c * jax.nn.sigmoid(c)
    o_ref[...] = _dot(a.astype(MXU_DTYPE), w_ref[...].astype(MXU_DTYPE)) + b_ref[...]


def _ada(c8, w_ada, b_ada):
    d, n = w_ada.shape
    tn = 1024
    return pl.pallas_call(
        _ada_kernel,
        grid=(n // tn,),
        in_specs=[pl.BlockSpec((8, d), lambda j: (0, 0)),
                  pl.BlockSpec((d, tn), lambda j: (0, j)),
                  pl.BlockSpec((1, tn), lambda j: (0, j))],
        out_specs=pl.BlockSpec((8, tn), lambda j: (0, j)),
        out_shape=jax.ShapeDtypeStruct((8, n), F32),
        compiler_params=_cparams(("arbitrary",)),
        name="ada",
    )(c8, w_ada, b_ada)


NP_COLS = 76 * LANES
U_QA, U_QIDX, U_QB, U_KB, U_VB, U_CKV, U_MISC = 32, 40, 48, 56, 64, 72, 74
TN_PROJ = 4 * LANES
TM_PROJ = 1024
MISC_TILE = (U_MISC * LANES) // TN_PROJ
MISC_OFF = U_MISC * LANES - MISC_TILE * TN_PROJ
MISC_K, MISC_F, MISC_W = 0, 64, 72


def _pack_moves():
    sizes = (1024, 256, 1024, 64, 16, 1024, 1024, 1024, 8, 4096)
    src = [0]
    for n in sizes:
        src.append(src[-1] + n)
    q_a, c_kv, q_idx, k_idx, w_idx, q_b, k_b, v_b, f_b, gate = src[:10]
    m = U_MISC * LANES
    return ((gate, 0, 4096), (q_a, U_QA * LANES, 1024), (q_idx, U_QIDX * LANES, 1024),
            (q_b, U_QB * LANES, 1024), (k_b, U_KB * LANES, 1024), (v_b, U_VB * LANES, 1024),
            (c_kv, U_CKV * LANES, 256), (f_b, m + MISC_F, 8), (w_idx, m + MISC_W, 16),
            (k_idx, m + MISC_K, 64))


def _pack_kernel(w_ref, o_ref):
    cols = o_ref.shape[1]
    m = U_MISC * LANES
    o_ref[m:m + 2 * LANES, :] = jnp.zeros((2 * LANES, cols), o_ref.dtype)
    for src, dst, n in _pack_moves():
        o_ref[dst:dst + n, :] = w_ref[src:src + n, :].astype(o_ref.dtype)


def _pack_w_in(w_in_t):
    n_in, d = w_in_t.shape
    tc = 256
    return pl.pallas_call(
        _pack_kernel,
        grid=(d // tc,),
        in_specs=[pl.BlockSpec((n_in, tc), lambda i: (0, i))],
        out_specs=pl.BlockSpec((NP_COLS, tc), lambda i: (0, i)),
        out_shape=jax.ShapeDtypeStruct((NP_COLS, d), MXU_DTYPE),
        compiler_params=_cparams(("parallel",)),
        name="pack",
    )(w_in_t)


def _inproj_kernel(x_ref, mod_ref, g_ref, w_ref, o_ref, misc_ref, h_scr):
    j = pl.program_id(1)

    @pl.when(j == 0)
    def _():
        y = _rms(x_ref[...], g_ref[...])
        sh = mod_ref[0, 0:1, :]
        sc = mod_ref[0, 1:2, :]
        h_scr[...] = (y * (1.0 + sc) + sh).astype(h_scr.dtype)

    acc = _dot_nt(h_scr[...], w_ref[...])
    o_ref[...] = acc.astype(o_ref.dtype)

    @pl.when(j == MISC_TILE)
    def _():
        misc_ref[...] = acc[:, MISC_OFF:MISC_OFF + LANES]


def _inproj(x2, mod3, g_mix, w_packed, seq):
    t, d = x2.shape
    tm = min(TM_PROJ, seq)
    per_b = seq // tm
    return pl.pallas_call(
        _inproj_kernel,
        grid=(t // tm, NP_COLS // TN_PROJ),
        in_specs=[pl.BlockSpec((tm, d), lambda i, j: (i, 0)),
                  pl.BlockSpec((1, 6, d), lambda i, j: (i // per_b, 0, 0)),
                  pl.BlockSpec((1, d), lambda i, j: (0, 0)),
                  pl.BlockSpec((TN_PROJ, d), lambda i, j: (j, 0))],
        out_specs=[pl.BlockSpec((tm, TN_PROJ), lambda i, j: (i, j)),
                   pl.BlockSpec((tm, LANES), lambda i, j: (i, 0))],
        out_shape=[jax.ShapeDtypeStruct((t, NP_COLS), MXU_DTYPE),
                   jax.ShapeDtypeStruct((t, LANES), F32)],
        scratch_shapes=[pltpu.VMEM((tm, d), MXU_DTYPE)],
        compiler_params=_cparams(("parallel", "arbitrary")),
        name="inproj",
    )(x2, mod3, g_mix, w_packed)


TK = 256


def _prep_kernel(ckv_ref, misc_ref, gkv_ref, bf_ref, kv_ref, cum_ref):
    s = ckv_ref.shape[1]
    kv_ref[0] = _rms(ckv_ref[0].astype(F32), gkv_ref[...]).astype(kv_ref.dtype)

    r = lax.broadcasted_iota(I32, (LANES, LANES), 0)
    c = lax.broadcasted_iota(I32, (LANES, LANES), 1)
    tri = jnp.where(c <= r, 1.0, 0.0).astype(MXU_DTYPE)
    carry = jnp.zeros((1, LANES), F32)
    for blk in range(s // LANES):
        z = misc_ref[0, blk * LANES:(blk + 1) * LANES, :] + bf_ref[...]
        ls = jnp.minimum(z, 0.0) - jnp.log1p(jnp.exp(-jnp.abs(z)))
        p1 = ls.astype(MXU_DTYPE)
        r1 = ls - p1.astype(F32)
        p2 = r1.astype(MXU_DTYPE)
        p3 = (r1 - p2.astype(F32)).astype(MXU_DTYPE)
        cs = _dot(tri, p1) + _dot(tri, p2) + _dot(tri, p3) + carry
        carry = cs[LANES - 1:LANES, :]
        cum_ref[0, blk * LANES:(blk + 1) * LANES, :] = cs * (HEAD_DIM ** 0.5)


def _prep(proj3, misc3, g_kv, bf_row):
    b, s, _ = proj3.shape
    return pl.pallas_call(
        _prep_kernel,
        grid=(b,),
        in_specs=[pl.BlockSpec((1, s, D_LAT), lambda i: (i, 0, U_CKV * LANES // D_LAT)),
                  pl.BlockSpec((1, s, LANES), lambda i: (i, 0, 0)),
                  pl.BlockSpec((1, D_LAT), lambda i: (0, 0)),
                  pl.BlockSpec((1, LANES), lambda i: (0, 0))],
        out_specs=[pl.BlockSpec((1, s, D_LAT), lambda i: (i, 0, 0)),
                   pl.BlockSpec((1, s, LANES), lambda i: (i, 0, 0))],
        out_shape=[jax.ShapeDtypeStruct((b, s, D_LAT), MXU_DTYPE),
                   jax.ShapeDtypeStruct((b, s, LANES), F32)],
        compiler_params=_cparams(("parallel",)),
        name="prep",
    )(proj3, misc3, g_kv, bf_row)


TQ_A = 256
N_BISECT = 32
LOG2E = 1.4426950408889634
assert TQ_A == TK


def _pair_loop(n, body):
    def pair(j, carry):
        body(2 * j, 2)
        return carry

    lax.fori_loop(0, jnp.right_shift(n, 1), pair, 0)

    @pl.when(jnp.bitwise_and(n, 1) == 1)
    def _():
        body(n - 1, 1)


def _dsa_kernel(qa_ref, qidx_ref, miscq_ref, misck_ref, kv_ref, wuk_ref, wuv_ref, gqa_ref, o_ref,
                sc_scr, bias_scr, logit_scr, qlat_scr, qh_scr, mm_scr, mpart_scr, lpart_scr, oacc_scr,
                *, topk, n_heads):
    i = pl.program_id(1)
    tq = TQ_A
    nk = i + 1
    q0 = i * tq

    wt = miscq_ref[0].T
    wq = wt[MISC_W:MISC_W + N_IDX_HEADS, :] * (D_IDX ** -0.5 * N_IDX_HEADS ** -0.5)
    for h in range(N_IDX_HEADS):
        qh_scr[h] = qidx_ref[0, :, h * D_IDX:(h + 1) * D_IDX]

    def fold(v, op, rows):
        return op(v.reshape(v.shape[0] // rows, rows, tq), axis=0)

    mm_scr[0:8, :] = jnp.full((8, tq), NEG_BIG, F32)
    mm_scr[8:16, :] = jnp.full((8, tq), -NEG_BIG, F32)

    def score_tiles(kt, nt):
        mx8 = mm_scr[0:8, :]
        mn8 = mm_scr[8:16, :]
        for half in range(nt * (TK // LANES)):
            k0 = pl.multiple_of(kt * TK + half * LANES, LANES)
            kx = misck_ref[0, pl.ds(k0, LANES), :][:, MISC_K:MISC_K + D_IDX].astype(MXU_DTYPE)
            acc = jnp.zeros((LANES, tq), F32)
            for h in range(N_IDX_HEADS):
                d = _dot_nt(kx, qh_scr[h])
                acc = acc + jnp.maximum(d, 0.0) * wq[h:h + 1, :]
            kpos = k0 + lax.broadcasted_iota(I32, (LANES, tq), 0)
            qpos = q0 + lax.broadcasted_iota(I32, (LANES, tq), 1)
            adm = (kpos // CHUNK) <= (qpos // CHUNK)
            sc_scr[pl.ds(k0, LANES), :] = jnp.where(adm, acc, NEG_BIG)
            mx8 = jnp.maximum(mx8, fold(jnp.where(adm, acc, NEG_BIG), jnp.max, 8))
            mn8 = jnp.minimum(mn8, fold(jnp.where(adm, acc, -NEG_BIG), jnp.min, 8))
        mm_scr[0:8, :] = mx8
        mm_scr[8:16, :] = mn8

    _pair_loop(nk, score_tiles)

    def bis_body(it, carry):
        lo, hi = carry
        mid = lo + 0.5 * (hi - lo)

        def cnt_body(kt, c32):
            k0 = pl.multiple_of(kt * TK, TK)
            return c32 + fold(jnp.where(sc_scr[pl.ds(k0, TK), :] >= mid, 1.0, 0.0), jnp.sum, 32)

        cnt = jnp.sum(lax.fori_loop(0, nk, cnt_body, jnp.zeros((32, tq), F32)), axis=0, keepdims=True)
        ok = cnt >= topk
        return jnp.where(ok, mid, lo), jnp.where(ok, hi, mid)

    thr, _ = lax.fori_loop(0, N_BISECT, bis_body, (jnp.min(mm_scr[8:16, :], axis=0, keepdims=True),
                                                   jnp.max(mm_scr[0:8, :], axis=0, keepdims=True)))

    def bias_body(kt, carry):
        k0 = pl.multiple_of(kt * TK, TK)
        sel_t = sc_scr[pl.ds(k0, TK), :] >= thr
        bias_scr[kt] = jnp.where(sel_t, 0.0, NEG_BIG).T
        return carry

    lax.fori_loop(0, nk, bias_body, 0)

    for h in range(n_heads):
        ql = _dot(qa_ref[0, :, h * HEAD_DIM:(h + 1) * HEAD_DIM], wuk_ref[h])
        ql = _rms(ql, gqa_ref[...]) * (D_LAT ** -0.5 * LOG2E)
        qlat_scr[h * tq:(h + 1) * tq, :] = ql.astype(qlat_scr.dtype)
    mpart_scr[...] = jnp.full(mpart_scr.shape, NEG_BIG, F32)
    lpart_scr[...] = jnp.zeros(lpart_scr.shape, F32)
    oacc_scr[...] = jnp.zeros(oacc_scr.shape, F32)
    slopes2 = [2.0 ** (-8.0 * (h + 1) / n_heads) * LOG2E for h in range(n_heads)]

    def logit_tiles(kt, nt, last):
        k0 = pl.multiple_of(kt * TK, TK)
        kvt = kv_ref[0, pl.ds(k0, nt * TK), :]
        kcol = (k0 + lax.broadcasted_iota(I32, (1, nt * TK), 1)).astype(F32)
        if last:
            ahead = jnp.maximum(lax.broadcasted_iota(I32, (tq, TK), 1)
                                - lax.broadcasted_iota(I32, (tq, TK), 0), 0).astype(F32)
        for h in range(n_heads):
            rows = slice(h * tq, (h + 1) * tq)
            lg = _dot_nt(qlat_scr[rows, :], kvt) + slopes2[h] * kcol
            for t in range(nt):
                lt = lg[:, t * TK:(t + 1) * TK] + bias_scr[kt + t]
                if last:
                    lt = lt - (2.0 * slopes2[h]) * ahead
                logit_scr[kt + t, rows, :] = lt
                mpart_scr[rows, :] = jnp.maximum(mpart_scr[rows, :],
                                                 jnp.maximum(lt[:, :LANES], lt[:, LANES:]))

    _pair_loop(i, lambda kt, nt: logit_tiles(kt, nt, False))
    logit_tiles(i, 1, True)

    m = jnp.max(mpart_scr[...], axis=-1, keepdims=True)
    mpart_scr[...] = jnp.broadcast_to(m, mpart_scr.shape)

    def pv_tiles(kt, nt):
        k0 = pl.multiple_of(kt * TK, TK)
        kvt = kv_ref[0, pl.ds(k0, nt * TK), :]
        for h in range(n_heads):
            rows = slice(h * tq, (h + 1) * tq)
            mb = mpart_scr[rows, :]
            mb2 = jnp.concatenate([mb, mb], axis=1)
            ps = [jnp.exp2(logit_scr[kt + t, rows, :] - mb2) for t in range(nt)]
            lsum = ps[0][:, :LANES] + ps[0][:, LANES:]
            for p in ps[1:]:
                lsum = lsum + p[:, :LANES] + p[:, LANES:]
            lpart_scr[rows, :] += lsum
            p_all = ps[0] if nt == 1 else jnp.concatenate(ps, axis=1)
            oacc_scr[rows, :] += _dot(p_all.astype(MXU_DTYPE), kvt)

    _pair_loop(nk, pv_tiles)

    for h in range(n_heads):
        rows = slice(h * tq, (h + 1) * tq)
        lsum = jnp.sum(lpart_scr[rows, :], axis=-1, keepdims=True)
        o_lat = oacc_scr[rows, :] / lsum
        o_ref[0, :, h * HEAD_DIM:(h + 1) * HEAD_DIM] = _dot(
            o_lat.astype(MXU_DTYPE), wuv_ref[h]).astype(o_ref.dtype)


def _dsa(proj3, misc3, kv, w_uk, w_uv, g_qa):
    b, s, _ = proj3.shape
    n_heads = w_uk.shape[0]
    width = n_heads * HEAD_DIM
    topk = min(TOPK_MAX, s // 4)
    tq = TQ_A
    nkt = s // TK
    kern = functools.partial(_dsa_kernel, topk=topk, n_heads=n_heads)
    return pl.pallas_call(
        kern,
        grid=(b, s // tq),
        in_specs=[pl.BlockSpec((1, tq, width), lambda bb, i: (bb, i, U_QA * LANES // width)),
                  pl.BlockSpec((1, tq, N_IDX_HEADS * D_IDX),
                               lambda bb, i: (bb, i, U_QIDX * LANES // (N_IDX_HEADS * D_IDX))),
                  pl.BlockSpec((1, tq, LANES), lambda bb, i: (bb, i, 0)),
                  pl.BlockSpec((1, s, LANES), lambda bb, i: (bb, 0, 0)),
                  pl.BlockSpec((1, s, D_LAT), lambda bb, i: (bb, 0, 0)),
                  pl.BlockSpec((n_heads, HEAD_DIM, D_LAT), lambda bb, i: (0, 0, 0)),
                  pl.BlockSpec((n_heads, D_LAT, HEAD_DIM), lambda bb, i: (0, 0, 0)),
                  pl.BlockSpec((1, D_LAT), lambda bb, i: (0, 0))],
        out_specs=pl.BlockSpec((1, tq, width), lambda bb, i: (bb, i, 0)),
        out_shape=jax.ShapeDtypeStruct((b, s, width), MXU_DTYPE),
        scratch_shapes=[pltpu.VMEM((s, tq), F32),
                        pltpu.VMEM((nkt, tq, TK), F32),
                        pltpu.VMEM((nkt, n_heads * tq, TK), F32),
                        pltpu.VMEM((n_heads * tq, D_LAT), MXU_DTYPE),
                        pltpu.VMEM((N_IDX_HEADS, tq, D_IDX), MXU_DTYPE),
                        pltpu.VMEM((16, tq), F32),
                        pltpu.VMEM((n_heads * tq, LANES), F32),
                        pltpu.VMEM((n_heads * tq, LANES), F32),
                        pltpu.VMEM((n_heads * tq, D_LAT), F32)],
        compiler_params=_cparams(("parallel", "arbitrary")),
        name="dsa",
    )(proj3, proj3, misc3, misc3, kv, w_uk, w_uv, g_qa)


TQ_B = 256


HG_B = 8
assert TQ_B == TK


def _split3(x):
    p1 = x.astype(MXU_DTYPE)
    r1 = x - p1.astype(F32)
    p2 = r1.astype(MXU_DTYPE)
    p3 = (r1 - p2.astype(F32)).astype(MXU_DTYPE)
    return p1, p2, p3


def _fox_kernel(q_ref, k_ref, v_ref, cumq_ref, cumk_ref, gq_ref, gk_ref, o_ref,
                kaug_scr, qaug_scr, logit_scr, mpart_scr, lpart_scr, oacc_scr):
    g = pl.program_id(1)
    qi = pl.program_id(2)
    tq = TQ_B
    s = k_ref.shape[1]
    cscale = (HEAD_DIM ** -0.5) * LOG2E

    rr = lax.broadcasted_iota(I32, (3 * LANES, LANES), 0)
    cc = lax.broadcasted_iota(I32, (3 * LANES, LANES), 1)
    lane_row = lax.broadcasted_iota(I32, (1, LANES), 1)
    ones_q = jnp.where((lane_row >= 3) & (lane_row < 6), 1.0, 0.0)
    ones_k = jnp.where(lane_row < 3, 1.0, 0.0)
    base_q = jnp.where(cc < 3, rr - LANES * cc, -1)
    base_k = jnp.where((cc >= 3) & (cc < 6), rr - LANES * (cc - 3), -1)

    def aug(pieces, h, base, sign, ones_row):
        e = jnp.where(base == h, sign, 0.0).astype(MXU_DTYPE)
        return (_dot(pieces, e) + ones_row).astype(MXU_DTYPE)

    @pl.when(qi == 0)
    def _():
        def kbody(c, carry):
            r0 = pl.multiple_of(c * TK, TK)
            ck = jnp.concatenate(_split3(cumk_ref[0, pl.ds(r0, TK), :]), axis=1)
            for hh in range(HG_B):
                kn = _rms(k_ref[0, pl.ds(r0, TK), hh * HEAD_DIM:(hh + 1) * HEAD_DIM].astype(F32), gk_ref[...])
                kaug_scr[hh, pl.ds(r0, TK), 0:HEAD_DIM] = kn.astype(MXU_DTYPE)
                kaug_scr[hh, pl.ds(r0, TK), HEAD_DIM:] = aug(ck, MISC_F + g * HG_B + hh, base_k, -1.0, ones_k)
            return carry

        lax.fori_loop(0, s // TK, kbody, 0)

    cq = jnp.concatenate(_split3(cumq_ref[0]), axis=1)
    for hh in range(HG_B):
        qn = _rms(q_ref[0, :, hh * HEAD_DIM:(hh + 1) * HEAD_DIM].astype(F32), gq_ref[...])
        qaug_scr[hh, :, 0:HEAD_DIM] = qn.astype(MXU_DTYPE)
        qaug_scr[hh, :, HEAD_DIM:] = aug(cq, MISC_F + g * HG_B + hh, base_q, 1.0, ones_q)
    mpart_scr[...] = jnp.full(mpart_scr.shape, NEG_BIG, F32)
    lpart_scr[...] = jnp.zeros(lpart_scr.shape, F32)
    oacc_scr[...] = jnp.zeros(oacc_scr.shape, F32)

    def put_logits(kt, hh, lg):
        logit_scr[kt, hh] = lg
        mpart_scr[hh] = jnp.maximum(mpart_scr[hh], jnp.maximum(lg[:, :LANES], lg[:, LANES:]))

    def raw_logits(kt, nt, hh):
        k0 = pl.multiple_of(kt * TK, TK)
        return _dot_nt(qaug_scr[hh], kaug_scr[hh, pl.ds(k0, nt * TK), :]) * cscale

    def off_tiles(kt, nt):
        for hh in range(HG_B):
            lg = raw_logits(kt, nt, hh)
            for t in range(nt):
                put_logits(kt + t, hh, lg[:, t * TK:(t + 1) * TK])

    _pair_loop(qi, off_tiles)
    causal = (lax.broadcasted_iota(I32, (tq, TK), 1) <= lax.broadcasted_iota(I32, (tq, TK), 0))
    for hh in range(HG_B):
        put_logits(qi, hh, jnp.where(causal, raw_logits(qi, 1, hh), NEG_BIG))

    for hh in range(HG_B):
        m = jnp.max(mpart_scr[hh], axis=-1, keepdims=True)
        mpart_scr[hh] = jnp.broadcast_to(m, (tq, LANES))

    def pv_tiles(kt, nt):
        k0 = pl.multiple_of(kt * TK, TK)
        for hh in range(HG_B):
            mb = mpart_scr[hh]
            mb2 = jnp.concatenate([mb, mb], axis=1)
            ps = [jnp.exp2(logit_scr[kt + t, hh] - mb2) for t in range(nt)]
            lsum = ps[0][:, :LANES] + ps[0][:, LANES:]
            for p in ps[1:]:
                lsum = lsum + p[:, :LANES] + p[:, LANES:]
            lpart_scr[hh] += lsum
            p_all = ps[0] if nt == 1 else jnp.concatenate(ps, axis=1)
            oacc_scr[hh] += _dot(p_all.astype(MXU_DTYPE),
                                 v_ref[0, pl.ds(k0, nt * TK), hh * HEAD_DIM:(hh + 1) * HEAD_DIM])

    _pair_loop(qi + 1, pv_tiles)
    for hh in range(HG_B):
        lsum = jnp.sum(lpart_scr[hh], axis=-1, keepdims=True)
        o_ref[0, :, hh * HEAD_DIM:(hh + 1) * HEAD_DIM] = (oacc_scr[hh] / lsum).astype(o_ref.dtype)


def _fox(proj3, cums, g_qb, g_kb, n_heads):
    b, s, _ = proj3.shape
    tq = TQ_B
    gw = HG_B * HEAD_DIM
    return pl.pallas_call(
        _fox_kernel,
        grid=(b, n_heads // HG_B, s // tq),
        in_specs=[pl.BlockSpec((1, tq, gw), lambda bb, g, i: (bb, i, U_QB // HG_B + g)),
                  pl.BlockSpec((1, s, gw), lambda bb, g, i: (bb, 0, U_KB // HG_B + g)),
                  pl.BlockSpec((1, s, gw), lambda bb, g, i: (bb, 0, U_VB // HG_B + g)),
                  pl.BlockSpec((1, tq, LANES), lambda bb, g, i: (bb, i, 0)),
                  pl.BlockSpec((1, s, LANES), lambda bb, g, i: (bb, 0, 0)),
                  pl.BlockSpec((1, HEAD_DIM), lambda bb, g, i: (0, 0)),
                  pl.BlockSpec((1, HEAD_DIM), lambda bb, g, i: (0, 0))],
        out_specs=pl.BlockSpec((1, tq, gw), lambda bb, g, i: (bb, i, g)),
        out_shape=jax.ShapeDtypeStruct((b, s, n_heads * HEAD_DIM), MXU_DTYPE),
        scratch_shapes=[pltpu.VMEM((HG_B, s, 2 * HEAD_DIM), MXU_DTYPE),
                        pltpu.VMEM((HG_B, tq, 2 * HEAD_DIM), MXU_DTYPE),
                        pltpu.VMEM((s // TK, HG_B, tq, TK), F32),
                        pltpu.VMEM((HG_B, tq, LANES), F32),
                        pltpu.VMEM((HG_B, tq, LANES), F32),
                        pltpu.VMEM((HG_B, tq, HEAD_DIM), F32)],
        compiler_params=_cparams(("parallel", "parallel", "arbitrary")),
        name="fox",
    )(proj3, proj3, proj3, cums, cums, g_qb, g_kb)


TM_MERGE = 256


def _merge_kernel(ga_ref, gb_ref, oa_ref, ob_ref, x_ref, mod_ref, bga_ref, bgb_ref, wpa_ref, wpb_ref,
                  wo_ref, gffn_ref, wrh_ref, wrl_ref, br_ref, x1_ref, h2_ref, rl_ref):
    ga = jax.nn.sigmoid(ga_ref[...].astype(F32) + bga_ref[...])
    gb = jax.nn.sigmoid(gb_ref[...].astype(F32) + bgb_ref[...])
    merged = ga * _dot(oa_ref[...], wpa_ref[...]) + gb * _dot(ob_ref[...], wpb_ref[...])
    upd = _dot(merged.astype(MXU_DTYPE), wo_ref[...])
    x1 = x_ref[...] + mod_ref[0, 2:3, :] * upd
    x1_ref[...] = x1
    h2 = _rms(x1, gffn_ref[...]) * (1.0 + mod_ref[0, 4:5, :]) + mod_ref[0, 3:4, :]
    h2_ref[...] = h2
    hh = h2.astype(MXU_DTYPE)
    hl = (h2 - hh.astype(F32)).astype(MXU_DTYPE)
    rl_ref[...] = (_dot(hh, wrh_ref[...]) + _dot(hl, wrh_ref[...]) + _dot(hh, wrl_ref[...])
                   + br_ref[...])


def _merge(proj, o_a, o_b, x2, mod3, b_gate, w_pa, w_pb, w_o, g_ffn, wr_hi, wr_lo, b_r, seq):
    t, d = x2.shape
    tm = TM_MERGE
    per_b = seq // tm
    wa = o_a.shape[1]
    res = lambda shape: pl.BlockSpec(shape, lambda i: (0,) * len(shape), pipeline_mode=pl.Buffered(1))
    return pl.pallas_call(
        _merge_kernel,
        grid=(t // tm,),
        in_specs=[pl.BlockSpec((tm, d), lambda i: (i, 0)),
                  pl.BlockSpec((tm, d), lambda i: (i, 1)),
                  pl.BlockSpec((tm, wa), lambda i: (i, 0)),
                  pl.BlockSpec((tm, wa), lambda i: (i, 0)),
                  pl.BlockSpec((tm, d), lambda i: (i, 0)),
                  pl.BlockSpec((1, 6, d), lambda i: (i // per_b, 0, 0)),
                  pl.BlockSpec((1, d), lambda i: (0, 0)),
                  pl.BlockSpec((1, d), lambda i: (0, 1)),
                  res((wa, d)), res((wa, d)), res((d, d)),
                  pl.BlockSpec((1, d), lambda i: (0, 0)),
                  res((d, LANES)), res((d, LANES)),
                  pl.BlockSpec((1, LANES), lambda i: (0, 0))],
        out_specs=[pl.BlockSpec((tm, d), lambda i: (i, 0)),
                   pl.BlockSpec((tm, d), lambda i: (i, 0)),
                   pl.BlockSpec((tm, LANES), lambda i: (i, 0))],
        out_shape=[jax.ShapeDtypeStruct((t, d), F32),
                   jax.ShapeDtypeStruct((t, d), F32),
                   jax.ShapeDtypeStruct((t, LANES), F32)],
        compiler_params=_cparams(("parallel",)),
        name="merge",
    )(proj, proj, o_a, o_b, x2, mod3, b_gate, b_gate, w_pa, w_pb, w_o, g_ffn, wr_hi, wr_lo, b_r)


TM_ROUTE = 1024
TM_ROWS = 128
R_E0, R_E1, R_P0, R_P1 = 0, 1, 4, 5
IT_TILE, IT_E, IT_LO, IT_HI, IT_FLAG, IT_NEXT, IT_NEXT2, IT_SLOT = range(8)
F_VALID, F_FIRST_OF_EXPERT, F_FIRST_OF_TILE = 1, 2, 4


def _route_kernel(rl_ref, route_ref, post_ref, items_ref, cnt_scr, run_scr, offs_scr, *, n_tiles):
    tm = rl_ref.shape[0]
    sweep = pl.program_id(0)
    step = pl.program_id(1)
    iw = items_ref.shape[1]

    @pl.when((sweep == 0) & (step == 0))
    def _():
        cnt_scr[...] = jnp.zeros(cnt_scr.shape, F32)

    r = rl_ref[...]
    lane = lax.broadcasted_iota(I32, (tm, LANES), 1).astype(F32)
    neg_inf = -jnp.inf
    gmask = lane < N_GROUPS
    gl = jnp.where(gmask, r, neg_inf)
    gmax = jnp.max(gl, axis=-1, keepdims=True)
    gidx = jnp.min(jnp.where(gl == gmax, lane, float(LANES)), axis=-1, keepdims=True)
    gsum = jnp.sum(jnp.where(gmask, jnp.exp(r - gmax), 0.0), axis=-1, keepdims=True)
    gw = 1.0 / gsum
    lo = N_GROUPS + EXPERTS_PER_GROUP * gidx
    emask = (lane >= lo) & (lane < lo + EXPERTS_PER_GROUP)
    el = jnp.where(emask, r, neg_inf)
    v0 = jnp.max(el, axis=-1, keepdims=True)
    i0 = jnp.min(jnp.where(el == v0, lane, float(LANES)), axis=-1, keepdims=True)
    el2 = jnp.where(lane == i0, neg_inf, el)
    v1 = jnp.max(el2, axis=-1, keepdims=True)
    i1 = jnp.min(jnp.where(el2 == v1, lane, float(LANES)), axis=-1, keepdims=True)
    tt = jnp.exp(v1 - v0)
    p0 = gw / (1.0 + tt)
    p1 = gw * tt / (1.0 + tt)
    e0 = i0 - N_GROUPS
    e1 = i1 - N_GROUPS

    hit0 = lane == e0
    hit1 = lane == e1
    oh = jnp.where(hit0 | hit1, 1.0, 0.0)

    @pl.when(sweep == 0)
    def _():
        ones = jnp.ones((tm, LANES), MXU_DTYPE)
        cnt_scr[...] += _dot(oh.T.astype(MXU_DTYPE), ones)

    @pl.when((sweep == 1) & (step == 0))
    def _():
        _plan_items(cnt_scr[...], items_ref, offs_scr, n_tiles, iw)
        run_scr[...] = jnp.zeros(run_scr.shape, F32)

    @pl.when(sweep == 1)
    def _():
        rr = lax.broadcasted_iota(I32, (tm, tm), 0)
        cc = lax.broadcasted_iota(I32, (tm, tm), 1)
        ltri = jnp.where(cc < rr, 1.0, 0.0).astype(MXU_DTYPE)
        before = _dot(ltri, oh.astype(MXU_DTYPE)) + run_scr[...] + offs_scr[...]
        pos0 = jnp.sum(jnp.where(hit0, before, 0.0), axis=-1, keepdims=True)
        pos1 = jnp.sum(jnp.where(hit1, before, 0.0), axis=-1, keepdims=True)
        run_scr[...] = run_scr[...] + jnp.sum(oh, axis=0, keepdims=True)
        out = jnp.zeros((tm, LANES), F32)
        for k, val in ((R_E0, e0), (R_E1, e1), (R_P0, p0), (R_P1, p1)):
            out = jnp.where(lane == k, val, out)
        route_ref[...] = out
        pmat = jnp.where(lane == 0.0, pos0, jnp.where(lane == 1.0, pos1, 0.0))
        post_ref[...] = pmat.T[0:8, :].astype(I32)


def _plan_items(cnt_col, items_ref, offs_scr, n_tiles, iw):
    tmr = float(TM_ROWS)
    sub = lax.broadcasted_iota(I32, (LANES, LANES), 0)
    lan = lax.broadcasted_iota(I32, (LANES, LANES), 1)
    lstrict = jnp.where(lan < sub, 1.0, 0.0).astype(MXU_DTYPE)
    hi = jnp.floor(cnt_col * (1.0 / LANES))
    lo = cnt_col - hi * LANES
    offs_col = _dot(lstrict, hi.astype(MXU_DTYPE)) * LANES + _dot(lstrict, lo.astype(MXU_DTYPE))
    first_t = jnp.floor(offs_col * (1.0 / tmr))
    last_t = jnp.floor((offs_col + cnt_col - 1.0) * (1.0 / tmr))
    n_col = jnp.where(cnt_col > 0.0, last_t - first_t + 1.0, 0.0)
    base_col = _dot(lstrict, n_col.astype(MXU_DTYPE))
    end_col = base_col + n_col
    offs_scr[...] = offs_col.T[0:1, :]

    rep = lambda col: jnp.concatenate([col] * (iw // LANES), axis=1)
    wl = lax.broadcasted_iota(I32, (LANES, iw), 1).astype(F32)
    sub_f = lax.broadcasted_iota(I32, (LANES, iw), 0).astype(F32)
    end_r = rep(end_col)
    w_total = end_r[LANES - 1:LANES, :]
    count_le = lambda v: jnp.sum(jnp.where(end_r <= v, 1.0, 0.0), axis=0, keepdims=True)
    w_row = wl[0:1, :]
    eidx = count_le(w_row)
    sel = sub_f == eidx
    pick = lambda col: jnp.sum(jnp.where(sel, rep(col), 0.0), axis=0, keepdims=True)
    e_base, e_first, e_offs, e_cnt, e_end = (pick(base_col), pick(first_t), pick(offs_col),
                                             pick(cnt_col), pick(end_col))
    valid = w_row < w_total
    tile = e_first + (w_row - e_base)
    row_lo = jnp.maximum(e_offs, tile * tmr) - tile * tmr
    row_hi = jnp.minimum(e_offs + e_cnt, (tile + 1.0) * tmr) - tile * tmr
    first_e = w_row == e_base
    flags = jnp.where(valid, F_VALID + jnp.where(first_e, float(F_FIRST_OF_EXPERT), 0.0)
                      + jnp.where(row_lo == 0.0, float(F_FIRST_OF_TILE), 0.0), 0.0)
    has1 = valid & first_e & (e_end < w_total)
    e1 = count_le(e_end)
    end1 = jnp.sum(jnp.where(sub_f == e1, end_r, 0.0), axis=0, keepdims=True)
    has2 = has1 & (end1 < w_total)
    nxt = jnp.where(has1, e1, -1.0)
    nxt2 = jnp.where(has2, count_le(end1), -1.0)
    ord_col = _dot(lstrict, jnp.where(cnt_col > 0.0, 1.0, 0.0).astype(MXU_DTYPE))
    e_ord = pick(ord_col)
    slot = e_ord - 2.0 * jnp.floor(e_ord * 0.5)
    e_last = count_le(w_total - 1.0)
    rows = {IT_TILE: jnp.where(valid, tile, n_tiles - 1.0),
            IT_E: jnp.where(valid, eidx, e_last),
            IT_LO: jnp.where(valid, row_lo, 0.0),
            IT_HI: jnp.where(valid, row_hi, 0.0),
            IT_FLAG: flags,
            IT_NEXT: nxt,
            IT_NEXT2: nxt2,
            IT_SLOT: jnp.where(valid, slot, 0.0)}
    sub8 = lax.broadcasted_iota(I32, (8, iw), 0)
    out = jnp.zeros((8, iw), F32)
    for k, val in rows.items():
        out = jnp.where(sub8 == k, val, out)
    items_ref[...] = out.astype(I32)


def _route(rlog, n_items):
    t = rlog.shape[0]
    tm = min(TM_ROUTE, t)
    iw = ((n_items + LANES - 1) // LANES) * LANES
    kern = functools.partial(_route_kernel, n_tiles=2 * t // TM_ROWS)
    return pl.pallas_call(
        kern,
        grid=(2, t // tm),
        in_specs=[pl.BlockSpec((tm, LANES), lambda p, i: (i, 0))],
        out_specs=[pl.BlockSpec((tm, LANES), lambda p, i: (i * p, 0)),
                   pl.BlockSpec((8, tm), lambda p, i: (0, i * p)),
                   pl.BlockSpec((8, iw), lambda p, i: (0, 0))],
        out_shape=[jax.ShapeDtypeStruct((t, LANES), F32),
                   jax.ShapeDtypeStruct((8, t), I32),
                   jax.ShapeDtypeStruct((8, iw), I32)],
        scratch_shapes=[pltpu.VMEM((LANES, LANES), F32),
                        pltpu.VMEM((1, LANES), F32),
                        pltpu.VMEM((1, LANES), F32)],
        compiler_params=_cparams(("arbitrary", "arbitrary")),
        name="route",
    )(rlog)


TM_DISP = 256


def _dispatch_kernel(pos0_ref, pos1_ref, h2_ref, xs_ref, sem):
    base = pl.program_id(0) * TM_DISP

    def row_copy(r, p):
        return pltpu.make_async_copy(h2_ref.at[pl.ds(r, 1), :], xs_ref.at[pl.ds(p, 1), :], sem)

    def issue(r, carry):
        row_copy(r, pos0_ref[base + r]).start(priority=0)
        row_copy(r, pos1_ref[base + r]).start(priority=1)
        return carry

    lax.fori_loop(0, TM_DISP, issue, 0, unroll=8)

    def drain(r, carry):
        row_copy(0, 0).wait()
        row_copy(0, 0).wait()
        return carry

    lax.fori_loop(0, TM_DISP, drain, 0, unroll=8)


def _dispatch(pos0, pos1, h2):
    t, d = h2.shape
    return pl.pallas_call(
        _dispatch_kernel,
        grid_spec=pltpu.PrefetchScalarGridSpec(
            num_scalar_prefetch=2,
            grid=(t // TM_DISP,),
            in_specs=[pl.BlockSpec((TM_DISP, d), lambda i, p0, p1: (i, 0))],
            out_specs=pl.BlockSpec(memory_space=pl.ANY),
            scratch_shapes=[pltpu.SemaphoreType.DMA(())]),
        out_shape=jax.ShapeDtypeStruct((2 * t, d), h2.dtype),
        compiler_params=pltpu.CompilerParams(dimension_semantics=("arbitrary",),
                                             vmem_limit_bytes=VMEM_LIMIT, has_side_effects=True),
        name="dispatch",
    )(pos0, pos1, h2)


WEIGHT_DMA_PRIORITY = 1


def _experts_kernel(tile_ref, e_ref, lo_ref, hi_ref, flag_ref, next_ref, next2_ref, slot_ref,
                    xs_ref, w1_hbm, w3_hbm, w2_hbm, ys_ref,
                    f1_scr, f3_scr, f2_scr, w1_scr, w3_scr, w2_scr, sems):
    del tile_ref
    w = pl.program_id(0)
    flag = flag_ref[w]

    mats = ((w1_hbm, f1_scr, w1_scr), (w3_hbm, f3_scr, w3_scr), (w2_hbm, f2_scr, w2_scr))

    def weight_copy(k, e, slot):
        return pltpu.make_async_copy(mats[k][0].at[e], mats[k][1].at[slot], sems.at[slot, k])

    @pl.when(w == 0)
    def _():
        for k in range(3):
            weight_copy(k, e_ref[0], 0).start(priority=WEIGHT_DMA_PRIORITY)
        nxt = next_ref[0]

        @pl.when(nxt >= 0)
        def _():
            for k in range(3):
                weight_copy(k, nxt, 1).start(priority=WEIGHT_DMA_PRIORITY)

    @pl.when((flag & F_FIRST_OF_EXPERT) != 0)
    def _():
        slot = slot_ref[w]
        nxt2 = next2_ref[w]
        for k in range(3):
            weight_copy(k, e_ref[w], slot).wait()
            mats[k][2][...] = mats[k][1][slot].astype(mats[k][2].dtype)

            @pl.when(nxt2 >= 0)
            def _():
                weight_copy(k, nxt2, slot).start(priority=WEIGHT_DMA_PRIORITY)

    @pl.when((flag & F_VALID) != 0)
    def _():
        x = xs_ref[...].astype(MXU_DTYPE)
        a = _dot(x, w1_scr[...])
        u = _dot(x, w3_scr[...])
        hm = (a * jax.nn.sigmoid(a)) * u
        res = _dot(hm.astype(MXU_DTYPE), w2_scr[...])
        row = lax.broadcasted_iota(I32, res.shape, 0)
        mine = (row >= lo_ref[w]) & (row < hi_ref[w])

        @pl.when((flag & F_FIRST_OF_TILE) != 0)
        def _():
            ys_ref[...] = jnp.where(mine, res, 0.0)

        @pl.when((flag & F_FIRST_OF_TILE) == 0)
        def _():
            ys_ref[...] = jnp.where(mine, res, ys_ref[...])


def _experts(items, xs, w1, w3, w2, n_items):
    n_rows, d = xs.shape
    f = w1.shape[2]
    tm = TM_ROWS
    tile_map = lambda w, tile, *_: (tile[w], 0)
    return pl.pallas_call(
        _experts_kernel,
        grid_spec=pltpu.PrefetchScalarGridSpec(
            num_scalar_prefetch=8,
            grid=(n_items,),
            in_specs=[pl.BlockSpec((tm, d), tile_map),
                      pl.BlockSpec(memory_space=pl.ANY),
                      pl.BlockSpec(memory_space=pl.ANY),
                      pl.BlockSpec(memory_space=pl.ANY)],
            out_specs=pl.BlockSpec((tm, d), tile_map),
            scratch_shapes=[pltpu.VMEM((2, d, f), F32),
                            pltpu.VMEM((2, d, f), F32),
                            pltpu.VMEM((2, f, d), F32),
                            pltpu.VMEM((d, f), MXU_DTYPE),
                            pltpu.VMEM((d, f), MXU_DTYPE),
                            pltpu.VMEM((f, d), MXU_DTYPE),
                            pltpu.SemaphoreType.DMA((2, 3))]),
        out_shape=jax.ShapeDtypeStruct((n_rows, d), F32),
        compiler_params=_cparams(("arbitrary",)),
        name="experts",
    )(items[IT_TILE], items[IT_E], items[IT_LO], items[IT_HI], items[IT_FLAG], items[IT_NEXT],
      items[IT_NEXT2], items[IT_SLOT], xs, w1, w3, w2)


TM_COMB = 256


def _combine_kernel(pos0_ref, pos1_ref, ys_ref, x1_ref, route_ref, mod_ref, o_ref, y0_scr, y1_scr, sems):
    step = pl.program_id(0)
    n_steps = pl.num_programs(0)

    def row_copy(p, dst, slot, r):
        return pltpu.make_async_copy(ys_ref.at[pl.ds(p, 1), :], dst.at[slot, pl.ds(r, 1), :], sems.at[slot])

    def issue_step(s):
        slot = s % 2
        base = s * TM_COMB

        def issue(r, carry):
            row_copy(pos0_ref[base + r], y0_scr, slot, r).start(priority=0)
            row_copy(pos1_ref[base + r], y1_scr, slot, r).start(priority=1)
            return carry

        lax.fori_loop(0, TM_COMB, issue, 0, unroll=8)

    @pl.when(step == 0)
    def _():
        issue_step(step)

    @pl.when(step + 1 < n_steps)
    def _():
        issue_step(step + 1)

    slot = step % 2

    def drain(r, carry):
        row_copy(0, y0_scr, slot, 0).wait()
        row_copy(0, y1_scr, slot, 0).wait()
        return carry

    lax.fori_loop(0, TM_COMB, drain, 0, unroll=8)

    rt = route_ref[...]
    lane = lax.broadcasted_iota(I32, rt.shape, 1)
    p0 = jnp.sum(jnp.where(lane == R_P0, rt, 0.0), axis=-1, keepdims=True)
    p1 = jnp.sum(jnp.where(lane == R_P1, rt, 0.0), axis=-1, keepdims=True)
    y = p0 * y0_scr[slot] + p1 * y1_scr[slot]
    o_ref[...] = x1_ref[...] + mod_ref[0, 5:6, :] * y


def _combine(pos0, pos1, ys, x1, route, mod3, seq):
    t, d = x1.shape
    tm = TM_COMB
    per_b = seq // tm
    return pl.pallas_call(
        _combine_kernel,
        grid_spec=pltpu.PrefetchScalarGridSpec(
            num_scalar_prefetch=2,
            grid=(t // tm,),
            in_specs=[pl.BlockSpec(memory_space=pl.ANY),
                      pl.BlockSpec((tm, d), lambda i, p0, p1: (i, 0)),
                      pl.BlockSpec((tm, LANES), lambda i, p0, p1: (i, 0)),
                      pl.BlockSpec((1, 6, d), lambda i, p0, p1: (i // per_b, 0, 0))],
            out_specs=pl.BlockSpec((tm, d), lambda i, p0, p1: (i, 0)),
            scratch_shapes=[pltpu.VMEM((2, tm, d), F32),
                            pltpu.VMEM((2, tm, d), F32),
                            pltpu.SemaphoreType.DMA((2,))]),
        out_shape=jax.ShapeDtypeStruct((t, d), F32),
        compiler_params=_cparams(("arbitrary",)),
        name="combine",
    )(pos0, pos1, ys, x1, route, mod3)


def kernel(x, c, w_ada, b_ada, g_mix, w_in, b_gate, b_forget, w_uk, w_uv, g_qa, g_kv, g_qb, g_kb,
           w_pa, w_pb, w_o, g_ffn, w_rg, b_rg, w_re, b_re, w1, w3, w2):
    b, s, d = x.shape
    depth = w_ada.shape[0]
    t = b * s
    n_heads_b = b_forget.shape[1]
    n_items = 2 * t // TM_ROWS + N_EXPERTS
    c8 = jnp.zeros((8, d), F32).at[:b].set(c)
    x2 = x.reshape(t, d)
    row = lambda v: v.reshape(1, -1)
    for l in range(depth):
        mod3 = _ada(c8, w_ada[l], row(b_ada[l]))[:b].reshape(b, 6, d)
        proj, misc = _inproj(x2, mod3, row(g_mix[l]), _pack_w_in(jnp.transpose(w_in[l])), s)
        proj3 = proj.reshape(b, s, NP_COLS)
        misc3 = misc.reshape(b, s, LANES)
        bf_row = jnp.zeros((1, LANES), F32).at[0, MISC_F:MISC_F + n_heads_b].set(b_forget[l])
        kv, cums = _prep(proj3, misc3, row(g_kv[l]), bf_row)
        o_a = _dsa(proj3, misc3, kv, w_uk[l].astype(MXU_DTYPE), w_uv[l].astype(MXU_DTYPE), row(g_qa[l]))
        o_b = _fox(proj3, cums, row(g_qb[l]), row(g_kb[l]), n_heads_b)

        w_r = jnp.zeros((d, LANES), F32).at[:, :N_GROUPS].set(w_rg[l])
        w_r = w_r.at[:, N_GROUPS:N_GROUPS + N_EXPERTS].set(w_re[l])
        wr_hi = w_r.astype(MXU_DTYPE)
        wr_lo = (w_r - wr_hi.astype(F32)).astype(MXU_DTYPE)
        b_r = jnp.zeros((1, LANES), F32).at[0, :N_GROUPS].set(b_rg[l])
        b_r = b_r.at[0, N_GROUPS:N_GROUPS + N_EXPERTS].set(b_re[l])
        x1, h2, rlog = _merge(proj, o_a.reshape(t, -1), o_b.reshape(t, -1), x2, mod3, row(b_gate[l]),
                              w_pa[l].astype(MXU_DTYPE), w_pb[l].astype(MXU_DTYPE),
                              w_o[l].astype(MXU_DTYPE), row(g_ffn[l]), wr_hi, wr_lo, b_r, s)
        route, pos_t, items = _route(rlog, n_items)
        pos0, pos1 = pos_t[0], pos_t[1]
        xs = _dispatch(pos0, pos1, h2)
        ys = _experts(items, xs, w1[l], w3[l], w2[l], n_items)
        x2 = _combine(pos0, pos1, ys, x1, route, mod3, s)
    return x2.reshape(b, s, d)
```

```python
import functools

import jax
import jax.numpy as jnp
from jax import lax
from jax.experimental import pallas as pl
from jax.experimental.pallas import tpu as pltpu

F32 = jnp.float32
I32 = jnp.int32
MXU_DTYPE = jnp.bfloat16

CHUNK = 64
HEAD_DIM = 128
D_LAT = 256
N_IDX_HEADS = 16
D_IDX = 64
TOPK_MAX = 256
N_GROUPS = 8
EXPERTS_PER_GROUP = 8
N_EXPERTS = N_GROUPS * EXPERTS_PER_GROUP
RMS_EPS = 1e-6

LANES = 128
VMEM_LIMIT = 56 * 1024 * 1024

NEG_BIG = -1e30
INT_MIN = -2147483648


def _cparams(sem):
    return pltpu.CompilerParams(dimension_semantics=sem, vmem_limit_bytes=VMEM_LIMIT)


def _dot(a, b):
    return jnp.dot(a, b, preferred_element_type=F32)


def _dot_nt(a, b):
    return lax.dot_general(a, b, (((1,), (1,)), ((), ())), preferred_element_type=F32)


def _rms(x, g):
    return x * lax.rsqrt(jnp.mean(x * x, axis=-1, keepdims=True) + RMS_EPS) * g


def _ada_kernel(c_ref, w_ref, b_ref, o_ref):
    c = c_ref[...]
    a = c * jax.nn.sigmoid(c)
    o_ref[...] = _dot(a.astype(MXU_DTYPE), w_ref[...].astype(MXU_DTYPE)) + b_ref[...]


def _ada(c8, w_ada, b_ada):
    d, n = w_ada.shape
    tn = 1024
    return pl.pallas_call(
        _ada_kernel,
        grid=(n // tn,),
        in_specs=[pl.BlockSpec((8, d), lambda j: (0, 0)),
                  pl.BlockSpec((d, tn), lambda j: (0, j)),
                  pl.BlockSpec((1, tn), lambda j: (0, j))],
        out_specs=pl.BlockSpec((8, tn), lambda j: (0, j)),
        out_shape=jax.ShapeDtypeStruct((8, n), F32),
        compiler_params=_cparams(("arbitrary",)),
        name="ada",
    )(c8, w_ada, b_ada)


NP_COLS = 76 * LANES
U_QA, U_QIDX, U_QB, U_KB, U_VB, U_CKV, U_MISC = 32, 40, 48, 56, 64, 72, 74
TN_PROJ = 4 * LANES
TM_PROJ = 1024
MISC_TILE = (U_MISC * LANES) // TN_PROJ
MISC_OFF = U_MISC * LANES - MISC_TILE * TN_PROJ
MISC_K, MISC_F, MISC_W = 0, 64, 72


def _pack_moves():
    sizes = (1024, 256, 1024, 64, 16, 1024, 1024, 1024, 8, 4096)
    src = [0]
    for n in sizes:
        src.append(src[-1] + n)
    q_a, c_kv, q_idx, k_idx, w_idx, q_b, k_b, v_b, f_b, gate = src[:10]
    m = U_MISC * LANES
    return ((gate, 0, 4096), (q_a, U_QA * LANES, 1024), (q_idx, U_QIDX * LANES, 1024),
            (q_b, U_QB * LANES, 1024), (k_b, U_KB * LANES, 1024), (v_b, U_VB * LANES, 1024),
            (c_kv, U_CKV * LANES, 256), (f_b, m + MISC_F, 8), (w_idx, m + MISC_W, 16),
            (k_idx, m + MISC_K, 64))


def _pack_kernel(w_ref, o_ref):
    cols = o_ref.shape[1]
    m = U_MISC * LANES
    o_ref[m:m + 2 * LANES, :] = jnp.zeros((2 * LANES, cols), o_ref.dtype)
    for src, dst, n in _pack_moves():
        o_ref[dst:dst + n, :] = w_ref[src:src + n, :].astype(o_ref.dtype)


def _pack_w_in(w_in_t):
    n_in, d = w_in_t.shape
    tc = 256
    return pl.pallas_call(
        _pack_kernel,
        grid=(d // tc,),
        in_specs=[pl.BlockSpec((n_in, tc), lambda i: (0, i))],
        out_specs=pl.BlockSpec((NP_COLS, tc), lambda i: (0, i)),
        out_shape=jax.ShapeDtypeStruct((NP_COLS, d), MXU_DTYPE),
        compiler_params=_cparams(("parallel",)),
        name="pack",
    )(w_in_t)


def _inproj_kernel(x_ref, mod_ref, g_ref, w_ref, o_ref, misc_ref, h_scr):
    j = pl.program_id(1)

    @pl.when(j == 0)
    def _():
        y = _rms(x_ref[...], g_ref[...])
        sh = mod_ref[0, 0:1, :]
        sc = mod_ref[0, 1:2, :]
        h_scr[...] = (y * (1.0 + sc) + sh).astype(h_scr.dtype)

    acc = _dot_nt(h_scr[...], w_ref[...])
    o_ref[...] = acc.astype(o_ref.dtype)

    @pl.when(j == MISC_TILE)
    def _():
        misc_ref[...] = acc[:, MISC_OFF:MISC_OFF + LANES]


def _inproj(x2, mod3, g_mix, w_packed, seq):
    t, d = x2.shape
    tm = min(TM_PROJ, seq)
    per_b = seq // tm
    return pl.pallas_call(
        _inproj_kernel,
        grid=(t // tm, NP_COLS // TN_PROJ),
        in_specs=[pl.BlockSpec((tm, d), lambda i, j: (i, 0)),
                  pl.BlockSpec((1, 6, d), lambda i, j: (i // per_b, 0, 0)),
                  pl.BlockSpec((1, d), lambda i, j: (0, 0)),
                  pl.BlockSpec((TN_PROJ, d), lambda i, j: (j, 0))],
        out_specs=[pl.BlockSpec((tm, TN_PROJ), lambda i, j: (i, j)),
                   pl.BlockSpec((tm, LANES), lambda i, j: (i, 0))],
        out_shape=[jax.ShapeDtypeStruct((t, NP_COLS), MXU_DTYPE),
                   jax.ShapeDtypeStruct((t, LANES), F32)],
        scratch_shapes=[pltpu.VMEM((tm, d), MXU_DTYPE)],
        compiler_params=_cparams(("parallel", "arbitrary")),
        name="inproj",
    )(x2, mod3, g_mix, w_packed)


TK = 256


def _prep_kernel(ckv_ref, misc_ref, gkv_ref, bf_ref, kv_ref, cum_ref):
    s = ckv_ref.shape[1]
    kv_ref[0] = _rms(ckv_ref[0].astype(F32), gkv_ref[...]).astype(kv_ref.dtype)

    r = lax.broadcasted_iota(I32, (LANES, LANES), 0)
    c = lax.broadcasted_iota(I32, (LANES, LANES), 1)
    tri = jnp.where(c <= r, 1.0, 0.0).astype(MXU_DTYPE)
    carry = jnp.zeros((1, LANES), F32)
    for blk in range(s // LANES):
        z = misc_ref[0, blk * LANES:(blk + 1) * LANES, :] + bf_ref[...]
        ls = jnp.minimum(z, 0.0) - jnp.log1p(jnp.exp(-jnp.abs(z)))
        p1 = ls.astype(MXU_DTYPE)
        r1 = ls - p1.astype(F32)
        p2 = r1.astype(MXU_DTYPE)
        p3 = (r1 - p2.astype(F32)).astype(MXU_DTYPE)
        cs = _dot(tri, p1) + _dot(tri, p2) + _dot(tri, p3) + carry
        carry = cs[LANES - 1:LANES, :]
        cum_ref[0, blk * LANES:(blk + 1) * LANES, :] = cs * (HEAD_DIM ** 0.5)


def _prep(proj3, misc3, g_kv, bf_row):
    b, s, _ = proj3.shape
    return pl.pallas_call(
        _prep_kernel,
        grid=(b,),
        in_specs=[pl.BlockSpec((1, s, D_LAT), lambda i: (i, 0, U_CKV * LANES // D_LAT)),
                  pl.BlockSpec((1, s, LANES), lambda i: (i, 0, 0)),
                  pl.BlockSpec((1, D_LAT), lambda i: (0, 0)),
                  pl.BlockSpec((1, LANES), lambda i: (0, 0))],
        out_specs=[pl.BlockSpec((1, s, D_LAT), lambda i: (i, 0, 0)),
                   pl.BlockSpec((1, s, LANES), lambda i: (i, 0, 0))],
        out_shape=[jax.ShapeDtypeStruct((b, s, D_LAT), MXU_DTYPE),
                   jax.ShapeDtypeStruct((b, s, LANES), F32)],
        compiler_params=_cparams(("parallel",)),
        name="prep",
    )(proj3, misc3, g_kv, bf_row)


TQ_A = 256
N_BISECT = 32
LOG2E = 1.4426950408889634
assert TQ_A == TK


def _pair_loop(n, body):
    def pair(j, carry):
        body(2 * j, 2)
        return carry

    lax.fori_loop(0, jnp.right_shift(n, 1), pair, 0)

    @pl.when(jnp.bitwise_and(n, 1) == 1)
    def _():
        body(n - 1, 1)


def _dsa_kernel(qa_ref, qidx_ref, miscq_ref, misck_ref, kv_ref, wuk_ref, wuv_ref, gqa_ref, o_ref,
                sc_scr, bias_scr, logit_scr, qlat_scr, qh_scr, mm_scr, mpart_scr, lpart_scr, oacc_scr,
                *, topk, n_heads):
    i = pl.program_id(1)
    tq = TQ_A
    nk = i + 1
    q0 = i * tq

    wt = miscq_ref[0].T
    wq = wt[MISC_W:MISC_W + N_IDX_HEADS, :] * (D_IDX ** -0.5 * N_IDX_HEADS ** -0.5)
    for h in range(N_IDX_HEADS):
        qh_scr[h] = qidx_ref[0, :, h * D_IDX:(h + 1) * D_IDX]

    def fold(v, op, rows):
        return op(v.reshape(v.shape[0] // rows, rows, tq), axis=0)

    mm_scr[0:8, :] = jnp.full((8, tq), NEG_BIG, F32)
    mm_scr[8:16, :] = jnp.full((8, tq), -NEG_BIG, F32)

    def score_tiles(kt, nt):
        mx8 = mm_scr[0:8, :]
        mn8 = mm_scr[8:16, :]
        for half in range(nt * (TK // LANES)):
            k0 = pl.multiple_of(kt * TK + half * LANES, LANES)
            kx = misck_ref[0, pl.ds(k0, LANES), :][:, MISC_K:MISC_K + D_IDX].astype(MXU_DTYPE)
            acc = jnp.zeros((LANES, tq), F32)
            for h in range(N_IDX_HEADS):
                d = _dot_nt(kx, qh_scr[h])
                acc = acc + jnp.maximum(d, 0.0) * wq[h:h + 1, :]
            kpos = k0 + lax.broadcasted_iota(I32, (LANES, tq), 0)
            qpos = q0 + lax.broadcasted_iota(I32, (LANES, tq), 1)
            adm = (kpos // CHUNK) <= (qpos // CHUNK)
            sc_scr[pl.ds(k0, LANES), :] = jnp.where(adm, acc, NEG_BIG)
            mx8 = jnp.maximum(mx8, fold(jnp.where(adm, acc, NEG_BIG), jnp.max, 8))
            mn8 = jnp.minimum(mn8, fold(jnp.where(adm, acc, -NEG_BIG), jnp.min, 8))
        mm_scr[0:8, :] = mx8
        mm_scr[8:16, :] = mn8

    _pair_loop(nk, score_tiles)

    def bis_body(it, carry):
        lo, hi = carry
        mid = lo + 0.5 * (hi - lo)

        def cnt_body(kt, c32):
            k0 = pl.multiple_of(kt * TK, TK)
            return c32 + fold(jnp.where(sc_scr[pl.ds(k0, TK), :] >= mid, 1.0, 0.0), jnp.sum, 32)

        cnt = jnp.sum(lax.fori_loop(0, nk, cnt_body, jnp.zeros((32, tq), F32)), axis=0, keepdims=True)
        ok = cnt >= topk
        return jnp.where(ok, mid, lo), jnp.where(ok, hi, mid)

    thr, _ = lax.fori_loop(0, N_BISECT, bis_body, (jnp.min(mm_scr[8:16, :], axis=0, keepdims=True),
                                                   jnp.max(mm_scr[0:8, :], axis=0, keepdims=True)))

    def bias_body(kt, carry):
        k0 = pl.multiple_of(kt * TK, TK)
        sel_t = sc_scr[pl.ds(k0, TK), :] >= thr
        bias_scr[kt] = jnp.where(sel_t, 0.0, NEG_BIG).T
        return carry

    lax.fori_loop(0, nk, bias_body, 0)

    for h in range(n_heads):
        ql = _dot(qa_ref[0, :, h * HEAD_DIM:(h + 1) * HEAD_DIM], wuk_ref[h])
        ql = _rms(ql, gqa_ref[...]) * (D_LAT ** -0.5 * LOG2E)
        qlat_scr[h * tq:(h + 1) * tq, :] = ql.astype(qlat_scr.dtype)
    mpart_scr[...] = jnp.full(mpart_scr.shape, NEG_BIG, F32)
    lpart_scr[...] = jnp.zeros(lpart_scr.shape, F32)
    oacc_scr[...] = jnp.zeros(oacc_scr.shape, F32)
    slopes2 = [2.0 ** (-8.0 * (h + 1) / n_heads) * LOG2E for h in range(n_heads)]

    def logit_tiles(kt, nt, last):
        k0 = pl.multiple_of(kt * TK, TK)
        kvt = kv_ref[0, pl.ds(k0, nt * TK), :]
        kcol = (k0 + lax.broadcasted_iota(I32, (1, nt * TK), 1)).astype(F32)
        if last:
            ahead = jnp.maximum(lax.broadcasted_iota(I32, (tq, TK), 1)
                                - lax.broadcasted_iota(I32, (tq, TK), 0), 0).astype(F32)
        for h in range(n_heads):
            rows = slice(h * tq, (h + 1) * tq)
            lg = _dot_nt(qlat_scr[rows, :], kvt) + slopes2[h] * kcol
            for t in range(nt):
                lt = lg[:, t * TK:(t + 1) * TK] + bias_scr[kt + t]
                if last:
                    lt = lt - (2.0 * slopes2[h]) * ahead
                logit_scr[kt + t, rows, :] = lt
                mpart_scr[rows, :] = jnp.maximum(mpart_scr[rows, :],
                                                 jnp.maximum(lt[:, :LANES], lt[:, LANES:]))

    _pair_loop(i, lambda kt, nt: logit_tiles(kt, nt, False))
    logit_tiles(i, 1, True)

    m = jnp.max(mpart_scr[...], axis=-1, keepdims=True)
    mpart_scr[...] = jnp.broadcast_to(m, mpart_scr.shape)

    def pv_tiles(kt, nt):
        k0 = pl.multiple_of(kt * TK, TK)
        kvt = kv_ref[0, pl.ds(k0, nt * TK), :]
        for h in range(n_heads):
            rows = slice(h * tq, (h + 1) * tq)
            mb = mpart_scr[rows, :]
            mb2 = jnp.concatenate([mb, mb], axis=1)
            ps = [jnp.exp2(logit_scr[kt + t, rows, :] - mb2) for t in range(nt)]
            lsum = ps[0][:, :LANES] + ps[0][:, LANES:]
            for p in ps[1:]:
                lsum = lsum + p[:, :LANES] + p[:, LANES:]
            lpart_scr[rows, :] += lsum
            p_all = ps[0] if nt == 1 else jnp.concatenate(ps, axis=1)
            oacc_scr[rows, :] += _dot(p_all.astype(MXU_DTYPE), kvt)

    _pair_loop(nk, pv_tiles)

    for h in range(n_heads):
        rows = slice(h * tq, (h + 1) * tq)
        lsum = jnp.sum(lpart_scr[rows, :], axis=-1, keepdims=True)
        o_lat = oacc_scr[rows, :] / lsum
        o_ref[0, :, h * HEAD_DIM:(h + 1) * HEAD_DIM] = _dot(
            o_lat.astype(MXU_DTYPE), wuv_ref[h]).astype(o_ref.dtype)


def _dsa(proj3, misc3, kv, w_uk, w_uv, g_qa):
    b, s, _ = proj3.shape
    n_heads = w_uk.shape[0]
    width = n_heads * HEAD_DIM
    topk = min(TOPK_MAX, s // 4)
    tq = TQ_A
    nkt = s // TK
    kern = functools.partial(_dsa_kernel, topk=topk, n_heads=n_heads)
    return pl.pallas_call(
        kern,
        grid=(b, s // tq),
        in_specs=[pl.BlockSpec((1, tq, width), lambda bb, i: (bb, i, U_QA * LANES // width)),
                  pl.BlockSpec((1, tq, N_IDX_HEADS * D_IDX),
                               lambda bb, i: (bb, i, U_QIDX * LANES // (N_IDX_HEADS * D_IDX))),
                  pl.BlockSpec((1, tq, LANES), lambda bb, i: (bb, i, 0)),
                  pl.BlockSpec((1, s, LANES), lambda bb, i: (bb, 0, 0)),
                  pl.BlockSpec((1, s, D_LAT), lambda bb, i: (bb, 0, 0)),
                  pl.BlockSpec((n_heads, HEAD_DIM, D_LAT), lambda bb, i: (0, 0, 0)),
                  pl.BlockSpec((n_heads, D_LAT, HEAD_DIM), lambda bb, i: (0, 0, 0)),
                  pl.BlockSpec((1, D_LAT), lambda bb, i: (0, 0))],
        out_specs=pl.BlockSpec((1, tq, width), lambda bb, i: (bb, i, 0)),
        out_shape=jax.ShapeDtypeStruct((b, s, width), MXU_DTYPE),
        scratch_shapes=[pltpu.VMEM((s, tq), F32),
                        pltpu.VMEM((nkt, tq, TK), F32),
                        pltpu.VMEM((nkt, n_heads * tq, TK), F32),
                        pltpu.VMEM((n_heads * tq, D_LAT), MXU_DTYPE),
                        pltpu.VMEM((N_IDX_HEADS, tq, D_IDX), MXU_DTYPE),
                        pltpu.VMEM((16, tq), F32),
                        pltpu.VMEM((n_heads * tq, LANES), F32),
                        pltpu.VMEM((n_heads * tq, LANES), F32),
                        pltpu.VMEM((n_heads * tq, D_LAT), F32)],
        compiler_params=_cparams(("parallel", "arbitrary")),
        name="dsa",
    )(proj3, proj3, misc3, misc3, kv, w_uk, w_uv, g_qa)


TQ_B = 256


HG_B = 8
assert TQ_B == TK


def _split3(x):
    p1 = x.astype(MXU_DTYPE)
    r1 = x - p1.astype(F32)
    p2 = r1.astype(MXU_DTYPE)
    p3 = (r1 - p2.astype(F32)).astype(MXU_DTYPE)
    return p1, p2, p3


def _fox_kernel(q_ref, k_ref, v_ref, cumq_ref, cumk_ref, gq_ref, gk_ref, o_ref,
                kaug_scr, qaug_scr, logit_scr, mpart_scr, lpart_scr, oacc_scr):
    g = pl.program_id(1)
    qi = pl.program_id(2)
    tq = TQ_B
    s = k_ref.shape[1]
    cscale = (HEAD_DIM ** -0.5) * LOG2E

    rr = lax.broadcasted_iota(I32, (3 * LANES, LANES), 0)
    cc = lax.broadcasted_iota(I32, (3 * LANES, LANES), 1)
    lane_row = lax.broadcasted_iota(I32, (1, LANES), 1)
    ones_q = jnp.where((lane_row >= 3) & (lane_row < 6), 1.0, 0.0)
    ones_k = jnp.where(lane_row < 3, 1.0, 0.0)
    base_q = jnp.where(cc < 3, rr - LANES * cc, -1)
    base_k = jnp.where((cc >= 3) & (cc < 6), rr - LANES * (cc - 3), -1)

    def aug(pieces, h, base, sign, ones_row):
        e = jnp.where(base == h, sign, 0.0).astype(MXU_DTYPE)
        return (_dot(pieces, e) + ones_row).astype(MXU_DTYPE)

    @pl.when(qi == 0)
    def _():
        def kbody(c, carry):
            r0 = pl.multiple_of(c * TK, TK)
            ck = jnp.concatenate(_split3(cumk_ref[0, pl.ds(r0, TK), :]), axis=1)
            for hh in range(HG_B):
                kn = _rms(k_ref[0, pl.ds(r0, TK), hh * HEAD_DIM:(hh + 1) * HEAD_DIM].astype(F32), gk_ref[...])
                kaug_scr[hh, pl.ds(r0, TK), 0:HEAD_DIM] = kn.astype(MXU_DTYPE)
                kaug_scr[hh, pl.ds(r0, TK), HEAD_DIM:] = aug(ck, MISC_F + g * HG_B + hh, base_k, -1.0, ones_k)
            return carry

        lax.fori_loop(0, s // TK, kbody, 0)

    cq = jnp.concatenate(_split3(cumq_ref[0]), axis=1)
    for hh in range(HG_B):
        qn = _rms(q_ref[0, :, hh * HEAD_DIM:(hh + 1) * HEAD_DIM].astype(F32), gq_ref[...])
        qaug_scr[hh, :, 0:HEAD_DIM] = qn.astype(MXU_DTYPE)
        qaug_scr[hh, :, HEAD_DIM:] = aug(cq, MISC_F + g * HG_B + hh, base_q, 1.0, ones_q)
    mpart_scr[...] = jnp.full(mpart_scr.shape, NEG_BIG, F32)
    lpart_scr[...] = jnp.zeros(lpart_scr.shape, F32)
    oacc_scr[...] = jnp.zeros(oacc_scr.shape, F32)

    def put_logits(kt, hh, lg):
        logit_scr[kt, hh] = lg
        mpart_scr[hh] = jnp.maximum(mpart_scr[hh], jnp.maximum(lg[:, :LANES], lg[:, LANES:]))

    def raw_logits(kt, nt, hh):
        k0 = pl.multiple_of(kt * TK, TK)
        return _dot_nt(qaug_scr[hh], kaug_scr[hh, pl.ds(k0, nt * TK), :]) * cscale

    def off_tiles(kt, nt):
        for hh in range(HG_B):
            lg = raw_logits(kt, nt, hh)
            for t in range(nt):
                put_logits(kt + t, hh, lg[:, t * TK:(t + 1) * TK])

    _pair_loop(qi, off_tiles)
    causal = (lax.broadcasted_iota(I32, (tq, TK), 1) <= lax.broadcasted_iota(I32, (tq, TK), 0))
    for hh in range(HG_B):
        put_logits(qi, hh, jnp.where(causal, raw_logits(qi, 1, hh), NEG_BIG))

    for hh in range(HG_B):
        m = jnp.max(mpart_scr[hh], axis=-1, keepdims=True)
        mpart_scr[hh] = jnp.broadcast_to(m, (tq, LANES))

    def pv_tiles(kt, nt):
        k0 = pl.multiple_of(kt * TK, TK)
        for hh in range(HG_B):
            mb = mpart_scr[hh]
            mb2 = jnp.concatenate([mb, mb], axis=1)
            ps = [jnp.exp2(logit_scr[kt + t, hh] - mb2) for t in range(nt)]
            lsum = ps[0][:, :LANES] + ps[0][:, LANES:]
            for p in ps[1:]:
                lsum = lsum + p[:, :LANES] + p[:, LANES:]
            lpart_scr[hh] += lsum
            p_all = ps[0] if nt == 1 else jnp.concatenate(ps, axis=1)
            oacc_scr[hh] += _dot(p_all.astype(MXU_DTYPE),
                                 v_ref[0, pl.ds(k0, nt * TK), hh * HEAD_DIM:(hh + 1) * HEAD_DIM])

    _pair_loop(qi + 1, pv_tiles)
    for hh in range(HG_B):
        lsum = jnp.sum(lpart_scr[hh], axis=-1, keepdims=True)
        o_ref[0, :, hh * HEAD_DIM:(hh + 1) * HEAD_DIM] = (oacc_scr[hh] / lsum).astype(o_ref.dtype)


def _fox(proj3, cums, g_qb, g_kb, n_heads):
    b, s, _ = proj3.shape
    tq = TQ_B
    gw = HG_B * HEAD_DIM
    return pl.pallas_call(
        _fox_kernel,
        grid=(b, n_heads // HG_B, s // tq),
        in_specs=[pl.BlockSpec((1, tq, gw), lambda bb, g, i: (bb, i, U_QB // HG_B + g)),
                  pl.BlockSpec((1, s, gw), lambda bb, g, i: (bb, 0, U_KB // HG_B + g)),
                  pl.BlockSpec((1, s, gw), lambda bb, g, i: (bb, 0, U_VB // HG_B + g)),
                  pl.BlockSpec((1, tq, LANES), lambda bb, g, i: (bb, i, 0)),
                  pl.BlockSpec((1, s, LANES), lambda bb, g, i: (bb, 0, 0)),
                  pl.BlockSpec((1, HEAD_DIM), lambda bb, g, i: (0, 0)),
                  pl.BlockSpec((1, HEAD_DIM), lambda bb, g, i: (0, 0))],
        out_specs=pl.BlockSpec((1, tq, gw), lambda bb, g, i: (bb, i, g)),
        out_shape=jax.ShapeDtypeStruct((b, s, n_heads * HEAD_DIM), MXU_DTYPE),
        scratch_shapes=[pltpu.VMEM((HG_B, s, 2 * HEAD_DIM), MXU_DTYPE),
                        pltpu.VMEM((HG_B, tq, 2 * HEAD_DIM), MXU_DTYPE),
                        pltpu.VMEM((s // TK, HG_B, tq, TK), F32),
                        pltpu.VMEM((HG_B, tq, LANES), F32),
                        pltpu.VMEM((HG_B, tq, LANES), F32),
                        pltpu.VMEM((HG_B, tq, HEAD_DIM), F32)],
        compiler_params=_cparams(("parallel", "parallel", "arbitrary")),
        name="fox",
    )(proj3, proj3, proj3, cums, cums, g_qb, g_kb)


TM_MERGE = 256


def _merge_kernel(ga_ref, gb_ref, oa_ref, ob_ref, x_ref, mod_ref, bga_ref, bgb_ref, wpa_ref, wpb_ref,
                  wo_ref, gffn_ref, wrh_ref, wrl_ref, br_ref, x1_ref, h2_ref, rl_ref):
    ga = jax.nn.sigmoid(ga_ref[...].astype(F32) + bga_ref[...])
    gb = jax.nn.sigmoid(gb_ref[...].astype(F32) + bgb_ref[...])
    merged = ga * _dot(oa_ref[...], wpa_ref[...]) + gb * _dot(ob_ref[...], wpb_ref[...])
    upd = _dot(merged.astype(MXU_DTYPE), wo_ref[...])
    x1 = x_ref[...] + mod_ref[0, 2:3, :] * upd
    x1_ref[...] = x1
    h2 = _rms(x1, gffn_ref[...]) * (1.0 + mod_ref[0, 4:5, :]) + mod_ref[0, 3:4, :]
    h2_ref[...] = h2
    hh = h2.astype(MXU_DTYPE)
    hl = (h2 - hh.astype(F32)).astype(MXU_DTYPE)
    rl_ref[...] = (_dot(hh, wrh_ref[...]) + _dot(hl, wrh_ref[...]) + _dot(hh, wrl_ref[...])
                   + br_ref[...])


def _merge(proj, o_a, o_b, x2, mod3, b_gate, w_pa, w_pb, w_o, g_ffn, wr_hi, wr_lo, b_r, seq):
    t, d = x2.shape
    tm = TM_MERGE
    per_b = seq // tm
    wa = o_a.shape[1]
    res = lambda shape: pl.BlockSpec(shape, lambda i: (0,) * len(shape), pipeline_mode=pl.Buffered(1))
    return pl.pallas_call(
        _merge_kernel,
        grid=(t // tm,),
        in_specs=[pl.BlockSpec((tm, d), lambda i: (i, 0)),
                  pl.BlockSpec((tm, d), lambda i: (i, 1)),
                  pl.BlockSpec((tm, wa), lambda i: (i, 0)),
                  pl.BlockSpec((tm, wa), lambda i: (i, 0)),
                  pl.BlockSpec((tm, d), lambda i: (i, 0)),
                  pl.BlockSpec((1, 6, d), lambda i: (i // per_b, 0, 0)),
                  pl.BlockSpec((1, d), lambda i: (0, 0)),
                  pl.BlockSpec((1, d), lambda i: (0, 1)),
                  res((wa, d)), res((wa, d)), res((d, d)),
                  pl.BlockSpec((1, d), lambda i: (0, 0)),
                  res((d, LANES)), res((d, LANES)),
                  pl.BlockSpec((1, LANES), lambda i: (0, 0))],
        out_specs=[pl.BlockSpec((tm, d), lambda i: (i, 0)),
                   pl.BlockSpec((tm, d), lambda i: (i, 0)),
                   pl.BlockSpec((tm, LANES), lambda i: (i, 0))],
        out_shape=[jax.ShapeDtypeStruct((t, d), F32),
                   jax.ShapeDtypeStruct((t, d), F32),
                   jax.ShapeDtypeStruct((t, LANES), F32)],
        compiler_params=_cparams(("parallel",)),
        name="merge",
    )(proj, proj, o_a, o_b, x2, mod3, b_gate, b_gate, w_pa, w_pb, w_o, g_ffn, wr_hi, wr_lo, b_r)


TM_ROUTE = 1024
TM_ROWS = 128
R_E0, R_E1, R_P0, R_P1 = 0, 1, 4, 5
IT_TILE, IT_E, IT_LO, IT_HI, IT_FLAG, IT_NEXT, IT_NEXT2, IT_SLOT = range(8)
F_VALID, F_FIRST_OF_EXPERT, F_FIRST_OF_TILE = 1, 2, 4


def _route_kernel(rl_ref, route_ref, post_ref, items_ref, cnt_scr, run_scr, offs_scr, *, n_tiles):
    tm = rl_ref.shape[0]
    sweep = pl.program_id(0)
    step = pl.program_id(1)
    iw = items_ref.shape[1]

    @pl.when((sweep == 0) & (step == 0))
    def _():
        cnt_scr[...] = jnp.zeros(cnt_scr.shape, F32)

    r = rl_ref[...]
    lane = lax.broadcasted_iota(I32, (tm, LANES), 1).astype(F32)
    neg_inf = -jnp.inf
    gmask = lane < N_GROUPS
    gl = jnp.where(gmask, r, neg_inf)
    gmax = jnp.max(gl, axis=-1, keepdims=True)
    gidx = jnp.min(jnp.where(gl == gmax, lane, float(LANES)), axis=-1, keepdims=True)
    gsum = jnp.sum(jnp.where(gmask, jnp.exp(r - gmax), 0.0), axis=-1, keepdims=True)
    gw = 1.0 / gsum
    lo = N_GROUPS + EXPERTS_PER_GROUP * gidx
    emask = (lane >= lo) & (lane < lo + EXPERTS_PER_GROUP)
    el = jnp.where(emask, r, neg_inf)
    v0 = jnp.max(el, axis=-1, keepdims=True)
    i0 = jnp.min(jnp.where(el == v0, lane, float(LANES)), axis=-1, keepdims=True)
    el2 = jnp.where(lane == i0, neg_inf, el)
    v1 = jnp.max(el2, axis=-1, keepdims=True)
    i1 = jnp.min(jnp.where(el2 == v1, lane, float(LANES)), axis=-1, keepdims=True)
    tt = jnp.exp(v1 - v0)
    p0 = gw / (1.0 + tt)
    p1 = gw * tt / (1.0 + tt)
    e0 = i0 - N_GROUPS
    e1 = i1 - N_GROUPS

    hit0 = lane == e0
    hit1 = lane == e1
    oh = jnp.where(hit0 | hit1, 1.0, 0.0)

    @pl.when(sweep == 0)
    def _():
        ones = jnp.ones((tm, LANES), MXU_DTYPE)
        cnt_scr[...] += _dot(oh.T.astype(MXU_DTYPE), ones)

    @pl.when((sweep == 1) & (step == 0))
    def _():
        _plan_items(cnt_scr[...], items_ref, offs_scr, n_tiles, iw)
        run_scr[...] = jnp.zeros(run_scr.shape, F32)

    @pl.when(sweep == 1)
    def _():
        rr = lax.broadcasted_iota(I32, (tm, tm), 0)
        cc = lax.broadcasted_iota(I32, (tm, tm), 1)
        ltri = jnp.where(cc < rr, 1.0, 0.0).astype(MXU_DTYPE)
        before = _dot(ltri, oh.astype(MXU_DTYPE)) + run_scr[...] + offs_scr[...]
        pos0 = jnp.sum(jnp.where(hit0, before, 0.0), axis=-1, keepdims=True)
        pos1 = jnp.sum(jnp.where(hit1, before, 0.0), axis=-1, keepdims=True)
        run_scr[...] = run_scr[...] + jnp.sum(oh, axis=0, keepdims=True)
        out = jnp.zeros((tm, LANES), F32)
        for k, val in ((R_E0, e0), (R_E1, e1), (R_P0, p0), (R_P1, p1)):
            out = jnp.where(lane == k, val, out)
        route_ref[...] = out
        pmat = jnp.where(lane == 0.0, pos0, jnp.where(lane == 1.0, pos1, 0.0))
        post_ref[...] = pmat.T[0:8, :].astype(I32)


def _plan_items(cnt_col, items_ref, offs_scr, n_tiles, iw):
    tmr = float(TM_ROWS)
    sub = lax.broadcasted_iota(I32, (LANES, LANES), 0)
    lan = lax.broadcasted_iota(I32, (LANES, LANES), 1)
    lstrict = jnp.where(lan < sub, 1.0, 0.0).astype(MXU_DTYPE)
    hi = jnp.floor(cnt_col * (1.0 / LANES))
    lo = cnt_col - hi * LANES
    offs_col = _dot(lstrict, hi.astype(MXU_DTYPE)) * LANES + _dot(lstrict, lo.astype(MXU_DTYPE))
    first_t = jnp.floor(offs_col * (1.0 / tmr))
    last_t = jnp.floor((offs_col + cnt_col - 1.0) * (1.0 / tmr))
    n_col = jnp.where(cnt_col > 0.0, last_t - first_t + 1.0, 0.0)
    base_col = _dot(lstrict, n_col.astype(MXU_DTYPE))
    end_col = base_col + n_col
    offs_scr[...] = offs_col.T[0:1, :]

    rep = lambda col: jnp.concatenate([col] * (iw // LANES), axis=1)
    wl = lax.broadcasted_iota(I32, (LANES, iw), 1).astype(F32)
    sub_f = lax.broadcasted_iota(I32, (LANES, iw), 0).astype(F32)
    end_r = rep(end_col)
    w_total = end_r[LANES - 1:LANES, :]
    count_le = lambda v: jnp.sum(jnp.where(end_r <= v, 1.0, 0.0), axis=0, keepdims=True)
    w_row = wl[0:1, :]
    eidx = count_le(w_row)
    sel = sub_f == eidx
    pick = lambda col: jnp.sum(jnp.where(sel, rep(col), 0.0), axis=0, keepdims=True)
    e_base, e_first, e_offs, e_cnt, e_end = (pick(base_col), pick(first_t), pick(offs_col),
                                             pick(cnt_col), pick(end_col))
    valid = w_row < w_total
    tile = e_first + (w_row - e_base)
    row_lo = jnp.maximum(e_offs, tile * tmr) - tile * tmr
    row_hi = jnp.minimum(e_offs + e_cnt, (tile + 1.0) * tmr) - tile * tmr
    first_e = w_row == e_base
    flags = jnp.where(valid, F_VALID + jnp.where(first_e, float(F_FIRST_OF_EXPERT), 0.0)
                      + jnp.where(row_lo == 0.0, float(F_FIRST_OF_TILE), 0.0), 0.0)
    has1 = valid & first_e & (e_end < w_total)
    e1 = count_le(e_end)
    end1 = jnp.sum(jnp.where(sub_f == e1, end_r, 0.0), axis=0, keepdims=True)
    has2 = has1 & (end1 < w_total)
    nxt = jnp.where(has1, e1, -1.0)
    nxt2 = jnp.where(has2, count_le(end1), -1.0)
    ord_col = _dot(lstrict, jnp.where(cnt_col > 0.0, 1.0, 0.0).astype(MXU_DTYPE))
    e_ord = pick(ord_col)
    slot = e_ord - 2.0 * jnp.floor(e_ord * 0.5)
    e_last = count_le(w_total - 1.0)
    rows = {IT_TILE: jnp.where(valid, tile, n_tiles - 1.0),
            IT_E: jnp.where(valid, eidx, e_last),
            IT_LO: jnp.where(valid, row_lo, 0.0),
            IT_HI: jnp.where(valid, row_hi, 0.0),
            IT_FLAG: flags,
            IT_NEXT: nxt,
            IT_NEXT2: nxt2,
            IT_SLOT: jnp.where(valid, slot, 0.0)}
    sub8 = lax.broadcasted_iota(I32, (8, iw), 0)
    out = jnp.zeros((8, iw), F32)
    for k, val in rows.items():
        out = jnp.where(sub8 == k, val, out)
    items_ref[...] = out.astype(I32)


def _route(rlog, n_items):
    t = rlog.shape[0]
    tm = min(TM_ROUTE, t)
    iw = ((n_items + LANES - 1) // LANES) * LANES
    kern = functools.partial(_route_kernel, n_tiles=2 * t // TM_ROWS)
    return pl.pallas_call(
        kern,
        grid=(2, t // tm),
        in_specs=[pl.BlockSpec((tm, LANES), lambda p, i: (i, 0))],
        out_specs=[pl.BlockSpec((tm, LANES), lambda p, i: (i * p, 0)),
                   pl.BlockSpec((8, tm), lambda p, i: (0, i * p)),
                   pl.BlockSpec((8, iw), lambda p, i: (0, 0))],
        out_shape=[jax.ShapeDtypeStruct((t, LANES), F32),
                   jax.ShapeDtypeStruct((8, t), I32),
                   jax.ShapeDtypeStruct((8, iw), I32)],
        scratch_shapes=[pltpu.VMEM((LANES, LANES), F32),
                        pltpu.VMEM((1, LANES), F32),
                        pltpu.VMEM((1, LANES), F32)],
        compiler_params=_cparams(("arbitrary", "arbitrary")),
        name="route",
    )(rlog)


TM_DISP = 256


def _dispatch_kernel(pos0_ref, pos1_ref, h2_ref, xs_ref, sems):
    step = pl.program_id(0)
    n_steps = pl.num_programs(0)
    base = step * TM_DISP

    def row_copy(r, p, slot):
        return pltpu.make_async_copy(h2_ref.at[pl.ds(r, 1), :], xs_ref.at[pl.ds(p, 1), :], sems.at[slot])

    def issue(r, carry):
        row_copy(base + r, pos0_ref[base + r], step % 2).start(priority=0)
        row_copy(base + r, pos1_ref[base + r], step % 2).start(priority=1)
        return carry

    lax.fori_loop(0, TM_DISP, issue, 0, unroll=8)

    def drain_slot(slot):
        def drain(r, carry):
            row_copy(0, 0, slot).wait()
            row_copy(0, 0, slot).wait()
            return carry

        lax.fori_loop(0, TM_DISP, drain, 0, unroll=8)

    @pl.when(step > 0)
    def _():
        drain_slot((step + 1) % 2)

    @pl.when(step == n_steps - 1)
    def _():
        drain_slot(step % 2)


def _dispatch(pos0, pos1, h2):
    t, d = h2.shape
    return pl.pallas_call(
        _dispatch_kernel,
        grid_spec=pltpu.PrefetchScalarGridSpec(
            num_scalar_prefetch=2,
            grid=(t // TM_DISP,),
            in_specs=[pl.BlockSpec(memory_space=pl.ANY)],
            out_specs=pl.BlockSpec(memory_space=pl.ANY),
            scratch_shapes=[pltpu.SemaphoreType.DMA((2,))]),
        out_shape=jax.ShapeDtypeStruct((2 * t, d), h2.dtype),
        compiler_params=pltpu.CompilerParams(dimension_semantics=("arbitrary",),
                                             vmem_limit_bytes=VMEM_LIMIT, has_side_effects=True),
        name="dispatch",
    )(pos0, pos1, h2)


WEIGHT_DMA_PRIORITY = 1


def _experts_kernel(tile_ref, e_ref, lo_ref, hi_ref, flag_ref, next_ref, next2_ref, slot_ref,
                    xs_ref, w1_hbm, w3_hbm, w2_hbm, ys_ref,
                    f1_scr, f3_scr, f2_scr, w1_scr, w3_scr, w2_scr, sems):
    del tile_ref
    w = pl.program_id(0)
    flag = flag_ref[w]

    mats = ((w1_hbm, f1_scr, w1_scr), (w3_hbm, f3_scr, w3_scr), (w2_hbm, f2_scr, w2_scr))

    def weight_copy(k, e, slot):
        return pltpu.make_async_copy(mats[k][0].at[e], mats[k][1].at[slot], sems.at[slot, k])

    @pl.when(w == 0)
    def _():
        for k in range(3):
            weight_copy(k, e_ref[0], 0).start(priority=WEIGHT_DMA_PRIORITY)
        nxt = next_ref[0]

        @pl.when(nxt >= 0)
        def _():
            for k in range(3):
                weight_copy(k, nxt, 1).start(priority=WEIGHT_DMA_PRIORITY)

    @pl.when((flag & F_FIRST_OF_EXPERT) != 0)
    def _():
        slot = slot_ref[w]
        nxt2 = next2_ref[w]
        for k in range(3):
            weight_copy(k, e_ref[w], slot).wait()
            mats[k][2][...] = mats[k][1][slot].astype(mats[k][2].dtype)

            @pl.when(nxt2 >= 0)
            def _():
                weight_copy(k, nxt2, slot).start(priority=WEIGHT_DMA_PRIORITY)

    @pl.when((flag & F_VALID) != 0)
    def _():
        x = xs_ref[...].astype(MXU_DTYPE)
        a = _dot(x, w1_scr[...])
        u = _dot(x, w3_scr[...])
        hm = (a * jax.nn.sigmoid(a)) * u
        res = _dot(hm.astype(MXU_DTYPE), w2_scr[...])
        row = lax.broadcasted_iota(I32, res.shape, 0)
        mine = (row >= lo_ref[w]) & (row < hi_ref[w])

        @pl.when((flag & F_FIRST_OF_TILE) != 0)
        def _():
            ys_ref[...] = jnp.where(mine, res, 0.0)

        @pl.when((flag & F_FIRST_OF_TILE) == 0)
        def _():
            ys_ref[...] = jnp.where(mine, res, ys_ref[...])


def _experts(items, xs, w1, w3, w2, n_items):
    n_rows, d = xs.shape
    f = w1.shape[2]
    tm = TM_ROWS
    tile_map = lambda w, tile, *_: (tile[w], 0)
    return pl.pallas_call(
        _experts_kernel,
        grid_spec=pltpu.PrefetchScalarGridSpec(
            num_scalar_prefetch=8,
            grid=(n_items,),
            in_specs=[pl.BlockSpec((tm, d), tile_map),
                      pl.BlockSpec(memory_space=pl.ANY),
                      pl.BlockSpec(memory_space=pl.ANY),
                      pl.BlockSpec(memory_space=pl.ANY)],
            out_specs=pl.BlockSpec((tm, d), tile_map),
            scratch_shapes=[pltpu.VMEM((2, d, f), F32),
                            pltpu.VMEM((2, d, f), F32),
                            pltpu.VMEM((2, f, d), F32),
                            pltpu.VMEM((d, f), MXU_DTYPE),
                            pltpu.VMEM((d, f), MXU_DTYPE),
                            pltpu.VMEM((f, d), MXU_DTYPE),
                            pltpu.SemaphoreType.DMA((2, 3))]),
        out_shape=jax.ShapeDtypeStruct((n_rows, d), F32),
        compiler_params=_cparams(("arbitrary",)),
        name="experts",
    )(items[IT_TILE], items[IT_E], items[IT_LO], items[IT_HI], items[IT_FLAG], items[IT_NEXT],
      items[IT_NEXT2], items[IT_SLOT], xs, w1, w3, w2)


TM_COMB = 256


def _combine_kernel(pos0_ref, pos1_ref, ys_ref, x1_ref, route_ref, mod_ref, o_ref, y0_scr, y1_scr, sems):
    step = pl.program_id(0)
    n_steps = pl.num_programs(0)

    def row_copy(p, dst, slot, r):
        return pltpu.make_async_copy(ys_ref.at[pl.ds(p, 1), :], dst.at[slot, pl.ds(r, 1), :], sems.at[slot])

    def issue_step(s):
        slot = s % 2
        base = s * TM_COMB

        def issue(r, carry):
            row_copy(pos0_ref[base + r], y0_scr, slot, r).start(priority=0)
            row_copy(pos1_ref[base + r], y1_scr, slot, r).start(priority=1)
            return carry

        lax.fori_loop(0, TM_COMB, issue, 0, unroll=8)

    @pl.when(step == 0)
    def _():
        issue_step(step)

    @pl.when(step + 1 < n_steps)
    def _():
        issue_step(step + 1)

    slot = step % 2

    def drain(r, carry):
        row_copy(0, y0_scr, slot, 0).wait()
        row_copy(0, y1_scr, slot, 0).wait()
        return carry

    lax.fori_loop(0, TM_COMB, drain, 0, unroll=8)

    rt = route_ref[...]
    lane = lax.broadcasted_iota(I32, rt.shape, 1)
    p0 = jnp.sum(jnp.where(lane == R_P0, rt, 0.0), axis=-1, keepdims=True)
    p1 = jnp.sum(jnp.where(lane == R_P1, rt, 0.0), axis=-1, keepdims=True)
    y = p0 * y0_scr[slot] + p1 * y1_scr[slot]
    o_ref[...] = x1_ref[...] + mod_ref[0, 5:6, :] * y


def _combine(pos0, pos1, ys, x1, route, mod3, seq):
    t, d = x1.shape
    tm = TM_COMB
    per_b = seq // tm
    return pl.pallas_call(
        _combine_kernel,
        grid_spec=pltpu.PrefetchScalarGridSpec(
            num_scalar_prefetch=2,
            grid=(t // tm,),
            in_specs=[pl.BlockSpec(memory_space=pl.ANY),
                      pl.BlockSpec((tm, d), lambda i, p0, p1: (i, 0)),
                      pl.BlockSpec((tm, LANES), lambda i, p0, p1: (i, 0)),
                      pl.BlockSpec((1, 6, d), lambda i, p0, p1: (i // per_b, 0, 0))],
            out_specs=pl.BlockSpec((tm, d), lambda i, p0, p1: (i, 0)),
            scratch_shapes=[pltpu.VMEM((2, tm, d), F32),
                            pltpu.VMEM((2, tm, d), F32),
                            pltpu.SemaphoreType.DMA((2,))]),
        out_shape=jax.ShapeDtypeStruct((t, d), F32),
        compiler_params=_cparams(("arbitrary",)),
        name="combine",
    )(pos0, pos1, ys, x1, route, mod3)


def kernel(x, c, w_ada, b_ada, g_mix, w_in, b_gate, b_forget, w_uk, w_uv, g_qa, g_kv, g_qb, g_kb,
           w_pa, w_pb, w_o, g_ffn, w_rg, b_rg, w_re, b_re, w1, w3, w2):
    b, s, d = x.shape
    depth = w_ada.shape[0]
    t = b * s
    n_heads_b = b_forget.shape[1]
    n_items = 2 * t // TM_ROWS + N_EXPERTS
    c8 = jnp.zeros((8, d), F32).at[:b].set(c)
    x2 = x.reshape(t, d)
    row = lambda v: v.reshape(1, -1)
    for l in range(depth):
        mod3 = _ada(c8, w_ada[l], row(b_ada[l]))[:b].reshape(b, 6, d)
        proj, misc = _inproj(x2, mod3, row(g_mix[l]), _pack_w_in(jnp.transpose(w_in[l])), s)
        proj3 = proj.reshape(b, s, NP_COLS)
        misc3 = misc.reshape(b, s, LANES)
        bf_row = jnp.zeros((1, LANES), F32).at[0, MISC_F:MISC_F + n_heads_b].set(b_forget[l])
        kv, cums = _prep(proj3, misc3, row(g_kv[l]), bf_row)
        o_a = _dsa(proj3, misc3, kv, w_uk[l].astype(MXU_DTYPE), w_uv[l].astype(MXU_DTYPE), row(g_qa[l]))
        o_b = _fox(proj3, cums, row(g_qb[l]), row(g_kb[l]), n_heads_b)

        w_r = jnp.zeros((d, LANES), F32).at[:, :N_GROUPS].set(w_rg[l])
        w_r = w_r.at[:, N_GROUPS:N_GROUPS + N_EXPERTS].set(w_re[l])
        wr_hi = w_r.astype(MXU_DTYPE)
        wr_lo = (w_r - wr_hi.astype(F32)).astype(MXU_DTYPE)
        b_r = jnp.zeros((1, LANES), F32).at[0, :N_GROUPS].set(b_rg[l])
        b_r = b_r.at[0, N_GROUPS:N_GROUPS + N_EXPERTS].set(b_re[l])
        x1, h2, rlog = _merge(proj, o_a.reshape(t, -1), o_b.reshape(t, -1), x2, mod3, row(b_gate[l]),
                              w_pa[l].astype(MXU_DTYPE), w_pb[l].astype(MXU_DTYPE),
                              w_o[l].astype(MXU_DTYPE), row(g_ffn[l]), wr_hi, wr_lo, b_r, s)
        route, pos_t, items = _route(rlog, n_items)
        pos0, pos1 = pos_t[0], pos_t[1]
        xs = _dispatch(pos0, pos1, h2)
        ys = _experts(items, xs, w1[l], w3[l], w2[l], n_items)
        x2 = _combine(pos0, pos1, ys, x1, route, mod3, s)
    return x2.reshape(b, s, d)
```

```python
import functools

import jax
import jax.numpy as jnp
from jax import lax
from jax.experimental import pallas as pl
from jax.experimental.pallas import tpu as pltpu

F32 = jnp.float32
I32 = jnp.int32
MXU_DTYPE = jnp.bfloat16

CHUNK = 64
HEAD_DIM = 128
D_LAT = 256
N_IDX_HEADS = 16
D_IDX = 64
TOPK_MAX = 256
N_GROUPS = 8
EXPERTS_PER_GROUP = 8
N_EXPERTS = N_GROUPS * EXPERTS_PER_GROUP
RMS_EPS = 1e-6

LANES = 128
VMEM_LIMIT = 56 * 1024 * 1024

NEG_BIG = -1e30
INT_MIN = -2147483648


def _cparams(sem):
    return pltpu.CompilerParams(dimension_semantics=sem, vmem_limit_bytes=VMEM_LIMIT)


def _dot(a, b):
    return jnp.dot(a, b, preferred_element_type=F32)


def _dot_nt(a, b):
    return lax.dot_general(a, b, (((1,), (1,)), ((), ())), preferred_element_type=F32)


def _rms(x, g):
    return x * lax.rsqrt(jnp.mean(x * x, axis=-1, keepdims=True) + RMS_EPS) * g


def _ada_kernel(c_ref, w_ref, b_ref, o_ref):
    c = c_ref[...]
    a = c * jax.nn.sigmoid(c)
    o_ref[...] = _dot(a.astype(MXU_DTYPE), w_ref[...].astype(MXU_DTYPE)) + b_ref[...]


def _ada(c8, w_ada, b_ada):
    d, n = w_ada.shape
    tn = 1024
    return pl.pallas_call(
        _ada_kernel,
        grid=(n // tn,),
        in_specs=[pl.BlockSpec((8, d), lambda j: (0, 0)),
                  pl.BlockSpec((d, tn), lambda j: (0, j)),
                  pl.BlockSpec((1, tn), lambda j: (0, j))],
        out_specs=pl.BlockSpec((8, tn), lambda j: (0, j)),
        out_shape=jax.ShapeDtypeStruct((8, n), F32),
        compiler_params=_cparams(("arbitrary",)),
        name="ada",
    )(c8, w_ada, b_ada)


NP_COLS = 76 * LANES
U_QA, U_QIDX, U_QB, U_KB, U_VB, U_CKV, U_MISC = 32, 40, 48, 56, 64, 72, 74
TN_PROJ = 4 * LANES
TM_PROJ = 1024
MISC_TILE = (U_MISC * LANES) // TN_PROJ
MISC_OFF = U_MISC * LANES - MISC_TILE * TN_PROJ
MISC_K, MISC_F, MISC_W = 0, 64, 72


def _pack_moves():
    sizes = (1024, 256, 1024, 64, 16, 1024, 1024, 1024, 8, 4096)
    src = [0]
    for n in sizes:
        src.append(src[-1] + n)
    q_a, c_kv, q_idx, k_idx, w_idx, q_b, k_b, v_b, f_b, gate = src[:10]
    m = U_MISC * LANES
    return ((gate, 0, 4096), (q_a, U_QA * LANES, 1024), (q_idx, U_QIDX * LANES, 1024),
            (q_b, U_QB * LANES, 1024), (k_b, U_KB * LANES, 1024), (v_b, U_VB * LANES, 1024),
            (c_kv, U_CKV * LANES, 256), (f_b, m + MISC_F, 8), (w_idx, m + MISC_W, 16),
            (k_idx, m + MISC_K, 64))


def _pack_kernel(w_ref, o_ref):
    cols = o_ref.shape[1]
    m = U_MISC * LANES
    o_ref[m:m + 2 * LANES, :] = jnp.zeros((2 * LANES, cols), o_ref.dtype)
    for src, dst, n in _pack_moves():
        o_ref[dst:dst + n, :] = w_ref[src:src + n, :].astype(o_ref.dtype)


def _pack_w_in(w_in_t):
    n_in, d = w_in_t.shape
    tc = 256
    return pl.pallas_call(
        _pack_kernel,
        grid=(d // tc,),
        in_specs=[pl.BlockSpec((n_in, tc), lambda i: (0, i))],
        out_specs=pl.BlockSpec((NP_COLS, tc), lambda i: (0, i)),
        out_shape=jax.ShapeDtypeStruct((NP_COLS, d), MXU_DTYPE),
        compiler_params=_cparams(("parallel",)),
        name="pack",
    )(w_in_t)


def _inproj_kernel(x_ref, mod_ref, g_ref, w_ref, o_ref, misc_ref, h_scr):
    j = pl.program_id(1)

    @pl.when(j == 0)
    def _():
        y = _rms(x_ref[...], g_ref[...])
        sh = mod_ref[0, 0:1, :]
        sc = mod_ref[0, 1:2, :]
        h_scr[...] = (y * (1.0 + sc) + sh).astype(h_scr.dtype)

    acc = _dot_nt(h_scr[...], w_ref[...])
    o_ref[...] = acc.astype(o_ref.dtype)

    @pl.when(j == MISC_TILE)
    def _():
        misc_ref[...] = acc[:, MISC_OFF:MISC_OFF + LANES]


def _inproj(x2, mod3, g_mix, w_packed, seq):
    t, d = x2.shape
    tm = min(TM_PROJ, seq)
    per_b = seq // tm
    return pl.pallas_call(
        _inproj_kernel,
        grid=(t // tm, NP_COLS // TN_PROJ),
        in_specs=[pl.BlockSpec((tm, d), lambda i, j: (i, 0)),
                  pl.BlockSpec((1, 6, d), lambda i, j: (i // per_b, 0, 0)),
                  pl.BlockSpec((1, d), lambda i, j: (0, 0)),
                  pl.BlockSpec((TN_PROJ, d), lambda i, j: (j, 0))],
        out_specs=[pl.BlockSpec((tm, TN_PROJ), lambda i, j: (i, j)),
                   pl.BlockSpec((tm, LANES), lambda i, j: (i, 0))],
        out_shape=[jax.ShapeDtypeStruct((t, NP_COLS), MXU_DTYPE),
                   jax.ShapeDtypeStruct((t, LANES), F32)],
        scratch_shapes=[pltpu.VMEM((tm, d), MXU_DTYPE)],
        compiler_params=_cparams(("parallel", "arbitrary")),
        name="inproj",
    )(x2, mod3, g_mix, w_packed)


TK = 256


def _prep_kernel(ckv_ref, misc_ref, gkv_ref, bf_ref, kv_ref, cum_ref):
    s = ckv_ref.shape[1]
    kv_ref[0] = _rms(ckv_ref[0].astype(F32), gkv_ref[...]).astype(kv_ref.dtype)

    r = lax.broadcasted_iota(I32, (LANES, LANES), 0)
    c = lax.broadcasted_iota(I32, (LANES, LANES), 1)
    tri = jnp.where(c <= r, 1.0, 0.0).astype(MXU_DTYPE)
    carry = jnp.zeros((1, LANES), F32)
    for blk in range(s // LANES):
        z = misc_ref[0, blk * LANES:(blk + 1) * LANES, :] + bf_ref[...]
        ls = jnp.minimum(z, 0.0) - jnp.log1p(jnp.exp(-jnp.abs(z)))
        p1 = ls.astype(MXU_DTYPE)
        r1 = ls - p1.astype(F32)
        p2 = r1.astype(MXU_DTYPE)
        p3 = (r1 - p2.astype(F32)).astype(MXU_DTYPE)
        cs = _dot(tri, p1) + _dot(tri, p2) + _dot(tri, p3) + carry
        carry = cs[LANES - 1:LANES, :]
        cum_ref[0, blk * LANES:(blk + 1) * LANES, :] = cs * (HEAD_DIM ** 0.5)


def _prep(proj3, misc3, g_kv, bf_row):
    b, s, _ = proj3.shape
    return pl.pallas_call(
        _prep_kernel,
        grid=(b,),
        in_specs=[pl.BlockSpec((1, s, D_LAT), lambda i: (i, 0, U_CKV * LANES // D_LAT)),
                  pl.BlockSpec((1, s, LANES), lambda i: (i, 0, 0)),
                  pl.BlockSpec((1, D_LAT), lambda i: (0, 0)),
                  pl.BlockSpec((1, LANES), lambda i: (0, 0))],
        out_specs=[pl.BlockSpec((1, s, D_LAT), lambda i: (i, 0, 0)),
                   pl.BlockSpec((1, s, LANES), lambda i: (i, 0, 0))],
        out_shape=[jax.ShapeDtypeStruct((b, s, D_LAT), MXU_DTYPE),
                   jax.ShapeDtypeStruct((b, s, LANES), F32)],
        compiler_params=_cparams(("parallel",)),
        name="prep",
    )(proj3, misc3, g_kv, bf_row)


TQ_A = 256
N_BISECT = 32
LOG2E = 1.4426950408889634
assert TQ_A == TK


def _pair_loop(n, body):
    def pair(j, carry):
        body(2 * j, 2)
        return carry

    lax.fori_loop(0, jnp.right_shift(n, 1), pair, 0)

    @pl.when(jnp.bitwise_and(n, 1) == 1)
    def _():
        body(n - 1, 1)


def _dsa_kernel(qa_ref, qidx_ref, miscq_ref, misck_ref, kv_ref, wuk_ref, wuv_ref, gqa_ref, o_ref,
                sc_scr, bias_scr, logit_scr, qlat_scr, qh_scr, mm_scr, mpart_scr, lpart_scr, oacc_scr,
                *, topk, n_heads):
    i = pl.program_id(1)
    tq = TQ_A
    nk = i + 1
    q0 = i * tq

    wt = miscq_ref[0].T
    wq = wt[MISC_W:MISC_W + N_IDX_HEADS, :] * (D_IDX ** -0.5 * N_IDX_HEADS ** -0.5)
    for h in range(N_IDX_HEADS):
        qh_scr[h] = qidx_ref[0, :, h * D_IDX:(h + 1) * D_IDX]

    def fold(v, op, rows):
        return op(v.reshape(v.shape[0] // rows, rows, tq), axis=0)

    mm_scr[0:8, :] = jnp.full((8, tq), NEG_BIG, F32)
    mm_scr[8:16, :] = jnp.full((8, tq), -NEG_BIG, F32)

    def score_tiles(kt, nt):
        mx8 = mm_scr[0:8, :]
        mn8 = mm_scr[8:16, :]
        for half in range(nt * (TK // LANES)):
            k0 = pl.multiple_of(kt * TK + half * LANES, LANES)
            kx = misck_ref[0, pl.ds(k0, LANES), :][:, MISC_K:MISC_K + D_IDX].astype(MXU_DTYPE)
            acc = jnp.zeros((LANES, tq), F32)
            for h in range(N_IDX_HEADS):
                d = _dot_nt(kx, qh_scr[h])
                acc = acc + jnp.maximum(d, 0.0) * wq[h:h + 1, :]
            kpos = k0 + lax.broadcasted_iota(I32, (LANES, tq), 0)
            qpos = q0 + lax.broadcasted_iota(I32, (LANES, tq), 1)
            adm = (kpos // CHUNK) <= (qpos // CHUNK)
            sc_scr[pl.ds(k0, LANES), :] = jnp.where(adm, acc, NEG_BIG)
            mx8 = jnp.maximum(mx8, fold(jnp.where(adm, acc, NEG_BIG), jnp.max, 8))
            mn8 = jnp.minimum(mn8, fold(jnp.where(adm, acc, -NEG_BIG), jnp.min, 8))
        mm_scr[0:8, :] = mx8
        mm_scr[8:16, :] = mn8

    _pair_loop(nk, score_tiles)

    def bis_body(it, carry):
        lo, hi = carry
        mid = lo + 0.5 * (hi - lo)

        def cnt_body(kt, c32):
            k0 = pl.multiple_of(kt * TK, TK)
            return c32 + fold(jnp.where(sc_scr[pl.ds(k0, TK), :] >= mid, 1.0, 0.0), jnp.sum, 32)

        cnt = jnp.sum(lax.fori_loop(0, nk, cnt_body, jnp.zeros((32, tq), F32)), axis=0, keepdims=True)
        ok = cnt >= topk
        return jnp.where(ok, mid, lo), jnp.where(ok, hi, mid)

    thr, _ = lax.fori_loop(0, N_BISECT, bis_body, (jnp.min(mm_scr[8:16, :], axis=0, keepdims=True),
                                                   jnp.max(mm_scr[0:8, :], axis=0, keepdims=True)))

    def bias_body(kt, carry):
        k0 = pl.multiple_of(kt * TK, TK)
        sel_t = sc_scr[pl.ds(k0, TK), :] >= thr
        bias_scr[kt] = jnp.where(sel_t, 0.0, NEG_BIG).T
        return carry

    lax.fori_loop(0, nk, bias_body, 0)

    for h in range(n_heads):
        ql = _dot(qa_ref[0, :, h * HEAD_DIM:(h + 1) * HEAD_DIM], wuk_ref[h])
        ql = _rms(ql, gqa_ref[...]) * (D_LAT ** -0.5 * LOG2E)
        qlat_scr[h * tq:(h + 1) * tq, :] = ql.astype(qlat_scr.dtype)
    mpart_scr[...] = jnp.full(mpart_scr.shape, NEG_BIG, F32)
    lpart_scr[...] = jnp.zeros(lpart_scr.shape, F32)
    oacc_scr[...] = jnp.zeros(oacc_scr.shape, F32)
    slopes2 = [2.0 ** (-8.0 * (h + 1) / n_heads) * LOG2E for h in range(n_heads)]

    def logit_tiles(kt, nt, last):
        k0 = pl.multiple_of(kt * TK, TK)
        kvt = kv_ref[0, pl.ds(k0, nt * TK), :]
        kcol = (k0 + lax.broadcasted_iota(I32, (1, nt * TK), 1)).astype(F32)
        if last:
            ahead = jnp.maximum(lax.broadcasted_iota(I32, (tq, TK), 1)
                                - lax.broadcasted_iota(I32, (tq, TK), 0), 0).astype(F32)
        for h in range(n_heads):
            rows = slice(h * tq, (h + 1) * tq)
            lg = _dot_nt(qlat_scr[rows, :], kvt) + slopes2[h] * kcol
            for t in range(nt):
                lt = lg[:, t * TK:(t + 1) * TK] + bias_scr[kt + t]
                if last:
                    lt = lt - (2.0 * slopes2[h]) * ahead
                logit_scr[kt + t, rows, :] = lt
                mpart_scr[rows, :] = jnp.maximum(mpart_scr[rows, :],
                                                 jnp.maximum(lt[:, :LANES], lt[:, LANES:]))

    _pair_loop(i, lambda kt, nt: logit_tiles(kt, nt, False))
    logit_tiles(i, 1, True)

    m = jnp.max(mpart_scr[...], axis=-1, keepdims=True)
    mpart_scr[...] = jnp.broadcast_to(m, mpart_scr.shape)

    def pv_tiles(kt, nt):
        k0 = pl.multiple_of(kt * TK, TK)
        kvt = kv_ref[0, pl.ds(k0, nt * TK), :]
        for h in range(n_heads):
            rows = slice(h * tq, (h + 1) * tq)
            mb = mpart_scr[rows, :]
            mb2 = jnp.concatenate([mb, mb], axis=1)
            ps = [jnp.exp2(logit_scr[kt + t, rows, :] - mb2) for t in range(nt)]
            lsum = ps[0][:, :LANES] + ps[0][:, LANES:]
            for p in ps[1:]:
                lsum = lsum + p[:, :LANES] + p[:, LANES:]
            lpart_scr[rows, :] += lsum
            p_all = ps[0] if nt == 1 else jnp.concatenate(ps, axis=1)
            oacc_scr[rows, :] += _dot(p_all.astype(MXU_DTYPE), kvt)

    _pair_loop(nk, pv_tiles)

    for h in range(n_heads):
        rows = slice(h * tq, (h + 1) * tq)
        lsum = jnp.sum(lpart_scr[rows, :], axis=-1, keepdims=True)
        o_lat = oacc_scr[rows, :] / lsum
        o_ref[0, :, h * HEAD_DIM:(h + 1) * HEAD_DIM] = _dot(
            o_lat.astype(MXU_DTYPE), wuv_ref[h]).astype(o_ref.dtype)


def _dsa(proj3, misc3, kv, w_uk, w_uv, g_qa):
    b, s, _ = proj3.shape
    n_heads = w_uk.shape[0]
    width = n_heads * HEAD_DIM
    topk = min(TOPK_MAX, s // 4)
    tq = TQ_A
    nkt = s // TK
    kern = functools.partial(_dsa_kernel, topk=topk, n_heads=n_heads)
    return pl.pallas_call(
        kern,
        grid=(b, s // tq),
        in_specs=[pl.BlockSpec((1, tq, width), lambda bb, i: (bb, i, U_QA * LANES // width)),
                  pl.BlockSpec((1, tq, N_IDX_HEADS * D_IDX),
                               lambda bb, i: (bb, i, U_QIDX * LANES // (N_IDX_HEADS * D_IDX))),
                  pl.BlockSpec((1, tq, LANES), lambda bb, i: (bb, i, 0)),
                  pl.BlockSpec((1, s, LANES), lambda bb, i: (bb, 0, 0)),
                  pl.BlockSpec((1, s, D_LAT), lambda bb, i: (bb, 0, 0)),
                  pl.BlockSpec((n_heads, HEAD_DIM, D_LAT), lambda bb, i: (0, 0, 0)),
                  pl.BlockSpec((n_heads, D_LAT, HEAD_DIM), lambda bb, i: (0, 0, 0)),
                  pl.BlockSpec((1, D_LAT), lambda bb, i: (0, 0))],
        out_specs=pl.BlockSpec((1, tq, width), lambda bb, i: (bb, i, 0)),
        out_shape=jax.ShapeDtypeStruct((b, s, width), MXU_DTYPE),
        scratch_shapes=[pltpu.VMEM((s, tq), F32),
                        pltpu.VMEM((nkt, tq, TK), F32),
                        pltpu.VMEM((nkt, n_heads * tq, TK), F32),
                        pltpu.VMEM((n_heads * tq, D_LAT), MXU_DTYPE),
                        pltpu.VMEM((N_IDX_HEADS, tq, D_IDX), MXU_DTYPE),
                        pltpu.VMEM((16, tq), F32),
                        pltpu.VMEM((n_heads * tq, LANES), F32),
                        pltpu.VMEM((n_heads * tq, LANES), F32),
                        pltpu.VMEM((n_heads * tq, D_LAT), F32)],
        compiler_params=_cparams(("parallel", "arbitrary")),
        name="dsa",
    )(proj3, proj3, misc3, misc3, kv, w_uk, w_uv, g_qa)


TQ_B = 256


HG_B = 8
assert TQ_B == TK


def _split3(x):
    p1 = x.astype(MXU_DTYPE)
    r1 = x - p1.astype(F32)
    p2 = r1.astype(MXU_DTYPE)
    p3 = (r1 - p2.astype(F32)).astype(MXU_DTYPE)
    return p1, p2, p3


def _fox_kernel(q_ref, k_ref, v_ref, cumq_ref, cumk_ref, gq_ref, gk_ref, o_ref,
                kaug_scr, qaug_scr, logit_scr, mpart_scr, lpart_scr, oacc_scr):
    g = pl.program_id(1)
    qi = pl.program_id(2)
    tq = TQ_B
    s = k_ref.shape[1]
    cscale = (HEAD_DIM ** -0.5) * LOG2E

    rr = lax.broadcasted_iota(I32, (3 * LANES, LANES), 0)
    cc = lax.broadcasted_iota(I32, (3 * LANES, LANES), 1)
    lane_row = lax.broadcasted_iota(I32, (1, LANES), 1)
    ones_q = jnp.where((lane_row >= 3) & (lane_row < 6), 1.0, 0.0)
    ones_k = jnp.where(lane_row < 3, 1.0, 0.0)
    base_q = jnp.where(cc < 3, rr - LANES * cc, -1)
    base_k = jnp.where((cc >= 3) & (cc < 6), rr - LANES * (cc - 3), -1)

    def aug(pieces, h, base, sign, ones_row):
        e = jnp.where(base == h, sign, 0.0).astype(MXU_DTYPE)
        return (_dot(pieces, e) + ones_row).astype(MXU_DTYPE)

    @pl.when(qi == 0)
    def _():
        def kbody(c, carry):
            r0 = pl.multiple_of(c * TK, TK)
            ck = jnp.concatenate(_split3(cumk_ref[0, pl.ds(r0, TK), :]), axis=1)
            for hh in range(HG_B):
                kn = _rms(k_ref[0, pl.ds(r0, TK), hh * HEAD_DIM:(hh + 1) * HEAD_DIM].astype(F32), gk_ref[...])
                kaug_scr[hh, pl.ds(r0, TK), 0:HEAD_DIM] = kn.astype(MXU_DTYPE)
                kaug_scr[hh, pl.ds(r0, TK), HEAD_DIM:] = aug(ck, MISC_F + g * HG_B + hh, base_k, -1.0, ones_k)
            return carry

        lax.fori_loop(0, s // TK, kbody, 0)

    cq = jnp.concatenate(_split3(cumq_ref[0]), axis=1)
    for hh in range(HG_B):
        qn = _rms(q_ref[0, :, hh * HEAD_DIM:(hh + 1) * HEAD_DIM].astype(F32), gq_ref[...])
        qaug_scr[hh, :, 0:HEAD_DIM] = qn.astype(MXU_DTYPE)
        qaug_scr[hh, :, HEAD_DIM:] = aug(cq, MISC_F + g * HG_B + hh, base_q, 1.0, ones_q)
    mpart_scr[...] = jnp.full(mpart_scr.shape, NEG_BIG, F32)
    lpart_scr[...] = jnp.zeros(lpart_scr.shape, F32)
    oacc_scr[...] = jnp.zeros(oacc_scr.shape, F32)

    def put_logits(kt, hh, lg):
        logit_scr[kt, hh] = lg
        mpart_scr[hh] = jnp.maximum(mpart_scr[hh], jnp.maximum(lg[:, :LANES], lg[:, LANES:]))

    def raw_logits(kt, nt, hh):
        k0 = pl.multiple_of(kt * TK, TK)
        return _dot_nt(qaug_scr[hh], kaug_scr[hh, pl.ds(k0, nt * TK), :]) * cscale

    def off_tiles(kt, nt):
        for hh in range(HG_B):
            lg = raw_logits(kt, nt, hh)
            for t in range(nt):
                put_logits(kt + t, hh, lg[:, t * TK:(t + 1) * TK])

    _pair_loop(qi, off_tiles)
    causal = (lax.broadcasted_iota(I32, (tq, TK), 1) <= lax.broadcasted_iota(I32, (tq, TK), 0))
    for hh in range(HG_B):
        put_logits(qi, hh, jnp.where(causal, raw_logits(qi, 1, hh), NEG_BIG))

    for hh in range(HG_B):
        m = jnp.max(mpart_scr[hh], axis=-1, keepdims=True)
        mpart_scr[hh] = jnp.broadcast_to(m, (tq, LANES))

    def pv_tiles(kt, nt):
        k0 = pl.multiple_of(kt * TK, TK)
        for hh in range(HG_B):
            mb = mpart_scr[hh]
            mb2 = jnp.concatenate([mb, mb], axis=1)
            ps = [jnp.exp2(logit_scr[kt + t, hh] - mb2) for t in range(nt)]
            lsum = ps[0][:, :LANES] + ps[0][:, LANES:]
            for p in ps[1:]:
                lsum = lsum + p[:, :LANES] + p[:, LANES:]
            lpart_scr[hh] += lsum
            p_all = ps[0] if nt == 1 else jnp.concatenate(ps, axis=1)
            oacc_scr[hh] += _dot(p_all.astype(MXU_DTYPE),
                                 v_ref[0, pl.ds(k0, nt * TK), hh * HEAD_DIM:(hh + 1) * HEAD_DIM])

    _pair_loop(qi + 1, pv_tiles)
    for hh in range(HG_B):
        lsum = jnp.sum(lpart_scr[hh], axis=-1, keepdims=True)
        o_ref[0, :, hh * HEAD_DIM:(hh + 1) * HEAD_DIM] = (oacc_scr[hh] / lsum).astype(o_ref.dtype)


def _fox(proj3, cums, g_qb, g_kb, n_heads):
    b, s, _ = proj3.shape
    tq = TQ_B
    gw = HG_B * HEAD_DIM
    return pl.pallas_call(
        _fox_kernel,
        grid=(b, n_heads // HG_B, s // tq),
        in_specs=[pl.BlockSpec((1, tq, gw), lambda bb, g, i: (bb, i, U_QB // HG_B + g)),
                  pl.BlockSpec((1, s, gw), lambda bb, g, i: (bb, 0, U_KB // HG_B + g)),
                  pl.BlockSpec((1, s, gw), lambda bb, g, i: (bb, 0, U_VB // HG_B + g)),
                  pl.BlockSpec((1, tq, LANES), lambda bb, g, i: (bb, i, 0)),
                  pl.BlockSpec((1, s, LANES), lambda bb, g, i: (bb, 0, 0)),
                  pl.BlockSpec((1, HEAD_DIM), lambda bb, g, i: (0, 0)),
                  pl.BlockSpec((1, HEAD_DIM), lambda bb, g, i: (0, 0))],
        out_specs=pl.BlockSpec((1, tq, gw), lambda bb, g, i: (bb, i, g)),
        out_shape=jax.ShapeDtypeStruct((b, s, n_heads * HEAD_DIM), MXU_DTYPE),
        scratch_shapes=[pltpu.VMEM((HG_B, s, 2 * HEAD_DIM), MXU_DTYPE),
                        pltpu.VMEM((HG_B, tq, 2 * HEAD_DIM), MXU_DTYPE),
                        pltpu.VMEM((s // TK, HG_B, tq, TK), F32),
                        pltpu.VMEM((HG_B, tq, LANES), F32),
                        pltpu.VMEM((HG_B, tq, LANES), F32),
                        pltpu.VMEM((HG_B, tq, HEAD_DIM), F32)],
        compiler_params=_cparams(("parallel", "parallel", "arbitrary")),
        name="fox",
    )(proj3, proj3, proj3, cums, cums, g_qb, g_kb)


TM_MERGE = 256


def _merge_kernel(ga_ref, gb_ref, oa_ref, ob_ref, x_ref, mod_ref, bga_ref, bgb_ref, wpa_ref, wpb_ref,
                  wo_ref, gffn_ref, wrh_ref, wrp_ref, br_ref, x1_ref, h2_ref, rl_ref):
    ga = jax.nn.sigmoid(ga_ref[...].astype(F32) + bga_ref[...])
    gb = jax.nn.sigmoid(gb_ref[...].astype(F32) + bgb_ref[...])
    merged = ga * _dot(oa_ref[...], wpa_ref[...]) + gb * _dot(ob_ref[...], wpb_ref[...])
    upd = _dot(merged.astype(MXU_DTYPE), wo_ref[...])
    x1 = x_ref[...] + mod_ref[0, 2:3, :] * upd
    x1_ref[...] = x1
    h2 = _rms(x1, gffn_ref[...]) * (1.0 + mod_ref[0, 4:5, :]) + mod_ref[0, 3:4, :]
    h2_ref[...] = h2
    hh = h2.astype(MXU_DTYPE)
    hl = (h2 - hh.astype(F32)).astype(MXU_DTYPE)
    both = _dot(hh, wrp_ref[...])
    rl_ref[...] = both[:, :LANES] + both[:, LANES:] + _dot(hl, wrh_ref[...]) + br_ref[...]


def _merge(proj, o_a, o_b, x2, mod3, b_gate, w_pa, w_pb, w_o, g_ffn, wr_hi, wr_pair, b_r, seq):
    t, d = x2.shape
    tm = TM_MERGE
    per_b = seq // tm
    wa = o_a.shape[1]
    res = lambda shape: pl.BlockSpec(shape, lambda i: (0,) * len(shape), pipeline_mode=pl.Buffered(1))
    return pl.pallas_call(
        _merge_kernel,
        grid=(t // tm,),
        in_specs=[pl.BlockSpec((tm, d), lambda i: (i, 0)),
                  pl.BlockSpec((tm, d), lambda i: (i, 1)),
                  pl.BlockSpec((tm, wa), lambda i: (i, 0)),
                  pl.BlockSpec((tm, wa), lambda i: (i, 0)),
                  pl.BlockSpec((tm, d), lambda i: (i, 0)),
                  pl.BlockSpec((1, 6, d), lambda i: (i // per_b, 0, 0)),
                  pl.BlockSpec((1, d), lambda i: (0, 0)),
                  pl.BlockSpec((1, d), lambda i: (0, 1)),
                  res((wa, d)), res((wa, d)), res((d, d)),
                  pl.BlockSpec((1, d), lambda i: (0, 0)),
                  res((d, LANES)), res((d, 2 * LANES)),
                  pl.BlockSpec((1, LANES), lambda i: (0, 0))],
        out_specs=[pl.BlockSpec((tm, d), lambda i: (i, 0)),
                   pl.BlockSpec((tm, d), lambda i: (i, 0)),
                   pl.BlockSpec((tm, LANES), lambda i: (i, 0))],
        out_shape=[jax.ShapeDtypeStruct((t, d), F32),
                   jax.ShapeDtypeStruct((t, d), F32),
                   jax.ShapeDtypeStruct((t, LANES), F32)],
        compiler_params=_cparams(("parallel",)),
        name="merge",
    )(proj, proj, o_a, o_b, x2, mod3, b_gate, b_gate, w_pa, w_pb, w_o, g_ffn, wr_hi, wr_pair, b_r)


TM_ROUTE = 1024
TM_ROWS = 128
R_E0, R_E1, R_P0, R_P1 = 0, 1, 4, 5
IT_TILE, IT_E, IT_LO, IT_HI, IT_FLAG, IT_NEXT, IT_NEXT2, IT_SLOT = range(8)
F_VALID, F_FIRST_OF_EXPERT, F_FIRST_OF_TILE = 1, 2, 4


def _route_kernel(rl_ref, route_ref, post_ref, items_ref, cnt_scr, run_scr, offs_scr, *, n_tiles):
    tm = rl_ref.shape[0]
    sweep = pl.program_id(0)
    step = pl.program_id(1)
    iw = items_ref.shape[1]

    @pl.when((sweep == 0) & (step == 0))
    def _():
        cnt_scr[...] = jnp.zeros(cnt_scr.shape, F32)

    r = rl_ref[...]
    lane = lax.broadcasted_iota(I32, (tm, LANES), 1).astype(F32)
    neg_inf = -jnp.inf
    gmask = lane < N_GROUPS
    gl = jnp.where(gmask, r, neg_inf)
    gmax = jnp.max(gl, axis=-1, keepdims=True)
    gidx = jnp.min(jnp.where(gl == gmax, lane, float(LANES)), axis=-1, keepdims=True)
    gsum = jnp.sum(jnp.where(gmask, jnp.exp(r - gmax), 0.0), axis=-1, keepdims=True)
    gw = 1.0 / gsum
    lo = N_GROUPS + EXPERTS_PER_GROUP * gidx
    emask = (lane >= lo) & (lane < lo + EXPERTS_PER_GROUP)
    el = jnp.where(emask, r, neg_inf)
    v0 = jnp.max(el, axis=-1, keepdims=True)
    i0 = jnp.min(jnp.where(el == v0, lane, float(LANES)), axis=-1, keepdims=True)
    el2 = jnp.where(lane == i0, neg_inf, el)
    v1 = jnp.max(el2, axis=-1, keepdims=True)
    i1 = jnp.min(jnp.where(el2 == v1, lane, float(LANES)), axis=-1, keepdims=True)
    tt = jnp.exp(v1 - v0)
    p0 = gw / (1.0 + tt)
    p1 = gw * tt / (1.0 + tt)
    e0 = i0 - N_GROUPS
    e1 = i1 - N_GROUPS

    hit0 = lane == e0
    hit1 = lane == e1
    oh = jnp.where(hit0 | hit1, 1.0, 0.0)

    @pl.when(sweep == 0)
    def _():
        ones = jnp.ones((tm, LANES), MXU_DTYPE)
        cnt_scr[...] += _dot(oh.T.astype(MXU_DTYPE), ones)

    @pl.when((sweep == 1) & (step == 0))
    def _():
        _plan_items(cnt_scr[...], items_ref, offs_scr, n_tiles, iw)
        run_scr[...] = jnp.zeros(run_scr.shape, F32)

    @pl.when(sweep == 1)
    def _():
        rr = lax.broadcasted_iota(I32, (tm, tm), 0)
        cc = lax.broadcasted_iota(I32, (tm, tm), 1)
        ltri = jnp.where(cc < rr, 1.0, 0.0).astype(MXU_DTYPE)
        before = _dot(ltri, oh.astype(MXU_DTYPE)) + run_scr[...] + offs_scr[...]
        pos0 = jnp.sum(jnp.where(hit0, before, 0.0), axis=-1, keepdims=True)
        pos1 = jnp.sum(jnp.where(hit1, before, 0.0), axis=-1, keepdims=True)
        run_scr[...] = run_scr[...] + jnp.sum(oh, axis=0, keepdims=True)
        out = jnp.zeros((tm, LANES), F32)
        for k, val in ((R_E0, e0), (R_E1, e1), (R_P0, p0), (R_P1, p1)):
            out = jnp.where(lane == k, val, out)
        route_ref[...] = out
        pmat = jnp.where(lane == 0.0, pos0, jnp.where(lane == 1.0, pos1, 0.0))
        post_ref[...] = pmat.T[0:8, :].astype(I32)


def _plan_items(cnt_col, items_ref, offs_scr, n_tiles, iw):
    tmr = float(TM_ROWS)
    sub = lax.broadcasted_iota(I32, (LANES, LANES), 0)
    lan = lax.broadcasted_iota(I32, (LANES, LANES), 1)
    lstrict = jnp.where(lan < sub, 1.0, 0.0).astype(MXU_DTYPE)
    hi = jnp.floor(cnt_col * (1.0 / LANES))
    lo = cnt_col - hi * LANES
    offs_col = _dot(lstrict, hi.astype(MXU_DTYPE)) * LANES + _dot(lstrict, lo.astype(MXU_DTYPE))
    first_t = jnp.floor(offs_col * (1.0 / tmr))
    last_t = jnp.floor((offs_col + cnt_col - 1.0) * (1.0 / tmr))
    n_col = jnp.where(cnt_col > 0.0, last_t - first_t + 1.0, 0.0)
    base_col = _dot(lstrict, n_col.astype(MXU_DTYPE))
    end_col = base_col + n_col
    offs_scr[...] = offs_col.T[0:1, :]

    rep = lambda col: jnp.concatenate([col] * (iw // LANES), axis=1)
    wl = lax.broadcasted_iota(I32, (LANES, iw), 1).astype(F32)
    sub_f = lax.broadcasted_iota(I32, (LANES, iw), 0).astype(F32)
    end_r = rep(end_col)
    w_total = end_r[LANES - 1:LANES, :]
    count_le = lambda v: jnp.sum(jnp.where(end_r <= v, 1.0, 0.0), axis=0, keepdims=True)
    w_row = wl[0:1, :]
    eidx = count_le(w_row)
    sel = sub_f == eidx
    pick = lambda col: jnp.sum(jnp.where(sel, rep(col), 0.0), axis=0, keepdims=True)
    e_base, e_first, e_offs, e_cnt, e_end = (pick(base_col), pick(first_t), pick(offs_col),
                                             pick(cnt_col), pick(end_col))
    valid = w_row < w_total
    tile = e_first + (w_row - e_base)
    row_lo = jnp.maximum(e_offs, tile * tmr) - tile * tmr
    row_hi = jnp.minimum(e_offs + e_cnt, (tile + 1.0) * tmr) - tile * tmr
    first_e = w_row == e_base
    flags = jnp.where(valid, F_VALID + jnp.where(first_e, float(F_FIRST_OF_EXPERT), 0.0)
                      + jnp.where(row_lo == 0.0, float(F_FIRST_OF_TILE), 0.0), 0.0)
    has1 = valid & first_e & (e_end < w_total)
    e1 = count_le(e_end)
    end1 = jnp.sum(jnp.where(sub_f == e1, end_r, 0.0), axis=0, keepdims=True)
    has2 = has1 & (end1 < w_total)
    nxt = jnp.where(has1, e1, -1.0)
    nxt2 = jnp.where(has2, count_le(end1), -1.0)
    ord_col = _dot(lstrict, jnp.where(cnt_col > 0.0, 1.0, 0.0).astype(MXU_DTYPE))
    e_ord = pick(ord_col)
    slot = e_ord - 2.0 * jnp.floor(e_ord * 0.5)
    e_last = count_le(w_total - 1.0)
    rows = {IT_TILE: jnp.where(valid, tile, n_tiles - 1.0),
            IT_E: jnp.where(valid, eidx, e_last),
            IT_LO: jnp.where(valid, row_lo, 0.0),
            IT_HI: jnp.where(valid, row_hi, 0.0),
            IT_FLAG: flags,
            IT_NEXT: nxt,
            IT_NEXT2: nxt2,
            IT_SLOT: jnp.where(valid, slot, 0.0)}
    sub8 = lax.broadcasted_iota(I32, (8, iw), 0)
    out = jnp.zeros((8, iw), F32)
    for k, val in rows.items():
        out = jnp.where(sub8 == k, val, out)
    items_ref[...] = out.astype(I32)


def _route(rlog, n_items):
    t = rlog.shape[0]
    tm = min(TM_ROUTE, t)
    iw = ((n_items + LANES - 1) // LANES) * LANES
    kern = functools.partial(_route_kernel, n_tiles=2 * t // TM_ROWS)
    return pl.pallas_call(
        kern,
        grid=(2, t // tm),
        in_specs=[pl.BlockSpec((tm, LANES), lambda p, i: (i, 0))],
        out_specs=[pl.BlockSpec((tm, LANES), lambda p, i: (i * p, 0)),
                   pl.BlockSpec((8, tm), lambda p, i: (0, i * p)),
                   pl.BlockSpec((8, iw), lambda p, i: (0, 0))],
        out_shape=[jax.ShapeDtypeStruct((t, LANES), F32),
                   jax.ShapeDtypeStruct((8, t), I32),
                   jax.ShapeDtypeStruct((8, iw), I32)],
        scratch_shapes=[pltpu.VMEM((LANES, LANES), F32),
                        pltpu.VMEM((1, LANES), F32),
                        pltpu.VMEM((1, LANES), F32)],
        compiler_params=_cparams(("arbitrary", "arbitrary")),
        name="route",
    )(rlog)


TM_DISP = 512


def _dispatch_kernel(pos0_ref, pos1_ref, h2_ref, xs_ref, sem):
    base = pl.program_id(0) * TM_DISP

    def row_copy(r, p):
        return pltpu.make_async_copy(h2_ref.at[pl.ds(r, 1), :], xs_ref.at[pl.ds(p, 1), :], sem)

    def issue(r, carry):
        row_copy(r, pos0_ref[base + r]).start(priority=0)
        row_copy(r, pos1_ref[base + r]).start(priority=1)
        return carry

    lax.fori_loop(0, TM_DISP, issue, 0, unroll=8)

    def drain(r, carry):
        row_copy(0, 0).wait()
        row_copy(0, 0).wait()
        return carry

    lax.fori_loop(0, TM_DISP, drain, 0, unroll=8)


def _dispatch(pos0, pos1, h2):
    t, d = h2.shape
    return pl.pallas_call(
        _dispatch_kernel,
        grid_spec=pltpu.PrefetchScalarGridSpec(
            num_scalar_prefetch=2,
            grid=(t // TM_DISP,),
            in_specs=[pl.BlockSpec((TM_DISP, d), lambda i, p0, p1: (i, 0))],
            out_specs=pl.BlockSpec(memory_space=pl.ANY),
            scratch_shapes=[pltpu.SemaphoreType.DMA(())]),
        out_shape=jax.ShapeDtypeStruct((2 * t, d), h2.dtype),
        compiler_params=pltpu.CompilerParams(dimension_semantics=("arbitrary",),
                                             vmem_limit_bytes=VMEM_LIMIT, has_side_effects=True),
        name="dispatch",
    )(pos0, pos1, h2)


WEIGHT_DMA_PRIORITY = 1


def _experts_kernel(tile_ref, e_ref, lo_ref, hi_ref, flag_ref, next_ref, next2_ref, slot_ref,
                    xs_ref, w1_hbm, w3_hbm, w2_hbm, ys_ref,
                    f1_scr, f3_scr, f2_scr, w1_scr, w3_scr, w2_scr, sems):
    del tile_ref
    w = pl.program_id(0)
    flag = flag_ref[w]

    mats = ((w1_hbm, f1_scr, w1_scr), (w3_hbm, f3_scr, w3_scr), (w2_hbm, f2_scr, w2_scr))

    def weight_copy(k, e, slot):
        return pltpu.make_async_copy(mats[k][0].at[e], mats[k][1].at[slot], sems.at[slot, k])

    @pl.when(w == 0)
    def _():
        for k in range(3):
            weight_copy(k, e_ref[0], 0).start(priority=WEIGHT_DMA_PRIORITY)
        nxt = next_ref[0]

        @pl.when(nxt >= 0)
        def _():
            for k in range(3):
                weight_copy(k, nxt, 1).start(priority=WEIGHT_DMA_PRIORITY)

    @pl.when((flag & F_FIRST_OF_EXPERT) != 0)
    def _():
        slot = slot_ref[w]
        nxt2 = next2_ref[w]
        for k in range(3):
            weight_copy(k, e_ref[w], slot).wait()
            mats[k][2][...] = mats[k][1][slot].astype(mats[k][2].dtype)

            @pl.when(nxt2 >= 0)
            def _():
                weight_copy(k, nxt2, slot).start(priority=WEIGHT_DMA_PRIORITY)

    @pl.when((flag & F_VALID) != 0)
    def _():
        x = xs_ref[...].astype(MXU_DTYPE)
        a = _dot(x, w1_scr[...])
        u = _dot(x, w3_scr[...])
        hm = (a * jax.nn.sigmoid(a)) * u
        res = _dot(hm.astype(MXU_DTYPE), w2_scr[...])
        row = lax.broadcasted_iota(I32, res.shape, 0)
        mine = (row >= lo_ref[w]) & (row < hi_ref[w])

        @pl.when((flag & F_FIRST_OF_TILE) != 0)
        def _():
            ys_ref[...] = jnp.where(mine, res, 0.0)

        @pl.when((flag & F_FIRST_OF_TILE) == 0)
        def _():
            ys_ref[...] = jnp.where(mine, res, ys_ref[...])


def _experts(items, xs, w1, w3, w2, n_items):
    n_rows, d = xs.shape
    f = w1.shape[2]
    tm = TM_ROWS
    tile_map = lambda w, tile, *_: (tile[w], 0)
    return pl.pallas_call(
        _experts_kernel,
        grid_spec=pltpu.PrefetchScalarGridSpec(
            num_scalar_prefetch=8,
            grid=(n_items,),
            in_specs=[pl.BlockSpec((tm, d), tile_map),
                      pl.BlockSpec(memory_space=pl.ANY),
                      pl.BlockSpec(memory_space=pl.ANY),
                      pl.BlockSpec(memory_space=pl.ANY)],
            out_specs=pl.BlockSpec((tm, d), tile_map),
            scratch_shapes=[pltpu.VMEM((2, d, f), F32),
                            pltpu.VMEM((2, d, f), F32),
                            pltpu.VMEM((2, f, d), F32),
                            pltpu.VMEM((d, f), MXU_DTYPE),
                            pltpu.VMEM((d, f), MXU_DTYPE),
                            pltpu.VMEM((f, d), MXU_DTYPE),
                            pltpu.SemaphoreType.DMA((2, 3))]),
        out_shape=jax.ShapeDtypeStruct((n_rows, d), F32),
        compiler_params=_cparams(("arbitrary",)),
        name="experts",
    )(items[IT_TILE], items[IT_E], items[IT_LO], items[IT_HI], items[IT_FLAG], items[IT_NEXT],
      items[IT_NEXT2], items[IT_SLOT], xs, w1, w3, w2)


TM_COMB = 512


def _combine_kernel(pos0_ref, pos1_ref, ys_ref, x1_ref, route_ref, mod_ref, o_ref, y0_scr, y1_scr, sems):
    step = pl.program_id(0)
    n_steps = pl.num_programs(0)

    def row_copy(p, dst, slot, r):
        return pltpu.make_async_copy(ys_ref.at[pl.ds(p, 1), :], dst.at[slot, pl.ds(r, 1), :], sems.at[slot])

    def issue_step(s):
        slot = s % 2
        base = s * TM_COMB

        def issue(r, carry):
            row_copy(pos0_ref[base + r], y0_scr, slot, r).start(priority=0)
            row_copy(pos1_ref[base + r], y1_scr, slot, r).start(priority=1)
            return carry

        lax.fori_loop(0, TM_COMB, issue, 0, unroll=8)

    @pl.when(step == 0)
    def _():
        issue_step(step)

    @pl.when(step + 1 < n_steps)
    def _():
        issue_step(step + 1)

    slot = step % 2

    def drain(r, carry):
        row_copy(0, y0_scr, slot, 0).wait()
        row_copy(0, y1_scr, slot, 0).wait()
        return carry

    lax.fori_loop(0, TM_COMB, drain, 0, unroll=8)

    rt = route_ref[...]
    lane = lax.broadcasted_iota(I32, rt.shape, 1)
    p0 = jnp.sum(jnp.where(lane == R_P0, rt, 0.0), axis=-1, keepdims=True)
    p1 = jnp.sum(jnp.where(lane == R_P1, rt, 0.0), axis=-1, keepdims=True)
    y = p0 * y0_scr[slot] + p1 * y1_scr[slot]
    o_ref[...] = x1_ref[...] + mod_ref[0, 5:6, :] * y


def _combine(pos0, pos1, ys, x1, route, mod3, seq):
    t, d = x1.shape
    tm = TM_COMB
    per_b = seq // tm
    return pl.pallas_call(
        _combine_kernel,
        grid_spec=pltpu.PrefetchScalarGridSpec(
            num_scalar_prefetch=2,
            grid=(t // tm,),
            in_specs=[pl.BlockSpec(memory_space=pl.ANY),
                      pl.BlockSpec((tm, d), lambda i, p0, p1: (i, 0)),
                      pl.BlockSpec((tm, LANES), lambda i, p0, p1: (i, 0)),
                      pl.BlockSpec((1, 6, d), lambda i, p0, p1: (i // per_b, 0, 0))],
            out_specs=pl.BlockSpec((tm, d), lambda i, p0, p1: (i, 0)),
            scratch_shapes=[pltpu.VMEM((2, tm, d), F32),
                            pltpu.VMEM((2, tm, d), F32),
                            pltpu.SemaphoreType.DMA((2,))]),
        out_shape=jax.ShapeDtypeStruct((t, d), F32),
        compiler_params=_cparams(("arbitrary",)),
        name="combine",
    )(pos0, pos1, ys, x1, route, mod3)


def kernel(x, c, w_ada, b_ada, g_mix, w_in, b_gate, b_forget, w_uk, w_uv, g_qa, g_kv, g_qb, g_kb,
           w_pa, w_pb, w_o, g_ffn, w_rg, b_rg, w_re, b_re, w1, w3, w2):
    b, s, d = x.shape
    depth = w_ada.shape[0]
    t = b * s
    n_heads_b = b_forget.shape[1]
    n_items = 2 * t // TM_ROWS + N_EXPERTS
    c8 = jnp.zeros((8, d), F32).at[:b].set(c)
    x2 = x.reshape(t, d)
    row = lambda v: v.reshape(1, -1)
    for l in range(depth):
        mod3 = _ada(c8, w_ada[l], row(b_ada[l]))[:b].reshape(b, 6, d)
        proj, misc = _inproj(x2, mod3, row(g_mix[l]), _pack_w_in(jnp.transpose(w_in[l])), s)
        proj3 = proj.reshape(b, s, NP_COLS)
        misc3 = misc.reshape(b, s, LANES)
        bf_row = jnp.zeros((1, LANES), F32).at[0, MISC_F:MISC_F + n_heads_b].set(b_forget[l])
        kv, cums = _prep(proj3, misc3, row(g_kv[l]), bf_row)
        o_a = _dsa(proj3, misc3, kv, w_uk[l].astype(MXU_DTYPE), w_uv[l].astype(MXU_DTYPE), row(g_qa[l]))
        o_b = _fox(proj3, cums, row(g_qb[l]), row(g_kb[l]), n_heads_b)

        w_r = jnp.zeros((d, LANES), F32).at[:, :N_GROUPS].set(w_rg[l])
        w_r = w_r.at[:, N_GROUPS:N_GROUPS + N_EXPERTS].set(w_re[l])
        wr_hi = w_r.astype(MXU_DTYPE)
        wr_pair = jnp.concatenate([wr_hi, (w_r - wr_hi.astype(F32)).astype(MXU_DTYPE)], axis=1)
        b_r = jnp.zeros((1, LANES), F32).at[0, :N_GROUPS].set(b_rg[l])
        b_r = b_r.at[0, N_GROUPS:N_GROUPS + N_EXPERTS].set(b_re[l])
        x1, h2, rlog = _merge(proj, o_a.reshape(t, -1), o_b.reshape(t, -1), x2, mod3, row(b_gate[l]),
                              w_pa[l].astype(MXU_DTYPE), w_pb[l].astype(MXU_DTYPE),
                              w_o[l].astype(MXU_DTYPE), row(g_ffn[l]), wr_hi, wr_pair, b_r, s)
        route, pos_t, items = _route(rlog, n_items)
        pos0, pos1 = pos_t[0], pos_t[1]
        xs = _dispatch(pos0, pos1, h2)
        ys = _experts(items, xs, w1[l], w3[l], w2[l], n_items)
        x2 = _combine(pos0, pos1, ys, x1, route, mod3, s)
    return x2.reshape(b, s, d)
```

```python
import functools

import jax
import jax.numpy as jnp
from jax import lax
from jax.experimental import pallas as pl
from jax.experimental.pallas import tpu as pltpu

F32 = jnp.float32
I32 = jnp.int32
MXU_DTYPE = jnp.bfloat16

CHUNK = 64
HEAD_DIM = 128
D_LAT = 256
N_IDX_HEADS = 16
D_IDX = 64
TOPK_MAX = 256
N_GROUPS = 8
EXPERTS_PER_GROUP = 8
N_EXPERTS = N_GROUPS * EXPERTS_PER_GROUP
RMS_EPS = 1e-6

LANES = 128
VMEM_LIMIT = 56 * 1024 * 1024

NEG_BIG = -1e30
INT_MIN = -2147483648


def _cparams(sem):
    return pltpu.CompilerParams(dimension_semantics=sem, vmem_limit_bytes=VMEM_LIMIT)


def _dot(a, b):
    return jnp.dot(a, b, preferred_element_type=F32)


def _dot_nt(a, b):
    return lax.dot_general(a, b, (((1,), (1,)), ((), ())), preferred_element_type=F32)


def _rms(x, g):
    return x * lax.rsqrt(jnp.mean(x * x, axis=-1, keepdims=True) + RMS_EPS) * g


def _ada_kernel(c_ref, w_ref, b_ref, o_ref):
    c = c_ref[...]
    a = c * jax.nn.sigmoid(c)
    o_ref[...] = _dot(a.astype(MXU_DTYPE), w_ref[...].astype(MXU_DTYPE)) + b_ref[...]


def _ada(c8, w_ada, b_ada):
    d, n = w_ada.shape
    tn = 1024
    return pl.pallas_call(
        _ada_kernel,
        grid=(n // tn,),
        in_specs=[pl.BlockSpec((8, d), lambda j: (0, 0)),
                  pl.BlockSpec((d, tn), lambda j: (0, j)),
                  pl.BlockSpec((1, tn), lambda j: (0, j))],
        out_specs=pl.BlockSpec((8, tn), lambda j: (0, j)),
        out_shape=jax.ShapeDtypeStruct((8, n), F32),
        compiler_params=_cparams(("arbitrary",)),
        name="ada",
    )(c8, w_ada, b_ada)


NP_COLS = 76 * LANES
U_QA, U_QIDX, U_QB, U_KB, U_VB, U_CKV, U_MISC = 32, 40, 48, 56, 64, 72, 74
TN_PROJ = 4 * LANES
TM_PROJ = 1024
MISC_TILE = (U_MISC * LANES) // TN_PROJ
MISC_OFF = U_MISC * LANES - MISC_TILE * TN_PROJ
MISC_K, MISC_F, MISC_W = 0, 64, 72


def _pack_moves():
    sizes = (1024, 256, 1024, 64, 16, 1024, 1024, 1024, 8, 4096)
    src = [0]
    for n in sizes:
        src.append(src[-1] + n)
    q_a, c_kv, q_idx, k_idx, w_idx, q_b, k_b, v_b, f_b, gate = src[:10]
    m = U_MISC * LANES
    return ((gate, 0, 4096), (q_a, U_QA * LANES, 1024), (q_idx, U_QIDX * LANES, 1024),
            (q_b, U_QB * LANES, 1024), (k_b, U_KB * LANES, 1024), (v_b, U_VB * LANES, 1024),
            (c_kv, U_CKV * LANES, 256), (f_b, m + MISC_F, 8), (w_idx, m + MISC_W, 16),
            (k_idx, m + MISC_K, 64))


def _pack_kernel(w_ref, o_ref):
    cols = o_ref.shape[1]
    m = U_MISC * LANES
    o_ref[m:m + 2 * LANES, :] = jnp.zeros((2 * LANES, cols), o_ref.dtype)
    for src, dst, n in _pack_moves():
        o_ref[dst:dst + n, :] = w_ref[src:src + n, :].astype(o_ref.dtype)


def _pack_w_in(w_in_t):
    n_in, d = w_in_t.shape
    tc = 256
    return pl.pallas_call(
        _pack_kernel,
        grid=(d // tc,),
        in_specs=[pl.BlockSpec((n_in, tc), lambda i: (0, i))],
        out_specs=pl.BlockSpec((NP_COLS, tc), lambda i: (0, i)),
        out_shape=jax.ShapeDtypeStruct((NP_COLS, d), MXU_DTYPE),
        compiler_params=_cparams(("parallel",)),
        name="pack",
    )(w_in_t)


def _inproj_kernel(x_ref, mod_ref, g_ref, w_ref, o_ref, misc_ref, h_scr):
    j = pl.program_id(1)

    @pl.when(j == 0)
    def _():
        y = _rms(x_ref[...], g_ref[...])
        sh = mod_ref[0, 0:1, :]
        sc = mod_ref[0, 1:2, :]
        h_scr[...] = (y * (1.0 + sc) + sh).astype(h_scr.dtype)

    acc = _dot_nt(h_scr[...], w_ref[...])
    o_ref[...] = acc.astype(o_ref.dtype)

    @pl.when(j == MISC_TILE)
    def _():
        misc_ref[...] = acc[:, MISC_OFF:MISC_OFF + LANES]


def _inproj(x2, mod3, g_mix, w_packed, seq):
    t, d = x2.shape
    tm = min(TM_PROJ, seq)
    per_b = seq // tm
    return pl.pallas_call(
        _inproj_kernel,
        grid=(t // tm, NP_COLS // TN_PROJ),
        in_specs=[pl.BlockSpec((tm, d), lambda i, j: (i, 0)),
                  pl.BlockSpec((1, 6, d), lambda i, j: (i // per_b, 0, 0)),
                  pl.BlockSpec((1, d), lambda i, j: (0, 0)),
                  pl.BlockSpec((TN_PROJ, d), lambda i, j: (j, 0))],
        out_specs=[pl.BlockSpec((tm, TN_PROJ), lambda i, j: (i, j)),
                   pl.BlockSpec((tm, LANES), lambda i, j: (i, 0))],
        out_shape=[jax.ShapeDtypeStruct((t, NP_COLS), MXU_DTYPE),
                   jax.ShapeDtypeStruct((t, LANES), F32)],
        scratch_shapes=[pltpu.VMEM((tm, d), MXU_DTYPE)],
        compiler_params=_cparams(("parallel", "arbitrary")),
        name="inproj",
    )(x2, mod3, g_mix, w_packed)


TK = 256


def _prep_kernel(ckv_ref, misc_ref, gkv_ref, bf_ref, kv_ref, cum_ref):
    s = ckv_ref.shape[1]
    kv_ref[0] = _rms(ckv_ref[0].astype(F32), gkv_ref[...]).astype(kv_ref.dtype)

    r = lax.broadcasted_iota(I32, (LANES, LANES), 0)
    c = lax.broadcasted_iota(I32, (LANES, LANES), 1)
    tri = jnp.where(c <= r, 1.0, 0.0).astype(MXU_DTYPE)
    carry = jnp.zeros((1, LANES), F32)
    for blk in range(s // LANES):
        z = misc_ref[0, blk * LANES:(blk + 1) * LANES, :] + bf_ref[...]
        ls = jnp.minimum(z, 0.0) - jnp.log1p(jnp.exp(-jnp.abs(z)))
        p1 = ls.astype(MXU_DTYPE)
        r1 = ls - p1.astype(F32)
        p2 = r1.astype(MXU_DTYPE)
        p3 = (r1 - p2.astype(F32)).astype(MXU_DTYPE)
        cs = _dot(tri, p1) + _dot(tri, p2) + _dot(tri, p3) + carry
        carry = cs[LANES - 1:LANES, :]
        cum_ref[0, blk * LANES:(blk + 1) * LANES, :] = cs * (HEAD_DIM ** 0.5)


def _prep(proj3, misc3, g_kv, bf_row):
    b, s, _ = proj3.shape
    return pl.pallas_call(
        _prep_kernel,
        grid=(b,),
        in_specs=[pl.BlockSpec((1, s, D_LAT), lambda i: (i, 0, U_CKV * LANES // D_LAT)),
                  pl.BlockSpec((1, s, LANES), lambda i: (i, 0, 0)),
                  pl.BlockSpec((1, D_LAT), lambda i: (0, 0)),
                  pl.BlockSpec((1, LANES), lambda i: (0, 0))],
        out_specs=[pl.BlockSpec((1, s, D_LAT), lambda i: (i, 0, 0)),
                   pl.BlockSpec((1, s, LANES), lambda i: (i, 0, 0))],
        out_shape=[jax.ShapeDtypeStruct((b, s, D_LAT), MXU_DTYPE),
                   jax.ShapeDtypeStruct((b, s, LANES), F32)],
        compiler_params=_cparams(("parallel",)),
        name="prep",
    )(proj3, misc3, g_kv, bf_row)


TQ_A = 256
N_BISECT = 32
LOG2E = 1.4426950408889634
assert TQ_A == TK


def _pair_loop(n, body):
    def pair(j, carry):
        body(2 * j, 2)
        return carry

    lax.fori_loop(0, jnp.right_shift(n, 1), pair, 0)

    @pl.when(jnp.bitwise_and(n, 1) == 1)
    def _():
        body(n - 1, 1)


def _dsa_kernel(qa_ref, qidx_ref, miscq_ref, misck_ref, kv_ref, wuk_ref, wuv_ref, gqa_ref, o_ref,
                sc_scr, bias_scr, logit_scr, qlat_scr, qh_scr, mm_scr, mpart_scr, lpart_scr, oacc_scr,
                *, topk, n_heads):
    i = pl.program_id(1)
    tq = TQ_A
    nk = i + 1
    q0 = i * tq

    wt = miscq_ref[0].T
    wq = wt[MISC_W:MISC_W + N_IDX_HEADS, :] * (D_IDX ** -0.5 * N_IDX_HEADS ** -0.5)
    for h in range(N_IDX_HEADS):
        qh_scr[h] = qidx_ref[0, :, h * D_IDX:(h + 1) * D_IDX]

    def fold(v, op, rows):
        return op(v.reshape(v.shape[0] // rows, rows, tq), axis=0)

    mm_scr[0:8, :] = jnp.full((8, tq), NEG_BIG, F32)
    mm_scr[8:16, :] = jnp.full((8, tq), -NEG_BIG, F32)

    def score_tiles(kt, nt):
        mx8 = mm_scr[0:8, :]
        mn8 = mm_scr[8:16, :]
        for half in range(nt * (TK // LANES)):
            k0 = pl.multiple_of(kt * TK + half * LANES, LANES)
            kx = misck_ref[0, pl.ds(k0, LANES), :][:, MISC_K:MISC_K + D_IDX].astype(MXU_DTYPE)
            acc = jnp.zeros((LANES, tq), F32)
            for h in range(N_IDX_HEADS):
                d = _dot_nt(kx, qh_scr[h])
                acc = acc + jnp.maximum(d, 0.0) * wq[h:h + 1, :]
            kpos = k0 + lax.broadcasted_iota(I32, (LANES, tq), 0)
            qpos = q0 + lax.broadcasted_iota(I32, (LANES, tq), 1)
            adm = (kpos // CHUNK) <= (qpos // CHUNK)
            sc_scr[pl.ds(k0, LANES), :] = jnp.where(adm, acc, NEG_BIG)
            mx8 = jnp.maximum(mx8, fold(jnp.where(adm, acc, NEG_BIG), jnp.max, 8))
            mn8 = jnp.minimum(mn8, fold(jnp.where(adm, acc, -NEG_BIG), jnp.min, 8))
        mm_scr[0:8, :] = mx8
        mm_scr[8:16, :] = mn8

    _pair_loop(nk, score_tiles)

    def bis_body(it, carry):
        lo, hi = carry
        mid = lo + 0.5 * (hi - lo)

        def cnt_body(kt, c32):
            k0 = pl.multiple_of(kt * TK, TK)
            return c32 + fold(jnp.where(sc_scr[pl.ds(k0, TK), :] >= mid, 1.0, 0.0), jnp.sum, 32)

        cnt = jnp.sum(lax.fori_loop(0, nk, cnt_body, jnp.zeros((32, tq), F32)), axis=0, keepdims=True)
        ok = cnt >= topk
        return jnp.where(ok, mid, lo), jnp.where(ok, hi, mid)

    thr, _ = lax.fori_loop(0, N_BISECT, bis_body, (jnp.min(mm_scr[8:16, :], axis=0, keepdims=True),
                                                   jnp.max(mm_scr[0:8, :], axis=0, keepdims=True)))

    def bias_body(kt, carry):
        k0 = pl.multiple_of(kt * TK, TK)
        sel_t = sc_scr[pl.ds(k0, TK), :] >= thr
        bias_scr[kt] = jnp.where(sel_t, 0.0, NEG_BIG).T
        return carry

    lax.fori_loop(0, nk, bias_body, 0)

    for h in range(n_heads):
        ql = _dot(qa_ref[0, :, h * HEAD_DIM:(h + 1) * HEAD_DIM], wuk_ref[h])
        ql = _rms(ql, gqa_ref[...]) * (D_LAT ** -0.5 * LOG2E)
        qlat_scr[h * tq:(h + 1) * tq, :] = ql.astype(qlat_scr.dtype)
    mpart_scr[...] = jnp.full(mpart_scr.shape, NEG_BIG, F32)
    lpart_scr[...] = jnp.zeros(lpart_scr.shape, F32)
    oacc_scr[...] = jnp.zeros(oacc_scr.shape, F32)
    slopes2 = [2.0 ** (-8.0 * (h + 1) / n_heads) * LOG2E for h in range(n_heads)]

    def logit_tiles(kt, nt, last):
        k0 = pl.multiple_of(kt * TK, TK)
        kvt = kv_ref[0, pl.ds(k0, nt * TK), :]
        kcol = (k0 + lax.broadcasted_iota(I32, (1, nt * TK), 1)).astype(F32)
        if last:
            ahead = jnp.maximum(lax.broadcasted_iota(I32, (tq, TK), 1)
                                - lax.broadcasted_iota(I32, (tq, TK), 0), 0).astype(F32)
        for h in range(n_heads):
            rows = slice(h * tq, (h + 1) * tq)
            lg = _dot_nt(qlat_scr[rows, :], kvt) + slopes2[h] * kcol
            for t in range(nt):
                lt = lg[:, t * TK:(t + 1) * TK] + bias_scr[kt + t]
                if last:
                    lt = lt - (2.0 * slopes2[h]) * ahead
                logit_scr[kt + t, rows, :] = lt
                mpart_scr[rows, :] = jnp.maximum(mpart_scr[rows, :],
                                                 jnp.maximum(lt[:, :LANES], lt[:, LANES:]))

    _pair_loop(i, lambda kt, nt: logit_tiles(kt, nt, False))
    logit_tiles(i, 1, True)

    m = jnp.max(mpart_scr[...], axis=-1, keepdims=True)
    mpart_scr[...] = jnp.broadcast_to(m, mpart_scr.shape)

    def pv_tiles(kt, nt):
        k0 = pl.multiple_of(kt * TK, TK)
        kvt = kv_ref[0, pl.ds(k0, nt * TK), :]
        for h in range(n_heads):
            rows = slice(h * tq, (h + 1) * tq)
            mb = mpart_scr[rows, :]
            mb2 = jnp.concatenate([mb, mb], axis=1)
            ps = [jnp.exp2(logit_scr[kt + t, rows, :] - mb2) for t in range(nt)]
            lsum = ps[0][:, :LANES] + ps[0][:, LANES:]
            for p in ps[1:]:
                lsum = lsum + p[:, :LANES] + p[:, LANES:]
            lpart_scr[rows, :] += lsum
            p_all = ps[0] if nt == 1 else jnp.concatenate(ps, axis=1)
            oacc_scr[rows, :] += _dot(p_all.astype(MXU_DTYPE), kvt)

    _pair_loop(nk, pv_tiles)

    for h in range(n_heads):
        rows = slice(h * tq, (h + 1) * tq)
        lsum = jnp.sum(lpart_scr[rows, :], axis=-1, keepdims=True)
        o_lat = oacc_scr[rows, :] / lsum
        o_ref[0, :, h * HEAD_DIM:(h + 1) * HEAD_DIM] = _dot(
            o_lat.astype(MXU_DTYPE), wuv_ref[h]).astype(o_ref.dtype)


def _dsa(proj3, misc3, kv, w_uk, w_uv, g_qa):
    b, s, _ = proj3.shape
    n_heads = w_uk.shape[0]
    width = n_heads * HEAD_DIM
    topk = min(TOPK_MAX, s // 4)
    tq = TQ_A
    nkt = s // TK
    kern = functools.partial(_dsa_kernel, topk=topk, n_heads=n_heads)
    return pl.pallas_call(
        kern,
        grid=(b, s // tq),
        in_specs=[pl.BlockSpec((1, tq, width), lambda bb, i: (bb, i, U_QA * LANES // width)),
                  pl.BlockSpec((1, tq, N_IDX_HEADS * D_IDX),
                               lambda bb, i: (bb, i, U_QIDX * LANES // (N_IDX_HEADS * D_IDX))),
                  pl.BlockSpec((1, tq, LANES), lambda bb, i: (bb, i, 0)),
                  pl.BlockSpec((1, s, LANES), lambda bb, i: (bb, 0, 0)),
                  pl.BlockSpec((1, s, D_LAT), lambda bb, i: (bb, 0, 0)),
                  pl.BlockSpec((n_heads, HEAD_DIM, D_LAT), lambda bb, i: (0, 0, 0)),
                  pl.BlockSpec((n_heads, D_LAT, HEAD_DIM), lambda bb, i: (0, 0, 0)),
                  pl.BlockSpec((1, D_LAT), lambda bb, i: (0, 0))],
        out_specs=pl.BlockSpec((1, tq, width), lambda bb, i: (bb, i, 0)),
        out_shape=jax.ShapeDtypeStruct((b, s, width), MXU_DTYPE),
        scratch_shapes=[pltpu.VMEM((s, tq), F32),
                        pltpu.VMEM((nkt, tq, TK), F32),
                        pltpu.VMEM((nkt, n_heads * tq, TK), F32),
                        pltpu.VMEM((n_heads * tq, D_LAT), MXU_DTYPE),
                        pltpu.VMEM((N_IDX_HEADS, tq, D_IDX), MXU_DTYPE),
                        pltpu.VMEM((16, tq), F32),
                        pltpu.VMEM((n_heads * tq, LANES), F32),
                        pltpu.VMEM((n_heads * tq, LANES), F32),
                        pltpu.VMEM((n_heads * tq, D_LAT), F32)],
        compiler_params=_cparams(("parallel", "arbitrary")),
        name="dsa",
    )(proj3, proj3, misc3, misc3, kv, w_uk, w_uv, g_qa)


TQ_B = 256


HG_B = 8
assert TQ_B == TK


def _split3(x):
    p1 = x.astype(MXU_DTYPE)
    r1 = x - p1.astype(F32)
    p2 = r1.astype(MXU_DTYPE)
    p3 = (r1 - p2.astype(F32)).astype(MXU_DTYPE)
    return p1, p2, p3


def _fox_kernel(q_ref, k_ref, v_ref, cumq_ref, cumk_ref, gq_ref, gk_ref, o_ref,
                kaug_scr, qaug_scr, logit_scr, mpart_scr, lpart_scr, oacc_scr):
    g = pl.program_id(1)
    qi = pl.program_id(2)
    tq = TQ_B
    s = k_ref.shape[1]
    cscale = (HEAD_DIM ** -0.5) * LOG2E

    rr = lax.broadcasted_iota(I32, (3 * LANES, LANES), 0)
    cc = lax.broadcasted_iota(I32, (3 * LANES, LANES), 1)
    lane_row = lax.broadcasted_iota(I32, (1, LANES), 1)
    ones_q = jnp.where((lane_row >= 3) & (lane_row < 6), 1.0, 0.0)
    ones_k = jnp.where(lane_row < 3, 1.0, 0.0)
    base_q = jnp.where(cc < 3, rr - LANES * cc, -1)
    base_k = jnp.where((cc >= 3) & (cc < 6), rr - LANES * (cc - 3), -1)

    def aug(pieces, h, base, sign, ones_row):
        e = jnp.where(base == h, sign, 0.0).astype(MXU_DTYPE)
        return (_dot(pieces, e) + ones_row).astype(MXU_DTYPE)

    @pl.when(qi == 0)
    def _():
        def kbody(c, carry):
            r0 = pl.multiple_of(c * TK, TK)
            ck = jnp.concatenate(_split3(cumk_ref[0, pl.ds(r0, TK), :]), axis=1)
            for hh in range(HG_B):
                kn = _rms(k_ref[0, pl.ds(r0, TK), hh * HEAD_DIM:(hh + 1) * HEAD_DIM].astype(F32), gk_ref[...])
                kaug_scr[hh, pl.ds(r0, TK), 0:HEAD_DIM] = kn.astype(MXU_DTYPE)
                kaug_scr[hh, pl.ds(r0, TK), HEAD_DIM:] = aug(ck, MISC_F + g * HG_B + hh, base_k, -1.0, ones_k)
            return carry

        lax.fori_loop(0, s // TK, kbody, 0)

    cq = jnp.concatenate(_split3(cumq_ref[0]), axis=1)
    for hh in range(HG_B):
        qn = _rms(q_ref[0, :, hh * HEAD_DIM:(hh + 1) * HEAD_DIM].astype(F32), gq_ref[...])
        qaug_scr[hh, :, 0:HEAD_DIM] = qn.astype(MXU_DTYPE)
        qaug_scr[hh, :, HEAD_DIM:] = aug(cq, MISC_F + g * HG_B + hh, base_q, 1.0, ones_q)
    mpart_scr[...] = jnp.full(mpart_scr.shape, NEG_BIG, F32)
    lpart_scr[...] = jnp.zeros(lpart_scr.shape, F32)
    oacc_scr[...] = jnp.zeros(oacc_scr.shape, F32)

    def put_logits(kt, hh, lg):
        logit_scr[kt, hh] = lg
        mpart_scr[hh] = jnp.maximum(mpart_scr[hh], jnp.maximum(lg[:, :LANES], lg[:, LANES:]))

    def raw_logits(kt, nt, hh):
        k0 = pl.multiple_of(kt * TK, TK)
        return _dot_nt(qaug_scr[hh], kaug_scr[hh, pl.ds(k0, nt * TK), :]) * cscale

    def off_tiles(kt, nt):
        for hh in range(HG_B):
            lg = raw_logits(kt, nt, hh)
            for t in range(nt):
                put_logits(kt + t, hh, lg[:, t * TK:(t + 1) * TK])

    _pair_loop(qi, off_tiles)
    causal = (lax.broadcasted_iota(I32, (tq, TK), 1) <= lax.broadcasted_iota(I32, (tq, TK), 0))
    for hh in range(HG_B):
        put_logits(qi, hh, jnp.where(causal, raw_logits(qi, 1, hh), NEG_BIG))

    for hh in range(HG_B):
        m = jnp.max(mpart_scr[hh], axis=-1, keepdims=True)
        mpart_scr[hh] = jnp.broadcast_to(m, (tq, LANES))

    def pv_tiles(kt, nt):
        k0 = pl.multiple_of(kt * TK, TK)
        for hh in range(HG_B):
            mb = mpart_scr[hh]
            mb2 = jnp.concatenate([mb, mb], axis=1)
            ps = [jnp.exp2(logit_scr[kt + t, hh] - mb2) for t in range(nt)]
            lsum = ps[0][:, :LANES] + ps[0][:, LANES:]
            for p in ps[1:]:
                lsum = lsum + p[:, :LANES] + p[:, LANES:]
            lpart_scr[hh] += lsum
            p_all = ps[0] if nt == 1 else jnp.concatenate(ps, axis=1)
            oacc_scr[hh] += _dot(p_all.astype(MXU_DTYPE),
                                 v_ref[0, pl.ds(k0, nt * TK), hh * HEAD_DIM:(hh + 1) * HEAD_DIM])

    _pair_loop(qi + 1, pv_tiles)
    for hh in range(HG_B):
        lsum = jnp.sum(lpart_scr[hh], axis=-1, keepdims=True)
        o_ref[0, :, hh * HEAD_DIM:(hh + 1) * HEAD_DIM] = (oacc_scr[hh] / lsum).astype(o_ref.dtype)


def _fox(proj3, cums, g_qb, g_kb, n_heads):
    b, s, _ = proj3.shape
    tq = TQ_B
    gw = HG_B * HEAD_DIM
    return pl.pallas_call(
        _fox_kernel,
        grid=(b, n_heads // HG_B, s // tq),
        in_specs=[pl.BlockSpec((1, tq, gw), lambda bb, g, i: (bb, i, U_QB // HG_B + g)),
                  pl.BlockSpec((1, s, gw), lambda bb, g, i: (bb, 0, U_KB // HG_B + g)),
                  pl.BlockSpec((1, s, gw), lambda bb, g, i: (bb, 0, U_VB // HG_B + g)),
                  pl.BlockSpec((1, tq, LANES), lambda bb, g, i: (bb, i, 0)),
                  pl.BlockSpec((1, s, LANES), lambda bb, g, i: (bb, 0, 0)),
                  pl.BlockSpec((1, HEAD_DIM), lambda bb, g, i: (0, 0)),
                  pl.BlockSpec((1, HEAD_DIM), lambda bb, g, i: (0, 0))],
        out_specs=pl.BlockSpec((1, tq, gw), lambda bb, g, i: (bb, i, g)),
        out_shape=jax.ShapeDtypeStruct((b, s, n_heads * HEAD_DIM), MXU_DTYPE),
        scratch_shapes=[pltpu.VMEM((HG_B, s, 2 * HEAD_DIM), MXU_DTYPE),
                        pltpu.VMEM((HG_B, tq, 2 * HEAD_DIM), MXU_DTYPE),
                        pltpu.VMEM((s // TK, HG_B, tq, TK), F32),
                        pltpu.VMEM((HG_B, tq, LANES), F32),
                        pltpu.VMEM((HG_B, tq, LANES), F32),
                        pltpu.VMEM((HG_B, tq, HEAD_DIM), F32)],
        compiler_params=_cparams(("parallel", "parallel", "arbitrary")),
        name="fox",
    )(proj3, proj3, proj3, cums, cums, g_qb, g_kb)


TM_MERGE = 256


def _merge_kernel(ga_ref, gb_ref, oa_ref, ob_ref, x_ref, mod_ref, bga_ref, bgb_ref, wpa_ref, wpb_ref,
                  wo_ref, gffn_ref, wrh_ref, wrp_ref, br_ref, x1_ref, h2_ref, rl_ref):
    ga = jax.nn.sigmoid(ga_ref[...].astype(F32) + bga_ref[...])
    gb = jax.nn.sigmoid(gb_ref[...].astype(F32) + bgb_ref[...])
    merged = ga * _dot(oa_ref[...], wpa_ref[...]) + gb * _dot(ob_ref[...], wpb_ref[...])
    upd = _dot(merged.astype(MXU_DTYPE), wo_ref[...])
    x1 = x_ref[...] + mod_ref[0, 2:3, :] * upd
    x1_ref[...] = x1
    h2 = _rms(x1, gffn_ref[...]) * (1.0 + mod_ref[0, 4:5, :]) + mod_ref[0, 3:4, :]
    h2_ref[...] = h2
    hh = h2.astype(MXU_DTYPE)
    hl = (h2 - hh.astype(F32)).astype(MXU_DTYPE)
    both = _dot(hh, wrp_ref[...])
    rl_ref[...] = both[:, :LANES] + both[:, LANES:] + _dot(hl, wrh_ref[...]) + br_ref[...]


def _merge(proj, o_a, o_b, x2, mod3, b_gate, w_pa, w_pb, w_o, g_ffn, wr_hi, wr_pair, b_r, seq):
    t, d = x2.shape
    tm = TM_MERGE
    per_b = seq // tm
    wa = o_a.shape[1]
    res = lambda shape: pl.BlockSpec(shape, lambda i: (0,) * len(shape), pipeline_mode=pl.Buffered(1))
    return pl.pallas_call(
        _merge_kernel,
        grid=(t // tm,),
        in_specs=[pl.BlockSpec((tm, d), lambda i: (i, 0)),
                  pl.BlockSpec((tm, d), lambda i: (i, 1)),
                  pl.BlockSpec((tm, wa), lambda i: (i, 0)),
                  pl.BlockSpec((tm, wa), lambda i: (i, 0)),
                  pl.BlockSpec((tm, d), lambda i: (i, 0)),
                  pl.BlockSpec((1, 6, d), lambda i: (i // per_b, 0, 0)),
                  pl.BlockSpec((1, d), lambda i: (0, 0)),
                  pl.BlockSpec((1, d), lambda i: (0, 1)),
                  res((wa, d)), res((wa, d)), res((d, d)),
                  pl.BlockSpec((1, d), lambda i: (0, 0)),
                  res((d, LANES)), res((d, 2 * LANES)),
                  pl.BlockSpec((1, LANES), lambda i: (0, 0))],
        out_specs=[pl.BlockSpec((tm, d), lambda i: (i, 0)),
                   pl.BlockSpec((tm, d), lambda i: (i, 0)),
                   pl.BlockSpec((tm, LANES), lambda i: (i, 0))],
        out_shape=[jax.ShapeDtypeStruct((t, d), F32),
                   jax.ShapeDtypeStruct((t, d), F32),
                   jax.ShapeDtypeStruct((t, LANES), F32)],
        compiler_params=_cparams(("parallel",)),
        name="merge",
    )(proj, proj, o_a, o_b, x2, mod3, b_gate, b_gate, w_pa, w_pb, w_o, g_ffn, wr_hi, wr_pair, b_r)


TM_ROUTE = 1024
TM_ROWS = 256
MM_ROWS = 128
R_E0, R_E1, R_P0, R_P1 = 0, 1, 4, 5
IT_TILE, IT_E, IT_LO, IT_HI, IT_FLAG, IT_NEXT, IT_NEXT2, IT_SLOT = range(8)
F_VALID, F_FIRST_OF_EXPERT, F_FIRST_OF_TILE = 1, 2, 4


def _route_kernel(rl_ref, route_ref, post_ref, items_ref, cnt_scr, run_scr, offs_scr, *, n_tiles):
    tm = rl_ref.shape[0]
    sweep = pl.program_id(0)
    step = pl.program_id(1)
    iw = items_ref.shape[1]

    @pl.when((sweep == 0) & (step == 0))
    def _():
        cnt_scr[...] = jnp.zeros(cnt_scr.shape, F32)

    r = rl_ref[...]
    lane = lax.broadcasted_iota(I32, (tm, LANES), 1).astype(F32)
    neg_inf = -jnp.inf
    gmask = lane < N_GROUPS
    gl = jnp.where(gmask, r, neg_inf)
    gmax = jnp.max(gl, axis=-1, keepdims=True)
    gidx = jnp.min(jnp.where(gl == gmax, lane, float(LANES)), axis=-1, keepdims=True)
    gsum = jnp.sum(jnp.where(gmask, jnp.exp(r - gmax), 0.0), axis=-1, keepdims=True)
    gw = 1.0 / gsum
    lo = N_GROUPS + EXPERTS_PER_GROUP * gidx
    emask = (lane >= lo) & (lane < lo + EXPERTS_PER_GROUP)
    el = jnp.where(emask, r, neg_inf)
    v0 = jnp.max(el, axis=-1, keepdims=True)
    i0 = jnp.min(jnp.where(el == v0, lane, float(LANES)), axis=-1, keepdims=True)
    el2 = jnp.where(lane == i0, neg_inf, el)
    v1 = jnp.max(el2, axis=-1, keepdims=True)
    i1 = jnp.min(jnp.where(el2 == v1, lane, float(LANES)), axis=-1, keepdims=True)
    tt = jnp.exp(v1 - v0)
    p0 = gw / (1.0 + tt)
    p1 = gw * tt / (1.0 + tt)
    e0 = i0 - N_GROUPS
    e1 = i1 - N_GROUPS

    hit0 = lane == e0
    hit1 = lane == e1
    oh = jnp.where(hit0 | hit1, 1.0, 0.0)

    @pl.when(sweep == 0)
    def _():
        ones = jnp.ones((tm, LANES), MXU_DTYPE)
        cnt_scr[...] += _dot(oh.T.astype(MXU_DTYPE), ones)

    @pl.when((sweep == 1) & (step == 0))
    def _():
        _plan_items(cnt_scr[...], items_ref, offs_scr, n_tiles, iw)
        run_scr[...] = jnp.zeros(run_scr.shape, F32)

    @pl.when(sweep == 1)
    def _():
        rr = lax.broadcasted_iota(I32, (tm, tm), 0)
        cc = lax.broadcasted_iota(I32, (tm, tm), 1)
        ltri = jnp.where(cc < rr, 1.0, 0.0).astype(MXU_DTYPE)
        before = _dot(ltri, oh.astype(MXU_DTYPE)) + run_scr[...] + offs_scr[...]
        pos0 = jnp.sum(jnp.where(hit0, before, 0.0), axis=-1, keepdims=True)
        pos1 = jnp.sum(jnp.where(hit1, before, 0.0), axis=-1, keepdims=True)
        run_scr[...] = run_scr[...] + jnp.sum(oh, axis=0, keepdims=True)
        out = jnp.zeros((tm, LANES), F32)
        for k, val in ((R_E0, e0), (R_E1, e1), (R_P0, p0), (R_P1, p1)):
            out = jnp.where(lane == k, val, out)
        route_ref[...] = out
        pmat = jnp.where(lane == 0.0, pos0, jnp.where(lane == 1.0, pos1, 0.0))
        post_ref[...] = pmat.T[0:8, :].astype(I32)


def _plan_items(cnt_col, items_ref, offs_scr, n_tiles, iw):
    tmr = float(TM_ROWS)
    sub = lax.broadcasted_iota(I32, (LANES, LANES), 0)
    lan = lax.broadcasted_iota(I32, (LANES, LANES), 1)
    lstrict = jnp.where(lan < sub, 1.0, 0.0).astype(MXU_DTYPE)
    hi = jnp.floor(cnt_col * (1.0 / LANES))
    lo = cnt_col - hi * LANES
    offs_col = _dot(lstrict, hi.astype(MXU_DTYPE)) * LANES + _dot(lstrict, lo.astype(MXU_DTYPE))
    first_t = jnp.floor(offs_col * (1.0 / tmr))
    last_t = jnp.floor((offs_col + cnt_col - 1.0) * (1.0 / tmr))
    n_col = jnp.where(cnt_col > 0.0, last_t - first_t + 1.0, 0.0)
    base_col = _dot(lstrict, n_col.astype(MXU_DTYPE))
    end_col = base_col + n_col
    offs_scr[...] = offs_col.T[0:1, :]

    rep = lambda col: jnp.concatenate([col] * (iw // LANES), axis=1)
    wl = lax.broadcasted_iota(I32, (LANES, iw), 1).astype(F32)
    sub_f = lax.broadcasted_iota(I32, (LANES, iw), 0).astype(F32)
    end_r = rep(end_col)
    w_total = end_r[LANES - 1:LANES, :]
    count_le = lambda v: jnp.sum(jnp.where(end_r <= v, 1.0, 0.0), axis=0, keepdims=True)
    w_row = wl[0:1, :]
    eidx = count_le(w_row)
    sel = sub_f == eidx
    pick = lambda col: jnp.sum(jnp.where(sel, rep(col), 0.0), axis=0, keepdims=True)
    e_base, e_first, e_offs, e_cnt, e_end = (pick(base_col), pick(first_t), pick(offs_col),
                                             pick(cnt_col), pick(end_col))
    valid = w_row < w_total
    tile = e_first + (w_row - e_base)
    row_lo = jnp.maximum(e_offs, tile * tmr) - tile * tmr
    row_hi = jnp.minimum(e_offs + e_cnt, (tile + 1.0) * tmr) - tile * tmr
    first_e = w_row == e_base
    flags = jnp.where(valid, F_VALID + jnp.where(first_e, float(F_FIRST_OF_EXPERT), 0.0)
                      + jnp.where(row_lo == 0.0, float(F_FIRST_OF_TILE), 0.0), 0.0)
    has1 = valid & first_e & (e_end < w_total)
    e1 = count_le(e_end)
    end1 = jnp.sum(jnp.where(sub_f == e1, end_r, 0.0), axis=0, keepdims=True)
    has2 = has1 & (end1 < w_total)
    nxt = jnp.where(has1, e1, -1.0)
    nxt2 = jnp.where(has2, count_le(end1), -1.0)
    ord_col = _dot(lstrict, jnp.where(cnt_col > 0.0, 1.0, 0.0).astype(MXU_DTYPE))
    e_ord = pick(ord_col)
    slot = e_ord - 2.0 * jnp.floor(e_ord * 0.5)
    e_last = count_le(w_total - 1.0)
    rows = {IT_TILE: jnp.where(valid, tile, n_tiles - 1.0),
            IT_E: jnp.where(valid, eidx, e_last),
            IT_LO: jnp.where(valid, row_lo, 0.0),
            IT_HI: jnp.where(valid, row_hi, 0.0),
            IT_FLAG: flags,
            IT_NEXT: nxt,
            IT_NEXT2: nxt2,
            IT_SLOT: jnp.where(valid, slot, 0.0)}
    sub8 = lax.broadcasted_iota(I32, (8, iw), 0)
    out = jnp.zeros((8, iw), F32)
    for k, val in rows.items():
        out = jnp.where(sub8 == k, val, out)
    items_ref[...] = out.astype(I32)


def _route(rlog, n_items):
    t = rlog.shape[0]
    tm = min(TM_ROUTE, t)
    iw = ((n_items + LANES - 1) // LANES) * LANES
    kern = functools.partial(_route_kernel, n_tiles=2 * t // TM_ROWS)
    return pl.pallas_call(
        kern,
        grid=(2, t // tm),
        in_specs=[pl.BlockSpec((tm, LANES), lambda p, i: (i, 0))],
        out_specs=[pl.BlockSpec((tm, LANES), lambda p, i: (i * p, 0)),
                   pl.BlockSpec((8, tm), lambda p, i: (0, i * p)),
                   pl.BlockSpec((8, iw), lambda p, i: (0, 0))],
        out_shape=[jax.ShapeDtypeStruct((t, LANES), F32),
                   jax.ShapeDtypeStruct((8, t), I32),
                   jax.ShapeDtypeStruct((8, iw), I32)],
        scratch_shapes=[pltpu.VMEM((LANES, LANES), F32),
                        pltpu.VMEM((1, LANES), F32),
                        pltpu.VMEM((1, LANES), F32)],
        compiler_params=_cparams(("arbitrary", "arbitrary")),
        name="route",
    )(rlog)


TM_DISP = 512


def _dispatch_kernel(pos0_ref, pos1_ref, h2_ref, xs_ref, sem):
    base = pl.program_id(0) * TM_DISP

    def row_copy(r, p):
        return pltpu.make_async_copy(h2_ref.at[pl.ds(r, 1), :], xs_ref.at[pl.ds(p, 1), :], sem)

    def issue(r, carry):
        row_copy(r, pos0_ref[base + r]).start(priority=0)
        row_copy(r, pos1_ref[base + r]).start(priority=1)
        return carry

    lax.fori_loop(0, TM_DISP, issue, 0, unroll=8)

    def drain(r, carry):
        row_copy(0, 0).wait()
        row_copy(0, 0).wait()
        return carry

    lax.fori_loop(0, TM_DISP, drain, 0, unroll=8)


def _dispatch(pos0, pos1, h2):
    t, d = h2.shape
    return pl.pallas_call(
        _dispatch_kernel,
        grid_spec=pltpu.PrefetchScalarGridSpec(
            num_scalar_prefetch=2,
            grid=(t // TM_DISP,),
            in_specs=[pl.BlockSpec((TM_DISP, d), lambda i, p0, p1: (i, 0))],
            out_specs=pl.BlockSpec(memory_space=pl.ANY),
            scratch_shapes=[pltpu.SemaphoreType.DMA(())]),
        out_shape=jax.ShapeDtypeStruct((2 * t, d), h2.dtype),
        compiler_params=pltpu.CompilerParams(dimension_semantics=("arbitrary",),
                                             vmem_limit_bytes=VMEM_LIMIT, has_side_effects=True),
        name="dispatch",
    )(pos0, pos1, h2)


WEIGHT_DMA_PRIORITY = 1


def _experts_kernel(tile_ref, e_ref, lo_ref, hi_ref, flag_ref, next_ref, next2_ref, slot_ref,
                    xs_ref, w1_hbm, w3_hbm, w2_hbm, ys_ref,
                    f1_scr, f3_scr, f2_scr, w1_scr, w3_scr, w2_scr, sems):
    del tile_ref
    w = pl.program_id(0)
    flag = flag_ref[w]

    mats = ((w1_hbm, f1_scr, w1_scr), (w3_hbm, f3_scr, w3_scr), (w2_hbm, f2_scr, w2_scr))

    def weight_copy(k, e, slot):
        return pltpu.make_async_copy(mats[k][0].at[e], mats[k][1].at[slot], sems.at[slot, k])

    @pl.when(w == 0)
    def _():
        for k in range(3):
            weight_copy(k, e_ref[0], 0).start(priority=WEIGHT_DMA_PRIORITY)
        nxt = next_ref[0]

        @pl.when(nxt >= 0)
        def _():
            for k in range(3):
                weight_copy(k, nxt, 1).start(priority=WEIGHT_DMA_PRIORITY)

    @pl.when((flag & F_FIRST_OF_EXPERT) != 0)
    def _():
        slot = slot_ref[w]
        nxt2 = next2_ref[w]
        for k in range(3):
            weight_copy(k, e_ref[w], slot).wait()
            mats[k][2][...] = mats[k][1][slot].astype(mats[k][2].dtype)

            @pl.when(nxt2 >= 0)
            def _():
                weight_copy(k, nxt2, slot).start(priority=WEIGHT_DMA_PRIORITY)

    @pl.when(((flag & F_VALID) != 0) & ((flag & F_FIRST_OF_TILE) != 0))
    def _():
        ys_ref[...] = jnp.zeros(ys_ref.shape, ys_ref.dtype)

    lo = lo_ref[w]
    hi = hi_ref[w]
    for part in range(TM_ROWS // MM_ROWS):
        r0 = part * MM_ROWS

        @pl.when(((flag & F_VALID) != 0) & (lo < r0 + MM_ROWS) & (hi > r0))
        def _():
            x = xs_ref[r0:r0 + MM_ROWS, :].astype(MXU_DTYPE)
            a = _dot(x, w1_scr[...])
            u = _dot(x, w3_scr[...])
            hm = (a * jax.nn.sigmoid(a)) * u
            res = _dot(hm.astype(MXU_DTYPE), w2_scr[...])
            row = r0 + lax.broadcasted_iota(I32, res.shape, 0)
            mine = (row >= lo) & (row < hi)
            ys_ref[r0:r0 + MM_ROWS, :] = jnp.where(mine, res, ys_ref[r0:r0 + MM_ROWS, :])


def _experts(items, xs, w1, w3, w2, n_items):
    n_rows, d = xs.shape
    f = w1.shape[2]
    tm = TM_ROWS
    tile_map = lambda w, tile, *_: (tile[w], 0)
    return pl.pallas_call(
        _experts_kernel,
        grid_spec=pltpu.PrefetchScalarGridSpec(
            num_scalar_prefetch=8,
            grid=(n_items,),
            in_specs=[pl.BlockSpec((tm, d), tile_map),
                      pl.BlockSpec(memory_space=pl.ANY),
                      pl.BlockSpec(memory_space=pl.ANY),
                      pl.BlockSpec(memory_space=pl.ANY)],
            out_specs=pl.BlockSpec((tm, d), tile_map),
            scratch_shapes=[pltpu.VMEM((2, d, f), F32),
                            pltpu.VMEM((2, d, f), F32),
                            pltpu.VMEM((2, f, d), F32),
                            pltpu.VMEM((d, f), MXU_DTYPE),
                            pltpu.VMEM((d, f), MXU_DTYPE),
                            pltpu.VMEM((f, d), MXU_DTYPE),
                            pltpu.SemaphoreType.DMA((2, 3))]),
        out_shape=jax.ShapeDtypeStruct((n_rows, d), F32),
        compiler_params=_cparams(("arbitrary",)),
        name="experts",
    )(items[IT_TILE], items[IT_E], items[IT_LO], items[IT_HI], items[IT_FLAG], items[IT_NEXT],
      items[IT_NEXT2], items[IT_SLOT], xs, w1, w3, w2)


TM_COMB = 512


def _combine_kernel(pos0_ref, pos1_ref, ys_ref, x1_ref, route_ref, mod_ref, o_ref, y0_scr, y1_scr, sems):
    step = pl.program_id(0)
    n_steps = pl.num_programs(0)

    def row_copy(p, dst, slot, r):
        return pltpu.make_async_copy(ys_ref.at[pl.ds(p, 1), :], dst.at[slot, pl.ds(r, 1), :], sems.at[slot])

    def issue_step(s):
        slot = s % 2
        base = s * TM_COMB

        def issue(r, carry):
            row_copy(pos0_ref[base + r], y0_scr, slot, r).start(priority=0)
            row_copy(pos1_ref[base + r], y1_scr, slot, r).start(priority=1)
            return carry

        lax.fori_loop(0, TM_COMB, issue, 0, unroll=8)

    @pl.when(step == 0)
    def _():
        issue_step(step)

    @pl.when(step + 1 < n_steps)
    def _():
        issue_step(step + 1)

    slot = step % 2

    def drain(r, carry):
        row_copy(0, y0_scr, slot, 0).wait()
        row_copy(0, y1_scr, slot, 0).wait()
        return carry

    lax.fori_loop(0, TM_COMB, drain, 0, unroll=8)

    rt = route_ref[...]
    lane = lax.broadcasted_iota(I32, rt.shape, 1)
    p0 = jnp.sum(jnp.where(lane == R_P0, rt, 0.0), axis=-1, keepdims=True)
    p1 = jnp.sum(jnp.where(lane == R_P1, rt, 0.0), axis=-1, keepdims=True)
    y = p0 * y0_scr[slot] + p1 * y1_scr[slot]
    o_ref[...] = x1_ref[...] + mod_ref[0, 5:6, :] * y


def _combine(pos0, pos1, ys, x1, route, mod3, seq):
    t, d = x1.shape
    tm = TM_COMB
    per_b = seq // tm
    return pl.pallas_call(
        _combine_kernel,
        grid_spec=pltpu.PrefetchScalarGridSpec(
            num_scalar_prefetch=2,
            grid=(t // tm,),
            in_specs=[pl.BlockSpec(memory_space=pl.ANY),
                      pl.BlockSpec((tm, d), lambda i, p0, p1: (i, 0)),
                      pl.BlockSpec((tm, LANES), lambda i, p0, p1: (i, 0)),
                      pl.BlockSpec((1, 6, d), lambda i, p0, p1: (i // per_b, 0, 0))],
            out_specs=pl.BlockSpec((tm, d), lambda i, p0, p1: (i, 0)),
            scratch_shapes=[pltpu.VMEM((2, tm, d), F32),
                            pltpu.VMEM((2, tm, d), F32),
                            pltpu.SemaphoreType.DMA((2,))]),
        out_shape=jax.ShapeDtypeStruct((t, d), F32),
        compiler_params=_cparams(("arbitrary",)),
        name="combine",
    )(pos0, pos1, ys, x1, route, mod3)


def kernel(x, c, w_ada, b_ada, g_mix, w_in, b_gate, b_forget, w_uk, w_uv, g_qa, g_kv, g_qb, g_kb,
           w_pa, w_pb, w_o, g_ffn, w_rg, b_rg, w_re, b_re, w1, w3, w2):
    b, s, d = x.shape
    depth = w_ada.shape[0]
    t = b * s
    n_heads_b = b_forget.shape[1]
    n_items = 2 * t // TM_ROWS + N_EXPERTS
    c8 = jnp.zeros((8, d), F32).at[:b].set(c)
    x2 = x.reshape(t, d)
    row = lambda v: v.reshape(1, -1)
    for l in range(depth):
        mod3 = _ada(c8, w_ada[l], row(b_ada[l]))[:b].reshape(b, 6, d)
        proj, misc = _inproj(x2, mod3, row(g_mix[l]), _pack_w_in(jnp.transpose(w_in[l])), s)
        proj3 = proj.reshape(b, s, NP_COLS)
        misc3 = misc.reshape(b, s, LANES)
        bf_row = jnp.zeros((1, LANES), F32).at[0, MISC_F:MISC_F + n_heads_b].set(b_forget[l])
        kv, cums = _prep(proj3, misc3, row(g_kv[l]), bf_row)
        o_a = _dsa(proj3, misc3, kv, w_uk[l].astype(MXU_DTYPE), w_uv[l].astype(MXU_DTYPE), row(g_qa[l]))
        o_b = _fox(proj3, cums, row(g_qb[l]), row(g_kb[l]), n_heads_b)

        w_r = jnp.zeros((d, LANES), F32).at[:, :N_GROUPS].set(w_rg[l])
        w_r = w_r.at[:, N_GROUPS:N_GROUPS + N_EXPERTS].set(w_re[l])
        wr_hi = w_r.astype(MXU_DTYPE)
        wr_pair = jnp.concatenate([wr_hi, (w_r - wr_hi.astype(F32)).astype(MXU_DTYPE)], axis=1)
        b_r = jnp.zeros((1, LANES), F32).at[0, :N_GROUPS].set(b_rg[l])
        b_r = b_r.at[0, N_GROUPS:N_GROUPS + N_EXPERTS].set(b_re[l])
        x1, h2, rlog = _merge(proj, o_a.reshape(t, -1), o_b.reshape(t, -1), x2, mod3, row(b_gate[l]),
                              w_pa[l].astype(MXU_DTYPE), w_pb[l].astype(MXU_DTYPE),
                              w_o[l].astype(MXU_DTYPE), row(g_ffn[l]), wr_hi, wr_pair, b_r, s)
        route, pos_t, items = _route(rlog, n_items)
        pos0, pos1 = pos_t[0], pos_t[1]
        xs = _dispatch(pos0, pos1, h2)
        ys = _experts(items, xs, w1[l], w3[l], w2[l], n_items)
        x2 = _combine(pos0, pos1, ys, x1, route, mod3, s)
    return x2.reshape(b, s, d)
```

```python
import functools

import jax
import jax.numpy as jnp
from jax import lax
from jax.experimental import pallas as pl
from jax.experimental.pallas import tpu as pltpu

F32 = jnp.float32
I32 = jnp.int32
MXU_DTYPE = jnp.bfloat16

CHUNK = 64
HEAD_DIM = 128
D_LAT = 256
N_IDX_HEADS = 16
D_IDX = 64
TOPK_MAX = 256
N_GROUPS = 8
EXPERTS_PER_GROUP = 8
N_EXPERTS = N_GROUPS * EXPERTS_PER_GROUP
RMS_EPS = 1e-6

LANES = 128
VMEM_LIMIT = 56 * 1024 * 1024

NEG_BIG = -1e30
INT_MIN = -2147483648


def _cparams(sem):
    return pltpu.CompilerParams(dimension_semantics=sem, vmem_limit_bytes=VMEM_LIMIT)


def _dot(a, b):
    return jnp.dot(a, b, preferred_element_type=F32)


def _dot_nt(a, b):
    return lax.dot_general(a, b, (((1,), (1,)), ((), ())), preferred_element_type=F32)


def _rms(x, g):
    return x * lax.rsqrt(jnp.mean(x * x, axis=-1, keepdims=True) + RMS_EPS) * g


def _ada_kernel(c_ref, w_ref, b_ref, o_ref):
    c = c_ref[...]
    a = c * jax.nn.sigmoid(c)
    o_ref[...] = _dot(a.astype(MXU_DTYPE), w_ref[...].astype(MXU_DTYPE)) + b_ref[...]


def _ada(c8, w_ada, b_ada):
    d, n = w_ada.shape
    tn = 1024
    return pl.pallas_call(
        _ada_kernel,
        grid=(n // tn,),
        in_specs=[pl.BlockSpec((8, d), lambda j: (0, 0)),
                  pl.BlockSpec((d, tn), lambda j: (0, j)),
                  pl.BlockSpec((1, tn), lambda j: (0, j))],
        out_specs=pl.BlockSpec((8, tn), lambda j: (0, j)),
        out_shape=jax.ShapeDtypeStruct((8, n), F32),
        compiler_params=_cparams(("arbitrary",)),
        name="ada",
    )(c8, w_ada, b_ada)


NP_COLS = 76 * LANES
U_QA, U_QIDX, U_QB, U_KB, U_VB, U_CKV, U_MISC = 32, 40, 48, 56, 64, 72, 74
TN_PROJ = 4 * LANES
TM_PROJ = 2048
TX_PROJ = 1024
MISC_TILE = (U_MISC * LANES) // TN_PROJ
MISC_OFF = U_MISC * LANES - MISC_TILE * TN_PROJ
MISC_K, MISC_F, MISC_W = 0, 64, 72


def _pack_moves():
    sizes = (1024, 256, 1024, 64, 16, 1024, 1024, 1024, 8, 4096)
    src = [0]
    for n in sizes:
        src.append(src[-1] + n)
    q_a, c_kv, q_idx, k_idx, w_idx, q_b, k_b, v_b, f_b, gate = src[:10]
    m = U_MISC * LANES
    return ((gate, 0, 4096), (q_a, U_QA * LANES, 1024), (q_idx, U_QIDX * LANES, 1024),
            (q_b, U_QB * LANES, 1024), (k_b, U_KB * LANES, 1024), (v_b, U_VB * LANES, 1024),
            (c_kv, U_CKV * LANES, 256), (f_b, m + MISC_F, 8), (w_idx, m + MISC_W, 16),
            (k_idx, m + MISC_K, 64))


def _pack_kernel(w_ref, o_ref):
    cols = o_ref.shape[1]
    m = U_MISC * LANES
    o_ref[m:m + 2 * LANES, :] = jnp.zeros((2 * LANES, cols), o_ref.dtype)
    for src, dst, n in _pack_moves():
        o_ref[dst:dst + n, :] = w_ref[src:src + n, :].astype(o_ref.dtype)


def _pack_w_in(w_in_t):
    n_in, d = w_in_t.shape
    tc = 256
    return pl.pallas_call(
        _pack_kernel,
        grid=(d // tc,),
        in_specs=[pl.BlockSpec((n_in, tc), lambda i: (0, i))],
        out_specs=pl.BlockSpec((NP_COLS, tc), lambda i: (0, i)),
        out_shape=jax.ShapeDtypeStruct((NP_COLS, d), MXU_DTYPE),
        compiler_params=_cparams(("parallel",)),
        name="pack",
    )(w_in_t)


def _inproj_kernel(x_ref, mod_ref, g_ref, w_ref, o_ref, misc_ref, h_scr, *, n_x):
    j = pl.program_id(1)
    tx = x_ref.shape[0]
    for part in range(n_x):
        @pl.when(j == part)
        def _():
            y = _rms(x_ref[...], g_ref[...])
            sh = mod_ref[0, 0:1, :]
            sc = mod_ref[0, 1:2, :]
            h_scr[part * tx:(part + 1) * tx, :] = (y * (1.0 + sc) + sh).astype(h_scr.dtype)

    @pl.when(j >= n_x)
    def _():
        acc = _dot_nt(h_scr[...], w_ref[...])
        o_ref[...] = acc.astype(o_ref.dtype)

        @pl.when(j == n_x + MISC_TILE)
        def _():
            misc_ref[...] = acc[:, MISC_OFF:MISC_OFF + LANES]


def _inproj(x2, mod3, g_mix, w_packed, seq):
    t, d = x2.shape
    tm = min(TM_PROJ, seq)
    tx = min(TX_PROJ, tm)
    n_x = tm // tx
    per_b = seq // tm
    wtile = lambda j: jnp.maximum(j - n_x, 0)
    return pl.pallas_call(
        functools.partial(_inproj_kernel, n_x=n_x),
        grid=(t // tm, n_x + NP_COLS // TN_PROJ),
        in_specs=[pl.BlockSpec((tx, d), lambda i, j: (i * n_x + jnp.minimum(j, n_x - 1), 0)),
                  pl.BlockSpec((1, 6, d), lambda i, j: (i // per_b, 0, 0)),
                  pl.BlockSpec((1, d), lambda i, j: (0, 0)),
                  pl.BlockSpec((TN_PROJ, d), lambda i, j: (wtile(j), 0))],
        out_specs=[pl.BlockSpec((tm, TN_PROJ), lambda i, j: (i, wtile(j))),
                   pl.BlockSpec((tm, LANES), lambda i, j: (i, 0))],
        out_shape=[jax.ShapeDtypeStruct((t, NP_COLS), MXU_DTYPE),
                   jax.ShapeDtypeStruct((t, LANES), F32)],
        scratch_shapes=[pltpu.VMEM((tm, d), MXU_DTYPE)],
        compiler_params=_cparams(("parallel", "arbitrary")),
        name="inproj",
    )(x2, mod3, g_mix, w_packed)


TK = 256


def _prep_kernel(ckv_ref, misc_ref, gkv_ref, bf_ref, kv_ref, cum_ref):
    s = ckv_ref.shape[1]
    kv_ref[0] = _rms(ckv_ref[0].astype(F32), gkv_ref[...]).astype(kv_ref.dtype)

    r = lax.broadcasted_iota(I32, (LANES, LANES), 0)
    c = lax.broadcasted_iota(I32, (LANES, LANES), 1)
    tri = jnp.where(c <= r, 1.0, 0.0).astype(MXU_DTYPE)
    carry = jnp.zeros((1, LANES), F32)
    for blk in range(s // LANES):
        z = misc_ref[0, blk * LANES:(blk + 1) * LANES, :] + bf_ref[...]
        ls = jnp.minimum(z, 0.0) - jnp.log1p(jnp.exp(-jnp.abs(z)))
        p1 = ls.astype(MXU_DTYPE)
        r1 = ls - p1.astype(F32)
        p2 = r1.astype(MXU_DTYPE)
        p3 = (r1 - p2.astype(F32)).astype(MXU_DTYPE)
        cs = _dot(tri, p1) + _dot(tri, p2) + _dot(tri, p3) + carry
        carry = cs[LANES - 1:LANES, :]
        cum_ref[0, blk * LANES:(blk + 1) * LANES, :] = cs * (HEAD_DIM ** 0.5)


def _prep(proj3, misc3, g_kv, bf_row):
    b, s, _ = proj3.shape
    return pl.pallas_call(
        _prep_kernel,
        grid=(b,),
        in_specs=[pl.BlockSpec((1, s, D_LAT), lambda i: (i, 0, U_CKV * LANES // D_LAT)),
                  pl.BlockSpec((1, s, LANES), lambda i: (i, 0, 0)),
                  pl.BlockSpec((1, D_LAT), lambda i: (0, 0)),
                  pl.BlockSpec((1, LANES), lambda i: (0, 0))],
        out_specs=[pl.BlockSpec((1, s, D_LAT), lambda i: (i, 0, 0)),
                   pl.BlockSpec((1, s, LANES), lambda i: (i, 0, 0))],
        out_shape=[jax.ShapeDtypeStruct((b, s, D_LAT), MXU_DTYPE),
                   jax.ShapeDtypeStruct((b, s, LANES), F32)],
        compiler_params=_cparams(("parallel",)),
        name="prep",
    )(proj3, misc3, g_kv, bf_row)


TQ_A = 256
N_BISECT = 32
LOG2E = 1.4426950408889634
assert TQ_A == TK


def _pair_loop(n, body):
    def pair(j, carry):
        body(2 * j, 2)
        return carry

    lax.fori_loop(0, jnp.right_shift(n, 1), pair, 0)

    @pl.when(jnp.bitwise_and(n, 1) == 1)
    def _():
        body(n - 1, 1)


def _dsa_kernel(qa_ref, qidx_ref, miscq_ref, misck_ref, kv_ref, wuk_ref, wuv_ref, gqa_ref, o_ref,
                sc_scr, bias_scr, logit_scr, qlat_scr, qh_scr, mm_scr, mpart_scr, lpart_scr, oacc_scr,
                *, topk, n_heads):
    i = pl.program_id(1)
    tq = TQ_A
    nk = i + 1
    q0 = i * tq

    wt = miscq_ref[0].T
    wq = wt[MISC_W:MISC_W + N_IDX_HEADS, :] * (D_IDX ** -0.5 * N_IDX_HEADS ** -0.5)
    for h in range(N_IDX_HEADS):
        qh_scr[h] = qidx_ref[0, :, h * D_IDX:(h + 1) * D_IDX]

    def fold(v, op, rows):
        return op(v.reshape(v.shape[0] // rows, rows, tq), axis=0)

    mm_scr[0:8, :] = jnp.full((8, tq), NEG_BIG, F32)
    mm_scr[8:16, :] = jnp.full((8, tq), -NEG_BIG, F32)

    def score_tiles(kt, nt):
        mx8 = mm_scr[0:8, :]
        mn8 = mm_scr[8:16, :]
        for half in range(nt * (TK // LANES)):
            k0 = pl.multiple_of(kt * TK + half * LANES, LANES)
            kx = misck_ref[0, pl.ds(k0, LANES), :][:, MISC_K:MISC_K + D_IDX].astype(MXU_DTYPE)
            acc = jnp.zeros((LANES, tq), F32)
            for h in range(N_IDX_HEADS):
                d = _dot_nt(kx, qh_scr[h])
                acc = acc + jnp.maximum(d, 0.0) * wq[h:h + 1, :]
            kpos = k0 + lax.broadcasted_iota(I32, (LANES, tq), 0)
            qpos = q0 + lax.broadcasted_iota(I32, (LANES, tq), 1)
            adm = (kpos // CHUNK) <= (qpos // CHUNK)
            sc_scr[pl.ds(k0, LANES), :] = jnp.where(adm, acc, NEG_BIG)
            mx8 = jnp.maximum(mx8, fold(jnp.where(adm, acc, NEG_BIG), jnp.max, 8))
            mn8 = jnp.minimum(mn8, fold(jnp.where(adm, acc, -NEG_BIG), jnp.min, 8))
        mm_scr[0:8, :] = mx8
        mm_scr[8:16, :] = mn8

    _pair_loop(nk, score_tiles)

    def bis_body(it, carry):
        lo, hi = carry
        mid = lo + 0.5 * (hi - lo)

        def cnt_body(kt, c32):
            k0 = pl.multiple_of(kt * TK, TK)
            return c32 + fold(jnp.where(sc_scr[pl.ds(k0, TK), :] >= mid, 1.0, 0.0), jnp.sum, 32)

        cnt = jnp.sum(lax.fori_loop(0, nk, cnt_body, jnp.zeros((32, tq), F32)), axis=0, keepdims=True)
        ok = cnt >= topk
        return jnp.where(ok, mid, lo), jnp.where(ok, hi, mid)

    thr, _ = lax.fori_loop(0, N_BISECT, bis_body, (jnp.min(mm_scr[8:16, :], axis=0, keepdims=True),
                                                   jnp.max(mm_scr[0:8, :], axis=0, keepdims=True)))

    def bias_body(kt, carry):
        k0 = pl.multiple_of(kt * TK, TK)
        sel_t = sc_scr[pl.ds(k0, TK), :] >= thr
        bias_scr[kt] = jnp.where(sel_t, 0.0, NEG_BIG).T
        return carry

    lax.fori_loop(0, nk, bias_body, 0)

    for h in range(n_heads):
        ql = _dot(qa_ref[0, :, h * HEAD_DIM:(h + 1) * HEAD_DIM], wuk_ref[h])
        ql = _rms(ql, gqa_ref[...]) * (D_LAT ** -0.5 * LOG2E)
        qlat_scr[h * tq:(h + 1) * tq, :] = ql.astype(qlat_scr.dtype)
    mpart_scr[...] = jnp.full(mpart_scr.shape, NEG_BIG, F32)
    lpart_scr[...] = jnp.zeros(lpart_scr.shape, F32)
    oacc_scr[...] = jnp.zeros(oacc_scr.shape, F32)
    slopes2 = [2.0 ** (-8.0 * (h + 1) / n_heads) * LOG2E for h in range(n_heads)]

    def logit_tiles(kt, nt, last):
        k0 = pl.multiple_of(kt * TK, TK)
        kvt = kv_ref[0, pl.ds(k0, nt * TK), :]
        kcol = (k0 + lax.broadcasted_iota(I32, (1, nt * TK), 1)).astype(F32)
        if last:
            ahead = jnp.maximum(lax.broadcasted_iota(I32, (tq, TK), 1)
                                - lax.broadcasted_iota(I32, (tq, TK), 0), 0).astype(F32)
        for h in range(n_heads):
            rows = slice(h * tq, (h + 1) * tq)
            lg = _dot_nt(qlat_scr[rows, :], kvt) + slopes2[h] * kcol
            for t in range(nt):
                lt = lg[:, t * TK:(t + 1) * TK] + bias_scr[kt + t]
                if last:
                    lt = lt - (2.0 * slopes2[h]) * ahead
                logit_scr[kt + t, rows, :] = lt
                mpart_scr[rows, :] = jnp.maximum(mpart_scr[rows, :],
                                                 jnp.maximum(lt[:, :LANES], lt[:, LANES:]))

    _pair_loop(i, lambda kt, nt: logit_tiles(kt, nt, False))
    logit_tiles(i, 1, True)

    m = jnp.max(mpart_scr[...], axis=-1, keepdims=True)
    mpart_scr[...] = jnp.broadcast_to(m, mpart_scr.shape)

    def pv_tiles(kt, nt):
        k0 = pl.multiple_of(kt * TK, TK)
        kvt = kv_ref[0, pl.ds(k0, nt * TK), :]
        for h in range(n_heads):
            rows = slice(h * tq, (h + 1) * tq)
            mb = mpart_scr[rows, :]
            mb2 = jnp.concatenate([mb, mb], axis=1)
            ps = [jnp.exp2(logit_scr[kt + t, rows, :] - mb2) for t in range(nt)]
            lsum = ps[0][:, :LANES] + ps[0][:, LANES:]
            for p in ps[1:]:
                lsum = lsum + p[:, :LANES] + p[:, LANES:]
            lpart_scr[rows, :] += lsum
            p_all = ps[0] if nt == 1 else jnp.concatenate(ps, axis=1)
            oacc_scr[rows, :] += _dot(p_all.astype(MXU_DTYPE), kvt)

    _pair_loop(nk, pv_tiles)

    for h in range(n_heads):
        rows = slice(h * tq, (h + 1) * tq)
        lsum = jnp.sum(lpart_scr[rows, :], axis=-1, keepdims=True)
        o_lat = oacc_scr[rows, :] / lsum
        o_ref[0, :, h * HEAD_DIM:(h + 1) * HEAD_DIM] = _dot(
            o_lat.astype(MXU_DTYPE), wuv_ref[h]).astype(o_ref.dtype)


def _dsa(proj3, misc3, kv, w_uk, w_uv, g_qa):
    b, s, _ = proj3.shape
    n_heads = w_uk.shape[0]
    width = n_heads * HEAD_DIM
    topk = min(TOPK_MAX, s // 4)
    tq = TQ_A
    nkt = s // TK
    kern = functools.partial(_dsa_kernel, topk=topk, n_heads=n_heads)
    return pl.pallas_call(
        kern,
        grid=(b, s // tq),
        in_specs=[pl.BlockSpec((1, tq, width), lambda bb, i: (bb, i, U_QA * LANES // width)),
                  pl.BlockSpec((1, tq, N_IDX_HEADS * D_IDX),
                               lambda bb, i: (bb, i, U_QIDX * LANES // (N_IDX_HEADS * D_IDX))),
                  pl.BlockSpec((1, tq, LANES), lambda bb, i: (bb, i, 0)),
                  pl.BlockSpec((1, s, LANES), lambda bb, i: (bb, 0, 0)),
                  pl.BlockSpec((1, s, D_LAT), lambda bb, i: (bb, 0, 0)),
                  pl.BlockSpec((n_heads, HEAD_DIM, D_LAT), lambda bb, i: (0, 0, 0)),
                  pl.BlockSpec((n_heads, D_LAT, HEAD_DIM), lambda bb, i: (0, 0, 0)),
                  pl.BlockSpec((1, D_LAT), lambda bb, i: (0, 0))],
        out_specs=pl.BlockSpec((1, tq, width), lambda bb, i: (bb, i, 0)),
        out_shape=jax.ShapeDtypeStruct((b, s, width), MXU_DTYPE),
        scratch_shapes=[pltpu.VMEM((s, tq), F32),
                        pltpu.VMEM((nkt, tq, TK), F32),
                        pltpu.VMEM((nkt, n_heads * tq, TK), F32),
                        pltpu.VMEM((n_heads * tq, D_LAT), MXU_DTYPE),
                        pltpu.VMEM((N_IDX_HEADS, tq, D_IDX), MXU_DTYPE),
                        pltpu.VMEM((16, tq), F32),
                        pltpu.VMEM((n_heads * tq, LANES), F32),
                        pltpu.VMEM((n_heads * tq, LANES), F32),
                        pltpu.VMEM((n_heads * tq, D_LAT), F32)],
        compiler_params=_cparams(("parallel", "arbitrary")),
        name="dsa",
    )(proj3, proj3, misc3, misc3, kv, w_uk, w_uv, g_qa)


TQ_B = 256


HG_B = 8
assert TQ_B == TK


def _split3(x):
    p1 = x.astype(MXU_DTYPE)
    r1 = x - p1.astype(F32)
    p2 = r1.astype(MXU_DTYPE)
    p3 = (r1 - p2.astype(F32)).astype(MXU_DTYPE)
    return p1, p2, p3


def _fox_kernel(q_ref, k_ref, v_ref, cumq_ref, cumk_ref, gq_ref, gk_ref, o_ref,
                kaug_scr, qaug_scr, logit_scr, mpart_scr, lpart_scr, oacc_scr):
    g = pl.program_id(1)
    qi = pl.program_id(2)
    tq = TQ_B
    s = k_ref.shape[1]
    cscale = (HEAD_DIM ** -0.5) * LOG2E

    rr = lax.broadcasted_iota(I32, (3 * LANES, LANES), 0)
    cc = lax.broadcasted_iota(I32, (3 * LANES, LANES), 1)
    lane_row = lax.broadcasted_iota(I32, (1, LANES), 1)
    ones_q = jnp.where((lane_row >= 3) & (lane_row < 6), 1.0, 0.0)
    ones_k = jnp.where(lane_row < 3, 1.0, 0.0)
    base_q = jnp.where(cc < 3, rr - LANES * cc, -1)
    base_k = jnp.where((cc >= 3) & (cc < 6), rr - LANES * (cc - 3), -1)

    def aug(pieces, h, base, sign, ones_row):
        e = jnp.where(base == h, sign, 0.0).astype(MXU_DTYPE)
        return (_dot(pieces, e) + ones_row).astype(MXU_DTYPE)

    @pl.when(qi == 0)
    def _():
        def kbody(c, carry):
            r0 = pl.multiple_of(c * TK, TK)
            ck = jnp.concatenate(_split3(cumk_ref[0, pl.ds(r0, TK), :]), axis=1)
            for hh in range(HG_B):
                kn = _rms(k_ref[0, pl.ds(r0, TK), hh * HEAD_DIM:(hh + 1) * HEAD_DIM].astype(F32), gk_ref[...])
                kaug_scr[hh, pl.ds(r0, TK), 0:HEAD_DIM] = kn.astype(MXU_DTYPE)
                kaug_scr[hh, pl.ds(r0, TK), HEAD_DIM:] = aug(ck, MISC_F + g * HG_B + hh, base_k, -1.0, ones_k)
            return carry

        lax.fori_loop(0, s // TK, kbody, 0)

    cq = jnp.concatenate(_split3(cumq_ref[0]), axis=1)
    for hh in range(HG_B):
        qn = _rms(q_ref[0, :, hh * HEAD_DIM:(hh + 1) * HEAD_DIM].astype(F32), gq_ref[...])
        qaug_scr[hh, :, 0:HEAD_DIM] = qn.astype(MXU_DTYPE)
        qaug_scr[hh, :, HEAD_DIM:] = aug(cq, MISC_F + g * HG_B + hh, base_q, 1.0, ones_q)
    mpart_scr[...] = jnp.full(mpart_scr.shape, NEG_BIG, F32)
    lpart_scr[...] = jnp.zeros(lpart_scr.shape, F32)
    oacc_scr[...] = jnp.zeros(oacc_scr.shape, F32)

    def put_logits(kt, hh, lg):
        logit_scr[kt, hh] = lg
        mpart_scr[hh] = jnp.maximum(mpart_scr[hh], jnp.maximum(lg[:, :LANES], lg[:, LANES:]))

    def raw_logits(kt, nt, hh):
        k0 = pl.multiple_of(kt * TK, TK)
        return _dot_nt(qaug_scr[hh], kaug_scr[hh, pl.ds(k0, nt * TK), :]) * cscale

    def off_tiles(kt, nt):
        for hh in range(HG_B):
            lg = raw_logits(kt, nt, hh)
            for t in range(nt):
                put_logits(kt + t, hh, lg[:, t * TK:(t + 1) * TK])

    _pair_loop(qi, off_tiles)
    causal = (lax.broadcasted_iota(I32, (tq, TK), 1) <= lax.broadcasted_iota(I32, (tq, TK), 0))
    for hh in range(HG_B):
        put_logits(qi, hh, jnp.where(causal, raw_logits(qi, 1, hh), NEG_BIG))

    for hh in range(HG_B):
        m = jnp.max(mpart_scr[hh], axis=-1, keepdims=True)
        mpart_scr[hh] = jnp.broadcast_to(m, (tq, LANES))

    def pv_tiles(kt, nt):
        k0 = pl.multiple_of(kt * TK, TK)
        for hh in range(HG_B):
            mb = mpart_scr[hh]
            mb2 = jnp.concatenate([mb, mb], axis=1)
            ps = [jnp.exp2(logit_scr[kt + t, hh] - mb2) for t in range(nt)]
            lsum = ps[0][:, :LANES] + ps[0][:, LANES:]
            for p in ps[1:]:
                lsum = lsum + p[:, :LANES] + p[:, LANES:]
            lpart_scr[hh] += lsum
            p_all = ps[0] if nt == 1 else jnp.concatenate(ps, axis=1)
            oacc_scr[hh] += _dot(p_all.astype(MXU_DTYPE),
                                 v_ref[0, pl.ds(k0, nt * TK), hh * HEAD_DIM:(hh + 1) * HEAD_DIM])

    _pair_loop(qi + 1, pv_tiles)
    for hh in range(HG_B):
        lsum = jnp.sum(lpart_scr[hh], axis=-1, keepdims=True)
        o_ref[0, :, hh * HEAD_DIM:(hh + 1) * HEAD_DIM] = (oacc_scr[hh] / lsum).astype(o_ref.dtype)


def _fox(proj3, cums, g_qb, g_kb, n_heads):
    b, s, _ = proj3.shape
    tq = TQ_B
    gw = HG_B * HEAD_DIM
    return pl.pallas_call(
        _fox_kernel,
        grid=(b, n_heads // HG_B, s // tq),
        in_specs=[pl.BlockSpec((1, tq, gw), lambda bb, g, i: (bb, i, U_QB // HG_B + g)),
                  pl.BlockSpec((1, s, gw), lambda bb, g, i: (bb, 0, U_KB // HG_B + g)),
                  pl.BlockSpec((1, s, gw), lambda bb, g, i: (bb, 0, U_VB // HG_B + g)),
                  pl.BlockSpec((1, tq, LANES), lambda bb, g, i: (bb, i, 0)),
                  pl.BlockSpec((1, s, LANES), lambda bb, g, i: (bb, 0, 0)),
                  pl.BlockSpec((1, HEAD_DIM), lambda bb, g, i: (0, 0)),
                  pl.BlockSpec((1, HEAD_DIM), lambda bb, g, i: (0, 0))],
        out_specs=pl.BlockSpec((1, tq, gw), lambda bb, g, i: (bb, i, g)),
        out_shape=jax.ShapeDtypeStruct((b, s, n_heads * HEAD_DIM), MXU_DTYPE),
        scratch_shapes=[pltpu.VMEM((HG_B, s, 2 * HEAD_DIM), MXU_DTYPE),
                        pltpu.VMEM((HG_B, tq, 2 * HEAD_DIM), MXU_DTYPE),
                        pltpu.VMEM((s // TK, HG_B, tq, TK), F32),
                        pltpu.VMEM((HG_B, tq, LANES), F32),
                        pltpu.VMEM((HG_B, tq, LANES), F32),
                        pltpu.VMEM((HG_B, tq, HEAD_DIM), F32)],
        compiler_params=_cparams(("parallel", "parallel", "arbitrary")),
        name="fox",
    )(proj3, proj3, proj3, cums, cums, g_qb, g_kb)


TM_MERGE = 256


def _merge_kernel(ga_ref, gb_ref, oa_ref, ob_ref, x_ref, mod_ref, bga_ref, bgb_ref, wpa_ref, wpb_ref,
                  wo_ref, gffn_ref, wrh_ref, wrp_ref, br_ref, x1_ref, h2_ref, rl_ref):
    ga = jax.nn.sigmoid(ga_ref[...].astype(F32) + bga_ref[...])
    gb = jax.nn.sigmoid(gb_ref[...].astype(F32) + bgb_ref[...])
    merged = ga * _dot(oa_ref[...], wpa_ref[...]) + gb * _dot(ob_ref[...], wpb_ref[...])
    upd = _dot(merged.astype(MXU_DTYPE), wo_ref[...])
    x1 = x_ref[...] + mod_ref[0, 2:3, :] * upd
    x1_ref[...] = x1
    h2 = _rms(x1, gffn_ref[...]) * (1.0 + mod_ref[0, 4:5, :]) + mod_ref[0, 3:4, :]
    h2_ref[...] = h2
    hh = h2.astype(MXU_DTYPE)
    hl = (h2 - hh.astype(F32)).astype(MXU_DTYPE)
    both = _dot(hh, wrp_ref[...])
    rl_ref[...] = both[:, :LANES] + both[:, LANES:] + _dot(hl, wrh_ref[...]) + br_ref[...]


def _merge(proj, o_a, o_b, x2, mod3, b_gate, w_pa, w_pb, w_o, g_ffn, wr_hi, wr_pair, b_r, seq):
    t, d = x2.shape
    tm = TM_MERGE
    per_b = seq // tm
    wa = o_a.shape[1]
    res = lambda shape: pl.BlockSpec(shape, lambda i: (0,) * len(shape), pipeline_mode=pl.Buffered(1))
    return pl.pallas_call(
        _merge_kernel,
        grid=(t // tm,),
        in_specs=[pl.BlockSpec((tm, d), lambda i: (i, 0)),
                  pl.BlockSpec((tm, d), lambda i: (i, 1)),
                  pl.BlockSpec((tm, wa), lambda i: (i, 0)),
                  pl.BlockSpec((tm, wa), lambda i: (i, 0)),
                  pl.BlockSpec((tm, d), lambda i: (i, 0)),
                  pl.BlockSpec((1, 6, d), lambda i: (i // per_b, 0, 0)),
                  pl.BlockSpec((1, d), lambda i: (0, 0)),
                  pl.BlockSpec((1, d), lambda i: (0, 1)),
                  res((wa, d)), res((wa, d)), res((d, d)),
                  pl.BlockSpec((1, d), lambda i: (0, 0)),
                  res((d, LANES)), res((d, 2 * LANES)),
                  pl.BlockSpec((1, LANES), lambda i: (0, 0))],
        out_specs=[pl.BlockSpec((tm, d), lambda i: (i, 0)),
                   pl.BlockSpec((tm, d), lambda i: (i, 0)),
                   pl.BlockSpec((tm, LANES), lambda i: (i, 0))],
        out_shape=[jax.ShapeDtypeStruct((t, d), F32),
                   jax.ShapeDtypeStruct((t, d), F32),
                   jax.ShapeDtypeStruct((t, LANES), F32)],
        compiler_params=_cparams(("parallel",)),
        name="merge",
    )(proj, proj, o_a, o_b, x2, mod3, b_gate, b_gate, w_pa, w_pb, w_o, g_ffn, wr_hi, wr_pair, b_r)


TM_ROUTE = 1024
TM_ROWS = 512
MM_ROWS = 128
R_E0, R_E1, R_P0, R_P1 = 0, 1, 4, 5
IT_TILE, IT_E, IT_LO, IT_HI, IT_FLAG, IT_NEXT, IT_NEXT2, IT_SLOT = range(8)
F_VALID, F_FIRST_OF_EXPERT, F_FIRST_OF_TILE = 1, 2, 4


def _route_kernel(rl_ref, route_ref, post_ref, items_ref, cnt_scr, run_scr, offs_scr, *, n_tiles):
    tm = rl_ref.shape[0]
    sweep = pl.program_id(0)
    step = pl.program_id(1)
    iw = items_ref.shape[1]

    @pl.when((sweep == 0) & (step == 0))
    def _():
        cnt_scr[...] = jnp.zeros(cnt_scr.shape, F32)

    r = rl_ref[...]
    lane = lax.broadcasted_iota(I32, (tm, LANES), 1).astype(F32)
    neg_inf = -jnp.inf
    gmask = lane < N_GROUPS
    gl = jnp.where(gmask, r, neg_inf)
    gmax = jnp.max(gl, axis=-1, keepdims=True)
    gidx = jnp.min(jnp.where(gl == gmax, lane, float(LANES)), axis=-1, keepdims=True)
    gsum = jnp.sum(jnp.where(gmask, jnp.exp(r - gmax), 0.0), axis=-1, keepdims=True)
    gw = 1.0 / gsum
    lo = N_GROUPS + EXPERTS_PER_GROUP * gidx
    emask = (lane >= lo) & (lane < lo + EXPERTS_PER_GROUP)
    el = jnp.where(emask, r, neg_inf)
    v0 = jnp.max(el, axis=-1, keepdims=True)
    i0 = jnp.min(jnp.where(el == v0, lane, float(LANES)), axis=-1, keepdims=True)
    el2 = jnp.where(lane == i0, neg_inf, el)
    v1 = jnp.max(el2, axis=-1, keepdims=True)
    i1 = jnp.min(jnp.where(el2 == v1, lane, float(LANES)), axis=-1, keepdims=True)
    tt = jnp.exp(v1 - v0)
    p0 = gw / (1.0 + tt)
    p1 = gw * tt / (1.0 + tt)
    e0 = i0 - N_GROUPS
    e1 = i1 - N_GROUPS

    hit0 = lane == e0
    hit1 = lane == e1
    oh = jnp.where(hit0 | hit1, 1.0, 0.0)

    @pl.when(sweep == 0)
    def _():
        ones = jnp.ones((tm, LANES), MXU_DTYPE)
        cnt_scr[...] += _dot(oh.T.astype(MXU_DTYPE), ones)

    @pl.when((sweep == 1) & (step == 0))
    def _():
        _plan_items(cnt_scr[...], items_ref, offs_scr, n_tiles, iw)
        run_scr[...] = jnp.zeros(run_scr.shape, F32)

    @pl.when(sweep == 1)
    def _():
        rr = lax.broadcasted_iota(I32, (tm, tm), 0)
        cc = lax.broadcasted_iota(I32, (tm, tm), 1)
        ltri = jnp.where(cc < rr, 1.0, 0.0).astype(MXU_DTYPE)
        before = _dot(ltri, oh.astype(MXU_DTYPE)) + run_scr[...] + offs_scr[...]
        pos0 = jnp.sum(jnp.where(hit0, before, 0.0), axis=-1, keepdims=True)
        pos1 = jnp.sum(jnp.where(hit1, before, 0.0), axis=-1, keepdims=True)
        run_scr[...] = run_scr[...] + jnp.sum(oh, axis=0, keepdims=True)
        out = jnp.zeros((tm, LANES), F32)
        for k, val in ((R_E0, e0), (R_E1, e1), (R_P0, p0), (R_P1, p1)):
            out = jnp.where(lane == k, val, out)
        route_ref[...] = out
        pmat = jnp.where(lane == 0.0, pos0, jnp.where(lane == 1.0, pos1, 0.0))
        post_ref[...] = pmat.T[0:8, :].astype(I32)


def _plan_items(cnt_col, items_ref, offs_scr, n_tiles, iw):
    tmr = float(TM_ROWS)
    sub = lax.broadcasted_iota(I32, (LANES, LANES), 0)
    lan = lax.broadcasted_iota(I32, (LANES, LANES), 1)
    lstrict = jnp.where(lan < sub, 1.0, 0.0).astype(MXU_DTYPE)
    hi = jnp.floor(cnt_col * (1.0 / LANES))
    lo = cnt_col - hi * LANES
    offs_col = _dot(lstrict, hi.astype(MXU_DTYPE)) * LANES + _dot(lstrict, lo.astype(MXU_DTYPE))
    first_t = jnp.floor(offs_col * (1.0 / tmr))
    last_t = jnp.floor((offs_col + cnt_col - 1.0) * (1.0 / tmr))
    n_col = jnp.where(cnt_col > 0.0, last_t - first_t + 1.0, 0.0)
    base_col = _dot(lstrict, n_col.astype(MXU_DTYPE))
    end_col = base_col + n_col
    offs_scr[...] = offs_col.T[0:1, :]

    rep = lambda col: jnp.concatenate([col] * (iw // LANES), axis=1)
    wl = lax.broadcasted_iota(I32, (LANES, iw), 1).astype(F32)
    sub_f = lax.broadcasted_iota(I32, (LANES, iw), 0).astype(F32)
    end_r = rep(end_col)
    w_total = end_r[LANES - 1:LANES, :]
    count_le = lambda v: jnp.sum(jnp.where(end_r <= v, 1.0, 0.0), axis=0, keepdims=True)
    w_row = wl[0:1, :]
    eidx = count_le(w_row)
    sel = sub_f == eidx
    pick = lambda col: jnp.sum(jnp.where(sel, rep(col), 0.0), axis=0, keepdims=True)
    e_base, e_first, e_offs, e_cnt, e_end = (pick(base_col), pick(first_t), pick(offs_col),
                                             pick(cnt_col), pick(end_col))
    valid = w_row < w_total
    tile = e_first + (w_row - e_base)
    row_lo = jnp.maximum(e_offs, tile * tmr) - tile * tmr
    row_hi = jnp.minimum(e_offs + e_cnt, (tile + 1.0) * tmr) - tile * tmr
    first_e = w_row == e_base
    flags = jnp.where(valid, F_VALID + jnp.where(first_e, float(F_FIRST_OF_EXPERT), 0.0)
                      + jnp.where(row_lo == 0.0, float(F_FIRST_OF_TILE), 0.0), 0.0)
    has1 = valid & first_e & (e_end < w_total)
    e1 = count_le(e_end)
    end1 = jnp.sum(jnp.where(sub_f == e1, end_r, 0.0), axis=0, keepdims=True)
    has2 = has1 & (end1 < w_total)
    nxt = jnp.where(has1, e1, -1.0)
    nxt2 = jnp.where(has2, count_le(end1), -1.0)
    ord_col = _dot(lstrict, jnp.where(cnt_col > 0.0, 1.0, 0.0).astype(MXU_DTYPE))
    e_ord = pick(ord_col)
    slot = e_ord - 2.0 * jnp.floor(e_ord * 0.5)
    e_last = count_le(w_total - 1.0)
    rows = {IT_TILE: jnp.where(valid, tile, n_tiles - 1.0),
            IT_E: jnp.where(valid, eidx, e_last),
            IT_LO: jnp.where(valid, row_lo, 0.0),
            IT_HI: jnp.where(valid, row_hi, 0.0),
            IT_FLAG: flags,
            IT_NEXT: nxt,
            IT_NEXT2: nxt2,
            IT_SLOT: jnp.where(valid, slot, 0.0)}
    sub8 = lax.broadcasted_iota(I32, (8, iw), 0)
    out = jnp.zeros((8, iw), F32)
    for k, val in rows.items():
        out = jnp.where(sub8 == k, val, out)
    items_ref[...] = out.astype(I32)


def _route(rlog, n_items):
    t = rlog.shape[0]
    tm = min(TM_ROUTE, t)
    iw = ((n_items + LANES - 1) // LANES) * LANES
    kern = functools.partial(_route_kernel, n_tiles=2 * t // TM_ROWS)
    return pl.pallas_call(
        kern,
        grid=(2, t // tm),
        in_specs=[pl.BlockSpec((tm, LANES), lambda p, i: (i, 0))],
        out_specs=[pl.BlockSpec((tm, LANES), lambda p, i: (i * p, 0)),
                   pl.BlockSpec((8, tm), lambda p, i: (0, i * p)),
                   pl.BlockSpec((8, iw), lambda p, i: (0, 0))],
        out_shape=[jax.ShapeDtypeStruct((t, LANES), F32),
                   jax.ShapeDtypeStruct((8, t), I32),
                   jax.ShapeDtypeStruct((8, iw), I32)],
        scratch_shapes=[pltpu.VMEM((LANES, LANES), F32),
                        pltpu.VMEM((1, LANES), F32),
                        pltpu.VMEM((1, LANES), F32)],
        compiler_params=_cparams(("arbitrary", "arbitrary")),
        name="route",
    )(rlog)


TM_DISP = 512


def _dispatch_kernel(pos0_ref, pos1_ref, h2_ref, xs_ref, sem):
    base = pl.program_id(0) * TM_DISP

    def row_copy(r, p):
        return pltpu.make_async_copy(h2_ref.at[pl.ds(r, 1), :], xs_ref.at[pl.ds(p, 1), :], sem)

    def issue(r, carry):
        row_copy(r, pos0_ref[base + r]).start(priority=0)
        row_copy(r, pos1_ref[base + r]).start(priority=1)
        return carry

    lax.fori_loop(0, TM_DISP, issue, 0, unroll=8)

    def drain(r, carry):
        row_copy(0, 0).wait()
        row_copy(0, 0).wait()
        return carry

    lax.fori_loop(0, TM_DISP, drain, 0, unroll=8)


def _dispatch(pos0, pos1, h2):
    t, d = h2.shape
    return pl.pallas_call(
        _dispatch_kernel,
        grid_spec=pltpu.PrefetchScalarGridSpec(
            num_scalar_prefetch=2,
            grid=(t // TM_DISP,),
            in_specs=[pl.BlockSpec((TM_DISP, d), lambda i, p0, p1: (i, 0))],
            out_specs=pl.BlockSpec(memory_space=pl.ANY),
            scratch_shapes=[pltpu.SemaphoreType.DMA(())]),
        out_shape=jax.ShapeDtypeStruct((2 * t, d), h2.dtype),
        compiler_params=pltpu.CompilerParams(dimension_semantics=("arbitrary",),
                                             vmem_limit_bytes=VMEM_LIMIT, has_side_effects=True),
        name="dispatch",
    )(pos0, pos1, h2)


WEIGHT_DMA_PRIORITY = 1


def _experts_kernel(tile_ref, e_ref, lo_ref, hi_ref, flag_ref, next_ref, next2_ref, slot_ref,
                    xs_ref, w1_hbm, w3_hbm, w2_hbm, ys_ref,
                    f1_scr, f3_scr, f2_scr, w1_scr, w3_scr, w2_scr, sems):
    del tile_ref
    w = pl.program_id(0)
    flag = flag_ref[w]

    mats = ((w1_hbm, f1_scr, w1_scr), (w3_hbm, f3_scr, w3_scr), (w2_hbm, f2_scr, w2_scr))

    def weight_copy(k, e, slot):
        return pltpu.make_async_copy(mats[k][0].at[e], mats[k][1].at[slot], sems.at[slot, k])

    @pl.when(w == 0)
    def _():
        for k in range(3):
            weight_copy(k, e_ref[0], 0).start(priority=WEIGHT_DMA_PRIORITY)
        nxt = next_ref[0]

        @pl.when(nxt >= 0)
        def _():
            for k in range(3):
                weight_copy(k, nxt, 1).start(priority=WEIGHT_DMA_PRIORITY)

    @pl.when((flag & F_FIRST_OF_EXPERT) != 0)
    def _():
        slot = slot_ref[w]
        nxt2 = next2_ref[w]
        for k in range(3):
            weight_copy(k, e_ref[w], slot).wait()
            mats[k][2][...] = mats[k][1][slot].astype(mats[k][2].dtype)

            @pl.when(nxt2 >= 0)
            def _():
                weight_copy(k, nxt2, slot).start(priority=WEIGHT_DMA_PRIORITY)

    @pl.when(((flag & F_VALID) != 0) & ((flag & F_FIRST_OF_TILE) != 0))
    def _():
        ys_ref[...] = jnp.zeros(ys_ref.shape, ys_ref.dtype)

    lo = lo_ref[w]
    hi = hi_ref[w]
    for part in range(TM_ROWS // MM_ROWS):
        r0 = part * MM_ROWS

        @pl.when(((flag & F_VALID) != 0) & (lo < r0 + MM_ROWS) & (hi > r0))
        def _():
            x = xs_ref[r0:r0 + MM_ROWS, :].astype(MXU_DTYPE)
            a = _dot(x, w1_scr[...])
            u = _dot(x, w3_scr[...])
            hm = (a * jax.nn.sigmoid(a)) * u
            res = _dot(hm.astype(MXU_DTYPE), w2_scr[...])
            row = r0 + lax.broadcasted_iota(I32, res.shape, 0)
            mine = (row >= lo) & (row < hi)
            ys_ref[r0:r0 + MM_ROWS, :] = jnp.where(mine, res, ys_ref[r0:r0 + MM_ROWS, :])


def _experts(items, xs, w1, w3, w2, n_items):
    n_rows, d = xs.shape
    f = w1.shape[2]
    tm = TM_ROWS
    tile_map = lambda w, tile, *_: (tile[w], 0)
    return pl.pallas_call(
        _experts_kernel,
        grid_spec=pltpu.PrefetchScalarGridSpec(
            num_scalar_prefetch=8,
            grid=(n_items,),
            in_specs=[pl.BlockSpec((tm, d), tile_map),
                      pl.BlockSpec(memory_space=pl.ANY),
                      pl.BlockSpec(memory_space=pl.ANY),
                      pl.BlockSpec(memory_space=pl.ANY)],
            out_specs=pl.BlockSpec((tm, d), tile_map),
            scratch_shapes=[pltpu.VMEM((2, d, f), F32),
                            pltpu.VMEM((2, d, f), F32),
                            pltpu.VMEM((2, f, d), F32),
                            pltpu.VMEM((d, f), MXU_DTYPE),
                            pltpu.VMEM((d, f), MXU_DTYPE),
                            pltpu.VMEM((f, d), MXU_DTYPE),
                            pltpu.SemaphoreType.DMA((2, 3))]),
        out_shape=jax.ShapeDtypeStruct((n_rows, d), F32),
        compiler_params=_cparams(("arbitrary",)),
        name="experts",
    )(items[IT_TILE], items[IT_E], items[IT_LO], items[IT_HI], items[IT_FLAG], items[IT_NEXT],
      items[IT_NEXT2], items[IT_SLOT], xs, w1, w3, w2)


TM_COMB = 512


def _combine_kernel(pos0_ref, pos1_ref, ys_ref, x1_ref, route_ref, mod_ref, o_ref, y0_scr, y1_scr, sems):
    step = pl.program_id(0)
    n_steps = pl.num_programs(0)

    def row_copy(p, dst, slot, r):
        return pltpu.make_async_copy(ys_ref.at[pl.ds(p, 1), :], dst.at[slot, pl.ds(r, 1), :], sems.at[slot])

    def issue_step(s):
        slot = s % 2
        base = s * TM_COMB

        def issue(r, carry):
            row_copy(pos0_ref[base + r], y0_scr, slot, r).start(priority=0)
            row_copy(pos1_ref[base + r], y1_scr, slot, r).start(priority=1)
            return carry

        lax.fori_loop(0, TM_COMB, issue, 0, unroll=8)

    @pl.when(step == 0)
    def _():
        issue_step(step)

    @pl.when(step + 1 < n_steps)
    def _():
        issue_step(step + 1)

    slot = step % 2

    def drain(r, carry):
        row_copy(0, y0_scr, slot, 0).wait()
        row_copy(0, y1_scr, slot, 0).wait()
        return carry

    lax.fori_loop(0, TM_COMB, drain, 0, unroll=8)

    rt = route_ref[...]
    lane = lax.broadcasted_iota(I32, rt.shape, 1)
    p0 = jnp.sum(jnp.where(lane == R_P0, rt, 0.0), axis=-1, keepdims=True)
    p1 = jnp.sum(jnp.where(lane == R_P1, rt, 0.0), axis=-1, keepdims=True)
    y = p0 * y0_scr[slot] + p1 * y1_scr[slot]
    o_ref[...] = x1_ref[...] + mod_ref[0, 5:6, :] * y


def _combine(pos0, pos1, ys, x1, route, mod3, seq):
    t, d = x1.shape
    tm = TM_COMB
    per_b = seq // tm
    return pl.pallas_call(
        _combine_kernel,
        grid_spec=pltpu.PrefetchScalarGridSpec(
            num_scalar_prefetch=2,
            grid=(t // tm,),
            in_specs=[pl.BlockSpec(memory_space=pl.ANY),
                      pl.BlockSpec((tm, d), lambda i, p0, p1: (i, 0)),
                      pl.BlockSpec((tm, LANES), lambda i, p0, p1: (i, 0)),
                      pl.BlockSpec((1, 6, d), lambda i, p0, p1: (i // per_b, 0, 0))],
            out_specs=pl.BlockSpec((tm, d), lambda i, p0, p1: (i, 0)),
            scratch_shapes=[pltpu.VMEM((2, tm, d), F32),
                            pltpu.VMEM((2, tm, d), F32),
                            pltpu.SemaphoreType.DMA((2,))]),
        out_shape=jax.ShapeDtypeStruct((t, d), F32),
        compiler_params=_cparams(("arbitrary",)),
        name="combine",
    )(pos0, pos1, ys, x1, route, mod3)


def kernel(x, c, w_ada, b_ada, g_mix, w_in, b_gate, b_forget, w_uk, w_uv, g_qa, g_kv, g_qb, g_kb,
           w_pa, w_pb, w_o, g_ffn, w_rg, b_rg, w_re, b_re, w1, w3, w2):
    b, s, d = x.shape
    depth = w_ada.shape[0]
    t = b * s
    n_heads_b = b_forget.shape[1]
    n_items = 2 * t // TM_ROWS + N_EXPERTS
    c8 = jnp.zeros((8, d), F32).at[:b].set(c)
    x2 = x.reshape(t, d)
    row = lambda v: v.reshape(1, -1)
    for l in range(depth):
        mod3 = _ada(c8, w_ada[l], row(b_ada[l]))[:b].reshape(b, 6, d)
        proj, misc = _inproj(x2, mod3, row(g_mix[l]), _pack_w_in(jnp.transpose(w_in[l])), s)
        proj3 = proj.reshape(b, s, NP_COLS)
        misc3 = misc.reshape(b, s, LANES)
        bf_row = jnp.zeros((1, LANES), F32).at[0, MISC_F:MISC_F + n_heads_b].set(b_forget[l])
        kv, cums = _prep(proj3, misc3, row(g_kv[l]), bf_row)
        o_a = _dsa(proj3, misc3, kv, w_uk[l].astype(MXU_DTYPE), w_uv[l].astype(MXU_DTYPE), row(g_qa[l]))
        o_b = _fox(proj3, cums, row(g_qb[l]), row(g_kb[l]), n_heads_b)

        w_r = jnp.zeros((d, LANES), F32).at[:, :N_GROUPS].set(w_rg[l])
        w_r = w_r.at[:, N_GROUPS:N_GROUPS + N_EXPERTS].set(w_re[l])
        wr_hi = w_r.astype(MXU_DTYPE)
        wr_pair = jnp.concatenate([wr_hi, (w_r - wr_hi.astype(F32)).astype(MXU_DTYPE)], axis=1)
        b_r = jnp.zeros((1, LANES), F32).at[0, :N_GROUPS].set(b_rg[l])
        b_r = b_r.at[0, N_GROUPS:N_GROUPS + N_EXPERTS].set(b_re[l])
        x1, h2, rlog = _merge(proj, o_a.reshape(t, -1), o_b.reshape(t, -1), x2, mod3, row(b_gate[l]),
                              w_pa[l].astype(MXU_DTYPE), w_pb[l].astype(MXU_DTYPE),
                              w_o[l].astype(MXU_DTYPE), row(g_ffn[l]), wr_hi, wr_pair, b_r, s)
        route, pos_t, items = _route(rlog, n_items)
        pos0, pos1 = pos_t[0], pos_t[1]
        xs = _dispatch(pos0, pos1, h2)
        ys = _experts(items, xs, w1[l], w3[l], w2[l], n_items)
        x2 = _combine(pos0, pos1, ys, x1, route, mod3, s)
    return x2.reshape(b, s, d)
```

```python
import functools

import jax
import jax.numpy as jnp
from jax import lax
from jax.experimental import pallas as pl
from jax.experimental.pallas import tpu as pltpu

F32 = jnp.float32
I32 = jnp.int32
MXU_DTYPE = jnp.bfloat16

CHUNK = 64
HEAD_DIM = 128
D_LAT = 256
N_IDX_HEADS = 16
D_IDX = 64
TOPK_MAX = 256
N_GROUPS = 8
EXPERTS_PER_GROUP = 8
N_EXPERTS = N_GROUPS * EXPERTS_PER_GROUP
RMS_EPS = 1e-6

LANES = 128
VMEM_LIMIT = 56 * 1024 * 1024

NEG_BIG = -1e30
INT_MIN = -2147483648


def _cparams(sem):
    return pltpu.CompilerParams(dimension_semantics=sem, vmem_limit_bytes=VMEM_LIMIT)


def _dot(a, b):
    return jnp.dot(a, b, preferred_element_type=F32)


def _dot_nt(a, b):
    return lax.dot_general(a, b, (((1,), (1,)), ((), ())), preferred_element_type=F32)


def _rms(x, g):
    return x * lax.rsqrt(jnp.mean(x * x, axis=-1, keepdims=True) + RMS_EPS) * g


def _ada_kernel(c_ref, w_ref, b_ref, o_ref):
    c = c_ref[...]
    a = c * jax.nn.sigmoid(c)
    o_ref[...] = _dot(a.astype(MXU_DTYPE), w_ref[...].astype(MXU_DTYPE)) + b_ref[...]


def _ada(c8, w_ada, b_ada):
    d, n = w_ada.shape
    tn = 1024
    return pl.pallas_call(
        _ada_kernel,
        grid=(n // tn,),
        in_specs=[pl.BlockSpec((8, d), lambda j: (0, 0)),
                  pl.BlockSpec((d, tn), lambda j: (0, j)),
                  pl.BlockSpec((1, tn), lambda j: (0, j))],
        out_specs=pl.BlockSpec((8, tn), lambda j: (0, j)),
        out_shape=jax.ShapeDtypeStruct((8, n), F32),
        compiler_params=_cparams(("arbitrary",)),
        name="ada",
    )(c8, w_ada, b_ada)


NP_COLS = 76 * LANES
U_QA, U_QIDX, U_QB, U_KB, U_VB, U_CKV, U_MISC = 32, 40, 48, 56, 64, 72, 74
TN_PROJ = 4 * LANES
TM_PROJ = 2048
TX_PROJ = 1024
MISC_TILE = (U_MISC * LANES) // TN_PROJ
MISC_OFF = U_MISC * LANES - MISC_TILE * TN_PROJ
MISC_K, MISC_F, MISC_W = 0, 64, 72


def _pack_moves():
    sizes = (1024, 256, 1024, 64, 16, 1024, 1024, 1024, 8, 4096)
    src = [0]
    for n in sizes:
        src.append(src[-1] + n)
    q_a, c_kv, q_idx, k_idx, w_idx, q_b, k_b, v_b, f_b, gate = src[:10]
    m = U_MISC * LANES
    return ((gate, 0, 4096), (q_a, U_QA * LANES, 1024), (q_idx, U_QIDX * LANES, 1024),
            (q_b, U_QB * LANES, 1024), (k_b, U_KB * LANES, 1024), (v_b, U_VB * LANES, 1024),
            (c_kv, U_CKV * LANES, 256), (f_b, m + MISC_F, 8), (w_idx, m + MISC_W, 16),
            (k_idx, m + MISC_K, 64))


def _pack_kernel(w_ref, o_ref):
    cols = o_ref.shape[1]
    m = U_MISC * LANES
    o_ref[m:m + 2 * LANES, :] = jnp.zeros((2 * LANES, cols), o_ref.dtype)
    for src, dst, n in _pack_moves():
        o_ref[dst:dst + n, :] = w_ref[src:src + n, :].astype(o_ref.dtype)


def _pack_w_in(w_in_t):
    n_in, d = w_in_t.shape
    tc = 256
    return pl.pallas_call(
        _pack_kernel,
        grid=(d // tc,),
        in_specs=[pl.BlockSpec((n_in, tc), lambda i: (0, i))],
        out_specs=pl.BlockSpec((NP_COLS, tc), lambda i: (0, i)),
        out_shape=jax.ShapeDtypeStruct((NP_COLS, d), MXU_DTYPE),
        compiler_params=_cparams(("parallel",)),
        name="pack",
    )(w_in_t)


def _inproj_kernel(x_ref, mod_ref, g_ref, w_ref, o_ref, misc_ref, h_scr, *, n_x):
    j = pl.program_id(1)
    tx = x_ref.shape[0]
    for part in range(n_x):
        @pl.when(j == part)
        def _():
            y = _rms(x_ref[...], g_ref[...])
            sh = mod_ref[0, 0:1, :]
            sc = mod_ref[0, 1:2, :]
            h_scr[part * tx:(part + 1) * tx, :] = (y * (1.0 + sc) + sh).astype(h_scr.dtype)

    @pl.when(j >= n_x)
    def _():
        acc = _dot_nt(h_scr[...], w_ref[...])
        o_ref[...] = acc.astype(o_ref.dtype)

        @pl.when(j == n_x + MISC_TILE)
        def _():
            misc_ref[...] = acc[:, MISC_OFF:MISC_OFF + LANES]


def _inproj(x2, mod3, g_mix, w_packed, seq):
    t, d = x2.shape
    tm = min(TM_PROJ, seq)
    tx = min(TX_PROJ, tm)
    n_x = tm // tx
    per_b = seq // tm
    wtile = lambda j: jnp.maximum(j - n_x, 0)
    return pl.pallas_call(
        functools.partial(_inproj_kernel, n_x=n_x),
        grid=(t // tm, n_x + NP_COLS // TN_PROJ),
        in_specs=[pl.BlockSpec((tx, d), lambda i, j: (i * n_x + jnp.minimum(j, n_x - 1), 0)),
                  pl.BlockSpec((1, 6, d), lambda i, j: (i // per_b, 0, 0)),
                  pl.BlockSpec((1, d), lambda i, j: (0, 0)),
                  pl.BlockSpec((TN_PROJ, d), lambda i, j: (wtile(j), 0))],
        out_specs=[pl.BlockSpec((tm, TN_PROJ), lambda i, j: (i, wtile(j))),
                   pl.BlockSpec((tm, LANES), lambda i, j: (i, 0))],
        out_shape=[jax.ShapeDtypeStruct((t, NP_COLS), MXU_DTYPE),
                   jax.ShapeDtypeStruct((t, LANES), F32)],
        scratch_shapes=[pltpu.VMEM((tm, d), MXU_DTYPE)],
        compiler_params=_cparams(("parallel", "arbitrary")),
        name="inproj",
    )(x2, mod3, g_mix, w_packed)


TK = 256


def _prep_kernel(ckv_ref, misc_ref, gkv_ref, bf_ref, kv_ref, cum_ref):
    s = ckv_ref.shape[1]
    kv_ref[0] = _rms(ckv_ref[0].astype(F32), gkv_ref[...]).astype(kv_ref.dtype)

    r = lax.broadcasted_iota(I32, (LANES, LANES), 0)
    c = lax.broadcasted_iota(I32, (LANES, LANES), 1)
    tri = jnp.where(c <= r, 1.0, 0.0).astype(MXU_DTYPE)
    carry = jnp.zeros((1, LANES), F32)
    for blk in range(s // LANES):
        z = misc_ref[0, blk * LANES:(blk + 1) * LANES, :] + bf_ref[...]
        ls = jnp.minimum(z, 0.0) - jnp.log1p(jnp.exp(-jnp.abs(z)))
        p1 = ls.astype(MXU_DTYPE)
        r1 = ls - p1.astype(F32)
        p2 = r1.astype(MXU_DTYPE)
        p3 = (r1 - p2.astype(F32)).astype(MXU_DTYPE)
        cs = _dot(tri, p1) + _dot(tri, p2) + _dot(tri, p3) + carry
        carry = cs[LANES - 1:LANES, :]
        cum_ref[0, blk * LANES:(blk + 1) * LANES, :] = cs * (HEAD_DIM ** 0.5)


def _prep(proj3, misc3, g_kv, bf_row):
    b, s, _ = proj3.shape
    return pl.pallas_call(
        _prep_kernel,
        grid=(b,),
        in_specs=[pl.BlockSpec((1, s, D_LAT), lambda i: (i, 0, U_CKV * LANES // D_LAT)),
                  pl.BlockSpec((1, s, LANES), lambda i: (i, 0, 0)),
                  pl.BlockSpec((1, D_LAT), lambda i: (0, 0)),
                  pl.BlockSpec((1, LANES), lambda i: (0, 0))],
        out_specs=[pl.BlockSpec((1, s, D_LAT), lambda i: (i, 0, 0)),
                   pl.BlockSpec((1, s, LANES), lambda i: (i, 0, 0))],
        out_shape=[jax.ShapeDtypeStruct((b, s, D_LAT), MXU_DTYPE),
                   jax.ShapeDtypeStruct((b, s, LANES), F32)],
        compiler_params=_cparams(("parallel",)),
        name="prep",
    )(proj3, misc3, g_kv, bf_row)


TQ_A = 256
N_BISECT = 32
LOG2E = 1.4426950408889634
assert TQ_A == TK


def _pair_loop(n, body):
    def pair(j, carry):
        body(2 * j, 2)
        return carry

    lax.fori_loop(0, jnp.right_shift(n, 1), pair, 0)

    @pl.when(jnp.bitwise_and(n, 1) == 1)
    def _():
        body(n - 1, 1)


def _dsa_kernel(qa_ref, qidx_ref, miscq_ref, misck_ref, kv_ref, wuk_ref, wuv_ref, gqa_ref, o_ref,
                sc_scr, bias_scr, logit_scr, qlat_scr, qh_scr, mm_scr, mpart_scr, lpart_scr, oacc_scr,
                *, topk, n_heads):
    i = pl.program_id(1)
    tq = TQ_A
    nk = i + 1
    q0 = i * tq

    wt = miscq_ref[0].T
    wq = wt[MISC_W:MISC_W + N_IDX_HEADS, :] * (D_IDX ** -0.5 * N_IDX_HEADS ** -0.5)
    for h in range(N_IDX_HEADS):
        qh_scr[h] = qidx_ref[0, :, h * D_IDX:(h + 1) * D_IDX]

    def fold(v, op, rows):
        return op(v.reshape(v.shape[0] // rows, rows, tq), axis=0)

    mm_scr[0:8, :] = jnp.full((8, tq), NEG_BIG, F32)
    mm_scr[8:16, :] = jnp.full((8, tq), -NEG_BIG, F32)

    def score_tiles(kt, nt):
        mx8 = mm_scr[0:8, :]
        mn8 = mm_scr[8:16, :]
        for half in range(nt * (TK // LANES)):
            k0 = pl.multiple_of(kt * TK + half * LANES, LANES)
            kx = misck_ref[0, pl.ds(k0, LANES), :][:, MISC_K:MISC_K + D_IDX].astype(MXU_DTYPE)
            acc = jnp.zeros((LANES, tq), F32)
            for h in range(N_IDX_HEADS):
                d = _dot_nt(kx, qh_scr[h])
                acc = acc + jnp.maximum(d, 0.0) * wq[h:h + 1, :]
            kpos = k0 + lax.broadcasted_iota(I32, (LANES, tq), 0)
            qpos = q0 + lax.broadcasted_iota(I32, (LANES, tq), 1)
            adm = (kpos // CHUNK) <= (qpos // CHUNK)
            sc_scr[pl.ds(k0, LANES), :] = jnp.where(adm, acc, NEG_BIG)
            mx8 = jnp.maximum(mx8, fold(jnp.where(adm, acc, NEG_BIG), jnp.max, 8))
            mn8 = jnp.minimum(mn8, fold(jnp.where(adm, acc, -NEG_BIG), jnp.min, 8))
        mm_scr[0:8, :] = mx8
        mm_scr[8:16, :] = mn8

    _pair_loop(nk, score_tiles)

    def bis_body(it, carry):
        lo, hi = carry
        mid = lo + 0.5 * (hi - lo)

        def cnt_body(kt, c32):
            k0 = pl.multiple_of(kt * TK, TK)
            return c32 + fold(jnp.where(sc_scr[pl.ds(k0, TK), :] >= mid, 1.0, 0.0), jnp.sum, 32)

        cnt = jnp.sum(lax.fori_loop(0, nk, cnt_body, jnp.zeros((32, tq), F32)), axis=0, keepdims=True)
        ok = cnt >= topk
        return jnp.where(ok, mid, lo), jnp.where(ok, hi, mid)

    thr, _ = lax.fori_loop(0, N_BISECT, bis_body, (jnp.min(mm_scr[8:16, :], axis=0, keepdims=True),
                                                   jnp.max(mm_scr[0:8, :], axis=0, keepdims=True)))

    def bias_body(kt, carry):
        k0 = pl.multiple_of(kt * TK, TK)
        sel_t = sc_scr[pl.ds(k0, TK), :] >= thr
        bias_scr[kt] = jnp.where(sel_t, 0.0, NEG_BIG).T
        return carry

    lax.fori_loop(0, nk, bias_body, 0)

    for h in range(n_heads):
        ql = _dot(qa_ref[0, :, h * HEAD_DIM:(h + 1) * HEAD_DIM], wuk_ref[h])
        ql = _rms(ql, gqa_ref[...]) * (D_LAT ** -0.5 * LOG2E)
        qlat_scr[h * tq:(h + 1) * tq, :] = ql.astype(qlat_scr.dtype)
    mpart_scr[...] = jnp.full(mpart_scr.shape, NEG_BIG, F32)
    lpart_scr[...] = jnp.zeros(lpart_scr.shape, F32)
    oacc_scr[...] = jnp.zeros(oacc_scr.shape, F32)
    slopes2 = [2.0 ** (-8.0 * (h + 1) / n_heads) * LOG2E for h in range(n_heads)]

    def logit_tiles(kt, nt, last):
        k0 = pl.multiple_of(kt * TK, TK)
        kvt = kv_ref[0, pl.ds(k0, nt * TK), :]
        kcol = (k0 + lax.broadcasted_iota(I32, (1, nt * TK), 1)).astype(F32)
        if last:
            ahead = jnp.maximum(lax.broadcasted_iota(I32, (tq, TK), 1)
                                - lax.broadcasted_iota(I32, (tq, TK), 0), 0).astype(F32)
        for h in range(n_heads):
            rows = slice(h * tq, (h + 1) * tq)
            lg = _dot_nt(qlat_scr[rows, :], kvt) + slopes2[h] * kcol
            for t in range(nt):
                lt = lg[:, t * TK:(t + 1) * TK] + bias_scr[kt + t]
                if last:
                    lt = lt - (2.0 * slopes2[h]) * ahead
                logit_scr[kt + t, rows, :] = lt
                mpart_scr[rows, :] = jnp.maximum(mpart_scr[rows, :],
                                                 jnp.maximum(lt[:, :LANES], lt[:, LANES:]))

    _pair_loop(i, lambda kt, nt: logit_tiles(kt, nt, False))
    logit_tiles(i, 1, True)

    m = jnp.max(mpart_scr[...], axis=-1, keepdims=True)
    mpart_scr[...] = jnp.broadcast_to(m, mpart_scr.shape)

    def pv_tiles(kt, nt):
        k0 = pl.multiple_of(kt * TK, TK)
        kvt = kv_ref[0, pl.ds(k0, nt * TK), :]
        for h in range(n_heads):
            rows = slice(h * tq, (h + 1) * tq)
            mb = mpart_scr[rows, :]
            mb2 = jnp.concatenate([mb, mb], axis=1)
            ps = [jnp.exp2(logit_scr[kt + t, rows, :] - mb2) for t in range(nt)]
            lsum = ps[0][:, :LANES] + ps[0][:, LANES:]
            for p in ps[1:]:
                lsum = lsum + p[:, :LANES] + p[:, LANES:]
            lpart_scr[rows, :] += lsum
            p_all = ps[0] if nt == 1 else jnp.concatenate(ps, axis=1)
            oacc_scr[rows, :] += _dot(p_all.astype(MXU_DTYPE), kvt)

    _pair_loop(nk, pv_tiles)

    for h in range(n_heads):
        rows = slice(h * tq, (h + 1) * tq)
        lsum = jnp.sum(lpart_scr[rows, :], axis=-1, keepdims=True)
        o_lat = oacc_scr[rows, :] / lsum
        o_ref[0, :, h * HEAD_DIM:(h + 1) * HEAD_DIM] = _dot(
            o_lat.astype(MXU_DTYPE), wuv_ref[h]).astype(o_ref.dtype)


def _dsa(proj3, misc3, kv, w_uk, w_uv, g_qa):
    b, s, _ = proj3.shape
    n_heads = w_uk.shape[0]
    width = n_heads * HEAD_DIM
    topk = min(TOPK_MAX, s // 4)
    tq = TQ_A
    nkt = s // TK
    kern = functools.partial(_dsa_kernel, topk=topk, n_heads=n_heads)
    return pl.pallas_call(
        kern,
        grid=(b, s // tq),
        in_specs=[pl.BlockSpec((1, tq, width), lambda bb, i: (bb, i, U_QA * LANES // width)),
                  pl.BlockSpec((1, tq, N_IDX_HEADS * D_IDX),
                               lambda bb, i: (bb, i, U_QIDX * LANES // (N_IDX_HEADS * D_IDX))),
                  pl.BlockSpec((1, tq, LANES), lambda bb, i: (bb, i, 0)),
                  pl.BlockSpec((1, s, LANES), lambda bb, i: (bb, 0, 0)),
                  pl.BlockSpec((1, s, D_LAT), lambda bb, i: (bb, 0, 0)),
                  pl.BlockSpec((n_heads, HEAD_DIM, D_LAT), lambda bb, i: (0, 0, 0)),
                  pl.BlockSpec((n_heads, D_LAT, HEAD_DIM), lambda bb, i: (0, 0, 0)),
                  pl.BlockSpec((1, D_LAT), lambda bb, i: (0, 0))],
        out_specs=pl.BlockSpec((1, tq, width), lambda bb, i: (bb, i, 0)),
        out_shape=jax.ShapeDtypeStruct((b, s, width), MXU_DTYPE),
        scratch_shapes=[pltpu.VMEM((s, tq), F32),
                        pltpu.VMEM((nkt, tq, TK), F32),
                        pltpu.VMEM((nkt, n_heads * tq, TK), F32),
                        pltpu.VMEM((n_heads * tq, D_LAT), MXU_DTYPE),
                        pltpu.VMEM((N_IDX_HEADS, tq, D_IDX), MXU_DTYPE),
                        pltpu.VMEM((16, tq), F32),
                        pltpu.VMEM((n_heads * tq, LANES), F32),
                        pltpu.VMEM((n_heads * tq, LANES), F32),
                        pltpu.VMEM((n_heads * tq, D_LAT), F32)],
        compiler_params=_cparams(("parallel", "arbitrary")),
        name="dsa",
    )(proj3, proj3, misc3, misc3, kv, w_uk, w_uv, g_qa)


TQ_B = 256


HG_B = 8
assert TQ_B == TK


def _split3(x):
    p1 = x.astype(MXU_DTYPE)
    r1 = x - p1.astype(F32)
    p2 = r1.astype(MXU_DTYPE)
    p3 = (r1 - p2.astype(F32)).astype(MXU_DTYPE)
    return p1, p2, p3


def _fox_kernel(q_ref, k_ref, v_ref, cumq_ref, cumk_ref, gq_ref, gk_ref, o_ref,
                kaug_scr, qaug_scr, logit_scr, mpart_scr, lpart_scr, oacc_scr):
    g = pl.program_id(1)
    qi = pl.program_id(2)
    tq = TQ_B
    s = k_ref.shape[1]
    cscale = (HEAD_DIM ** -0.5) * LOG2E

    rr = lax.broadcasted_iota(I32, (3 * LANES, LANES), 0)
    cc = lax.broadcasted_iota(I32, (3 * LANES, LANES), 1)
    lane_row = lax.broadcasted_iota(I32, (1, LANES), 1)
    ones_q = jnp.where((lane_row >= 3) & (lane_row < 6), 1.0, 0.0)
    ones_k = jnp.where(lane_row < 3, 1.0, 0.0)
    base_q = jnp.where(cc < 3, rr - LANES * cc, -1)
    base_k = jnp.where((cc >= 3) & (cc < 6), rr - LANES * (cc - 3), -1)

    def aug(pieces, h, base, sign, ones_row):
        e = jnp.where(base == h, sign, 0.0).astype(MXU_DTYPE)
        return (_dot(pieces, e) + ones_row).astype(MXU_DTYPE)

    @pl.when(qi == 0)
    def _():
        def kbody(c, carry):
            r0 = pl.multiple_of(c * TK, TK)
            ck = jnp.concatenate(_split3(cumk_ref[0, pl.ds(r0, TK), :]), axis=1)
            for hh in range(HG_B):
                kn = _rms(k_ref[0, pl.ds(r0, TK), hh * HEAD_DIM:(hh + 1) * HEAD_DIM].astype(F32), gk_ref[...])
                kaug_scr[hh, pl.ds(r0, TK), 0:HEAD_DIM] = kn.astype(MXU_DTYPE)
                kaug_scr[hh, pl.ds(r0, TK), HEAD_DIM:] = aug(ck, MISC_F + g * HG_B + hh, base_k, -1.0, ones_k)
            return carry

        lax.fori_loop(0, s // TK, kbody, 0)

    cq = jnp.concatenate(_split3(cumq_ref[0]), axis=1)
    for hh in range(HG_B):
        qn = _rms(q_ref[0, :, hh * HEAD_DIM:(hh + 1) * HEAD_DIM].astype(F32), gq_ref[...])
        qaug_scr[hh, :, 0:HEAD_DIM] = qn.astype(MXU_DTYPE)
        qaug_scr[hh, :, HEAD_DIM:] = aug(cq, MISC_F + g * HG_B + hh, base_q, 1.0, ones_q)
    mpart_scr[...] = jnp.full(mpart_scr.shape, NEG_BIG, F32)
    lpart_scr[...] = jnp.zeros(lpart_scr.shape, F32)
    oacc_scr[...] = jnp.zeros(oacc_scr.shape, F32)

    def put_logits(kt, hh, lg):
        logit_scr[kt, hh] = lg
        mpart_scr[hh] = jnp.maximum(mpart_scr[hh], jnp.maximum(lg[:, :LANES], lg[:, LANES:]))

    def raw_logits(kt, nt, hh):
        k0 = pl.multiple_of(kt * TK, TK)
        return _dot_nt(qaug_scr[hh], kaug_scr[hh, pl.ds(k0, nt * TK), :]) * cscale

    def off_tiles(kt, nt):
        for hh in range(HG_B):
            lg = raw_logits(kt, nt, hh)
            for t in range(nt):
                put_logits(kt + t, hh, lg[:, t * TK:(t + 1) * TK])

    _pair_loop(qi, off_tiles)
    causal = (lax.broadcasted_iota(I32, (tq, TK), 1) <= lax.broadcasted_iota(I32, (tq, TK), 0))
    for hh in range(HG_B):
        put_logits(qi, hh, jnp.where(causal, raw_logits(qi, 1, hh), NEG_BIG))

    for hh in range(HG_B):
        m = jnp.max(mpart_scr[hh], axis=-1, keepdims=True)
        mpart_scr[hh] = jnp.broadcast_to(m, (tq, LANES))

    def pv_tiles(kt, nt):
        k0 = pl.multiple_of(kt * TK, TK)
        for hh in range(HG_B):
            mb = mpart_scr[hh]
            mb2 = jnp.concatenate([mb, mb], axis=1)
            ps = [jnp.exp2(logit_scr[kt + t, hh] - mb2) for t in range(nt)]
            lsum = ps[0][:, :LANES] + ps[0][:, LANES:]
            for p in ps[1:]:
                lsum = lsum + p[:, :LANES] + p[:, LANES:]
            lpart_scr[hh] += lsum
            p_all = ps[0] if nt == 1 else jnp.concatenate(ps, axis=1)
            oacc_scr[hh] += _dot(p_all.astype(MXU_DTYPE),
                                 v_ref[0, pl.ds(k0, nt * TK), hh * HEAD_DIM:(hh + 1) * HEAD_DIM])

    _pair_loop(qi + 1, pv_tiles)
    for hh in range(HG_B):
        lsum = jnp.sum(lpart_scr[hh], axis=-1, keepdims=True)
        o_ref[0, :, hh * HEAD_DIM:(hh + 1) * HEAD_DIM] = (oacc_scr[hh] / lsum).astype(o_ref.dtype)


def _fox(proj3, cums, g_qb, g_kb, n_heads):
    b, s, _ = proj3.shape
    tq = TQ_B
    gw = HG_B * HEAD_DIM
    return pl.pallas_call(
        _fox_kernel,
        grid=(b, n_heads // HG_B, s // tq),
        in_specs=[pl.BlockSpec((1, tq, gw), lambda bb, g, i: (bb, i, U_QB // HG_B + g)),
                  pl.BlockSpec((1, s, gw), lambda bb, g, i: (bb, 0, U_KB // HG_B + g)),
                  pl.BlockSpec((1, s, gw), lambda bb, g, i: (bb, 0, U_VB // HG_B + g)),
                  pl.BlockSpec((1, tq, LANES), lambda bb, g, i: (bb, i, 0)),
                  pl.BlockSpec((1, s, LANES), lambda bb, g, i: (bb, 0, 0)),
                  pl.BlockSpec((1, HEAD_DIM), lambda bb, g, i: (0, 0)),
                  pl.BlockSpec((1, HEAD_DIM), lambda bb, g, i: (0, 0))],
        out_specs=pl.BlockSpec((1, tq, gw), lambda bb, g, i: (bb, i, g)),
        out_shape=jax.ShapeDtypeStruct((b, s, n_heads * HEAD_DIM), MXU_DTYPE),
        scratch_shapes=[pltpu.VMEM((HG_B, s, 2 * HEAD_DIM), MXU_DTYPE),
                        pltpu.VMEM((HG_B, tq, 2 * HEAD_DIM), MXU_DTYPE),
                        pltpu.VMEM((s // TK, HG_B, tq, TK), F32),
                        pltpu.VMEM((HG_B, tq, LANES), F32),
                        pltpu.VMEM((HG_B, tq, LANES), F32),
                        pltpu.VMEM((HG_B, tq, HEAD_DIM), F32)],
        compiler_params=_cparams(("parallel", "parallel", "arbitrary")),
        name="fox",
    )(proj3, proj3, proj3, cums, cums, g_qb, g_kb)


TM_MERGE = 256


def _merge_kernel(ga_ref, gb_ref, oa_ref, ob_ref, x_ref, mod_ref, bga_ref, bgb_ref, wpa_ref, wpb_ref,
                  wo_ref, gffn_ref, wrh_ref, wrp_ref, br_ref, x1_ref, h2_ref, rl_ref):
    ga = jax.nn.sigmoid(ga_ref[...].astype(F32) + bga_ref[...])
    gb = jax.nn.sigmoid(gb_ref[...].astype(F32) + bgb_ref[...])
    merged = ga * _dot(oa_ref[...], wpa_ref[...]) + gb * _dot(ob_ref[...], wpb_ref[...])
    upd = _dot(merged.astype(MXU_DTYPE), wo_ref[...])
    x1 = x_ref[...] + mod_ref[0, 2:3, :] * upd
    x1_ref[...] = x1
    h2 = _rms(x1, gffn_ref[...]) * (1.0 + mod_ref[0, 4:5, :]) + mod_ref[0, 3:4, :]
    h2_ref[...] = h2
    hh = h2.astype(MXU_DTYPE)
    hl = (h2 - hh.astype(F32)).astype(MXU_DTYPE)
    both = _dot(hh, wrp_ref[...])
    rl_ref[...] = both[:, :LANES] + both[:, LANES:] + _dot(hl, wrh_ref[...]) + br_ref[...]


def _merge(proj, o_a, o_b, x2, mod3, b_gate, w_pa, w_pb, w_o, g_ffn, wr_hi, wr_pair, b_r, seq):
    t, d = x2.shape
    tm = TM_MERGE
    per_b = seq // tm
    wa = o_a.shape[1]
    res = lambda shape: pl.BlockSpec(shape, lambda i: (0,) * len(shape), pipeline_mode=pl.Buffered(1))
    return pl.pallas_call(
        _merge_kernel,
        grid=(t // tm,),
        in_specs=[pl.BlockSpec((tm, d), lambda i: (i, 0)),
                  pl.BlockSpec((tm, d), lambda i: (i, 1)),
                  pl.BlockSpec((tm, wa), lambda i: (i, 0)),
                  pl.BlockSpec((tm, wa), lambda i: (i, 0)),
                  pl.BlockSpec((tm, d), lambda i: (i, 0)),
                  pl.BlockSpec((1, 6, d), lambda i: (i // per_b, 0, 0)),
                  pl.BlockSpec((1, d), lambda i: (0, 0)),
                  pl.BlockSpec((1, d), lambda i: (0, 1)),
                  res((wa, d)), res((wa, d)), res((d, d)),
                  pl.BlockSpec((1, d), lambda i: (0, 0)),
                  res((d, LANES)), res((d, 2 * LANES)),
                  pl.BlockSpec((1, LANES), lambda i: (0, 0))],
        out_specs=[pl.BlockSpec((tm, d), lambda i: (i, 0)),
                   pl.BlockSpec((tm, d), lambda i: (i, 0)),
                   pl.BlockSpec((tm, LANES), lambda i: (i, 0))],
        out_shape=[jax.ShapeDtypeStruct((t, d), F32),
                   jax.ShapeDtypeStruct((t, d), F32),
                   jax.ShapeDtypeStruct((t, LANES), F32)],
        compiler_params=_cparams(("parallel",)),
        name="merge",
    )(proj, proj, o_a, o_b, x2, mod3, b_gate, b_gate, w_pa, w_pb, w_o, g_ffn, wr_hi, wr_pair, b_r)


TM_ROUTE = 1024
TM_ROWS = 256
MM_ROWS = 128
R_E0, R_E1, R_P0, R_P1 = 0, 1, 4, 5
IT_TILE, IT_E, IT_LO, IT_HI, IT_FLAG, IT_NEXT, IT_NEXT2, IT_SLOT = range(8)
F_VALID, F_FIRST_OF_EXPERT, F_FIRST_OF_TILE = 1, 2, 4


def _route_kernel(rl_ref, route_ref, post_ref, items_ref, cnt_scr, run_scr, offs_scr, *, n_tiles):
    tm = rl_ref.shape[0]
    sweep = pl.program_id(0)
    step = pl.program_id(1)
    iw = items_ref.shape[1]

    @pl.when((sweep == 0) & (step == 0))
    def _():
        cnt_scr[...] = jnp.zeros(cnt_scr.shape, F32)

    r = rl_ref[...]
    lane = lax.broadcasted_iota(I32, (tm, LANES), 1).astype(F32)
    neg_inf = -jnp.inf
    gmask = lane < N_GROUPS
    gl = jnp.where(gmask, r, neg_inf)
    gmax = jnp.max(gl, axis=-1, keepdims=True)
    gidx = jnp.min(jnp.where(gl == gmax, lane, float(LANES)), axis=-1, keepdims=True)
    gsum = jnp.sum(jnp.where(gmask, jnp.exp(r - gmax), 0.0), axis=-1, keepdims=True)
    gw = 1.0 / gsum
    lo = N_GROUPS + EXPERTS_PER_GROUP * gidx
    emask = (lane >= lo) & (lane < lo + EXPERTS_PER_GROUP)
    el = jnp.where(emask, r, neg_inf)
    v0 = jnp.max(el, axis=-1, keepdims=True)
    i0 = jnp.min(jnp.where(el == v0, lane, float(LANES)), axis=-1, keepdims=True)
    el2 = jnp.where(lane == i0, neg_inf, el)
    v1 = jnp.max(el2, axis=-1, keepdims=True)
    i1 = jnp.min(jnp.where(el2 == v1, lane, float(LANES)), axis=-1, keepdims=True)
    tt = jnp.exp(v1 - v0)
    p0 = gw / (1.0 + tt)
    p1 = gw * tt / (1.0 + tt)
    e0 = i0 - N_GROUPS
    e1 = i1 - N_GROUPS

    hit0 = lane == e0
    hit1 = lane == e1
    oh = jnp.where(hit0 | hit1, 1.0, 0.0)

    @pl.when(sweep == 0)
    def _():
        ones = jnp.ones((tm, LANES), MXU_DTYPE)
        cnt_scr[...] += _dot(oh.T.astype(MXU_DTYPE), ones)

    @pl.when((sweep == 1) & (step == 0))
    def _():
        _plan_items(cnt_scr[...], items_ref, offs_scr, n_tiles, iw)
        run_scr[...] = jnp.zeros(run_scr.shape, F32)

    @pl.when(sweep == 1)
    def _():
        rr = lax.broadcasted_iota(I32, (tm, tm), 0)
        cc = lax.broadcasted_iota(I32, (tm, tm), 1)
        ltri = jnp.where(cc < rr, 1.0, 0.0).astype(MXU_DTYPE)
        before = _dot(ltri, oh.astype(MXU_DTYPE)) + run_scr[...] + offs_scr[...]
        pos0 = jnp.sum(jnp.where(hit0, before, 0.0), axis=-1, keepdims=True)
        pos1 = jnp.sum(jnp.where(hit1, before, 0.0), axis=-1, keepdims=True)
        run_scr[...] = run_scr[...] + jnp.sum(oh, axis=0, keepdims=True)
        out = jnp.zeros((tm, LANES), F32)
        for k, val in ((R_E0, e0), (R_E1, e1), (R_P0, p0), (R_P1, p1)):
            out = jnp.where(lane == k, val, out)
        route_ref[...] = out
        pmat = jnp.where(lane == 0.0, pos0, jnp.where(lane == 1.0, pos1, 0.0))
        post_ref[...] = pmat.T[0:8, :].astype(I32)


def _plan_items(cnt_col, items_ref, offs_scr, n_tiles, iw):
    tmr = float(TM_ROWS)
    sub = lax.broadcasted_iota(I32, (LANES, LANES), 0)
    lan = lax.broadcasted_iota(I32, (LANES, LANES), 1)
    lstrict = jnp.where(lan < sub, 1.0, 0.0).astype(MXU_DTYPE)
    hi = jnp.floor(cnt_col * (1.0 / LANES))
    lo = cnt_col - hi * LANES
    offs_col = _dot(lstrict, hi.astype(MXU_DTYPE)) * LANES + _dot(lstrict, lo.astype(MXU_DTYPE))
    first_t = jnp.floor(offs_col * (1.0 / tmr))
    last_t = jnp.floor((offs_col + cnt_col - 1.0) * (1.0 / tmr))
    n_col = jnp.where(cnt_col > 0.0, last_t - first_t + 1.0, 0.0)
    base_col = _dot(lstrict, n_col.astype(MXU_DTYPE))
    end_col = base_col + n_col
    offs_scr[...] = offs_col.T[0:1, :]

    rep = lambda col: jnp.concatenate([col] * (iw // LANES), axis=1)
    wl = lax.broadcasted_iota(I32, (LANES, iw), 1).astype(F32)
    sub_f = lax.broadcasted_iota(I32, (LANES, iw), 0).astype(F32)
    end_r = rep(end_col)
    w_total = end_r[LANES - 1:LANES, :]
    count_le = lambda v: jnp.sum(jnp.where(end_r <= v, 1.0, 0.0), axis=0, keepdims=True)
    w_row = wl[0:1, :]
    eidx = count_le(w_row)
    sel = sub_f == eidx
    pick = lambda col: jnp.sum(jnp.where(sel, rep(col), 0.0), axis=0, keepdims=True)
    e_base, e_first, e_offs, e_cnt, e_end = (pick(base_col), pick(first_t), pick(offs_col),
                                             pick(cnt_col), pick(end_col))
    valid = w_row < w_total
    tile = e_first + (w_row - e_base)
    row_lo = jnp.maximum(e_offs, tile * tmr) - tile * tmr
    row_hi = jnp.minimum(e_offs + e_cnt, (tile + 1.0) * tmr) - tile * tmr
    first_e = w_row == e_base
    flags = jnp.where(valid, F_VALID + jnp.where(first_e, float(F_FIRST_OF_EXPERT), 0.0)
                      + jnp.where(row_lo == 0.0, float(F_FIRST_OF_TILE), 0.0), 0.0)
    has1 = valid & first_e & (e_end < w_total)
    e1 = count_le(e_end)
    end1 = jnp.sum(jnp.where(sub_f == e1, end_r, 0.0), axis=0, keepdims=True)
    has2 = has1 & (end1 < w_total)
    nxt = jnp.where(has1, e1, -1.0)
    nxt2 = jnp.where(has2, count_le(end1), -1.0)
    ord_col = _dot(lstrict, jnp.where(cnt_col > 0.0, 1.0, 0.0).astype(MXU_DTYPE))
    e_ord = pick(ord_col)
    slot = e_ord - 2.0 * jnp.floor(e_ord * 0.5)
    e_last = count_le(w_total - 1.0)
    rows = {IT_TILE: jnp.where(valid, tile, n_tiles - 1.0),
            IT_E: jnp.where(valid, eidx, e_last),
            IT_LO: jnp.where(valid, row_lo, 0.0),
            IT_HI: jnp.where(valid, row_hi, 0.0),
            IT_FLAG: flags,
            IT_NEXT: nxt,
            IT_NEXT2: nxt2,
            IT_SLOT: jnp.where(valid, slot, 0.0)}
    sub8 = lax.broadcasted_iota(I32, (8, iw), 0)
    out = jnp.zeros((8, iw), F32)
    for k, val in rows.items():
        out = jnp.where(sub8 == k, val, out)
    items_ref[...] = out.astype(I32)


def _route(rlog, n_items):
    t = rlog.shape[0]
    tm = min(TM_ROUTE, t)
    iw = ((n_items + LANES - 1) // LANES) * LANES
    kern = functools.partial(_route_kernel, n_tiles=2 * t // TM_ROWS)
    return pl.pallas_call(
        kern,
        grid=(2, t // tm),
        in_specs=[pl.BlockSpec((tm, LANES), lambda p, i: (i, 0))],
        out_specs=[pl.BlockSpec((tm, LANES), lambda p, i: (i * p, 0)),
                   pl.BlockSpec((8, tm), lambda p, i: (0, i * p)),
                   pl.BlockSpec((8, iw), lambda p, i: (0, 0))],
        out_shape=[jax.ShapeDtypeStruct((t, LANES), F32),
                   jax.ShapeDtypeStruct((8, t), I32),
                   jax.ShapeDtypeStruct((8, iw), I32)],
        scratch_shapes=[pltpu.VMEM((LANES, LANES), F32),
                        pltpu.VMEM((1, LANES), F32),
                        pltpu.VMEM((1, LANES), F32)],
        compiler_params=_cparams(("arbitrary", "arbitrary")),
        name="route",
    )(rlog)


TM_DISP = 1024


def _dispatch_kernel(pos0_ref, pos1_ref, h2_ref, xs_ref, sem):
    tm = h2_ref.shape[0]
    base = pl.program_id(0) * tm

    def row_copy(r, p):
        return pltpu.make_async_copy(h2_ref.at[pl.ds(r, 1), :], xs_ref.at[pl.ds(p, 1), :], sem)

    def issue(r, carry):
        row_copy(r, pos0_ref[base + r]).start(priority=0)
        row_copy(r, pos1_ref[base + r]).start(priority=1)
        return carry

    lax.fori_loop(0, tm, issue, 0, unroll=8)

    def drain(r, carry):
        row_copy(0, 0).wait()
        row_copy(0, 0).wait()
        return carry

    lax.fori_loop(0, tm, drain, 0, unroll=8)


def _dispatch(pos0, pos1, h2):
    t, d = h2.shape
    tm = min(TM_DISP, t)
    return pl.pallas_call(
        _dispatch_kernel,
        grid_spec=pltpu.PrefetchScalarGridSpec(
            num_scalar_prefetch=2,
            grid=(t // tm,),
            in_specs=[pl.BlockSpec((tm, d), lambda i, p0, p1: (i, 0))],
            out_specs=pl.BlockSpec(memory_space=pl.ANY),
            scratch_shapes=[pltpu.SemaphoreType.DMA(())]),
        out_shape=jax.ShapeDtypeStruct((2 * t, d), h2.dtype),
        compiler_params=pltpu.CompilerParams(dimension_semantics=("arbitrary",),
                                             vmem_limit_bytes=VMEM_LIMIT, has_side_effects=True),
        name="dispatch",
    )(pos0, pos1, h2)


WEIGHT_DMA_PRIORITY = 1


def _experts_kernel(tile_ref, e_ref, lo_ref, hi_ref, flag_ref, next_ref, next2_ref, slot_ref,
                    xs_ref, w1_hbm, w3_hbm, w2_hbm, ys_ref,
                    f1_scr, f3_scr, f2_scr, w1_scr, w3_scr, w2_scr, sems):
    del tile_ref
    w = pl.program_id(0)
    flag = flag_ref[w]

    mats = ((w1_hbm, f1_scr, w1_scr), (w3_hbm, f3_scr, w3_scr), (w2_hbm, f2_scr, w2_scr))

    def weight_copy(k, e, slot):
        return pltpu.make_async_copy(mats[k][0].at[e], mats[k][1].at[slot], sems.at[slot, k])

    @pl.when(w == 0)
    def _():
        for k in range(3):
            weight_copy(k, e_ref[0], 0).start(priority=WEIGHT_DMA_PRIORITY)
        nxt = next_ref[0]

        @pl.when(nxt >= 0)
        def _():
            for k in range(3):
                weight_copy(k, nxt, 1).start(priority=WEIGHT_DMA_PRIORITY)

    @pl.when((flag & F_FIRST_OF_EXPERT) != 0)
    def _():
        slot = slot_ref[w]
        nxt2 = next2_ref[w]
        for k in range(3):
            weight_copy(k, e_ref[w], slot).wait()
            mats[k][2][...] = mats[k][1][slot].astype(mats[k][2].dtype)

            @pl.when(nxt2 >= 0)
            def _():
                weight_copy(k, nxt2, slot).start(priority=WEIGHT_DMA_PRIORITY)

    @pl.when(((flag & F_VALID) != 0) & ((flag & F_FIRST_OF_TILE) != 0))
    def _():
        ys_ref[...] = jnp.zeros(ys_ref.shape, ys_ref.dtype)

    lo = lo_ref[w]
    hi = hi_ref[w]
    for part in range(TM_ROWS // MM_ROWS):
        r0 = part * MM_ROWS

        @pl.when(((flag & F_VALID) != 0) & (lo < r0 + MM_ROWS) & (hi > r0))
        def _():
            x = xs_ref[r0:r0 + MM_ROWS, :].astype(MXU_DTYPE)
            a = _dot(x, w1_scr[...])
            u = _dot(x, w3_scr[...])
            hm = (a * jax.nn.sigmoid(a)) * u
            res = _dot(hm.astype(MXU_DTYPE), w2_scr[...])
            row = r0 + lax.broadcasted_iota(I32, res.shape, 0)
            mine = (row >= lo) & (row < hi)
            ys_ref[r0:r0 + MM_ROWS, :] = jnp.where(mine, res, ys_ref[r0:r0 + MM_ROWS, :])


def _experts(items, xs, w1, w3, w2, n_items):
    n_rows, d = xs.shape
    f = w1.shape[2]
    tm = TM_ROWS
    tile_map = lambda w, tile, *_: (tile[w], 0)
    return pl.pallas_call(
        _experts_kernel,
        grid_spec=pltpu.PrefetchScalarGridSpec(
            num_scalar_prefetch=8,
            grid=(n_items,),
            in_specs=[pl.BlockSpec((tm, d), tile_map),
                      pl.BlockSpec(memory_space=pl.ANY),
                      pl.BlockSpec(memory_space=pl.ANY),
                      pl.BlockSpec(memory_space=pl.ANY)],
            out_specs=pl.BlockSpec((tm, d), tile_map),
            scratch_shapes=[pltpu.VMEM((2, d, f), F32),
                            pltpu.VMEM((2, d, f), F32),
                            pltpu.VMEM((2, f, d), F32),
                            pltpu.VMEM((d, f), MXU_DTYPE),
                            pltpu.VMEM((d, f), MXU_DTYPE),
                            pltpu.VMEM((f, d), MXU_DTYPE),
                            pltpu.SemaphoreType.DMA((2, 3))]),
        out_shape=jax.ShapeDtypeStruct((n_rows, d), F32),
        compiler_params=_cparams(("arbitrary",)),
        name="experts",
    )(items[IT_TILE], items[IT_E], items[IT_LO], items[IT_HI], items[IT_FLAG], items[IT_NEXT],
      items[IT_NEXT2], items[IT_SLOT], xs, w1, w3, w2)


TM_COMB = 256


def _combine_kernel(pos0_ref, pos1_ref, ys_ref, x1_ref, route_ref, mod_ref, o_ref, y0_scr, y1_scr, sems):
    step = pl.program_id(0)
    n_steps = pl.num_programs(0)

    def row_copy(p, dst, slot, r):
        return pltpu.make_async_copy(ys_ref.at[pl.ds(p, 1), :], dst.at[slot, pl.ds(r, 1), :], sems.at[slot])

    def issue_step(s):
        slot = s % 2
        base = s * TM_COMB

        def issue(r, carry):
            row_copy(pos0_ref[base + r], y0_scr, slot, r).start(priority=0)
            row_copy(pos1_ref[base + r], y1_scr, slot, r).start(priority=1)
            return carry

        lax.fori_loop(0, TM_COMB, issue, 0, unroll=8)

    @pl.when(step == 0)
    def _():
        issue_step(step)

    @pl.when(step + 1 < n_steps)
    def _():
        issue_step(step + 1)

    slot = step % 2

    def drain(r, carry):
        row_copy(0, y0_scr, slot, 0).wait()
        row_copy(0, y1_scr, slot, 0).wait()
        return carry

    lax.fori_loop(0, TM_COMB, drain, 0, unroll=8)

    rt = route_ref[...]
    lane = lax.broadcasted_iota(I32, rt.shape, 1)
    p0 = jnp.sum(jnp.where(lane == R_P0, rt, 0.0), axis=-1, keepdims=True)
    p1 = jnp.sum(jnp.where(lane == R_P1, rt, 0.0), axis=-1, keepdims=True)
    y = p0 * y0_scr[slot] + p1 * y1_scr[slot]
    o_ref[...] = x1_ref[...] + mod_ref[0, 5:6, :] * y


def _combine(pos0, pos1, ys, x1, route, mod3, seq):
    t, d = x1.shape
    tm = TM_COMB
    per_b = seq // tm
    return pl.pallas_call(
        _combine_kernel,
        grid_spec=pltpu.PrefetchScalarGridSpec(
            num_scalar_prefetch=2,
            grid=(t // tm,),
            in_specs=[pl.BlockSpec(memory_space=pl.ANY),
                      pl.BlockSpec((tm, d), lambda i, p0, p1: (i, 0)),
                      pl.BlockSpec((tm, LANES), lambda i, p0, p1: (i, 0)),
                      pl.BlockSpec((1, 6, d), lambda i, p0, p1: (i // per_b, 0, 0))],
            out_specs=pl.BlockSpec((tm, d), lambda i, p0, p1: (i, 0)),
            scratch_shapes=[pltpu.VMEM((2, tm, d), F32),
                            pltpu.VMEM((2, tm, d), F32),
                            pltpu.SemaphoreType.DMA((2,))]),
        out_shape=jax.ShapeDtypeStruct((t, d), F32),
        compiler_params=_cparams(("arbitrary",)),
        name="combine",
    )(pos0, pos1, ys, x1, route, mod3)


def kernel(x, c, w_ada, b_ada, g_mix, w_in, b_gate, b_forget, w_uk, w_uv, g_qa, g_kv, g_qb, g_kb,
           w_pa, w_pb, w_o, g_ffn, w_rg, b_rg, w_re, b_re, w1, w3, w2):
    b, s, d = x.shape
    depth = w_ada.shape[0]
    t = b * s
    n_heads_b = b_forget.shape[1]
    n_items = 2 * t // TM_ROWS + N_EXPERTS
    c8 = jnp.zeros((8, d), F32).at[:b].set(c)
    x2 = x.reshape(t, d)
    row = lambda v: v.reshape(1, -1)
    for l in range(depth):
        mod3 = _ada(c8, w_ada[l], row(b_ada[l]))[:b].reshape(b, 6, d)
        proj, misc = _inproj(x2, mod3, row(g_mix[l]), _pack_w_in(jnp.transpose(w_in[l])), s)
        proj3 = proj.reshape(b, s, NP_COLS)
        misc3 = misc.reshape(b, s, LANES)
        bf_row = jnp.zeros((1, LANES), F32).at[0, MISC_F:MISC_F + n_heads_b].set(b_forget[l])
        kv, cums = _prep(proj3, misc3, row(g_kv[l]), bf_row)
        o_a = _dsa(proj3, misc3, kv, w_uk[l].astype(MXU_DTYPE), w_uv[l].astype(MXU_DTYPE), row(g_qa[l]))
        o_b = _fox(proj3, cums, row(g_qb[l]), row(g_kb[l]), n_heads_b)

        w_r = jnp.zeros((d, LANES), F32).at[:, :N_GROUPS].set(w_rg[l])
        w_r = w_r.at[:, N_GROUPS:N_GROUPS + N_EXPERTS].set(w_re[l])
        wr_hi = w_r.astype(MXU_DTYPE)
        wr_pair = jnp.concatenate([wr_hi, (w_r - wr_hi.astype(F32)).astype(MXU_DTYPE)], axis=1)
        b_r = jnp.zeros((1, LANES), F32).at[0, :N_GROUPS].set(b_rg[l])
        b_r = b_r.at[0, N_GROUPS:N_GROUPS + N_EXPERTS].set(b_re[l])
        x1, h2, rlog = _merge(proj, o_a.reshape(t, -1), o_b.reshape(t, -1), x2, mod3, row(b_gate[l]),
                              w_pa[l].astype(MXU_DTYPE), w_pb[l].astype(MXU_DTYPE),
                              w_o[l].astype(MXU_DTYPE), row(g_ffn[l]), wr_hi, wr_pair, b_r, s)
        route, pos_t, items = _route(rlog, n_items)
        pos0, pos1 = pos_t[0], pos_t[1]
        xs = _dispatch(pos0, pos1, h2)
        ys = _experts(items, xs, w1[l], w3[l], w2[l], n_items)
        x2 = _combine(pos0, pos1, ys, x1, route, mod3, s)
    return x2.reshape(b, s, d)
```

```python
import functools

import jax
import jax.numpy as jnp
from jax import lax
from jax.experimental import pallas as pl
from jax.experimental.pallas import tpu as pltpu

F32 = jnp.float32
I32 = jnp.int32
MXU_DTYPE = jnp.bfloat16

CHUNK = 64
HEAD_DIM = 128
D_LAT = 256
N_IDX_HEADS = 16
D_IDX = 64
TOPK_MAX = 256
N_GROUPS = 8
EXPERTS_PER_GROUP = 8
N_EXPERTS = N_GROUPS * EXPERTS_PER_GROUP
RMS_EPS = 1e-6

LANES = 128
VMEM_LIMIT = 56 * 1024 * 1024

NEG_BIG = -1e30
INT_MIN = -2147483648


def _cparams(sem):
    return pltpu.CompilerParams(dimension_semantics=sem, vmem_limit_bytes=VMEM_LIMIT)


def _dot(a, b):
    return jnp.dot(a, b, preferred_element_type=F32)


def _dot_nt(a, b):
    return lax.dot_general(a, b, (((1,), (1,)), ((), ())), preferred_element_type=F32)


def _rms(x, g):
    return x * lax.rsqrt(jnp.mean(x * x, axis=-1, keepdims=True) + RMS_EPS) * g


def _ada_kernel(c_ref, w_ref, b_ref, o_ref):
    c = c_ref[...]
    a = c * jax.nn.sigmoid(c)
    o_ref[...] = _dot(a.astype(MXU_DTYPE), w_ref[...].astype(MXU_DTYPE)) + b_ref[...]


def _ada(c8, w_ada, b_ada):
    d, n = w_ada.shape
    tn = 2048
    return pl.pallas_call(
        _ada_kernel,
        grid=(n // tn,),
        in_specs=[pl.BlockSpec((8, d), lambda j: (0, 0)),
                  pl.BlockSpec((d, tn), lambda j: (0, j)),
                  pl.BlockSpec((1, tn), lambda j: (0, j))],
        out_specs=pl.BlockSpec((8, tn), lambda j: (0, j)),
        out_shape=jax.ShapeDtypeStruct((8, n), F32),
        compiler_params=_cparams(("arbitrary",)),
        name="ada",
    )(c8, w_ada, b_ada)


NP_COLS = 76 * LANES
U_QA, U_QIDX, U_QB, U_KB, U_VB, U_CKV, U_MISC = 32, 40, 48, 56, 64, 72, 74
TN_PROJ = 4 * LANES
TM_PROJ = 2048
TX_PROJ = 1024
MISC_TILE = (U_MISC * LANES) // TN_PROJ
MISC_OFF = U_MISC * LANES - MISC_TILE * TN_PROJ
MISC_K, MISC_F, MISC_W = 0, 64, 72


def _pack_moves():
    sizes = (1024, 256, 1024, 64, 16, 1024, 1024, 1024, 8, 4096)
    src = [0]
    for n in sizes:
        src.append(src[-1] + n)
    q_a, c_kv, q_idx, k_idx, w_idx, q_b, k_b, v_b, f_b, gate = src[:10]
    m = U_MISC * LANES
    return ((gate, 0, 4096), (q_a, U_QA * LANES, 1024), (q_idx, U_QIDX * LANES, 1024),
            (q_b, U_QB * LANES, 1024), (k_b, U_KB * LANES, 1024), (v_b, U_VB * LANES, 1024),
            (c_kv, U_CKV * LANES, 256), (f_b, m + MISC_F, 8), (w_idx, m + MISC_W, 16),
            (k_idx, m + MISC_K, 64))


def _pack_kernel(w_ref, o_ref):
    cols = o_ref.shape[1]
    m = U_MISC * LANES
    o_ref[m:m + 2 * LANES, :] = jnp.zeros((2 * LANES, cols), o_ref.dtype)
    for src, dst, n in _pack_moves():
        o_ref[dst:dst + n, :] = w_ref[src:src + n, :].astype(o_ref.dtype)


def _pack_w_in(w_in_t):
    n_in, d = w_in_t.shape
    tc = 256
    return pl.pallas_call(
        _pack_kernel,
        grid=(d // tc,),
        in_specs=[pl.BlockSpec((n_in, tc), lambda i: (0, i))],
        out_specs=pl.BlockSpec((NP_COLS, tc), lambda i: (0, i)),
        out_shape=jax.ShapeDtypeStruct((NP_COLS, d), MXU_DTYPE),
        compiler_params=_cparams(("parallel",)),
        name="pack",
    )(w_in_t)


def _inproj_kernel(x_ref, mod_ref, g_ref, w_ref, o_ref, misc_ref, h_scr, *, n_x):
    j = pl.program_id(1)
    tx = x_ref.shape[0]
    for part in range(n_x):
        @pl.when(j == part)
        def _():
            y = _rms(x_ref[...], g_ref[...])
            sh = mod_ref[0, 0:1, :]
            sc = mod_ref[0, 1:2, :]
            h_scr[part * tx:(part + 1) * tx, :] = (y * (1.0 + sc) + sh).astype(h_scr.dtype)

    @pl.when(j >= n_x)
    def _():
        acc = _dot_nt(h_scr[...], w_ref[...])
        o_ref[...] = acc.astype(o_ref.dtype)

        @pl.when(j == n_x + MISC_TILE)
        def _():
            misc_ref[...] = acc[:, MISC_OFF:MISC_OFF + LANES]


def _inproj(x2, mod3, g_mix, w_packed, seq):
    t, d = x2.shape
    tm = min(TM_PROJ, seq)
    tx = min(TX_PROJ, tm)
    n_x = tm // tx
    per_b = seq // tm
    wtile = lambda j: jnp.maximum(j - n_x, 0)
    return pl.pallas_call(
        functools.partial(_inproj_kernel, n_x=n_x),
        grid=(t // tm, n_x + NP_COLS // TN_PROJ),
        in_specs=[pl.BlockSpec((tx, d), lambda i, j: (i * n_x + jnp.minimum(j, n_x - 1), 0)),
                  pl.BlockSpec((1, 6, d), lambda i, j: (i // per_b, 0, 0)),
                  pl.BlockSpec((1, d), lambda i, j: (0, 0)),
                  pl.BlockSpec((TN_PROJ, d), lambda i, j: (wtile(j), 0))],
        out_specs=[pl.BlockSpec((tm, TN_PROJ), lambda i, j: (i, wtile(j))),
                   pl.BlockSpec((tm, LANES), lambda i, j: (i, 0))],
        out_shape=[jax.ShapeDtypeStruct((t, NP_COLS), MXU_DTYPE),
                   jax.ShapeDtypeStruct((t, LANES), F32)],
        scratch_shapes=[pltpu.VMEM((tm, d), MXU_DTYPE)],
        compiler_params=_cparams(("parallel", "arbitrary")),
        name="inproj",
    )(x2, mod3, g_mix, w_packed)


TK = 256


def _prep_kernel(ckv_ref, misc_ref, gkv_ref, bf_ref, kv_ref, cum_ref):
    s = ckv_ref.shape[1]
    kv_ref[0] = _rms(ckv_ref[0].astype(F32), gkv_ref[...]).astype(kv_ref.dtype)

    r = lax.broadcasted_iota(I32, (LANES, LANES), 0)
    c = lax.broadcasted_iota(I32, (LANES, LANES), 1)
    tri = jnp.where(c <= r, 1.0, 0.0).astype(MXU_DTYPE)
    carry = jnp.zeros((1, LANES), F32)
    for blk in range(s // LANES):
        z = misc_ref[0, blk * LANES:(blk + 1) * LANES, :] + bf_ref[...]
        ls = jnp.minimum(z, 0.0) - jnp.log1p(jnp.exp(-jnp.abs(z)))
        p1 = ls.astype(MXU_DTYPE)
        r1 = ls - p1.astype(F32)
        p2 = r1.astype(MXU_DTYPE)
        p3 = (r1 - p2.astype(F32)).astype(MXU_DTYPE)
        cs = _dot(tri, p1) + _dot(tri, p2) + _dot(tri, p3) + carry
        carry = cs[LANES - 1:LANES, :]
        cum_ref[0, blk * LANES:(blk + 1) * LANES, :] = cs * (HEAD_DIM ** 0.5)


def _prep(proj3, misc3, g_kv, bf_row):
    b, s, _ = proj3.shape
    return pl.pallas_call(
        _prep_kernel,
        grid=(b,),
        in_specs=[pl.BlockSpec((1, s, D_LAT), lambda i: (i, 0, U_CKV * LANES // D_LAT)),
                  pl.BlockSpec((1, s, LANES), lambda i: (i, 0, 0)),
                  pl.BlockSpec((1, D_LAT), lambda i: (0, 0)),
                  pl.BlockSpec((1, LANES), lambda i: (0, 0))],
        out_specs=[pl.BlockSpec((1, s, D_LAT), lambda i: (i, 0, 0)),
                   pl.BlockSpec((1, s, LANES), lambda i: (i, 0, 0))],
        out_shape=[jax.ShapeDtypeStruct((b, s, D_LAT), MXU_DTYPE),
                   jax.ShapeDtypeStruct((b, s, LANES), F32)],
        compiler_params=_cparams(("parallel",)),
        name="prep",
    )(proj3, misc3, g_kv, bf_row)


TQ_A = 256
N_BISECT = 32
LOG2E = 1.4426950408889634
assert TQ_A == TK


def _pair_loop(n, body):
    def pair(j, carry):
        body(2 * j, 2)
        return carry

    lax.fori_loop(0, jnp.right_shift(n, 1), pair, 0)

    @pl.when(jnp.bitwise_and(n, 1) == 1)
    def _():
        body(n - 1, 1)


def _dsa_kernel(qa_ref, qidx_ref, miscq_ref, misck_ref, kv_ref, wuk_ref, wuv_ref, gqa_ref, o_ref,
                sc_scr, bias_scr, logit_scr, qlat_scr, qh_scr, mm_scr, mpart_scr, lpart_scr, oacc_scr,
                *, topk, n_heads):
    i = pl.program_id(1)
    tq = TQ_A
    nk = i + 1
    q0 = i * tq

    wt = miscq_ref[0].T
    wq = wt[MISC_W:MISC_W + N_IDX_HEADS, :] * (D_IDX ** -0.5 * N_IDX_HEADS ** -0.5)
    for h in range(N_IDX_HEADS):
        qh_scr[h] = qidx_ref[0, :, h * D_IDX:(h + 1) * D_IDX]

    def fold(v, op, rows):
        return op(v.reshape(v.shape[0] // rows, rows, tq), axis=0)

    mm_scr[0:8, :] = jnp.full((8, tq), NEG_BIG, F32)
    mm_scr[8:16, :] = jnp.full((8, tq), -NEG_BIG, F32)

    def score_tiles(kt, nt, last):
        mx8 = mm_scr[0:8, :]
        mn8 = mm_scr[8:16, :]
        for half in range(nt * (TK // LANES)):
            k0 = pl.multiple_of(kt * TK + half * LANES, LANES)
            kx = misck_ref[0, pl.ds(k0, LANES), :][:, MISC_K:MISC_K + D_IDX].astype(MXU_DTYPE)
            acc = jnp.zeros((LANES, tq), F32)
            for h in range(N_IDX_HEADS):
                d = _dot_nt(kx, qh_scr[h])
                acc = acc + jnp.maximum(d, 0.0) * wq[h:h + 1, :]
            if last:
                kpos = k0 + lax.broadcasted_iota(I32, (LANES, tq), 0)
                qpos = q0 + lax.broadcasted_iota(I32, (LANES, tq), 1)
                adm = (kpos // CHUNK) <= (qpos // CHUNK)
                sc_scr[pl.ds(k0, LANES), :] = jnp.where(adm, acc, NEG_BIG)
                mx8 = jnp.maximum(mx8, fold(jnp.where(adm, acc, NEG_BIG), jnp.max, 8))
                mn8 = jnp.minimum(mn8, fold(jnp.where(adm, acc, -NEG_BIG), jnp.min, 8))
            else:
                sc_scr[pl.ds(k0, LANES), :] = acc
                mx8 = jnp.maximum(mx8, fold(acc, jnp.max, 8))
                mn8 = jnp.minimum(mn8, fold(acc, jnp.min, 8))
        mm_scr[0:8, :] = mx8
        mm_scr[8:16, :] = mn8

    _pair_loop(i, lambda kt, nt: score_tiles(kt, nt, False))
    score_tiles(i, 1, True)

    def bis_body(it, carry):
        lo, hi = carry
        mid = lo + 0.5 * (hi - lo)

        def cnt_body(kt, c32):
            k0 = pl.multiple_of(kt * TK, TK)
            return c32 + fold(jnp.where(sc_scr[pl.ds(k0, TK), :] >= mid, 1.0, 0.0), jnp.sum, 32)

        cnt = jnp.sum(lax.fori_loop(0, nk, cnt_body, jnp.zeros((32, tq), F32)), axis=0, keepdims=True)
        ok = cnt >= topk
        return jnp.where(ok, mid, lo), jnp.where(ok, hi, mid)

    thr, _ = lax.fori_loop(0, N_BISECT, bis_body, (jnp.min(mm_scr[8:16, :], axis=0, keepdims=True),
                                                   jnp.max(mm_scr[0:8, :], axis=0, keepdims=True)))

    def bias_body(kt, carry):
        k0 = pl.multiple_of(kt * TK, TK)
        sel_t = sc_scr[pl.ds(k0, TK), :] >= thr
        bias_scr[kt] = jnp.where(sel_t, 0.0, NEG_BIG).T
        return carry

    lax.fori_loop(0, nk, bias_body, 0)

    for h in range(n_heads):
        ql = _dot(qa_ref[0, :, h * HEAD_DIM:(h + 1) * HEAD_DIM], wuk_ref[h])
        ql = _rms(ql, gqa_ref[...]) * (D_LAT ** -0.5 * LOG2E)
        qlat_scr[h * tq:(h + 1) * tq, :] = ql.astype(qlat_scr.dtype)
    mpart_scr[...] = jnp.full(mpart_scr.shape, NEG_BIG, F32)
    lpart_scr[...] = jnp.zeros(lpart_scr.shape, F32)
    oacc_scr[...] = jnp.zeros(oacc_scr.shape, F32)
    slopes2 = [2.0 ** (-8.0 * (h + 1) / n_heads) * LOG2E for h in range(n_heads)]

    def logit_tiles(kt, nt, last):
        k0 = pl.multiple_of(kt * TK, TK)
        kvt = kv_ref[0, pl.ds(k0, nt * TK), :]
        kcol = (k0 + lax.broadcasted_iota(I32, (1, nt * TK), 1)).astype(F32)
        if last:
            ahead = jnp.maximum(lax.broadcasted_iota(I32, (tq, TK), 1)
                                - lax.broadcasted_iota(I32, (tq, TK), 0), 0).astype(F32)
        for h in range(n_heads):
            rows = slice(h * tq, (h + 1) * tq)
            lg = _dot_nt(qlat_scr[rows, :], kvt) + slopes2[h] * kcol
            for t in range(nt):
                lt = lg[:, t * TK:(t + 1) * TK] + bias_scr[kt + t]
                if last:
                    lt = lt - (2.0 * slopes2[h]) * ahead
                logit_scr[kt + t, rows, :] = lt
                mpart_scr[rows, :] = jnp.maximum(mpart_scr[rows, :],
                                                 jnp.maximum(lt[:, :LANES], lt[:, LANES:]))

    _pair_loop(i, lambda kt, nt: logit_tiles(kt, nt, False))
    logit_tiles(i, 1, True)

    m = jnp.max(mpart_scr[...], axis=-1, keepdims=True)
    mpart_scr[...] = jnp.broadcast_to(m, mpart_scr.shape)

    def pv_tiles(kt, nt):
        k0 = pl.multiple_of(kt * TK, TK)
        kvt = kv_ref[0, pl.ds(k0, nt * TK), :]
        for h in range(n_heads):
            rows = slice(h * tq, (h + 1) * tq)
            mb = mpart_scr[rows, :]
            mb2 = jnp.concatenate([mb, mb], axis=1)
            ps = [jnp.exp2(logit_scr[kt + t, rows, :] - mb2) for t in range(nt)]
            lsum = ps[0][:, :LANES] + ps[0][:, LANES:]
            for p in ps[1:]:
                lsum = lsum + p[:, :LANES] + p[:, LANES:]
            lpart_scr[rows, :] += lsum
            p_all = ps[0] if nt == 1 else jnp.concatenate(ps, axis=1)
            oacc_scr[rows, :] += _dot(p_all.astype(MXU_DTYPE), kvt)

    _pair_loop(nk, pv_tiles)

    for h in range(n_heads):
        rows = slice(h * tq, (h + 1) * tq)
        lsum = jnp.sum(lpart_scr[rows, :], axis=-1, keepdims=True)
        o_lat = oacc_scr[rows, :] / lsum
        o_ref[0, :, h * HEAD_DIM:(h + 1) * HEAD_DIM] = _dot(
            o_lat.astype(MXU_DTYPE), wuv_ref[h]).astype(o_ref.dtype)


def _dsa(proj3, misc3, kv, w_uk, w_uv, g_qa):
    b, s, _ = proj3.shape
    n_heads = w_uk.shape[0]
    width = n_heads * HEAD_DIM
    topk = min(TOPK_MAX, s // 4)
    tq = TQ_A
    nkt = s // TK
    kern = functools.partial(_dsa_kernel, topk=topk, n_heads=n_heads)
    return pl.pallas_call(
        kern,
        grid=(b, s // tq),
        in_specs=[pl.BlockSpec((1, tq, width), lambda bb, i: (bb, i, U_QA * LANES // width)),
                  pl.BlockSpec((1, tq, N_IDX_HEADS * D_IDX),
                               lambda bb, i: (bb, i, U_QIDX * LANES // (N_IDX_HEADS * D_IDX))),
                  pl.BlockSpec((1, tq, LANES), lambda bb, i: (bb, i, 0)),
                  pl.BlockSpec((1, s, LANES), lambda bb, i: (bb, 0, 0)),
                  pl.BlockSpec((1, s, D_LAT), lambda bb, i: (bb, 0, 0)),
                  pl.BlockSpec((n_heads, HEAD_DIM, D_LAT), lambda bb, i: (0, 0, 0)),
                  pl.BlockSpec((n_heads, D_LAT, HEAD_DIM), lambda bb, i: (0, 0, 0)),
                  pl.BlockSpec((1, D_LAT), lambda bb, i: (0, 0))],
        out_specs=pl.BlockSpec((1, tq, width), lambda bb, i: (bb, i, 0)),
        out_shape=jax.ShapeDtypeStruct((b, s, width), MXU_DTYPE),
        scratch_shapes=[pltpu.VMEM((s, tq), F32),
                        pltpu.VMEM((nkt, tq, TK), F32),
                        pltpu.VMEM((nkt, n_heads * tq, TK), F32),
                        pltpu.VMEM((n_heads * tq, D_LAT), MXU_DTYPE),
                        pltpu.VMEM((N_IDX_HEADS, tq, D_IDX), MXU_DTYPE),
                        pltpu.VMEM((16, tq), F32),
                        pltpu.VMEM((n_heads * tq, LANES), F32),
                        pltpu.VMEM((n_heads * tq, LANES), F32),
                        pltpu.VMEM((n_heads * tq, D_LAT), F32)],
        compiler_params=_cparams(("parallel", "arbitrary")),
        name="dsa",
    )(proj3, proj3, misc3, misc3, kv, w_uk, w_uv, g_qa)


TQ_B = 256


HG_B = 8
assert TQ_B == TK


def _split3(x):
    p1 = x.astype(MXU_DTYPE)
    r1 = x - p1.astype(F32)
    p2 = r1.astype(MXU_DTYPE)
    p3 = (r1 - p2.astype(F32)).astype(MXU_DTYPE)
    return p1, p2, p3


def _fox_kernel(q_ref, k_ref, v_ref, cumq_ref, cumk_ref, gq_ref, gk_ref, o_ref,
                kaug_scr, qaug_scr, logit_scr, mpart_scr, lpart_scr, oacc_scr):
    g = pl.program_id(1)
    qi = pl.program_id(2)
    tq = TQ_B
    s = k_ref.shape[1]
    cscale = (HEAD_DIM ** -0.5) * LOG2E

    rr = lax.broadcasted_iota(I32, (3 * LANES, LANES), 0)
    cc = lax.broadcasted_iota(I32, (3 * LANES, LANES), 1)
    lane_row = lax.broadcasted_iota(I32, (1, LANES), 1)
    ones_q = jnp.where((lane_row >= 3) & (lane_row < 6), 1.0, 0.0)
    ones_k = jnp.where(lane_row < 3, 1.0, 0.0)
    base_q = jnp.where(cc < 3, rr - LANES * cc, -1)
    base_k = jnp.where((cc >= 3) & (cc < 6), rr - LANES * (cc - 3), -1)

    def aug(pieces, h, base, sign, ones_row):
        e = jnp.where(base == h, sign, 0.0).astype(MXU_DTYPE)
        return (_dot(pieces, e) + ones_row).astype(MXU_DTYPE)

    @pl.when(qi == 0)
    def _():
        def kbody(c, carry):
            r0 = pl.multiple_of(c * TK, TK)
            ck = jnp.concatenate(_split3(cumk_ref[0, pl.ds(r0, TK), :]), axis=1)
            for hh in range(HG_B):
                kn = _rms(k_ref[0, pl.ds(r0, TK), hh * HEAD_DIM:(hh + 1) * HEAD_DIM].astype(F32), gk_ref[...])
                kaug_scr[hh, pl.ds(r0, TK), 0:HEAD_DIM] = kn.astype(MXU_DTYPE)
                kaug_scr[hh, pl.ds(r0, TK), HEAD_DIM:] = aug(ck, MISC_F + g * HG_B + hh, base_k, -1.0, ones_k)
            return carry

        lax.fori_loop(0, s // TK, kbody, 0)

    cq = jnp.concatenate(_split3(cumq_ref[0]), axis=1)
    for hh in range(HG_B):
        qn = _rms(q_ref[0, :, hh * HEAD_DIM:(hh + 1) * HEAD_DIM].astype(F32), gq_ref[...])
        qaug_scr[hh, :, 0:HEAD_DIM] = qn.astype(MXU_DTYPE)
        qaug_scr[hh, :, HEAD_DIM:] = aug(cq, MISC_F + g * HG_B + hh, base_q, 1.0, ones_q)
    mpart_scr[...] = jnp.full(mpart_scr.shape, NEG_BIG, F32)
    lpart_scr[...] = jnp.zeros(lpart_scr.shape, F32)
    oacc_scr[...] = jnp.zeros(oacc_scr.shape, F32)

    def put_logits(kt, hh, lg):
        logit_scr[kt, hh] = lg
        mpart_scr[hh] = jnp.maximum(mpart_scr[hh], jnp.maximum(lg[:, :LANES], lg[:, LANES:]))

    def raw_logits(kt, nt, hh):
        k0 = pl.multiple_of(kt * TK, TK)
        return _dot_nt(qaug_scr[hh], kaug_scr[hh, pl.ds(k0, nt * TK), :]) * cscale

    def off_tiles(kt, nt):
        for hh in range(HG_B):
            lg = raw_logits(kt, nt, hh)
            for t in range(nt):
                put_logits(kt + t, hh, lg[:, t * TK:(t + 1) * TK])

    _pair_loop(qi, off_tiles)
    causal = (lax.broadcasted_iota(I32, (tq, TK), 1) <= lax.broadcasted_iota(I32, (tq, TK), 0))
    for hh in range(HG_B):
        put_logits(qi, hh, jnp.where(causal, raw_logits(qi, 1, hh), NEG_BIG))

    for hh in range(HG_B):
        m = jnp.max(mpart_scr[hh], axis=-1, keepdims=True)
        mpart_scr[hh] = jnp.broadcast_to(m, (tq, LANES))

    def pv_tiles(kt, nt):
        k0 = pl.multiple_of(kt * TK, TK)
        for hh in range(HG_B):
            mb = mpart_scr[hh]
            mb2 = jnp.concatenate([mb, mb], axis=1)
            ps = [jnp.exp2(logit_scr[kt + t, hh] - mb2) for t in range(nt)]
            lsum = ps[0][:, :LANES] + ps[0][:, LANES:]
            for p in ps[1:]:
                lsum = lsum + p[:, :LANES] + p[:, LANES:]
            lpart_scr[hh] += lsum
            p_all = ps[0] if nt == 1 else jnp.concatenate(ps, axis=1)
            oacc_scr[hh] += _dot(p_all.astype(MXU_DTYPE),
                                 v_ref[0, pl.ds(k0, nt * TK), hh * HEAD_DIM:(hh + 1) * HEAD_DIM])

    _pair_loop(qi + 1, pv_tiles)
    for hh in range(HG_B):
        lsum = jnp.sum(lpart_scr[hh], axis=-1, keepdims=True)
        o_ref[0, :, hh * HEAD_DIM:(hh + 1) * HEAD_DIM] = (oacc_scr[hh] / lsum).astype(o_ref.dtype)


def _fox(proj3, cums, g_qb, g_kb, n_heads):
    b, s, _ = proj3.shape
    tq = TQ_B
    gw = HG_B * HEAD_DIM
    return pl.pallas_call(
        _fox_kernel,
        grid=(b, n_heads // HG_B, s // tq),
        in_specs=[pl.BlockSpec((1, tq, gw), lambda bb, g, i: (bb, i, U_QB // HG_B + g)),
                  pl.BlockSpec((1, s, gw), lambda bb, g, i: (bb, 0, U_KB // HG_B + g)),
                  pl.BlockSpec((1, s, gw), lambda bb, g, i: (bb, 0, U_VB // HG_B + g)),
                  pl.BlockSpec((1, tq, LANES), lambda bb, g, i: (bb, i, 0)),
                  pl.BlockSpec((1, s, LANES), lambda bb, g, i: (bb, 0, 0)),
                  pl.BlockSpec((1, HEAD_DIM), lambda bb, g, i: (0, 0)),
                  pl.BlockSpec((1, HEAD_DIM), lambda bb, g, i: (0, 0))],
        out_specs=pl.BlockSpec((1, tq, gw), lambda bb, g, i: (bb, i, g)),
        out_shape=jax.ShapeDtypeStruct((b, s, n_heads * HEAD_DIM), MXU_DTYPE),
        scratch_shapes=[pltpu.VMEM((HG_B, s, 2 * HEAD_DIM), MXU_DTYPE),
                        pltpu.VMEM((HG_B, tq, 2 * HEAD_DIM), MXU_DTYPE),
                        pltpu.VMEM((s // TK, HG_B, tq, TK), F32),
                        pltpu.VMEM((HG_B, tq, LANES), F32),
                        pltpu.VMEM((HG_B, tq, LANES), F32),
                        pltpu.VMEM((HG_B, tq, HEAD_DIM), F32)],
        compiler_params=_cparams(("parallel", "parallel", "arbitrary")),
        name="fox",
    )(proj3, proj3, proj3, cums, cums, g_qb, g_kb)


TM_MERGE = 256


def _merge_kernel(ga_ref, gb_ref, oa_ref, ob_ref, x_ref, mod_ref, bga_ref, bgb_ref, wpa_ref, wpb_ref,
                  wo_ref, gffn_ref, wrh_ref, wrp_ref, br_ref, x1_ref, h2_ref, rl_ref):
    ga = jax.nn.sigmoid(ga_ref[...].astype(F32) + bga_ref[...])
    gb = jax.nn.sigmoid(gb_ref[...].astype(F32) + bgb_ref[...])
    merged = ga * _dot(oa_ref[...], wpa_ref[...]) + gb * _dot(ob_ref[...], wpb_ref[...])
    upd = _dot(merged.astype(MXU_DTYPE), wo_ref[...])
    x1 = x_ref[...] + mod_ref[0, 2:3, :] * upd
    x1_ref[...] = x1
    h2 = _rms(x1, gffn_ref[...]) * (1.0 + mod_ref[0, 4:5, :]) + mod_ref[0, 3:4, :]
    h2_ref[...] = h2
    hh = h2.astype(MXU_DTYPE)
    hl = (h2 - hh.astype(F32)).astype(MXU_DTYPE)
    both = _dot(hh, wrp_ref[...])
    rl_ref[...] = both[:, :LANES] + both[:, LANES:] + _dot(hl, wrh_ref[...]) + br_ref[...]


def _merge(proj, o_a, o_b, x2, mod3, b_gate, w_pa, w_pb, w_o, g_ffn, wr_hi, wr_pair, b_r, seq):
    t, d = x2.shape
    tm = TM_MERGE
    per_b = seq // tm
    wa = o_a.shape[1]
    res = lambda shape: pl.BlockSpec(shape, lambda i: (0,) * len(shape), pipeline_mode=pl.Buffered(1))
    return pl.pallas_call(
        _merge_kernel,
        grid=(t // tm,),
        in_specs=[pl.BlockSpec((tm, d), lambda i: (i, 0)),
                  pl.BlockSpec((tm, d), lambda i: (i, 1)),
                  pl.BlockSpec((tm, wa), lambda i: (i, 0)),
                  pl.BlockSpec((tm, wa), lambda i: (i, 0)),
                  pl.BlockSpec((tm, d), lambda i: (i, 0)),
                  pl.BlockSpec((1, 6, d), lambda i: (i // per_b, 0, 0)),
                  pl.BlockSpec((1, d), lambda i: (0, 0)),
                  pl.BlockSpec((1, d), lambda i: (0, 1)),
                  res((wa, d)), res((wa, d)), res((d, d)),
                  pl.BlockSpec((1, d), lambda i: (0, 0)),
                  res((d, LANES)), res((d, 2 * LANES)),
                  pl.BlockSpec((1, LANES), lambda i: (0, 0))],
        out_specs=[pl.BlockSpec((tm, d), lambda i: (i, 0)),
                   pl.BlockSpec((tm, d), lambda i: (i, 0)),
                   pl.BlockSpec((tm, LANES), lambda i: (i, 0))],
        out_shape=[jax.ShapeDtypeStruct((t, d), F32),
                   jax.ShapeDtypeStruct((t, d), F32),
                   jax.ShapeDtypeStruct((t, LANES), F32)],
        compiler_params=_cparams(("parallel",)),
        name="merge",
    )(proj, proj, o_a, o_b, x2, mod3, b_gate, b_gate, w_pa, w_pb, w_o, g_ffn, wr_hi, wr_pair, b_r)


TM_ROUTE = 1024
TM_ROWS = 256
MM_ROWS = 128
R_E0, R_E1, R_P0, R_P1 = 0, 1, 4, 5
IT_TILE, IT_E, IT_LO, IT_HI, IT_FLAG, IT_NEXT, IT_NEXT2, IT_SLOT = range(8)
F_VALID, F_FIRST_OF_EXPERT, F_FIRST_OF_TILE = 1, 2, 4


def _route_kernel(rl_ref, route_ref, post_ref, items_ref, cnt_scr, run_scr, offs_scr, *, n_tiles):
    tm = rl_ref.shape[0]
    sweep = pl.program_id(0)
    step = pl.program_id(1)
    iw = items_ref.shape[1]

    @pl.when((sweep == 0) & (step == 0))
    def _():
        cnt_scr[...] = jnp.zeros(cnt_scr.shape, F32)

    r = rl_ref[...]
    lane = lax.broadcasted_iota(I32, (tm, LANES), 1).astype(F32)
    neg_inf = -jnp.inf
    gmask = lane < N_GROUPS
    gl = jnp.where(gmask, r, neg_inf)
    gmax = jnp.max(gl, axis=-1, keepdims=True)
    gidx = jnp.min(jnp.where(gl == gmax, lane, float(LANES)), axis=-1, keepdims=True)
    gsum = jnp.sum(jnp.where(gmask, jnp.exp(r - gmax), 0.0), axis=-1, keepdims=True)
    gw = 1.0 / gsum
    lo = N_GROUPS + EXPERTS_PER_GROUP * gidx
    emask = (lane >= lo) & (lane < lo + EXPERTS_PER_GROUP)
    el = jnp.where(emask, r, neg_inf)
    v0 = jnp.max(el, axis=-1, keepdims=True)
    i0 = jnp.min(jnp.where(el == v0, lane, float(LANES)), axis=-1, keepdims=True)
    el2 = jnp.where(lane == i0, neg_inf, el)
    v1 = jnp.max(el2, axis=-1, keepdims=True)
    i1 = jnp.min(jnp.where(el2 == v1, lane, float(LANES)), axis=-1, keepdims=True)
    tt = jnp.exp(v1 - v0)
    p0 = gw / (1.0 + tt)
    p1 = gw * tt / (1.0 + tt)
    e0 = i0 - N_GROUPS
    e1 = i1 - N_GROUPS

    hit0 = lane == e0
    hit1 = lane == e1
    oh = jnp.where(hit0 | hit1, 1.0, 0.0)

    @pl.when(sweep == 0)
    def _():
        ones = jnp.ones((tm, LANES), MXU_DTYPE)
        cnt_scr[...] += _dot(oh.T.astype(MXU_DTYPE), ones)

    @pl.when((sweep == 1) & (step == 0))
    def _():
        _plan_items(cnt_scr[...], items_ref, offs_scr, n_tiles, iw)
        run_scr[...] = jnp.zeros(run_scr.shape, F32)

    @pl.when(sweep == 1)
    def _():
        rr = lax.broadcasted_iota(I32, (tm, tm), 0)
        cc = lax.broadcasted_iota(I32, (tm, tm), 1)
        ltri = jnp.where(cc < rr, 1.0, 0.0).astype(MXU_DTYPE)
        before = _dot(ltri, oh.astype(MXU_DTYPE)) + run_scr[...] + offs_scr[...]
        pos0 = jnp.sum(jnp.where(hit0, before, 0.0), axis=-1, keepdims=True)
        pos1 = jnp.sum(jnp.where(hit1, before, 0.0), axis=-1, keepdims=True)
        run_scr[...] = run_scr[...] + jnp.sum(oh, axis=0, keepdims=True)
        out = jnp.zeros((tm, LANES), F32)
        for k, val in ((R_E0, e0), (R_E1, e1), (R_P0, p0), (R_P1, p1)):
            out = jnp.where(lane == k, val, out)
        route_ref[...] = out
        pmat = jnp.where(lane == 0.0, pos0, jnp.where(lane == 1.0, pos1, 0.0))
        post_ref[...] = pmat.T[0:8, :].astype(I32)


def _plan_items(cnt_col, items_ref, offs_scr, n_tiles, iw):
    tmr = float(TM_ROWS)
    sub = lax.broadcasted_iota(I32, (LANES, LANES), 0)
    lan = lax.broadcasted_iota(I32, (LANES, LANES), 1)
    lstrict = jnp.where(lan < sub, 1.0, 0.0).astype(MXU_DTYPE)
    hi = jnp.floor(cnt_col * (1.0 / LANES))
    lo = cnt_col - hi * LANES
    offs_col = _dot(lstrict, hi.astype(MXU_DTYPE)) * LANES + _dot(lstrict, lo.astype(MXU_DTYPE))
    first_t = jnp.floor(offs_col * (1.0 / tmr))
    last_t = jnp.floor((offs_col + cnt_col - 1.0) * (1.0 / tmr))
    n_col = jnp.where(cnt_col > 0.0, last_t - first_t + 1.0, 0.0)
    base_col = _dot(lstrict, n_col.astype(MXU_DTYPE))
    end_col = base_col + n_col
    offs_scr[...] = offs_col.T[0:1, :]

    rep = lambda col: jnp.concatenate([col] * (iw // LANES), axis=1)
    wl = lax.broadcasted_iota(I32, (LANES, iw), 1).astype(F32)
    sub_f = lax.broadcasted_iota(I32, (LANES, iw), 0).astype(F32)
    end_r = rep(end_col)
    w_total = end_r[LANES - 1:LANES, :]
    count_le = lambda v: jnp.sum(jnp.where(end_r <= v, 1.0, 0.0), axis=0, keepdims=True)
    w_row = wl[0:1, :]
    eidx = count_le(w_row)
    sel = sub_f == eidx
    pick = lambda col: jnp.sum(jnp.where(sel, rep(col), 0.0), axis=0, keepdims=True)
    e_base, e_first, e_offs, e_cnt, e_end = (pick(base_col), pick(first_t), pick(offs_col),
                                             pick(cnt_col), pick(end_col))
    valid = w_row < w_total
    tile = e_first + (w_row - e_base)
    row_lo = jnp.maximum(e_offs, tile * tmr) - tile * tmr
    row_hi = jnp.minimum(e_offs + e_cnt, (tile + 1.0) * tmr) - tile * tmr
    first_e = w_row == e_base
    flags = jnp.where(valid, F_VALID + jnp.where(first_e, float(F_FIRST_OF_EXPERT), 0.0)
                      + jnp.where(row_lo == 0.0, float(F_FIRST_OF_TILE), 0.0), 0.0)
    has1 = valid & first_e & (e_end < w_total)
    e1 = count_le(e_end)
    end1 = jnp.sum(jnp.where(sub_f == e1, end_r, 0.0), axis=0, keepdims=True)
    has2 = has1 & (end1 < w_total)
    nxt = jnp.where(has1, e1, -1.0)
    nxt2 = jnp.where(has2, count_le(end1), -1.0)
    ord_col = _dot(lstrict, jnp.where(cnt_col > 0.0, 1.0, 0.0).astype(MXU_DTYPE))
    e_ord = pick(ord_col)
    slot = e_ord - 2.0 * jnp.floor(e_ord * 0.5)
    e_last = count_le(w_total - 1.0)
    rows = {IT_TILE: jnp.where(valid, tile, n_tiles - 1.0),
            IT_E: jnp.where(valid, eidx, e_last),
            IT_LO: jnp.where(valid, row_lo, 0.0),
            IT_HI: jnp.where(valid, row_hi, 0.0),
            IT_FLAG: flags,
            IT_NEXT: nxt,
            IT_NEXT2: nxt2,
            IT_SLOT: jnp.where(valid, slot, 0.0)}
    sub8 = lax.broadcasted_iota(I32, (8, iw), 0)
    out = jnp.zeros((8, iw), F32)
    for k, val in rows.items():
        out = jnp.where(sub8 == k, val, out)
    items_ref[...] = out.astype(I32)


def _route(rlog, n_items):
    t = rlog.shape[0]
    tm = min(TM_ROUTE, t)
    iw = ((n_items + LANES - 1) // LANES) * LANES
    kern = functools.partial(_route_kernel, n_tiles=2 * t // TM_ROWS)
    return pl.pallas_call(
        kern,
        grid=(2, t // tm),
        in_specs=[pl.BlockSpec((tm, LANES), lambda p, i: (i, 0))],
        out_specs=[pl.BlockSpec((tm, LANES), lambda p, i: (i * p, 0)),
                   pl.BlockSpec((8, tm), lambda p, i: (0, i * p)),
                   pl.BlockSpec((8, iw), lambda p, i: (0, 0))],
        out_shape=[jax.ShapeDtypeStruct((t, LANES), F32),
                   jax.ShapeDtypeStruct((8, t), I32),
                   jax.ShapeDtypeStruct((8, iw), I32)],
        scratch_shapes=[pltpu.VMEM((LANES, LANES), F32),
                        pltpu.VMEM((1, LANES), F32),
                        pltpu.VMEM((1, LANES), F32)],
        compiler_params=_cparams(("arbitrary", "arbitrary")),
        name="route",
    )(rlog)


TM_DISP = 1024


def _dispatch_kernel(pos0_ref, pos1_ref, h2_ref, xs_ref, sem):
    tm = h2_ref.shape[0]
    base = pl.program_id(0) * tm

    def row_copy(r, p):
        return pltpu.make_async_copy(h2_ref.at[pl.ds(r, 1), :], xs_ref.at[pl.ds(p, 1), :], sem)

    def issue(r, carry):
        row_copy(r, pos0_ref[base + r]).start(priority=0)
        row_copy(r, pos1_ref[base + r]).start(priority=1)
        return carry

    lax.fori_loop(0, tm, issue, 0, unroll=8)

    def drain(r, carry):
        row_copy(0, 0).wait()
        row_copy(0, 0).wait()
        return carry

    lax.fori_loop(0, tm, drain, 0, unroll=8)


def _dispatch(pos0, pos1, h2):
    t, d = h2.shape
    tm = min(TM_DISP, t)
    return pl.pallas_call(
        _dispatch_kernel,
        grid_spec=pltpu.PrefetchScalarGridSpec(
            num_scalar_prefetch=2,
            grid=(t // tm,),
            in_specs=[pl.BlockSpec((tm, d), lambda i, p0, p1: (i, 0))],
            out_specs=pl.BlockSpec(memory_space=pl.ANY),
            scratch_shapes=[pltpu.SemaphoreType.DMA(())]),
        out_shape=jax.ShapeDtypeStruct((2 * t, d), h2.dtype),
        compiler_params=pltpu.CompilerParams(dimension_semantics=("arbitrary",),
                                             vmem_limit_bytes=VMEM_LIMIT, has_side_effects=True),
        name="dispatch",
    )(pos0, pos1, h2)


WEIGHT_DMA_PRIORITY = 1


def _experts_kernel(tile_ref, e_ref, lo_ref, hi_ref, flag_ref, next_ref, next2_ref, slot_ref,
                    xs_ref, w1_hbm, w3_hbm, w2_hbm, ys_ref,
                    f1_scr, f3_scr, f2_scr, w1_scr, w3_scr, w2_scr, sems):
    del tile_ref
    w = pl.program_id(0)
    flag = flag_ref[w]

    mats = ((w1_hbm, f1_scr, w1_scr), (w3_hbm, f3_scr, w3_scr), (w2_hbm, f2_scr, w2_scr))

    def weight_copy(k, e, slot):
        return pltpu.make_async_copy(mats[k][0].at[e], mats[k][1].at[slot], sems.at[slot, k])

    @pl.when(w == 0)
    def _():
        for k in range(3):
            weight_copy(k, e_ref[0], 0).start(priority=WEIGHT_DMA_PRIORITY)
        nxt = next_ref[0]

        @pl.when(nxt >= 0)
        def _():
            for k in range(3):
                weight_copy(k, nxt, 1).start(priority=WEIGHT_DMA_PRIORITY)

    @pl.when((flag & F_FIRST_OF_EXPERT) != 0)
    def _():
        slot = slot_ref[w]
        nxt2 = next2_ref[w]
        for k in range(3):
            weight_copy(k, e_ref[w], slot).wait()
            mats[k][2][...] = mats[k][1][slot].astype(mats[k][2].dtype)

            @pl.when(nxt2 >= 0)
            def _():
                weight_copy(k, nxt2, slot).start(priority=WEIGHT_DMA_PRIORITY)

    @pl.when(((flag & F_VALID) != 0) & ((flag & F_FIRST_OF_TILE) != 0))
    def _():
        ys_ref[...] = jnp.zeros(ys_ref.shape, ys_ref.dtype)

    lo = lo_ref[w]
    hi = hi_ref[w]
    for part in range(TM_ROWS // MM_ROWS):
        r0 = part * MM_ROWS

        @pl.when(((flag & F_VALID) != 0) & (lo < r0 + MM_ROWS) & (hi > r0))
        def _():
            x = xs_ref[r0:r0 + MM_ROWS, :].astype(MXU_DTYPE)
            a = _dot(x, w1_scr[...])
            u = _dot(x, w3_scr[...])
            hm = (a * jax.nn.sigmoid(a)) * u
            res = _dot(hm.astype(MXU_DTYPE), w2_scr[...])
            row = r0 + lax.broadcasted_iota(I32, res.shape, 0)
            mine = (row >= lo) & (row < hi)
            ys_ref[r0:r0 + MM_ROWS, :] = jnp.where(mine, res, ys_ref[r0:r0 + MM_ROWS, :])


def _experts(items, xs, w1, w3, w2, n_items):
    n_rows, d = xs.shape
    f = w1.shape[2]
    tm = TM_ROWS
    tile_map = lambda w, tile, *_: (tile[w], 0)
    return pl.pallas_call(
        _experts_kernel,
        grid_spec=pltpu.PrefetchScalarGridSpec(
            num_scalar_prefetch=8,
            grid=(n_items,),
            in_specs=[pl.BlockSpec((tm, d), tile_map),
                      pl.BlockSpec(memory_space=pl.ANY),
                      pl.BlockSpec(memory_space=pl.ANY),
                      pl.BlockSpec(memory_space=pl.ANY)],
            out_specs=pl.BlockSpec((tm, d), tile_map),
            scratch_shapes=[pltpu.VMEM((2, d, f), F32),
                            pltpu.VMEM((2, d, f), F32),
                            pltpu.VMEM((2, f, d), F32),
                            pltpu.VMEM((d, f), MXU_DTYPE),
                            pltpu.VMEM((d, f), MXU_DTYPE),
                            pltpu.VMEM((f, d), MXU_DTYPE),
                            pltpu.SemaphoreType.DMA((2, 3))]),
        out_shape=jax.ShapeDtypeStruct((n_rows, d), F32),
        compiler_params=_cparams(("arbitrary",)),
        name="experts",
    )(items[IT_TILE], items[IT_E], items[IT_LO], items[IT_HI], items[IT_FLAG], items[IT_NEXT],
      items[IT_NEXT2], items[IT_SLOT], xs, w1, w3, w2)


TM_COMB = 256


def _combine_kernel(pos0_ref, pos1_ref, ys_ref, x1_ref, route_ref, mod_ref, o_ref, y0_scr, y1_scr, sems):
    step = pl.program_id(0)
    n_steps = pl.num_programs(0)

    def row_copy(p, dst, slot, r):
        return pltpu.make_async_copy(ys_ref.at[pl.ds(p, 1), :], dst.at[slot, pl.ds(r, 1), :], sems.at[slot])

    def issue_step(s):
        slot = s % 2
        base = s * TM_COMB

        def issue(r, carry):
            row_copy(pos0_ref[base + r], y0_scr, slot, r).start(priority=0)
            row_copy(pos1_ref[base + r], y1_scr, slot, r).start(priority=1)
            return carry

        lax.fori_loop(0, TM_COMB, issue, 0, unroll=8)

    @pl.when(step == 0)
    def _():
        issue_step(step)

    @pl.when(step + 1 < n_steps)
    def _():
        issue_step(step + 1)

    slot = step % 2

    def drain(r, carry):
        row_copy(0, y0_scr, slot, 0).wait()
        row_copy(0, y1_scr, slot, 0).wait()
        return carry

    lax.fori_loop(0, TM_COMB, drain, 0, unroll=8)

    rt = route_ref[...]
    lane = lax.broadcasted_iota(I32, rt.shape, 1)
    p0 = jnp.sum(jnp.where(lane == R_P0, rt, 0.0), axis=-1, keepdims=True)
    p1 = jnp.sum(jnp.where(lane == R_P1, rt, 0.0), axis=-1, keepdims=True)
    y = p0 * y0_scr[slot] + p1 * y1_scr[slot]
    o_ref[...] = x1_ref[...] + mod_ref[0, 5:6, :] * y


def _combine(pos0, pos1, ys, x1, route, mod3, seq):
    t, d = x1.shape
    tm = TM_COMB
    per_b = seq // tm
    return pl.pallas_call(
        _combine_kernel,
        grid_spec=pltpu.PrefetchScalarGridSpec(
            num_scalar_prefetch=2,
            grid=(t // tm,),
            in_specs=[pl.BlockSpec(memory_space=pl.ANY),
                      pl.BlockSpec((tm, d), lambda i, p0, p1: (i, 0)),
                      pl.BlockSpec((tm, LANES), lambda i, p0, p1: (i, 0)),
                      pl.BlockSpec((1, 6, d), lambda i, p0, p1: (i // per_b, 0, 0))],
            out_specs=pl.BlockSpec((tm, d), lambda i, p0, p1: (i, 0)),
            scratch_shapes=[pltpu.VMEM((2, tm, d), F32),
                            pltpu.VMEM((2, tm, d), F32),
                            pltpu.SemaphoreType.DMA((2,))]),
        out_shape=jax.ShapeDtypeStruct((t, d), F32),
        compiler_params=_cparams(("arbitrary",)),
        name="combine",
    )(pos0, pos1, ys, x1, route, mod3)


def kernel(x, c, w_ada, b_ada, g_mix, w_in, b_gate, b_forget, w_uk, w_uv, g_qa, g_kv, g_qb, g_kb,
           w_pa, w_pb, w_o, g_ffn, w_rg, b_rg, w_re, b_re, w1, w3, w2):
    b, s, d = x.shape
    depth = w_ada.shape[0]
    t = b * s
    n_heads_b = b_forget.shape[1]
    n_items = 2 * t // TM_ROWS + N_EXPERTS
    c8 = jnp.zeros((8, d), F32).at[:b].set(c)
    x2 = x.reshape(t, d)
    row = lambda v: v.reshape(1, -1)
    for l in range(depth):
        mod3 = _ada(c8, w_ada[l], row(b_ada[l]))[:b].reshape(b, 6, d)
        proj, misc = _inproj(x2, mod3, row(g_mix[l]), _pack_w_in(jnp.transpose(w_in[l])), s)
        proj3 = proj.reshape(b, s, NP_COLS)
        misc3 = misc.reshape(b, s, LANES)
        bf_row = jnp.zeros((1, LANES), F32).at[0, MISC_F:MISC_F + n_heads_b].set(b_forget[l])
        kv, cums = _prep(proj3, misc3, row(g_kv[l]), bf_row)
        o_a = _dsa(proj3, misc3, kv, w_uk[l].astype(MXU_DTYPE), w_uv[l].astype(MXU_DTYPE), row(g_qa[l]))
        o_b = _fox(proj3, cums, row(g_qb[l]), row(g_kb[l]), n_heads_b)

        w_r = jnp.zeros((d, LANES), F32).at[:, :N_GROUPS].set(w_rg[l])
        w_r = w_r.at[:, N_GROUPS:N_GROUPS + N_EXPERTS].set(w_re[l])
        wr_hi = w_r.astype(MXU_DTYPE)
        wr_pair = jnp.concatenate([wr_hi, (w_r - wr_hi.astype(F32)).astype(MXU_DTYPE)], axis=1)
        b_r = jnp.zeros((1, LANES), F32).at[0, :N_GROUPS].set(b_rg[l])
        b_r = b_r.at[0, N_GROUPS:N_GROUPS + N_EXPERTS].set(b_re[l])
        x1, h2, rlog = _merge(proj, o_a.reshape(t, -1), o_b.reshape(t, -1), x2, mod3, row(b_gate[l]),
                              w_pa[l].astype(MXU_DTYPE), w_pb[l].astype(MXU_DTYPE),
                              w_o[l].astype(MXU_DTYPE), row(g_ffn[l]), wr_hi, wr_pair, b_r, s)
        route, pos_t, items = _route(rlog, n_items)
        pos0, pos1 = pos_t[0], pos_t[1]
        xs = _dispatch(pos0, pos1, h2)
        ys = _experts(items, xs, w1[l], w3[l], w2[l], n_items)
        x2 = _combine(pos0, pos1, ys, x1, route, mod3, s)
    return x2.reshape(b, s, d)
```

```python
import functools

import jax
import jax.numpy as jnp
from jax import lax
from jax.experimental import pallas as pl
from jax.experimental.pallas import tpu as pltpu

F32 = jnp.float32
I32 = jnp.int32
MXU_DTYPE = jnp.bfloat16

CHUNK = 64
HEAD_DIM = 128
D_LAT = 256
N_IDX_HEADS = 16
D_IDX = 64
TOPK_MAX = 256
N_GROUPS = 8
EXPERTS_PER_GROUP = 8
N_EXPERTS = N_GROUPS * EXPERTS_PER_GROUP
RMS_EPS = 1e-6

LANES = 128
VMEM_LIMIT = 56 * 1024 * 1024

NEG_BIG = -1e30
INT_MIN = -2147483648


def _cparams(sem):
    return pltpu.CompilerParams(dimension_semantics=sem, vmem_limit_bytes=VMEM_LIMIT)


def _dot(a, b):
    return jnp.dot(a, b, preferred_element_type=F32)


def _dot_nt(a, b):
    return lax.dot_general(a, b, (((1,), (1,)), ((), ())), preferred_element_type=F32)


def _rms(x, g):
    return x * lax.rsqrt(jnp.mean(x * x, axis=-1, keepdims=True) + RMS_EPS) * g


def _ada_kernel(c_ref, w_ref, b_ref, o_ref):
    c = c_ref[...]
    a = c * jax.nn.sigmoid(c)
    o_ref[...] = _dot(a.astype(MXU_DTYPE), w_ref[...].astype(MXU_DTYPE)) + b_ref[...]


def _ada(c8, w_ada, b_ada):
    d, n = w_ada.shape
    tn = 2048
    return pl.pallas_call(
        _ada_kernel,
        grid=(n // tn,),
        in_specs=[pl.BlockSpec((8, d), lambda j: (0, 0)),
                  pl.BlockSpec((d, tn), lambda j: (0, j)),
                  pl.BlockSpec((1, tn), lambda j: (0, j))],
        out_specs=pl.BlockSpec((8, tn), lambda j: (0, j)),
        out_shape=jax.ShapeDtypeStruct((8, n), F32),
        compiler_params=_cparams(("arbitrary",)),
        name="ada",
    )(c8, w_ada, b_ada)


NP_COLS = 76 * LANES
U_QA, U_QIDX, U_QB, U_KB, U_VB, U_CKV, U_MISC = 32, 40, 48, 56, 64, 72, 74
TN_PROJ = 4 * LANES
TM_PROJ = 2048
TX_PROJ = 1024
MISC_TILE = (U_MISC * LANES) // TN_PROJ
MISC_OFF = U_MISC * LANES - MISC_TILE * TN_PROJ
MISC_K, MISC_F, MISC_W = 0, 64, 72


def _pack_moves():
    sizes = (1024, 256, 1024, 64, 16, 1024, 1024, 1024, 8, 4096)
    src = [0]
    for n in sizes:
        src.append(src[-1] + n)
    q_a, c_kv, q_idx, k_idx, w_idx, q_b, k_b, v_b, f_b, gate = src[:10]
    m = U_MISC * LANES
    return ((gate, 0, 4096), (q_a, U_QA * LANES, 1024), (q_idx, U_QIDX * LANES, 1024),
            (q_b, U_QB * LANES, 1024), (k_b, U_KB * LANES, 1024), (v_b, U_VB * LANES, 1024),
            (c_kv, U_CKV * LANES, 256), (f_b, m + MISC_F, 8), (w_idx, m + MISC_W, 16),
            (k_idx, m + MISC_K, 64))


def _pack_kernel(w_ref, o_ref):
    cols = o_ref.shape[1]
    m = U_MISC * LANES
    o_ref[m:m + 2 * LANES, :] = jnp.zeros((2 * LANES, cols), o_ref.dtype)
    for src, dst, n in _pack_moves():
        o_ref[dst:dst + n, :] = w_ref[src:src + n, :].astype(o_ref.dtype)


def _pack_w_in(w_in_t):
    n_in, d = w_in_t.shape
    tc = 256
    return pl.pallas_call(
        _pack_kernel,
        grid=(d // tc,),
        in_specs=[pl.BlockSpec((n_in, tc), lambda i: (0, i))],
        out_specs=pl.BlockSpec((NP_COLS, tc), lambda i: (0, i)),
        out_shape=jax.ShapeDtypeStruct((NP_COLS, d), MXU_DTYPE),
        compiler_params=_cparams(("parallel",)),
        name="pack",
    )(w_in_t)


def _inproj_kernel(x_ref, mod_ref, g_ref, w_ref, o_ref, misc_ref, h_scr, *, n_x):
    j = pl.program_id(1)
    tx = x_ref.shape[0]
    for part in range(n_x):
        @pl.when(j == part)
        def _():
            y = _rms(x_ref[...], g_ref[...])
            sh = mod_ref[0, 0:1, :]
            sc = mod_ref[0, 1:2, :]
            h_scr[part * tx:(part + 1) * tx, :] = (y * (1.0 + sc) + sh).astype(h_scr.dtype)

    @pl.when(j >= n_x)
    def _():
        acc = _dot_nt(h_scr[...], w_ref[...])
        o_ref[...] = acc.astype(o_ref.dtype)

        @pl.when(j == n_x + MISC_TILE)
        def _():
            misc_ref[...] = acc[:, MISC_OFF:MISC_OFF + LANES]


def _inproj(x2, mod3, g_mix, w_packed, seq):
    t, d = x2.shape
    tm = min(TM_PROJ, seq)
    tx = min(TX_PROJ, tm)
    n_x = tm // tx
    per_b = seq // tm
    wtile = lambda j: jnp.maximum(j - n_x, 0)
    return pl.pallas_call(
        functools.partial(_inproj_kernel, n_x=n_x),
        grid=(t // tm, n_x + NP_COLS // TN_PROJ),
        in_specs=[pl.BlockSpec((tx, d), lambda i, j: (i * n_x + jnp.minimum(j, n_x - 1), 0)),
                  pl.BlockSpec((1, 6, d), lambda i, j: (i // per_b, 0, 0)),
                  pl.BlockSpec((1, d), lambda i, j: (0, 0)),
                  pl.BlockSpec((TN_PROJ, d), lambda i, j: (wtile(j), 0))],
        out_specs=[pl.BlockSpec((tm, TN_PROJ), lambda i, j: (i, wtile(j))),
                   pl.BlockSpec((tm, LANES), lambda i, j: (i, 0))],
        out_shape=[jax.ShapeDtypeStruct((t, NP_COLS), MXU_DTYPE),
                   jax.ShapeDtypeStruct((t, LANES), F32)],
        scratch_shapes=[pltpu.VMEM((tm, d), MXU_DTYPE)],
        compiler_params=_cparams(("parallel", "arbitrary")),
        name="inproj",
    )(x2, mod3, g_mix, w_packed)


TK = 256


def _prep_kernel(ckv_ref, misc_ref, gkv_ref, bf_ref, kv_ref, cum_ref):
    s = ckv_ref.shape[1]
    kv_ref[0] = _rms(ckv_ref[0].astype(F32), gkv_ref[...]).astype(kv_ref.dtype)

    r = lax.broadcasted_iota(I32, (LANES, LANES), 0)
    c = lax.broadcasted_iota(I32, (LANES, LANES), 1)
    tri = jnp.where(c <= r, 1.0, 0.0).astype(MXU_DTYPE)
    carry = jnp.zeros((1, LANES), F32)
    for blk in range(s // LANES):
        z = misc_ref[0, blk * LANES:(blk + 1) * LANES, :] + bf_ref[...]
        ls = jnp.minimum(z, 0.0) - jnp.log1p(jnp.exp(-jnp.abs(z)))
        p1 = ls.astype(MXU_DTYPE)
        r1 = ls - p1.astype(F32)
        p2 = r1.astype(MXU_DTYPE)
        p3 = (r1 - p2.astype(F32)).astype(MXU_DTYPE)
        cs = _dot(tri, p1) + _dot(tri, p2) + _dot(tri, p3) + carry
        carry = cs[LANES - 1:LANES, :]
        cum_ref[0, blk * LANES:(blk + 1) * LANES, :] = cs * (HEAD_DIM ** 0.5)


def _prep(proj3, misc3, g_kv, bf_row):
    b, s, _ = proj3.shape
    return pl.pallas_call(
        _prep_kernel,
        grid=(b,),
        in_specs=[pl.BlockSpec((1, s, D_LAT), lambda i: (i, 0, U_CKV * LANES // D_LAT)),
                  pl.BlockSpec((1, s, LANES), lambda i: (i, 0, 0)),
                  pl.BlockSpec((1, D_LAT), lambda i: (0, 0)),
                  pl.BlockSpec((1, LANES), lambda i: (0, 0))],
        out_specs=[pl.BlockSpec((1, s, D_LAT), lambda i: (i, 0, 0)),
                   pl.BlockSpec((1, s, LANES), lambda i: (i, 0, 0))],
        out_shape=[jax.ShapeDtypeStruct((b, s, D_LAT), MXU_DTYPE),
                   jax.ShapeDtypeStruct((b, s, LANES), F32)],
        compiler_params=_cparams(("parallel",)),
        name="prep",
    )(proj3, misc3, g_kv, bf_row)


TQ_A = 256
N_BISECT = 32
LOG2E = 1.4426950408889634
assert TQ_A == TK


def _pair_loop(n, body):
    def pair(j, carry):
        body(2 * j, 2)
        return carry

    lax.fori_loop(0, jnp.right_shift(n, 1), pair, 0)

    @pl.when(jnp.bitwise_and(n, 1) == 1)
    def _():
        body(n - 1, 1)


def _dsa_kernel(qa_ref, qidx_ref, miscq_ref, misck_ref, kv_ref, wuk_ref, wuv_ref, gqa_ref, o_ref,
                sc_scr, bias_scr, logit_scr, qlat_scr, qh_scr, mm_scr, thr_scr, mpart_scr, lpart_scr, oacc_scr,
                *, topk, n_heads):
    i = pl.program_id(1)
    tq = TQ_A
    nk = i + 1
    q0 = i * tq

    wt = miscq_ref[0].T
    wq = wt[MISC_W:MISC_W + N_IDX_HEADS, :] * (D_IDX ** -0.5 * N_IDX_HEADS ** -0.5)
    for h in range(N_IDX_HEADS):
        qh_scr[h] = qidx_ref[0, :, h * D_IDX:(h + 1) * D_IDX]

    def fold(v, op, rows):
        return op(v.reshape(v.shape[0] // rows, rows, tq), axis=0)

    mm_scr[0:8, :] = jnp.full((8, tq), NEG_BIG, F32)
    mm_scr[8:16, :] = jnp.full((8, tq), -NEG_BIG, F32)

    def score_tiles(kt, nt, last):
        mx8 = mm_scr[0:8, :]
        mn8 = mm_scr[8:16, :]
        for half in range(nt * (TK // LANES)):
            k0 = pl.multiple_of(kt * TK + half * LANES, LANES)
            kx = misck_ref[0, pl.ds(k0, LANES), :][:, MISC_K:MISC_K + D_IDX].astype(MXU_DTYPE)
            acc = jnp.zeros((LANES, tq), F32)
            for h in range(N_IDX_HEADS):
                d = _dot_nt(kx, qh_scr[h])
                acc = acc + jnp.maximum(d, 0.0) * wq[h:h + 1, :]
            if last:
                kpos = k0 + lax.broadcasted_iota(I32, (LANES, tq), 0)
                qpos = q0 + lax.broadcasted_iota(I32, (LANES, tq), 1)
                adm = (kpos // CHUNK) <= (qpos // CHUNK)
                sc_scr[pl.ds(k0, LANES), :] = jnp.where(adm, acc, NEG_BIG)
                mx8 = jnp.maximum(mx8, fold(jnp.where(adm, acc, NEG_BIG), jnp.max, 8))
                mn8 = jnp.minimum(mn8, fold(jnp.where(adm, acc, -NEG_BIG), jnp.min, 8))
            else:
                sc_scr[pl.ds(k0, LANES), :] = acc
                mx8 = jnp.maximum(mx8, fold(acc, jnp.max, 8))
                mn8 = jnp.minimum(mn8, fold(acc, jnp.min, 8))
        mm_scr[0:8, :] = mx8
        mm_scr[8:16, :] = mn8

    _pair_loop(i, lambda kt, nt: score_tiles(kt, nt, False))
    score_tiles(i, 1, True)

    def bisect(n_tiles):
        def bis_body(it, carry):
            lo, hi = carry
            mid = lo + 0.5 * (hi - lo)
            c32 = jnp.zeros((32, tq), F32)
            for kt in range(n_tiles):
                c32 = c32 + fold(jnp.where(sc_scr[kt * TK:(kt + 1) * TK, :] >= mid, 1.0, 0.0), jnp.sum, 32)
            ok = jnp.sum(c32, axis=0, keepdims=True) >= topk
            return jnp.where(ok, mid, lo), jnp.where(ok, hi, mid)

        lo, _ = lax.fori_loop(0, N_BISECT, bis_body, (jnp.min(mm_scr[8:16, :], axis=0, keepdims=True),
                                                      jnp.max(mm_scr[0:8, :], axis=0, keepdims=True)))
        thr_scr[...] = lo

    for c in range(sc_scr.shape[0] // TK):
        @pl.when(i == c)
        def _():
            bisect(c + 1)

    thr = thr_scr[...]

    def bias_body(kt, carry):
        k0 = pl.multiple_of(kt * TK, TK)
        sel_t = sc_scr[pl.ds(k0, TK), :] >= thr
        bias_scr[kt] = jnp.where(sel_t, 0.0, NEG_BIG).T
        return carry

    lax.fori_loop(0, nk, bias_body, 0)

    for h in range(n_heads):
        ql = _dot(qa_ref[0, :, h * HEAD_DIM:(h + 1) * HEAD_DIM], wuk_ref[h])
        ql = _rms(ql, gqa_ref[...]) * (D_LAT ** -0.5 * LOG2E)
        qlat_scr[h * tq:(h + 1) * tq, :] = ql.astype(qlat_scr.dtype)
    mpart_scr[...] = jnp.full(mpart_scr.shape, NEG_BIG, F32)
    lpart_scr[...] = jnp.zeros(lpart_scr.shape, F32)
    oacc_scr[...] = jnp.zeros(oacc_scr.shape, F32)
    slopes2 = [2.0 ** (-8.0 * (h + 1) / n_heads) * LOG2E for h in range(n_heads)]

    def logit_tiles(kt, nt, last):
        k0 = pl.multiple_of(kt * TK, TK)
        kvt = kv_ref[0, pl.ds(k0, nt * TK), :]
        kcol = (k0 + lax.broadcasted_iota(I32, (1, nt * TK), 1)).astype(F32)
        if last:
            ahead = jnp.maximum(lax.broadcasted_iota(I32, (tq, TK), 1)
                                - lax.broadcasted_iota(I32, (tq, TK), 0), 0).astype(F32)
        for h in range(n_heads):
            rows = slice(h * tq, (h + 1) * tq)
            lg = _dot_nt(qlat_scr[rows, :], kvt) + slopes2[h] * kcol
            for t in range(nt):
                lt = lg[:, t * TK:(t + 1) * TK] + bias_scr[kt + t]
                if last:
                    lt = lt - (2.0 * slopes2[h]) * ahead
                logit_scr[kt + t, rows, :] = lt
                mpart_scr[rows, :] = jnp.maximum(mpart_scr[rows, :],
                                                 jnp.maximum(lt[:, :LANES], lt[:, LANES:]))

    _pair_loop(i, lambda kt, nt: logit_tiles(kt, nt, False))
    logit_tiles(i, 1, True)

    m = jnp.max(mpart_scr[...], axis=-1, keepdims=True)
    mpart_scr[...] = jnp.broadcast_to(m, mpart_scr.shape)

    def pv_tiles(kt, nt):
        k0 = pl.multiple_of(kt * TK, TK)
        kvt = kv_ref[0, pl.ds(k0, nt * TK), :]
        for h in range(n_heads):
            rows = slice(h * tq, (h + 1) * tq)
            mb = mpart_scr[rows, :]
            mb2 = jnp.concatenate([mb, mb], axis=1)
            ps = [jnp.exp2(logit_scr[kt + t, rows, :] - mb2) for t in range(nt)]
            lsum = ps[0][:, :LANES] + ps[0][:, LANES:]
            for p in ps[1:]:
                lsum = lsum + p[:, :LANES] + p[:, LANES:]
            lpart_scr[rows, :] += lsum
            p_all = ps[0] if nt == 1 else jnp.concatenate(ps, axis=1)
            oacc_scr[rows, :] += _dot(p_all.astype(MXU_DTYPE), kvt)

    _pair_loop(nk, pv_tiles)

    for h in range(n_heads):
        rows = slice(h * tq, (h + 1) * tq)
        lsum = jnp.sum(lpart_scr[rows, :], axis=-1, keepdims=True)
        o_lat = oacc_scr[rows, :] / lsum
        o_ref[0, :, h * HEAD_DIM:(h + 1) * HEAD_DIM] = _dot(
            o_lat.astype(MXU_DTYPE), wuv_ref[h]).astype(o_ref.dtype)


def _dsa(proj3, misc3, kv, w_uk, w_uv, g_qa):
    b, s, _ = proj3.shape
    n_heads = w_uk.shape[0]
    width = n_heads * HEAD_DIM
    topk = min(TOPK_MAX, s // 4)
    tq = TQ_A
    nkt = s // TK
    kern = functools.partial(_dsa_kernel, topk=topk, n_heads=n_heads)
    return pl.pallas_call(
        kern,
        grid=(b, s // tq),
        in_specs=[pl.BlockSpec((1, tq, width), lambda bb, i: (bb, i, U_QA * LANES // width)),
                  pl.BlockSpec((1, tq, N_IDX_HEADS * D_IDX),
                               lambda bb, i: (bb, i, U_QIDX * LANES // (N_IDX_HEADS * D_IDX))),
                  pl.BlockSpec((1, tq, LANES), lambda bb, i: (bb, i, 0)),
                  pl.BlockSpec((1, s, LANES), lambda bb, i: (bb, 0, 0)),
                  pl.BlockSpec((1, s, D_LAT), lambda bb, i: (bb, 0, 0)),
                  pl.BlockSpec((n_heads, HEAD_DIM, D_LAT), lambda bb, i: (0, 0, 0)),
                  pl.BlockSpec((n_heads, D_LAT, HEAD_DIM), lambda bb, i: (0, 0, 0)),
                  pl.BlockSpec((1, D_LAT), lambda bb, i: (0, 0))],
        out_specs=pl.BlockSpec((1, tq, width), lambda bb, i: (bb, i, 0)),
        out_shape=jax.ShapeDtypeStruct((b, s, width), MXU_DTYPE),
        scratch_shapes=[pltpu.VMEM((s, tq), F32),
                        pltpu.VMEM((nkt, tq, TK), F32),
                        pltpu.VMEM((nkt, n_heads * tq, TK), F32),
                        pltpu.VMEM((n_heads * tq, D_LAT), MXU_DTYPE),
                        pltpu.VMEM((N_IDX_HEADS, tq, D_IDX), MXU_DTYPE),
                        pltpu.VMEM((16, tq), F32),
                        pltpu.VMEM((1, tq), F32),
                        pltpu.VMEM((n_heads * tq, LANES), F32),
                        pltpu.VMEM((n_heads * tq, LANES), F32),
                        pltpu.VMEM((n_heads * tq, D_LAT), F32)],
        compiler_params=_cparams(("parallel", "arbitrary")),
        name="dsa",
    )(proj3, proj3, misc3, misc3, kv, w_uk, w_uv, g_qa)


TQ_B = 256


HG_B = 8
assert TQ_B == TK


def _split3(x):
    p1 = x.astype(MXU_DTYPE)
    r1 = x - p1.astype(F32)
    p2 = r1.astype(MXU_DTYPE)
    p3 = (r1 - p2.astype(F32)).astype(MXU_DTYPE)
    return p1, p2, p3


def _fox_kernel(q_ref, k_ref, v_ref, cumq_ref, cumk_ref, gq_ref, gk_ref, o_ref,
                kaug_scr, qaug_scr, logit_scr, mpart_scr, lpart_scr, oacc_scr):
    g = pl.program_id(1)
    qi = pl.program_id(2)
    tq = TQ_B
    s = k_ref.shape[1]
    cscale = (HEAD_DIM ** -0.5) * LOG2E

    rr = lax.broadcasted_iota(I32, (3 * LANES, LANES), 0)
    cc = lax.broadcasted_iota(I32, (3 * LANES, LANES), 1)
    lane_row = lax.broadcasted_iota(I32, (1, LANES), 1)
    ones_q = jnp.where((lane_row >= 3) & (lane_row < 6), 1.0, 0.0)
    ones_k = jnp.where(lane_row < 3, 1.0, 0.0)
    base_q = jnp.where(cc < 3, rr - LANES * cc, -1)
    base_k = jnp.where((cc >= 3) & (cc < 6), rr - LANES * (cc - 3), -1)

    def aug(pieces, h, base, sign, ones_row):
        e = jnp.where(base == h, sign, 0.0).astype(MXU_DTYPE)
        return (_dot(pieces, e) + ones_row).astype(MXU_DTYPE)

    @pl.when(qi == 0)
    def _():
        def kbody(c, carry):
            r0 = pl.multiple_of(c * TK, TK)
            ck = jnp.concatenate(_split3(cumk_ref[0, pl.ds(r0, TK), :]), axis=1)
            for hh in range(HG_B):
                kn = _rms(k_ref[0, pl.ds(r0, TK), hh * HEAD_DIM:(hh + 1) * HEAD_DIM].astype(F32), gk_ref[...])
                kaug_scr[hh, pl.ds(r0, TK), 0:HEAD_DIM] = kn.astype(MXU_DTYPE)
                kaug_scr[hh, pl.ds(r0, TK), HEAD_DIM:] = aug(ck, MISC_F + g * HG_B + hh, base_k, -1.0, ones_k)
            return carry

        lax.fori_loop(0, s // TK, kbody, 0)

    cq = jnp.concatenate(_split3(cumq_ref[0]), axis=1)
    for hh in range(HG_B):
        qn = _rms(q_ref[0, :, hh * HEAD_DIM:(hh + 1) * HEAD_DIM].astype(F32), gq_ref[...])
        qaug_scr[hh, :, 0:HEAD_DIM] = qn.astype(MXU_DTYPE)
        qaug_scr[hh, :, HEAD_DIM:] = aug(cq, MISC_F + g * HG_B + hh, base_q, 1.0, ones_q)
    mpart_scr[...] = jnp.full(mpart_scr.shape, NEG_BIG, F32)
    lpart_scr[...] = jnp.zeros(lpart_scr.shape, F32)
    oacc_scr[...] = jnp.zeros(oacc_scr.shape, F32)

    def put_logits(kt, hh, lg):
        logit_scr[kt, hh] = lg
        mpart_scr[hh] = jnp.maximum(mpart_scr[hh], jnp.maximum(lg[:, :LANES], lg[:, LANES:]))

    def raw_logits(kt, nt, hh):
        k0 = pl.multiple_of(kt * TK, TK)
        return _dot_nt(qaug_scr[hh], kaug_scr[hh, pl.ds(k0, nt * TK), :]) * cscale

    def off_tiles(kt, nt):
        for hh in range(HG_B):
            lg = raw_logits(kt, nt, hh)
            for t in range(nt):
                put_logits(kt + t, hh, lg[:, t * TK:(t + 1) * TK])

    _pair_loop(qi, off_tiles)
    causal = (lax.broadcasted_iota(I32, (tq, TK), 1) <= lax.broadcasted_iota(I32, (tq, TK), 0))
    for hh in range(HG_B):
        put_logits(qi, hh, jnp.where(causal, raw_logits(qi, 1, hh), NEG_BIG))

    for hh in range(HG_B):
        m = jnp.max(mpart_scr[hh], axis=-1, keepdims=True)
        mpart_scr[hh] = jnp.broadcast_to(m, (tq, LANES))

    def pv_tiles(kt, nt):
        k0 = pl.multiple_of(kt * TK, TK)
        for hh in range(HG_B):
            mb = mpart_scr[hh]
            mb2 = jnp.concatenate([mb, mb], axis=1)
            ps = [jnp.exp2(logit_scr[kt + t, hh] - mb2) for t in range(nt)]
            lsum = ps[0][:, :LANES] + ps[0][:, LANES:]
            for p in ps[1:]:
                lsum = lsum + p[:, :LANES] + p[:, LANES:]
            lpart_scr[hh] += lsum
            p_all = ps[0] if nt == 1 else jnp.concatenate(ps, axis=1)
            oacc_scr[hh] += _dot(p_all.astype(MXU_DTYPE),
                                 v_ref[0, pl.ds(k0, nt * TK), hh * HEAD_DIM:(hh + 1) * HEAD_DIM])

    _pair_loop(qi + 1, pv_tiles)
    for hh in range(HG_B):
        lsum = jnp.sum(lpart_scr[hh], axis=-1, keepdims=True)
        o_ref[0, :, hh * HEAD_DIM:(hh + 1) * HEAD_DIM] = (oacc_scr[hh] / lsum).astype(o_ref.dtype)


def _fox(proj3, cums, g_qb, g_kb, n_heads):
    b, s, _ = proj3.shape
    tq = TQ_B
    gw = HG_B * HEAD_DIM
    return pl.pallas_call(
        _fox_kernel,
        grid=(b, n_heads // HG_B, s // tq),
        in_specs=[pl.BlockSpec((1, tq, gw), lambda bb, g, i: (bb, i, U_QB // HG_B + g)),
                  pl.BlockSpec((1, s, gw), lambda bb, g, i: (bb, 0, U_KB // HG_B + g)),
                  pl.BlockSpec((1, s, gw), lambda bb, g, i: (bb, 0, U_VB // HG_B + g)),
                  pl.BlockSpec((1, tq, LANES), lambda bb, g, i: (bb, i, 0)),
                  pl.BlockSpec((1, s, LANES), lambda bb, g, i: (bb, 0, 0)),
                  pl.BlockSpec((1, HEAD_DIM), lambda bb, g, i: (0, 0)),
                  pl.BlockSpec((1, HEAD_DIM), lambda bb, g, i: (0, 0))],
        out_specs=pl.BlockSpec((1, tq, gw), lambda bb, g, i: (bb, i, g)),
        out_shape=jax.ShapeDtypeStruct((b, s, n_heads * HEAD_DIM), MXU_DTYPE),
        scratch_shapes=[pltpu.VMEM((HG_B, s, 2 * HEAD_DIM), MXU_DTYPE),
                        pltpu.VMEM((HG_B, tq, 2 * HEAD_DIM), MXU_DTYPE),
                        pltpu.VMEM((s // TK, HG_B, tq, TK), F32),
                        pltpu.VMEM((HG_B, tq, LANES), F32),
                        pltpu.VMEM((HG_B, tq, LANES), F32),
                        pltpu.VMEM((HG_B, tq, HEAD_DIM), F32)],
        compiler_params=_cparams(("parallel", "parallel", "arbitrary")),
        name="fox",
    )(proj3, proj3, proj3, cums, cums, g_qb, g_kb)


TM_MERGE = 256


def _merge_kernel(ga_ref, gb_ref, oa_ref, ob_ref, x_ref, mod_ref, bga_ref, bgb_ref, wpa_ref, wpb_ref,
                  wo_ref, gffn_ref, wrh_ref, wrp_ref, br_ref, x1_ref, h2_ref, rl_ref):
    ga = jax.nn.sigmoid(ga_ref[...].astype(F32) + bga_ref[...])
    gb = jax.nn.sigmoid(gb_ref[...].astype(F32) + bgb_ref[...])
    merged = ga * _dot(oa_ref[...], wpa_ref[...]) + gb * _dot(ob_ref[...], wpb_ref[...])
    upd = _dot(merged.astype(MXU_DTYPE), wo_ref[...])
    x1 = x_ref[...] + mod_ref[0, 2:3, :] * upd
    x1_ref[...] = x1
    h2 = _rms(x1, gffn_ref[...]) * (1.0 + mod_ref[0, 4:5, :]) + mod_ref[0, 3:4, :]
    h2_ref[...] = h2
    hh = h2.astype(MXU_DTYPE)
    hl = (h2 - hh.astype(F32)).astype(MXU_DTYPE)
    both = _dot(hh, wrp_ref[...])
    rl_ref[...] = both[:, :LANES] + both[:, LANES:] + _dot(hl, wrh_ref[...]) + br_ref[...]


def _merge(proj, o_a, o_b, x2, mod3, b_gate, w_pa, w_pb, w_o, g_ffn, wr_hi, wr_pair, b_r, seq):
    t, d = x2.shape
    tm = TM_MERGE
    per_b = seq // tm
    wa = o_a.shape[1]
    res = lambda shape: pl.BlockSpec(shape, lambda i: (0,) * len(shape), pipeline_mode=pl.Buffered(1))
    return pl.pallas_call(
        _merge_kernel,
        grid=(t // tm,),
        in_specs=[pl.BlockSpec((tm, d), lambda i: (i, 0)),
                  pl.BlockSpec((tm, d), lambda i: (i, 1)),
                  pl.BlockSpec((tm, wa), lambda i: (i, 0)),
                  pl.BlockSpec((tm, wa), lambda i: (i, 0)),
                  pl.BlockSpec((tm, d), lambda i: (i, 0)),
                  pl.BlockSpec((1, 6, d), lambda i: (i // per_b, 0, 0)),
                  pl.BlockSpec((1, d), lambda i: (0, 0)),
                  pl.BlockSpec((1, d), lambda i: (0, 1)),
                  res((wa, d)), res((wa, d)), res((d, d)),
                  pl.BlockSpec((1, d), lambda i: (0, 0)),
                  res((d, LANES)), res((d, 2 * LANES)),
                  pl.BlockSpec((1, LANES), lambda i: (0, 0))],
        out_specs=[pl.BlockSpec((tm, d), lambda i: (i, 0)),
                   pl.BlockSpec((tm, d), lambda i: (i, 0)),
                   pl.BlockSpec((tm, LANES), lambda i: (i, 0))],
        out_shape=[jax.ShapeDtypeStruct((t, d), F32),
                   jax.ShapeDtypeStruct((t, d), F32),
                   jax.ShapeDtypeStruct((t, LANES), F32)],
        compiler_params=_cparams(("parallel",)),
        name="merge",
    )(proj, proj, o_a, o_b, x2, mod3, b_gate, b_gate, w_pa, w_pb, w_o, g_ffn, wr_hi, wr_pair, b_r)


TM_ROUTE = 1024
TM_ROWS = 256
MM_ROWS = 128
R_E0, R_E1, R_P0, R_P1 = 0, 1, 4, 5
IT_TILE, IT_E, IT_LO, IT_HI, IT_FLAG, IT_NEXT, IT_NEXT2, IT_SLOT = range(8)
F_VALID, F_FIRST_OF_EXPERT, F_FIRST_OF_TILE = 1, 2, 4


def _route_kernel(rl_ref, route_ref, post_ref, items_ref, cnt_scr, run_scr, offs_scr, *, n_tiles):
    tm = rl_ref.shape[0]
    sweep = pl.program_id(0)
    step = pl.program_id(1)
    iw = items_ref.shape[1]

    @pl.when((sweep == 0) & (step == 0))
    def _():
        cnt_scr[...] = jnp.zeros(cnt_scr.shape, F32)

    r = rl_ref[...]
    lane = lax.broadcasted_iota(I32, (tm, LANES), 1).astype(F32)
    neg_inf = -jnp.inf
    gmask = lane < N_GROUPS
    gl = jnp.where(gmask, r, neg_inf)
    gmax = jnp.max(gl, axis=-1, keepdims=True)
    gidx = jnp.min(jnp.where(gl == gmax, lane, float(LANES)), axis=-1, keepdims=True)
    gsum = jnp.sum(jnp.where(gmask, jnp.exp(r - gmax), 0.0), axis=-1, keepdims=True)
    gw = 1.0 / gsum
    lo = N_GROUPS + EXPERTS_PER_GROUP * gidx
    emask = (lane >= lo) & (lane < lo + EXPERTS_PER_GROUP)
    el = jnp.where(emask, r, neg_inf)
    v0 = jnp.max(el, axis=-1, keepdims=True)
    i0 = jnp.min(jnp.where(el == v0, lane, float(LANES)), axis=-1, keepdims=True)
    el2 = jnp.where(lane == i0, neg_inf, el)
    v1 = jnp.max(el2, axis=-1, keepdims=True)
    i1 = jnp.min(jnp.where(el2 == v1, lane, float(LANES)), axis=-1, keepdims=True)
    tt = jnp.exp(v1 - v0)
    p0 = gw / (1.0 + tt)
    p1 = gw * tt / (1.0 + tt)
    e0 = i0 - N_GROUPS
    e1 = i1 - N_GROUPS

    hit0 = lane == e0
    hit1 = lane == e1
    oh = jnp.where(hit0 | hit1, 1.0, 0.0)

    @pl.when(sweep == 0)
    def _():
        ones = jnp.ones((tm, LANES), MXU_DTYPE)
        cnt_scr[...] += _dot(oh.T.astype(MXU_DTYPE), ones)

    @pl.when((sweep == 1) & (step == 0))
    def _():
        _plan_items(cnt_scr[...], items_ref, offs_scr, n_tiles, iw)
        run_scr[...] = jnp.zeros(run_scr.shape, F32)

    @pl.when(sweep == 1)
    def _():
        rr = lax.broadcasted_iota(I32, (tm, tm), 0)
        cc = lax.broadcasted_iota(I32, (tm, tm), 1)
        ltri = jnp.where(cc < rr, 1.0, 0.0).astype(MXU_DTYPE)
        before = _dot(ltri, oh.astype(MXU_DTYPE)) + run_scr[...] + offs_scr[...]
        pos0 = jnp.sum(jnp.where(hit0, before, 0.0), axis=-1, keepdims=True)
        pos1 = jnp.sum(jnp.where(hit1, before, 0.0), axis=-1, keepdims=True)
        run_scr[...] = run_scr[...] + jnp.sum(oh, axis=0, keepdims=True)
        out = jnp.zeros((tm, LANES), F32)
        for k, val in ((R_E0, e0), (R_E1, e1), (R_P0, p0), (R_P1, p1)):
            out = jnp.where(lane == k, val, out)
        route_ref[...] = out
        pmat = jnp.where(lane == 0.0, pos0, jnp.where(lane == 1.0, pos1, 0.0))
        post_ref[...] = pmat.T[0:8, :].astype(I32)


def _plan_items(cnt_col, items_ref, offs_scr, n_tiles, iw):
    tmr = float(TM_ROWS)
    sub = lax.broadcasted_iota(I32, (LANES, LANES), 0)
    lan = lax.broadcasted_iota(I32, (LANES, LANES), 1)
    lstrict = jnp.where(lan < sub, 1.0, 0.0).astype(MXU_DTYPE)
    hi = jnp.floor(cnt_col * (1.0 / LANES))
    lo = cnt_col - hi * LANES
    offs_col = _dot(lstrict, hi.astype(MXU_DTYPE)) * LANES + _dot(lstrict, lo.astype(MXU_DTYPE))
    first_t = jnp.floor(offs_col * (1.0 / tmr))
    last_t = jnp.floor((offs_col + cnt_col - 1.0) * (1.0 / tmr))
    n_col = jnp.where(cnt_col > 0.0, last_t - first_t + 1.0, 0.0)
    base_col = _dot(lstrict, n_col.astype(MXU_DTYPE))
    end_col = base_col + n_col
    offs_scr[...] = offs_col.T[0:1, :]

    rep = lambda col: jnp.concatenate([col] * (iw // LANES), axis=1)
    wl = lax.broadcasted_iota(I32, (LANES, iw), 1).astype(F32)
    sub_f = lax.broadcasted_iota(I32, (LANES, iw), 0).astype(F32)
    end_r = rep(end_col)
    w_total = end_r[LANES - 1:LANES, :]
    count_le = lambda v: jnp.sum(jnp.where(end_r <= v, 1.0, 0.0), axis=0, keepdims=True)
    w_row = wl[0:1, :]
    eidx = count_le(w_row)
    sel = sub_f == eidx
    pick = lambda col: jnp.sum(jnp.where(sel, rep(col), 0.0), axis=0, keepdims=True)
    e_base, e_first, e_offs, e_cnt, e_end = (pick(base_col), pick(first_t), pick(offs_col),
                                             pick(cnt_col), pick(end_col))
    valid = w_row < w_total
    tile = e_first + (w_row - e_base)
    row_lo = jnp.maximum(e_offs, tile * tmr) - tile * tmr
    row_hi = jnp.minimum(e_offs + e_cnt, (tile + 1.0) * tmr) - tile * tmr
    first_e = w_row == e_base
    flags = jnp.where(valid, F_VALID + jnp.where(first_e, float(F_FIRST_OF_EXPERT), 0.0)
                      + jnp.where(row_lo == 0.0, float(F_FIRST_OF_TILE), 0.0), 0.0)
    has1 = valid & first_e & (e_end < w_total)
    e1 = count_le(e_end)
    end1 = jnp.sum(jnp.where(sub_f == e1, end_r, 0.0), axis=0, keepdims=True)
    has2 = has1 & (end1 < w_total)
    nxt = jnp.where(has1, e1, -1.0)
    nxt2 = jnp.where(has2, count_le(end1), -1.0)
    ord_col = _dot(lstrict, jnp.where(cnt_col > 0.0, 1.0, 0.0).astype(MXU_DTYPE))
    e_ord = pick(ord_col)
    slot = e_ord - 2.0 * jnp.floor(e_ord * 0.5)
    e_last = count_le(w_total - 1.0)
    rows = {IT_TILE: jnp.where(valid, tile, n_tiles - 1.0),
            IT_E: jnp.where(valid, eidx, e_last),
            IT_LO: jnp.where(valid, row_lo, 0.0),
            IT_HI: jnp.where(valid, row_hi, 0.0),
            IT_FLAG: flags,
            IT_NEXT: nxt,
            IT_NEXT2: nxt2,
            IT_SLOT: jnp.where(valid, slot, 0.0)}
    sub8 = lax.broadcasted_iota(I32, (8, iw), 0)
    out = jnp.zeros((8, iw), F32)
    for k, val in rows.items():
        out = jnp.where(sub8 == k, val, out)
    items_ref[...] = out.astype(I32)


def _route(rlog, n_items):
    t = rlog.shape[0]
    tm = min(TM_ROUTE, t)
    iw = ((n_items + LANES - 1) // LANES) * LANES
    kern = functools.partial(_route_kernel, n_tiles=2 * t // TM_ROWS)
    return pl.pallas_call(
        kern,
        grid=(2, t // tm),
        in_specs=[pl.BlockSpec((tm, LANES), lambda p, i: (i, 0))],
        out_specs=[pl.BlockSpec((tm, LANES), lambda p, i: (i * p, 0)),
                   pl.BlockSpec((8, tm), lambda p, i: (0, i * p)),
                   pl.BlockSpec((8, iw), lambda p, i: (0, 0))],
        out_shape=[jax.ShapeDtypeStruct((t, LANES), F32),
                   jax.ShapeDtypeStruct((8, t), I32),
                   jax.ShapeDtypeStruct((8, iw), I32)],
        scratch_shapes=[pltpu.VMEM((LANES, LANES), F32),
                        pltpu.VMEM((1, LANES), F32),
                        pltpu.VMEM((1, LANES), F32)],
        compiler_params=_cparams(("arbitrary", "arbitrary")),
        name="route",
    )(rlog)


TM_DISP = 1024


def _dispatch_kernel(pos0_ref, pos1_ref, h2_ref, xs_ref, sem):
    tm = h2_ref.shape[0]
    base = pl.program_id(0) * tm

    def row_copy(r, p):
        return pltpu.make_async_copy(h2_ref.at[pl.ds(r, 1), :], xs_ref.at[pl.ds(p, 1), :], sem)

    def issue(r, carry):
        row_copy(r, pos0_ref[base + r]).start(priority=0)
        row_copy(r, pos1_ref[base + r]).start(priority=1)
        return carry

    lax.fori_loop(0, tm, issue, 0, unroll=8)

    def drain(r, carry):
        row_copy(0, 0).wait()
        row_copy(0, 0).wait()
        return carry

    lax.fori_loop(0, tm, drain, 0, unroll=8)


def _dispatch(pos0, pos1, h2):
    t, d = h2.shape
    tm = min(TM_DISP, t)
    return pl.pallas_call(
        _dispatch_kernel,
        grid_spec=pltpu.PrefetchScalarGridSpec(
            num_scalar_prefetch=2,
            grid=(t // tm,),
            in_specs=[pl.BlockSpec((tm, d), lambda i, p0, p1: (i, 0))],
            out_specs=pl.BlockSpec(memory_space=pl.ANY),
            scratch_shapes=[pltpu.SemaphoreType.DMA(())]),
        out_shape=jax.ShapeDtypeStruct((2 * t, d), h2.dtype),
        compiler_params=pltpu.CompilerParams(dimension_semantics=("arbitrary",),
                                             vmem_limit_bytes=VMEM_LIMIT, has_side_effects=True),
        name="dispatch",
    )(pos0, pos1, h2)


WEIGHT_DMA_PRIORITY = 1


def _experts_kernel(tile_ref, e_ref, lo_ref, hi_ref, flag_ref, next_ref, next2_ref, slot_ref,
                    xs_ref, w1_hbm, w3_hbm, w2_hbm, ys_ref,
                    f1_scr, f3_scr, f2_scr, w1_scr, w3_scr, w2_scr, sems):
    del tile_ref
    w = pl.program_id(0)
    flag = flag_ref[w]

    mats = ((w1_hbm, f1_scr, w1_scr), (w3_hbm, f3_scr, w3_scr), (w2_hbm, f2_scr, w2_scr))

    def weight_copy(k, e, slot):
        return pltpu.make_async_copy(mats[k][0].at[e], mats[k][1].at[slot], sems.at[slot, k])

    @pl.when(w == 0)
    def _():
        for k in range(3):
            weight_copy(k, e_ref[0], 0).start(priority=WEIGHT_DMA_PRIORITY)
        nxt = next_ref[0]

        @pl.when(nxt >= 0)
        def _():
            for k in range(3):
                weight_copy(k, nxt, 1).start(priority=WEIGHT_DMA_PRIORITY)

    @pl.when((flag & F_FIRST_OF_EXPERT) != 0)
    def _():
        slot = slot_ref[w]
        nxt2 = next2_ref[w]
        for k in range(3):
            weight_copy(k, e_ref[w], slot).wait()
            mats[k][2][...] = mats[k][1][slot].astype(mats[k][2].dtype)

            @pl.when(nxt2 >= 0)
            def _():
                weight_copy(k, nxt2, slot).start(priority=WEIGHT_DMA_PRIORITY)

    @pl.when(((flag & F_VALID) != 0) & ((flag & F_FIRST_OF_TILE) != 0))
    def _():
        ys_ref[...] = jnp.zeros(ys_ref.shape, ys_ref.dtype)

    lo = lo_ref[w]
    hi = hi_ref[w]
    for part in range(TM_ROWS // MM_ROWS):
        r0 = part * MM_ROWS

        @pl.when(((flag & F_VALID) != 0) & (lo < r0 + MM_ROWS) & (hi > r0))
        def _():
            x = xs_ref[r0:r0 + MM_ROWS, :].astype(MXU_DTYPE)
            a = _dot(x, w1_scr[...])
            u = _dot(x, w3_scr[...])
            hm = (a * jax.nn.sigmoid(a)) * u
            res = _dot(hm.astype(MXU_DTYPE), w2_scr[...])
            row = r0 + lax.broadcasted_iota(I32, res.shape, 0)
            mine = (row >= lo) & (row < hi)
            ys_ref[r0:r0 + MM_ROWS, :] = jnp.where(mine, res, ys_ref[r0:r0 + MM_ROWS, :])


def _experts(items, xs, w1, w3, w2, n_items):
    n_rows, d = xs.shape
    f = w1.shape[2]
    tm = TM_ROWS
    tile_map = lambda w, tile, *_: (tile[w], 0)
    return pl.pallas_call(
        _experts_kernel,
        grid_spec=pltpu.PrefetchScalarGridSpec(
            num_scalar_prefetch=8,
            grid=(n_items,),
            in_specs=[pl.BlockSpec((tm, d), tile_map),
                      pl.BlockSpec(memory_space=pl.ANY),
                      pl.BlockSpec(memory_space=pl.ANY),
                      pl.BlockSpec(memory_space=pl.ANY)],
            out_specs=pl.BlockSpec((tm, d), tile_map),
            scratch_shapes=[pltpu.VMEM((2, d, f), F32),
                            pltpu.VMEM((2, d, f), F32),
                            pltpu.VMEM((2, f, d), F32),
                            pltpu.VMEM((d, f), MXU_DTYPE),
                            pltpu.VMEM((d, f), MXU_DTYPE),
                            pltpu.VMEM((f, d), MXU_DTYPE),
                            pltpu.SemaphoreType.DMA((2, 3))]),
        out_shape=jax.ShapeDtypeStruct((n_rows, d), F32),
        compiler_params=_cparams(("arbitrary",)),
        name="experts",
    )(items[IT_TILE], items[IT_E], items[IT_LO], items[IT_HI], items[IT_FLAG], items[IT_NEXT],
      items[IT_NEXT2], items[IT_SLOT], xs, w1, w3, w2)


TM_COMB = 256


def _combine_kernel(pos0_ref, pos1_ref, ys_ref, x1_ref, route_ref, mod_ref, o_ref, y0_scr, y1_scr, sems):
    step = pl.program_id(0)
    n_steps = pl.num_programs(0)

    def row_copy(p, dst, slot, r):
        return pltpu.make_async_copy(ys_ref.at[pl.ds(p, 1), :], dst.at[slot, pl.ds(r, 1), :], sems.at[slot])

    def issue_step(s):
        slot = s % 2
        base = s * TM_COMB

        def issue(r, carry):
            row_copy(pos0_ref[base + r], y0_scr, slot, r).start(priority=0)
            row_copy(pos1_ref[base + r], y1_scr, slot, r).start(priority=1)
            return carry

        lax.fori_loop(0, TM_COMB, issue, 0, unroll=8)

    @pl.when(step == 0)
    def _():
        issue_step(step)

    @pl.when(step + 1 < n_steps)
    def _():
        issue_step(step + 1)

    slot = step % 2

    def drain(r, carry):
        row_copy(0, y0_scr, slot, 0).wait()
        row_copy(0, y1_scr, slot, 0).wait()
        return carry

    lax.fori_loop(0, TM_COMB, drain, 0, unroll=8)

    rt = route_ref[...]
    lane = lax.broadcasted_iota(I32, rt.shape, 1)
    p0 = jnp.sum(jnp.where(lane == R_P0, rt, 0.0), axis=-1, keepdims=True)
    p1 = jnp.sum(jnp.where(lane == R_P1, rt, 0.0), axis=-1, keepdims=True)
    y = p0 * y0_scr[slot] + p1 * y1_scr[slot]
    o_ref[...] = x1_ref[...] + mod_ref[0, 5:6, :] * y


def _combine(pos0, pos1, ys, x1, route, mod3, seq):
    t, d = x1.shape
    tm = TM_COMB
    per_b = seq // tm
    return pl.pallas_call(
        _combine_kernel,
        grid_spec=pltpu.PrefetchScalarGridSpec(
            num_scalar_prefetch=2,
            grid=(t // tm,),
            in_specs=[pl.BlockSpec(memory_space=pl.ANY),
                      pl.BlockSpec((tm, d), lambda i, p0, p1: (i, 0)),
                      pl.BlockSpec((tm, LANES), lambda i, p0, p1: (i, 0)),
                      pl.BlockSpec((1, 6, d), lambda i, p0, p1: (i // per_b, 0, 0))],
            out_specs=pl.BlockSpec((tm, d), lambda i, p0, p1: (i, 0)),
            scratch_shapes=[pltpu.VMEM((2, tm, d), F32),
                            pltpu.VMEM((2, tm, d), F32),
                            pltpu.SemaphoreType.DMA((2,))]),
        out_shape=jax.ShapeDtypeStruct((t, d), F32),
        compiler_params=_cparams(("arbitrary",)),
        name="combine",
    )(pos0, pos1, ys, x1, route, mod3)


def kernel(x, c, w_ada, b_ada, g_mix, w_in, b_gate, b_forget, w_uk, w_uv, g_qa, g_kv, g_qb, g_kb,
           w_pa, w_pb, w_o, g_ffn, w_rg, b_rg, w_re, b_re, w1, w3, w2):
    b, s, d = x.shape
    depth = w_ada.shape[0]
    t = b * s
    n_heads_b = b_forget.shape[1]
    n_items = 2 * t // TM_ROWS + N_EXPERTS
    c8 = jnp.zeros((8, d), F32).at[:b].set(c)
    x2 = x.reshape(t, d)
    row = lambda v: v.reshape(1, -1)
    for l in range(depth):
        mod3 = _ada(c8, w_ada[l], row(b_ada[l]))[:b].reshape(b, 6, d)
        proj, misc = _inproj(x2, mod3, row(g_mix[l]), _pack_w_in(jnp.transpose(w_in[l])), s)
        proj3 = proj.reshape(b, s, NP_COLS)
        misc3 = misc.reshape(b, s, LANES)
        bf_row = jnp.zeros((1, LANES), F32).at[0, MISC_F:MISC_F + n_heads_b].set(b_forget[l])
        kv, cums = _prep(proj3, misc3, row(g_kv[l]), bf_row)
        o_a = _dsa(proj3, misc3, kv, w_uk[l].astype(MXU_DTYPE), w_uv[l].astype(MXU_DTYPE), row(g_qa[l]))
        o_b = _fox(proj3, cums, row(g_qb[l]), row(g_kb[l]), n_heads_b)

        w_r = jnp.zeros((d, LANES), F32).at[:, :N_GROUPS].set(w_rg[l])
        w_r = w_r.at[:, N_GROUPS:N_GROUPS + N_EXPERTS].set(w_re[l])
        wr_hi = w_r.astype(MXU_DTYPE)
        wr_pair = jnp.concatenate([wr_hi, (w_r - wr_hi.astype(F32)).astype(MXU_DTYPE)], axis=1)
        b_r = jnp.zeros((1, LANES), F32).at[0, :N_GROUPS].set(b_rg[l])
        b_r = b_r.at[0, N_GROUPS:N_GROUPS + N_EXPERTS].set(b_re[l])
        x1, h2, rlog = _merge(proj, o_a.reshape(t, -1), o_b.reshape(t, -1), x2, mod3, row(b_gate[l]),
                              w_pa[l].astype(MXU_DTYPE), w_pb[l].astype(MXU_DTYPE),
                              w_o[l].astype(MXU_DTYPE), row(g_ffn[l]), wr_hi, wr_pair, b_r, s)
        route, pos_t, items = _route(rlog, n_items)
        pos0, pos1 = pos_t[0], pos_t[1]
        xs = _dispatch(pos0, pos1, h2)
        ys = _experts(items, xs, w1[l], w3[l], w2[l], n_items)
        x2 = _combine(pos0, pos1, ys, x1, route, mod3, s)
    return x2.reshape(b, s, d)
```

```python
import functools

import jax
import jax.numpy as jnp
from jax import lax
from jax.experimental import pallas as pl
from jax.experimental.pallas import tpu as pltpu

F32 = jnp.float32
I32 = jnp.int32
MXU_DTYPE = jnp.bfloat16

CHUNK = 64
HEAD_DIM = 128
D_LAT = 256
N_IDX_HEADS = 16
D_IDX = 64
TOPK_MAX = 256
N_GROUPS = 8
EXPERTS_PER_GROUP = 8
N_EXPERTS = N_GROUPS * EXPERTS_PER_GROUP
RMS_EPS = 1e-6

LANES = 128
VMEM_LIMIT = 56 * 1024 * 1024

NEG_BIG = -1e30
INT_MIN = -2147483648


def _cparams(sem):
    return pltpu.CompilerParams(dimension_semantics=sem, vmem_limit_bytes=VMEM_LIMIT)


def _dot(a, b):
    return jnp.dot(a, b, preferred_element_type=F32)


def _dot_nt(a, b):
    return lax.dot_general(a, b, (((1,), (1,)), ((), ())), preferred_element_type=F32)


def _rms(x, g):
    return x * lax.rsqrt(jnp.mean(x * x, axis=-1, keepdims=True) + RMS_EPS) * g


def _ada_kernel(c_ref, w_ref, b_ref, o_ref):
    c = c_ref[...]
    a = c * jax.nn.sigmoid(c)
    o_ref[...] = _dot(a.astype(MXU_DTYPE), w_ref[...].astype(MXU_DTYPE)) + b_ref[...]


def _ada(c8, w_ada, b_ada):
    d, n = w_ada.shape
    tn = 2048
    return pl.pallas_call(
        _ada_kernel,
        grid=(n // tn,),
        in_specs=[pl.BlockSpec((8, d), lambda j: (0, 0)),
                  pl.BlockSpec((d, tn), lambda j: (0, j)),
                  pl.BlockSpec((1, tn), lambda j: (0, j))],
        out_specs=pl.BlockSpec((8, tn), lambda j: (0, j)),
        out_shape=jax.ShapeDtypeStruct((8, n), F32),
        compiler_params=_cparams(("arbitrary",)),
        name="ada",
    )(c8, w_ada, b_ada)


NP_COLS = 76 * LANES
U_QA, U_QIDX, U_QB, U_KB, U_VB, U_CKV, U_MISC = 32, 40, 48, 56, 64, 72, 74
TN_PROJ = 4 * LANES
TM_PROJ = 2048
TX_PROJ = 1024
MISC_TILE = (U_MISC * LANES) // TN_PROJ
MISC_OFF = U_MISC * LANES - MISC_TILE * TN_PROJ
MISC_K, MISC_F, MISC_W = 0, 64, 72


def _pack_moves():
    sizes = (1024, 256, 1024, 64, 16, 1024, 1024, 1024, 8, 4096)
    src = [0]
    for n in sizes:
        src.append(src[-1] + n)
    q_a, c_kv, q_idx, k_idx, w_idx, q_b, k_b, v_b, f_b, gate = src[:10]
    m = U_MISC * LANES
    return ((gate, 0, 4096), (q_a, U_QA * LANES, 1024), (q_idx, U_QIDX * LANES, 1024),
            (q_b, U_QB * LANES, 1024), (k_b, U_KB * LANES, 1024), (v_b, U_VB * LANES, 1024),
            (c_kv, U_CKV * LANES, 256), (f_b, m + MISC_F, 8), (w_idx, m + MISC_W, 16),
            (k_idx, m + MISC_K, 64))


def _pack_kernel(w_ref, o_ref):
    cols = o_ref.shape[1]
    m = U_MISC * LANES
    o_ref[m:m + 2 * LANES, :] = jnp.zeros((2 * LANES, cols), o_ref.dtype)
    for src, dst, n in _pack_moves():
        o_ref[dst:dst + n, :] = w_ref[src:src + n, :].astype(o_ref.dtype)


def _pack_w_in(w_in_t):
    n_in, d = w_in_t.shape
    tc = 256
    return pl.pallas_call(
        _pack_kernel,
        grid=(d // tc,),
        in_specs=[pl.BlockSpec((n_in, tc), lambda i: (0, i))],
        out_specs=pl.BlockSpec((NP_COLS, tc), lambda i: (0, i)),
        out_shape=jax.ShapeDtypeStruct((NP_COLS, d), MXU_DTYPE),
        compiler_params=_cparams(("parallel",)),
        name="pack",
    )(w_in_t)


def _inproj_kernel(x_ref, mod_ref, g_ref, w_ref, o_ref, misc_ref, h_scr, *, n_x):
    j = pl.program_id(1)
    tx = x_ref.shape[0]
    for part in range(n_x):
        @pl.when(j == part)
        def _():
            y = _rms(x_ref[...], g_ref[...])
            sh = mod_ref[0, 0:1, :]
            sc = mod_ref[0, 1:2, :]
            h_scr[part * tx:(part + 1) * tx, :] = (y * (1.0 + sc) + sh).astype(h_scr.dtype)

    @pl.when(j >= n_x)
    def _():
        acc = _dot_nt(h_scr[...], w_ref[...])
        o_ref[...] = acc.astype(o_ref.dtype)

        @pl.when(j == n_x + MISC_TILE)
        def _():
            misc_ref[...] = acc[:, MISC_OFF:MISC_OFF + LANES]


def _inproj(x2, mod3, g_mix, w_packed, seq):
    t, d = x2.shape
    tm = min(TM_PROJ, seq)
    tx = min(TX_PROJ, tm)
    n_x = tm // tx
    per_b = seq // tm
    wtile = lambda j: jnp.maximum(j - n_x, 0)
    return pl.pallas_call(
        functools.partial(_inproj_kernel, n_x=n_x),
        grid=(t // tm, n_x + NP_COLS // TN_PROJ),
        in_specs=[pl.BlockSpec((tx, d), lambda i, j: (i * n_x + jnp.minimum(j, n_x - 1), 0)),
                  pl.BlockSpec((1, 6, d), lambda i, j: (i // per_b, 0, 0)),
                  pl.BlockSpec((1, d), lambda i, j: (0, 0)),
                  pl.BlockSpec((TN_PROJ, d), lambda i, j: (wtile(j), 0))],
        out_specs=[pl.BlockSpec((tm, TN_PROJ), lambda i, j: (i, wtile(j))),
                   pl.BlockSpec((tm, LANES), lambda i, j: (i, 0))],
        out_shape=[jax.ShapeDtypeStruct((t, NP_COLS), MXU_DTYPE),
                   jax.ShapeDtypeStruct((t, LANES), F32)],
        scratch_shapes=[pltpu.VMEM((tm, d), MXU_DTYPE)],
        compiler_params=_cparams(("parallel", "arbitrary")),
        name="inproj",
    )(x2, mod3, g_mix, w_packed)


TK = 256


def _prep_kernel(ckv_ref, misc_ref, gkv_ref, bf_ref, kv_ref, cum_ref):
    s = ckv_ref.shape[1]
    kv_ref[0] = _rms(ckv_ref[0].astype(F32), gkv_ref[...]).astype(kv_ref.dtype)

    r = lax.broadcasted_iota(I32, (LANES, LANES), 0)
    c = lax.broadcasted_iota(I32, (LANES, LANES), 1)
    tri = jnp.where(c <= r, 1.0, 0.0).astype(MXU_DTYPE)
    carry = jnp.zeros((1, LANES), F32)
    for blk in range(s // LANES):
        z = misc_ref[0, blk * LANES:(blk + 1) * LANES, :] + bf_ref[...]
        ls = jnp.minimum(z, 0.0) - jnp.log1p(jnp.exp(-jnp.abs(z)))
        p1 = ls.astype(MXU_DTYPE)
        r1 = ls - p1.astype(F32)
        p2 = r1.astype(MXU_DTYPE)
        p3 = (r1 - p2.astype(F32)).astype(MXU_DTYPE)
        cs = _dot(tri, p1) + _dot(tri, p2) + _dot(tri, p3) + carry
        carry = cs[LANES - 1:LANES, :]
        cum_ref[0, blk * LANES:(blk + 1) * LANES, :] = cs * (HEAD_DIM ** 0.5)


def _prep(proj3, misc3, g_kv, bf_row):
    b, s, _ = proj3.shape
    return pl.pallas_call(
        _prep_kernel,
        grid=(b,),
        in_specs=[pl.BlockSpec((1, s, D_LAT), lambda i: (i, 0, U_CKV * LANES // D_LAT)),
                  pl.BlockSpec((1, s, LANES), lambda i: (i, 0, 0)),
                  pl.BlockSpec((1, D_LAT), lambda i: (0, 0)),
                  pl.BlockSpec((1, LANES), lambda i: (0, 0))],
        out_specs=[pl.BlockSpec((1, s, D_LAT), lambda i: (i, 0, 0)),
                   pl.BlockSpec((1, s, LANES), lambda i: (i, 0, 0))],
        out_shape=[jax.ShapeDtypeStruct((b, s, D_LAT), MXU_DTYPE),
                   jax.ShapeDtypeStruct((b, s, LANES), F32)],
        compiler_params=_cparams(("parallel",)),
        name="prep",
    )(proj3, misc3, g_kv, bf_row)


TQ_A = 256
N_BISECT = 32
LOG2E = 1.4426950408889634
assert TQ_A == TK


def _pair_loop(n, body):
    def pair(j, carry):
        body(2 * j, 2)
        return carry

    lax.fori_loop(0, jnp.right_shift(n, 1), pair, 0)

    @pl.when(jnp.bitwise_and(n, 1) == 1)
    def _():
        body(n - 1, 1)


def _tile_start(kt):
    return kt * TK if isinstance(kt, int) else pl.multiple_of(kt * TK, TK)


def _static_pairs(idx, n_cases, count, body):
    for c in range(n_cases):
        @pl.when(idx == c)
        def _():
            n = count(c)
            for kt in range(0, n, 2):
                body(kt, min(2, n - kt))


def _dsa_kernel(qa_ref, qidx_ref, miscq_ref, misck_ref, kv_ref, wuk_ref, wuv_ref, gqa_ref, o_ref,
                sc_scr, bias_scr, logit_scr, qlat_scr, qh_scr, mm_scr, thr_scr, mpart_scr, lpart_scr, oacc_scr,
                *, topk, n_heads):
    i = pl.program_id(1)
    tq = TQ_A
    nk = i + 1
    q0 = i * tq

    wt = miscq_ref[0].T
    wq = wt[MISC_W:MISC_W + N_IDX_HEADS, :] * (D_IDX ** -0.5 * N_IDX_HEADS ** -0.5)
    for h in range(N_IDX_HEADS):
        qh_scr[h] = qidx_ref[0, :, h * D_IDX:(h + 1) * D_IDX]

    def fold(v, op, rows):
        return op(v.reshape(v.shape[0] // rows, rows, tq), axis=0)

    mm_scr[0:8, :] = jnp.full((8, tq), NEG_BIG, F32)
    mm_scr[8:16, :] = jnp.full((8, tq), -NEG_BIG, F32)

    def score_tiles(kt, nt, last):
        mx8 = mm_scr[0:8, :]
        mn8 = mm_scr[8:16, :]
        for half in range(nt * (TK // LANES)):
            k0 = pl.multiple_of(kt * TK + half * LANES, LANES)
            kx = misck_ref[0, pl.ds(k0, LANES), :][:, MISC_K:MISC_K + D_IDX].astype(MXU_DTYPE)
            acc = jnp.zeros((LANES, tq), F32)
            for h in range(N_IDX_HEADS):
                d = _dot_nt(kx, qh_scr[h])
                acc = acc + jnp.maximum(d, 0.0) * wq[h:h + 1, :]
            if last:
                kpos = k0 + lax.broadcasted_iota(I32, (LANES, tq), 0)
                qpos = q0 + lax.broadcasted_iota(I32, (LANES, tq), 1)
                adm = (kpos // CHUNK) <= (qpos // CHUNK)
                sc_scr[pl.ds(k0, LANES), :] = jnp.where(adm, acc, NEG_BIG)
                mx8 = jnp.maximum(mx8, fold(jnp.where(adm, acc, NEG_BIG), jnp.max, 8))
                mn8 = jnp.minimum(mn8, fold(jnp.where(adm, acc, -NEG_BIG), jnp.min, 8))
            else:
                sc_scr[pl.ds(k0, LANES), :] = acc
                mx8 = jnp.maximum(mx8, fold(acc, jnp.max, 8))
                mn8 = jnp.minimum(mn8, fold(acc, jnp.min, 8))
        mm_scr[0:8, :] = mx8
        mm_scr[8:16, :] = mn8

    _pair_loop(i, lambda kt, nt: score_tiles(kt, nt, False))
    score_tiles(i, 1, True)

    def bisect(n_tiles):
        def bis_body(it, carry):
            lo, hi = carry
            mid = lo + 0.5 * (hi - lo)
            c32 = jnp.zeros((32, tq), F32)
            for kt in range(n_tiles):
                c32 = c32 + fold(jnp.where(sc_scr[kt * TK:(kt + 1) * TK, :] >= mid, 1.0, 0.0), jnp.sum, 32)
            ok = jnp.sum(c32, axis=0, keepdims=True) >= topk
            return jnp.where(ok, mid, lo), jnp.where(ok, hi, mid)

        lo, _ = lax.fori_loop(0, N_BISECT, bis_body, (jnp.min(mm_scr[8:16, :], axis=0, keepdims=True),
                                                      jnp.max(mm_scr[0:8, :], axis=0, keepdims=True)))
        thr_scr[...] = lo

    for c in range(sc_scr.shape[0] // TK):
        @pl.when(i == c)
        def _():
            bisect(c + 1)

    thr = thr_scr[...]

    def bias_body(kt, carry):
        k0 = _tile_start(kt)
        sel_t = sc_scr[pl.ds(k0, TK), :] >= thr
        bias_scr[kt] = jnp.where(sel_t, 0.0, NEG_BIG).T
        return carry

    lax.fori_loop(0, nk, bias_body, 0)

    for h in range(n_heads):
        ql = _dot(qa_ref[0, :, h * HEAD_DIM:(h + 1) * HEAD_DIM], wuk_ref[h])
        ql = _rms(ql, gqa_ref[...]) * (D_LAT ** -0.5 * LOG2E)
        qlat_scr[h * tq:(h + 1) * tq, :] = ql.astype(qlat_scr.dtype)
    mpart_scr[...] = jnp.full(mpart_scr.shape, NEG_BIG, F32)
    lpart_scr[...] = jnp.zeros(lpart_scr.shape, F32)
    oacc_scr[...] = jnp.zeros(oacc_scr.shape, F32)
    slopes2 = [2.0 ** (-8.0 * (h + 1) / n_heads) * LOG2E for h in range(n_heads)]

    def logit_tiles(kt, nt, last):
        k0 = _tile_start(kt)
        kvt = kv_ref[0, pl.ds(k0, nt * TK), :]
        kcol = (k0 + lax.broadcasted_iota(I32, (1, nt * TK), 1)).astype(F32)
        if last:
            ahead = jnp.maximum(lax.broadcasted_iota(I32, (tq, TK), 1)
                                - lax.broadcasted_iota(I32, (tq, TK), 0), 0).astype(F32)
        for h in range(n_heads):
            rows = slice(h * tq, (h + 1) * tq)
            lg = _dot_nt(qlat_scr[rows, :], kvt) + slopes2[h] * kcol
            for t in range(nt):
                lt = lg[:, t * TK:(t + 1) * TK] + bias_scr[kt + t]
                if last:
                    lt = lt - (2.0 * slopes2[h]) * ahead
                logit_scr[kt + t, rows, :] = lt
                mpart_scr[rows, :] = jnp.maximum(mpart_scr[rows, :],
                                                 jnp.maximum(lt[:, :LANES], lt[:, LANES:]))

    _pair_loop(i, lambda kt, nt: logit_tiles(kt, nt, False))
    logit_tiles(i, 1, True)

    m = jnp.max(mpart_scr[...], axis=-1, keepdims=True)
    mpart_scr[...] = jnp.broadcast_to(m, mpart_scr.shape)

    def pv_tiles(kt, nt):
        k0 = _tile_start(kt)
        kvt = kv_ref[0, pl.ds(k0, nt * TK), :]
        for h in range(n_heads):
            rows = slice(h * tq, (h + 1) * tq)
            mb = mpart_scr[rows, :]
            mb2 = jnp.concatenate([mb, mb], axis=1)
            ps = [jnp.exp2(logit_scr[kt + t, rows, :] - mb2) for t in range(nt)]
            lsum = ps[0][:, :LANES] + ps[0][:, LANES:]
            for p in ps[1:]:
                lsum = lsum + p[:, :LANES] + p[:, LANES:]
            lpart_scr[rows, :] += lsum
            p_all = ps[0] if nt == 1 else jnp.concatenate(ps, axis=1)
            oacc_scr[rows, :] += _dot(p_all.astype(MXU_DTYPE), kvt)

    _pair_loop(nk, pv_tiles)

    for h in range(n_heads):
        rows = slice(h * tq, (h + 1) * tq)
        lsum = jnp.sum(lpart_scr[rows, :], axis=-1, keepdims=True)
        o_lat = oacc_scr[rows, :] / lsum
        o_ref[0, :, h * HEAD_DIM:(h + 1) * HEAD_DIM] = _dot(
            o_lat.astype(MXU_DTYPE), wuv_ref[h]).astype(o_ref.dtype)


def _dsa(proj3, misc3, kv, w_uk, w_uv, g_qa):
    b, s, _ = proj3.shape
    n_heads = w_uk.shape[0]
    width = n_heads * HEAD_DIM
    topk = min(TOPK_MAX, s // 4)
    tq = TQ_A
    nkt = s // TK
    kern = functools.partial(_dsa_kernel, topk=topk, n_heads=n_heads)
    return pl.pallas_call(
        kern,
        grid=(b, s // tq),
        in_specs=[pl.BlockSpec((1, tq, width), lambda bb, i: (bb, i, U_QA * LANES // width)),
                  pl.BlockSpec((1, tq, N_IDX_HEADS * D_IDX),
                               lambda bb, i: (bb, i, U_QIDX * LANES // (N_IDX_HEADS * D_IDX))),
                  pl.BlockSpec((1, tq, LANES), lambda bb, i: (bb, i, 0)),
                  pl.BlockSpec((1, s, LANES), lambda bb, i: (bb, 0, 0)),
                  pl.BlockSpec((1, s, D_LAT), lambda bb, i: (bb, 0, 0)),
                  pl.BlockSpec((n_heads, HEAD_DIM, D_LAT), lambda bb, i: (0, 0, 0)),
                  pl.BlockSpec((n_heads, D_LAT, HEAD_DIM), lambda bb, i: (0, 0, 0)),
                  pl.BlockSpec((1, D_LAT), lambda bb, i: (0, 0))],
        out_specs=pl.BlockSpec((1, tq, width), lambda bb, i: (bb, i, 0)),
        out_shape=jax.ShapeDtypeStruct((b, s, width), MXU_DTYPE),
        scratch_shapes=[pltpu.VMEM((s, tq), F32),
                        pltpu.VMEM((nkt, tq, TK), F32),
                        pltpu.VMEM((nkt, n_heads * tq, TK), F32),
                        pltpu.VMEM((n_heads * tq, D_LAT), MXU_DTYPE),
                        pltpu.VMEM((N_IDX_HEADS, tq, D_IDX), MXU_DTYPE),
                        pltpu.VMEM((16, tq), F32),
                        pltpu.VMEM((1, tq), F32),
                        pltpu.VMEM((n_heads * tq, LANES), F32),
                        pltpu.VMEM((n_heads * tq, LANES), F32),
                        pltpu.VMEM((n_heads * tq, D_LAT), F32)],
        compiler_params=_cparams(("parallel", "arbitrary")),
        name="dsa",
    )(proj3, proj3, misc3, misc3, kv, w_uk, w_uv, g_qa)


TQ_B = 256


HG_B = 8
assert TQ_B == TK


def _split3(x):
    p1 = x.astype(MXU_DTYPE)
    r1 = x - p1.astype(F32)
    p2 = r1.astype(MXU_DTYPE)
    p3 = (r1 - p2.astype(F32)).astype(MXU_DTYPE)
    return p1, p2, p3


def _fox_kernel(q_ref, k_ref, v_ref, cumq_ref, cumk_ref, gq_ref, gk_ref, o_ref,
                kaug_scr, qaug_scr, logit_scr, mpart_scr, lpart_scr, oacc_scr):
    g = pl.program_id(1)
    qi = pl.program_id(2)
    tq = TQ_B
    s = k_ref.shape[1]
    cscale = (HEAD_DIM ** -0.5) * LOG2E

    rr = lax.broadcasted_iota(I32, (3 * LANES, LANES), 0)
    cc = lax.broadcasted_iota(I32, (3 * LANES, LANES), 1)
    lane_row = lax.broadcasted_iota(I32, (1, LANES), 1)
    ones_q = jnp.where((lane_row >= 3) & (lane_row < 6), 1.0, 0.0)
    ones_k = jnp.where(lane_row < 3, 1.0, 0.0)
    base_q = jnp.where(cc < 3, rr - LANES * cc, -1)
    base_k = jnp.where((cc >= 3) & (cc < 6), rr - LANES * (cc - 3), -1)

    def aug(pieces, h, base, sign, ones_row):
        e = jnp.where(base == h, sign, 0.0).astype(MXU_DTYPE)
        return (_dot(pieces, e) + ones_row).astype(MXU_DTYPE)

    @pl.when(qi == 0)
    def _():
        def kbody(c, carry):
            r0 = pl.multiple_of(c * TK, TK)
            ck = jnp.concatenate(_split3(cumk_ref[0, pl.ds(r0, TK), :]), axis=1)
            for hh in range(HG_B):
                kn = _rms(k_ref[0, pl.ds(r0, TK), hh * HEAD_DIM:(hh + 1) * HEAD_DIM].astype(F32), gk_ref[...])
                kaug_scr[hh, pl.ds(r0, TK), 0:HEAD_DIM] = kn.astype(MXU_DTYPE)
                kaug_scr[hh, pl.ds(r0, TK), HEAD_DIM:] = aug(ck, MISC_F + g * HG_B + hh, base_k, -1.0, ones_k)
            return carry

        lax.fori_loop(0, s // TK, kbody, 0)

    cq = jnp.concatenate(_split3(cumq_ref[0]), axis=1)
    for hh in range(HG_B):
        qn = _rms(q_ref[0, :, hh * HEAD_DIM:(hh + 1) * HEAD_DIM].astype(F32), gq_ref[...])
        qaug_scr[hh, :, 0:HEAD_DIM] = qn.astype(MXU_DTYPE)
        qaug_scr[hh, :, HEAD_DIM:] = aug(cq, MISC_F + g * HG_B + hh, base_q, 1.0, ones_q)
    mpart_scr[...] = jnp.full(mpart_scr.shape, NEG_BIG, F32)
    lpart_scr[...] = jnp.zeros(lpart_scr.shape, F32)
    oacc_scr[...] = jnp.zeros(oacc_scr.shape, F32)

    def put_logits(kt, hh, lg):
        logit_scr[kt, hh] = lg
        mpart_scr[hh] = jnp.maximum(mpart_scr[hh], jnp.maximum(lg[:, :LANES], lg[:, LANES:]))

    def raw_logits(kt, nt, hh):
        k0 = _tile_start(kt)
        return _dot_nt(qaug_scr[hh], kaug_scr[hh, pl.ds(k0, nt * TK), :]) * cscale

    def off_tiles(kt, nt):
        for hh in range(HG_B):
            lg = raw_logits(kt, nt, hh)
            for t in range(nt):
                put_logits(kt + t, hh, lg[:, t * TK:(t + 1) * TK])

    _static_pairs(qi, s // TK, lambda c: c, off_tiles)
    causal = (lax.broadcasted_iota(I32, (tq, TK), 1) <= lax.broadcasted_iota(I32, (tq, TK), 0))
    for hh in range(HG_B):
        put_logits(qi, hh, jnp.where(causal, raw_logits(qi, 1, hh), NEG_BIG))

    for hh in range(HG_B):
        m = jnp.max(mpart_scr[hh], axis=-1, keepdims=True)
        mpart_scr[hh] = jnp.broadcast_to(m, (tq, LANES))

    def pv_tiles(kt, nt):
        k0 = _tile_start(kt)
        for hh in range(HG_B):
            mb = mpart_scr[hh]
            mb2 = jnp.concatenate([mb, mb], axis=1)
            ps = [jnp.exp2(logit_scr[kt + t, hh] - mb2) for t in range(nt)]
            lsum = ps[0][:, :LANES] + ps[0][:, LANES:]
            for p in ps[1:]:
                lsum = lsum + p[:, :LANES] + p[:, LANES:]
            lpart_scr[hh] += lsum
            p_all = ps[0] if nt == 1 else jnp.concatenate(ps, axis=1)
            oacc_scr[hh] += _dot(p_all.astype(MXU_DTYPE),
                                 v_ref[0, pl.ds(k0, nt * TK), hh * HEAD_DIM:(hh + 1) * HEAD_DIM])

    _static_pairs(qi, s // TK, lambda c: c + 1, pv_tiles)
    for hh in range(HG_B):
        lsum = jnp.sum(lpart_scr[hh], axis=-1, keepdims=True)
        o_ref[0, :, hh * HEAD_DIM:(hh + 1) * HEAD_DIM] = (oacc_scr[hh] / lsum).astype(o_ref.dtype)


def _fox(proj3, cums, g_qb, g_kb, n_heads):
    b, s, _ = proj3.shape
    tq = TQ_B
    gw = HG_B * HEAD_DIM
    return pl.pallas_call(
        _fox_kernel,
        grid=(b, n_heads // HG_B, s // tq),
        in_specs=[pl.BlockSpec((1, tq, gw), lambda bb, g, i: (bb, i, U_QB // HG_B + g)),
                  pl.BlockSpec((1, s, gw), lambda bb, g, i: (bb, 0, U_KB // HG_B + g)),
                  pl.BlockSpec((1, s, gw), lambda bb, g, i: (bb, 0, U_VB // HG_B + g)),
                  pl.BlockSpec((1, tq, LANES), lambda bb, g, i: (bb, i, 0)),
                  pl.BlockSpec((1, s, LANES), lambda bb, g, i: (bb, 0, 0)),
                  pl.BlockSpec((1, HEAD_DIM), lambda bb, g, i: (0, 0)),
                  pl.BlockSpec((1, HEAD_DIM), lambda bb, g, i: (0, 0))],
        out_specs=pl.BlockSpec((1, tq, gw), lambda bb, g, i: (bb, i, g)),
        out_shape=jax.ShapeDtypeStruct((b, s, n_heads * HEAD_DIM), MXU_DTYPE),
        scratch_shapes=[pltpu.VMEM((HG_B, s, 2 * HEAD_DIM), MXU_DTYPE),
                        pltpu.VMEM((HG_B, tq, 2 * HEAD_DIM), MXU_DTYPE),
                        pltpu.VMEM((s // TK, HG_B, tq, TK), F32),
                        pltpu.VMEM((HG_B, tq, LANES), F32),
                        pltpu.VMEM((HG_B, tq, LANES), F32),
                        pltpu.VMEM((HG_B, tq, HEAD_DIM), F32)],
        compiler_params=_cparams(("parallel", "parallel", "arbitrary")),
        name="fox",
    )(proj3, proj3, proj3, cums, cums, g_qb, g_kb)


TM_MERGE = 256


def _merge_kernel(ga_ref, gb_ref, oa_ref, ob_ref, x_ref, mod_ref, bga_ref, bgb_ref, wpa_ref, wpb_ref,
                  wo_ref, gffn_ref, wrh_ref, wrp_ref, br_ref, x1_ref, h2_ref, rl_ref):
    ga = jax.nn.sigmoid(ga_ref[...].astype(F32) + bga_ref[...])
    gb = jax.nn.sigmoid(gb_ref[...].astype(F32) + bgb_ref[...])
    merged = ga * _dot(oa_ref[...], wpa_ref[...]) + gb * _dot(ob_ref[...], wpb_ref[...])
    upd = _dot(merged.astype(MXU_DTYPE), wo_ref[...])
    x1 = x_ref[...] + mod_ref[0, 2:3, :] * upd
    x1_ref[...] = x1
    h2 = _rms(x1, gffn_ref[...]) * (1.0 + mod_ref[0, 4:5, :]) + mod_ref[0, 3:4, :]
    h2_ref[...] = h2
    hh = h2.astype(MXU_DTYPE)
    hl = (h2 - hh.astype(F32)).astype(MXU_DTYPE)
    both = _dot(hh, wrp_ref[...])
    rl_ref[...] = both[:, :LANES] + both[:, LANES:] + _dot(hl, wrh_ref[...]) + br_ref[...]


def _merge(proj, o_a, o_b, x2, mod3, b_gate, w_pa, w_pb, w_o, g_ffn, wr_hi, wr_pair, b_r, seq):
    t, d = x2.shape
    tm = TM_MERGE
    per_b = seq // tm
    wa = o_a.shape[1]
    res = lambda shape: pl.BlockSpec(shape, lambda i: (0,) * len(shape), pipeline_mode=pl.Buffered(1))
    return pl.pallas_call(
        _merge_kernel,
        grid=(t // tm,),
        in_specs=[pl.BlockSpec((tm, d), lambda i: (i, 0)),
                  pl.BlockSpec((tm, d), lambda i: (i, 1)),
                  pl.BlockSpec((tm, wa), lambda i: (i, 0)),
                  pl.BlockSpec((tm, wa), lambda i: (i, 0)),
                  pl.BlockSpec((tm, d), lambda i: (i, 0)),
                  pl.BlockSpec((1, 6, d), lambda i: (i // per_b, 0, 0)),
                  pl.BlockSpec((1, d), lambda i: (0, 0)),
                  pl.BlockSpec((1, d), lambda i: (0, 1)),
                  res((wa, d)), res((wa, d)), res((d, d)),
                  pl.BlockSpec((1, d), lambda i: (0, 0)),
                  res((d, LANES)), res((d, 2 * LANES)),
                  pl.BlockSpec((1, LANES), lambda i: (0, 0))],
        out_specs=[pl.BlockSpec((tm, d), lambda i: (i, 0)),
                   pl.BlockSpec((tm, d), lambda i: (i, 0)),
                   pl.BlockSpec((tm, LANES), lambda i: (i, 0))],
        out_shape=[jax.ShapeDtypeStruct((t, d), F32),
                   jax.ShapeDtypeStruct((t, d), F32),
                   jax.ShapeDtypeStruct((t, LANES), F32)],
        compiler_params=_cparams(("parallel",)),
        name="merge",
    )(proj, proj, o_a, o_b, x2, mod3, b_gate, b_gate, w_pa, w_pb, w_o, g_ffn, wr_hi, wr_pair, b_r)


TM_ROUTE = 1024
TM_ROWS = 256
MM_ROWS = 128
R_E0, R_E1, R_P0, R_P1 = 0, 1, 4, 5
IT_TILE, IT_E, IT_LO, IT_HI, IT_FLAG, IT_NEXT, IT_NEXT2, IT_SLOT = range(8)
F_VALID, F_FIRST_OF_EXPERT, F_FIRST_OF_TILE = 1, 2, 4


def _route_kernel(rl_ref, route_ref, post_ref, items_ref, cnt_scr, run_scr, offs_scr, *, n_tiles):
    tm = rl_ref.shape[0]
    sweep = pl.program_id(0)
    step = pl.program_id(1)
    iw = items_ref.shape[1]

    @pl.when((sweep == 0) & (step == 0))
    def _():
        cnt_scr[...] = jnp.zeros(cnt_scr.shape, F32)

    r = rl_ref[...]
    lane = lax.broadcasted_iota(I32, (tm, LANES), 1).astype(F32)
    neg_inf = -jnp.inf
    gmask = lane < N_GROUPS
    gl = jnp.where(gmask, r, neg_inf)
    gmax = jnp.max(gl, axis=-1, keepdims=True)
    gidx = jnp.min(jnp.where(gl == gmax, lane, float(LANES)), axis=-1, keepdims=True)
    gsum = jnp.sum(jnp.where(gmask, jnp.exp(r - gmax), 0.0), axis=-1, keepdims=True)
    gw = 1.0 / gsum
    lo = N_GROUPS + EXPERTS_PER_GROUP * gidx
    emask = (lane >= lo) & (lane < lo + EXPERTS_PER_GROUP)
    el = jnp.where(emask, r, neg_inf)
    v0 = jnp.max(el, axis=-1, keepdims=True)
    i0 = jnp.min(jnp.where(el == v0, lane, float(LANES)), axis=-1, keepdims=True)
    el2 = jnp.where(lane == i0, neg_inf, el)
    v1 = jnp.max(el2, axis=-1, keepdims=True)
    i1 = jnp.min(jnp.where(el2 == v1, lane, float(LANES)), axis=-1, keepdims=True)
    tt = jnp.exp(v1 - v0)
    p0 = gw / (1.0 + tt)
    p1 = gw * tt / (1.0 + tt)
    e0 = i0 - N_GROUPS
    e1 = i1 - N_GROUPS

    hit0 = lane == e0
    hit1 = lane == e1
    oh = jnp.where(hit0 | hit1, 1.0, 0.0)

    @pl.when(sweep == 0)
    def _():
        ones = jnp.ones((tm, LANES), MXU_DTYPE)
        cnt_scr[...] += _dot(oh.T.astype(MXU_DTYPE), ones)

    @pl.when((sweep == 1) & (step == 0))
    def _():
        _plan_items(cnt_scr[...], items_ref, offs_scr, n_tiles, iw)
        run_scr[...] = jnp.zeros(run_scr.shape, F32)

    @pl.when(sweep == 1)
    def _():
        rr = lax.broadcasted_iota(I32, (tm, tm), 0)
        cc = lax.broadcasted_iota(I32, (tm, tm), 1)
        ltri = jnp.where(cc < rr, 1.0, 0.0).astype(MXU_DTYPE)
        before = _dot(ltri, oh.astype(MXU_DTYPE)) + run_scr[...] + offs_scr[...]
        pos0 = jnp.sum(jnp.where(hit0, before, 0.0), axis=-1, keepdims=True)
        pos1 = jnp.sum(jnp.where(hit1, before, 0.0), axis=-1, keepdims=True)
        run_scr[...] = run_scr[...] + jnp.sum(oh, axis=0, keepdims=True)
        out = jnp.zeros((tm, LANES), F32)
        for k, val in ((R_E0, e0), (R_E1, e1), (R_P0, p0), (R_P1, p1)):
            out = jnp.where(lane == k, val, out)
        route_ref[...] = out
        pmat = jnp.where(lane == 0.0, pos0, jnp.where(lane == 1.0, pos1, 0.0))
        post_ref[...] = pmat.T[0:8, :].astype(I32)


def _plan_items(cnt_col, items_ref, offs_scr, n_tiles, iw):
    tmr = float(TM_ROWS)
    sub = lax.broadcasted_iota(I32, (LANES, LANES), 0)
    lan = lax.broadcasted_iota(I32, (LANES, LANES), 1)
    lstrict = jnp.where(lan < sub, 1.0, 0.0).astype(MXU_DTYPE)
    hi = jnp.floor(cnt_col * (1.0 / LANES))
    lo = cnt_col - hi * LANES
    offs_col = _dot(lstrict, hi.astype(MXU_DTYPE)) * LANES + _dot(lstrict, lo.astype(MXU_DTYPE))
    first_t = jnp.floor(offs_col * (1.0 / tmr))
    last_t = jnp.floor((offs_col + cnt_col - 1.0) * (1.0 / tmr))
    n_col = jnp.where(cnt_col > 0.0, last_t - first_t + 1.0, 0.0)
    base_col = _dot(lstrict, n_col.astype(MXU_DTYPE))
    end_col = base_col + n_col
    offs_scr[...] = offs_col.T[0:1, :]

    rep = lambda col: jnp.concatenate([col] * (iw // LANES), axis=1)
    wl = lax.broadcasted_iota(I32, (LANES, iw), 1).astype(F32)
    sub_f = lax.broadcasted_iota(I32, (LANES, iw), 0).astype(F32)
    end_r = rep(end_col)
    w_total = end_r[LANES - 1:LANES, :]
    count_le = lambda v: jnp.sum(jnp.where(end_r <= v, 1.0, 0.0), axis=0, keepdims=True)
    w_row = wl[0:1, :]
    eidx = count_le(w_row)
    sel = sub_f == eidx
    pick = lambda col: jnp.sum(jnp.where(sel, rep(col), 0.0), axis=0, keepdims=True)
    e_base, e_first, e_offs, e_cnt, e_end = (pick(base_col), pick(first_t), pick(offs_col),
                                             pick(cnt_col), pick(end_col))
    valid = w_row < w_total
    tile = e_first + (w_row - e_base)
    row_lo = jnp.maximum(e_offs, tile * tmr) - tile * tmr
    row_hi = jnp.minimum(e_offs + e_cnt, (tile + 1.0) * tmr) - tile * tmr
    first_e = w_row == e_base
    flags = jnp.where(valid, F_VALID + jnp.where(first_e, float(F_FIRST_OF_EXPERT), 0.0)
                      + jnp.where(row_lo == 0.0, float(F_FIRST_OF_TILE), 0.0), 0.0)
    has1 = valid & first_e & (e_end < w_total)
    e1 = count_le(e_end)
    end1 = jnp.sum(jnp.where(sub_f == e1, end_r, 0.0), axis=0, keepdims=True)
    has2 = has1 & (end1 < w_total)
    nxt = jnp.where(has1, e1, -1.0)
    nxt2 = jnp.where(has2, count_le(end1), -1.0)
    ord_col = _dot(lstrict, jnp.where(cnt_col > 0.0, 1.0, 0.0).astype(MXU_DTYPE))
    e_ord = pick(ord_col)
    slot = e_ord - 2.0 * jnp.floor(e_ord * 0.5)
    e_last = count_le(w_total - 1.0)
    rows = {IT_TILE: jnp.where(valid, tile, n_tiles - 1.0),
            IT_E: jnp.where(valid, eidx, e_last),
            IT_LO: jnp.where(valid, row_lo, 0.0),
            IT_HI: jnp.where(valid, row_hi, 0.0),
            IT_FLAG: flags,
            IT_NEXT: nxt,
            IT_NEXT2: nxt2,
            IT_SLOT: jnp.where(valid, slot, 0.0)}
    sub8 = lax.broadcasted_iota(I32, (8, iw), 0)
    out = jnp.zeros((8, iw), F32)
    for k, val in rows.items():
        out = jnp.where(sub8 == k, val, out)
    items_ref[...] = out.astype(I32)


def _route(rlog, n_items):
    t = rlog.shape[0]
    tm = min(TM_ROUTE, t)
    iw = ((n_items + LANES - 1) // LANES) * LANES
    kern = functools.partial(_route_kernel, n_tiles=2 * t // TM_ROWS)
    return pl.pallas_call(
        kern,
        grid=(2, t // tm),
        in_specs=[pl.BlockSpec((tm, LANES), lambda p, i: (i, 0))],
        out_specs=[pl.BlockSpec((tm, LANES), lambda p, i: (i * p, 0)),
                   pl.BlockSpec((8, tm), lambda p, i: (0, i * p)),
                   pl.BlockSpec((8, iw), lambda p, i: (0, 0))],
        out_shape=[jax.ShapeDtypeStruct((t, LANES), F32),
                   jax.ShapeDtypeStruct((8, t), I32),
                   jax.ShapeDtypeStruct((8, iw), I32)],
        scratch_shapes=[pltpu.VMEM((LANES, LANES), F32),
                        pltpu.VMEM((1, LANES), F32),
                        pltpu.VMEM((1, LANES), F32)],
        compiler_params=_cparams(("arbitrary", "arbitrary")),
        name="route",
    )(rlog)


TM_DISP = 1024


def _dispatch_kernel(pos0_ref, pos1_ref, h2_ref, xs_ref, sem):
    tm = h2_ref.shape[0]
    base = pl.program_id(0) * tm

    def row_copy(r, p):
        return pltpu.make_async_copy(h2_ref.at[pl.ds(r, 1), :], xs_ref.at[pl.ds(p, 1), :], sem)

    def issue(r, carry):
        row_copy(r, pos0_ref[base + r]).start(priority=0)
        row_copy(r, pos1_ref[base + r]).start(priority=1)
        return carry

    lax.fori_loop(0, tm, issue, 0, unroll=8)

    def drain(r, carry):
        row_copy(0, 0).wait()
        row_copy(0, 0).wait()
        return carry

    lax.fori_loop(0, tm, drain, 0, unroll=8)


def _dispatch(pos0, pos1, h2):
    t, d = h2.shape
    tm = min(TM_DISP, t)
    return pl.pallas_call(
        _dispatch_kernel,
        grid_spec=pltpu.PrefetchScalarGridSpec(
            num_scalar_prefetch=2,
            grid=(t // tm,),
            in_specs=[pl.BlockSpec((tm, d), lambda i, p0, p1: (i, 0))],
            out_specs=pl.BlockSpec(memory_space=pl.ANY),
            scratch_shapes=[pltpu.SemaphoreType.DMA(())]),
        out_shape=jax.ShapeDtypeStruct((2 * t, d), h2.dtype),
        compiler_params=pltpu.CompilerParams(dimension_semantics=("arbitrary",),
                                             vmem_limit_bytes=VMEM_LIMIT, has_side_effects=True),
        name="dispatch",
    )(pos0, pos1, h2)


WEIGHT_DMA_PRIORITY = 1


def _experts_kernel(tile_ref, e_ref, lo_ref, hi_ref, flag_ref, next_ref, next2_ref, slot_ref,
                    xs_ref, w1_hbm, w3_hbm, w2_hbm, ys_ref,
                    f1_scr, f3_scr, f2_scr, w1_scr, w3_scr, w2_scr, sems):
    del tile_ref
    w = pl.program_id(0)
    flag = flag_ref[w]

    mats = ((w1_hbm, f1_scr, w1_scr), (w3_hbm, f3_scr, w3_scr), (w2_hbm, f2_scr, w2_scr))

    def weight_copy(k, e, slot):
        return pltpu.make_async_copy(mats[k][0].at[e], mats[k][1].at[slot], sems.at[slot, k])

    @pl.when(w == 0)
    def _():
        for k in range(3):
            weight_copy(k, e_ref[0], 0).start(priority=WEIGHT_DMA_PRIORITY)
        nxt = next_ref[0]

        @pl.when(nxt >= 0)
        def _():
            for k in range(3):
                weight_copy(k, nxt, 1).start(priority=WEIGHT_DMA_PRIORITY)

    @pl.when((flag & F_FIRST_OF_EXPERT) != 0)
    def _():
        slot = slot_ref[w]
        nxt2 = next2_ref[w]
        for k in range(3):
            weight_copy(k, e_ref[w], slot).wait()
            mats[k][2][...] = mats[k][1][slot].astype(mats[k][2].dtype)

            @pl.when(nxt2 >= 0)
            def _():
                weight_copy(k, nxt2, slot).start(priority=WEIGHT_DMA_PRIORITY)

    @pl.when(((flag & F_VALID) != 0) & ((flag & F_FIRST_OF_TILE) != 0))
    def _():
        ys_ref[...] = jnp.zeros(ys_ref.shape, ys_ref.dtype)

    lo = lo_ref[w]
    hi = hi_ref[w]
    for part in range(TM_ROWS // MM_ROWS):
        r0 = part * MM_ROWS

        @pl.when(((flag & F_VALID) != 0) & (lo < r0 + MM_ROWS) & (hi > r0))
        def _():
            x = xs_ref[r0:r0 + MM_ROWS, :].astype(MXU_DTYPE)
            a = _dot(x, w1_scr[...])
            u = _dot(x, w3_scr[...])
            hm = (a * jax.nn.sigmoid(a)) * u
            res = _dot(hm.astype(MXU_DTYPE), w2_scr[...])
            row = r0 + lax.broadcasted_iota(I32, res.shape, 0)
            mine = (row >= lo) & (row < hi)
            ys_ref[r0:r0 + MM_ROWS, :] = jnp.where(mine, res, ys_ref[r0:r0 + MM_ROWS, :])


def _experts(items, xs, w1, w3, w2, n_items):
    n_rows, d = xs.shape
    f = w1.shape[2]
    tm = TM_ROWS
    tile_map = lambda w, tile, *_: (tile[w], 0)
    return pl.pallas_call(
        _experts_kernel,
        grid_spec=pltpu.PrefetchScalarGridSpec(
            num_scalar_prefetch=8,
            grid=(n_items,),
            in_specs=[pl.BlockSpec((tm, d), tile_map),
                      pl.BlockSpec(memory_space=pl.ANY),
                      pl.BlockSpec(memory_space=pl.ANY),
                      pl.BlockSpec(memory_space=pl.ANY)],
            out_specs=pl.BlockSpec((tm, d), tile_map),
            scratch_shapes=[pltpu.VMEM((2, d, f), F32),
                            pltpu.VMEM((2, d, f), F32),
                            pltpu.VMEM((2, f, d), F32),
                            pltpu.VMEM((d, f), MXU_DTYPE),
                            pltpu.VMEM((d, f), MXU_DTYPE),
                            pltpu.VMEM((f, d), MXU_DTYPE),
                            pltpu.SemaphoreType.DMA((2, 3))]),
        out_shape=jax.ShapeDtypeStruct((n_rows, d), F32),
        compiler_params=_cparams(("arbitrary",)),
        name="experts",
    )(items[IT_TILE], items[IT_E], items[IT_LO], items[IT_HI], items[IT_FLAG], items[IT_NEXT],
      items[IT_NEXT2], items[IT_SLOT], xs, w1, w3, w2)


TM_COMB = 256


def _combine_kernel(pos0_ref, pos1_ref, ys_ref, x1_ref, route_ref, mod_ref, o_ref, y0_scr, y1_scr, sems):
    step = pl.program_id(0)
    n_steps = pl.num_programs(0)

    def row_copy(p, dst, slot, r):
        return pltpu.make_async_copy(ys_ref.at[pl.ds(p, 1), :], dst.at[slot, pl.ds(r, 1), :], sems.at[slot])

    def issue_step(s):
        slot = s % 2
        base = s * TM_COMB

        def issue(r, carry):
            row_copy(pos0_ref[base + r], y0_scr, slot, r).start(priority=0)
            row_copy(pos1_ref[base + r], y1_scr, slot, r).start(priority=1)
            return carry

        lax.fori_loop(0, TM_COMB, issue, 0, unroll=8)

    @pl.when(step == 0)
    def _():
        issue_step(step)

    @pl.when(step + 1 < n_steps)
    def _():
        issue_step(step + 1)

    slot = step % 2

    def drain(r, carry):
        row_copy(0, y0_scr, slot, 0).wait()
        row_copy(0, y1_scr, slot, 0).wait()
        return carry

    lax.fori_loop(0, TM_COMB, drain, 0, unroll=8)

    rt = route_ref[...]
    lane = lax.broadcasted_iota(I32, rt.shape, 1)
    p0 = jnp.sum(jnp.where(lane == R_P0, rt, 0.0), axis=-1, keepdims=True)
    p1 = jnp.sum(jnp.where(lane == R_P1, rt, 0.0), axis=-1, keepdims=True)
    y = p0 * y0_scr[slot] + p1 * y1_scr[slot]
    o_ref[...] = x1_ref[...] + mod_ref[0, 5:6, :] * y


def _combine(pos0, pos1, ys, x1, route, mod3, seq):
    t, d = x1.shape
    tm = TM_COMB
    per_b = seq // tm
    return pl.pallas_call(
        _combine_kernel,
        grid_spec=pltpu.PrefetchScalarGridSpec(
            num_scalar_prefetch=2,
            grid=(t // tm,),
            in_specs=[pl.BlockSpec(memory_space=pl.ANY),
                      pl.BlockSpec((tm, d), lambda i, p0, p1: (i, 0)),
                      pl.BlockSpec((tm, LANES), lambda i, p0, p1: (i, 0)),
                      pl.BlockSpec((1, 6, d), lambda i, p0, p1: (i // per_b, 0, 0))],
            out_specs=pl.BlockSpec((tm, d), lambda i, p0, p1: (i, 0)),
            scratch_shapes=[pltpu.VMEM((2, tm, d), F32),
                            pltpu.VMEM((2, tm, d), F32),
                            pltpu.SemaphoreType.DMA((2,))]),
        out_shape=jax.ShapeDtypeStruct((t, d), F32),
        compiler_params=_cparams(("arbitrary",)),
        name="combine",
    )(pos0, pos1, ys, x1, route, mod3)


def kernel(x, c, w_ada, b_ada, g_mix, w_in, b_gate, b_forget, w_uk, w_uv, g_qa, g_kv, g_qb, g_kb,
           w_pa, w_pb, w_o, g_ffn, w_rg, b_rg, w_re, b_re, w1, w3, w2):
    b, s, d = x.shape
    depth = w_ada.shape[0]
    t = b * s
    n_heads_b = b_forget.shape[1]
    n_items = 2 * t // TM_ROWS + N_EXPERTS
    c8 = jnp.zeros((8, d), F32).at[:b].set(c)
    x2 = x.reshape(t, d)
    row = lambda v: v.reshape(1, -1)
    for l in range(depth):
        mod3 = _ada(c8, w_ada[l], row(b_ada[l]))[:b].reshape(b, 6, d)
        proj, misc = _inproj(x2, mod3, row(g_mix[l]), _pack_w_in(jnp.transpose(w_in[l])), s)
        proj3 = proj.reshape(b, s, NP_COLS)
        misc3 = misc.reshape(b, s, LANES)
        bf_row = jnp.zeros((1, LANES), F32).at[0, MISC_F:MISC_F + n_heads_b].set(b_forget[l])
        kv, cums = _prep(proj3, misc3, row(g_kv[l]), bf_row)
        o_a = _dsa(proj3, misc3, kv, w_uk[l].astype(MXU_DTYPE), w_uv[l].astype(MXU_DTYPE), row(g_qa[l]))
        o_b = _fox(proj3, cums, row(g_qb[l]), row(g_kb[l]), n_heads_b)

        w_r = jnp.zeros((d, LANES), F32).at[:, :N_GROUPS].set(w_rg[l])
        w_r = w_r.at[:, N_GROUPS:N_GROUPS + N_EXPERTS].set(w_re[l])
        wr_hi = w_r.astype(MXU_DTYPE)
        wr_pair = jnp.concatenate([wr_hi, (w_r - wr_hi.astype(F32)).astype(MXU_DTYPE)], axis=1)
        b_r = jnp.zeros((1, LANES), F32).at[0, :N_GROUPS].set(b_rg[l])
        b_r = b_r.at[0, N_GROUPS:N_GROUPS + N_EXPERTS].set(b_re[l])
        x1, h2, rlog = _merge(proj, o_a.reshape(t, -1), o_b.reshape(t, -1), x2, mod3, row(b_gate[l]),
                              w_pa[l].astype(MXU_DTYPE), w_pb[l].astype(MXU_DTYPE),
                              w_o[l].astype(MXU_DTYPE), row(g_ffn[l]), wr_hi, wr_pair, b_r, s)
        route, pos_t, items = _route(rlog, n_items)
        pos0, pos1 = pos_t[0], pos_t[1]
        xs = _dispatch(pos0, pos1, h2)
        ys = _experts(items, xs, w1[l], w3[l], w2[l], n_items)
        x2 = _combine(pos0, pos1, ys, x1, route, mod3, s)
    return x2.reshape(b, s, d)
```

```python
import functools

import jax
import jax.numpy as jnp
from jax import lax
from jax.experimental import pallas as pl
from jax.experimental.pallas import tpu as pltpu

F32 = jnp.float32
I32 = jnp.int32
MXU_DTYPE = jnp.bfloat16

CHUNK = 64
HEAD_DIM = 128
D_LAT = 256
N_IDX_HEADS = 16
D_IDX = 64
TOPK_MAX = 256
N_GROUPS = 8
EXPERTS_PER_GROUP = 8
N_EXPERTS = N_GROUPS * EXPERTS_PER_GROUP
RMS_EPS = 1e-6

LANES = 128
VMEM_LIMIT = 56 * 1024 * 1024

NEG_BIG = -1e30
INT_MIN = -2147483648


def _cparams(sem):
    return pltpu.CompilerParams(dimension_semantics=sem, vmem_limit_bytes=VMEM_LIMIT)


def _dot(a, b):
    return jnp.dot(a, b, preferred_element_type=F32)


def _dot_nt(a, b):
    return lax.dot_general(a, b, (((1,), (1,)), ((), ())), preferred_element_type=F32)


def _rms(x, g):
    return x * lax.rsqrt(jnp.mean(x * x, axis=-1, keepdims=True) + RMS_EPS) * g


def _ada_kernel(c_ref, w_ref, b_ref, o_ref):
    c = c_ref[...]
    a = c * jax.nn.sigmoid(c)
    o_ref[...] = _dot(a.astype(MXU_DTYPE), w_ref[...].astype(MXU_DTYPE)) + b_ref[...]


def _ada(c8, w_ada, b_ada):
    d, n = w_ada.shape
    tn = 2048
    return pl.pallas_call(
        _ada_kernel,
        grid=(n // tn,),
        in_specs=[pl.BlockSpec((8, d), lambda j: (0, 0)),
                  pl.BlockSpec((d, tn), lambda j: (0, j)),
                  pl.BlockSpec((1, tn), lambda j: (0, j))],
        out_specs=pl.BlockSpec((8, tn), lambda j: (0, j)),
        out_shape=jax.ShapeDtypeStruct((8, n), F32),
        compiler_params=_cparams(("arbitrary",)),
        name="ada",
    )(c8, w_ada, b_ada)


NP_COLS = 76 * LANES
U_QA, U_QIDX, U_QB, U_KB, U_VB, U_CKV, U_MISC = 32, 40, 48, 56, 64, 72, 74
TN_PROJ = 4 * LANES
TM_PROJ = 2048
TX_PROJ = 1024
MISC_TILE = (U_MISC * LANES) // TN_PROJ
MISC_OFF = U_MISC * LANES - MISC_TILE * TN_PROJ
MISC_K, MISC_F, MISC_W = 0, 64, 72


def _pack_moves():
    sizes = (1024, 256, 1024, 64, 16, 1024, 1024, 1024, 8, 4096)
    src = [0]
    for n in sizes:
        src.append(src[-1] + n)
    q_a, c_kv, q_idx, k_idx, w_idx, q_b, k_b, v_b, f_b, gate = src[:10]
    m = U_MISC * LANES
    return ((gate, 0, 4096), (q_a, U_QA * LANES, 1024), (q_idx, U_QIDX * LANES, 1024),
            (q_b, U_QB * LANES, 1024), (k_b, U_KB * LANES, 1024), (v_b, U_VB * LANES, 1024),
            (c_kv, U_CKV * LANES, 256), (f_b, m + MISC_F, 8), (w_idx, m + MISC_W, 16),
            (k_idx, m + MISC_K, 64))


def _pack_kernel(w_ref, o_ref):
    cols = o_ref.shape[1]
    m = U_MISC * LANES
    o_ref[m:m + 2 * LANES, :] = jnp.zeros((2 * LANES, cols), o_ref.dtype)
    for src, dst, n in _pack_moves():
        o_ref[dst:dst + n, :] = w_ref[src:src + n, :].astype(o_ref.dtype)


def _pack_w_in(w_in_t):
    n_in, d = w_in_t.shape
    tc = 256
    return pl.pallas_call(
        _pack_kernel,
        grid=(d // tc,),
        in_specs=[pl.BlockSpec((n_in, tc), lambda i: (0, i))],
        out_specs=pl.BlockSpec((NP_COLS, tc), lambda i: (0, i)),
        out_shape=jax.ShapeDtypeStruct((NP_COLS, d), MXU_DTYPE),
        compiler_params=_cparams(("parallel",)),
        name="pack",
    )(w_in_t)


def _inproj_kernel(x_ref, mod_ref, g_ref, w_ref, o_ref, misc_ref, h_scr, *, n_x):
    j = pl.program_id(1)
    tx = x_ref.shape[0]
    for part in range(n_x):
        @pl.when(j == part)
        def _():
            y = _rms(x_ref[...], g_ref[...])
            sh = mod_ref[0, 0:1, :]
            sc = mod_ref[0, 1:2, :]
            h_scr[part * tx:(part + 1) * tx, :] = (y * (1.0 + sc) + sh).astype(h_scr.dtype)

    @pl.when(j >= n_x)
    def _():
        acc = _dot_nt(h_scr[...], w_ref[...])
        o_ref[...] = acc.astype(o_ref.dtype)

        @pl.when(j == n_x + MISC_TILE)
        def _():
            misc_ref[...] = acc[:, MISC_OFF:MISC_OFF + LANES]


def _inproj(x2, mod3, g_mix, w_packed, seq):
    t, d = x2.shape
    tm = min(TM_PROJ, seq)
    tx = min(TX_PROJ, tm)
    n_x = tm // tx
    per_b = seq // tm
    wtile = lambda j: jnp.maximum(j - n_x, 0)
    return pl.pallas_call(
        functools.partial(_inproj_kernel, n_x=n_x),
        grid=(t // tm, n_x + NP_COLS // TN_PROJ),
        in_specs=[pl.BlockSpec((tx, d), lambda i, j: (i * n_x + jnp.minimum(j, n_x - 1), 0)),
                  pl.BlockSpec((1, 6, d), lambda i, j: (i // per_b, 0, 0)),
                  pl.BlockSpec((1, d), lambda i, j: (0, 0)),
                  pl.BlockSpec((TN_PROJ, d), lambda i, j: (wtile(j), 0))],
        out_specs=[pl.BlockSpec((tm, TN_PROJ), lambda i, j: (i, wtile(j))),
                   pl.BlockSpec((tm, LANES), lambda i, j: (i, 0))],
        out_shape=[jax.ShapeDtypeStruct((t, NP_COLS), MXU_DTYPE),
                   jax.ShapeDtypeStruct((t, LANES), F32)],
        scratch_shapes=[pltpu.VMEM((tm, d), MXU_DTYPE)],
        compiler_params=_cparams(("parallel", "arbitrary")),
        name="inproj",
    )(x2, mod3, g_mix, w_packed)


TK = 256


def _prep_kernel(ckv_ref, misc_ref, gkv_ref, bf_ref, kv_ref, cum_ref):
    s = ckv_ref.shape[1]
    kv_ref[0] = _rms(ckv_ref[0].astype(F32), gkv_ref[...]).astype(kv_ref.dtype)

    r = lax.broadcasted_iota(I32, (LANES, LANES), 0)
    c = lax.broadcasted_iota(I32, (LANES, LANES), 1)
    tri = jnp.where(c <= r, 1.0, 0.0).astype(MXU_DTYPE)
    carry = jnp.zeros((1, LANES), F32)
    for blk in range(s // LANES):
        z = misc_ref[0, blk * LANES:(blk + 1) * LANES, :] + bf_ref[...]
        ls = jnp.minimum(z, 0.0) - jnp.log1p(jnp.exp(-jnp.abs(z)))
        p1 = ls.astype(MXU_DTYPE)
        r1 = ls - p1.astype(F32)
        p2 = r1.astype(MXU_DTYPE)
        p3 = (r1 - p2.astype(F32)).astype(MXU_DTYPE)
        cs = _dot(tri, p1) + _dot(tri, p2) + _dot(tri, p3) + carry
        carry = cs[LANES - 1:LANES, :]
        cum_ref[0, blk * LANES:(blk + 1) * LANES, :] = cs * (HEAD_DIM ** 0.5)


def _prep(proj3, misc3, g_kv, bf_row):
    b, s, _ = proj3.shape
    return pl.pallas_call(
        _prep_kernel,
        grid=(b,),
        in_specs=[pl.BlockSpec((1, s, D_LAT), lambda i: (i, 0, U_CKV * LANES // D_LAT)),
                  pl.BlockSpec((1, s, LANES), lambda i: (i, 0, 0)),
                  pl.BlockSpec((1, D_LAT), lambda i: (0, 0)),
                  pl.BlockSpec((1, LANES), lambda i: (0, 0))],
        out_specs=[pl.BlockSpec((1, s, D_LAT), lambda i: (i, 0, 0)),
                   pl.BlockSpec((1, s, LANES), lambda i: (i, 0, 0))],
        out_shape=[jax.ShapeDtypeStruct((b, s, D_LAT), MXU_DTYPE),
                   jax.ShapeDtypeStruct((b, s, LANES), F32)],
        compiler_params=_cparams(("parallel",)),
        name="prep",
    )(proj3, misc3, g_kv, bf_row)


TQ_A = 256
N_BISECT = 32
LOG2E = 1.4426950408889634
assert TQ_A == TK


def _pair_loop(n, body):
    def pair(j, carry):
        body(2 * j, 2)
        return carry

    lax.fori_loop(0, jnp.right_shift(n, 1), pair, 0)

    @pl.when(jnp.bitwise_and(n, 1) == 1)
    def _():
        body(n - 1, 1)


def _tile_start(kt):
    return kt * TK if isinstance(kt, int) else pl.multiple_of(kt * TK, TK)


def _static_pairs(idx, n_cases, count, body):
    for c in range(n_cases):
        @pl.when(idx == c)
        def _():
            n = count(c)
            for kt in range(0, n, 2):
                body(kt, min(2, n - kt))


def _dsa_kernel(qa_ref, qidx_ref, miscq_ref, misck_ref, kv_ref, wuk_ref, wuv_ref, gqa_ref, o_ref,
                sc_scr, bias_scr, logit_scr, qlat_scr, qh_scr, mm_scr, thr_scr, mpart_scr, lpart_scr, oacc_scr,
                *, topk, n_heads):
    i = pl.program_id(1)
    tq = TQ_A
    nk = i + 1
    q0 = i * tq

    wt = miscq_ref[0].T
    wq = wt[MISC_W:MISC_W + N_IDX_HEADS, :] * (D_IDX ** -0.5 * N_IDX_HEADS ** -0.5)
    for h in range(N_IDX_HEADS):
        qh_scr[h] = qidx_ref[0, :, h * D_IDX:(h + 1) * D_IDX]

    def fold(v, op, rows):
        return op(v.reshape(v.shape[0] // rows, rows, tq), axis=0)

    mm_scr[0:8, :] = jnp.full((8, tq), NEG_BIG, F32)
    mm_scr[8:16, :] = jnp.full((8, tq), -NEG_BIG, F32)

    def score_tiles(kt, nt, last):
        mx8 = mm_scr[0:8, :]
        mn8 = mm_scr[8:16, :]
        for half in range(nt * (TK // LANES)):
            k0 = pl.multiple_of(kt * TK + half * LANES, LANES)
            kx = misck_ref[0, pl.ds(k0, LANES), :][:, MISC_K:MISC_K + D_IDX].astype(MXU_DTYPE)
            acc = jnp.zeros((LANES, tq), F32)
            for h in range(N_IDX_HEADS):
                d = _dot_nt(kx, qh_scr[h])
                acc = acc + jnp.maximum(d, 0.0) * wq[h:h + 1, :]
            if last:
                kpos = k0 + lax.broadcasted_iota(I32, (LANES, tq), 0)
                qpos = q0 + lax.broadcasted_iota(I32, (LANES, tq), 1)
                adm = (kpos // CHUNK) <= (qpos // CHUNK)
                sc_scr[pl.ds(k0, LANES), :] = jnp.where(adm, acc, NEG_BIG)
                mx8 = jnp.maximum(mx8, fold(jnp.where(adm, acc, NEG_BIG), jnp.max, 8))
                mn8 = jnp.minimum(mn8, fold(jnp.where(adm, acc, -NEG_BIG), jnp.min, 8))
            else:
                sc_scr[pl.ds(k0, LANES), :] = acc
                mx8 = jnp.maximum(mx8, fold(acc, jnp.max, 8))
                mn8 = jnp.minimum(mn8, fold(acc, jnp.min, 8))
        mm_scr[0:8, :] = mx8
        mm_scr[8:16, :] = mn8

    _pair_loop(i, lambda kt, nt: score_tiles(kt, nt, False))
    score_tiles(i, 1, True)

    def bisect(n_tiles):
        def bis_body(it, carry):
            lo, hi = carry
            mid = lo + 0.5 * (hi - lo)
            c32 = jnp.zeros((32, tq), F32)
            for kt in range(n_tiles):
                c32 = c32 + fold(jnp.where(sc_scr[kt * TK:(kt + 1) * TK, :] >= mid, 1.0, 0.0), jnp.sum, 32)
            ok = jnp.sum(c32, axis=0, keepdims=True) >= topk
            return jnp.where(ok, mid, lo), jnp.where(ok, hi, mid)

        lo, _ = lax.fori_loop(0, N_BISECT, bis_body, (jnp.min(mm_scr[8:16, :], axis=0, keepdims=True),
                                                      jnp.max(mm_scr[0:8, :], axis=0, keepdims=True)))
        thr_scr[...] = lo

    for c in range(sc_scr.shape[0] // TK):
        @pl.when(i == c)
        def _():
            bisect(c + 1)

    thr = thr_scr[...]

    def bias_body(kt, carry):
        k0 = _tile_start(kt)
        sel_t = sc_scr[pl.ds(k0, TK), :] >= thr
        bias_scr[kt] = jnp.where(sel_t, 0.0, NEG_BIG).T
        return carry

    lax.fori_loop(0, nk, bias_body, 0)

    for h in range(n_heads):
        ql = _dot(qa_ref[0, :, h * HEAD_DIM:(h + 1) * HEAD_DIM], wuk_ref[h])
        ql = _rms(ql, gqa_ref[...]) * (D_LAT ** -0.5 * LOG2E)
        qlat_scr[h * tq:(h + 1) * tq, :] = ql.astype(qlat_scr.dtype)
    mpart_scr[...] = jnp.full(mpart_scr.shape, NEG_BIG, F32)
    lpart_scr[...] = jnp.zeros(lpart_scr.shape, F32)
    oacc_scr[...] = jnp.zeros(oacc_scr.shape, F32)
    slopes2 = [2.0 ** (-8.0 * (h + 1) / n_heads) * LOG2E for h in range(n_heads)]

    def logit_tiles(kt, nt, last):
        k0 = _tile_start(kt)
        kvt = kv_ref[0, pl.ds(k0, nt * TK), :]
        kcol = (k0 + lax.broadcasted_iota(I32, (1, nt * TK), 1)).astype(F32)
        if last:
            ahead = jnp.maximum(lax.broadcasted_iota(I32, (tq, TK), 1)
                                - lax.broadcasted_iota(I32, (tq, TK), 0), 0).astype(F32)
        for h in range(n_heads):
            rows = slice(h * tq, (h + 1) * tq)
            lg = _dot_nt(qlat_scr[rows, :], kvt) + slopes2[h] * kcol
            for t in range(nt):
                lt = lg[:, t * TK:(t + 1) * TK] + bias_scr[kt + t]
                if last:
                    lt = lt - (2.0 * slopes2[h]) * ahead
                logit_scr[kt + t, rows, :] = lt
                mpart_scr[rows, :] = jnp.maximum(mpart_scr[rows, :],
                                                 jnp.maximum(lt[:, :LANES], lt[:, LANES:]))

    n_cases = sc_scr.shape[0] // TK
    _static_pairs(i, n_cases, lambda c: c, lambda kt, nt: logit_tiles(kt, nt, False))
    logit_tiles(i, 1, True)

    m = jnp.max(mpart_scr[...], axis=-1, keepdims=True)
    mpart_scr[...] = jnp.broadcast_to(m, mpart_scr.shape)

    def pv_tiles(kt, nt):
        k0 = _tile_start(kt)
        kvt = kv_ref[0, pl.ds(k0, nt * TK), :]
        for h in range(n_heads):
            rows = slice(h * tq, (h + 1) * tq)
            mb = mpart_scr[rows, :]
            mb2 = jnp.concatenate([mb, mb], axis=1)
            ps = [jnp.exp2(logit_scr[kt + t, rows, :] - mb2) for t in range(nt)]
            lsum = ps[0][:, :LANES] + ps[0][:, LANES:]
            for p in ps[1:]:
                lsum = lsum + p[:, :LANES] + p[:, LANES:]
            lpart_scr[rows, :] += lsum
            p_all = ps[0] if nt == 1 else jnp.concatenate(ps, axis=1)
            oacc_scr[rows, :] += _dot(p_all.astype(MXU_DTYPE), kvt)

    _static_pairs(i, n_cases, lambda c: c + 1, pv_tiles)

    for h in range(n_heads):
        rows = slice(h * tq, (h + 1) * tq)
        lsum = jnp.sum(lpart_scr[rows, :], axis=-1, keepdims=True)
        o_lat = oacc_scr[rows, :] / lsum
        o_ref[0, :, h * HEAD_DIM:(h + 1) * HEAD_DIM] = _dot(
            o_lat.astype(MXU_DTYPE), wuv_ref[h]).astype(o_ref.dtype)


def _dsa(proj3, misc3, kv, w_uk, w_uv, g_qa):
    b, s, _ = proj3.shape
    n_heads = w_uk.shape[0]
    width = n_heads * HEAD_DIM
    topk = min(TOPK_MAX, s // 4)
    tq = TQ_A
    nkt = s // TK
    kern = functools.partial(_dsa_kernel, topk=topk, n_heads=n_heads)
    return pl.pallas_call(
        kern,
        grid=(b, s // tq),
        in_specs=[pl.BlockSpec((1, tq, width), lambda bb, i: (bb, i, U_QA * LANES // width)),
                  pl.BlockSpec((1, tq, N_IDX_HEADS * D_IDX),
                               lambda bb, i: (bb, i, U_QIDX * LANES // (N_IDX_HEADS * D_IDX))),
                  pl.BlockSpec((1, tq, LANES), lambda bb, i: (bb, i, 0)),
                  pl.BlockSpec((1, s, LANES), lambda bb, i: (bb, 0, 0)),
                  pl.BlockSpec((1, s, D_LAT), lambda bb, i: (bb, 0, 0)),
                  pl.BlockSpec((n_heads, HEAD_DIM, D_LAT), lambda bb, i: (0, 0, 0)),
                  pl.BlockSpec((n_heads, D_LAT, HEAD_DIM), lambda bb, i: (0, 0, 0)),
                  pl.BlockSpec((1, D_LAT), lambda bb, i: (0, 0))],
        out_specs=pl.BlockSpec((1, tq, width), lambda bb, i: (bb, i, 0)),
        out_shape=jax.ShapeDtypeStruct((b, s, width), MXU_DTYPE),
        scratch_shapes=[pltpu.VMEM((s, tq), F32),
                        pltpu.VMEM((nkt, tq, TK), F32),
                        pltpu.VMEM((nkt, n_heads * tq, TK), F32),
                        pltpu.VMEM((n_heads * tq, D_LAT), MXU_DTYPE),
                        pltpu.VMEM((N_IDX_HEADS, tq, D_IDX), MXU_DTYPE),
                        pltpu.VMEM((16, tq), F32),
                        pltpu.VMEM((1, tq), F32),
                        pltpu.VMEM((n_heads * tq, LANES), F32),
                        pltpu.VMEM((n_heads * tq, LANES), F32),
                        pltpu.VMEM((n_heads * tq, D_LAT), F32)],
        compiler_params=_cparams(("parallel", "arbitrary")),
        name="dsa",
    )(proj3, proj3, misc3, misc3, kv, w_uk, w_uv, g_qa)


TQ_B = 256


HG_B = 8
assert TQ_B == TK


def _split3(x):
    p1 = x.astype(MXU_DTYPE)
    r1 = x - p1.astype(F32)
    p2 = r1.astype(MXU_DTYPE)
    p3 = (r1 - p2.astype(F32)).astype(MXU_DTYPE)
    return p1, p2, p3


def _fox_kernel(q_ref, k_ref, v_ref, cumq_ref, cumk_ref, gq_ref, gk_ref, o_ref,
                kaug_scr, qaug_scr, logit_scr, mpart_scr, lpart_scr, oacc_scr):
    g = pl.program_id(1)
    qi = pl.program_id(2)
    tq = TQ_B
    s = k_ref.shape[1]
    cscale = (HEAD_DIM ** -0.5) * LOG2E

    rr = lax.broadcasted_iota(I32, (3 * LANES, LANES), 0)
    cc = lax.broadcasted_iota(I32, (3 * LANES, LANES), 1)
    lane_row = lax.broadcasted_iota(I32, (1, LANES), 1)
    ones_q = jnp.where((lane_row >= 3) & (lane_row < 6), 1.0, 0.0)
    ones_k = jnp.where(lane_row < 3, 1.0, 0.0)
    base_q = jnp.where(cc < 3, rr - LANES * cc, -1)
    base_k = jnp.where((cc >= 3) & (cc < 6), rr - LANES * (cc - 3), -1)

    def aug(pieces, h, base, sign, ones_row):
        e = jnp.where(base == h, sign, 0.0).astype(MXU_DTYPE)
        return (_dot(pieces, e) + ones_row).astype(MXU_DTYPE)

    @pl.when(qi == 0)
    def _():
        def kbody(c, carry):
            r0 = pl.multiple_of(c * TK, TK)
            ck = jnp.concatenate(_split3(cumk_ref[0, pl.ds(r0, TK), :]), axis=1)
            for hh in range(HG_B):
                kn = _rms(k_ref[0, pl.ds(r0, TK), hh * HEAD_DIM:(hh + 1) * HEAD_DIM].astype(F32), gk_ref[...])
                kaug_scr[hh, pl.ds(r0, TK), 0:HEAD_DIM] = kn.astype(MXU_DTYPE)
                kaug_scr[hh, pl.ds(r0, TK), HEAD_DIM:] = aug(ck, MISC_F + g * HG_B + hh, base_k, -1.0, ones_k)
            return carry

        lax.fori_loop(0, s // TK, kbody, 0)

    cq = jnp.concatenate(_split3(cumq_ref[0]), axis=1)
    for hh in range(HG_B):
        qn = _rms(q_ref[0, :, hh * HEAD_DIM:(hh + 1) * HEAD_DIM].astype(F32), gq_ref[...])
        qaug_scr[hh, :, 0:HEAD_DIM] = qn.astype(MXU_DTYPE)
        qaug_scr[hh, :, HEAD_DIM:] = aug(cq, MISC_F + g * HG_B + hh, base_q, 1.0, ones_q)
    mpart_scr[...] = jnp.full(mpart_scr.shape, NEG_BIG, F32)
    lpart_scr[...] = jnp.zeros(lpart_scr.shape, F32)
    oacc_scr[...] = jnp.zeros(oacc_scr.shape, F32)

    def put_logits(kt, hh, lg):
        logit_scr[kt, hh] = lg
        mpart_scr[hh] = jnp.maximum(mpart_scr[hh], jnp.maximum(lg[:, :LANES], lg[:, LANES:]))

    def raw_logits(kt, nt, hh):
        k0 = _tile_start(kt)
        return _dot_nt(qaug_scr[hh], kaug_scr[hh, pl.ds(k0, nt * TK), :]) * cscale

    def off_tiles(kt, nt):
        for hh in range(HG_B):
            lg = raw_logits(kt, nt, hh)
            for t in range(nt):
                put_logits(kt + t, hh, lg[:, t * TK:(t + 1) * TK])

    _static_pairs(qi, s // TK, lambda c: c, off_tiles)
    causal = (lax.broadcasted_iota(I32, (tq, TK), 1) <= lax.broadcasted_iota(I32, (tq, TK), 0))
    for hh in range(HG_B):
        put_logits(qi, hh, jnp.where(causal, raw_logits(qi, 1, hh), NEG_BIG))

    for hh in range(HG_B):
        m = jnp.max(mpart_scr[hh], axis=-1, keepdims=True)
        mpart_scr[hh] = jnp.broadcast_to(m, (tq, LANES))

    def pv_tiles(kt, nt):
        k0 = _tile_start(kt)
        for hh in range(HG_B):
            mb = mpart_scr[hh]
            mb2 = jnp.concatenate([mb, mb], axis=1)
            ps = [jnp.exp2(logit_scr[kt + t, hh] - mb2) for t in range(nt)]
            lsum = ps[0][:, :LANES] + ps[0][:, LANES:]
            for p in ps[1:]:
                lsum = lsum + p[:, :LANES] + p[:, LANES:]
            lpart_scr[hh] += lsum
            p_all = ps[0] if nt == 1 else jnp.concatenate(ps, axis=1)
            oacc_scr[hh] += _dot(p_all.astype(MXU_DTYPE),
                                 v_ref[0, pl.ds(k0, nt * TK), hh * HEAD_DIM:(hh + 1) * HEAD_DIM])

    _static_pairs(qi, s // TK, lambda c: c + 1, pv_tiles)
    for hh in range(HG_B):
        lsum = jnp.sum(lpart_scr[hh], axis=-1, keepdims=True)
        o_ref[0, :, hh * HEAD_DIM:(hh + 1) * HEAD_DIM] = (oacc_scr[hh] / lsum).astype(o_ref.dtype)


def _fox(proj3, cums, g_qb, g_kb, n_heads):
    b, s, _ = proj3.shape
    tq = TQ_B
    gw = HG_B * HEAD_DIM
    return pl.pallas_call(
        _fox_kernel,
        grid=(b, n_heads // HG_B, s // tq),
        in_specs=[pl.BlockSpec((1, tq, gw), lambda bb, g, i: (bb, i, U_QB // HG_B + g)),
                  pl.BlockSpec((1, s, gw), lambda bb, g, i: (bb, 0, U_KB // HG_B + g)),
                  pl.BlockSpec((1, s, gw), lambda bb, g, i: (bb, 0, U_VB // HG_B + g)),
                  pl.BlockSpec((1, tq, LANES), lambda bb, g, i: (bb, i, 0)),
                  pl.BlockSpec((1, s, LANES), lambda bb, g, i: (bb, 0, 0)),
                  pl.BlockSpec((1, HEAD_DIM), lambda bb, g, i: (0, 0)),
                  pl.BlockSpec((1, HEAD_DIM), lambda bb, g, i: (0, 0))],
        out_specs=pl.BlockSpec((1, tq, gw), lambda bb, g, i: (bb, i, g)),
        out_shape=jax.ShapeDtypeStruct((b, s, n_heads * HEAD_DIM), MXU_DTYPE),
        scratch_shapes=[pltpu.VMEM((HG_B, s, 2 * HEAD_DIM), MXU_DTYPE),
                        pltpu.VMEM((HG_B, tq, 2 * HEAD_DIM), MXU_DTYPE),
                        pltpu.VMEM((s // TK, HG_B, tq, TK), F32),
                        pltpu.VMEM((HG_B, tq, LANES), F32),
                        pltpu.VMEM((HG_B, tq, LANES), F32),
                        pltpu.VMEM((HG_B, tq, HEAD_DIM), F32)],
        compiler_params=_cparams(("parallel", "parallel", "arbitrary")),
        name="fox",
    )(proj3, proj3, proj3, cums, cums, g_qb, g_kb)


TM_MERGE = 256


def _merge_kernel(ga_ref, gb_ref, oa_ref, ob_ref, x_ref, mod_ref, bga_ref, bgb_ref, wpa_ref, wpb_ref,
                  wo_ref, gffn_ref, wrh_ref, wrp_ref, br_ref, x1_ref, h2_ref, rl_ref):
    ga = jax.nn.sigmoid(ga_ref[...].astype(F32) + bga_ref[...])
    gb = jax.nn.sigmoid(gb_ref[...].astype(F32) + bgb_ref[...])
    merged = ga * _dot(oa_ref[...], wpa_ref[...]) + gb * _dot(ob_ref[...], wpb_ref[...])
    upd = _dot(merged.astype(MXU_DTYPE), wo_ref[...])
    x1 = x_ref[...] + mod_ref[0, 2:3, :] * upd
    x1_ref[...] = x1
    h2 = _rms(x1, gffn_ref[...]) * (1.0 + mod_ref[0, 4:5, :]) + mod_ref[0, 3:4, :]
    h2_ref[...] = h2
    hh = h2.astype(MXU_DTYPE)
    hl = (h2 - hh.astype(F32)).astype(MXU_DTYPE)
    both = _dot(hh, wrp_ref[...])
    rl_ref[...] = both[:, :LANES] + both[:, LANES:] + _dot(hl, wrh_ref[...]) + br_ref[...]


def _merge(proj, o_a, o_b, x2, mod3, b_gate, w_pa, w_pb, w_o, g_ffn, wr_hi, wr_pair, b_r, seq):
    t, d = x2.shape
    tm = TM_MERGE
    per_b = seq // tm
    wa = o_a.shape[1]
    res = lambda shape: pl.BlockSpec(shape, lambda i: (0,) * len(shape), pipeline_mode=pl.Buffered(1))
    return pl.pallas_call(
        _merge_kernel,
        grid=(t // tm,),
        in_specs=[pl.BlockSpec((tm, d), lambda i: (i, 0)),
                  pl.BlockSpec((tm, d), lambda i: (i, 1)),
                  pl.BlockSpec((tm, wa), lambda i: (i, 0)),
                  pl.BlockSpec((tm, wa), lambda i: (i, 0)),
                  pl.BlockSpec((tm, d), lambda i: (i, 0)),
                  pl.BlockSpec((1, 6, d), lambda i: (i // per_b, 0, 0)),
                  pl.BlockSpec((1, d), lambda i: (0, 0)),
                  pl.BlockSpec((1, d), lambda i: (0, 1)),
                  res((wa, d)), res((wa, d)), res((d, d)),
                  pl.BlockSpec((1, d), lambda i: (0, 0)),
                  res((d, LANES)), res((d, 2 * LANES)),
                  pl.BlockSpec((1, LANES), lambda i: (0, 0))],
        out_specs=[pl.BlockSpec((tm, d), lambda i: (i, 0)),
                   pl.BlockSpec((tm, d), lambda i: (i, 0)),
                   pl.BlockSpec((tm, LANES), lambda i: (i, 0))],
        out_shape=[jax.ShapeDtypeStruct((t, d), F32),
                   jax.ShapeDtypeStruct((t, d), F32),
                   jax.ShapeDtypeStruct((t, LANES), F32)],
        compiler_params=_cparams(("parallel",)),
        name="merge",
    )(proj, proj, o_a, o_b, x2, mod3, b_gate, b_gate, w_pa, w_pb, w_o, g_ffn, wr_hi, wr_pair, b_r)


TM_ROUTE = 1024
TM_ROWS = 256
MM_ROWS = 128
R_E0, R_E1, R_P0, R_P1 = 0, 1, 4, 5
IT_TILE, IT_E, IT_LO, IT_HI, IT_FLAG, IT_NEXT, IT_NEXT2, IT_SLOT = range(8)
F_VALID, F_FIRST_OF_EXPERT, F_FIRST_OF_TILE = 1, 2, 4


def _route_kernel(rl_ref, route_ref, post_ref, items_ref, cnt_scr, run_scr, offs_scr, *, n_tiles):
    tm = rl_ref.shape[0]
    sweep = pl.program_id(0)
    step = pl.program_id(1)
    iw = items_ref.shape[1]

    @pl.when((sweep == 0) & (step == 0))
    def _():
        cnt_scr[...] = jnp.zeros(cnt_scr.shape, F32)

    r = rl_ref[...]
    lane = lax.broadcasted_iota(I32, (tm, LANES), 1).astype(F32)
    neg_inf = -jnp.inf
    gmask = lane < N_GROUPS
    gl = jnp.where(gmask, r, neg_inf)
    gmax = jnp.max(gl, axis=-1, keepdims=True)
    gidx = jnp.min(jnp.where(gl == gmax, lane, float(LANES)), axis=-1, keepdims=True)
    gsum = jnp.sum(jnp.where(gmask, jnp.exp(r - gmax), 0.0), axis=-1, keepdims=True)
    gw = 1.0 / gsum
    lo = N_GROUPS + EXPERTS_PER_GROUP * gidx
    emask = (lane >= lo) & (lane < lo + EXPERTS_PER_GROUP)
    el = jnp.where(emask, r, neg_inf)
    v0 = jnp.max(el, axis=-1, keepdims=True)
    i0 = jnp.min(jnp.where(el == v0, lane, float(LANES)), axis=-1, keepdims=True)
    el2 = jnp.where(lane == i0, neg_inf, el)
    v1 = jnp.max(el2, axis=-1, keepdims=True)
    i1 = jnp.min(jnp.where(el2 == v1, lane, float(LANES)), axis=-1, keepdims=True)
    tt = jnp.exp(v1 - v0)
    p0 = gw / (1.0 + tt)
    p1 = gw * tt / (1.0 + tt)
    e0 = i0 - N_GROUPS
    e1 = i1 - N_GROUPS

    hit0 = lane == e0
    hit1 = lane == e1
    oh = jnp.where(hit0 | hit1, 1.0, 0.0)

    @pl.when(sweep == 0)
    def _():
        ones = jnp.ones((tm, LANES), MXU_DTYPE)
        cnt_scr[...] += _dot(oh.T.astype(MXU_DTYPE), ones)

    @pl.when((sweep == 1) & (step == 0))
    def _():
        _plan_items(cnt_scr[...], items_ref, offs_scr, n_tiles, iw)
        run_scr[...] = jnp.zeros(run_scr.shape, F32)

    @pl.when(sweep == 1)
    def _():
        rr = lax.broadcasted_iota(I32, (tm, tm), 0)
        cc = lax.broadcasted_iota(I32, (tm, tm), 1)
        ltri = jnp.where(cc < rr, 1.0, 0.0).astype(MXU_DTYPE)
        before = _dot(ltri, oh.astype(MXU_DTYPE)) + run_scr[...] + offs_scr[...]
        pos0 = jnp.sum(jnp.where(hit0, before, 0.0), axis=-1, keepdims=True)
        pos1 = jnp.sum(jnp.where(hit1, before, 0.0), axis=-1, keepdims=True)
        run_scr[...] = run_scr[...] + jnp.sum(oh, axis=0, keepdims=True)
        out = jnp.zeros((tm, LANES), F32)
        for k, val in ((R_E0, e0), (R_E1, e1), (R_P0, p0), (R_P1, p1)):
            out = jnp.where(lane == k, val, out)
        route_ref[...] = out
        pmat = jnp.where(lane == 0.0, pos0, jnp.where(lane == 1.0, pos1, 0.0))
        post_ref[...] = pmat.T[0:8, :].astype(I32)


def _plan_items(cnt_col, items_ref, offs_scr, n_tiles, iw):
    tmr = float(TM_ROWS)
    sub = lax.broadcasted_iota(I32, (LANES, LANES), 0)
    lan = lax.broadcasted_iota(I32, (LANES, LANES), 1)
    lstrict = jnp.where(lan < sub, 1.0, 0.0).astype(MXU_DTYPE)
    hi = jnp.floor(cnt_col * (1.0 / LANES))
    lo = cnt_col - hi * LANES
    offs_col = _dot(lstrict, hi.astype(MXU_DTYPE)) * LANES + _dot(lstrict, lo.astype(MXU_DTYPE))
    first_t = jnp.floor(offs_col * (1.0 / tmr))
    last_t = jnp.floor((offs_col + cnt_col - 1.0) * (1.0 / tmr))
    n_col = jnp.where(cnt_col > 0.0, last_t - first_t + 1.0, 0.0)
    base_col = _dot(lstrict, n_col.astype(MXU_DTYPE))
    end_col = base_col + n_col
    offs_scr[...] = offs_col.T[0:1, :]

    rep = lambda col: jnp.concatenate([col] * (iw // LANES), axis=1)
    wl = lax.broadcasted_iota(I32, (LANES, iw), 1).astype(F32)
    sub_f = lax.broadcasted_iota(I32, (LANES, iw), 0).astype(F32)
    end_r = rep(end_col)
    w_total = end_r[LANES - 1:LANES, :]
    count_le = lambda v: jnp.sum(jnp.where(end_r <= v, 1.0, 0.0), axis=0, keepdims=True)
    w_row = wl[0:1, :]
    eidx = count_le(w_row)
    sel = sub_f == eidx
    pick = lambda col: jnp.sum(jnp.where(sel, rep(col), 0.0), axis=0, keepdims=True)
    e_base, e_first, e_offs, e_cnt, e_end = (pick(base_col), pick(first_t), pick(offs_col),
                                             pick(cnt_col), pick(end_col))
    valid = w_row < w_total
    tile = e_first + (w_row - e_base)
    row_lo = jnp.maximum(e_offs, tile * tmr) - tile * tmr
    row_hi = jnp.minimum(e_offs + e_cnt, (tile + 1.0) * tmr) - tile * tmr
    first_e = w_row == e_base
    flags = jnp.where(valid, F_VALID + jnp.where(first_e, float(F_FIRST_OF_EXPERT), 0.0)
                      + jnp.where(row_lo == 0.0, float(F_FIRST_OF_TILE), 0.0), 0.0)
    has1 = valid & first_e & (e_end < w_total)
    e1 = count_le(e_end)
    end1 = jnp.sum(jnp.where(sub_f == e1, end_r, 0.0), axis=0, keepdims=True)
    has2 = has1 & (end1 < w_total)
    nxt = jnp.where(has1, e1, -1.0)
    nxt2 = jnp.where(has2, count_le(end1), -1.0)
    ord_col = _dot(lstrict, jnp.where(cnt_col > 0.0, 1.0, 0.0).astype(MXU_DTYPE))
    e_ord = pick(ord_col)
    slot = e_ord - 2.0 * jnp.floor(e_ord * 0.5)
    e_last = count_le(w_total - 1.0)
    rows = {IT_TILE: jnp.where(valid, tile, n_tiles - 1.0),
            IT_E: jnp.where(valid, eidx, e_last),
            IT_LO: jnp.where(valid, row_lo, 0.0),
            IT_HI: jnp.where(valid, row_hi, 0.0),
            IT_FLAG: flags,
            IT_NEXT: nxt,
            IT_NEXT2: nxt2,
            IT_SLOT: jnp.where(valid, slot, 0.0)}
    sub8 = lax.broadcasted_iota(I32, (8, iw), 0)
    out = jnp.zeros((8, iw), F32)
    for k, val in rows.items():
        out = jnp.where(sub8 == k, val, out)
    items_ref[...] = out.astype(I32)


def _route(rlog, n_items):
    t = rlog.shape[0]
    tm = min(TM_ROUTE, t)
    iw = ((n_items + LANES - 1) // LANES) * LANES
    kern = functools.partial(_route_kernel, n_tiles=2 * t // TM_ROWS)
    return pl.pallas_call(
        kern,
        grid=(2, t // tm),
        in_specs=[pl.BlockSpec((tm, LANES), lambda p, i: (i, 0))],
        out_specs=[pl.BlockSpec((tm, LANES), lambda p, i: (i * p, 0)),
                   pl.BlockSpec((8, tm), lambda p, i: (0, i * p)),
                   pl.BlockSpec((8, iw), lambda p, i: (0, 0))],
        out_shape=[jax.ShapeDtypeStruct((t, LANES), F32),
                   jax.ShapeDtypeStruct((8, t), I32),
                   jax.ShapeDtypeStruct((8, iw), I32)],
        scratch_shapes=[pltpu.VMEM((LANES, LANES), F32),
                        pltpu.VMEM((1, LANES), F32),
                        pltpu.VMEM((1, LANES), F32)],
        compiler_params=_cparams(("arbitrary", "arbitrary")),
        name="route",
    )(rlog)


TM_DISP = 1024


def _dispatch_kernel(pos0_ref, pos1_ref, h2_ref, xs_ref, sem):
    tm = h2_ref.shape[0]
    base = pl.program_id(0) * tm

    def row_copy(r, p):
        return pltpu.make_async_copy(h2_ref.at[pl.ds(r, 1), :], xs_ref.at[pl.ds(p, 1), :], sem)

    def issue(r, carry):
        row_copy(r, pos0_ref[base + r]).start(priority=0)
        row_copy(r, pos1_ref[base + r]).start(priority=1)
        return carry

    lax.fori_loop(0, tm, issue, 0, unroll=8)

    def drain(r, carry):
        row_copy(0, 0).wait()
        row_copy(0, 0).wait()
        return carry

    lax.fori_loop(0, tm, drain, 0, unroll=8)


def _dispatch(pos0, pos1, h2):
    t, d = h2.shape
    tm = min(TM_DISP, t)
    return pl.pallas_call(
        _dispatch_kernel,
        grid_spec=pltpu.PrefetchScalarGridSpec(
            num_scalar_prefetch=2,
            grid=(t // tm,),
            in_specs=[pl.BlockSpec((tm, d), lambda i, p0, p1: (i, 0))],
            out_specs=pl.BlockSpec(memory_space=pl.ANY),
            scratch_shapes=[pltpu.SemaphoreType.DMA(())]),
        out_shape=jax.ShapeDtypeStruct((2 * t, d), h2.dtype),
        compiler_params=pltpu.CompilerParams(dimension_semantics=("arbitrary",),
                                             vmem_limit_bytes=VMEM_LIMIT, has_side_effects=True),
        name="dispatch",
    )(pos0, pos1, h2)


WEIGHT_DMA_PRIORITY = 1


def _experts_kernel(tile_ref, e_ref, lo_ref, hi_ref, flag_ref, next_ref, next2_ref, slot_ref,
                    xs_ref, w1_hbm, w3_hbm, w2_hbm, ys_ref,
                    f1_scr, f3_scr, f2_scr, w1_scr, w3_scr, w2_scr, sems):
    del tile_ref
    w = pl.program_id(0)
    flag = flag_ref[w]

    mats = ((w1_hbm, f1_scr, w1_scr), (w3_hbm, f3_scr, w3_scr), (w2_hbm, f2_scr, w2_scr))

    def weight_copy(k, e, slot):
        return pltpu.make_async_copy(mats[k][0].at[e], mats[k][1].at[slot], sems.at[slot, k])

    @pl.when(w == 0)
    def _():
        for k in range(3):
            weight_copy(k, e_ref[0], 0).start(priority=WEIGHT_DMA_PRIORITY)
        nxt = next_ref[0]

        @pl.when(nxt >= 0)
        def _():
            for k in range(3):
                weight_copy(k, nxt, 1).start(priority=WEIGHT_DMA_PRIORITY)

    @pl.when((flag & F_FIRST_OF_EXPERT) != 0)
    def _():
        slot = slot_ref[w]
        nxt2 = next2_ref[w]
        for k in range(3):
            weight_copy(k, e_ref[w], slot).wait()
            mats[k][2][...] = mats[k][1][slot].astype(mats[k][2].dtype)

            @pl.when(nxt2 >= 0)
            def _():
                weight_copy(k, nxt2, slot).start(priority=WEIGHT_DMA_PRIORITY)

    @pl.when(((flag & F_VALID) != 0) & ((flag & F_FIRST_OF_TILE) != 0))
    def _():
        ys_ref[...] = jnp.zeros(ys_ref.shape, ys_ref.dtype)

    lo = lo_ref[w]
    hi = hi_ref[w]
    for part in range(TM_ROWS // MM_ROWS):
        r0 = part * MM_ROWS

        @pl.when(((flag & F_VALID) != 0) & (lo < r0 + MM_ROWS) & (hi > r0))
        def _():
            x = xs_ref[r0:r0 + MM_ROWS, :].astype(MXU_DTYPE)
            a = _dot(x, w1_scr[...])
            u = _dot(x, w3_scr[...])
            hm = (a * jax.nn.sigmoid(a)) * u
            res = _dot(hm.astype(MXU_DTYPE), w2_scr[...])
            row = r0 + lax.broadcasted_iota(I32, res.shape, 0)
            mine = (row >= lo) & (row < hi)
            ys_ref[r0:r0 + MM_ROWS, :] = jnp.where(mine, res, ys_ref[r0:r0 + MM_ROWS, :])


def _experts(items, xs, w1, w3, w2, n_items):
    n_rows, d = xs.shape
    f = w1.shape[2]
    tm = TM_ROWS
    tile_map = lambda w, tile, *_: (tile[w], 0)
    return pl.pallas_call(
        _experts_kernel,
        grid_spec=pltpu.PrefetchScalarGridSpec(
            num_scalar_prefetch=8,
            grid=(n_items,),
            in_specs=[pl.BlockSpec((tm, d), tile_map),
                      pl.BlockSpec(memory_space=pl.ANY),
                      pl.BlockSpec(memory_space=pl.ANY),
                      pl.BlockSpec(memory_space=pl.ANY)],
            out_specs=pl.BlockSpec((tm, d), tile_map),
            scratch_shapes=[pltpu.VMEM((2, d, f), F32),
                            pltpu.VMEM((2, d, f), F32),
                            pltpu.VMEM((2, f, d), F32),
                            pltpu.VMEM((d, f), MXU_DTYPE),
                            pltpu.VMEM((d, f), MXU_DTYPE),
                            pltpu.VMEM((f, d), MXU_DTYPE),
                            pltpu.SemaphoreType.DMA((2, 3))]),
        out_shape=jax.ShapeDtypeStruct((n_rows, d), F32),
        compiler_params=_cparams(("arbitrary",)),
        name="experts",
    )(items[IT_TILE], items[IT_E], items[IT_LO], items[IT_HI], items[IT_FLAG], items[IT_NEXT],
      items[IT_NEXT2], items[IT_SLOT], xs, w1, w3, w2)


TM_COMB = 256


def _combine_kernel(pos0_ref, pos1_ref, ys_ref, x1_ref, route_ref, mod_ref, o_ref, y0_scr, y1_scr, sems):
    step = pl.program_id(0)
    n_steps = pl.num_programs(0)

    def row_copy(p, dst, slot, r):
        return pltpu.make_async_copy(ys_ref.at[pl.ds(p, 1), :], dst.at[slot, pl.ds(r, 1), :], sems.at[slot])

    def issue_step(s):
        slot = s % 2
        base = s * TM_COMB

        def issue(r, carry):
            row_copy(pos0_ref[base + r], y0_scr, slot, r).start(priority=0)
            row_copy(pos1_ref[base + r], y1_scr, slot, r).start(priority=1)
            return carry

        lax.fori_loop(0, TM_COMB, issue, 0, unroll=8)

    @pl.when(step == 0)
    def _():
        issue_step(step)

    @pl.when(step + 1 < n_steps)
    def _():
        issue_step(step + 1)

    slot = step % 2

    def drain(r, carry):
        row_copy(0, y0_scr, slot, 0).wait()
        row_copy(0, y1_scr, slot, 0).wait()
        return carry

    lax.fori_loop(0, TM_COMB, drain, 0, unroll=8)

    rt = route_ref[...]
    lane = lax.broadcasted_iota(I32, rt.shape, 1)
    p0 = jnp.sum(jnp.where(lane == R_P0, rt, 0.0), axis=-1, keepdims=True)
    p1 = jnp.sum(jnp.where(lane == R_P1, rt, 0.0), axis=-1, keepdims=True)
    y = p0 * y0_scr[slot] + p1 * y1_scr[slot]
    o_ref[...] = x1_ref[...] + mod_ref[0, 5:6, :] * y


def _combine(pos0, pos1, ys, x1, route, mod3, seq):
    t, d = x1.shape
    tm = TM_COMB
    per_b = seq // tm
    return pl.pallas_call(
        _combine_kernel,
        grid_spec=pltpu.PrefetchScalarGridSpec(
            num_scalar_prefetch=2,
            grid=(t // tm,),
            in_specs=[pl.BlockSpec(memory_space=pl.ANY),
                      pl.BlockSpec((tm, d), lambda i, p0, p1: (i, 0)),
                      pl.BlockSpec((tm, LANES), lambda i, p0, p1: (i, 0)),
                      pl.BlockSpec((1, 6, d), lambda i, p0, p1: (i // per_b, 0, 0))],
            out_specs=pl.BlockSpec((tm, d), lambda i, p0, p1: (i, 0)),
            scratch_shapes=[pltpu.VMEM((2, tm, d), F32),
                            pltpu.VMEM((2, tm, d), F32),
                            pltpu.SemaphoreType.DMA((2,))]),
        out_shape=jax.ShapeDtypeStruct((t, d), F32),
        compiler_params=_cparams(("arbitrary",)),
        name="combine",
    )(pos0, pos1, ys, x1, route, mod3)


def kernel(x, c, w_ada, b_ada, g_mix, w_in, b_gate, b_forget, w_uk, w_uv, g_qa, g_kv, g_qb, g_kb,
           w_pa, w_pb, w_o, g_ffn, w_rg, b_rg, w_re, b_re, w1, w3, w2):
    b, s, d = x.shape
    depth = w_ada.shape[0]
    t = b * s
    n_heads_b = b_forget.shape[1]
    n_items = 2 * t // TM_ROWS + N_EXPERTS
    c8 = jnp.zeros((8, d), F32).at[:b].set(c)
    x2 = x.reshape(t, d)
    row = lambda v: v.reshape(1, -1)
    for l in range(depth):
        mod3 = _ada(c8, w_ada[l], row(b_ada[l]))[:b].reshape(b, 6, d)
        proj, misc = _inproj(x2, mod3, row(g_mix[l]), _pack_w_in(jnp.transpose(w_in[l])), s)
        proj3 = proj.reshape(b, s, NP_COLS)
        misc3 = misc.reshape(b, s, LANES)
        bf_row = jnp.zeros((1, LANES), F32).at[0, MISC_F:MISC_F + n_heads_b].set(b_forget[l])
        kv, cums = _prep(proj3, misc3, row(g_kv[l]), bf_row)
        o_a = _dsa(proj3, misc3, kv, w_uk[l].astype(MXU_DTYPE), w_uv[l].astype(MXU_DTYPE), row(g_qa[l]))
        o_b = _fox(proj3, cums, row(g_qb[l]), row(g_kb[l]), n_heads_b)

        w_r = jnp.zeros((d, LANES), F32).at[:, :N_GROUPS].set(w_rg[l])
        w_r = w_r.at[:, N_GROUPS:N_GROUPS + N_EXPERTS].set(w_re[l])
        wr_hi = w_r.astype(MXU_DTYPE)
        wr_pair = jnp.concatenate([wr_hi, (w_r - wr_hi.astype(F32)).astype(MXU_DTYPE)], axis=1)
        b_r = jnp.zeros((1, LANES), F32).at[0, :N_GROUPS].set(b_rg[l])
        b_r = b_r.at[0, N_GROUPS:N_GROUPS + N_EXPERTS].set(b_re[l])
        x1, h2, rlog = _merge(proj, o_a.reshape(t, -1), o_b.reshape(t, -1), x2, mod3, row(b_gate[l]),
                              w_pa[l].astype(MXU_DTYPE), w_pb[l].astype(MXU_DTYPE),
                              w_o[l].astype(MXU_DTYPE), row(g_ffn[l]), wr_hi, wr_pair, b_r, s)
        route, pos_t, items = _route(rlog, n_items)
        pos0, pos1 = pos_t[0], pos_t[1]
        xs = _dispatch(pos0, pos1, h2)
        ys = _experts(items, xs, w1[l], w3[l], w2[l], n_items)
        x2 = _combine(pos0, pos1, ys, x1, route, mod3, s)
    return x2.reshape(b, s, d)
```

```python
import functools

import jax
import jax.numpy as jnp
from jax import lax
from jax.experimental import pallas as pl
from jax.experimental.pallas import tpu as pltpu

F32 = jnp.float32
I32 = jnp.int32
MXU_DTYPE = jnp.bfloat16

CHUNK = 64
HEAD_DIM = 128
D_LAT = 256
N_IDX_HEADS = 16
D_IDX = 64
TOPK_MAX = 256
N_GROUPS = 8
EXPERTS_PER_GROUP = 8
N_EXPERTS = N_GROUPS * EXPERTS_PER_GROUP
RMS_EPS = 1e-6

LANES = 128
VMEM_LIMIT = 56 * 1024 * 1024

NEG_BIG = -1e30


def _cparams(sem):
    return pltpu.CompilerParams(dimension_semantics=sem, vmem_limit_bytes=VMEM_LIMIT)


def _dot(a, b):
    return jnp.dot(a, b, preferred_element_type=F32)


def _dot_nt(a, b):
    return lax.dot_general(a, b, (((1,), (1,)), ((), ())), preferred_element_type=F32)


def _rms(x, g):
    return x * lax.rsqrt(jnp.mean(x * x, axis=-1, keepdims=True) + RMS_EPS) * g


def _ada_kernel(c_ref, w_ref, b_ref, o_ref):
    c = c_ref[...]
    a = c * jax.nn.sigmoid(c)
    o_ref[...] = _dot(a.astype(MXU_DTYPE), w_ref[...].astype(MXU_DTYPE)) + b_ref[...]


def _ada(c8, w_ada, b_ada):
    d, n = w_ada.shape
    tn = 2048
    return pl.pallas_call(
        _ada_kernel,
        grid=(n // tn,),
        in_specs=[pl.BlockSpec((8, d), lambda j: (0, 0)),
                  pl.BlockSpec((d, tn), lambda j: (0, j)),
                  pl.BlockSpec((1, tn), lambda j: (0, j))],
        out_specs=pl.BlockSpec((8, tn), lambda j: (0, j)),
        out_shape=jax.ShapeDtypeStruct((8, n), F32),
        compiler_params=_cparams(("arbitrary",)),
        name="ada",
    )(c8, w_ada, b_ada)


NP_COLS = 76 * LANES
U_QA, U_QIDX, U_QB, U_KB, U_VB, U_CKV, U_MISC = 32, 40, 48, 56, 64, 72, 74
TN_PROJ = 4 * LANES
TM_PROJ = 2048
TX_PROJ = 1024
MISC_TILE = (U_MISC * LANES) // TN_PROJ
MISC_OFF = U_MISC * LANES - MISC_TILE * TN_PROJ
MISC_K, MISC_F, MISC_W = 0, 64, 72


def _pack_moves():
    sizes = (1024, 256, 1024, 64, 16, 1024, 1024, 1024, 8, 4096)
    src = [0]
    for n in sizes:
        src.append(src[-1] + n)
    q_a, c_kv, q_idx, k_idx, w_idx, q_b, k_b, v_b, f_b, gate = src[:10]
    m = U_MISC * LANES
    return ((gate, 0, 4096), (q_a, U_QA * LANES, 1024), (q_idx, U_QIDX * LANES, 1024),
            (q_b, U_QB * LANES, 1024), (k_b, U_KB * LANES, 1024), (v_b, U_VB * LANES, 1024),
            (c_kv, U_CKV * LANES, 256), (f_b, m + MISC_F, 8), (w_idx, m + MISC_W, 16),
            (k_idx, m + MISC_K, 64))


def _pack_kernel(w_ref, o_ref):
    cols = o_ref.shape[1]
    m = U_MISC * LANES
    o_ref[m:m + 2 * LANES, :] = jnp.zeros((2 * LANES, cols), o_ref.dtype)
    for src, dst, n in _pack_moves():
        o_ref[dst:dst + n, :] = w_ref[src:src + n, :].astype(o_ref.dtype)


def _pack_w_in(w_in_t):
    n_in, d = w_in_t.shape
    tc = 256
    return pl.pallas_call(
        _pack_kernel,
        grid=(d // tc,),
        in_specs=[pl.BlockSpec((n_in, tc), lambda i: (0, i))],
        out_specs=pl.BlockSpec((NP_COLS, tc), lambda i: (0, i)),
        out_shape=jax.ShapeDtypeStruct((NP_COLS, d), MXU_DTYPE),
        compiler_params=_cparams(("parallel",)),
        name="pack",
    )(w_in_t)


def _inproj_kernel(x_ref, mod_ref, g_ref, w_ref, o_ref, misc_ref, h_scr, *, n_x):
    j = pl.program_id(1)
    tx = x_ref.shape[0]
    for part in range(n_x):
        @pl.when(j == part)
        def _():
            y = _rms(x_ref[...], g_ref[...])
            sh = mod_ref[0, 0:1, :]
            sc = mod_ref[0, 1:2, :]
            h_scr[part * tx:(part + 1) * tx, :] = (y * (1.0 + sc) + sh).astype(h_scr.dtype)

    @pl.when(j >= n_x)
    def _():
        acc = _dot_nt(h_scr[...], w_ref[...])
        o_ref[...] = acc.astype(o_ref.dtype)

        @pl.when(j == n_x + MISC_TILE)
        def _():
            misc_ref[...] = acc[:, MISC_OFF:MISC_OFF + LANES]


def _inproj(x2, mod3, g_mix, w_packed, seq):
    t, d = x2.shape
    tm = min(TM_PROJ, seq)
    tx = min(TX_PROJ, tm)
    n_x = tm // tx
    per_b = seq // tm
    wtile = lambda j: jnp.maximum(j - n_x, 0)
    return pl.pallas_call(
        functools.partial(_inproj_kernel, n_x=n_x),
        grid=(t // tm, n_x + NP_COLS // TN_PROJ),
        in_specs=[pl.BlockSpec((tx, d), lambda i, j: (i * n_x + jnp.minimum(j, n_x - 1), 0)),
                  pl.BlockSpec((1, 6, d), lambda i, j: (i // per_b, 0, 0)),
                  pl.BlockSpec((1, d), lambda i, j: (0, 0)),
                  pl.BlockSpec((TN_PROJ, d), lambda i, j: (wtile(j), 0))],
        out_specs=[pl.BlockSpec((tm, TN_PROJ), lambda i, j: (i, wtile(j))),
                   pl.BlockSpec((tm, LANES), lambda i, j: (i, 0))],
        out_shape=[jax.ShapeDtypeStruct((t, NP_COLS), MXU_DTYPE),
                   jax.ShapeDtypeStruct((t, LANES), F32)],
        scratch_shapes=[pltpu.VMEM((tm, d), MXU_DTYPE)],
        compiler_params=_cparams(("parallel", "arbitrary")),
        name="inproj",
    )(x2, mod3, g_mix, w_packed)


TK = 256


def _prep_kernel(ckv_ref, misc_ref, gkv_ref, bf_ref, kv_ref, cum_ref):
    s = ckv_ref.shape[1]
    kv_ref[0] = _rms(ckv_ref[0].astype(F32), gkv_ref[...]).astype(kv_ref.dtype)

    r = lax.broadcasted_iota(I32, (LANES, LANES), 0)
    c = lax.broadcasted_iota(I32, (LANES, LANES), 1)
    tri = jnp.where(c <= r, 1.0, 0.0).astype(MXU_DTYPE)
    carry = jnp.zeros((1, LANES), F32)
    for blk in range(s // LANES):
        z = misc_ref[0, blk * LANES:(blk + 1) * LANES, :] + bf_ref[...]
        ls = jnp.minimum(z, 0.0) - jnp.log1p(jnp.exp(-jnp.abs(z)))
        p1 = ls.astype(MXU_DTYPE)
        r1 = ls - p1.astype(F32)
        p2 = r1.astype(MXU_DTYPE)
        p3 = (r1 - p2.astype(F32)).astype(MXU_DTYPE)
        cs = _dot(tri, p1) + _dot(tri, p2) + _dot(tri, p3) + carry
        carry = cs[LANES - 1:LANES, :]
        cum_ref[0, blk * LANES:(blk + 1) * LANES, :] = cs * (HEAD_DIM ** 0.5)


def _prep(proj3, misc3, g_kv, bf_row):
    b, s, _ = proj3.shape
    return pl.pallas_call(
        _prep_kernel,
        grid=(b,),
        in_specs=[pl.BlockSpec((1, s, D_LAT), lambda i: (i, 0, U_CKV * LANES // D_LAT)),
                  pl.BlockSpec((1, s, LANES), lambda i: (i, 0, 0)),
                  pl.BlockSpec((1, D_LAT), lambda i: (0, 0)),
                  pl.BlockSpec((1, LANES), lambda i: (0, 0))],
        out_specs=[pl.BlockSpec((1, s, D_LAT), lambda i: (i, 0, 0)),
                   pl.BlockSpec((1, s, LANES), lambda i: (i, 0, 0))],
        out_shape=[jax.ShapeDtypeStruct((b, s, D_LAT), MXU_DTYPE),
                   jax.ShapeDtypeStruct((b, s, LANES), F32)],
        compiler_params=_cparams(("parallel",)),
        name="prep",
    )(proj3, misc3, g_kv, bf_row)


TQ_A = 256
N_BISECT = 32
LOG2E = 1.4426950408889634
assert TQ_A == TK


def _pair_loop(n, body):
    def pair(j, carry):
        body(2 * j, 2)
        return carry

    lax.fori_loop(0, jnp.right_shift(n, 1), pair, 0)

    @pl.when(jnp.bitwise_and(n, 1) == 1)
    def _():
        body(n - 1, 1)


def _tile_start(kt):
    return kt * TK if isinstance(kt, int) else pl.multiple_of(kt * TK, TK)


def _static_pairs(idx, n_cases, count, body):
    for c in range(n_cases):
        @pl.when(idx == c)
        def _():
            n = count(c)
            for kt in range(0, n, 2):
                body(kt, min(2, n - kt))


def _dsa_kernel(qa_ref, qidx_ref, miscq_ref, misck_ref, kv_ref, wuk_ref, wuv_ref, gqa_ref, o_ref,
                sc_scr, bias_scr, logit_scr, qlat_scr, qh_scr, mm_scr, thr_scr, mpart_scr, lpart_scr, oacc_scr,
                *, topk, n_heads):
    i = pl.program_id(1)
    tq = TQ_A
    nk = i + 1
    q0 = i * tq

    wt = miscq_ref[0].T
    wq = wt[MISC_W:MISC_W + N_IDX_HEADS, :] * (D_IDX ** -0.5 * N_IDX_HEADS ** -0.5)
    for h in range(N_IDX_HEADS):
        qh_scr[h] = qidx_ref[0, :, h * D_IDX:(h + 1) * D_IDX]

    def fold(v, op, rows):
        return op(v.reshape(v.shape[0] // rows, rows, tq), axis=0)

    mm_scr[0:8, :] = jnp.full((8, tq), NEG_BIG, F32)
    mm_scr[8:16, :] = jnp.full((8, tq), -NEG_BIG, F32)

    def score_tiles(kt, nt, last):
        mx8 = mm_scr[0:8, :]
        mn8 = mm_scr[8:16, :]
        for half in range(nt * (TK // LANES)):
            k0 = pl.multiple_of(kt * TK + half * LANES, LANES)
            kx = misck_ref[0, pl.ds(k0, LANES), :][:, MISC_K:MISC_K + D_IDX].astype(MXU_DTYPE)
            acc = jnp.zeros((LANES, tq), F32)
            for h in range(N_IDX_HEADS):
                d = _dot_nt(kx, qh_scr[h])
                acc = acc + jnp.maximum(d, 0.0) * wq[h:h + 1, :]
            if last:
                kpos = k0 + lax.broadcasted_iota(I32, (LANES, tq), 0)
                qpos = q0 + lax.broadcasted_iota(I32, (LANES, tq), 1)
                adm = (kpos // CHUNK) <= (qpos // CHUNK)
                sc_scr[pl.ds(k0, LANES), :] = jnp.where(adm, acc, NEG_BIG)
                mx8 = jnp.maximum(mx8, fold(jnp.where(adm, acc, NEG_BIG), jnp.max, 8))
                mn8 = jnp.minimum(mn8, fold(jnp.where(adm, acc, -NEG_BIG), jnp.min, 8))
            else:
                sc_scr[pl.ds(k0, LANES), :] = acc
                mx8 = jnp.maximum(mx8, fold(acc, jnp.max, 8))
                mn8 = jnp.minimum(mn8, fold(acc, jnp.min, 8))
        mm_scr[0:8, :] = mx8
        mm_scr[8:16, :] = mn8

    _pair_loop(i, lambda kt, nt: score_tiles(kt, nt, False))
    score_tiles(i, 1, True)

    def bisect(n_tiles):
        def bis_body(it, carry):
            lo, hi = carry
            mid = lo + 0.5 * (hi - lo)
            c32 = jnp.zeros((32, tq), F32)
            for kt in range(n_tiles):
                c32 = c32 + fold(jnp.where(sc_scr[kt * TK:(kt + 1) * TK, :] >= mid, 1.0, 0.0), jnp.sum, 32)
            ok = jnp.sum(c32, axis=0, keepdims=True) >= topk
            return jnp.where(ok, mid, lo), jnp.where(ok, hi, mid)

        lo, _ = lax.fori_loop(0, N_BISECT, bis_body, (jnp.min(mm_scr[8:16, :], axis=0, keepdims=True),
                                                      jnp.max(mm_scr[0:8, :], axis=0, keepdims=True)))
        thr_scr[...] = lo

    for c in range(sc_scr.shape[0] // TK):
        @pl.when(i == c)
        def _():
            bisect(c + 1)

    thr = thr_scr[...]

    def bias_body(kt, carry):
        k0 = _tile_start(kt)
        sel_t = sc_scr[pl.ds(k0, TK), :] >= thr
        bias_scr[kt] = jnp.where(sel_t, 0.0, NEG_BIG).T
        return carry

    lax.fori_loop(0, nk, bias_body, 0)

    for h in range(n_heads):
        ql = _dot(qa_ref[0, :, h * HEAD_DIM:(h + 1) * HEAD_DIM], wuk_ref[h])
        ql = _rms(ql, gqa_ref[...]) * (D_LAT ** -0.5 * LOG2E)
        qlat_scr[h * tq:(h + 1) * tq, :] = ql.astype(qlat_scr.dtype)
    mpart_scr[...] = jnp.full(mpart_scr.shape, NEG_BIG, F32)
    lpart_scr[...] = jnp.zeros(lpart_scr.shape, F32)
    oacc_scr[...] = jnp.zeros(oacc_scr.shape, F32)
    slopes2 = [2.0 ** (-8.0 * (h + 1) / n_heads) * LOG2E for h in range(n_heads)]

    def logit_tiles(kt, nt, last):
        k0 = _tile_start(kt)
        kvt = kv_ref[0, pl.ds(k0, nt * TK), :]
        kcol = (k0 + lax.broadcasted_iota(I32, (1, nt * TK), 1)).astype(F32)
        if last:
            ahead = jnp.maximum(lax.broadcasted_iota(I32, (tq, TK), 1)
                                - lax.broadcasted_iota(I32, (tq, TK), 0), 0).astype(F32)
        for h in range(n_heads):
            rows = slice(h * tq, (h + 1) * tq)
            lg = _dot_nt(qlat_scr[rows, :], kvt) + slopes2[h] * kcol
            for t in range(nt):
                lt = lg[:, t * TK:(t + 1) * TK] + bias_scr[kt + t]
                if last:
                    lt = lt - (2.0 * slopes2[h]) * ahead
                logit_scr[kt + t, rows, :] = lt
                mpart_scr[rows, :] = jnp.maximum(mpart_scr[rows, :],
                                                 jnp.maximum(lt[:, :LANES], lt[:, LANES:]))

    _pair_loop(i, lambda kt, nt: logit_tiles(kt, nt, False))
    logit_tiles(i, 1, True)

    m = jnp.max(mpart_scr[...], axis=-1, keepdims=True)
    mpart_scr[...] = jnp.broadcast_to(m, mpart_scr.shape)

    def pv_tiles(kt, nt):
        k0 = _tile_start(kt)
        kvt = kv_ref[0, pl.ds(k0, nt * TK), :]
        for h in range(n_heads):
            rows = slice(h * tq, (h + 1) * tq)
            mb = mpart_scr[rows, :]
            mb2 = jnp.concatenate([mb, mb], axis=1)
            ps = [jnp.exp2(logit_scr[kt + t, rows, :] - mb2) for t in range(nt)]
            lsum = ps[0][:, :LANES] + ps[0][:, LANES:]
            for p in ps[1:]:
                lsum = lsum + p[:, :LANES] + p[:, LANES:]
            lpart_scr[rows, :] += lsum
            p_all = ps[0] if nt == 1 else jnp.concatenate(ps, axis=1)
            oacc_scr[rows, :] += _dot(p_all.astype(MXU_DTYPE), kvt)

    _pair_loop(nk, pv_tiles)

    for h in range(n_heads):
        rows = slice(h * tq, (h + 1) * tq)
        lsum = jnp.sum(lpart_scr[rows, :], axis=-1, keepdims=True)
        o_lat = oacc_scr[rows, :] / lsum
        o_ref[0, :, h * HEAD_DIM:(h + 1) * HEAD_DIM] = _dot(
            o_lat.astype(MXU_DTYPE), wuv_ref[h]).astype(o_ref.dtype)


def _dsa(proj3, misc3, kv, w_uk, w_uv, g_qa):
    b, s, _ = proj3.shape
    n_heads = w_uk.shape[0]
    width = n_heads * HEAD_DIM
    topk = min(TOPK_MAX, s // 4)
    tq = TQ_A
    nkt = s // TK
    kern = functools.partial(_dsa_kernel, topk=topk, n_heads=n_heads)
    return pl.pallas_call(
        kern,
        grid=(b, s // tq),
        in_specs=[pl.BlockSpec((1, tq, width), lambda bb, i: (bb, i, U_QA * LANES // width)),
                  pl.BlockSpec((1, tq, N_IDX_HEADS * D_IDX),
                               lambda bb, i: (bb, i, U_QIDX * LANES // (N_IDX_HEADS * D_IDX))),
                  pl.BlockSpec((1, tq, LANES), lambda bb, i: (bb, i, 0)),
                  pl.BlockSpec((1, s, LANES), lambda bb, i: (bb, 0, 0)),
                  pl.BlockSpec((1, s, D_LAT), lambda bb, i: (bb, 0, 0)),
                  pl.BlockSpec((n_heads, HEAD_DIM, D_LAT), lambda bb, i: (0, 0, 0)),
                  pl.BlockSpec((n_heads, D_LAT, HEAD_DIM), lambda bb, i: (0, 0, 0)),
                  pl.BlockSpec((1, D_LAT), lambda bb, i: (0, 0))],
        out_specs=pl.BlockSpec((1, tq, width), lambda bb, i: (bb, i, 0)),
        out_shape=jax.ShapeDtypeStruct((b, s, width), MXU_DTYPE),
        scratch_shapes=[pltpu.VMEM((s, tq), F32),
                        pltpu.VMEM((nkt, tq, TK), F32),
                        pltpu.VMEM((nkt, n_heads * tq, TK), F32),
                        pltpu.VMEM((n_heads * tq, D_LAT), MXU_DTYPE),
                        pltpu.VMEM((N_IDX_HEADS, tq, D_IDX), MXU_DTYPE),
                        pltpu.VMEM((16, tq), F32),
                        pltpu.VMEM((1, tq), F32),
                        pltpu.VMEM((n_heads * tq, LANES), F32),
                        pltpu.VMEM((n_heads * tq, LANES), F32),
                        pltpu.VMEM((n_heads * tq, D_LAT), F32)],
        compiler_params=_cparams(("parallel", "arbitrary")),
        name="dsa",
    )(proj3, proj3, misc3, misc3, kv, w_uk, w_uv, g_qa)


TQ_B = 256


HG_B = 8
assert TQ_B == TK


def _split3(x):
    p1 = x.astype(MXU_DTYPE)
    r1 = x - p1.astype(F32)
    p2 = r1.astype(MXU_DTYPE)
    p3 = (r1 - p2.astype(F32)).astype(MXU_DTYPE)
    return p1, p2, p3


def _fox_kernel(q_ref, k_ref, v_ref, cumq_ref, cumk_ref, gq_ref, gk_ref, o_ref,
                kaug_scr, qaug_scr, logit_scr, mpart_scr, lpart_scr, oacc_scr):
    g = pl.program_id(1)
    qi = pl.program_id(2)
    tq = TQ_B
    s = k_ref.shape[1]
    cscale = (HEAD_DIM ** -0.5) * LOG2E

    rr = lax.broadcasted_iota(I32, (3 * LANES, LANES), 0)
    cc = lax.broadcasted_iota(I32, (3 * LANES, LANES), 1)
    lane_row = lax.broadcasted_iota(I32, (1, LANES), 1)
    ones_q = jnp.where((lane_row >= 3) & (lane_row < 6), 1.0, 0.0)
    ones_k = jnp.where(lane_row < 3, 1.0, 0.0)
    base_q = jnp.where(cc < 3, rr - LANES * cc, -1)
    base_k = jnp.where((cc >= 3) & (cc < 6), rr - LANES * (cc - 3), -1)

    def aug(pieces, h, base, sign, ones_row):
        e = jnp.where(base == h, sign, 0.0).astype(MXU_DTYPE)
        return (_dot(pieces, e) + ones_row).astype(MXU_DTYPE)

    @pl.when(qi == 0)
    def _():
        def kbody(c, carry):
            r0 = pl.multiple_of(c * TK, TK)
            ck = jnp.concatenate(_split3(cumk_ref[0, pl.ds(r0, TK), :]), axis=1)
            for hh in range(HG_B):
                kn = _rms(k_ref[0, pl.ds(r0, TK), hh * HEAD_DIM:(hh + 1) * HEAD_DIM].astype(F32), gk_ref[...])
                kaug_scr[hh, pl.ds(r0, TK), 0:HEAD_DIM] = kn.astype(MXU_DTYPE)
                kaug_scr[hh, pl.ds(r0, TK), HEAD_DIM:] = aug(ck, MISC_F + g * HG_B + hh, base_k, -1.0, ones_k)
            return carry

        lax.fori_loop(0, s // TK, kbody, 0)

    cq = jnp.concatenate(_split3(cumq_ref[0]), axis=1)
    for hh in range(HG_B):
        qn = _rms(q_ref[0, :, hh * HEAD_DIM:(hh + 1) * HEAD_DIM].astype(F32), gq_ref[...])
        qaug_scr[hh, :, 0:HEAD_DIM] = qn.astype(MXU_DTYPE)
        qaug_scr[hh, :, HEAD_DIM:] = aug(cq, MISC_F + g * HG_B + hh, base_q, 1.0, ones_q)
    mpart_scr[...] = jnp.full(mpart_scr.shape, NEG_BIG, F32)
    lpart_scr[...] = jnp.zeros(lpart_scr.shape, F32)
    oacc_scr[...] = jnp.zeros(oacc_scr.shape, F32)

    def put_logits(kt, hh, lg):
        logit_scr[kt, hh] = lg
        mpart_scr[hh] = jnp.maximum(mpart_scr[hh], jnp.maximum(lg[:, :LANES], lg[:, LANES:]))

    def raw_logits(kt, nt, hh):
        k0 = _tile_start(kt)
        return _dot_nt(qaug_scr[hh], kaug_scr[hh, pl.ds(k0, nt * TK), :]) * cscale

    def off_tiles(kt, nt):
        for hh in range(HG_B):
            lg = raw_logits(kt, nt, hh)
            for t in range(nt):
                put_logits(kt + t, hh, lg[:, t * TK:(t + 1) * TK])

    _static_pairs(qi, s // TK, lambda c: c, off_tiles)
    causal = (lax.broadcasted_iota(I32, (tq, TK), 1) <= lax.broadcasted_iota(I32, (tq, TK), 0))
    for hh in range(HG_B):
        put_logits(qi, hh, jnp.where(causal, raw_logits(qi, 1, hh), NEG_BIG))

    for hh in range(HG_B):
        m = jnp.max(mpart_scr[hh], axis=-1, keepdims=True)
        mpart_scr[hh] = jnp.broadcast_to(m, (tq, LANES))

    def pv_tiles(kt, nt):
        k0 = _tile_start(kt)
        for hh in range(HG_B):
            mb = mpart_scr[hh]
            mb2 = jnp.concatenate([mb, mb], axis=1)
            ps = [jnp.exp2(logit_scr[kt + t, hh] - mb2) for t in range(nt)]
            lsum = ps[0][:, :LANES] + ps[0][:, LANES:]
            for p in ps[1:]:
                lsum = lsum + p[:, :LANES] + p[:, LANES:]
            lpart_scr[hh] += lsum
            p_all = ps[0] if nt == 1 else jnp.concatenate(ps, axis=1)
            oacc_scr[hh] += _dot(p_all.astype(MXU_DTYPE),
                                 v_ref[0, pl.ds(k0, nt * TK), hh * HEAD_DIM:(hh + 1) * HEAD_DIM])

    _static_pairs(qi, s // TK, lambda c: c + 1, pv_tiles)
    for hh in range(HG_B):
        lsum = jnp.sum(lpart_scr[hh], axis=-1, keepdims=True)
        o_ref[0, :, hh * HEAD_DIM:(hh + 1) * HEAD_DIM] = (oacc_scr[hh] / lsum).astype(o_ref.dtype)


def _fox(proj3, cums, g_qb, g_kb, n_heads):
    b, s, _ = proj3.shape
    tq = TQ_B
    gw = HG_B * HEAD_DIM
    return pl.pallas_call(
        _fox_kernel,
        grid=(b, n_heads // HG_B, s // tq),
        in_specs=[pl.BlockSpec((1, tq, gw), lambda bb, g, i: (bb, i, U_QB // HG_B + g)),
                  pl.BlockSpec((1, s, gw), lambda bb, g, i: (bb, 0, U_KB // HG_B + g)),
                  pl.BlockSpec((1, s, gw), lambda bb, g, i: (bb, 0, U_VB // HG_B + g)),
                  pl.BlockSpec((1, tq, LANES), lambda bb, g, i: (bb, i, 0)),
                  pl.BlockSpec((1, s, LANES), lambda bb, g, i: (bb, 0, 0)),
                  pl.BlockSpec((1, HEAD_DIM), lambda bb, g, i: (0, 0)),
                  pl.BlockSpec((1, HEAD_DIM), lambda bb, g, i: (0, 0))],
        out_specs=pl.BlockSpec((1, tq, gw), lambda bb, g, i: (bb, i, g)),
        out_shape=jax.ShapeDtypeStruct((b, s, n_heads * HEAD_DIM), MXU_DTYPE),
        scratch_shapes=[pltpu.VMEM((HG_B, s, 2 * HEAD_DIM), MXU_DTYPE),
                        pltpu.VMEM((HG_B, tq, 2 * HEAD_DIM), MXU_DTYPE),
                        pltpu.VMEM((s // TK, HG_B, tq, TK), F32),
                        pltpu.VMEM((HG_B, tq, LANES), F32),
                        pltpu.VMEM((HG_B, tq, LANES), F32),
                        pltpu.VMEM((HG_B, tq, HEAD_DIM), F32)],
        compiler_params=_cparams(("parallel", "parallel", "arbitrary")),
        name="fox",
    )(proj3, proj3, proj3, cums, cums, g_qb, g_kb)


TM_MERGE = 256


def _merge_kernel(ga_ref, gb_ref, oa_ref, ob_ref, x_ref, mod_ref, bga_ref, bgb_ref, wpa_ref, wpb_ref,
                  wo_ref, gffn_ref, wrh_ref, wrp_ref, br_ref, x1_ref, h2_ref, rl_ref):
    ga = jax.nn.sigmoid(ga_ref[...].astype(F32) + bga_ref[...])
    gb = jax.nn.sigmoid(gb_ref[...].astype(F32) + bgb_ref[...])
    merged = ga * _dot(oa_ref[...], wpa_ref[...]) + gb * _dot(ob_ref[...], wpb_ref[...])
    upd = _dot(merged.astype(MXU_DTYPE), wo_ref[...])
    x1 = x_ref[...] + mod_ref[0, 2:3, :] * upd
    x1_ref[...] = x1
    h2 = _rms(x1, gffn_ref[...]) * (1.0 + mod_ref[0, 4:5, :]) + mod_ref[0, 3:4, :]
    h2_ref[...] = h2
    hh = h2.astype(MXU_DTYPE)
    hl = (h2 - hh.astype(F32)).astype(MXU_DTYPE)
    both = _dot(hh, wrp_ref[...])
    rl_ref[...] = both[:, :LANES] + both[:, LANES:] + _dot(hl, wrh_ref[...]) + br_ref[...]


def _merge(proj, o_a, o_b, x2, mod3, b_gate, w_pa, w_pb, w_o, g_ffn, wr_hi, wr_pair, b_r, seq):
    t, d = x2.shape
    tm = TM_MERGE
    per_b = seq // tm
    wa = o_a.shape[1]
    res = lambda shape: pl.BlockSpec(shape, lambda i: (0,) * len(shape), pipeline_mode=pl.Buffered(1))
    return pl.pallas_call(
        _merge_kernel,
        grid=(t // tm,),
        in_specs=[pl.BlockSpec((tm, d), lambda i: (i, 0)),
                  pl.BlockSpec((tm, d), lambda i: (i, 1)),
                  pl.BlockSpec((tm, wa), lambda i: (i, 0)),
                  pl.BlockSpec((tm, wa), lambda i: (i, 0)),
                  pl.BlockSpec((tm, d), lambda i: (i, 0)),
                  pl.BlockSpec((1, 6, d), lambda i: (i // per_b, 0, 0)),
                  pl.BlockSpec((1, d), lambda i: (0, 0)),
                  pl.BlockSpec((1, d), lambda i: (0, 1)),
                  res((wa, d)), res((wa, d)), res((d, d)),
                  pl.BlockSpec((1, d), lambda i: (0, 0)),
                  res((d, LANES)), res((d, 2 * LANES)),
                  pl.BlockSpec((1, LANES), lambda i: (0, 0))],
        out_specs=[pl.BlockSpec((tm, d), lambda i: (i, 0)),
                   pl.BlockSpec((tm, d), lambda i: (i, 0)),
                   pl.BlockSpec((tm, LANES), lambda i: (i, 0))],
        out_shape=[jax.ShapeDtypeStruct((t, d), F32),
                   jax.ShapeDtypeStruct((t, d), F32),
                   jax.ShapeDtypeStruct((t, LANES), F32)],
        compiler_params=_cparams(("parallel",)),
        name="merge",
    )(proj, proj, o_a, o_b, x2, mod3, b_gate, b_gate, w_pa, w_pb, w_o, g_ffn, wr_hi, wr_pair, b_r)


TM_ROUTE = 1024
TM_ROWS = 256
MM_ROWS = 128
R_E0, R_E1, R_P0, R_P1 = 0, 1, 4, 5
IT_TILE, IT_E, IT_LO, IT_HI, IT_FLAG, IT_NEXT, IT_NEXT2, IT_SLOT = range(8)
F_VALID, F_FIRST_OF_EXPERT, F_FIRST_OF_TILE = 1, 2, 4


def _route_kernel(rl_ref, route_ref, post_ref, items_ref, cnt_scr, run_scr, offs_scr, *, n_tiles):
    tm = rl_ref.shape[0]
    sweep = pl.program_id(0)
    step = pl.program_id(1)
    iw = items_ref.shape[1]

    @pl.when((sweep == 0) & (step == 0))
    def _():
        cnt_scr[...] = jnp.zeros(cnt_scr.shape, F32)

    r = rl_ref[...]
    lane = lax.broadcasted_iota(I32, (tm, LANES), 1).astype(F32)
    neg_inf = -jnp.inf
    gmask = lane < N_GROUPS
    gl = jnp.where(gmask, r, neg_inf)
    gmax = jnp.max(gl, axis=-1, keepdims=True)
    gidx = jnp.min(jnp.where(gl == gmax, lane, float(LANES)), axis=-1, keepdims=True)
    gsum = jnp.sum(jnp.where(gmask, jnp.exp(r - gmax), 0.0), axis=-1, keepdims=True)
    gw = 1.0 / gsum
    lo = N_GROUPS + EXPERTS_PER_GROUP * gidx
    emask = (lane >= lo) & (lane < lo + EXPERTS_PER_GROUP)
    el = jnp.where(emask, r, neg_inf)
    v0 = jnp.max(el, axis=-1, keepdims=True)
    i0 = jnp.min(jnp.where(el == v0, lane, float(LANES)), axis=-1, keepdims=True)
    el2 = jnp.where(lane == i0, neg_inf, el)
    v1 = jnp.max(el2, axis=-1, keepdims=True)
    i1 = jnp.min(jnp.where(el2 == v1, lane, float(LANES)), axis=-1, keepdims=True)
    tt = jnp.exp(v1 - v0)
    p0 = gw / (1.0 + tt)
    p1 = gw * tt / (1.0 + tt)
    e0 = i0 - N_GROUPS
    e1 = i1 - N_GROUPS

    hit0 = lane == e0
    hit1 = lane == e1
    oh = jnp.where(hit0 | hit1, 1.0, 0.0)

    @pl.when(sweep == 0)
    def _():
        ones = jnp.ones((tm, LANES), MXU_DTYPE)
        cnt_scr[...] += _dot(oh.T.astype(MXU_DTYPE), ones)

    @pl.when((sweep == 1) & (step == 0))
    def _():
        _plan_items(cnt_scr[...], items_ref, offs_scr, n_tiles, iw)
        run_scr[...] = jnp.zeros(run_scr.shape, F32)

    @pl.when(sweep == 1)
    def _():
        rr = lax.broadcasted_iota(I32, (tm, tm), 0)
        cc = lax.broadcasted_iota(I32, (tm, tm), 1)
        ltri = jnp.where(cc < rr, 1.0, 0.0).astype(MXU_DTYPE)
        before = _dot(ltri, oh.astype(MXU_DTYPE)) + run_scr[...] + offs_scr[...]
        pos0 = jnp.sum(jnp.where(hit0, before, 0.0), axis=-1, keepdims=True)
        pos1 = jnp.sum(jnp.where(hit1, before, 0.0), axis=-1, keepdims=True)
        run_scr[...] = run_scr[...] + jnp.sum(oh, axis=0, keepdims=True)
        out = jnp.zeros((tm, LANES), F32)
        for k, val in ((R_E0, e0), (R_E1, e1), (R_P0, p0), (R_P1, p1)):
            out = jnp.where(lane == k, val, out)
        route_ref[...] = out
        pmat = jnp.where(lane == 0.0, pos0, jnp.where(lane == 1.0, pos1, 0.0))
        post_ref[...] = pmat.T[0:8, :].astype(I32)


def _plan_items(cnt_col, items_ref, offs_scr, n_tiles, iw):
    tmr = float(TM_ROWS)
    sub = lax.broadcasted_iota(I32, (LANES, LANES), 0)
    lan = lax.broadcasted_iota(I32, (LANES, LANES), 1)
    lstrict = jnp.where(lan < sub, 1.0, 0.0).astype(MXU_DTYPE)
    hi = jnp.floor(cnt_col * (1.0 / LANES))
    lo = cnt_col - hi * LANES
    offs_col = _dot(lstrict, hi.astype(MXU_DTYPE)) * LANES + _dot(lstrict, lo.astype(MXU_DTYPE))
    first_t = jnp.floor(offs_col * (1.0 / tmr))
    last_t = jnp.floor((offs_col + cnt_col - 1.0) * (1.0 / tmr))
    n_col = jnp.where(cnt_col > 0.0, last_t - first_t + 1.0, 0.0)
    base_col = _dot(lstrict, n_col.astype(MXU_DTYPE))
    end_col = base_col + n_col
    offs_scr[...] = offs_col.T[0:1, :]

    rep = lambda col: jnp.concatenate([col] * (iw // LANES), axis=1)
    wl = lax.broadcasted_iota(I32, (LANES, iw), 1).astype(F32)
    sub_f = lax.broadcasted_iota(I32, (LANES, iw), 0).astype(F32)
    end_r = rep(end_col)
    w_total = end_r[LANES - 1:LANES, :]
    count_le = lambda v: jnp.sum(jnp.where(end_r <= v, 1.0, 0.0), axis=0, keepdims=True)
    w_row = wl[0:1, :]
    eidx = count_le(w_row)
    sel = sub_f == eidx
    pick = lambda col: jnp.sum(jnp.where(sel, rep(col), 0.0), axis=0, keepdims=True)
    e_base, e_first, e_offs, e_cnt, e_end = (pick(base_col), pick(first_t), pick(offs_col),
                                             pick(cnt_col), pick(end_col))
    valid = w_row < w_total
    tile = e_first + (w_row - e_base)
    row_lo = jnp.maximum(e_offs, tile * tmr) - tile * tmr
    row_hi = jnp.minimum(e_offs + e_cnt, (tile + 1.0) * tmr) - tile * tmr
    first_e = w_row == e_base
    flags = jnp.where(valid, F_VALID + jnp.where(first_e, float(F_FIRST_OF_EXPERT), 0.0)
                      + jnp.where(row_lo == 0.0, float(F_FIRST_OF_TILE), 0.0), 0.0)
    has1 = valid & first_e & (e_end < w_total)
    e1 = count_le(e_end)
    end1 = jnp.sum(jnp.where(sub_f == e1, end_r, 0.0), axis=0, keepdims=True)
    has2 = has1 & (end1 < w_total)
    nxt = jnp.where(has1, e1, -1.0)
    nxt2 = jnp.where(has2, count_le(end1), -1.0)
    ord_col = _dot(lstrict, jnp.where(cnt_col > 0.0, 1.0, 0.0).astype(MXU_DTYPE))
    e_ord = pick(ord_col)
    slot = e_ord - 2.0 * jnp.floor(e_ord * 0.5)
    e_last = count_le(w_total - 1.0)
    rows = {IT_TILE: jnp.where(valid, tile, n_tiles - 1.0),
            IT_E: jnp.where(valid, eidx, e_last),
            IT_LO: jnp.where(valid, row_lo, 0.0),
            IT_HI: jnp.where(valid, row_hi, 0.0),
            IT_FLAG: flags,
            IT_NEXT: nxt,
            IT_NEXT2: nxt2,
            IT_SLOT: jnp.where(valid, slot, 0.0)}
    sub8 = lax.broadcasted_iota(I32, (8, iw), 0)
    out = jnp.zeros((8, iw), F32)
    for k, val in rows.items():
        out = jnp.where(sub8 == k, val, out)
    items_ref[...] = out.astype(I32)


def _route(rlog, n_items):
    t = rlog.shape[0]
    tm = min(TM_ROUTE, t)
    iw = ((n_items + LANES - 1) // LANES) * LANES
    kern = functools.partial(_route_kernel, n_tiles=2 * t // TM_ROWS)
    return pl.pallas_call(
        kern,
        grid=(2, t // tm),
        in_specs=[pl.BlockSpec((tm, LANES), lambda p, i: (i, 0))],
        out_specs=[pl.BlockSpec((tm, LANES), lambda p, i: (i * p, 0)),
                   pl.BlockSpec((8, tm), lambda p, i: (0, i * p)),
                   pl.BlockSpec((8, iw), lambda p, i: (0, 0))],
        out_shape=[jax.ShapeDtypeStruct((t, LANES), F32),
                   jax.ShapeDtypeStruct((8, t), I32),
                   jax.ShapeDtypeStruct((8, iw), I32)],
        scratch_shapes=[pltpu.VMEM((LANES, LANES), F32),
                        pltpu.VMEM((1, LANES), F32),
                        pltpu.VMEM((1, LANES), F32)],
        compiler_params=_cparams(("arbitrary", "arbitrary")),
        name="route",
    )(rlog)


TM_DISP = 1024


def _dispatch_kernel(pos0_ref, pos1_ref, h2_ref, xs_ref, sem):
    tm = h2_ref.shape[0]
    base = pl.program_id(0) * tm

    def row_copy(r, p):
        return pltpu.make_async_copy(h2_ref.at[pl.ds(r, 1), :], xs_ref.at[pl.ds(p, 1), :], sem)

    def issue(r, carry):
        row_copy(r, pos0_ref[base + r]).start(priority=0)
        row_copy(r, pos1_ref[base + r]).start(priority=1)
        return carry

    lax.fori_loop(0, tm, issue, 0, unroll=8)

    def drain(r, carry):
        row_copy(0, 0).wait()
        row_copy(0, 0).wait()
        return carry

    lax.fori_loop(0, tm, drain, 0, unroll=8)


def _dispatch(pos0, pos1, h2):
    t, d = h2.shape
    tm = min(TM_DISP, t)
    return pl.pallas_call(
        _dispatch_kernel,
        grid_spec=pltpu.PrefetchScalarGridSpec(
            num_scalar_prefetch=2,
            grid=(t // tm,),
            in_specs=[pl.BlockSpec((tm, d), lambda i, p0, p1: (i, 0))],
            out_specs=pl.BlockSpec(memory_space=pl.ANY),
            scratch_shapes=[pltpu.SemaphoreType.DMA(())]),
        out_shape=jax.ShapeDtypeStruct((2 * t, d), h2.dtype),
        compiler_params=pltpu.CompilerParams(dimension_semantics=("arbitrary",),
                                             vmem_limit_bytes=VMEM_LIMIT, has_side_effects=True),
        name="dispatch",
    )(pos0, pos1, h2)


WEIGHT_DMA_PRIORITY = 1


def _experts_kernel(tile_ref, e_ref, lo_ref, hi_ref, flag_ref, next_ref, next2_ref, slot_ref,
                    xs_ref, w1_hbm, w3_hbm, w2_hbm, ys_ref,
                    f1_scr, f3_scr, f2_scr, w1_scr, w3_scr, w2_scr, sems):
    del tile_ref
    w = pl.program_id(0)
    flag = flag_ref[w]

    mats = ((w1_hbm, f1_scr, w1_scr), (w3_hbm, f3_scr, w3_scr), (w2_hbm, f2_scr, w2_scr))

    def weight_copy(k, e, slot):
        return pltpu.make_async_copy(mats[k][0].at[e], mats[k][1].at[slot], sems.at[slot, k])

    @pl.when(w == 0)
    def _():
        for k in range(3):
            weight_copy(k, e_ref[0], 0).start(priority=WEIGHT_DMA_PRIORITY)
        nxt = next_ref[0]

        @pl.when(nxt >= 0)
        def _():
            for k in range(3):
                weight_copy(k, nxt, 1).start(priority=WEIGHT_DMA_PRIORITY)

    @pl.when((flag & F_FIRST_OF_EXPERT) != 0)
    def _():
        slot = slot_ref[w]
        nxt2 = next2_ref[w]
        for k in range(3):
            weight_copy(k, e_ref[w], slot).wait()
            mats[k][2][...] = mats[k][1][slot].astype(mats[k][2].dtype)

            @pl.when(nxt2 >= 0)
            def _():
                weight_copy(k, nxt2, slot).start(priority=WEIGHT_DMA_PRIORITY)

    @pl.when(((flag & F_VALID) != 0) & ((flag & F_FIRST_OF_TILE) != 0))
    def _():
        ys_ref[...] = jnp.zeros(ys_ref.shape, ys_ref.dtype)

    lo = lo_ref[w]
    hi = hi_ref[w]
    for part in range(TM_ROWS // MM_ROWS):
        r0 = part * MM_ROWS

        @pl.when(((flag & F_VALID) != 0) & (lo < r0 + MM_ROWS) & (hi > r0))
        def _():
            x = xs_ref[r0:r0 + MM_ROWS, :].astype(MXU_DTYPE)
            a = _dot(x, w1_scr[...])
            u = _dot(x, w3_scr[...])
            hm = (a * jax.nn.sigmoid(a)) * u
            res = _dot(hm.astype(MXU_DTYPE), w2_scr[...])
            row = r0 + lax.broadcasted_iota(I32, res.shape, 0)
            mine = (row >= lo) & (row < hi)
            ys_ref[r0:r0 + MM_ROWS, :] = jnp.where(mine, res, ys_ref[r0:r0 + MM_ROWS, :])


def _experts(items, xs, w1, w3, w2, n_items):
    n_rows, d = xs.shape
    f = w1.shape[2]
    tm = TM_ROWS
    tile_map = lambda w, tile, *_: (tile[w], 0)
    return pl.pallas_call(
        _experts_kernel,
        grid_spec=pltpu.PrefetchScalarGridSpec(
            num_scalar_prefetch=8,
            grid=(n_items,),
            in_specs=[pl.BlockSpec((tm, d), tile_map),
                      pl.BlockSpec(memory_space=pl.ANY),
                      pl.BlockSpec(memory_space=pl.ANY),
                      pl.BlockSpec(memory_space=pl.ANY)],
            out_specs=pl.BlockSpec((tm, d), tile_map),
            scratch_shapes=[pltpu.VMEM((2, d, f), F32),
                            pltpu.VMEM((2, d, f), F32),
                            pltpu.VMEM((2, f, d), F32),
                            pltpu.VMEM((d, f), MXU_DTYPE),
                            pltpu.VMEM((d, f), MXU_DTYPE),
                            pltpu.VMEM((f, d), MXU_DTYPE),
                            pltpu.SemaphoreType.DMA((2, 3))]),
        out_shape=jax.ShapeDtypeStruct((n_rows, d), F32),
        compiler_params=_cparams(("arbitrary",)),
        name="experts",
    )(items[IT_TILE], items[IT_E], items[IT_LO], items[IT_HI], items[IT_FLAG], items[IT_NEXT],
      items[IT_NEXT2], items[IT_SLOT], xs, w1, w3, w2)


TM_COMB = 256


def _combine_kernel(pos0_ref, pos1_ref, ys_ref, x1_ref, route_ref, mod_ref, o_ref, y0_scr, y1_scr, sems):
    step = pl.program_id(0)
    n_steps = pl.num_programs(0)

    def row_copy(p, dst, slot, r):
        return pltpu.make_async_copy(ys_ref.at[pl.ds(p, 1), :], dst.at[slot, pl.ds(r, 1), :], sems.at[slot])

    def issue_step(s):
        slot = s % 2
        base = s * TM_COMB

        def issue(r, carry):
            row_copy(pos0_ref[base + r], y0_scr, slot, r).start(priority=0)
            row_copy(pos1_ref[base + r], y1_scr, slot, r).start(priority=1)
            return carry

        lax.fori_loop(0, TM_COMB, issue, 0, unroll=8)

    @pl.when(step == 0)
    def _():
        issue_step(step)

    @pl.when(step + 1 < n_steps)
    def _():
        issue_step(step + 1)

    slot = step % 2

    def drain(r, carry):
        row_copy(0, y0_scr, slot, 0).wait()
        row_copy(0, y1_scr, slot, 0).wait()
        return carry

    lax.fori_loop(0, TM_COMB, drain, 0, unroll=8)

    rt = route_ref[...]
    lane = lax.broadcasted_iota(I32, rt.shape, 1)
    p0 = jnp.sum(jnp.where(lane == R_P0, rt, 0.0), axis=-1, keepdims=True)
    p1 = jnp.sum(jnp.where(lane == R_P1, rt, 0.0), axis=-1, keepdims=True)
    y = p0 * y0_scr[slot] + p1 * y1_scr[slot]
    o_ref[...] = x1_ref[...] + mod_ref[0, 5:6, :] * y


def _combine(pos0, pos1, ys, x1, route, mod3, seq):
    t, d = x1.shape
    tm = TM_COMB
    per_b = seq // tm
    return pl.pallas_call(
        _combine_kernel,
        grid_spec=pltpu.PrefetchScalarGridSpec(
            num_scalar_prefetch=2,
            grid=(t // tm,),
            in_specs=[pl.BlockSpec(memory_space=pl.ANY),
                      pl.BlockSpec((tm, d), lambda i, p0, p1: (i, 0)),
                      pl.BlockSpec((tm, LANES), lambda i, p0, p1: (i, 0)),
                      pl.BlockSpec((1, 6, d), lambda i, p0, p1: (i // per_b, 0, 0))],
            out_specs=pl.BlockSpec((tm, d), lambda i, p0, p1: (i, 0)),
            scratch_shapes=[pltpu.VMEM((2, tm, d), F32),
                            pltpu.VMEM((2, tm, d), F32),
                            pltpu.SemaphoreType.DMA((2,))]),
        out_shape=jax.ShapeDtypeStruct((t, d), F32),
        compiler_params=_cparams(("arbitrary",)),
        name="combine",
    )(pos0, pos1, ys, x1, route, mod3)


def kernel(x, c, w_ada, b_ada, g_mix, w_in, b_gate, b_forget, w_uk, w_uv, g_qa, g_kv, g_qb, g_kb,
           w_pa, w_pb, w_o, g_ffn, w_rg, b_rg, w_re, b_re, w1, w3, w2):
    b, s, d = x.shape
    depth = w_ada.shape[0]
    t = b * s
    n_heads_b = b_forget.shape[1]
    n_items = 2 * t // TM_ROWS + N_EXPERTS
    c8 = jnp.zeros((8, d), F32).at[:b].set(c)
    x2 = x.reshape(t, d)
    row = lambda v: v.reshape(1, -1)
    for l in range(depth):
        mod3 = _ada(c8, w_ada[l], row(b_ada[l]))[:b].reshape(b, 6, d)
        proj, misc = _inproj(x2, mod3, row(g_mix[l]), _pack_w_in(jnp.transpose(w_in[l])), s)
        proj3 = proj.reshape(b, s, NP_COLS)
        misc3 = misc.reshape(b, s, LANES)
        bf_row = jnp.zeros((1, LANES), F32).at[0, MISC_F:MISC_F + n_heads_b].set(b_forget[l])
        kv, cums = _prep(proj3, misc3, row(g_kv[l]), bf_row)
        o_a = _dsa(proj3, misc3, kv, w_uk[l].astype(MXU_DTYPE), w_uv[l].astype(MXU_DTYPE), row(g_qa[l]))
        o_b = _fox(proj3, cums, row(g_qb[l]), row(g_kb[l]), n_heads_b)

        w_r = jnp.zeros((d, LANES), F32).at[:, :N_GROUPS].set(w_rg[l])
        w_r = w_r.at[:, N_GROUPS:N_GROUPS + N_EXPERTS].set(w_re[l])
        wr_hi = w_r.astype(MXU_DTYPE)
        wr_pair = jnp.concatenate([wr_hi, (w_r - wr_hi.astype(F32)).astype(MXU_DTYPE)], axis=1)
        b_r = jnp.zeros((1, LANES), F32).at[0, :N_GROUPS].set(b_rg[l])
        b_r = b_r.at[0, N_GROUPS:N_GROUPS + N_EXPERTS].set(b_re[l])
        x1, h2, rlog = _merge(proj, o_a.reshape(t, -1), o_b.reshape(t, -1), x2, mod3, row(b_gate[l]),
                              w_pa[l].astype(MXU_DTYPE), w_pb[l].astype(MXU_DTYPE),
                              w_o[l].astype(MXU_DTYPE), row(g_ffn[l]), wr_hi, wr_pair, b_r, s)
        route, pos_t, items = _route(rlog, n_items)
        pos0, pos1 = pos_t[0], pos_t[1]
        xs = _dispatch(pos0, pos1, h2)
        ys = _experts(items, xs, w1[l], w3[l], w2[l], n_items)
        x2 = _combine(pos0, pos1, ys, x1, route, mod3, s)
    return x2.reshape(b, s, d)
```

```python
import functools

import jax
import jax.numpy as jnp
from jax import lax
from jax.experimental import pallas as pl
from jax.experimental.pallas import tpu as pltpu

F32 = jnp.float32
I32 = jnp.int32
MXU_DTYPE = jnp.bfloat16

CHUNK = 64
HEAD_DIM = 128
D_LAT = 256
N_IDX_HEADS = 16
D_IDX = 64
TOPK_MAX = 256
N_GROUPS = 8
EXPERTS_PER_GROUP = 8
N_EXPERTS = N_GROUPS * EXPERTS_PER_GROUP
RMS_EPS = 1e-6

LANES = 128
VMEM_LIMIT = 56 * 1024 * 1024

NEG_BIG = -1e30


def _cparams(sem):
    return pltpu.CompilerParams(dimension_semantics=sem, vmem_limit_bytes=VMEM_LIMIT)


def _dot(a, b):
    return jnp.dot(a, b, preferred_element_type=F32)


def _dot_nt(a, b):
    return lax.dot_general(a, b, (((1,), (1,)), ((), ())), preferred_element_type=F32)


def _rms(x, g):
    return x * lax.rsqrt(jnp.mean(x * x, axis=-1, keepdims=True) + RMS_EPS) * g


def _ada_kernel(c_ref, w_ref, b_ref, o_ref):
    c = c_ref[...]
    a = c * jax.nn.sigmoid(c)
    o_ref[...] = _dot(a.astype(MXU_DTYPE), w_ref[...].astype(MXU_DTYPE)) + b_ref[...]


def _ada(c8, w_ada, b_ada):
    d, n = w_ada.shape
    tn = 2048
    return pl.pallas_call(
        _ada_kernel,
        grid=(n // tn,),
        in_specs=[pl.BlockSpec((8, d), lambda j: (0, 0)),
                  pl.BlockSpec((d, tn), lambda j: (0, j)),
                  pl.BlockSpec((1, tn), lambda j: (0, j))],
        out_specs=pl.BlockSpec((8, tn), lambda j: (0, j)),
        out_shape=jax.ShapeDtypeStruct((8, n), F32),
        compiler_params=_cparams(("arbitrary",)),
        name="ada",
    )(c8, w_ada, b_ada)


NP_COLS = 76 * LANES
U_QA, U_QIDX, U_QB, U_KB, U_VB, U_CKV, U_MISC = 32, 40, 48, 56, 64, 72, 74
TN_PROJ = 4 * LANES
TM_PROJ = 2048
TX_PROJ = 1024
MISC_TILE = (U_MISC * LANES) // TN_PROJ
MISC_OFF = U_MISC * LANES - MISC_TILE * TN_PROJ
MISC_K, MISC_F, MISC_W = 0, 64, 72


def _pack_moves():
    sizes = (1024, 256, 1024, 64, 16, 1024, 1024, 1024, 8, 4096)
    src = [0]
    for n in sizes:
        src.append(src[-1] + n)
    q_a, c_kv, q_idx, k_idx, w_idx, q_b, k_b, v_b, f_b, gate = src[:10]
    m = U_MISC * LANES
    return ((gate, 0, 4096), (q_a, U_QA * LANES, 1024), (q_idx, U_QIDX * LANES, 1024),
            (q_b, U_QB * LANES, 1024), (k_b, U_KB * LANES, 1024), (v_b, U_VB * LANES, 1024),
            (c_kv, U_CKV * LANES, 256), (f_b, m + MISC_F, 8), (w_idx, m + MISC_W, 16),
            (k_idx, m + MISC_K, 64))


def _pack_kernel(w_ref, o_ref):
    cols = o_ref.shape[1]
    m = U_MISC * LANES
    o_ref[m:m + 2 * LANES, :] = jnp.zeros((2 * LANES, cols), o_ref.dtype)
    for src, dst, n in _pack_moves():
        o_ref[dst:dst + n, :] = w_ref[src:src + n, :].astype(o_ref.dtype)


def _pack_w_in(w_in_t):
    n_in, d = w_in_t.shape
    tc = 256
    return pl.pallas_call(
        _pack_kernel,
        grid=(d // tc,),
        in_specs=[pl.BlockSpec((n_in, tc), lambda i: (0, i))],
        out_specs=pl.BlockSpec((NP_COLS, tc), lambda i: (0, i)),
        out_shape=jax.ShapeDtypeStruct((NP_COLS, d), MXU_DTYPE),
        compiler_params=_cparams(("parallel",)),
        name="pack",
    )(w_in_t)


def _inproj_kernel(x_ref, mod_ref, g_ref, w_ref, o_ref, misc_ref, h_scr, *, n_x):
    j = pl.program_id(1)
    tx = x_ref.shape[0]
    for part in range(n_x):
        @pl.when(j == part)
        def _():
            y = _rms(x_ref[...], g_ref[...])
            sh = mod_ref[0, 0:1, :]
            sc = mod_ref[0, 1:2, :]
            h_scr[part * tx:(part + 1) * tx, :] = (y * (1.0 + sc) + sh).astype(h_scr.dtype)

    @pl.when(j >= n_x)
    def _():
        acc = _dot_nt(h_scr[...], w_ref[...])
        o_ref[...] = acc.astype(o_ref.dtype)

        @pl.when(j == n_x + MISC_TILE)
        def _():
            misc_ref[...] = acc[:, MISC_OFF:MISC_OFF + LANES]


def _inproj(x2, mod3, g_mix, w_packed, seq):
    t, d = x2.shape
    tm = min(TM_PROJ, seq)
    tx = min(TX_PROJ, tm)
    n_x = tm // tx
    per_b = seq // tm
    wtile = lambda j: jnp.maximum(j - n_x, 0)
    return pl.pallas_call(
        functools.partial(_inproj_kernel, n_x=n_x),
        grid=(t // tm, n_x + NP_COLS // TN_PROJ),
        in_specs=[pl.BlockSpec((tx, d), lambda i, j: (i * n_x + jnp.minimum(j, n_x - 1), 0)),
                  pl.BlockSpec((1, 6, d), lambda i, j: (i // per_b, 0, 0)),
                  pl.BlockSpec((1, d), lambda i, j: (0, 0)),
                  pl.BlockSpec((TN_PROJ, d), lambda i, j: (wtile(j), 0))],
        out_specs=[pl.BlockSpec((tm, TN_PROJ), lambda i, j: (i, wtile(j))),
                   pl.BlockSpec((tm, LANES), lambda i, j: (i, 0))],
        out_shape=[jax.ShapeDtypeStruct((t, NP_COLS), MXU_DTYPE),
                   jax.ShapeDtypeStruct((t, LANES), F32)],
        scratch_shapes=[pltpu.VMEM((tm, d), MXU_DTYPE)],
        compiler_params=_cparams(("parallel", "arbitrary")),
        name="inproj",
    )(x2, mod3, g_mix, w_packed)


TK = 256


def _prep_kernel(ckv_ref, misc_ref, gkv_ref, bf_ref, kv_ref, cum_ref):
    s = ckv_ref.shape[1]
    kv_ref[0] = _rms(ckv_ref[0].astype(F32), gkv_ref[...]).astype(kv_ref.dtype)

    r = lax.broadcasted_iota(I32, (LANES, LANES), 0)
    c = lax.broadcasted_iota(I32, (LANES, LANES), 1)
    tri = jnp.where(c <= r, 1.0, 0.0).astype(MXU_DTYPE)
    carry = jnp.zeros((1, LANES), F32)
    for blk in range(s // LANES):
        z = misc_ref[0, blk * LANES:(blk + 1) * LANES, :] + bf_ref[...]
        ls = jnp.minimum(z, 0.0) - jnp.log1p(jnp.exp(-jnp.abs(z)))
        p1 = ls.astype(MXU_DTYPE)
        r1 = ls - p1.astype(F32)
        p2 = r1.astype(MXU_DTYPE)
        p3 = (r1 - p2.astype(F32)).astype(MXU_DTYPE)
        cs = _dot(tri, p1) + _dot(tri, p2) + _dot(tri, p3) + carry
        carry = cs[LANES - 1:LANES, :]
        cum_ref[0, blk * LANES:(blk + 1) * LANES, :] = cs * (HEAD_DIM ** 0.5)


def _prep(proj3, misc3, g_kv, bf_row):
    b, s, _ = proj3.shape
    return pl.pallas_call(
        _prep_kernel,
        grid=(b,),
        in_specs=[pl.BlockSpec((1, s, D_LAT), lambda i: (i, 0, U_CKV * LANES // D_LAT)),
                  pl.BlockSpec((1, s, LANES), lambda i: (i, 0, 0)),
                  pl.BlockSpec((1, D_LAT), lambda i: (0, 0)),
                  pl.BlockSpec((1, LANES), lambda i: (0, 0))],
        out_specs=[pl.BlockSpec((1, s, D_LAT), lambda i: (i, 0, 0)),
                   pl.BlockSpec((1, s, LANES), lambda i: (i, 0, 0))],
        out_shape=[jax.ShapeDtypeStruct((b, s, D_LAT), MXU_DTYPE),
                   jax.ShapeDtypeStruct((b, s, LANES), F32)],
        compiler_params=_cparams(("parallel",)),
        name="prep",
    )(proj3, misc3, g_kv, bf_row)


TQ_A = 256
N_BISECT = 32
LOG2E = 1.4426950408889634
assert TQ_A == TK


def _pair_loop(n, body):
    def pair(j, carry):
        body(2 * j, 2)
        return carry

    lax.fori_loop(0, jnp.right_shift(n, 1), pair, 0)

    @pl.when(jnp.bitwise_and(n, 1) == 1)
    def _():
        body(n - 1, 1)


def _tile_start(kt):
    return kt * TK if isinstance(kt, int) else pl.multiple_of(kt * TK, TK)


def _static_pairs(idx, n_cases, count, body):
    for c in range(n_cases):
        @pl.when(idx == c)
        def _():
            n = count(c)
            for kt in range(0, n, 2):
                body(kt, min(2, n - kt))


def _dsa_kernel(qa_ref, qidx_ref, miscq_ref, misck_ref, kv_ref, wuk_ref, wuv_ref, gqa_ref, o_ref,
                sc_scr, bias_scr, logit_scr, qlat_scr, qh_scr, mm_scr, thr_scr, mpart_scr, lpart_scr, oacc_scr,
                *, topk, n_heads):
    i = pl.program_id(1)
    tq = TQ_A
    nk = i + 1
    q0 = i * tq

    wt = miscq_ref[0].T
    wq = wt[MISC_W:MISC_W + N_IDX_HEADS, :] * (D_IDX ** -0.5 * N_IDX_HEADS ** -0.5)
    for h in range(N_IDX_HEADS):
        qh_scr[h] = qidx_ref[0, :, h * D_IDX:(h + 1) * D_IDX]

    def fold(v, op, rows):
        return op(v.reshape(v.shape[0] // rows, rows, tq), axis=0)

    mm_scr[0:8, :] = jnp.full((8, tq), NEG_BIG, F32)
    mm_scr[8:16, :] = jnp.full((8, tq), -NEG_BIG, F32)

    def score_tiles(kt, nt, last):
        mx8 = mm_scr[0:8, :]
        mn8 = mm_scr[8:16, :]
        for half in range(nt * (TK // LANES)):
            k0 = pl.multiple_of(kt * TK + half * LANES, LANES)
            kx = misck_ref[0, pl.ds(k0, LANES), :][:, MISC_K:MISC_K + D_IDX].astype(MXU_DTYPE)
            acc = jnp.zeros((LANES, tq), F32)
            for h in range(N_IDX_HEADS):
                d = _dot_nt(kx, qh_scr[h])
                acc = acc + jnp.maximum(d, 0.0) * wq[h:h + 1, :]
            if last:
                kpos = k0 + lax.broadcasted_iota(I32, (LANES, tq), 0)
                qpos = q0 + lax.broadcasted_iota(I32, (LANES, tq), 1)
                adm = (kpos // CHUNK) <= (qpos // CHUNK)
                sc_scr[pl.ds(k0, LANES), :] = jnp.where(adm, acc, NEG_BIG)
                mx8 = jnp.maximum(mx8, fold(jnp.where(adm, acc, NEG_BIG), jnp.max, 8))
                mn8 = jnp.minimum(mn8, fold(jnp.where(adm, acc, -NEG_BIG), jnp.min, 8))
            else:
                sc_scr[pl.ds(k0, LANES), :] = acc
                mx8 = jnp.maximum(mx8, fold(acc, jnp.max, 8))
                mn8 = jnp.minimum(mn8, fold(acc, jnp.min, 8))
        mm_scr[0:8, :] = mx8
        mm_scr[8:16, :] = mn8

    n_free = topk // TK

    @pl.when(i >= n_free)
    def _():
        _pair_loop(i, lambda kt, nt: score_tiles(kt, nt, False))
        score_tiles(i, 1, True)

    def bisect(n_tiles):
        def bis_body(it, carry):
            lo, hi = carry
            mid = lo + 0.5 * (hi - lo)
            c32 = jnp.zeros((32, tq), F32)
            for kt in range(n_tiles):
                c32 = c32 + fold(jnp.where(sc_scr[kt * TK:(kt + 1) * TK, :] >= mid, 1.0, 0.0), jnp.sum, 32)
            ok = jnp.sum(c32, axis=0, keepdims=True) >= topk
            return jnp.where(ok, mid, lo), jnp.where(ok, hi, mid)

        lo, _ = lax.fori_loop(0, N_BISECT, bis_body, (jnp.min(mm_scr[8:16, :], axis=0, keepdims=True),
                                                      jnp.max(mm_scr[0:8, :], axis=0, keepdims=True)))
        thr_scr[...] = lo

    for c in range(sc_scr.shape[0] // TK):
        @pl.when(i == c)
        def _():
            if c >= n_free:
                bisect(c + 1)
            else:
                for kt in range(c):
                    bias_scr[kt] = jnp.zeros((tq, TK), F32)
                kchunk = lax.broadcasted_iota(I32, (tq, TK), 1) // CHUNK
                qchunk = lax.broadcasted_iota(I32, (tq, TK), 0) // CHUNK
                bias_scr[c] = jnp.where(kchunk <= qchunk, 0.0, NEG_BIG)

    @pl.when(i >= n_free)
    def _():
        thr = thr_scr[...]

        def bias_body(kt, carry):
            k0 = _tile_start(kt)
            sel_t = sc_scr[pl.ds(k0, TK), :] >= thr
            bias_scr[kt] = jnp.where(sel_t, 0.0, NEG_BIG).T
            return carry

        lax.fori_loop(0, nk, bias_body, 0)

    for h in range(n_heads):
        ql = _dot(qa_ref[0, :, h * HEAD_DIM:(h + 1) * HEAD_DIM], wuk_ref[h])
        ql = _rms(ql, gqa_ref[...]) * (D_LAT ** -0.5 * LOG2E)
        qlat_scr[h * tq:(h + 1) * tq, :] = ql.astype(qlat_scr.dtype)
    mpart_scr[...] = jnp.full(mpart_scr.shape, NEG_BIG, F32)
    lpart_scr[...] = jnp.zeros(lpart_scr.shape, F32)
    oacc_scr[...] = jnp.zeros(oacc_scr.shape, F32)
    slopes2 = [2.0 ** (-8.0 * (h + 1) / n_heads) * LOG2E for h in range(n_heads)]

    def logit_tiles(kt, nt, last):
        k0 = _tile_start(kt)
        kvt = kv_ref[0, pl.ds(k0, nt * TK), :]
        kcol = (k0 + lax.broadcasted_iota(I32, (1, nt * TK), 1)).astype(F32)
        if last:
            ahead = jnp.maximum(lax.broadcasted_iota(I32, (tq, TK), 1)
                                - lax.broadcasted_iota(I32, (tq, TK), 0), 0).astype(F32)
        for h in range(n_heads):
            rows = slice(h * tq, (h + 1) * tq)
            lg = _dot_nt(qlat_scr[rows, :], kvt) + slopes2[h] * kcol
            for t in range(nt):
                lt = lg[:, t * TK:(t + 1) * TK] + bias_scr[kt + t]
                if last:
                    lt = lt - (2.0 * slopes2[h]) * ahead
                logit_scr[kt + t, rows, :] = lt
                mpart_scr[rows, :] = jnp.maximum(mpart_scr[rows, :],
                                                 jnp.maximum(lt[:, :LANES], lt[:, LANES:]))

    _pair_loop(i, lambda kt, nt: logit_tiles(kt, nt, False))
    logit_tiles(i, 1, True)

    m = jnp.max(mpart_scr[...], axis=-1, keepdims=True)
    mpart_scr[...] = jnp.broadcast_to(m, mpart_scr.shape)

    def pv_tiles(kt, nt):
        k0 = _tile_start(kt)
        kvt = kv_ref[0, pl.ds(k0, nt * TK), :]
        for h in range(n_heads):
            rows = slice(h * tq, (h + 1) * tq)
            mb = mpart_scr[rows, :]
            mb2 = jnp.concatenate([mb, mb], axis=1)
            ps = [jnp.exp2(logit_scr[kt + t, rows, :] - mb2) for t in range(nt)]
            lsum = ps[0][:, :LANES] + ps[0][:, LANES:]
            for p in ps[1:]:
                lsum = lsum + p[:, :LANES] + p[:, LANES:]
            lpart_scr[rows, :] += lsum
            p_all = ps[0] if nt == 1 else jnp.concatenate(ps, axis=1)
            oacc_scr[rows, :] += _dot(p_all.astype(MXU_DTYPE), kvt)

    _pair_loop(nk, pv_tiles)

    for h in range(n_heads):
        rows = slice(h * tq, (h + 1) * tq)
        lsum = jnp.sum(lpart_scr[rows, :], axis=-1, keepdims=True)
        o_lat = oacc_scr[rows, :] / lsum
        o_ref[0, :, h * HEAD_DIM:(h + 1) * HEAD_DIM] = _dot(
            o_lat.astype(MXU_DTYPE), wuv_ref[h]).astype(o_ref.dtype)


def _dsa(proj3, misc3, kv, w_uk, w_uv, g_qa):
    b, s, _ = proj3.shape
    n_heads = w_uk.shape[0]
    width = n_heads * HEAD_DIM
    topk = min(TOPK_MAX, s // 4)
    tq = TQ_A
    nkt = s // TK
    kern = functools.partial(_dsa_kernel, topk=topk, n_heads=n_heads)
    return pl.pallas_call(
        kern,
        grid=(b, s // tq),
        in_specs=[pl.BlockSpec((1, tq, width), lambda bb, i: (bb, i, U_QA * LANES // width)),
                  pl.BlockSpec((1, tq, N_IDX_HEADS * D_IDX),
                               lambda bb, i: (bb, i, U_QIDX * LANES // (N_IDX_HEADS * D_IDX))),
                  pl.BlockSpec((1, tq, LANES), lambda bb, i: (bb, i, 0)),
                  pl.BlockSpec((1, s, LANES), lambda bb, i: (bb, 0, 0)),
                  pl.BlockSpec((1, s, D_LAT), lambda bb, i: (bb, 0, 0)),
                  pl.BlockSpec((n_heads, HEAD_DIM, D_LAT), lambda bb, i: (0, 0, 0)),
                  pl.BlockSpec((n_heads, D_LAT, HEAD_DIM), lambda bb, i: (0, 0, 0)),
                  pl.BlockSpec((1, D_LAT), lambda bb, i: (0, 0))],
        out_specs=pl.BlockSpec((1, tq, width), lambda bb, i: (bb, i, 0)),
        out_shape=jax.ShapeDtypeStruct((b, s, width), MXU_DTYPE),
        scratch_shapes=[pltpu.VMEM((s, tq), F32),
                        pltpu.VMEM((nkt, tq, TK), F32),
                        pltpu.VMEM((nkt, n_heads * tq, TK), F32),
                        pltpu.VMEM((n_heads * tq, D_LAT), MXU_DTYPE),
                        pltpu.VMEM((N_IDX_HEADS, tq, D_IDX), MXU_DTYPE),
                        pltpu.VMEM((16, tq), F32),
                        pltpu.VMEM((1, tq), F32),
                        pltpu.VMEM((n_heads * tq, LANES), F32),
                        pltpu.VMEM((n_heads * tq, LANES), F32),
                        pltpu.VMEM((n_heads * tq, D_LAT), F32)],
        compiler_params=_cparams(("parallel", "arbitrary")),
        name="dsa",
    )(proj3, proj3, misc3, misc3, kv, w_uk, w_uv, g_qa)


TQ_B = 256


HG_B = 8
assert TQ_B == TK


def _split3(x):
    p1 = x.astype(MXU_DTYPE)
    r1 = x - p1.astype(F32)
    p2 = r1.astype(MXU_DTYPE)
    p3 = (r1 - p2.astype(F32)).astype(MXU_DTYPE)
    return p1, p2, p3


def _fox_kernel(q_ref, k_ref, v_ref, cumq_ref, cumk_ref, gq_ref, gk_ref, o_ref,
                kaug_scr, qaug_scr, logit_scr, mpart_scr, lpart_scr, oacc_scr):
    g = pl.program_id(1)
    qi = pl.program_id(2)
    tq = TQ_B
    s = k_ref.shape[1]
    cscale = (HEAD_DIM ** -0.5) * LOG2E

    rr = lax.broadcasted_iota(I32, (3 * LANES, LANES), 0)
    cc = lax.broadcasted_iota(I32, (3 * LANES, LANES), 1)
    lane_row = lax.broadcasted_iota(I32, (1, LANES), 1)
    ones_q = jnp.where((lane_row >= 3) & (lane_row < 6), 1.0, 0.0)
    ones_k = jnp.where(lane_row < 3, 1.0, 0.0)
    base_q = jnp.where(cc < 3, rr - LANES * cc, -1)
    base_k = jnp.where((cc >= 3) & (cc < 6), rr - LANES * (cc - 3), -1)

    def aug(pieces, h, base, sign, ones_row):
        e = jnp.where(base == h, sign, 0.0).astype(MXU_DTYPE)
        return (_dot(pieces, e) + ones_row).astype(MXU_DTYPE)

    @pl.when(qi == 0)
    def _():
        def kbody(c, carry):
            r0 = pl.multiple_of(c * TK, TK)
            ck = jnp.concatenate(_split3(cumk_ref[0, pl.ds(r0, TK), :]), axis=1)
            for hh in range(HG_B):
                kn = _rms(k_ref[0, pl.ds(r0, TK), hh * HEAD_DIM:(hh + 1) * HEAD_DIM].astype(F32), gk_ref[...])
                kaug_scr[hh, pl.ds(r0, TK), 0:HEAD_DIM] = kn.astype(MXU_DTYPE)
                kaug_scr[hh, pl.ds(r0, TK), HEAD_DIM:] = aug(ck, MISC_F + g * HG_B + hh, base_k, -1.0, ones_k)
            return carry

        lax.fori_loop(0, s // TK, kbody, 0)

    cq = jnp.concatenate(_split3(cumq_ref[0]), axis=1)
    for hh in range(HG_B):
        qn = _rms(q_ref[0, :, hh * HEAD_DIM:(hh + 1) * HEAD_DIM].astype(F32), gq_ref[...])
        qaug_scr[hh, :, 0:HEAD_DIM] = qn.astype(MXU_DTYPE)
        qaug_scr[hh, :, HEAD_DIM:] = aug(cq, MISC_F + g * HG_B + hh, base_q, 1.0, ones_q)
    mpart_scr[...] = jnp.full(mpart_scr.shape, NEG_BIG, F32)
    lpart_scr[...] = jnp.zeros(lpart_scr.shape, F32)
    oacc_scr[...] = jnp.zeros(oacc_scr.shape, F32)

    def put_logits(kt, hh, lg):
        logit_scr[kt, hh] = lg
        mpart_scr[hh] = jnp.maximum(mpart_scr[hh], jnp.maximum(lg[:, :LANES], lg[:, LANES:]))

    def raw_logits(kt, nt, hh):
        k0 = _tile_start(kt)
        return _dot_nt(qaug_scr[hh], kaug_scr[hh, pl.ds(k0, nt * TK), :]) * cscale

    def off_tiles(kt, nt):
        for hh in range(HG_B):
            lg = raw_logits(kt, nt, hh)
            for t in range(nt):
                put_logits(kt + t, hh, lg[:, t * TK:(t + 1) * TK])

    _static_pairs(qi, s // TK, lambda c: c, off_tiles)
    causal = (lax.broadcasted_iota(I32, (tq, TK), 1) <= lax.broadcasted_iota(I32, (tq, TK), 0))
    for hh in range(HG_B):
        put_logits(qi, hh, jnp.where(causal, raw_logits(qi, 1, hh), NEG_BIG))

    for hh in range(HG_B):
        m = jnp.max(mpart_scr[hh], axis=-1, keepdims=True)
        mpart_scr[hh] = jnp.broadcast_to(m, (tq, LANES))

    def pv_tiles(kt, nt):
        k0 = _tile_start(kt)
        for hh in range(HG_B):
            mb = mpart_scr[hh]
            mb2 = jnp.concatenate([mb, mb], axis=1)
            ps = [jnp.exp2(logit_scr[kt + t, hh] - mb2) for t in range(nt)]
            lsum = ps[0][:, :LANES] + ps[0][:, LANES:]
            for p in ps[1:]:
                lsum = lsum + p[:, :LANES] + p[:, LANES:]
            lpart_scr[hh] += lsum
            p_all = ps[0] if nt == 1 else jnp.concatenate(ps, axis=1)
            oacc_scr[hh] += _dot(p_all.astype(MXU_DTYPE),
                                 v_ref[0, pl.ds(k0, nt * TK), hh * HEAD_DIM:(hh + 1) * HEAD_DIM])

    _static_pairs(qi, s // TK, lambda c: c + 1, pv_tiles)
    for hh in range(HG_B):
        lsum = jnp.sum(lpart_scr[hh], axis=-1, keepdims=True)
        o_ref[0, :, hh * HEAD_DIM:(hh + 1) * HEAD_DIM] = (oacc_scr[hh] / lsum).astype(o_ref.dtype)


def _fox(proj3, cums, g_qb, g_kb, n_heads):
    b, s, _ = proj3.shape
    tq = TQ_B
    gw = HG_B * HEAD_DIM
    return pl.pallas_call(
        _fox_kernel,
        grid=(b, n_heads // HG_B, s // tq),
        in_specs=[pl.BlockSpec((1, tq, gw), lambda bb, g, i: (bb, i, U_QB // HG_B + g)),
                  pl.BlockSpec((1, s, gw), lambda bb, g, i: (bb, 0, U_KB // HG_B + g)),
                  pl.BlockSpec((1, s, gw), lambda bb, g, i: (bb, 0, U_VB // HG_B + g)),
                  pl.BlockSpec((1, tq, LANES), lambda bb, g, i: (bb, i, 0)),
                  pl.BlockSpec((1, s, LANES), lambda bb, g, i: (bb, 0, 0)),
                  pl.BlockSpec((1, HEAD_DIM), lambda bb, g, i: (0, 0)),
                  pl.BlockSpec((1, HEAD_DIM), lambda bb, g, i: (0, 0))],
        out_specs=pl.BlockSpec((1, tq, gw), lambda bb, g, i: (bb, i, g)),
        out_shape=jax.ShapeDtypeStruct((b, s, n_heads * HEAD_DIM), MXU_DTYPE),
        scratch_shapes=[pltpu.VMEM((HG_B, s, 2 * HEAD_DIM), MXU_DTYPE),
                        pltpu.VMEM((HG_B, tq, 2 * HEAD_DIM), MXU_DTYPE),
                        pltpu.VMEM((s // TK, HG_B, tq, TK), F32),
                        pltpu.VMEM((HG_B, tq, LANES), F32),
                        pltpu.VMEM((HG_B, tq, LANES), F32),
                        pltpu.VMEM((HG_B, tq, HEAD_DIM), F32)],
        compiler_params=_cparams(("parallel", "parallel", "arbitrary")),
        name="fox",
    )(proj3, proj3, proj3, cums, cums, g_qb, g_kb)


TM_MERGE = 256


def _merge_kernel(ga_ref, gb_ref, oa_ref, ob_ref, x_ref, mod_ref, bga_ref, bgb_ref, wpa_ref, wpb_ref,
                  wo_ref, gffn_ref, wrh_ref, wrp_ref, br_ref, x1_ref, h2_ref, rl_ref):
    ga = jax.nn.sigmoid(ga_ref[...].astype(F32) + bga_ref[...])
    gb = jax.nn.sigmoid(gb_ref[...].astype(F32) + bgb_ref[...])
    merged = ga * _dot(oa_ref[...], wpa_ref[...]) + gb * _dot(ob_ref[...], wpb_ref[...])
    upd = _dot(merged.astype(MXU_DTYPE), wo_ref[...])
    x1 = x_ref[...] + mod_ref[0, 2:3, :] * upd
    x1_ref[...] = x1
    h2 = _rms(x1, gffn_ref[...]) * (1.0 + mod_ref[0, 4:5, :]) + mod_ref[0, 3:4, :]
    h2_ref[...] = h2
    hh = h2.astype(MXU_DTYPE)
    hl = (h2 - hh.astype(F32)).astype(MXU_DTYPE)
    both = _dot(hh, wrp_ref[...])
    rl_ref[...] = both[:, :LANES] + both[:, LANES:] + _dot(hl, wrh_ref[...]) + br_ref[...]


def _merge(proj, o_a, o_b, x2, mod3, b_gate, w_pa, w_pb, w_o, g_ffn, wr_hi, wr_pair, b_r, seq):
    t, d = x2.shape
    tm = TM_MERGE
    per_b = seq // tm
    wa = o_a.shape[1]
    res = lambda shape: pl.BlockSpec(shape, lambda i: (0,) * len(shape), pipeline_mode=pl.Buffered(1))
    return pl.pallas_call(
        _merge_kernel,
        grid=(t // tm,),
        in_specs=[pl.BlockSpec((tm, d), lambda i: (i, 0)),
                  pl.BlockSpec((tm, d), lambda i: (i, 1)),
                  pl.BlockSpec((tm, wa), lambda i: (i, 0)),
                  pl.BlockSpec((tm, wa), lambda i: (i, 0)),
                  pl.BlockSpec((tm, d), lambda i: (i, 0)),
                  pl.BlockSpec((1, 6, d), lambda i: (i // per_b, 0, 0)),
                  pl.BlockSpec((1, d), lambda i: (0, 0)),
                  pl.BlockSpec((1, d), lambda i: (0, 1)),
                  res((wa, d)), res((wa, d)), res((d, d)),
                  pl.BlockSpec((1, d), lambda i: (0, 0)),
                  res((d, LANES)), res((d, 2 * LANES)),
                  pl.BlockSpec((1, LANES), lambda i: (0, 0))],
        out_specs=[pl.BlockSpec((tm, d), lambda i: (i, 0)),
                   pl.BlockSpec((tm, d), lambda i: (i, 0)),
                   pl.BlockSpec((tm, LANES), lambda i: (i, 0))],
        out_shape=[jax.ShapeDtypeStruct((t, d), F32),
                   jax.ShapeDtypeStruct((t, d), F32),
                   jax.ShapeDtypeStruct((t, LANES), F32)],
        compiler_params=_cparams(("parallel",)),
        name="merge",
    )(proj, proj, o_a, o_b, x2, mod3, b_gate, b_gate, w_pa, w_pb, w_o, g_ffn, wr_hi, wr_pair, b_r)


TM_ROUTE = 1024
TM_ROWS = 256
MM_ROWS = 128
R_E0, R_E1, R_P0, R_P1 = 0, 1, 4, 5
IT_TILE, IT_E, IT_LO, IT_HI, IT_FLAG, IT_NEXT, IT_NEXT2, IT_SLOT = range(8)
F_VALID, F_FIRST_OF_EXPERT, F_FIRST_OF_TILE = 1, 2, 4


def _route_kernel(rl_ref, route_ref, post_ref, items_ref, cnt_scr, run_scr, offs_scr, *, n_tiles):
    tm = rl_ref.shape[0]
    sweep = pl.program_id(0)
    step = pl.program_id(1)
    iw = items_ref.shape[1]

    @pl.when((sweep == 0) & (step == 0))
    def _():
        cnt_scr[...] = jnp.zeros(cnt_scr.shape, F32)

    r = rl_ref[...]
    lane = lax.broadcasted_iota(I32, (tm, LANES), 1).astype(F32)
    neg_inf = -jnp.inf
    gmask = lane < N_GROUPS
    gl = jnp.where(gmask, r, neg_inf)
    gmax = jnp.max(gl, axis=-1, keepdims=True)
    gidx = jnp.min(jnp.where(gl == gmax, lane, float(LANES)), axis=-1, keepdims=True)
    gsum = jnp.sum(jnp.where(gmask, jnp.exp(r - gmax), 0.0), axis=-1, keepdims=True)
    gw = 1.0 / gsum
    lo = N_GROUPS + EXPERTS_PER_GROUP * gidx
    emask = (lane >= lo) & (lane < lo + EXPERTS_PER_GROUP)
    el = jnp.where(emask, r, neg_inf)
    v0 = jnp.max(el, axis=-1, keepdims=True)
    i0 = jnp.min(jnp.where(el == v0, lane, float(LANES)), axis=-1, keepdims=True)
    el2 = jnp.where(lane == i0, neg_inf, el)
    v1 = jnp.max(el2, axis=-1, keepdims=True)
    i1 = jnp.min(jnp.where(el2 == v1, lane, float(LANES)), axis=-1, keepdims=True)
    tt = jnp.exp(v1 - v0)
    p0 = gw / (1.0 + tt)
    p1 = gw * tt / (1.0 + tt)
    e0 = i0 - N_GROUPS
    e1 = i1 - N_GROUPS

    hit0 = lane == e0
    hit1 = lane == e1
    oh = jnp.where(hit0 | hit1, 1.0, 0.0)

    @pl.when(sweep == 0)
    def _():
        ones = jnp.ones((tm, LANES), MXU_DTYPE)
        cnt_scr[...] += _dot(oh.T.astype(MXU_DTYPE), ones)

    @pl.when((sweep == 1) & (step == 0))
    def _():
        _plan_items(cnt_scr[...], items_ref, offs_scr, n_tiles, iw)
        run_scr[...] = jnp.zeros(run_scr.shape, F32)

    @pl.when(sweep == 1)
    def _():
        rr = lax.broadcasted_iota(I32, (tm, tm), 0)
        cc = lax.broadcasted_iota(I32, (tm, tm), 1)
        ltri = jnp.where(cc < rr, 1.0, 0.0).astype(MXU_DTYPE)
        before = _dot(ltri, oh.astype(MXU_DTYPE)) + run_scr[...] + offs_scr[...]
        pos0 = jnp.sum(jnp.where(hit0, before, 0.0), axis=-1, keepdims=True)
        pos1 = jnp.sum(jnp.where(hit1, before, 0.0), axis=-1, keepdims=True)
        run_scr[...] = run_scr[...] + jnp.sum(oh, axis=0, keepdims=True)
        out = jnp.zeros((tm, LANES), F32)
        for k, val in ((R_E0, e0), (R_E1, e1), (R_P0, p0), (R_P1, p1)):
            out = jnp.where(lane == k, val, out)
        route_ref[...] = out
        pmat = jnp.where(lane == 0.0, pos0, jnp.where(lane == 1.0, pos1, 0.0))
        post_ref[...] = pmat.T[0:8, :].astype(I32)


def _plan_items(cnt_col, items_ref, offs_scr, n_tiles, iw):
    tmr = float(TM_ROWS)
    sub = lax.broadcasted_iota(I32, (LANES, LANES), 0)
    lan = lax.broadcasted_iota(I32, (LANES, LANES), 1)
    lstrict = jnp.where(lan < sub, 1.0, 0.0).astype(MXU_DTYPE)
    hi = jnp.floor(cnt_col * (1.0 / LANES))
    lo = cnt_col - hi * LANES
    offs_col = _dot(lstrict, hi.astype(MXU_DTYPE)) * LANES + _dot(lstrict, lo.astype(MXU_DTYPE))
    first_t = jnp.floor(offs_col * (1.0 / tmr))
    last_t = jnp.floor((offs_col + cnt_col - 1.0) * (1.0 / tmr))
    n_col = jnp.where(cnt_col > 0.0, last_t - first_t + 1.0, 0.0)
    base_col = _dot(lstrict, n_col.astype(MXU_DTYPE))
    end_col = base_col + n_col
    offs_scr[...] = offs_col.T[0:1, :]

    rep = lambda col: jnp.concatenate([col] * (iw // LANES), axis=1)
    wl = lax.broadcasted_iota(I32, (LANES, iw), 1).astype(F32)
    sub_f = lax.broadcasted_iota(I32, (LANES, iw), 0).astype(F32)
    end_r = rep(end_col)
    w_total = end_r[LANES - 1:LANES, :]
    count_le = lambda v: jnp.sum(jnp.where(end_r <= v, 1.0, 0.0), axis=0, keepdims=True)
    w_row = wl[0:1, :]
    eidx = count_le(w_row)
    sel = sub_f == eidx
    pick = lambda col: jnp.sum(jnp.where(sel, rep(col), 0.0), axis=0, keepdims=True)
    e_base, e_first, e_offs, e_cnt, e_end = (pick(base_col), pick(first_t), pick(offs_col),
                                             pick(cnt_col), pick(end_col))
    valid = w_row < w_total
    tile = e_first + (w_row - e_base)
    row_lo = jnp.maximum(e_offs, tile * tmr) - tile * tmr
    row_hi = jnp.minimum(e_offs + e_cnt, (tile + 1.0) * tmr) - tile * tmr
    first_e = w_row == e_base
    flags = jnp.where(valid, F_VALID + jnp.where(first_e, float(F_FIRST_OF_EXPERT), 0.0)
                      + jnp.where(row_lo == 0.0, float(F_FIRST_OF_TILE), 0.0), 0.0)
    has1 = valid & first_e & (e_end < w_total)
    e1 = count_le(e_end)
    end1 = jnp.sum(jnp.where(sub_f == e1, end_r, 0.0), axis=0, keepdims=True)
    has2 = has1 & (end1 < w_total)
    nxt = jnp.where(has1, e1, -1.0)
    nxt2 = jnp.where(has2, count_le(end1), -1.0)
    ord_col = _dot(lstrict, jnp.where(cnt_col > 0.0, 1.0, 0.0).astype(MXU_DTYPE))
    e_ord = pick(ord_col)
    slot = e_ord - 2.0 * jnp.floor(e_ord * 0.5)
    e_last = count_le(w_total - 1.0)
    rows = {IT_TILE: jnp.where(valid, tile, n_tiles - 1.0),
            IT_E: jnp.where(valid, eidx, e_last),
            IT_LO: jnp.where(valid, row_lo, 0.0),
            IT_HI: jnp.where(valid, row_hi, 0.0),
            IT_FLAG: flags,
            IT_NEXT: nxt,
            IT_NEXT2: nxt2,
            IT_SLOT: jnp.where(valid, slot, 0.0)}
    sub8 = lax.broadcasted_iota(I32, (8, iw), 0)
    out = jnp.zeros((8, iw), F32)
    for k, val in rows.items():
        out = jnp.where(sub8 == k, val, out)
    items_ref[...] = out.astype(I32)


def _route(rlog, n_items):
    t = rlog.shape[0]
    tm = min(TM_ROUTE, t)
    iw = ((n_items + LANES - 1) // LANES) * LANES
    kern = functools.partial(_route_kernel, n_tiles=2 * t // TM_ROWS)
    return pl.pallas_call(
        kern,
        grid=(2, t // tm),
        in_specs=[pl.BlockSpec((tm, LANES), lambda p, i: (i, 0))],
        out_specs=[pl.BlockSpec((tm, LANES), lambda p, i: (i * p, 0)),
                   pl.BlockSpec((8, tm), lambda p, i: (0, i * p)),
                   pl.BlockSpec((8, iw), lambda p, i: (0, 0))],
        out_shape=[jax.ShapeDtypeStruct((t, LANES), F32),
                   jax.ShapeDtypeStruct((8, t), I32),
                   jax.ShapeDtypeStruct((8, iw), I32)],
        scratch_shapes=[pltpu.VMEM((LANES, LANES), F32),
                        pltpu.VMEM((1, LANES), F32),
                        pltpu.VMEM((1, LANES), F32)],
        compiler_params=_cparams(("arbitrary", "arbitrary")),
        name="route",
    )(rlog)


TM_DISP = 1024


def _dispatch_kernel(pos0_ref, pos1_ref, h2_ref, xs_ref, sem):
    tm = h2_ref.shape[0]
    base = pl.program_id(0) * tm

    def row_copy(r, p):
        return pltpu.make_async_copy(h2_ref.at[pl.ds(r, 1), :], xs_ref.at[pl.ds(p, 1), :], sem)

    def issue(r, carry):
        row_copy(r, pos0_ref[base + r]).start(priority=0)
        row_copy(r, pos1_ref[base + r]).start(priority=1)
        return carry

    lax.fori_loop(0, tm, issue, 0, unroll=8)

    def drain(r, carry):
        row_copy(0, 0).wait()
        row_copy(0, 0).wait()
        return carry

    lax.fori_loop(0, tm, drain, 0, unroll=8)


def _dispatch(pos0, pos1, h2):
    t, d = h2.shape
    tm = min(TM_DISP, t)
    return pl.pallas_call(
        _dispatch_kernel,
        grid_spec=pltpu.PrefetchScalarGridSpec(
            num_scalar_prefetch=2,
            grid=(t // tm,),
            in_specs=[pl.BlockSpec((tm, d), lambda i, p0, p1: (i, 0))],
            out_specs=pl.BlockSpec(memory_space=pl.ANY),
            scratch_shapes=[pltpu.SemaphoreType.DMA(())]),
        out_shape=jax.ShapeDtypeStruct((2 * t, d), h2.dtype),
        compiler_params=pltpu.CompilerParams(dimension_semantics=("arbitrary",),
                                             vmem_limit_bytes=VMEM_LIMIT, has_side_effects=True),
        name="dispatch",
    )(pos0, pos1, h2)


WEIGHT_DMA_PRIORITY = 1


def _experts_kernel(tile_ref, e_ref, lo_ref, hi_ref, flag_ref, next_ref, next2_ref, slot_ref,
                    xs_ref, w1_hbm, w3_hbm, w2_hbm, ys_ref,
                    f1_scr, f3_scr, f2_scr, w1_scr, w3_scr, w2_scr, sems):
    del tile_ref
    w = pl.program_id(0)
    flag = flag_ref[w]

    mats = ((w1_hbm, f1_scr, w1_scr), (w3_hbm, f3_scr, w3_scr), (w2_hbm, f2_scr, w2_scr))

    def weight_copy(k, e, slot):
        return pltpu.make_async_copy(mats[k][0].at[e], mats[k][1].at[slot], sems.at[slot, k])

    @pl.when(w == 0)
    def _():
        for k in range(3):
            weight_copy(k, e_ref[0], 0).start(priority=WEIGHT_DMA_PRIORITY)
        nxt = next_ref[0]

        @pl.when(nxt >= 0)
        def _():
            for k in range(3):
                weight_copy(k, nxt, 1).start(priority=WEIGHT_DMA_PRIORITY)

    @pl.when((flag & F_FIRST_OF_EXPERT) != 0)
    def _():
        slot = slot_ref[w]
        nxt2 = next2_ref[w]
        for k in range(3):
            weight_copy(k, e_ref[w], slot).wait()
            mats[k][2][...] = mats[k][1][slot].astype(mats[k][2].dtype)

            @pl.when(nxt2 >= 0)
            def _():
                weight_copy(k, nxt2, slot).start(priority=WEIGHT_DMA_PRIORITY)

    @pl.when(((flag & F_VALID) != 0) & ((flag & F_FIRST_OF_TILE) != 0))
    def _():
        ys_ref[...] = jnp.zeros(ys_ref.shape, ys_ref.dtype)

    lo = lo_ref[w]
    hi = hi_ref[w]
    for part in range(TM_ROWS // MM_ROWS):
        r0 = part * MM_ROWS

        @pl.when(((flag & F_VALID) != 0) & (lo < r0 + MM_ROWS) & (hi > r0))
        def _():
            x = xs_ref[r0:r0 + MM_ROWS, :].astype(MXU_DTYPE)
            a = _dot(x, w1_scr[...])
            u = _dot(x, w3_scr[...])
            hm = (a * jax.nn.sigmoid(a)) * u
            res = _dot(hm.astype(MXU_DTYPE), w2_scr[...])
            row = r0 + lax.broadcasted_iota(I32, res.shape, 0)
            mine = (row >= lo) & (row < hi)
            ys_ref[r0:r0 + MM_ROWS, :] = jnp.where(mine, res, ys_ref[r0:r0 + MM_ROWS, :])


def _experts(items, xs, w1, w3, w2, n_items):
    n_rows, d = xs.shape
    f = w1.shape[2]
    tm = TM_ROWS
    tile_map = lambda w, tile, *_: (tile[w], 0)
    return pl.pallas_call(
        _experts_kernel,
        grid_spec=pltpu.PrefetchScalarGridSpec(
            num_scalar_prefetch=8,
            grid=(n_items,),
            in_specs=[pl.BlockSpec((tm, d), tile_map),
                      pl.BlockSpec(memory_space=pl.ANY),
                      pl.BlockSpec(memory_space=pl.ANY),
                      pl.BlockSpec(memory_space=pl.ANY)],
            out_specs=pl.BlockSpec((tm, d), tile_map),
            scratch_shapes=[pltpu.VMEM((2, d, f), F32),
                            pltpu.VMEM((2, d, f), F32),
                            pltpu.VMEM((2, f, d), F32),
                            pltpu.VMEM((d, f), MXU_DTYPE),
                            pltpu.VMEM((d, f), MXU_DTYPE),
                            pltpu.VMEM((f, d), MXU_DTYPE),
                            pltpu.SemaphoreType.DMA((2, 3))]),
        out_shape=jax.ShapeDtypeStruct((n_rows, d), F32),
        compiler_params=_cparams(("arbitrary",)),
        name="experts",
    )(items[IT_TILE], items[IT_E], items[IT_LO], items[IT_HI], items[IT_FLAG], items[IT_NEXT],
      items[IT_NEXT2], items[IT_SLOT], xs, w1, w3, w2)


TM_COMB = 256


def _combine_kernel(pos0_ref, pos1_ref, ys_ref, x1_ref, route_ref, mod_ref, o_ref, y0_scr, y1_scr, sems):
    step = pl.program_id(0)
    n_steps = pl.num_programs(0)

    def row_copy(p, dst, slot, r):
        return pltpu.make_async_copy(ys_ref.at[pl.ds(p, 1), :], dst.at[slot, pl.ds(r, 1), :], sems.at[slot])

    def issue_step(s):
        slot = s % 2
        base = s * TM_COMB

        def issue(r, carry):
            row_copy(pos0_ref[base + r], y0_scr, slot, r).start(priority=0)
            row_copy(pos1_ref[base + r], y1_scr, slot, r).start(priority=1)
            return carry

        lax.fori_loop(0, TM_COMB, issue, 0, unroll=8)

    @pl.when(step == 0)
    def _():
        issue_step(step)

    @pl.when(step + 1 < n_steps)
    def _():
        issue_step(step + 1)

    slot = step % 2

    def drain(r, carry):
        row_copy(0, y0_scr, slot, 0).wait()
        row_copy(0, y1_scr, slot, 0).wait()
        return carry

    lax.fori_loop(0, TM_COMB, drain, 0, unroll=8)

    rt = route_ref[...]
    lane = lax.broadcasted_iota(I32, rt.shape, 1)
    p0 = jnp.sum(jnp.where(lane == R_P0, rt, 0.0), axis=-1, keepdims=True)
    p1 = jnp.sum(jnp.where(lane == R_P1, rt, 0.0), axis=-1, keepdims=True)
    y = p0 * y0_scr[slot] + p1 * y1_scr[slot]
    o_ref[...] = x1_ref[...] + mod_ref[0, 5:6, :] * y


def _combine(pos0, pos1, ys, x1, route, mod3, seq):
    t, d = x1.shape
    tm = TM_COMB
    per_b = seq // tm
    return pl.pallas_call(
        _combine_kernel,
        grid_spec=pltpu.PrefetchScalarGridSpec(
            num_scalar_prefetch=2,
            grid=(t // tm,),
            in_specs=[pl.BlockSpec(memory_space=pl.ANY),
                      pl.BlockSpec((tm, d), lambda i, p0, p1: (i, 0)),
                      pl.BlockSpec((tm, LANES), lambda i, p0, p1: (i, 0)),
                      pl.BlockSpec((1, 6, d), lambda i, p0, p1: (i // per_b, 0, 0))],
            out_specs=pl.BlockSpec((tm, d), lambda i, p0, p1: (i, 0)),
            scratch_shapes=[pltpu.VMEM((2, tm, d), F32),
                            pltpu.VMEM((2, tm, d), F32),
                            pltpu.SemaphoreType.DMA((2,))]),
        out_shape=jax.ShapeDtypeStruct((t, d), F32),
        compiler_params=_cparams(("arbitrary",)),
        name="combine",
    )(pos0, pos1, ys, x1, route, mod3)


def kernel(x, c, w_ada, b_ada, g_mix, w_in, b_gate, b_forget, w_uk, w_uv, g_qa, g_kv, g_qb, g_kb,
           w_pa, w_pb, w_o, g_ffn, w_rg, b_rg, w_re, b_re, w1, w3, w2):
    b, s, d = x.shape
    depth = w_ada.shape[0]
    t = b * s
    n_heads_b = b_forget.shape[1]
    n_items = 2 * t // TM_ROWS + N_EXPERTS
    c8 = jnp.zeros((8, d), F32).at[:b].set(c)
    x2 = x.reshape(t, d)
    row = lambda v: v.reshape(1, -1)
    for l in range(depth):
        mod3 = _ada(c8, w_ada[l], row(b_ada[l]))[:b].reshape(b, 6, d)
        proj, misc = _inproj(x2, mod3, row(g_mix[l]), _pack_w_in(jnp.transpose(w_in[l])), s)
        proj3 = proj.reshape(b, s, NP_COLS)
        misc3 = misc.reshape(b, s, LANES)
        bf_row = jnp.zeros((1, LANES), F32).at[0, MISC_F:MISC_F + n_heads_b].set(b_forget[l])
        kv, cums = _prep(proj3, misc3, row(g_kv[l]), bf_row)
        o_a = _dsa(proj3, misc3, kv, w_uk[l].astype(MXU_DTYPE), w_uv[l].astype(MXU_DTYPE), row(g_qa[l]))
        o_b = _fox(proj3, cums, row(g_qb[l]), row(g_kb[l]), n_heads_b)

        w_r = jnp.zeros((d, LANES), F32).at[:, :N_GROUPS].set(w_rg[l])
        w_r = w_r.at[:, N_GROUPS:N_GROUPS + N_EXPERTS].set(w_re[l])
        wr_hi = w_r.astype(MXU_DTYPE)
        wr_pair = jnp.concatenate([wr_hi, (w_r - wr_hi.astype(F32)).astype(MXU_DTYPE)], axis=1)
        b_r = jnp.zeros((1, LANES), F32).at[0, :N_GROUPS].set(b_rg[l])
        b_r = b_r.at[0, N_GROUPS:N_GROUPS + N_EXPERTS].set(b_re[l])
        x1, h2, rlog = _merge(proj, o_a.reshape(t, -1), o_b.reshape(t, -1), x2, mod3, row(b_gate[l]),
                              w_pa[l].astype(MXU_DTYPE), w_pb[l].astype(MXU_DTYPE),
                              w_o[l].astype(MXU_DTYPE), row(g_ffn[l]), wr_hi, wr_pair, b_r, s)
        route, pos_t, items = _route(rlog, n_items)
        pos0, pos1 = pos_t[0], pos_t[1]
        xs = _dispatch(pos0, pos1, h2)
        ys = _experts(items, xs, w1[l], w3[l], w2[l], n_items)
        x2 = _combine(pos0, pos1, ys, x1, route, mod3, s)
    return x2.reshape(b, s, d)
```
